```python
import math
import jax, jax.numpy as jnp
from jax import lax
import numpy as np

D_MODEL = 1024
BATCH = 1
SEQ = 16384
DEPTH = 1
DEC_BATCH = 128
DEC_SEQ = 4
PAST_LEN = 16384
PAGE_SIZE = 128

MIX_WIDTH = D_MODEL
ATTN_WIDTH = MIX_WIDTH // 2
CONV_CH = MIX_WIDTH - ATTN_WIDTH
HEAD_DIM = 64
N_HEADS = ATTN_WIDTH // HEAD_DIM
N_KV_HEADS = 2
GQA_GROUP = N_HEADS // N_KV_HEADS
KV_WIDTH = N_KV_HEADS * HEAD_DIM
WINDOW = 128
ROPE_THETA = 500000.0
ROPE_DIM = HEAD_DIM // 4
CONV_K = 3
CONV_GROUPS = 8
Q_END = ATTN_WIDTH
K_END = Q_END + KV_WIDTH
V_END = K_END + KV_WIDTH
B_END = V_END + CONV_CH
C_END = B_END + CONV_CH
IN_WIDTH = C_END + CONV_CH
N_MEM = 256
MEM_HEADS = 4
MEM_HEAD_DIM = D_MODEL // MEM_HEADS
N_GROUPS = 4
EXPERTS_PER_GROUP = 8
N_EXPERTS = N_GROUPS * EXPERTS_PER_GROUP
TOP_K = 2
EXPERT_FF = 256
ALPHA = (2.0 * DEPTH) ** 0.25
BETA = (8.0 * DEPTH) ** -0.25
LN_EPS = 1e-5

kernel_name = "hymba_swa_sinks_shortconv_memxattn_hmoe_step"


def layer_norm(x, g, b):
    xf = x.astype(jnp.float32)
    mu = jnp.mean(xf, -1, keepdims=True)
    var = jnp.mean(jnp.square(xf - mu), -1, keepdims=True)
    y = (xf - mu) * lax.rsqrt(var + LN_EPS) * g.astype(jnp.float32) + b.astype(jnp.float32)
    return y.astype(x.dtype)


def rope_partial(x, pos):
    half = ROPE_DIM // 2
    inv = ROPE_THETA ** (-jnp.arange(0, ROPE_DIM, 2, dtype=jnp.float32) / ROPE_DIM)
    ang = pos.astype(jnp.float32)[:, None] * inv
    cos = jnp.cos(ang)[:, None, :]
    sin = jnp.sin(ang)[:, None, :]
    xr = x[..., :ROPE_DIM].astype(jnp.float32)
    x1, x2 = xr[..., :half], xr[..., half:]
    rot = jnp.concatenate([x1 * cos - x2 * sin, x2 * cos + x1 * sin], -1)
    return jnp.concatenate([rot.astype(x.dtype), x[..., ROPE_DIM:]], -1)


def attend_with_sinks(q, k, v, mask, sinks):
    s = jnp.einsum('...qhgd,...khd->...hgqk', q.astype(jnp.float32), k.astype(jnp.float32))
    s = jnp.where(mask, s * (HEAD_DIM ** -0.5), -jnp.inf)
    sink = sinks.astype(jnp.float32).reshape(N_KV_HEADS, GQA_GROUP, 1, 1)
    m = jnp.maximum(jnp.max(s, -1, keepdims=True), sink)
    p = jnp.exp(s - m)
    p = p / (jnp.sum(p, -1, keepdims=True) + jnp.exp(sink - m))
    o = jnp.einsum('...hgqk,...khd->...qhgd', p, v.astype(jnp.float32))
    return o.astype(q.dtype)


def swa_prompt(q, k, v, sinks):
    b, s = q.shape[:2]
    nb = s // WINDOW
    qb = q.reshape(b, nb, WINDOW, N_KV_HEADS, GQA_GROUP, HEAD_DIM)
    pad = jnp.zeros((b, WINDOW, N_KV_HEADS, HEAD_DIM), k.dtype)
    kp = jnp.concatenate([pad, k], 1).reshape(b, nb + 1, WINDOW, N_KV_HEADS, HEAD_DIM)
    vp = jnp.concatenate([pad, v], 1).reshape(b, nb + 1, WINDOW, N_KV_HEADS, HEAD_DIM)
    kb = jnp.concatenate([kp[:, :-1], kp[:, 1:]], 2)
    vb = jnp.concatenate([vp[:, :-1], vp[:, 1:]], 2)
    i = jnp.arange(WINDOW)[:, None]
    j = jnp.arange(2 * WINDOW)[None, :]
    diff = i - j + WINDOW
    blk = jnp.arange(nb)[:, None, None]
    valid = (diff >= 0) & (diff < WINDOW) & (blk * WINDOW - WINDOW + j >= 0)
    out = attend_with_sinks(qb, kb, vb, valid[:, None, None], sinks)
    return out.reshape(b, s, ATTN_WIDTH)


def swa_sample(q, k_new, v_new, cache_k, cache_v, sinks):
    db, t = q.shape[:2]
    kc = jnp.concatenate([cache_k, k_new], 1)
    vc = jnp.concatenate([cache_v, v_new], 1)
    qg = q.reshape(db, t, N_KV_HEADS, GQA_GROUP, HEAD_DIM)
    i = jnp.arange(t)[:, None]
    j = jnp.arange(WINDOW + t)[None, :]
    diff = i - j + WINDOW
    mask = (diff >= 0) & (diff < WINDOW)
    out = attend_with_sinks(qg, kc, vc, mask, sinks).reshape(db, t, ATTN_WIDTH)
    return out, kc[:, t:], vc[:, t:]


def causal_conv3(up, w):
    L = up.shape[1] - (CONV_K - 1)
    return w[0] * up[:, 0:L] + w[1] * up[:, 1:L + 1] + w[2] * up[:, 2:L + 2]


def mem_kv(mem, w_k_mem, w_v_mem):
    b, n, _ = mem.shape
    mk = (mem @ w_k_mem).reshape(b, n, MEM_HEADS, MEM_HEAD_DIM)
    mv = (mem @ w_v_mem).reshape(b, n, MEM_HEADS, MEM_HEAD_DIM)
    return mk, mv


def mem_attend(x, mk, mv, w_q_mem, w_o_mem):
    b, s, _ = x.shape
    q = (x @ w_q_mem).reshape(b, s, MEM_HEADS, MEM_HEAD_DIM)
    sc = jnp.einsum('bqhd,bkhd->bhqk', q.astype(jnp.float32), mk.astype(jnp.float32)) * (MEM_HEAD_DIM ** -0.5)
    p = jax.nn.softmax(sc, -1)
    o = jnp.einsum('bhqk,bkhd->bqhd', p, mv.astype(jnp.float32)).astype(x.dtype)
    return o.reshape(b, s, D_MODEL) @ w_o_mem


def hier_moe(x, w_router_group, b_router_group, w_router_expert, b_router_expert, w_gate, w_up, w_down):
    shp = x.shape
    xt = x.reshape(-1, D_MODEL)
    g_logits = (xt @ w_router_group).astype(jnp.float32) + b_router_group.astype(jnp.float32)
    g_prob = jax.nn.softmax(g_logits, -1)
    g_idx = jnp.argmax(g_logits, -1)
    g_w = jnp.take_along_axis(g_prob, g_idx[:, None], 1)
    e_logits = ((xt @ w_router_expert).astype(jnp.float32) + b_router_expert.astype(jnp.float32))
    e_logits = e_logits.reshape(-1, N_GROUPS, EXPERTS_PER_GROUP)
    e_in = jnp.take_along_axis(e_logits, g_idx[:, None, None], 1)[:, 0]
    top_v, top_i = lax.top_k(e_in, TOP_K)
    w = jax.nn.softmax(top_v, -1) * g_w
    ids = g_idx[:, None] * EXPERTS_PER_GROUP + top_i
    gates = jnp.einsum('nk,nke->ne', w, jax.nn.one_hot(ids, N_EXPERTS, dtype=jnp.float32))
    hg = jnp.einsum('nd,edf->nef', xt, w_gate)
    hu = jnp.einsum('nd,edf->nef', xt, w_up)
    h = jax.nn.silu(hg) * hu * gates[..., None].astype(x.dtype)
    y = jnp.einsum('nef,efd->nd', h, w_down)
    return y.reshape(shp)


def finish_layer(x, attn, conv, mk, mv, w_mix_out, ln1_g, ln1_b, w_q_mem, w_o_mem, ln2_g, ln2_b,
                 w_router_group, b_router_group, w_router_expert, b_router_expert,
                 w_gate, w_up, w_down, ln3_g, ln3_b):
    mix = jnp.concatenate([attn, conv], -1) @ w_mix_out
    h1 = layer_norm(ALPHA * x + mix, ln1_g, ln1_b)
    h2 = layer_norm(ALPHA * h1 + mem_attend(h1, mk, mv, w_q_mem, w_o_mem), ln2_g, ln2_b)
    ff = hier_moe(h2, w_router_group, b_router_group, w_router_expert, b_router_expert, w_gate, w_up, w_down)
    return layer_norm(ALPHA * h2 + ff, ln3_g, ln3_b)


def setup_inputs(seed: int = 0) -> dict:
    key = jax.random.key(seed)
    ks = jax.random.split(key, 40)
    f32 = jnp.float32
    nrm = lambda k, shp, sc: jax.random.normal(k, shp, f32) * sc
    din = D_MODEL ** -0.5
    w_in = jnp.concatenate([
        nrm(ks[10], (D_MODEL, ATTN_WIDTH), din),
        nrm(ks[11], (D_MODEL, KV_WIDTH), din),
        nrm(ks[12], (D_MODEL, KV_WIDTH), din * BETA),
        nrm(ks[13], (D_MODEL, CONV_CH), din),
        nrm(ks[14], (D_MODEL, CONV_CH), din),
        nrm(ks[15], (D_MODEL, CONV_CH), din * BETA),
    ], axis=1)
    return {
        "x_prompt": nrm(ks[0], (BATCH, SEQ, D_MODEL), 1.0),
        "x_sample": nrm(ks[1], (DEC_BATCH, DEC_SEQ, D_MODEL), 1.0),
        "mem_prompt": nrm(ks[2], (BATCH, N_MEM, D_MODEL), 1.0),
        "cache_swa_k": nrm(ks[3], (DEC_BATCH, WINDOW, N_KV_HEADS, HEAD_DIM), 1.0),
        "cache_swa_v": nrm(ks[4], (DEC_BATCH, WINDOW, N_KV_HEADS, HEAD_DIM), BETA),
        "cache_conv": nrm(ks[5], (DEC_BATCH, CONV_K - 1, CONV_CH), 1.0),
        "cache_mem_k": nrm(ks[6], (DEC_BATCH, N_MEM, MEM_HEADS, MEM_HEAD_DIM), 1.0),
        "cache_mem_v": nrm(ks[7], (DEC_BATCH, N_MEM, MEM_HEADS, MEM_HEAD_DIM), BETA),
        "w_in": w_in,
        "sinks": nrm(ks[16], (N_HEADS,), 0.5),
        "conv_w": nrm(ks[17], (CONV_K, CONV_CH), CONV_K ** -0.5),
        "w_mix_out": nrm(ks[18], (MIX_WIDTH, D_MODEL), MIX_WIDTH ** -0.5 * BETA),
        "ln1_g": 1.0 + nrm(ks[19], (D_MODEL,), 0.02),
        "ln1_b": nrm(ks[20], (D_MODEL,), 0.02),
        "w_q_mem": nrm(ks[21], (D_MODEL, D_MODEL), din),
        "w_k_mem": nrm(ks[22], (D_MODEL, D_MODEL), din),
        "w_v_mem": nrm(ks[23], (D_MODEL, D_MODEL), din * BETA),
        "w_o_mem": nrm(ks[24], (D_MODEL, D_MODEL), din * BETA),
        "ln2_g": 1.0 + nrm(ks[25], (D_MODEL,), 0.02),
        "ln2_b": nrm(ks[26], (D_MODEL,), 0.02),
        "w_router_group": nrm(ks[27], (D_MODEL, N_GROUPS), din),
        "b_router_group": nrm(ks[28], (N_GROUPS,), 0.01),
        "w_router_expert": nrm(ks[29], (D_MODEL, N_EXPERTS), din),
        "b_router_expert": nrm(ks[30], (N_EXPERTS,), 0.01),
        "w_gate": nrm(ks[31], (N_EXPERTS, D_MODEL, EXPERT_FF), din),
        "w_up": nrm(ks[32], (N_EXPERTS, D_MODEL, EXPERT_FF), din),
        "w_down": nrm(ks[33], (N_EXPERTS, EXPERT_FF, D_MODEL), EXPERT_FF ** -0.5 * BETA),
        "ln3_g": 1.0 + nrm(ks[34], (D_MODEL,), 0.02),
        "ln3_b": nrm(ks[35], (D_MODEL,), 0.02),
    }


def reference(x_prompt, x_sample, mem_prompt, cache_swa_k, cache_swa_v, cache_conv, cache_mem_k, cache_mem_v,
              w_in, sinks, conv_w, w_mix_out, ln1_g, ln1_b, w_q_mem, w_k_mem, w_v_mem, w_o_mem, ln2_g, ln2_b,
              w_router_group, b_router_group, w_router_expert, b_router_expert, w_gate, w_up, w_down,
              ln3_g, ln3_b):
    def split_proj(x):
        b, s, _ = x.shape
        q, k, v, bg, cg, hh = jnp.split(x @ w_in, [Q_END, K_END, V_END, B_END, C_END], axis=-1)
        q = q.reshape(b, s, N_HEADS, HEAD_DIM)
        k = k.reshape(b, s, N_KV_HEADS, HEAD_DIM)
        v = v.reshape(b, s, N_KV_HEADS, HEAD_DIM)
        return q, k, v, bg, cg, hh

    yp = x_prompt
    ys = x_sample
    for _ in range(DEPTH):
        q, k, v, bg, cg, hh = split_proj(yp)
        pos_p = jnp.arange(yp.shape[1])
        q = rope_partial(q, pos_p)
        k = rope_partial(k, pos_p)
        attn_p = swa_prompt(q, k, v, sinks)
        u_p = cg * hh
        up_p = jnp.concatenate([jnp.zeros((yp.shape[0], CONV_K - 1, CONV_CH), u_p.dtype), u_p], 1)
        conv_p = bg * causal_conv3(up_p, conv_w)
        swa_k_prompt = k[:, -WINDOW:]
        swa_v_prompt = v[:, -WINDOW:]
        conv_prompt = u_p[:, -(CONV_K - 1):]
        mem_k_prompt, mem_v_prompt = mem_kv(mem_prompt, w_k_mem, w_v_mem)
        yp = finish_layer(yp, attn_p, conv_p, mem_k_prompt, mem_v_prompt, w_mix_out, ln1_g, ln1_b,
                          w_q_mem, w_o_mem, ln2_g, ln2_b, w_router_group, b_router_group,
                          w_router_expert, b_router_expert, w_gate, w_up, w_down, ln3_g, ln3_b)

        q, k, v, bg, cg, hh = split_proj(ys)
        pos_s = PAST_LEN + jnp.arange(ys.shape[1])
        q = rope_partial(q, pos_s)
        k = rope_partial(k, pos_s)
        attn_s, swa_k_sample, swa_v_sample = swa_sample(q, k, v, cache_swa_k, cache_swa_v, sinks)
        u_s = cg * hh
        up_s = jnp.concatenate([cache_conv.astype(u_s.dtype), u_s], 1)
        conv_s = bg * causal_conv3(up_s, conv_w)
        conv_sample = up_s[:, -(CONV_K - 1):]
        ys = finish_layer(ys, attn_s, conv_s, cache_mem_k, cache_mem_v, w_mix_out, ln1_g, ln1_b,
                          w_q_mem, w_o_mem, ln2_g, ln2_b, w_router_group, b_router_group,
                          w_router_expert, b_router_expert, w_gate, w_up, w_down, ln3_g, ln3_b)

    return (yp, ys, swa_k_prompt, swa_v_prompt, conv_prompt, mem_k_prompt, mem_v_prompt,
            swa_k_sample, swa_v_sample, conv_sample)
```

```python
import functools

import jax
import jax.numpy as jnp
from jax import lax
from jax.experimental import pallas as pl
from jax.experimental.pallas import tpu as pltpu

D_MODEL = 1024
SEQ = 16384
DEC_BATCH = 128
DEC_SEQ = 4
PAST_LEN = 16384
ATTN_WIDTH = 512
CONV_CH = 512
HEAD_DIM = 64
N_HEADS = 8
N_KV_HEADS = 2
KV_WIDTH = 128
WINDOW = 128
ROPE_THETA = 500000.0
ROPE_DIM = 16
CONV_K = 3
Q_END = ATTN_WIDTH
K_END = Q_END + KV_WIDTH
V_END = K_END + KV_WIDTH
B_END = V_END + CONV_CH
C_END = B_END + CONV_CH
IN_WIDTH = C_END + CONV_CH
N_MEM = 256
MEM_HEADS = 4
MEM_HEAD_DIM = 256
N_GROUPS = 4
EXPERTS_PER_GROUP = 8
N_EXPERTS = 32
EXPERT_FF = 256
ALPHA = 2.0 ** 0.25
LN_EPS = 1e-5

LANES = 128
SUBLANES = 8
ROW_CHUNKS = D_MODEL // LANES
VMEM_LIMIT = 56 * 1024 * 1024

N_PROMPT = SEQ
N_SAMPLE = DEC_BATCH * DEC_SEQ
N_ALL = N_PROMPT + N_SAMPLE
TM_PROJ = 512
TM_POST = 512
TM_MOE = 256
TM_COMB = 256
N_ASSIGN = 2 * N_ALL
MOE_TILES = (N_ASSIGN + N_EXPERTS * (TM_MOE - 1) + TM_MOE - 1) // TM_MOE
P_MAX = MOE_TILES * TM_MOE
SAMPLE_BB = 4

assert ROW_CHUNKS == SUBLANES
assert N_SAMPLE == TM_POST


def _params(sem, vmem=VMEM_LIMIT):
    return pltpu.CompilerParams(dimension_semantics=sem, vmem_limit_bytes=vmem)


def _bf16(x):
    return x.astype(jnp.bfloat16)


def _dot(a, b):
    return jnp.dot(a, b, preferred_element_type=jnp.float32)


def _dot_nt(a, b):
    return lax.dot_general(a, b, (((1,), (1,)), ((), ())), preferred_element_type=jnp.float32)


def _layer_norm(x, g, b):
    mu = jnp.mean(x, axis=-1, keepdims=True)
    xc = x - mu
    var = jnp.mean(xc * xc, axis=-1, keepdims=True)
    return xc * lax.rsqrt(var + LN_EPS) * g + b


def _rope(x, cos_t, sin_t):
    lane = lax.broadcasted_iota(jnp.int32, x.shape, 1) % HEAD_DIM
    half = ROPE_DIM // 2
    partner = jnp.where(lane < half, pltpu.roll(x, LANES - half, axis=1), pltpu.roll(x, half, axis=1))
    return x * cos_t + partner * sin_t


def _head_slabs(x):
    lane = lax.broadcasted_iota(jnp.int32, x.shape, 1)
    lo = lane < HEAD_DIM
    sw = pltpu.roll(x, HEAD_DIM, axis=1)
    zero = jnp.zeros_like(x)
    slabs = [jnp.where(lo, x, zero), jnp.where(lo, zero, sw), jnp.where(lo, sw, zero), jnp.where(lo, zero, x)]
    return _bf16(jnp.concatenate(slabs, axis=1))


def _store_token_tiles(ref, val):
    rows = val.shape[0]
    for c in range(ROW_CHUNKS):
        ref[pl.ds(c, rows, stride=ROW_CHUNKS), :] = val[:, c * LANES:(c + 1) * LANES]


def _load_token_tiles(ref, base, rows):
    return jnp.concatenate(
        [ref[pl.ds(base + c, rows, stride=ROW_CHUNKS), :] for c in range(ROW_CHUNKS)], axis=1)


def _proj_common(x_ref, w_ref, cos_ref, sin_ref):
    xb = _bf16(x_ref[...])
    cos_t = cos_ref[...]
    sin_t = sin_ref[...]
    q = _dot(xb, w_ref[:, 0:Q_END])
    q_rot = jnp.concatenate(
        [_rope(q[:, p * LANES:(p + 1) * LANES], cos_t, sin_t) for p in range(ATTN_WIDTH // LANES)], axis=1)
    q_out = _bf16(q_rot * (HEAD_DIM ** -0.5))
    k = _rope(_dot(xb, w_ref[:, Q_END:K_END]), cos_t, sin_t)
    v = _dot(xb, w_ref[:, K_END:V_END])
    bg = _dot(xb, w_ref[:, V_END:B_END])
    u = _dot(xb, w_ref[:, B_END:C_END]) * _dot(xb, w_ref[:, C_END:IN_WIDTH])
    return q_out, k, v, bg, u


def _conv3(bg, u, u1, u2, cw_ref):
    cw = cw_ref[...]
    return bg * (cw[0:1, :] * u2 + cw[1:2, :] * u1 + cw[2:3, :] * u)


def _proj_prompt_kernel(x_ref, w_ref, cos_ref, sin_ref, cw_ref,
                        q_ref, kx_ref, vx_ref, conv_ref, ktail_ref, vtail_ref, utail_ref, carry_ref):
    @pl.when(pl.program_id(0) == 0)
    def _():
        carry_ref[...] = jnp.zeros_like(carry_ref)

    q_out, k, v, bg, u = _proj_common(x_ref, w_ref, cos_ref, sin_ref)
    tm = u.shape[0]
    ext = jnp.concatenate([carry_ref[...], u], axis=0)
    u1 = pltpu.roll(ext, 1, axis=0)[SUBLANES:SUBLANES + tm]
    u2 = pltpu.roll(ext, 2, axis=0)[SUBLANES:SUBLANES + tm]
    q_ref[...] = q_out
    kx_ref[...] = _head_slabs(k)
    vx_ref[...] = _head_slabs(v)
    conv_ref[...] = _bf16(_conv3(bg, u, u1, u2, cw_ref))
    ktail_ref[...] = k[tm - WINDOW:tm]
    vtail_ref[...] = v[tm - WINDOW:tm]
    utail_ref[...] = u[tm - SUBLANES:tm]
    carry_ref[...] = u[tm - SUBLANES:tm]


def _proj_sample_kernel(x_ref, w_ref, cos_ref, sin_ref, cw_ref, c0_ref, c1_ref,
                        q_ref, k_ref, v_ref, conv_ref, u_ref):
    q_out, k, v, bg, u = _proj_common(x_ref, w_ref, cos_ref, sin_ref)
    t = lax.broadcasted_iota(jnp.int32, u.shape, 0) % DEC_SEQ
    c0 = c0_ref[...]
    c1 = c1_ref[...]
    u1 = jnp.where(t >= 1, pltpu.roll(u, 1, axis=0), c1)
    u2 = jnp.where(t >= 2, pltpu.roll(u, 2, axis=0), jnp.where(t == 1, c1, c0))
    q_ref[...] = q_out.astype(jnp.float32)
    k_ref[...] = k
    v_ref[...] = v
    conv_ref[...] = _bf16(_conv3(bg, u, u1, u2, cw_ref))
    u_ref[...] = u


def _rope_tables(pos):
    half = ROPE_DIM // 2
    inv = ROPE_THETA ** (-jnp.arange(0, ROPE_DIM, 2, dtype=jnp.float32) / ROPE_DIM)
    ang = pos.astype(jnp.float32)[:, None] * inv
    cos, sin = jnp.cos(ang), jnp.sin(ang)
    n = pos.shape[0]
    pad = HEAD_DIM - ROPE_DIM
    cos_h = jnp.concatenate([cos, cos, jnp.ones((n, pad), jnp.float32)], axis=1)
    sin_h = jnp.concatenate([-sin, sin, jnp.zeros((n, pad), jnp.float32)], axis=1)
    reps = LANES // HEAD_DIM
    assert half * 2 == ROPE_DIM
    return jnp.tile(cos_h, (1, reps)), jnp.tile(sin_h, (1, reps))


def _proj_prompt(x, w_in_b, cos_t, sin_t, conv_w):
    n = x.shape[0]
    tm = TM_PROJ
    row = lambda w: pl.BlockSpec((tm, w), lambda i: (i, 0))
    full = lambda a: pl.BlockSpec(a.shape, lambda i: (0,) * a.ndim)
    const = lambda r, w: pl.BlockSpec((r, w), lambda i: (0, 0))
    return pl.pallas_call(
        _proj_prompt_kernel,
        grid=(n // tm,),
        in_specs=[row(D_MODEL), full(w_in_b), row(LANES), row(LANES), full(conv_w)],
        out_specs=[row(ATTN_WIDTH), row(4 * LANES), row(4 * LANES), row(CONV_CH),
                   const(WINDOW, KV_WIDTH), const(WINDOW, KV_WIDTH), const(SUBLANES, CONV_CH)],
        out_shape=[jax.ShapeDtypeStruct((n, ATTN_WIDTH), jnp.bfloat16),
                   jax.ShapeDtypeStruct((n, 4 * LANES), jnp.bfloat16),
                   jax.ShapeDtypeStruct((n, 4 * LANES), jnp.bfloat16),
                   jax.ShapeDtypeStruct((n, CONV_CH), jnp.bfloat16),
                   jax.ShapeDtypeStruct((WINDOW, KV_WIDTH), jnp.float32),
                   jax.ShapeDtypeStruct((WINDOW, KV_WIDTH), jnp.float32),
                   jax.ShapeDtypeStruct((SUBLANES, CONV_CH), jnp.float32)],
        scratch_shapes=[pltpu.VMEM((SUBLANES, CONV_CH), jnp.float32)],
        compiler_params=_params(("arbitrary",)),
        name="proj_prompt",
    )(x, w_in_b, cos_t, sin_t, conv_w)


def _proj_sample(x, w_in_b, cos_t, sin_t, conv_w, c0, c1):
    n = x.shape[0]
    full = lambda a: pl.BlockSpec(a.shape, lambda i: (0,) * a.ndim)
    out = lambda w, dt: jax.ShapeDtypeStruct((n, w), dt)
    blk = lambda w: pl.BlockSpec((n, w), lambda i: (0, 0))
    return pl.pallas_call(
        _proj_sample_kernel,
        grid=(1,),
        in_specs=[full(x), full(w_in_b), full(cos_t), full(sin_t), full(conv_w), full(c0), full(c1)],
        out_specs=[blk(ATTN_WIDTH), blk(KV_WIDTH), blk(KV_WIDTH), blk(CONV_CH), blk(CONV_CH)],
        out_shape=[out(ATTN_WIDTH, jnp.float32), out(KV_WIDTH, jnp.float32), out(KV_WIDTH, jnp.float32),
                   out(CONV_CH, jnp.bfloat16), out(CONV_CH, jnp.float32)],
        compiler_params=_params(("arbitrary",)),
        name="proj_sample",
    )(x, w_in_b, cos_t, sin_t, conv_w, c0, c1)


def _sink_softmax_pv(s, valid, sink, vx):
    s = jnp.where(valid, s, -jnp.inf)
    m = jnp.maximum(jnp.max(s, axis=1, keepdims=True), sink)
    p = jnp.exp(s - m)
    den = jnp.sum(p, axis=1, keepdims=True) + jnp.exp(sink - m)
    return _dot(_bf16(p), vx) / den


def _swa_prompt_kernel(sinks_ref, q_ref, kc_ref, kp_ref, vc_ref, vp_ref, o_ref):
    blk = pl.program_id(0)
    kcat = jnp.concatenate([kp_ref[...], kc_ref[...]], axis=0)
    vcat = jnp.concatenate([vp_ref[...], vc_ref[...]], axis=0)
    i = lax.broadcasted_iota(jnp.int32, (WINDOW, 2 * WINDOW), 0)
    j = lax.broadcasted_iota(jnp.int32, (WINDOW, 2 * WINDOW), 1)
    valid = (j > i) & (j <= i + WINDOW) & ((blk > 0) | (j >= WINDOW))
    for p in range(N_HEADS // 2):
        qs = q_ref[:, p * LANES:(p + 1) * LANES]
        acc = None
        for e in range(2):
            hd = 2 * p + e
            slab = 2 * (hd // (N_HEADS // N_KV_HEADS)) + e
            kx = kcat[:, slab * LANES:(slab + 1) * LANES]
            vx = vcat[:, slab * LANES:(slab + 1) * LANES]
            o = _sink_softmax_pv(_dot_nt(qs, kx), valid, sinks_ref[hd], vx)
            acc = o if acc is None else acc + o
        o_ref[:, p * LANES:(p + 1) * LANES] = _bf16(acc)


def _swa_prompt(sinks, q, kx, vx):
    n = q.shape[0]
    nb = n // WINDOW
    cur = lambda w: pl.BlockSpec((WINDOW, w), lambda i: (i, 0))
    prev = lambda w: pl.BlockSpec((WINDOW, w), lambda i: (jnp.maximum(i - 1, 0), 0))
    return pl.pallas_call(
        _swa_prompt_kernel,
        grid=(nb,),
        in_specs=[pl.BlockSpec(memory_space=pltpu.SMEM), cur(ATTN_WIDTH),
                  cur(4 * LANES), prev(4 * LANES), cur(4 * LANES), prev(4 * LANES)],
        out_specs=cur(ATTN_WIDTH),
        out_shape=jax.ShapeDtypeStruct((n, ATTN_WIDTH), jnp.bfloat16),
        compiler_params=_params(("arbitrary",)),
        name="swa_prompt",
    )(sinks, q, kx, kx, vx, vx)


def _swa_sample_kernel(sinks_ref, q_ref, kn_ref, vn_ref, ck_ref, cv_ref, o_ref, ok_ref, ov_ref):
    nk = WINDOW + SUBLANES
    i = lax.broadcasted_iota(jnp.int32, (DEC_SEQ, nk), 0)
    j = lax.broadcasted_iota(jnp.int32, (DEC_SEQ, nk), 1)
    valid = ((j < WINDOW) & (j > i)) | ((j >= WINDOW) & (j - WINDOW <= i))
    sub = lax.broadcasted_iota(jnp.int32, (SUBLANES, KV_WIDTH), 0)
    for b in range(SAMPLE_BB):
        ck = ck_ref[b]
        cv = cv_ref[b]
        kn = kn_ref[b * SUBLANES:(b + 1) * SUBLANES, :]
        vn = vn_ref[b * SUBLANES:(b + 1) * SUBLANES, :]
        kslab = _head_slabs(jnp.concatenate([ck, kn], axis=0))
        vslab = _head_slabs(jnp.concatenate([cv, vn], axis=0))
        qb = _bf16(q_ref[b * DEC_SEQ:(b + 1) * DEC_SEQ, :])
        for p in range(N_HEADS // 2):
            qs = qb[:, p * LANES:(p + 1) * LANES]
            acc = None
            for e in range(2):
                hd = 2 * p + e
                slab = 2 * (hd // (N_HEADS // N_KV_HEADS)) + e
                kx = kslab[:, slab * LANES:(slab + 1) * LANES]
                vx = vslab[:, slab * LANES:(slab + 1) * LANES]
                o = _sink_softmax_pv(_dot_nt(qs, kx), valid, sinks_ref[hd], vx)
                acc = o if acc is None else acc + o
            o_ref[b * DEC_SEQ:(b + 1) * DEC_SEQ, p * LANES:(p + 1) * LANES] = acc
        shift = WINDOW - DEC_SEQ
        for old, new, dst in ((ck, kn, ok_ref), (cv, vn, ov_ref)):
            rolled = pltpu.roll(old, shift, axis=0)
            new8 = pltpu.roll(new, SUBLANES - DEC_SEQ, axis=0)
            dst[b, 0:WINDOW - SUBLANES, :] = rolled[0:WINDOW - SUBLANES]
            dst[b, WINDOW - SUBLANES:WINDOW, :] = jnp.where(
                sub >= SUBLANES - DEC_SEQ, new8, rolled[WINDOW - SUBLANES:WINDOW])


def _swa_sample(sinks, q, kn, vn, cache_k, cache_v):
    nb = cache_k.shape[0]
    bb = SAMPLE_BB
    rows = lambda w: pl.BlockSpec((bb * DEC_SEQ, w), lambda i: (i, 0))
    new = pl.BlockSpec((bb * SUBLANES, KV_WIDTH), lambda i: (i, 0))
    cache = pl.BlockSpec((bb, WINDOW, KV_WIDTH), lambda i: (i, 0, 0))
    pad8 = lambda a: jnp.pad(a.reshape(nb, DEC_SEQ, KV_WIDTH),
                             ((0, 0), (0, SUBLANES - DEC_SEQ), (0, 0))).reshape(nb * SUBLANES, KV_WIDTH)
    kn, vn = pad8(kn), pad8(vn)
    return pl.pallas_call(
        _swa_sample_kernel,
        grid=(nb // bb,),
        in_specs=[pl.BlockSpec(memory_space=pltpu.SMEM), rows(ATTN_WIDTH), new, new, cache, cache],
        out_specs=[rows(ATTN_WIDTH), cache, cache],
        out_shape=[jax.ShapeDtypeStruct((nb * DEC_SEQ, ATTN_WIDTH), jnp.float32),
                   jax.ShapeDtypeStruct(cache_k.shape, jnp.float32),
                   jax.ShapeDtypeStruct(cache_v.shape, jnp.float32)],
        compiler_params=_params(("arbitrary",)),
        name="swa_sample",
    )(sinks, q, kn, vn, cache_k, cache_v)


def _mem_kv_kernel(mem_ref, wk_ref, wv_ref, mk_ref, mv_ref, mkb_ref, mvb_ref):
    mb = _bf16(mem_ref[...])
    mk = _dot(mb, wk_ref[...])
    mv = _dot(mb, wv_ref[...])
    mk_ref[...] = mk
    mv_ref[...] = mv
    mkb_ref[...] = _bf16(mk)
    mvb_ref[...] = _bf16(mv)


def _mem_kv(mem, wk_b, wv_b):
    full = lambda a: pl.BlockSpec(a.shape, lambda i: (0,) * a.ndim)
    blk = pl.BlockSpec((N_MEM, D_MODEL), lambda i: (0, 0))
    f32 = jax.ShapeDtypeStruct((N_MEM, D_MODEL), jnp.float32)
    b16 = jax.ShapeDtypeStruct((N_MEM, D_MODEL), jnp.bfloat16)
    return pl.pallas_call(
        _mem_kv_kernel,
        grid=(1,),
        in_specs=[full(mem), full(wk_b), full(wv_b)],
        out_specs=[blk, blk, blk, blk],
        out_shape=[f32, f32, b16, b16],
        compiler_params=_params(("arbitrary",)),
        name="mem_kv",
    )(mem, wk_b, wv_b)


def _mix_ln1(attn_ref, conv_ref, x_ref, wmix_ref, g1_ref, b1_ref):
    mix = _dot(_bf16(attn_ref[...]), wmix_ref[0:ATTN_WIDTH, :]) + _dot(conv_ref[...], wmix_ref[ATTN_WIDTH:, :])
    return _layer_norm(ALPHA * x_ref[...] + mix, g1_ref[...], b1_ref[...])


def _mem_q(h1, wq_ref):
    return _bf16(_dot(_bf16(h1), wq_ref[...]) * (MEM_HEAD_DIM ** -0.5))


def _route(h2, wrh_ref, wrl_ref, br_ref):
    hi = _bf16(h2)
    lo = _bf16(h2 - hi.astype(jnp.float32))
    logits = _dot(hi, wrh_ref[...]) + _dot(lo, wrh_ref[...]) + _dot(hi, wrl_ref[...]) + br_ref[...]
    lane_i = lax.broadcasted_iota(jnp.int32, logits.shape, 1)
    lane = lane_i.astype(jnp.float32)
    big = jnp.float32(LANES)
    is_g = lane_i < N_GROUPS
    gl = jnp.where(is_g, logits, -jnp.inf)
    gmax = jnp.max(gl, axis=1, keepdims=True)
    gidx = jnp.min(jnp.where(is_g & (logits == gmax), lane, big), axis=1, keepdims=True)
    gsum = jnp.sum(jnp.exp(gl - gmax), axis=1, keepdims=True)
    gw = 1.0 / gsum
    eid = lane_i - N_GROUPS
    assert EXPERTS_PER_GROUP == 8
    grp = lax.shift_right_arithmetic(eid, jnp.full_like(eid, 3)).astype(jnp.float32)
    in_e = (lane_i >= N_GROUPS) & (lane_i < N_GROUPS + N_EXPERTS) & (grp == gidx)
    v1 = jnp.max(jnp.where(in_e, logits, -jnp.inf), axis=1, keepdims=True)
    i1 = jnp.min(jnp.where(in_e & (logits == v1), lane, big), axis=1, keepdims=True)
    rest = in_e & (lane != i1)
    v2 = jnp.max(jnp.where(rest, logits, -jnp.inf), axis=1, keepdims=True)
    i2 = jnp.min(jnp.where(rest & (logits == v2), lane, big), axis=1, keepdims=True)
    ex = jnp.exp(v2 - v1)
    den = 1.0 + ex
    w1 = gw / den
    w2 = gw * ex / den
    zero = jnp.zeros_like(logits)
    return jnp.where(lane_i == 0, i1 - N_GROUPS,
                     jnp.where(lane_i == 1, i2 - N_GROUPS,
                               jnp.where(lane_i == 2, w1, jnp.where(lane_i == 3, w2, zero))))


def _post_prompt_kernel(attn_ref, conv_ref, x_ref, wmix_ref, g1_ref, b1_ref, wq_ref, mk_ref, mv_ref, wo_ref,
                        g2_ref, b2_ref, wrh_ref, wrl_ref, br_ref, h2s_ref, rts_ref, h2t_ref, rt_ref):
    is_tail = pl.program_id(0) == N_PROMPT // TM_POST

    @pl.when(is_tail)
    def _():
        h2t_ref[...] = h2s_ref[...]
        rt_ref[...] = rts_ref[...]

    @pl.when(jnp.logical_not(is_tail))
    def _():
        h1 = _mix_ln1(attn_ref, conv_ref, x_ref, wmix_ref, g1_ref, b1_ref)
        qm = _mem_q(h1, wq_ref)
        outs = []
        for h in range(MEM_HEADS):
            sl = slice(h * MEM_HEAD_DIM, (h + 1) * MEM_HEAD_DIM)
            s = _dot_nt(qm[:, sl], mk_ref[:, sl])
            m = jnp.max(s, axis=1, keepdims=True)
            p = jnp.exp(s - m)
            den = jnp.sum(p, axis=1, keepdims=True)
            outs.append(_dot(_bf16(p), mv_ref[:, sl]) / den)
        o = _bf16(jnp.concatenate(outs, axis=1))
        h2 = _layer_norm(ALPHA * h1 + _dot(o, wo_ref[...]), g2_ref[...], b2_ref[...])
        _store_token_tiles(h2t_ref, h2)
        rt_ref[...] = _route(h2, wrh_ref, wrl_ref, br_ref)


def _post_prompt(attn, conv, x, wmix_b, g1, b1, wq_b, mk_b, mv_b, wo_b, g2, b2, wrh, wrl, br, h2t_s, rt_s):
    n = x.shape[0]
    tm = TM_POST
    steps = n // tm
    row = lambda w: pl.BlockSpec((tm, w), lambda i: (jnp.minimum(i, steps - 1), 0))
    full = lambda a: pl.BlockSpec(a.shape, lambda i: (0,) * a.ndim)
    weights = (wmix_b, g1, b1, wq_b, mk_b, mv_b, wo_b, g2, b2, wrh, wrl, br, h2t_s, rt_s)
    n_out = n + h2t_s.shape[0] // ROW_CHUNKS
    return pl.pallas_call(
        _post_prompt_kernel,
        grid=(steps + 1,),
        in_specs=[row(ATTN_WIDTH), row(CONV_CH), row(D_MODEL)] + [full(a) for a in weights],
        out_specs=[pl.BlockSpec((tm * ROW_CHUNKS, LANES), lambda i: (i, 0)),
                   pl.BlockSpec((tm, LANES), lambda i: (i, 0))],
        out_shape=[jax.ShapeDtypeStruct((n_out * ROW_CHUNKS, LANES), jnp.float32),
                   jax.ShapeDtypeStruct((n_out, LANES), jnp.float32)],
        compiler_params=_params(("arbitrary",)),
        name="post_prompt",
    )(attn, conv, x, *weights)


def _post_a_sample_kernel(attn_ref, conv_ref, x_ref, wmix_ref, g1_ref, b1_ref, wq_ref, h1_ref, qm_ref):
    h1 = _mix_ln1(attn_ref, conv_ref, x_ref, wmix_ref, g1_ref, b1_ref)
    h1_ref[...] = h1
    qm_ref[...] = _mem_q(h1, wq_ref).astype(jnp.float32)


def _post_a_sample(attn, conv, x, wmix_b, g1, b1, wq_b):
    n = x.shape[0]
    args = (attn, conv, x, wmix_b, g1, b1, wq_b)
    full = lambda a: pl.BlockSpec(a.shape, lambda i: (0,) * a.ndim)
    blk = pl.BlockSpec((n, D_MODEL), lambda i: (0, 0))
    return pl.pallas_call(
        _post_a_sample_kernel,
        grid=(1,),
        in_specs=[full(a) for a in args],
        out_specs=[blk, blk],
        out_shape=[jax.ShapeDtypeStruct((n, D_MODEL), jnp.float32),
                   jax.ShapeDtypeStruct((n, D_MODEL), jnp.float32)],
        compiler_params=_params(("arbitrary",)),
        name="post_a_sample",
    )(*args)


def _mem_attn_sample_kernel(qm_ref, mk_ref, mv_ref, o_ref):
    for b in range(SAMPLE_BB):
        qb = _bf16(qm_ref[b * DEC_SEQ:(b + 1) * DEC_SEQ, :])
        outs = []
        for h in range(MEM_HEADS):
            sl = slice(h * MEM_HEAD_DIM, (h + 1) * MEM_HEAD_DIM)
            kh = _bf16(mk_ref[b, :, sl])
            vh = _bf16(mv_ref[b, :, sl])
            s = _dot_nt(qb[:, sl], kh)
            m = jnp.max(s, axis=1, keepdims=True)
            p = jnp.exp(s - m)
            den = jnp.sum(p, axis=1, keepdims=True)
            outs.append(_dot(_bf16(p), vh) / den)
        o_ref[b * DEC_SEQ:(b + 1) * DEC_SEQ, :] = jnp.concatenate(outs, axis=1)


def _mem_attn_sample(qm, mk, mv):
    nb = mk.shape[0]
    bb = SAMPLE_BB
    rows = pl.BlockSpec((bb * DEC_SEQ, D_MODEL), lambda i: (i, 0))
    kv = pl.BlockSpec((bb, N_MEM, D_MODEL), lambda i: (i, 0, 0))
    return pl.pallas_call(
        _mem_attn_sample_kernel,
        grid=(nb // bb,),
        in_specs=[rows, kv, kv],
        out_specs=rows,
        out_shape=jax.ShapeDtypeStruct((nb * DEC_SEQ, D_MODEL), jnp.float32),
        compiler_params=_params(("arbitrary",)),
        name="mem_attn_sample",
    )(qm, mk, mv)


def _post_b_sample_kernel(o_ref, h1_ref, wo_ref, g2_ref, b2_ref, wrh_ref, wrl_ref, br_ref, h2t_ref, rt_ref):
    h2 = _layer_norm(ALPHA * h1_ref[...] + _dot(_bf16(o_ref[...]), wo_ref[...]), g2_ref[...], b2_ref[...])
    _store_token_tiles(h2t_ref, h2)
    rt_ref[...] = _route(h2, wrh_ref, wrl_ref, br_ref)


def _post_b_sample(o, h1, wo_b, g2, b2, wrh, wrl, br):
    n = h1.shape[0]
    args = (o, h1, wo_b, g2, b2, wrh, wrl, br)
    full = lambda a: pl.BlockSpec(a.shape, lambda i: (0,) * a.ndim)
    return pl.pallas_call(
        _post_b_sample_kernel,
        grid=(1,),
        in_specs=[full(a) for a in args],
        out_specs=[pl.BlockSpec((n * ROW_CHUNKS, LANES), lambda i: (0, 0)),
                   pl.BlockSpec((n, LANES), lambda i: (0, 0))],
        out_shape=[jax.ShapeDtypeStruct((n * ROW_CHUNKS, LANES), jnp.float32),
                   jax.ShapeDtypeStruct((n, LANES), jnp.float32)],
        compiler_params=_params(("arbitrary",)),
        name="post_b_sample",
    )(*args)


def _row_gather_copy(src_hbm, idx, dst, dst_row, sem):
    s0 = pl.multiple_of(idx * ROW_CHUNKS, ROW_CHUNKS)
    d0 = pl.multiple_of(dst_row * ROW_CHUNKS, ROW_CHUNKS)
    return pltpu.make_async_copy(src_hbm.at[pl.ds(s0, ROW_CHUNKS), :], dst.at[pl.ds(d0, ROW_CHUNKS), :], sem)


def _moe_ffn_kernel(te_ref, tv_ref, idx_cur_ref, idx_nxt_ref, h2t_hbm, wg_ref, wu_ref, wd_ref,
                    y_ref, xbuf, sem, wgb, wub, wdb):
    t = pl.program_id(0)
    nt = MOE_TILES
    slot = t % 2

    def issue(idx_ref, s):
        def body(r, c):
            _row_gather_copy(h2t_hbm, idx_ref[0, 0, r], xbuf, s * TM_MOE + r, sem.at[s]).start()
            return c
        lax.fori_loop(0, TM_MOE, body, 0, unroll=8)

    @pl.when(t == 0)
    def _():
        issue(idx_cur_ref, 0)

    @pl.when((t + 1 < nt) & (tv_ref[jnp.minimum(t + 1, nt - 1)] > 0))
    def _():
        issue(idx_nxt_ref, 1 - slot)

    changed = (t == 0) | (te_ref[t] != te_ref[jnp.maximum(t - 1, 0)])

    @pl.when((tv_ref[t] > 0) & changed)
    def _():
        wgb[...] = _bf16(wg_ref[0])
        wub[...] = _bf16(wu_ref[0])
        wdb[...] = _bf16(wd_ref[0])

    @pl.when(tv_ref[t] > 0)
    def _():
        base = pl.multiple_of(slot * (TM_MOE * ROW_CHUNKS), TM_MOE * ROW_CHUNKS)
        pltpu.make_async_copy(h2t_hbm.at[pl.ds(0, TM_MOE * ROW_CHUNKS), :],
                              xbuf.at[pl.ds(base, TM_MOE * ROW_CHUNKS), :], sem.at[slot]).wait()
        x = _bf16(_load_token_tiles(xbuf, base, TM_MOE))
        hg = _dot(x, wgb[...])
        hu = _dot(x, wub[...])
        h = hg / (1.0 + jnp.exp(-hg)) * hu
        _store_token_tiles(y_ref, _dot(_bf16(h), wdb[...]))

    @pl.when(tv_ref[t] == 0)
    def _():
        y_ref[...] = jnp.zeros_like(y_ref)


def _moe_ffn(tile_expert, tile_valid, tok_sorted, h2t, w_gate, w_up, w_down):
    idx3 = tok_sorted.reshape(MOE_TILES, 1, TM_MOE)
    last = MOE_TILES - 1
    smem_idx = lambda f: pl.BlockSpec((1, 1, TM_MOE), f, memory_space=pltpu.SMEM)
    wspec = lambda shp: pl.BlockSpec((1,) + shp, lambda t, te, tv: (te[t], 0, 0))
    grid_spec = pltpu.PrefetchScalarGridSpec(
        num_scalar_prefetch=2,
        grid=(MOE_TILES,),
        in_specs=[smem_idx(lambda t, te, tv: (t, 0, 0)),
                  smem_idx(lambda t, te, tv: (jnp.minimum(t + 1, last), 0, 0)),
                  pl.BlockSpec(memory_space=pl.ANY),
                  wspec((D_MODEL, EXPERT_FF)), wspec((D_MODEL, EXPERT_FF)), wspec((EXPERT_FF, D_MODEL))],
        out_specs=pl.BlockSpec((TM_MOE * ROW_CHUNKS, LANES), lambda t, te, tv: (t, 0)),
        scratch_shapes=[pltpu.VMEM((2 * TM_MOE * ROW_CHUNKS, LANES), jnp.float32),
                        pltpu.SemaphoreType.DMA((2,)),
                        pltpu.VMEM((D_MODEL, EXPERT_FF), jnp.bfloat16),
                        pltpu.VMEM((D_MODEL, EXPERT_FF), jnp.bfloat16),
                        pltpu.VMEM((EXPERT_FF, D_MODEL), jnp.bfloat16)],
    )
    return pl.pallas_call(
        _moe_ffn_kernel,
        grid_spec=grid_spec,
        out_shape=jax.ShapeDtypeStruct((P_MAX * ROW_CHUNKS, LANES), jnp.float32),
        compiler_params=_params(("arbitrary",)),
        name="moe_ffn",
    )(tile_expert, tile_valid, idx3, idx3, h2t, w_gate, w_up, w_down)


def _combine_kernel(nt, pos_cur_ref, pos_nxt_ref, yt_hbm, h2t_ref, rt_ref, g3_ref, b3_ref, o_ref, abuf, sem):
    t = pl.program_id(0)
    slot = t % 2
    rows = 2 * TM_COMB

    def issue(pos_ref, s):
        def body(r, c):
            _row_gather_copy(yt_hbm, pos_ref[0, 0, r], abuf, s * rows + r, sem.at[s]).start()
            return c
        lax.fori_loop(0, rows, body, 0, unroll=8)

    @pl.when(t == 0)
    def _():
        issue(pos_cur_ref, 0)

    @pl.when(t + 1 < nt)
    def _():
        issue(pos_nxt_ref, 1 - slot)

    base = pl.multiple_of(slot * (rows * ROW_CHUNKS), rows * ROW_CHUNKS)
    pltpu.make_async_copy(yt_hbm.at[pl.ds(0, rows * ROW_CHUNKS), :],
                          abuf.at[pl.ds(base, rows * ROW_CHUNKS), :], sem.at[slot]).wait()
    ya = _load_token_tiles(abuf, base, TM_COMB)
    yb = _load_token_tiles(abuf, base + TM_COMB * ROW_CHUNKS, TM_COMB)
    rt = rt_ref[...]
    ff = rt[:, 2:3] * ya + rt[:, 3:4] * yb
    h2 = _load_token_tiles(h2t_ref, 0, TM_COMB)
    o_ref[...] = _layer_norm(ALPHA * h2 + ff, g3_ref[...], b3_ref[...])


def _combine(pos3, yt, h2t, rt, g3, b3, tile0, n_tiles):
    last = tile0 + n_tiles - 1
    smem_pos = lambda f: pl.BlockSpec((1, 1, 2 * TM_COMB), f, memory_space=pltpu.SMEM)
    full = lambda a: pl.BlockSpec(a.shape, lambda i: (0,) * a.ndim)
    return pl.pallas_call(
        functools.partial(_combine_kernel, n_tiles),
        grid=(n_tiles,),
        in_specs=[smem_pos(lambda i: (tile0 + i, 0, 0)),
                  smem_pos(lambda i: (jnp.minimum(tile0 + i + 1, last), 0, 0)),
                  pl.BlockSpec(memory_space=pl.ANY),
                  pl.BlockSpec((TM_COMB * ROW_CHUNKS, LANES), lambda i: (tile0 + i, 0)),
                  pl.BlockSpec((TM_COMB, LANES), lambda i: (tile0 + i, 0)),
                  full(g3), full(b3)],
        out_specs=pl.BlockSpec((TM_COMB, D_MODEL), lambda i: (i, 0)),
        out_shape=jax.ShapeDtypeStruct((n_tiles * TM_COMB, D_MODEL), jnp.float32),
        scratch_shapes=[pltpu.VMEM((2 * 2 * TM_COMB * ROW_CHUNKS, LANES), jnp.float32),
                        pltpu.SemaphoreType.DMA((2,))],
        compiler_params=_params(("arbitrary",)),
        name="moe_combine",
    )(pos3, pos3, yt, h2t, rt, g3, b3)


def _routing_plan(rt):
    e = jnp.concatenate([rt[:, 0], rt[:, 1]]).astype(jnp.int32)
    onehot = (e[:, None] == jnp.arange(N_EXPERTS, dtype=jnp.int32)[None, :]).astype(jnp.int32)
    csum = jnp.cumsum(onehot, axis=0)
    rank = jnp.take_along_axis(csum, e[:, None], axis=1)[:, 0] - 1
    counts = csum[-1]
    padded = (counts + TM_MOE - 1) // TM_MOE * TM_MOE
    ends = jnp.cumsum(padded)
    starts = ends - padded
    pos = starts[e] + rank
    tok = jnp.tile(jnp.arange(N_ALL, dtype=jnp.int32), 2)
    tok_sorted = jnp.zeros((P_MAX,), jnp.int32).at[pos].set(tok)
    tile_start = jnp.arange(MOE_TILES, dtype=jnp.int32) * TM_MOE
    owner = jnp.sum((ends[None, :] <= tile_start[:, None]).astype(jnp.int32), axis=1)
    tile_expert = jnp.minimum(owner, N_EXPERTS - 1)
    tile_valid = (tile_start < ends[-1]).astype(jnp.int32)
    pos3 = jnp.concatenate([pos[:N_ALL].reshape(-1, 1, TM_COMB), pos[N_ALL:].reshape(-1, 1, TM_COMB)], axis=2)
    return tile_expert, tile_valid, tok_sorted, pos3


def kernel(x_prompt, x_sample, mem_prompt, cache_swa_k, cache_swa_v, cache_conv, cache_mem_k, cache_mem_v,
           w_in, sinks, conv_w, w_mix_out, ln1_g, ln1_b, w_q_mem, w_k_mem, w_v_mem, w_o_mem, ln2_g, ln2_b,
           w_router_group, b_router_group, w_router_expert, b_router_expert, w_gate, w_up, w_down,
           ln3_g, ln3_b):
    f32 = jnp.float32
    row = lambda a: a.reshape(1, -1).astype(f32)
    w_in_b, wmix_b, wq_b, wk_b, wv_b, wo_b = (_bf16(w) for w in (w_in, w_mix_out, w_q_mem, w_k_mem, w_v_mem, w_o_mem))
    g1, b1, g2, b2, g3, b3 = (row(a) for a in (ln1_g, ln1_b, ln2_g, ln2_b, ln3_g, ln3_b))
    pad = LANES - N_GROUPS - N_EXPERTS
    wr = jnp.concatenate([w_router_group, w_router_expert, jnp.zeros((D_MODEL, pad), f32)], axis=1)
    wrh = _bf16(wr)
    wrl = _bf16(wr - wrh.astype(f32))
    br = jnp.concatenate([b_router_group, b_router_expert, jnp.zeros((pad,), f32)]).reshape(1, LANES)

    xs = x_sample.reshape(N_SAMPLE, D_MODEL)
    cos_s, sin_s = _rope_tables(PAST_LEN + jnp.arange(DEC_SEQ))
    cos_s, sin_s = jnp.tile(cos_s, (DEC_BATCH, 1)), jnp.tile(sin_s, (DEC_BATCH, 1))
    c0 = jnp.repeat(cache_conv[:, 0], DEC_SEQ, axis=0)
    c1 = jnp.repeat(cache_conv[:, 1], DEC_SEQ, axis=0)
    q_s, k_s, v_s, conv_s, u_s = _proj_sample(xs, w_in_b, cos_s, sin_s, conv_w, c0, c1)
    attn_s, swa_k_s, swa_v_s = _swa_sample(
        sinks, q_s, k_s, v_s,
        cache_swa_k.reshape(DEC_BATCH, WINDOW, KV_WIDTH), cache_swa_v.reshape(DEC_BATCH, WINDOW, KV_WIDTH))
    h1_s, qm_s = _post_a_sample(attn_s, conv_s, xs, wmix_b, g1, b1, wq_b)
    o_s = _mem_attn_sample(qm_s, cache_mem_k.reshape(DEC_BATCH, N_MEM, D_MODEL),
                           cache_mem_v.reshape(DEC_BATCH, N_MEM, D_MODEL))
    h2t_s, rt_s = _post_b_sample(o_s, h1_s, wo_b, g2, b2, wrh, wrl, br)

    xp = x_prompt.reshape(N_PROMPT, D_MODEL)
    cos_p, sin_p = _rope_tables(jnp.arange(N_PROMPT))
    q_p, kx_p, vx_p, conv_p, k_tail, v_tail, u_tail = _proj_prompt(xp, w_in_b, cos_p, sin_p, conv_w)
    attn_p = _swa_prompt(sinks, q_p, kx_p, vx_p)
    mk, mv, mk_b, mv_b = _mem_kv(mem_prompt.reshape(N_MEM, D_MODEL), wk_b, wv_b)
    h2t, rt = _post_prompt(attn_p, conv_p, xp, wmix_b, g1, b1, wq_b, mk_b, mv_b, wo_b, g2, b2,
                           wrh, wrl, br, h2t_s, rt_s)

    tile_expert, tile_valid, tok_sorted, pos3 = _routing_plan(rt)
    yt = _moe_ffn(tile_expert, tile_valid, tok_sorted, h2t, w_gate, w_up, w_down)
    y_p = _combine(pos3, yt, h2t, rt, g3, b3, 0, N_PROMPT // TM_COMB)
    y_s = _combine(pos3, yt, h2t, rt, g3, b3, N_PROMPT // TM_COMB, N_SAMPLE // TM_COMB)

    return (y_p.reshape(1, SEQ, D_MODEL),
            y_s.reshape(DEC_BATCH, DEC_SEQ, D_MODEL),
            k_tail.reshape(1, WINDOW, N_KV_HEADS, HEAD_DIM),
            v_tail.reshape(1, WINDOW, N_KV_HEADS, HEAD_DIM),
            u_tail[SUBLANES - (CONV_K - 1):].reshape(1, CONV_K - 1, CONV_CH),
            mk.reshape(1, N_MEM, MEM_HEADS, MEM_HEAD_DIM),
            mv.reshape(1, N_MEM, MEM_HEADS, MEM_HEAD_DIM),
            swa_k_s.reshape(DEC_BATCH, WINDOW, N_KV_HEADS, HEAD_DIM),
            swa_v_s.reshape(DEC_BATCH, WINDOW, N_KV_HEADS, HEAD_DIM),
            u_s.reshape(DEC_BATCH, DEC_SEQ, CONV_CH)[:, DEC_SEQ - (CONV_K - 1):])
```

```python
import functools

import jax
import jax.numpy as jnp
from jax import lax
from jax.experimental import pallas as pl
from jax.experimental.pallas import tpu as pltpu

D_MODEL = 1024
SEQ = 16384
DEC_BATCH = 128
DEC_SEQ = 4
PAST_LEN = 16384
ATTN_WIDTH = 512
CONV_CH = 512
HEAD_DIM = 64
N_HEADS = 8
N_KV_HEADS = 2
KV_WIDTH = 128
WINDOW = 128
ROPE_THETA = 500000.0
ROPE_DIM = 16
CONV_K = 3
Q_END = ATTN_WIDTH
K_END = Q_END + KV_WIDTH
V_END = K_END + KV_WIDTH
B_END = V_END + CONV_CH
C_END = B_END + CONV_CH
IN_WIDTH = C_END + CONV_CH
N_MEM = 256
MEM_HEADS = 4
MEM_HEAD_DIM = 256
N_GROUPS = 4
EXPERTS_PER_GROUP = 8
N_EXPERTS = 32
EXPERT_FF = 256
ALPHA = 2.0 ** 0.25
LN_EPS = 1e-5

LANES = 128
SUBLANES = 8
ROW_CHUNKS = D_MODEL // LANES
VMEM_LIMIT = 56 * 1024 * 1024

N_PROMPT = SEQ
N_SAMPLE = DEC_BATCH * DEC_SEQ
N_ALL = N_PROMPT + N_SAMPLE
TM_PROJ = 512
TM_POST = 512
TM_MOE = 256
TM_COMB = 256
N_ASSIGN = 2 * N_ALL
MOE_TILES = N_ASSIGN // TM_MOE
MOE_ITEMS = MOE_TILES + N_EXPERTS
SAMPLE_BB = 4

assert ROW_CHUNKS == SUBLANES
assert N_SAMPLE == TM_POST
assert N_ASSIGN % TM_MOE == 0 and N_ALL % TM_COMB == 0


def _params(sem, vmem=VMEM_LIMIT):
    return pltpu.CompilerParams(dimension_semantics=sem, vmem_limit_bytes=vmem)


def _bf16(x):
    return x.astype(jnp.bfloat16)


def _dot(a, b):
    return jnp.dot(a, b, preferred_element_type=jnp.float32)


def _dot_nt(a, b):
    return lax.dot_general(a, b, (((1,), (1,)), ((), ())), preferred_element_type=jnp.float32)


def _layer_norm(x, g, b):
    mu = jnp.mean(x, axis=-1, keepdims=True)
    xc = x - mu
    var = jnp.mean(xc * xc, axis=-1, keepdims=True)
    return xc * lax.rsqrt(var + LN_EPS) * g + b


def _rope(x, cos_t, sin_t):
    lane = lax.broadcasted_iota(jnp.int32, x.shape, 1) % HEAD_DIM
    half = ROPE_DIM // 2
    partner = jnp.where(lane < half, pltpu.roll(x, LANES - half, axis=1), pltpu.roll(x, half, axis=1))
    return x * cos_t + partner * sin_t


def _head_slabs(x):
    lane = lax.broadcasted_iota(jnp.int32, x.shape, 1)
    lo = lane < HEAD_DIM
    sw = pltpu.roll(x, HEAD_DIM, axis=1)
    zero = jnp.zeros_like(x)
    slabs = [jnp.where(lo, x, zero), jnp.where(lo, zero, sw), jnp.where(lo, sw, zero), jnp.where(lo, zero, x)]
    return _bf16(jnp.concatenate(slabs, axis=1))


def _store_token_tiles(ref, val):
    rows = val.shape[0]
    for c in range(ROW_CHUNKS):
        ref[pl.ds(c, rows, stride=ROW_CHUNKS), :] = val[:, c * LANES:(c + 1) * LANES]


def _load_token_tiles(ref, base, rows):
    return jnp.concatenate(
        [ref[pl.ds(base + c, rows, stride=ROW_CHUNKS), :] for c in range(ROW_CHUNKS)], axis=1)


def _proj_common(x_ref, w_ref, cos_ref, sin_ref):
    xb = _bf16(x_ref[...])
    cos_t = cos_ref[...]
    sin_t = sin_ref[...]
    q = _dot(xb, w_ref[:, 0:Q_END])
    q_rot = jnp.concatenate(
        [_rope(q[:, p * LANES:(p + 1) * LANES], cos_t, sin_t) for p in range(ATTN_WIDTH // LANES)], axis=1)
    q_out = _bf16(q_rot * (HEAD_DIM ** -0.5))
    k = _rope(_dot(xb, w_ref[:, Q_END:K_END]), cos_t, sin_t)
    v = _dot(xb, w_ref[:, K_END:V_END])
    bg = _dot(xb, w_ref[:, V_END:B_END])
    u = _dot(xb, w_ref[:, B_END:C_END]) * _dot(xb, w_ref[:, C_END:IN_WIDTH])
    return q_out, k, v, bg, u


def _conv3(bg, u, u1, u2, cw_ref):
    cw = cw_ref[...]
    return bg * (cw[0:1, :] * u2 + cw[1:2, :] * u1 + cw[2:3, :] * u)


def _proj_prompt_kernel(x_ref, w_ref, cos_ref, sin_ref, cw_ref,
                        q_ref, kx_ref, vx_ref, conv_ref, ktail_ref, vtail_ref, utail_ref, carry_ref):
    @pl.when(pl.program_id(0) == 0)
    def _():
        carry_ref[...] = jnp.zeros_like(carry_ref)

    q_out, k, v, bg, u = _proj_common(x_ref, w_ref, cos_ref, sin_ref)
    tm = u.shape[0]
    ext = jnp.concatenate([carry_ref[...], u], axis=0)
    u1 = pltpu.roll(ext, 1, axis=0)[SUBLANES:SUBLANES + tm]
    u2 = pltpu.roll(ext, 2, axis=0)[SUBLANES:SUBLANES + tm]
    q_ref[...] = q_out
    kx_ref[...] = _head_slabs(k)
    vx_ref[...] = _head_slabs(v)
    conv_ref[...] = _bf16(_conv3(bg, u, u1, u2, cw_ref))
    ktail_ref[...] = k[tm - WINDOW:tm]
    vtail_ref[...] = v[tm - WINDOW:tm]
    utail_ref[...] = u[tm - SUBLANES:tm]
    carry_ref[...] = u[tm - SUBLANES:tm]


def _proj_sample_kernel(x_ref, w_ref, cos_ref, sin_ref, cw_ref, c0_ref, c1_ref,
                        q_ref, k_ref, v_ref, conv_ref, u_ref):
    q_out, k, v, bg, u = _proj_common(x_ref, w_ref, cos_ref, sin_ref)
    t = lax.broadcasted_iota(jnp.int32, u.shape, 0) % DEC_SEQ
    c0 = c0_ref[...]
    c1 = c1_ref[...]
    u1 = jnp.where(t >= 1, pltpu.roll(u, 1, axis=0), c1)
    u2 = jnp.where(t >= 2, pltpu.roll(u, 2, axis=0), jnp.where(t == 1, c1, c0))
    q_ref[...] = q_out.astype(jnp.float32)
    k_ref[...] = k
    v_ref[...] = v
    conv_ref[...] = _bf16(_conv3(bg, u, u1, u2, cw_ref))
    u_ref[...] = u


def _rope_tables(pos):
    half = ROPE_DIM // 2
    inv = ROPE_THETA ** (-jnp.arange(0, ROPE_DIM, 2, dtype=jnp.float32) / ROPE_DIM)
    ang = pos.astype(jnp.float32)[:, None] * inv
    cos, sin = jnp.cos(ang), jnp.sin(ang)
    n = pos.shape[0]
    pad = HEAD_DIM - ROPE_DIM
    cos_h = jnp.concatenate([cos, cos, jnp.ones((n, pad), jnp.float32)], axis=1)
    sin_h = jnp.concatenate([-sin, sin, jnp.zeros((n, pad), jnp.float32)], axis=1)
    reps = LANES // HEAD_DIM
    assert half * 2 == ROPE_DIM
    return jnp.tile(cos_h, (1, reps)), jnp.tile(sin_h, (1, reps))


def _proj_prompt(x, w_in_b, cos_t, sin_t, conv_w):
    n = x.shape[0]
    tm = TM_PROJ
    row = lambda w: pl.BlockSpec((tm, w), lambda i: (i, 0))
    full = lambda a: pl.BlockSpec(a.shape, lambda i: (0,) * a.ndim)
    const = lambda r, w: pl.BlockSpec((r, w), lambda i: (0, 0))
    return pl.pallas_call(
        _proj_prompt_kernel,
        grid=(n // tm,),
        in_specs=[row(D_MODEL), full(w_in_b), row(LANES), row(LANES), full(conv_w)],
        out_specs=[row(ATTN_WIDTH), row(4 * LANES), row(4 * LANES), row(CONV_CH),
                   const(WINDOW, KV_WIDTH), const(WINDOW, KV_WIDTH), const(SUBLANES, CONV_CH)],
        out_shape=[jax.ShapeDtypeStruct((n, ATTN_WIDTH), jnp.bfloat16),
                   jax.ShapeDtypeStruct((n, 4 * LANES), jnp.bfloat16),
                   jax.ShapeDtypeStruct((n, 4 * LANES), jnp.bfloat16),
                   jax.ShapeDtypeStruct((n, CONV_CH), jnp.bfloat16),
                   jax.ShapeDtypeStruct((WINDOW, KV_WIDTH), jnp.float32),
                   jax.ShapeDtypeStruct((WINDOW, KV_WIDTH), jnp.float32),
                   jax.ShapeDtypeStruct((SUBLANES, CONV_CH), jnp.float32)],
        scratch_shapes=[pltpu.VMEM((SUBLANES, CONV_CH), jnp.float32)],
        compiler_params=_params(("arbitrary",)),
        name="proj_prompt",
    )(x, w_in_b, cos_t, sin_t, conv_w)


def _proj_sample(x, w_in_b, cos_t, sin_t, conv_w, c0, c1):
    n = x.shape[0]
    full = lambda a: pl.BlockSpec(a.shape, lambda i: (0,) * a.ndim)
    out = lambda w, dt: jax.ShapeDtypeStruct((n, w), dt)
    blk = lambda w: pl.BlockSpec((n, w), lambda i: (0, 0))
    return pl.pallas_call(
        _proj_sample_kernel,
        grid=(1,),
        in_specs=[full(x), full(w_in_b), full(cos_t), full(sin_t), full(conv_w), full(c0), full(c1)],
        out_specs=[blk(ATTN_WIDTH), blk(KV_WIDTH), blk(KV_WIDTH), blk(CONV_CH), blk(CONV_CH)],
        out_shape=[out(ATTN_WIDTH, jnp.float32), out(KV_WIDTH, jnp.float32), out(KV_WIDTH, jnp.float32),
                   out(CONV_CH, jnp.bfloat16), out(CONV_CH, jnp.float32)],
        compiler_params=_params(("arbitrary",)),
        name="proj_sample",
    )(x, w_in_b, cos_t, sin_t, conv_w, c0, c1)


def _sink_softmax_pv(s, valid, sink, vx):
    s = jnp.where(valid, s, -jnp.inf)
    m = jnp.maximum(jnp.max(s, axis=1, keepdims=True), sink)
    p = jnp.exp(s - m)
    den = jnp.sum(p, axis=1, keepdims=True) + jnp.exp(sink - m)
    return _dot(_bf16(p), vx) / den


def _swa_prompt_kernel(sinks_ref, q_ref, kc_ref, kp_ref, vc_ref, vp_ref, o_ref):
    blk = pl.program_id(0)
    kcat = jnp.concatenate([kp_ref[...], kc_ref[...]], axis=0)
    vcat = jnp.concatenate([vp_ref[...], vc_ref[...]], axis=0)
    i = lax.broadcasted_iota(jnp.int32, (WINDOW, 2 * WINDOW), 0)
    j = lax.broadcasted_iota(jnp.int32, (WINDOW, 2 * WINDOW), 1)
    valid = (j > i) & (j <= i + WINDOW) & ((blk > 0) | (j >= WINDOW))
    for p in range(N_HEADS // 2):
        qs = q_ref[:, p * LANES:(p + 1) * LANES]
        acc = None
        for e in range(2):
            hd = 2 * p + e
            slab = 2 * (hd // (N_HEADS // N_KV_HEADS)) + e
            kx = kcat[:, slab * LANES:(slab + 1) * LANES]
            vx = vcat[:, slab * LANES:(slab + 1) * LANES]
            o = _sink_softmax_pv(_dot_nt(qs, kx), valid, sinks_ref[hd], vx)
            acc = o if acc is None else acc + o
        o_ref[:, p * LANES:(p + 1) * LANES] = _bf16(acc)


def _swa_prompt(sinks, q, kx, vx):
    n = q.shape[0]
    nb = n // WINDOW
    cur = lambda w: pl.BlockSpec((WINDOW, w), lambda i: (i, 0))
    prev = lambda w: pl.BlockSpec((WINDOW, w), lambda i: (jnp.maximum(i - 1, 0), 0))
    return pl.pallas_call(
        _swa_prompt_kernel,
        grid=(nb,),
        in_specs=[pl.BlockSpec(memory_space=pltpu.SMEM), cur(ATTN_WIDTH),
                  cur(4 * LANES), prev(4 * LANES), cur(4 * LANES), prev(4 * LANES)],
        out_specs=cur(ATTN_WIDTH),
        out_shape=jax.ShapeDtypeStruct((n, ATTN_WIDTH), jnp.bfloat16),
        compiler_params=_params(("arbitrary",)),
        name="swa_prompt",
    )(sinks, q, kx, kx, vx, vx)


SWA_ROWS = N_HEADS * DEC_SEQ
NEW_ROWS = 2 * SUBLANES


def _swa_sample_kernel(q_ref, sink_ref, kn_ref, vn_ref, kt_ref, vt_ref, o_ref, okt_ref, ovt_ref):
    t = lax.broadcasted_iota(jnp.int32, (SWA_ROWS, WINDOW), 0) % DEC_SEQ
    j = lax.broadcasted_iota(jnp.int32, (SWA_ROWS, WINDOW), 1)
    valid_c = j > t
    valid_n = (lax.broadcasted_iota(jnp.int32, (SWA_ROWS, NEW_ROWS), 1)
               <= lax.broadcasted_iota(jnp.int32, (SWA_ROWS, NEW_ROWS), 0) % DEC_SEQ)
    lane = lax.broadcasted_iota(jnp.int32, (KV_WIDTH, WINDOW), 1)
    sink = sink_ref[:, 0:1]
    shift = WINDOW - DEC_SEQ
    zrows = jnp.zeros((KV_WIDTH - NEW_ROWS, KV_WIDTH), jnp.float32)
    for b in range(SAMPLE_BB):
        q = _bf16(q_ref[b])
        kt, vt = kt_ref[b], vt_ref[b]
        kn, vn = kn_ref[b], vn_ref[b]
        s_c = jnp.where(valid_c, _dot(q, _bf16(kt)), -jnp.inf)
        s_n = jnp.where(valid_n, _dot_nt(q, _bf16(kn)), -jnp.inf)
        m = jnp.maximum(jnp.maximum(jnp.max(s_c, axis=1, keepdims=True), jnp.max(s_n, axis=1, keepdims=True)), sink)
        p_c = jnp.exp(s_c - m)
        p_n = jnp.exp(s_n - m)
        den = jnp.sum(p_c, axis=1, keepdims=True) + jnp.sum(p_n, axis=1, keepdims=True) + jnp.exp(sink - m)
        o_ref[b] = (_dot_nt(_bf16(p_c), _bf16(vt)) + _dot(_bf16(p_n), _bf16(vn))) / den
        for old, new, dst in ((kt, kn, okt_ref), (vt, vn, ovt_ref)):
            new_cols = pltpu.roll(jnp.concatenate([new, zrows], axis=0).T, shift, axis=1)
            dst[b] = jnp.where(lane >= shift, new_cols, pltpu.roll(old, shift, axis=1))


def _swa_sample(sinks, q, kn, vn, cache_k, cache_v):
    nb = cache_k.shape[0]
    bb = SAMPLE_BB
    groups = N_HEADS // N_KV_HEADS
    qh = q.reshape(nb, DEC_SEQ, N_KV_HEADS, groups, HEAD_DIM).transpose(0, 2, 3, 1, 4)
    qh = qh.reshape(nb, N_KV_HEADS, groups * DEC_SEQ, HEAD_DIM)
    zeros = jnp.zeros_like(qh[:, 0])
    qbd = jnp.concatenate([jnp.concatenate([qh[:, 0], zeros], axis=-1),
                           jnp.concatenate([zeros, qh[:, 1]], axis=-1)], axis=1)
    sink_col = jnp.broadcast_to(jnp.repeat(sinks, DEC_SEQ).reshape(SWA_ROWS, 1), (SWA_ROWS, LANES))
    pad8 = lambda a: jnp.pad(a.reshape(nb, DEC_SEQ, KV_WIDTH), ((0, 0), (0, NEW_ROWS - DEC_SEQ), (0, 0)))
    to_t = lambda c: c.transpose(0, 2, 3, 1).reshape(nb, KV_WIDTH, WINDOW)
    blk = lambda r, w: pl.BlockSpec((bb, r, w), lambda i: (i, 0, 0))
    o, okt, ovt = pl.pallas_call(
        _swa_sample_kernel,
        grid=(nb // bb,),
        in_specs=[blk(SWA_ROWS, KV_WIDTH), pl.BlockSpec((SWA_ROWS, LANES), lambda i: (0, 0)),
                  blk(NEW_ROWS, KV_WIDTH), blk(NEW_ROWS, KV_WIDTH), blk(KV_WIDTH, WINDOW), blk(KV_WIDTH, WINDOW)],
        out_specs=[blk(SWA_ROWS, KV_WIDTH), blk(KV_WIDTH, WINDOW), blk(KV_WIDTH, WINDOW)],
        out_shape=[jax.ShapeDtypeStruct((nb, SWA_ROWS, KV_WIDTH), jnp.float32),
                   jax.ShapeDtypeStruct((nb, KV_WIDTH, WINDOW), jnp.float32),
                   jax.ShapeDtypeStruct((nb, KV_WIDTH, WINDOW), jnp.float32)],
        compiler_params=_params(("arbitrary",)),
        name="swa_sample",
    )(qbd, sink_col, pad8(kn), pad8(vn), to_t(cache_k), to_t(cache_v))
    o = o.reshape(nb, N_KV_HEADS, groups, DEC_SEQ, N_KV_HEADS, HEAD_DIM)
    attn = jnp.stack([o[:, h, :, :, h, :] for h in range(N_KV_HEADS)], axis=1)
    attn = attn.transpose(0, 3, 1, 2, 4).reshape(nb * DEC_SEQ, ATTN_WIDTH)
    from_t = lambda c: c.reshape(nb, N_KV_HEADS, HEAD_DIM, WINDOW).transpose(0, 3, 1, 2)
    return attn, from_t(okt), from_t(ovt)


def _mem_kv_kernel(mem_ref, wk_ref, wv_ref, mk_ref, mv_ref, mkb_ref, mvb_ref):
    mb = _bf16(mem_ref[...])
    mk = _dot(mb, wk_ref[...])
    mv = _dot(mb, wv_ref[...])
    mk_ref[...] = mk
    mv_ref[...] = mv
    mkb_ref[...] = _bf16(mk)
    mvb_ref[...] = _bf16(mv)


def _mem_kv(mem, wk_b, wv_b):
    full = lambda a: pl.BlockSpec(a.shape, lambda i: (0,) * a.ndim)
    blk = pl.BlockSpec((N_MEM, D_MODEL), lambda i: (0, 0))
    f32 = jax.ShapeDtypeStruct((N_MEM, D_MODEL), jnp.float32)
    b16 = jax.ShapeDtypeStruct((N_MEM, D_MODEL), jnp.bfloat16)
    return pl.pallas_call(
        _mem_kv_kernel,
        grid=(1,),
        in_specs=[full(mem), full(wk_b), full(wv_b)],
        out_specs=[blk, blk, blk, blk],
        out_shape=[f32, f32, b16, b16],
        compiler_params=_params(("arbitrary",)),
        name="mem_kv",
    )(mem, wk_b, wv_b)


def _mix_ln1(attn_ref, conv_ref, x_ref, wmix_ref, g1_ref, b1_ref):
    mix = _dot(_bf16(attn_ref[...]), wmix_ref[0:ATTN_WIDTH, :]) + _dot(conv_ref[...], wmix_ref[ATTN_WIDTH:, :])
    return _layer_norm(ALPHA * x_ref[...] + mix, g1_ref[...], b1_ref[...])


def _mem_q(h1, wq_ref):
    return _bf16(_dot(_bf16(h1), wq_ref[...]) * (MEM_HEAD_DIM ** -0.5))


def _route(h2, wrh_ref, wrl_ref, br_ref):
    hi = _bf16(h2)
    lo = _bf16(h2 - hi.astype(jnp.float32))
    logits = _dot(hi, wrh_ref[...]) + _dot(lo, wrh_ref[...]) + _dot(hi, wrl_ref[...]) + br_ref[...]
    lane_i = lax.broadcasted_iota(jnp.int32, logits.shape, 1)
    lane = lane_i.astype(jnp.float32)
    big = jnp.float32(LANES)
    is_g = lane_i < N_GROUPS
    gl = jnp.where(is_g, logits, -jnp.inf)
    gmax = jnp.max(gl, axis=1, keepdims=True)
    gidx = jnp.min(jnp.where(is_g & (logits == gmax), lane, big), axis=1, keepdims=True)
    gsum = jnp.sum(jnp.exp(gl - gmax), axis=1, keepdims=True)
    gw = 1.0 / gsum
    eid = lane_i - N_GROUPS
    assert EXPERTS_PER_GROUP == 8
    grp = lax.shift_right_arithmetic(eid, jnp.full_like(eid, 3)).astype(jnp.float32)
    in_e = (lane_i >= N_GROUPS) & (lane_i < N_GROUPS + N_EXPERTS) & (grp == gidx)
    v1 = jnp.max(jnp.where(in_e, logits, -jnp.inf), axis=1, keepdims=True)
    i1 = jnp.min(jnp.where(in_e & (logits == v1), lane, big), axis=1, keepdims=True)
    rest = in_e & (lane != i1)
    v2 = jnp.max(jnp.where(rest, logits, -jnp.inf), axis=1, keepdims=True)
    i2 = jnp.min(jnp.where(rest & (logits == v2), lane, big), axis=1, keepdims=True)
    ex = jnp.exp(v2 - v1)
    den = 1.0 + ex
    w1 = gw / den
    w2 = gw * ex / den
    zero = jnp.zeros_like(logits)
    return jnp.where(lane_i == 0, i1 - N_GROUPS,
                     jnp.where(lane_i == 1, i2 - N_GROUPS,
                               jnp.where(lane_i == 2, w1, jnp.where(lane_i == 3, w2, zero))))


def _post_prompt_kernel(attn_ref, conv_ref, x_ref, wmix_ref, g1_ref, b1_ref, wq_ref, mk_ref, mv_ref, wo_ref,
                        g2_ref, b2_ref, wrh_ref, wrl_ref, br_ref, h2s_ref, rts_ref, h2t_ref, rt_ref):
    is_tail = pl.program_id(0) == N_PROMPT // TM_POST

    @pl.when(is_tail)
    def _():
        h2t_ref[...] = h2s_ref[...]
        rt_ref[...] = rts_ref[...]

    @pl.when(jnp.logical_not(is_tail))
    def _():
        h1 = _mix_ln1(attn_ref, conv_ref, x_ref, wmix_ref, g1_ref, b1_ref)
        qm = _mem_q(h1, wq_ref)
        outs = []
        for h in range(MEM_HEADS):
            sl = slice(h * MEM_HEAD_DIM, (h + 1) * MEM_HEAD_DIM)
            s = _dot_nt(qm[:, sl], mk_ref[:, sl])
            m = jnp.max(s, axis=1, keepdims=True)
            p = jnp.exp(s - m)
            den = jnp.sum(p, axis=1, keepdims=True)
            outs.append(_dot(_bf16(p), mv_ref[:, sl]) / den)
        o = _bf16(jnp.concatenate(outs, axis=1))
        h2 = _layer_norm(ALPHA * h1 + _dot(o, wo_ref[...]), g2_ref[...], b2_ref[...])
        _store_token_tiles(h2t_ref, h2)
        rt_ref[...] = _route(h2, wrh_ref, wrl_ref, br_ref)


def _post_prompt(attn, conv, x, wmix_b, g1, b1, wq_b, mk_b, mv_b, wo_b, g2, b2, wrh, wrl, br, h2t_s, rt_s):
    n = x.shape[0]
    tm = TM_POST
    steps = n // tm
    row = lambda w: pl.BlockSpec((tm, w), lambda i: (jnp.minimum(i, steps - 1), 0))
    full = lambda a: pl.BlockSpec(a.shape, lambda i: (0,) * a.ndim)
    weights = (wmix_b, g1, b1, wq_b, mk_b, mv_b, wo_b, g2, b2, wrh, wrl, br, h2t_s, rt_s)
    n_out = n + h2t_s.shape[0] // ROW_CHUNKS
    return pl.pallas_call(
        _post_prompt_kernel,
        grid=(steps + 1,),
        in_specs=[row(ATTN_WIDTH), row(CONV_CH), row(D_MODEL)] + [full(a) for a in weights],
        out_specs=[pl.BlockSpec((tm * ROW_CHUNKS, LANES), lambda i: (i, 0)),
                   pl.BlockSpec((tm, LANES), lambda i: (i, 0))],
        out_shape=[jax.ShapeDtypeStruct((n_out * ROW_CHUNKS, LANES), jnp.float32),
                   jax.ShapeDtypeStruct((n_out, LANES), jnp.float32)],
        compiler_params=_params(("arbitrary",)),
        name="post_prompt",
    )(attn, conv, x, *weights)


def _post_a_sample_kernel(attn_ref, conv_ref, x_ref, wmix_ref, g1_ref, b1_ref, wq_ref, h1_ref, qm_ref):
    h1 = _mix_ln1(attn_ref, conv_ref, x_ref, wmix_ref, g1_ref, b1_ref)
    h1_ref[...] = h1
    qm_ref[...] = _mem_q(h1, wq_ref).astype(jnp.float32)


def _post_a_sample(attn, conv, x, wmix_b, g1, b1, wq_b):
    n = x.shape[0]
    args = (attn, conv, x, wmix_b, g1, b1, wq_b)
    full = lambda a: pl.BlockSpec(a.shape, lambda i: (0,) * a.ndim)
    blk = pl.BlockSpec((n, D_MODEL), lambda i: (0, 0))
    return pl.pallas_call(
        _post_a_sample_kernel,
        grid=(1,),
        in_specs=[full(a) for a in args],
        out_specs=[blk, blk],
        out_shape=[jax.ShapeDtypeStruct((n, D_MODEL), jnp.float32),
                   jax.ShapeDtypeStruct((n, D_MODEL), jnp.float32)],
        compiler_params=_params(("arbitrary",)),
        name="post_a_sample",
    )(*args)


MEM_ROWS = MEM_HEADS * DEC_SEQ


def _mem_attn_sample_kernel(q_ref, mk_ref, mv_ref, o_ref):
    nk = N_MEM * MEM_HEADS
    row_h = lax.broadcasted_iota(jnp.int32, (MEM_ROWS, nk), 0) // DEC_SEQ
    key_h = lax.broadcasted_iota(jnp.int32, (MEM_ROWS, nk), 1) % MEM_HEADS
    own = row_h == key_h
    for b in range(SAMPLE_BB):
        k2 = _bf16(mk_ref[b].reshape(nk, MEM_HEAD_DIM))
        v2 = _bf16(mv_ref[b].reshape(nk, MEM_HEAD_DIM))
        s = jnp.where(own, _dot_nt(_bf16(q_ref[b]), k2), -jnp.inf)
        m = jnp.max(s, axis=1, keepdims=True)
        p = jnp.exp(s - m)
        den = jnp.sum(p, axis=1, keepdims=True)
        o_ref[b] = _dot(_bf16(p), v2) / den


def _mem_attn_sample(qm, mk, mv):
    nb = mk.shape[0]
    bb = SAMPLE_BB
    q = qm.reshape(nb, DEC_SEQ, MEM_HEADS, MEM_HEAD_DIM).transpose(0, 2, 1, 3).reshape(nb, MEM_ROWS, MEM_HEAD_DIM)
    rows = pl.BlockSpec((bb, MEM_ROWS, MEM_HEAD_DIM), lambda i: (i, 0, 0))
    kv = pl.BlockSpec((bb, N_MEM, MEM_HEADS, MEM_HEAD_DIM), lambda i: (i, 0, 0, 0))
    o = pl.pallas_call(
        _mem_attn_sample_kernel,
        grid=(nb // bb,),
        in_specs=[rows, kv, kv],
        out_specs=rows,
        out_shape=jax.ShapeDtypeStruct((nb, MEM_ROWS, MEM_HEAD_DIM), jnp.float32),
        compiler_params=_params(("arbitrary",)),
        name="mem_attn_sample",
    )(q, mk, mv)
    return o.reshape(nb, MEM_HEADS, DEC_SEQ, MEM_HEAD_DIM).transpose(0, 2, 1, 3).reshape(nb * DEC_SEQ, D_MODEL)


def _post_b_sample_kernel(o_ref, h1_ref, wo_ref, g2_ref, b2_ref, wrh_ref, wrl_ref, br_ref, h2t_ref, rt_ref):
    h2 = _layer_norm(ALPHA * h1_ref[...] + _dot(_bf16(o_ref[...]), wo_ref[...]), g2_ref[...], b2_ref[...])
    _store_token_tiles(h2t_ref, h2)
    rt_ref[...] = _route(h2, wrh_ref, wrl_ref, br_ref)


def _post_b_sample(o, h1, wo_b, g2, b2, wrh, wrl, br):
    n = h1.shape[0]
    args = (o, h1, wo_b, g2, b2, wrh, wrl, br)
    full = lambda a: pl.BlockSpec(a.shape, lambda i: (0,) * a.ndim)
    return pl.pallas_call(
        _post_b_sample_kernel,
        grid=(1,),
        in_specs=[full(a) for a in args],
        out_specs=[pl.BlockSpec((n * ROW_CHUNKS, LANES), lambda i: (0, 0)),
                   pl.BlockSpec((n, LANES), lambda i: (0, 0))],
        out_shape=[jax.ShapeDtypeStruct((n * ROW_CHUNKS, LANES), jnp.float32),
                   jax.ShapeDtypeStruct((n, LANES), jnp.float32)],
        compiler_params=_params(("arbitrary",)),
        name="post_b_sample",
    )(*args)


def _row_gather_copy(src_hbm, idx, dst, dst_row, sem):
    s0 = pl.multiple_of(idx * ROW_CHUNKS, ROW_CHUNKS)
    d0 = pl.multiple_of(dst_row * ROW_CHUNKS, ROW_CHUNKS)
    return pltpu.make_async_copy(src_hbm.at[pl.ds(s0, ROW_CHUNKS), :], dst.at[pl.ds(d0, ROW_CHUNKS), :], sem)


def _dispatch_kernel(nt, pos_ref, h2t_hbm, xs_hbm, sem):
    t = pl.program_id(0)
    slot = t % 2

    def body(r, c):
        src = h2t_hbm.at[pl.ds(pl.multiple_of((t * TM_COMB + r) * ROW_CHUNKS, ROW_CHUNKS), ROW_CHUNKS), :]
        for k in range(2):
            d0 = pl.multiple_of(pos_ref[0, 0, k * TM_COMB + r] * ROW_CHUNKS, ROW_CHUNKS)
            pltpu.make_async_copy(src, xs_hbm.at[pl.ds(d0, ROW_CHUNKS), :], sem.at[slot]).start()
        return c
    lax.fori_loop(0, TM_COMB, body, 0, unroll=8)

    def wait_batch(s):
        n = 2 * TM_COMB * ROW_CHUNKS
        pltpu.make_async_copy(h2t_hbm.at[pl.ds(0, n), :], xs_hbm.at[pl.ds(0, n), :], sem.at[s]).wait()

    @pl.when(t > 0)
    def _():
        wait_batch(1 - slot)

    @pl.when(t == nt - 1)
    def _():
        wait_batch(slot)


def _dispatch(pos3, h2t):
    nt = N_ALL // TM_COMB
    return pl.pallas_call(
        functools.partial(_dispatch_kernel, nt),
        grid=(nt,),
        in_specs=[pl.BlockSpec((1, 1, 2 * TM_COMB), lambda i: (i, 0, 0), memory_space=pltpu.SMEM),
                  pl.BlockSpec(memory_space=pl.ANY)],
        out_specs=pl.BlockSpec(memory_space=pl.ANY),
        out_shape=jax.ShapeDtypeStruct((N_ASSIGN * ROW_CHUNKS, LANES), jnp.float32),
        scratch_shapes=[pltpu.SemaphoreType.DMA((2,))],
        compiler_params=_params(("arbitrary",)),
        name="moe_dispatch",
    )(pos3, h2t)


def _moe_ffn_kernel(it_ref, ie_ref, lo_ref, hi_ref, x_ref, wg_ref, wu_ref, wd_ref, y_ref, wgb, wub, wdb, cur_e):
    i = pl.program_id(0)
    lo = lo_ref[i]
    hi = hi_ref[i]
    e = ie_ref[i]

    @pl.when(i == 0)
    def _():
        cur_e[0] = -1

    @pl.when((hi > lo) & (cur_e[0] != e))
    def _():
        wgb[...] = _bf16(wg_ref[0])
        wub[...] = _bf16(wu_ref[0])
        wdb[...] = _bf16(wd_ref[0])
        cur_e[0] = e

    def ffn():
        x = _bf16(_load_token_tiles(x_ref, 0, TM_MOE))
        hg = _dot(x, wgb[...])
        hu = _dot(x, wub[...])
        h = hg / (1.0 + jnp.exp(-hg)) * hu
        return _dot(_bf16(h), wdb[...])

    def rows_mask():
        row = lax.broadcasted_iota(jnp.int32, (TM_MOE, LANES), 0)
        return (row >= lo) & (row < hi)

    @pl.when((hi > lo) & (lo == 0))
    def _():
        y = ffn()
        mask = rows_mask()
        for c in range(ROW_CHUNKS):
            y_ref[pl.ds(c, TM_MOE, stride=ROW_CHUNKS), :] = jnp.where(mask, y[:, c * LANES:(c + 1) * LANES], 0.0)

    @pl.when((hi > lo) & (lo > 0))
    def _():
        y = ffn()
        mask = rows_mask()
        for c in range(ROW_CHUNKS):
            sl = pl.ds(c, TM_MOE, stride=ROW_CHUNKS)
            y_ref[sl, :] = jnp.where(mask, y[:, c * LANES:(c + 1) * LANES], y_ref[sl, :])


def _moe_ffn(item_tile, item_expert, item_lo, item_hi, x_sorted, w_gate, w_up, w_down):
    wspec = lambda shp: pl.BlockSpec((1,) + shp, lambda i, it, ie, lo, hi: (ie[i], 0, 0))
    tile = pl.BlockSpec((TM_MOE * ROW_CHUNKS, LANES), lambda i, it, ie, lo, hi: (it[i], 0))
    grid_spec = pltpu.PrefetchScalarGridSpec(
        num_scalar_prefetch=4,
        grid=(MOE_ITEMS,),
        in_specs=[tile, wspec((D_MODEL, EXPERT_FF)), wspec((D_MODEL, EXPERT_FF)), wspec((EXPERT_FF, D_MODEL))],
        out_specs=tile,
        scratch_shapes=[pltpu.VMEM((D_MODEL, EXPERT_FF), jnp.bfloat16),
                        pltpu.VMEM((D_MODEL, EXPERT_FF), jnp.bfloat16),
                        pltpu.VMEM((EXPERT_FF, D_MODEL), jnp.bfloat16),
                        pltpu.SMEM((1,), jnp.int32)],
    )
    return pl.pallas_call(
        _moe_ffn_kernel,
        grid_spec=grid_spec,
        out_shape=jax.ShapeDtypeStruct((N_ASSIGN * ROW_CHUNKS, LANES), jnp.float32),
        compiler_params=_params(("arbitrary",)),
        name="moe_ffn",
    )(item_tile, item_expert, item_lo, item_hi, x_sorted, w_gate, w_up, w_down)


def _combine_kernel(nt, pos_cur_ref, pos_nxt_ref, yt_hbm, h2t_ref, rt_ref, g3_ref, b3_ref, o_ref, abuf, sem):
    t = pl.program_id(0)
    slot = t % 2
    rows = 2 * TM_COMB

    def issue(pos_ref, s):
        def body(j, c):
            for k in range(2):
                r = 2 * j + k
                _row_gather_copy(yt_hbm, pos_ref[0, 0, r], abuf, s * rows + r, sem.at[s]).start(priority=k)
            return c
        lax.fori_loop(0, rows // 2, body, 0, unroll=4)

    @pl.when(t == 0)
    def _():
        issue(pos_cur_ref, 0)

    @pl.when(t + 1 < nt)
    def _():
        issue(pos_nxt_ref, 1 - slot)

    base = pl.multiple_of(slot * (rows * ROW_CHUNKS), rows * ROW_CHUNKS)
    pltpu.make_async_copy(yt_hbm.at[pl.ds(0, rows * ROW_CHUNKS), :],
                          abuf.at[pl.ds(base, rows * ROW_CHUNKS), :], sem.at[slot]).wait()
    ya = _load_token_tiles(abuf, base, TM_COMB)
    yb = _load_token_tiles(abuf, base + TM_COMB * ROW_CHUNKS, TM_COMB)
    rt = rt_ref[...]
    ff = rt[:, 2:3] * ya + rt[:, 3:4] * yb
    h2 = _load_token_tiles(h2t_ref, 0, TM_COMB)
    o_ref[...] = _layer_norm(ALPHA * h2 + ff, g3_ref[...], b3_ref[...])


def _combine(pos3, yt, h2t, rt, g3, b3, tile0, n_tiles):
    last = tile0 + n_tiles - 1
    smem_pos = lambda f: pl.BlockSpec((1, 1, 2 * TM_COMB), f, memory_space=pltpu.SMEM)
    full = lambda a: pl.BlockSpec(a.shape, lambda i: (0,) * a.ndim)
    return pl.pallas_call(
        functools.partial(_combine_kernel, n_tiles),
        grid=(n_tiles,),
        in_specs=[smem_pos(lambda i: (tile0 + i, 0, 0)),
                  smem_pos(lambda i: (jnp.minimum(tile0 + i + 1, last), 0, 0)),
                  pl.BlockSpec(memory_space=pl.ANY),
                  pl.BlockSpec((TM_COMB * ROW_CHUNKS, LANES), lambda i: (tile0 + i, 0)),
                  pl.BlockSpec((TM_COMB, LANES), lambda i: (tile0 + i, 0)),
                  full(g3), full(b3)],
        out_specs=pl.BlockSpec((TM_COMB, D_MODEL), lambda i: (i, 0)),
        out_shape=jax.ShapeDtypeStruct((n_tiles * TM_COMB, D_MODEL), jnp.float32),
        scratch_shapes=[pltpu.VMEM((2 * 2 * TM_COMB * ROW_CHUNKS, LANES), jnp.float32),
                        pltpu.SemaphoreType.DMA((2,))],
        compiler_params=_params(("arbitrary",)),
        name="moe_combine",
    )(pos3, pos3, yt, h2t, rt, g3, b3)


def _routing_plan(rt):
    i32 = jnp.int32
    e = jnp.concatenate([rt[:, 0], rt[:, 1]]).astype(i32)
    onehot = (e[:, None] == jnp.arange(N_EXPERTS, dtype=i32)[None, :]).astype(i32)
    csum = jnp.cumsum(onehot, axis=0)
    rank = jnp.take_along_axis(csum, e[:, None], axis=1)[:, 0] - 1
    counts = csum[-1]
    starts = jnp.cumsum(counts) - counts
    pos = starts[e] + rank
    pos3 = jnp.concatenate([pos[:N_ALL].reshape(-1, 1, TM_COMB), pos[N_ALL:].reshape(-1, 1, TM_COMB)], axis=2)
    tiles = jnp.arange(MOE_TILES, dtype=i32) * TM_MOE
    rank_t = jnp.arange(MOE_TILES, dtype=i32) + jnp.sum((starts[None, :] < tiles[:, None]).astype(i32), axis=1)
    rank_s = jnp.arange(N_EXPERTS, dtype=i32) + jnp.sum((tiles[None, :] <= starts[:, None]).astype(i32), axis=1)
    vals = jnp.concatenate([tiles, starts])
    ranks = jnp.concatenate([rank_t, rank_s])
    slot = jnp.arange(MOE_ITEMS, dtype=i32)
    lo = jnp.sum(jnp.where(ranks[None, :] == slot[:, None], vals[None, :], 0), axis=1)
    hi = jnp.concatenate([lo[1:], jnp.full((1,), N_ASSIGN, i32)])
    item_tile = jnp.minimum(lo // TM_MOE, MOE_TILES - 1)
    item_expert = jnp.clip(jnp.sum((starts[None, :] <= lo[:, None]).astype(i32), axis=1) - 1, 0, N_EXPERTS - 1)
    base = item_tile * TM_MOE
    return item_tile, item_expert, lo - base, hi - base, pos3


def kernel(x_prompt, x_sample, mem_prompt, cache_swa_k, cache_swa_v, cache_conv, cache_mem_k, cache_mem_v,
           w_in, sinks, conv_w, w_mix_out, ln1_g, ln1_b, w_q_mem, w_k_mem, w_v_mem, w_o_mem, ln2_g, ln2_b,
           w_router_group, b_router_group, w_router_expert, b_router_expert, w_gate, w_up, w_down,
           ln3_g, ln3_b):
    f32 = jnp.float32
    row = lambda a: a.reshape(1, -1).astype(f32)
    w_in_b, wmix_b, wq_b, wk_b, wv_b, wo_b = (_bf16(w) for w in (w_in, w_mix_out, w_q_mem, w_k_mem, w_v_mem, w_o_mem))
    g1, b1, g2, b2, g3, b3 = (row(a) for a in (ln1_g, ln1_b, ln2_g, ln2_b, ln3_g, ln3_b))
    pad = LANES - N_GROUPS - N_EXPERTS
    wr = jnp.concatenate([w_router_group, w_router_expert, jnp.zeros((D_MODEL, pad), f32)], axis=1)
    wrh = _bf16(wr)
    wrl = _bf16(wr - wrh.astype(f32))
    br = jnp.concatenate([b_router_group, b_router_expert, jnp.zeros((pad,), f32)]).reshape(1, LANES)

    xs = x_sample.reshape(N_SAMPLE, D_MODEL)
    cos_s, sin_s = _rope_tables(PAST_LEN + jnp.arange(DEC_SEQ))
    cos_s, sin_s = jnp.tile(cos_s, (DEC_BATCH, 1)), jnp.tile(sin_s, (DEC_BATCH, 1))
    c0 = jnp.repeat(cache_conv[:, 0], DEC_SEQ, axis=0)
    c1 = jnp.repeat(cache_conv[:, 1], DEC_SEQ, axis=0)
    q_s, k_s, v_s, conv_s, u_s = _proj_sample(xs, w_in_b, cos_s, sin_s, conv_w, c0, c1)
    attn_s, swa_k_s, swa_v_s = _swa_sample(sinks, q_s, k_s, v_s, cache_swa_k, cache_swa_v)
    h1_s, qm_s = _post_a_sample(attn_s, conv_s, xs, wmix_b, g1, b1, wq_b)
    o_s = _mem_attn_sample(qm_s, cache_mem_k, cache_mem_v)
    h2t_s, rt_s = _post_b_sample(o_s, h1_s, wo_b, g2, b2, wrh, wrl, br)

    xp = x_prompt.reshape(N_PROMPT, D_MODEL)
    cos_p, sin_p = _rope_tables(jnp.arange(N_PROMPT))
    q_p, kx_p, vx_p, conv_p, k_tail, v_tail, u_tail = _proj_prompt(xp, w_in_b, cos_p, sin_p, conv_w)
    attn_p = _swa_prompt(sinks, q_p, kx_p, vx_p)
    mk, mv, mk_b, mv_b = _mem_kv(mem_prompt.reshape(N_MEM, D_MODEL), wk_b, wv_b)
    h2t, rt = _post_prompt(attn_p, conv_p, xp, wmix_b, g1, b1, wq_b, mk_b, mv_b, wo_b, g2, b2,
                           wrh, wrl, br, h2t_s, rt_s)

    item_tile, item_expert, item_lo, item_hi, pos3 = _routing_plan(rt)
    x_sorted = _dispatch(pos3, h2t)
    yt = _moe_ffn(item_tile, item_expert, item_lo, item_hi, x_sorted, w_gate, w_up, w_down)
    y_p = _combine(pos3, yt, h2t, rt, g3, b3, 0, N_PROMPT // TM_COMB)
    y_s = _combine(pos3, yt, h2t, rt, g3, b3, N_PROMPT // TM_COMB, N_SAMPLE // TM_COMB)

    return (y_p.reshape(1, SEQ, D_MODEL),
            y_s.reshape(DEC_BATCH, DEC_SEQ, D_MODEL),
            k_tail.reshape(1, WINDOW, N_KV_HEADS, HEAD_DIM),
            v_tail.reshape(1, WINDOW, N_KV_HEADS, HEAD_DIM),
            u_tail[SUBLANES - (CONV_K - 1):].reshape(1, CONV_K - 1, CONV_CH),
            mk.reshape(1, N_MEM, MEM_HEADS, MEM_HEAD_DIM),
            mv.reshape(1, N_MEM, MEM_HEADS, MEM_HEAD_DIM),
            swa_k_s.reshape(DEC_BATCH, WINDOW, N_KV_HEADS, HEAD_DIM),
            swa_v_s.reshape(DEC_BATCH, WINDOW, N_KV_HEADS, HEAD_DIM),
            u_s.reshape(DEC_BATCH, DEC_SEQ, CONV_CH)[:, DEC_SEQ - (CONV_K - 1):])
```

```python
import functools

import jax
import jax.numpy as jnp
from jax import lax
from jax.experimental import pallas as pl
from jax.experimental.pallas import tpu as pltpu

D_MODEL = 1024
SEQ = 16384
DEC_BATCH = 128
DEC_SEQ = 4
PAST_LEN = 16384
ATTN_WIDTH = 512
CONV_CH = 512
HEAD_DIM = 64
N_HEADS = 8
N_KV_HEADS = 2
KV_WIDTH = 128
WINDOW = 128
ROPE_THETA = 500000.0
ROPE_DIM = 16
CONV_K = 3
Q_END = ATTN_WIDTH
K_END = Q_END + KV_WIDTH
V_END = K_END + KV_WIDTH
B_END = V_END + CONV_CH
C_END = B_END + CONV_CH
IN_WIDTH = C_END + CONV_CH
N_MEM = 256
MEM_HEADS = 4
MEM_HEAD_DIM = 256
N_GROUPS = 4
EXPERTS_PER_GROUP = 8
N_EXPERTS = 32
EXPERT_FF = 256
ALPHA = 2.0 ** 0.25
LN_EPS = 1e-5

LANES = 128
SUBLANES = 8
ROW_CHUNKS = D_MODEL // LANES
VMEM_LIMIT = 56 * 1024 * 1024

N_PROMPT = SEQ
N_SAMPLE = DEC_BATCH * DEC_SEQ
N_ALL = N_PROMPT + N_SAMPLE
TM_PROJ = 512
TM_POST = 512
TM_MOE = 512
TM_COMB = 256
N_ASSIGN = 2 * N_ALL
MOE_TILES = N_ASSIGN // TM_MOE
MOE_ITEMS = MOE_TILES + N_EXPERTS
SAMPLE_BB = 4

assert ROW_CHUNKS == SUBLANES
assert N_SAMPLE == TM_POST
assert N_ASSIGN % TM_MOE == 0 and N_ALL % TM_COMB == 0


def _params(sem, vmem=VMEM_LIMIT):
    return pltpu.CompilerParams(dimension_semantics=sem, vmem_limit_bytes=vmem)


def _bf16(x):
    return x.astype(jnp.bfloat16)


def _dot(a, b):
    return jnp.dot(a, b, preferred_element_type=jnp.float32)


def _dot_nt(a, b):
    return lax.dot_general(a, b, (((1,), (1,)), ((), ())), preferred_element_type=jnp.float32)


def _layer_norm(x, g, b):
    mu = jnp.mean(x, axis=-1, keepdims=True)
    xc = x - mu
    var = jnp.mean(xc * xc, axis=-1, keepdims=True)
    return xc * lax.rsqrt(var + LN_EPS) * g + b


def _rope(x, cos_t, sin_t):
    lane = lax.broadcasted_iota(jnp.int32, x.shape, 1) % HEAD_DIM
    half = ROPE_DIM // 2
    partner = jnp.where(lane < half, pltpu.roll(x, LANES - half, axis=1), pltpu.roll(x, half, axis=1))
    return x * cos_t + partner * sin_t


def _head_slabs(x):
    lane = lax.broadcasted_iota(jnp.int32, x.shape, 1)
    lo = lane < HEAD_DIM
    sw = pltpu.roll(x, HEAD_DIM, axis=1)
    zero = jnp.zeros_like(x)
    slabs = [jnp.where(lo, x, zero), jnp.where(lo, zero, sw), jnp.where(lo, sw, zero), jnp.where(lo, zero, x)]
    return _bf16(jnp.concatenate(slabs, axis=1))


def _store_token_tiles(ref, val):
    rows = val.shape[0]
    for c in range(ROW_CHUNKS):
        ref[pl.ds(c, rows, stride=ROW_CHUNKS), :] = val[:, c * LANES:(c + 1) * LANES]


def _load_token_tiles(ref, base, rows):
    return jnp.concatenate(
        [ref[pl.ds(base + c, rows, stride=ROW_CHUNKS), :] for c in range(ROW_CHUNKS)], axis=1)


def _proj_common(x_ref, w_ref, cos_ref, sin_ref):
    xb = _bf16(x_ref[...])
    cos_t = cos_ref[...]
    sin_t = sin_ref[...]
    q = _dot(xb, w_ref[:, 0:Q_END])
    q_rot = jnp.concatenate(
        [_rope(q[:, p * LANES:(p + 1) * LANES], cos_t, sin_t) for p in range(ATTN_WIDTH // LANES)], axis=1)
    q_out = _bf16(q_rot * (HEAD_DIM ** -0.5))
    k = _rope(_dot(xb, w_ref[:, Q_END:K_END]), cos_t, sin_t)
    v = _dot(xb, w_ref[:, K_END:V_END])
    bg = _dot(xb, w_ref[:, V_END:B_END])
    u = _dot(xb, w_ref[:, B_END:C_END]) * _dot(xb, w_ref[:, C_END:IN_WIDTH])
    return q_out, k, v, bg, u


def _conv3(bg, u, u1, u2, cw_ref):
    cw = cw_ref[...]
    return bg * (cw[0:1, :] * u2 + cw[1:2, :] * u1 + cw[2:3, :] * u)


def _proj_prompt_kernel(x_ref, w_ref, cos_ref, sin_ref, cw_ref,
                        q_ref, kx_ref, vx_ref, conv_ref, ktail_ref, vtail_ref, utail_ref, carry_ref):
    @pl.when(pl.program_id(0) == 0)
    def _():
        carry_ref[...] = jnp.zeros_like(carry_ref)

    q_out, k, v, bg, u = _proj_common(x_ref, w_ref, cos_ref, sin_ref)
    tm = u.shape[0]
    ext = jnp.concatenate([carry_ref[...], u], axis=0)
    u1 = pltpu.roll(ext, 1, axis=0)[SUBLANES:SUBLANES + tm]
    u2 = pltpu.roll(ext, 2, axis=0)[SUBLANES:SUBLANES + tm]
    q_ref[...] = q_out
    kx_ref[...] = _head_slabs(k)
    vx_ref[...] = _head_slabs(v)
    conv_ref[...] = _bf16(_conv3(bg, u, u1, u2, cw_ref))
    ktail_ref[...] = k[tm - WINDOW:tm]
    vtail_ref[...] = v[tm - WINDOW:tm]
    utail_ref[...] = u[tm - SUBLANES:tm]
    carry_ref[...] = u[tm - SUBLANES:tm]


def _proj_sample_kernel(x_ref, w_ref, cos_ref, sin_ref, cw_ref, c0_ref, c1_ref,
                        q_ref, k_ref, v_ref, conv_ref, u_ref):
    q_out, k, v, bg, u = _proj_common(x_ref, w_ref, cos_ref, sin_ref)
    t = lax.broadcasted_iota(jnp.int32, u.shape, 0) % DEC_SEQ
    c0 = c0_ref[...]
    c1 = c1_ref[...]
    u1 = jnp.where(t >= 1, pltpu.roll(u, 1, axis=0), c1)
    u2 = jnp.where(t >= 2, pltpu.roll(u, 2, axis=0), jnp.where(t == 1, c1, c0))
    q_ref[...] = q_out.astype(jnp.float32)
    k_ref[...] = k
    v_ref[...] = v
    conv_ref[...] = _bf16(_conv3(bg, u, u1, u2, cw_ref))
    u_ref[...] = u


def _rope_tables(pos):
    half = ROPE_DIM // 2
    inv = ROPE_THETA ** (-jnp.arange(0, ROPE_DIM, 2, dtype=jnp.float32) / ROPE_DIM)
    ang = pos.astype(jnp.float32)[:, None] * inv
    cos, sin = jnp.cos(ang), jnp.sin(ang)
    n = pos.shape[0]
    pad = HEAD_DIM - ROPE_DIM
    cos_h = jnp.concatenate([cos, cos, jnp.ones((n, pad), jnp.float32)], axis=1)
    sin_h = jnp.concatenate([-sin, sin, jnp.zeros((n, pad), jnp.float32)], axis=1)
    reps = LANES // HEAD_DIM
    assert half * 2 == ROPE_DIM
    return jnp.tile(cos_h, (1, reps)), jnp.tile(sin_h, (1, reps))


def _proj_prompt(x, w_in_b, cos_t, sin_t, conv_w):
    n = x.shape[0]
    tm = TM_PROJ
    row = lambda w: pl.BlockSpec((tm, w), lambda i: (i, 0))
    full = lambda a: pl.BlockSpec(a.shape, lambda i: (0,) * a.ndim)
    const = lambda r, w: pl.BlockSpec((r, w), lambda i: (0, 0))
    return pl.pallas_call(
        _proj_prompt_kernel,
        grid=(n // tm,),
        in_specs=[row(D_MODEL), full(w_in_b), row(LANES), row(LANES), full(conv_w)],
        out_specs=[row(ATTN_WIDTH), row(4 * LANES), row(4 * LANES), row(CONV_CH),
                   const(WINDOW, KV_WIDTH), const(WINDOW, KV_WIDTH), const(SUBLANES, CONV_CH)],
        out_shape=[jax.ShapeDtypeStruct((n, ATTN_WIDTH), jnp.bfloat16),
                   jax.ShapeDtypeStruct((n, 4 * LANES), jnp.bfloat16),
                   jax.ShapeDtypeStruct((n, 4 * LANES), jnp.bfloat16),
                   jax.ShapeDtypeStruct((n, CONV_CH), jnp.bfloat16),
                   jax.ShapeDtypeStruct((WINDOW, KV_WIDTH), jnp.float32),
                   jax.ShapeDtypeStruct((WINDOW, KV_WIDTH), jnp.float32),
                   jax.ShapeDtypeStruct((SUBLANES, CONV_CH), jnp.float32)],
        scratch_shapes=[pltpu.VMEM((SUBLANES, CONV_CH), jnp.float32)],
        compiler_params=_params(("arbitrary",)),
        name="proj_prompt",
    )(x, w_in_b, cos_t, sin_t, conv_w)


def _proj_sample(x, w_in_b, cos_t, sin_t, conv_w, c0, c1):
    n = x.shape[0]
    full = lambda a: pl.BlockSpec(a.shape, lambda i: (0,) * a.ndim)
    out = lambda w, dt: jax.ShapeDtypeStruct((n, w), dt)
    blk = lambda w: pl.BlockSpec((n, w), lambda i: (0, 0))
    return pl.pallas_call(
        _proj_sample_kernel,
        grid=(1,),
        in_specs=[full(x), full(w_in_b), full(cos_t), full(sin_t), full(conv_w), full(c0), full(c1)],
        out_specs=[blk(ATTN_WIDTH), blk(KV_WIDTH), blk(KV_WIDTH), blk(CONV_CH), blk(CONV_CH)],
        out_shape=[out(ATTN_WIDTH, jnp.float32), out(KV_WIDTH, jnp.float32), out(KV_WIDTH, jnp.float32),
                   out(CONV_CH, jnp.bfloat16), out(CONV_CH, jnp.float32)],
        compiler_params=_params(("arbitrary",)),
        name="proj_sample",
    )(x, w_in_b, cos_t, sin_t, conv_w, c0, c1)


def _sink_softmax_pv(s, valid, sink, vx):
    s = jnp.where(valid, s, -jnp.inf)
    m = jnp.maximum(jnp.max(s, axis=1, keepdims=True), sink)
    p = jnp.exp(s - m)
    den = jnp.sum(p, axis=1, keepdims=True) + jnp.exp(sink - m)
    return _dot(_bf16(p), vx) / den


def _swa_prompt_kernel(sinks_ref, q_ref, kc_ref, kp_ref, vc_ref, vp_ref, o_ref):
    blk = pl.program_id(0)
    kcat = jnp.concatenate([kp_ref[...], kc_ref[...]], axis=0)
    vcat = jnp.concatenate([vp_ref[...], vc_ref[...]], axis=0)
    i = lax.broadcasted_iota(jnp.int32, (WINDOW, 2 * WINDOW), 0)
    j = lax.broadcasted_iota(jnp.int32, (WINDOW, 2 * WINDOW), 1)
    valid = (j > i) & (j <= i + WINDOW) & ((blk > 0) | (j >= WINDOW))
    for p in range(N_HEADS // 2):
        qs = q_ref[:, p * LANES:(p + 1) * LANES]
        acc = None
        for e in range(2):
            hd = 2 * p + e
            slab = 2 * (hd // (N_HEADS // N_KV_HEADS)) + e
            kx = kcat[:, slab * LANES:(slab + 1) * LANES]
            vx = vcat[:, slab * LANES:(slab + 1) * LANES]
            o = _sink_softmax_pv(_dot_nt(qs, kx), valid, sinks_ref[hd], vx)
            acc = o if acc is None else acc + o
        o_ref[:, p * LANES:(p + 1) * LANES] = _bf16(acc)


def _swa_prompt(sinks, q, kx, vx):
    n = q.shape[0]
    nb = n // WINDOW
    cur = lambda w: pl.BlockSpec((WINDOW, w), lambda i: (i, 0))
    prev = lambda w: pl.BlockSpec((WINDOW, w), lambda i: (jnp.maximum(i - 1, 0), 0))
    return pl.pallas_call(
        _swa_prompt_kernel,
        grid=(nb,),
        in_specs=[pl.BlockSpec(memory_space=pltpu.SMEM), cur(ATTN_WIDTH),
                  cur(4 * LANES), prev(4 * LANES), cur(4 * LANES), prev(4 * LANES)],
        out_specs=cur(ATTN_WIDTH),
        out_shape=jax.ShapeDtypeStruct((n, ATTN_WIDTH), jnp.bfloat16),
        compiler_params=_params(("arbitrary",)),
        name="swa_prompt",
    )(sinks, q, kx, kx, vx, vx)


SWA_ROWS = N_HEADS * DEC_SEQ
NEW_ROWS = 2 * SUBLANES


def _swa_sample_kernel(q_ref, sink_ref, kn_ref, vn_ref, kt_ref, vt_ref, o_ref, okt_ref, ovt_ref):
    t = lax.broadcasted_iota(jnp.int32, (SWA_ROWS, WINDOW), 0) % DEC_SEQ
    j = lax.broadcasted_iota(jnp.int32, (SWA_ROWS, WINDOW), 1)
    valid_c = j > t
    valid_n = (lax.broadcasted_iota(jnp.int32, (SWA_ROWS, NEW_ROWS), 1)
               <= lax.broadcasted_iota(jnp.int32, (SWA_ROWS, NEW_ROWS), 0) % DEC_SEQ)
    lane = lax.broadcasted_iota(jnp.int32, (KV_WIDTH, WINDOW), 1)
    sink = sink_ref[:, 0:1]
    shift = WINDOW - DEC_SEQ
    zrows = jnp.zeros((KV_WIDTH - NEW_ROWS, KV_WIDTH), jnp.float32)
    for b in range(SAMPLE_BB):
        q = _bf16(q_ref[b])
        kt, vt = kt_ref[b], vt_ref[b]
        kn, vn = kn_ref[b], vn_ref[b]
        s_c = jnp.where(valid_c, _dot(q, _bf16(kt)), -jnp.inf)
        s_n = jnp.where(valid_n, _dot_nt(q, _bf16(kn)), -jnp.inf)
        m = jnp.maximum(jnp.maximum(jnp.max(s_c, axis=1, keepdims=True), jnp.max(s_n, axis=1, keepdims=True)), sink)
        p_c = jnp.exp(s_c - m)
        p_n = jnp.exp(s_n - m)
        den = jnp.sum(p_c, axis=1, keepdims=True) + jnp.sum(p_n, axis=1, keepdims=True) + jnp.exp(sink - m)
        o_ref[b] = (_dot_nt(_bf16(p_c), _bf16(vt)) + _dot(_bf16(p_n), _bf16(vn))) / den
        for old, new, dst in ((kt, kn, okt_ref), (vt, vn, ovt_ref)):
            new_cols = pltpu.roll(jnp.concatenate([new, zrows], axis=0).T, shift, axis=1)
            dst[b] = jnp.where(lane >= shift, new_cols, pltpu.roll(old, shift, axis=1))


def _swa_sample(sinks, q, kn, vn, cache_k, cache_v):
    nb = cache_k.shape[0]
    bb = SAMPLE_BB
    groups = N_HEADS // N_KV_HEADS
    qh = q.reshape(nb, DEC_SEQ, N_KV_HEADS, groups, HEAD_DIM).transpose(0, 2, 3, 1, 4)
    qh = qh.reshape(nb, N_KV_HEADS, groups * DEC_SEQ, HEAD_DIM)
    zeros = jnp.zeros_like(qh[:, 0])
    qbd = jnp.concatenate([jnp.concatenate([qh[:, 0], zeros], axis=-1),
                           jnp.concatenate([zeros, qh[:, 1]], axis=-1)], axis=1)
    sink_col = jnp.broadcast_to(jnp.repeat(sinks, DEC_SEQ).reshape(SWA_ROWS, 1), (SWA_ROWS, LANES))
    pad8 = lambda a: jnp.pad(a.reshape(nb, DEC_SEQ, KV_WIDTH), ((0, 0), (0, NEW_ROWS - DEC_SEQ), (0, 0)))
    to_t = lambda c: c.transpose(0, 2, 3, 1).reshape(nb, KV_WIDTH, WINDOW)
    blk = lambda r, w: pl.BlockSpec((bb, r, w), lambda i: (i, 0, 0))
    o, okt, ovt = pl.pallas_call(
        _swa_sample_kernel,
        grid=(nb // bb,),
        in_specs=[blk(SWA_ROWS, KV_WIDTH), pl.BlockSpec((SWA_ROWS, LANES), lambda i: (0, 0)),
                  blk(NEW_ROWS, KV_WIDTH), blk(NEW_ROWS, KV_WIDTH), blk(KV_WIDTH, WINDOW), blk(KV_WIDTH, WINDOW)],
        out_specs=[blk(SWA_ROWS, KV_WIDTH), blk(KV_WIDTH, WINDOW), blk(KV_WIDTH, WINDOW)],
        out_shape=[jax.ShapeDtypeStruct((nb, SWA_ROWS, KV_WIDTH), jnp.float32),
                   jax.ShapeDtypeStruct((nb, KV_WIDTH, WINDOW), jnp.float32),
                   jax.ShapeDtypeStruct((nb, KV_WIDTH, WINDOW), jnp.float32)],
        compiler_params=_params(("arbitrary",)),
        name="swa_sample",
    )(qbd, sink_col, pad8(kn), pad8(vn), to_t(cache_k), to_t(cache_v))
    o = o.reshape(nb, N_KV_HEADS, groups, DEC_SEQ, N_KV_HEADS, HEAD_DIM)
    attn = jnp.stack([o[:, h, :, :, h, :] for h in range(N_KV_HEADS)], axis=1)
    attn = attn.transpose(0, 3, 1, 2, 4).reshape(nb * DEC_SEQ, ATTN_WIDTH)
    from_t = lambda c: c.reshape(nb, N_KV_HEADS, HEAD_DIM, WINDOW).transpose(0, 3, 1, 2)
    return attn, from_t(okt), from_t(ovt)


def _mem_kv_kernel(mem_ref, wk_ref, wv_ref, mk_ref, mv_ref, mkb_ref, mvb_ref):
    mb = _bf16(mem_ref[...])
    mk = _dot(mb, wk_ref[...])
    mv = _dot(mb, wv_ref[...])
    mk_ref[...] = mk
    mv_ref[...] = mv
    mkb_ref[...] = _bf16(mk)
    mvb_ref[...] = _bf16(mv)


def _mem_kv(mem, wk_b, wv_b):
    full = lambda a: pl.BlockSpec(a.shape, lambda i: (0,) * a.ndim)
    blk = pl.BlockSpec((N_MEM, D_MODEL), lambda i: (0, 0))
    f32 = jax.ShapeDtypeStruct((N_MEM, D_MODEL), jnp.float32)
    b16 = jax.ShapeDtypeStruct((N_MEM, D_MODEL), jnp.bfloat16)
    return pl.pallas_call(
        _mem_kv_kernel,
        grid=(1,),
        in_specs=[full(mem), full(wk_b), full(wv_b)],
        out_specs=[blk, blk, blk, blk],
        out_shape=[f32, f32, b16, b16],
        compiler_params=_params(("arbitrary",)),
        name="mem_kv",
    )(mem, wk_b, wv_b)


def _mix_ln1(attn_ref, conv_ref, x_ref, wmix_ref, g1_ref, b1_ref):
    mix = _dot(_bf16(attn_ref[...]), wmix_ref[0:ATTN_WIDTH, :]) + _dot(conv_ref[...], wmix_ref[ATTN_WIDTH:, :])
    return _layer_norm(ALPHA * x_ref[...] + mix, g1_ref[...], b1_ref[...])


def _mem_q(h1, wq_ref):
    return _bf16(_dot(_bf16(h1), wq_ref[...]) * (MEM_HEAD_DIM ** -0.5))


def _route(h2, wrh_ref, wrl_ref, br_ref):
    hi = _bf16(h2)
    lo = _bf16(h2 - hi.astype(jnp.float32))
    logits = _dot(hi, wrh_ref[...]) + _dot(lo, wrh_ref[...]) + _dot(hi, wrl_ref[...]) + br_ref[...]
    lane_i = lax.broadcasted_iota(jnp.int32, logits.shape, 1)
    lane = lane_i.astype(jnp.float32)
    big = jnp.float32(LANES)
    is_g = lane_i < N_GROUPS
    gl = jnp.where(is_g, logits, -jnp.inf)
    gmax = jnp.max(gl, axis=1, keepdims=True)
    gidx = jnp.min(jnp.where(is_g & (logits == gmax), lane, big), axis=1, keepdims=True)
    gsum = jnp.sum(jnp.exp(gl - gmax), axis=1, keepdims=True)
    gw = 1.0 / gsum
    eid = lane_i - N_GROUPS
    assert EXPERTS_PER_GROUP == 8
    grp = lax.shift_right_arithmetic(eid, jnp.full_like(eid, 3)).astype(jnp.float32)
    in_e = (lane_i >= N_GROUPS) & (lane_i < N_GROUPS + N_EXPERTS) & (grp == gidx)
    v1 = jnp.max(jnp.where(in_e, logits, -jnp.inf), axis=1, keepdims=True)
    i1 = jnp.min(jnp.where(in_e & (logits == v1), lane, big), axis=1, keepdims=True)
    rest = in_e & (lane != i1)
    v2 = jnp.max(jnp.where(rest, logits, -jnp.inf), axis=1, keepdims=True)
    i2 = jnp.min(jnp.where(rest & (logits == v2), lane, big), axis=1, keepdims=True)
    ex = jnp.exp(v2 - v1)
    den = 1.0 + ex
    w1 = gw / den
    w2 = gw * ex / den
    zero = jnp.zeros_like(logits)
    return jnp.where(lane_i == 0, i1 - N_GROUPS,
                     jnp.where(lane_i == 1, i2 - N_GROUPS,
                               jnp.where(lane_i == 2, w1, jnp.where(lane_i == 3, w2, zero))))


def _post_prompt_kernel(attn_ref, conv_ref, x_ref, wmix_ref, g1_ref, b1_ref, wq_ref, mk_ref, mv_ref, wo_ref,
                        g2_ref, b2_ref, wrh_ref, wrl_ref, br_ref, h2s_ref, rts_ref, h2t_ref, rt_ref):
    is_tail = pl.program_id(0) == N_PROMPT // TM_POST

    @pl.when(is_tail)
    def _():
        h2t_ref[...] = h2s_ref[...]
        rt_ref[...] = rts_ref[...]

    @pl.when(jnp.logical_not(is_tail))
    def _():
        h1 = _mix_ln1(attn_ref, conv_ref, x_ref, wmix_ref, g1_ref, b1_ref)
        qm = _mem_q(h1, wq_ref)
        outs = []
        for h in range(MEM_HEADS):
            sl = slice(h * MEM_HEAD_DIM, (h + 1) * MEM_HEAD_DIM)
            s = _dot_nt(qm[:, sl], mk_ref[:, sl])
            m = jnp.max(s, axis=1, keepdims=True)
            p = jnp.exp(s - m)
            den = jnp.sum(p, axis=1, keepdims=True)
            outs.append(_dot(_bf16(p), mv_ref[:, sl]) / den)
        o = _bf16(jnp.concatenate(outs, axis=1))
        h2 = _layer_norm(ALPHA * h1 + _dot(o, wo_ref[...]), g2_ref[...], b2_ref[...])
        _store_token_tiles(h2t_ref, h2)
        rt_ref[...] = _route(h2, wrh_ref, wrl_ref, br_ref)


def _post_prompt(attn, conv, x, wmix_b, g1, b1, wq_b, mk_b, mv_b, wo_b, g2, b2, wrh, wrl, br, h2t_s, rt_s):
    n = x.shape[0]
    tm = TM_POST
    steps = n // tm
    row = lambda w: pl.BlockSpec((tm, w), lambda i: (jnp.minimum(i, steps - 1), 0))
    full = lambda a: pl.BlockSpec(a.shape, lambda i: (0,) * a.ndim)
    weights = (wmix_b, g1, b1, wq_b, mk_b, mv_b, wo_b, g2, b2, wrh, wrl, br, h2t_s, rt_s)
    n_out = n + h2t_s.shape[0] // ROW_CHUNKS
    return pl.pallas_call(
        _post_prompt_kernel,
        grid=(steps + 1,),
        in_specs=[row(ATTN_WIDTH), row(CONV_CH), row(D_MODEL)] + [full(a) for a in weights],
        out_specs=[pl.BlockSpec((tm * ROW_CHUNKS, LANES), lambda i: (i, 0)),
                   pl.BlockSpec((tm, LANES), lambda i: (i, 0))],
        out_shape=[jax.ShapeDtypeStruct((n_out * ROW_CHUNKS, LANES), jnp.float32),
                   jax.ShapeDtypeStruct((n_out, LANES), jnp.float32)],
        compiler_params=_params(("arbitrary",)),
        name="post_prompt",
    )(attn, conv, x, *weights)


def _post_a_sample_kernel(attn_ref, conv_ref, x_ref, wmix_ref, g1_ref, b1_ref, wq_ref, h1_ref, qm_ref):
    h1 = _mix_ln1(attn_ref, conv_ref, x_ref, wmix_ref, g1_ref, b1_ref)
    h1_ref[...] = h1
    qm_ref[...] = _mem_q(h1, wq_ref).astype(jnp.float32)


def _post_a_sample(attn, conv, x, wmix_b, g1, b1, wq_b):
    n = x.shape[0]
    args = (attn, conv, x, wmix_b, g1, b1, wq_b)
    full = lambda a: pl.BlockSpec(a.shape, lambda i: (0,) * a.ndim)
    blk = pl.BlockSpec((n, D_MODEL), lambda i: (0, 0))
    return pl.pallas_call(
        _post_a_sample_kernel,
        grid=(1,),
        in_specs=[full(a) for a in args],
        out_specs=[blk, blk],
        out_shape=[jax.ShapeDtypeStruct((n, D_MODEL), jnp.float32),
                   jax.ShapeDtypeStruct((n, D_MODEL), jnp.float32)],
        compiler_params=_params(("arbitrary",)),
        name="post_a_sample",
    )(*args)


MEM_ROWS = MEM_HEADS * DEC_SEQ


def _mem_attn_sample_kernel(q_ref, mk_ref, mv_ref, o_ref):
    nk = N_MEM * MEM_HEADS
    row_h = lax.broadcasted_iota(jnp.int32, (MEM_ROWS, nk), 0) // DEC_SEQ
    key_h = lax.broadcasted_iota(jnp.int32, (MEM_ROWS, nk), 1) % MEM_HEADS
    own = row_h == key_h
    for b in range(SAMPLE_BB):
        k2 = _bf16(mk_ref[b].reshape(nk, MEM_HEAD_DIM))
        v2 = _bf16(mv_ref[b].reshape(nk, MEM_HEAD_DIM))
        s = jnp.where(own, _dot_nt(_bf16(q_ref[b]), k2), -jnp.inf)
        m = jnp.max(s, axis=1, keepdims=True)
        p = jnp.exp(s - m)
        den = jnp.sum(p, axis=1, keepdims=True)
        o_ref[b] = _dot(_bf16(p), v2) / den


def _mem_attn_sample(qm, mk, mv):
    nb = mk.shape[0]
    bb = SAMPLE_BB
    q = qm.reshape(nb, DEC_SEQ, MEM_HEADS, MEM_HEAD_DIM).transpose(0, 2, 1, 3).reshape(nb, MEM_ROWS, MEM_HEAD_DIM)
    rows = pl.BlockSpec((bb, MEM_ROWS, MEM_HEAD_DIM), lambda i: (i, 0, 0))
    kv = pl.BlockSpec((bb, N_MEM, MEM_HEADS, MEM_HEAD_DIM), lambda i: (i, 0, 0, 0))
    o = pl.pallas_call(
        _mem_attn_sample_kernel,
        grid=(nb // bb,),
        in_specs=[rows, kv, kv],
        out_specs=rows,
        out_shape=jax.ShapeDtypeStruct((nb, MEM_ROWS, MEM_HEAD_DIM), jnp.float32),
        compiler_params=_params(("arbitrary",)),
        name="mem_attn_sample",
    )(q, mk, mv)
    return o.reshape(nb, MEM_HEADS, DEC_SEQ, MEM_HEAD_DIM).transpose(0, 2, 1, 3).reshape(nb * DEC_SEQ, D_MODEL)


def _post_b_sample_kernel(o_ref, h1_ref, wo_ref, g2_ref, b2_ref, wrh_ref, wrl_ref, br_ref, h2t_ref, rt_ref):
    h2 = _layer_norm(ALPHA * h1_ref[...] + _dot(_bf16(o_ref[...]), wo_ref[...]), g2_ref[...], b2_ref[...])
    _store_token_tiles(h2t_ref, h2)
    rt_ref[...] = _route(h2, wrh_ref, wrl_ref, br_ref)


def _post_b_sample(o, h1, wo_b, g2, b2, wrh, wrl, br):
    n = h1.shape[0]
    args = (o, h1, wo_b, g2, b2, wrh, wrl, br)
    full = lambda a: pl.BlockSpec(a.shape, lambda i: (0,) * a.ndim)
    return pl.pallas_call(
        _post_b_sample_kernel,
        grid=(1,),
        in_specs=[full(a) for a in args],
        out_specs=[pl.BlockSpec((n * ROW_CHUNKS, LANES), lambda i: (0, 0)),
                   pl.BlockSpec((n, LANES), lambda i: (0, 0))],
        out_shape=[jax.ShapeDtypeStruct((n * ROW_CHUNKS, LANES), jnp.float32),
                   jax.ShapeDtypeStruct((n, LANES), jnp.float32)],
        compiler_params=_params(("arbitrary",)),
        name="post_b_sample",
    )(*args)


def _row_gather_copy(src_hbm, idx, dst, dst_row, sem):
    s0 = pl.multiple_of(idx * ROW_CHUNKS, ROW_CHUNKS)
    d0 = pl.multiple_of(dst_row * ROW_CHUNKS, ROW_CHUNKS)
    return pltpu.make_async_copy(src_hbm.at[pl.ds(s0, ROW_CHUNKS), :], dst.at[pl.ds(d0, ROW_CHUNKS), :], sem)


def _dispatch_kernel(pos_ref, h2t_ref, xs_hbm, sem):
    def body(r, c):
        src = h2t_ref.at[pl.ds(pl.multiple_of(r * ROW_CHUNKS, ROW_CHUNKS), ROW_CHUNKS), :]
        for k in range(2):
            d0 = pl.multiple_of(pos_ref[0, 0, k * TM_COMB + r] * ROW_CHUNKS, ROW_CHUNKS)
            pltpu.make_async_copy(src, xs_hbm.at[pl.ds(d0, ROW_CHUNKS), :], sem.at[0]).start(priority=k)
        return c
    lax.fori_loop(0, TM_COMB, body, 0, unroll=8)
    for _ in range(2):
        pltpu.make_async_copy(h2t_ref, xs_hbm.at[pl.ds(0, TM_COMB * ROW_CHUNKS), :], sem.at[0]).wait()


def _dispatch(pos3, h2t):
    nt = N_ALL // TM_COMB
    return pl.pallas_call(
        _dispatch_kernel,
        grid=(nt,),
        in_specs=[pl.BlockSpec((1, 1, 2 * TM_COMB), lambda i: (i, 0, 0), memory_space=pltpu.SMEM),
                  pl.BlockSpec((TM_COMB * ROW_CHUNKS, LANES), lambda i: (i, 0))],
        out_specs=pl.BlockSpec(memory_space=pl.ANY),
        out_shape=jax.ShapeDtypeStruct((N_ASSIGN * ROW_CHUNKS, LANES), jnp.float32),
        scratch_shapes=[pltpu.SemaphoreType.DMA((1,))],
        compiler_params=_params(("arbitrary",)),
        name="moe_dispatch",
    )(pos3, h2t)


def _moe_ffn_kernel(it_ref, ie_ref, lo_ref, hi_ref, x_ref, wg_ref, wu_ref, wd_ref, y_ref, wgb, wub, wdb, cur_e):
    i = pl.program_id(0)
    lo = lo_ref[i]
    hi = hi_ref[i]
    e = ie_ref[i]

    @pl.when(i == 0)
    def _():
        cur_e[0] = -1

    @pl.when((hi > lo) & (cur_e[0] != e))
    def _():
        wgb[...] = _bf16(wg_ref[0])
        wub[...] = _bf16(wu_ref[0])
        wdb[...] = _bf16(wd_ref[0])
        cur_e[0] = e

    def ffn():
        x = _bf16(_load_token_tiles(x_ref, 0, TM_MOE))
        hg = _dot(x, wgb[...])
        hu = _dot(x, wub[...])
        h = hg / (1.0 + jnp.exp(-hg)) * hu
        return _dot(_bf16(h), wdb[...])

    def rows_mask():
        row = lax.broadcasted_iota(jnp.int32, (TM_MOE, LANES), 0)
        return (row >= lo) & (row < hi)

    @pl.when((hi > lo) & (lo == 0))
    def _():
        y = ffn()
        mask = rows_mask()
        for c in range(ROW_CHUNKS):
            y_ref[pl.ds(c, TM_MOE, stride=ROW_CHUNKS), :] = jnp.where(mask, y[:, c * LANES:(c + 1) * LANES], 0.0)

    @pl.when((hi > lo) & (lo > 0))
    def _():
        y = ffn()
        mask = rows_mask()
        for c in range(ROW_CHUNKS):
            sl = pl.ds(c, TM_MOE, stride=ROW_CHUNKS)
            y_ref[sl, :] = jnp.where(mask, y[:, c * LANES:(c + 1) * LANES], y_ref[sl, :])


def _moe_ffn(item_tile, item_expert, item_lo, item_hi, x_sorted, w_gate, w_up, w_down):
    wspec = lambda shp: pl.BlockSpec((1,) + shp, lambda i, it, ie, lo, hi: (ie[i], 0, 0))
    tile = pl.BlockSpec((TM_MOE * ROW_CHUNKS, LANES), lambda i, it, ie, lo, hi: (it[i], 0))
    grid_spec = pltpu.PrefetchScalarGridSpec(
        num_scalar_prefetch=4,
        grid=(MOE_ITEMS,),
        in_specs=[tile, wspec((D_MODEL, EXPERT_FF)), wspec((D_MODEL, EXPERT_FF)), wspec((EXPERT_FF, D_MODEL))],
        out_specs=tile,
        scratch_shapes=[pltpu.VMEM((D_MODEL, EXPERT_FF), jnp.bfloat16),
                        pltpu.VMEM((D_MODEL, EXPERT_FF), jnp.bfloat16),
                        pltpu.VMEM((EXPERT_FF, D_MODEL), jnp.bfloat16),
                        pltpu.SMEM((1,), jnp.int32)],
    )
    return pl.pallas_call(
        _moe_ffn_kernel,
        grid_spec=grid_spec,
        out_shape=jax.ShapeDtypeStruct((N_ASSIGN * ROW_CHUNKS, LANES), jnp.float32),
        compiler_params=_params(("arbitrary",)),
        name="moe_ffn",
    )(item_tile, item_expert, item_lo, item_hi, x_sorted, w_gate, w_up, w_down)


def _combine_kernel(nt, pos_cur_ref, pos_nxt_ref, yt_hbm, h2t_ref, rt_ref, g3_ref, b3_ref, o_ref, abuf, sem):
    t = pl.program_id(0)
    slot = t % 2
    rows = 2 * TM_COMB

    def issue(pos_ref, s):
        def body(j, c):
            for k in range(2):
                r = 2 * j + k
                _row_gather_copy(yt_hbm, pos_ref[0, 0, r], abuf, s * rows + r, sem.at[s]).start(priority=k)
            return c
        lax.fori_loop(0, rows // 2, body, 0, unroll=4)

    @pl.when(t == 0)
    def _():
        issue(pos_cur_ref, 0)

    @pl.when(t + 1 < nt)
    def _():
        issue(pos_nxt_ref, 1 - slot)

    base = pl.multiple_of(slot * (rows * ROW_CHUNKS), rows * ROW_CHUNKS)
    pltpu.make_async_copy(yt_hbm.at[pl.ds(0, rows * ROW_CHUNKS), :],
                          abuf.at[pl.ds(base, rows * ROW_CHUNKS), :], sem.at[slot]).wait()
    ya = _load_token_tiles(abuf, base, TM_COMB)
    yb = _load_token_tiles(abuf, base + TM_COMB * ROW_CHUNKS, TM_COMB)
    rt = rt_ref[...]
    ff = rt[:, 2:3] * ya + rt[:, 3:4] * yb
    h2 = _load_token_tiles(h2t_ref, 0, TM_COMB)
    o_ref[...] = _layer_norm(ALPHA * h2 + ff, g3_ref[...], b3_ref[...])


def _combine(pos3, yt, h2t, rt, g3, b3, tile0, n_tiles):
    last = tile0 + n_tiles - 1
    smem_pos = lambda f: pl.BlockSpec((1, 1, 2 * TM_COMB), f, memory_space=pltpu.SMEM)
    full = lambda a: pl.BlockSpec(a.shape, lambda i: (0,) * a.ndim)
    return pl.pallas_call(
        functools.partial(_combine_kernel, n_tiles),
        grid=(n_tiles,),
        in_specs=[smem_pos(lambda i: (tile0 + i, 0, 0)),
                  smem_pos(lambda i: (jnp.minimum(tile0 + i + 1, last), 0, 0)),
                  pl.BlockSpec(memory_space=pl.ANY),
                  pl.BlockSpec((TM_COMB * ROW_CHUNKS, LANES), lambda i: (tile0 + i, 0)),
                  pl.BlockSpec((TM_COMB, LANES), lambda i: (tile0 + i, 0)),
                  full(g3), full(b3)],
        out_specs=pl.BlockSpec((TM_COMB, D_MODEL), lambda i: (i, 0)),
        out_shape=jax.ShapeDtypeStruct((n_tiles * TM_COMB, D_MODEL), jnp.float32),
        scratch_shapes=[pltpu.VMEM((2 * 2 * TM_COMB * ROW_CHUNKS, LANES), jnp.float32),
                        pltpu.SemaphoreType.DMA((2,))],
        compiler_params=_params(("arbitrary",)),
        name="moe_combine",
    )(pos3, pos3, yt, h2t, rt, g3, b3)


def _routing_plan(rt):
    i32 = jnp.int32
    e = jnp.concatenate([rt[:, 0], rt[:, 1]]).astype(i32)
    onehot = (e[:, None] == jnp.arange(N_EXPERTS, dtype=i32)[None, :]).astype(i32)
    csum = jnp.cumsum(onehot, axis=0)
    rank = jnp.take_along_axis(csum, e[:, None], axis=1)[:, 0] - 1
    counts = csum[-1]
    starts = jnp.cumsum(counts) - counts
    pos = starts[e] + rank
    pos3 = jnp.concatenate([pos[:N_ALL].reshape(-1, 1, TM_COMB), pos[N_ALL:].reshape(-1, 1, TM_COMB)], axis=2)
    tiles = jnp.arange(MOE_TILES, dtype=i32) * TM_MOE
    rank_t = jnp.arange(MOE_TILES, dtype=i32) + jnp.sum((starts[None, :] < tiles[:, None]).astype(i32), axis=1)
    rank_s = jnp.arange(N_EXPERTS, dtype=i32) + jnp.sum((tiles[None, :] <= starts[:, None]).astype(i32), axis=1)
    vals = jnp.concatenate([tiles, starts])
    ranks = jnp.concatenate([rank_t, rank_s])
    slot = jnp.arange(MOE_ITEMS, dtype=i32)
    lo = jnp.sum(jnp.where(ranks[None, :] == slot[:, None], vals[None, :], 0), axis=1)
    hi = jnp.concatenate([lo[1:], jnp.full((1,), N_ASSIGN, i32)])
    item_tile = jnp.minimum(lo // TM_MOE, MOE_TILES - 1)
    item_expert = jnp.clip(jnp.sum((starts[None, :] <= lo[:, None]).astype(i32), axis=1) - 1, 0, N_EXPERTS - 1)
    base = item_tile * TM_MOE
    return item_tile, item_expert, lo - base, hi - base, pos3


def kernel(x_prompt, x_sample, mem_prompt, cache_swa_k, cache_swa_v, cache_conv, cache_mem_k, cache_mem_v,
           w_in, sinks, conv_w, w_mix_out, ln1_g, ln1_b, w_q_mem, w_k_mem, w_v_mem, w_o_mem, ln2_g, ln2_b,
           w_router_group, b_router_group, w_router_expert, b_router_expert, w_gate, w_up, w_down,
           ln3_g, ln3_b):
    f32 = jnp.float32
    row = lambda a: a.reshape(1, -1).astype(f32)
    w_in_b, wmix_b, wq_b, wk_b, wv_b, wo_b = (_bf16(w) for w in (w_in, w_mix_out, w_q_mem, w_k_mem, w_v_mem, w_o_mem))
    g1, b1, g2, b2, g3, b3 = (row(a) for a in (ln1_g, ln1_b, ln2_g, ln2_b, ln3_g, ln3_b))
    pad = LANES - N_GROUPS - N_EXPERTS
    wr = jnp.concatenate([w_router_group, w_router_expert, jnp.zeros((D_MODEL, pad), f32)], axis=1)
    wrh = _bf16(wr)
    wrl = _bf16(wr - wrh.astype(f32))
    br = jnp.concatenate([b_router_group, b_router_expert, jnp.zeros((pad,), f32)]).reshape(1, LANES)

    xs = x_sample.reshape(N_SAMPLE, D_MODEL)
    cos_s, sin_s = _rope_tables(PAST_LEN + jnp.arange(DEC_SEQ))
    cos_s, sin_s = jnp.tile(cos_s, (DEC_BATCH, 1)), jnp.tile(sin_s, (DEC_BATCH, 1))
    c0 = jnp.repeat(cache_conv[:, 0], DEC_SEQ, axis=0)
    c1 = jnp.repeat(cache_conv[:, 1], DEC_SEQ, axis=0)
    q_s, k_s, v_s, conv_s, u_s = _proj_sample(xs, w_in_b, cos_s, sin_s, conv_w, c0, c1)
    attn_s, swa_k_s, swa_v_s = _swa_sample(sinks, q_s, k_s, v_s, cache_swa_k, cache_swa_v)
    h1_s, qm_s = _post_a_sample(attn_s, conv_s, xs, wmix_b, g1, b1, wq_b)
    o_s = _mem_attn_sample(qm_s, cache_mem_k, cache_mem_v)
    h2t_s, rt_s = _post_b_sample(o_s, h1_s, wo_b, g2, b2, wrh, wrl, br)

    xp = x_prompt.reshape(N_PROMPT, D_MODEL)
    cos_p, sin_p = _rope_tables(jnp.arange(N_PROMPT))
    q_p, kx_p, vx_p, conv_p, k_tail, v_tail, u_tail = _proj_prompt(xp, w_in_b, cos_p, sin_p, conv_w)
    attn_p = _swa_prompt(sinks, q_p, kx_p, vx_p)
    mk, mv, mk_b, mv_b = _mem_kv(mem_prompt.reshape(N_MEM, D_MODEL), wk_b, wv_b)
    h2t, rt = _post_prompt(attn_p, conv_p, xp, wmix_b, g1, b1, wq_b, mk_b, mv_b, wo_b, g2, b2,
                           wrh, wrl, br, h2t_s, rt_s)

    item_tile, item_expert, item_lo, item_hi, pos3 = _routing_plan(rt)
    x_sorted = _dispatch(pos3, h2t)
    yt = _moe_ffn(item_tile, item_expert, item_lo, item_hi, x_sorted, w_gate, w_up, w_down)
    y_p = _combine(pos3, yt, h2t, rt, g3, b3, 0, N_PROMPT // TM_COMB)
    y_s = _combine(pos3, yt, h2t, rt, g3, b3, N_PROMPT // TM_COMB, N_SAMPLE // TM_COMB)

    return (y_p.reshape(1, SEQ, D_MODEL),
            y_s.reshape(DEC_BATCH, DEC_SEQ, D_MODEL),
            k_tail.reshape(1, WINDOW, N_KV_HEADS, HEAD_DIM),
            v_tail.reshape(1, WINDOW, N_KV_HEADS, HEAD_DIM),
            u_tail[SUBLANES - (CONV_K - 1):].reshape(1, CONV_K - 1, CONV_CH),
            mk.reshape(1, N_MEM, MEM_HEADS, MEM_HEAD_DIM),
            mv.reshape(1, N_MEM, MEM_HEADS, MEM_HEAD_DIM),
            swa_k_s.reshape(DEC_BATCH, WINDOW, N_KV_HEADS, HEAD_DIM),
            swa_v_s.reshape(DEC_BATCH, WINDOW, N_KV_HEADS, HEAD_DIM),
            u_s.reshape(DEC_BATCH, DEC_SEQ, CONV_CH)[:, DEC_SEQ - (CONV_K - 1):])
```

```python
import functools

import jax
import jax.numpy as jnp
from jax import lax
from jax.experimental import pallas as pl
from jax.experimental.pallas import tpu as pltpu

D_MODEL = 1024
SEQ = 16384
DEC_BATCH = 128
DEC_SEQ = 4
PAST_LEN = 16384
ATTN_WIDTH = 512
CONV_CH = 512
HEAD_DIM = 64
N_HEADS = 8
N_KV_HEADS = 2
KV_WIDTH = 128
WINDOW = 128
ROPE_THETA = 500000.0
ROPE_DIM = 16
CONV_K = 3
Q_END = ATTN_WIDTH
K_END = Q_END + KV_WIDTH
V_END = K_END + KV_WIDTH
B_END = V_END + CONV_CH
C_END = B_END + CONV_CH
IN_WIDTH = C_END + CONV_CH
N_MEM = 256
MEM_HEADS = 4
MEM_HEAD_DIM = 256
N_GROUPS = 4
EXPERTS_PER_GROUP = 8
N_EXPERTS = 32
EXPERT_FF = 256
ALPHA = 2.0 ** 0.25
LN_EPS = 1e-5

LANES = 128
SUBLANES = 8
ROW_CHUNKS = D_MODEL // LANES
VMEM_LIMIT = 56 * 1024 * 1024

N_PROMPT = SEQ
N_SAMPLE = DEC_BATCH * DEC_SEQ
N_ALL = N_PROMPT + N_SAMPLE
TM_PROJ = 512
TM_POST = 512
TM_MOE = 512
TM_COMB = 256
N_ASSIGN = 2 * N_ALL
MOE_TILES = N_ASSIGN // TM_MOE
MOE_ITEMS = MOE_TILES + N_EXPERTS
SAMPLE_BB = 4

assert ROW_CHUNKS == SUBLANES
assert N_SAMPLE == TM_POST
assert N_ASSIGN % TM_MOE == 0 and N_ALL % TM_COMB == 0


def _params(sem, vmem=VMEM_LIMIT):
    return pltpu.CompilerParams(dimension_semantics=sem, vmem_limit_bytes=vmem)


def _bf16(x):
    return x.astype(jnp.bfloat16)


def _dot(a, b):
    return jnp.dot(a, b, preferred_element_type=jnp.float32)


def _dot_nt(a, b):
    return lax.dot_general(a, b, (((1,), (1,)), ((), ())), preferred_element_type=jnp.float32)


def _layer_norm(x, g, b):
    mu = jnp.mean(x, axis=-1, keepdims=True)
    xc = x - mu
    var = jnp.mean(xc * xc, axis=-1, keepdims=True)
    return xc * lax.rsqrt(var + LN_EPS) * g + b


def _rope(x, cos_t, sin_t):
    lane = lax.broadcasted_iota(jnp.int32, x.shape, 1) % HEAD_DIM
    half = ROPE_DIM // 2
    partner = jnp.where(lane < half, pltpu.roll(x, LANES - half, axis=1), pltpu.roll(x, half, axis=1))
    return x * cos_t + partner * sin_t


def _head_slabs(x):
    lane = lax.broadcasted_iota(jnp.int32, x.shape, 1)
    lo = lane < HEAD_DIM
    sw = pltpu.roll(x, HEAD_DIM, axis=1)
    zero = jnp.zeros_like(x)
    slabs = [jnp.where(lo, x, zero), jnp.where(lo, zero, sw), jnp.where(lo, sw, zero), jnp.where(lo, zero, x)]
    return _bf16(jnp.concatenate(slabs, axis=1))


def _store_token_tiles(ref, val):
    rows = val.shape[0]
    for c in range(ROW_CHUNKS):
        ref[pl.ds(c, rows, stride=ROW_CHUNKS), :] = val[:, c * LANES:(c + 1) * LANES]


def _load_token_tiles(ref, base, rows):
    return jnp.concatenate(
        [ref[pl.ds(base + c, rows, stride=ROW_CHUNKS), :] for c in range(ROW_CHUNKS)], axis=1)


def _proj_common(x_ref, w_ref, cos_ref, sin_ref):
    xb = _bf16(x_ref[...])
    cos_t = cos_ref[...]
    sin_t = sin_ref[...]
    q = _dot(xb, w_ref[:, 0:Q_END])
    q_rot = jnp.concatenate(
        [_rope(q[:, p * LANES:(p + 1) * LANES], cos_t, sin_t) for p in range(ATTN_WIDTH // LANES)], axis=1)
    q_out = _bf16(q_rot * (HEAD_DIM ** -0.5))
    k = _rope(_dot(xb, w_ref[:, Q_END:K_END]), cos_t, sin_t)
    v = _dot(xb, w_ref[:, K_END:V_END])
    bg = _dot(xb, w_ref[:, V_END:B_END])
    u = _dot(xb, w_ref[:, B_END:C_END]) * _dot(xb, w_ref[:, C_END:IN_WIDTH])
    return q_out, k, v, bg, u


def _conv3(bg, u, u1, u2, cw_ref):
    cw = cw_ref[...]
    return bg * (cw[0:1, :] * u2 + cw[1:2, :] * u1 + cw[2:3, :] * u)


def _proj_prompt_kernel(x_ref, w_ref, cos_ref, sin_ref, cw_ref,
                        q_ref, kx_ref, vx_ref, conv_ref, ktail_ref, vtail_ref, utail_ref, carry_ref):
    @pl.when(pl.program_id(0) == 0)
    def _():
        carry_ref[...] = jnp.zeros_like(carry_ref)

    q_out, k, v, bg, u = _proj_common(x_ref, w_ref, cos_ref, sin_ref)
    tm = u.shape[0]
    ext = jnp.concatenate([carry_ref[...], u], axis=0)
    u1 = pltpu.roll(ext, 1, axis=0)[SUBLANES:SUBLANES + tm]
    u2 = pltpu.roll(ext, 2, axis=0)[SUBLANES:SUBLANES + tm]
    q_ref[...] = q_out
    kx_ref[...] = _head_slabs(k)
    vx_ref[...] = _head_slabs(v)
    conv_ref[...] = _bf16(_conv3(bg, u, u1, u2, cw_ref))
    ktail_ref[...] = k[tm - WINDOW:tm]
    vtail_ref[...] = v[tm - WINDOW:tm]
    utail_ref[...] = u[tm - SUBLANES:tm]
    carry_ref[...] = u[tm - SUBLANES:tm]


def _proj_sample_kernel(x_ref, w_ref, cos_ref, sin_ref, cw_ref, c0_ref, c1_ref,
                        q_ref, k_ref, v_ref, conv_ref, u_ref):
    q_out, k, v, bg, u = _proj_common(x_ref, w_ref, cos_ref, sin_ref)
    t = lax.broadcasted_iota(jnp.int32, u.shape, 0) % DEC_SEQ
    c0 = c0_ref[...]
    c1 = c1_ref[...]
    u1 = jnp.where(t >= 1, pltpu.roll(u, 1, axis=0), c1)
    u2 = jnp.where(t >= 2, pltpu.roll(u, 2, axis=0), jnp.where(t == 1, c1, c0))
    q_ref[...] = q_out.astype(jnp.float32)
    k_ref[...] = k
    v_ref[...] = v
    conv_ref[...] = _bf16(_conv3(bg, u, u1, u2, cw_ref))
    u_ref[...] = u


def _rope_tables(pos):
    half = ROPE_DIM // 2
    inv = ROPE_THETA ** (-jnp.arange(0, ROPE_DIM, 2, dtype=jnp.float32) / ROPE_DIM)
    zeros = jnp.zeros((HEAD_DIM - ROPE_DIM,), jnp.float32)
    inv_h = jnp.concatenate([inv, inv, zeros])
    sign_h = jnp.concatenate([-jnp.ones((half,), jnp.float32), jnp.ones((half,), jnp.float32), zeros])
    reps = LANES // HEAD_DIM
    ang = pos.astype(jnp.float32)[:, None] * jnp.tile(inv_h, reps)[None, :]
    return jnp.cos(ang), jnp.sin(ang) * jnp.tile(sign_h, reps)[None, :]


def _proj_prompt(x, w_in_b, cos_t, sin_t, conv_w):
    n = x.shape[0]
    tm = TM_PROJ
    row = lambda w: pl.BlockSpec((tm, w), lambda i: (i, 0))
    full = lambda a: pl.BlockSpec(a.shape, lambda i: (0,) * a.ndim)
    const = lambda r, w: pl.BlockSpec((r, w), lambda i: (0, 0))
    return pl.pallas_call(
        _proj_prompt_kernel,
        grid=(n // tm,),
        in_specs=[row(D_MODEL), full(w_in_b), row(LANES), row(LANES), full(conv_w)],
        out_specs=[row(ATTN_WIDTH), row(4 * LANES), row(4 * LANES), row(CONV_CH),
                   const(WINDOW, KV_WIDTH), const(WINDOW, KV_WIDTH), const(SUBLANES, CONV_CH)],
        out_shape=[jax.ShapeDtypeStruct((n, ATTN_WIDTH), jnp.bfloat16),
                   jax.ShapeDtypeStruct((n, 4 * LANES), jnp.bfloat16),
                   jax.ShapeDtypeStruct((n, 4 * LANES), jnp.bfloat16),
                   jax.ShapeDtypeStruct((n, CONV_CH), jnp.bfloat16),
                   jax.ShapeDtypeStruct((WINDOW, KV_WIDTH), jnp.float32),
                   jax.ShapeDtypeStruct((WINDOW, KV_WIDTH), jnp.float32),
                   jax.ShapeDtypeStruct((SUBLANES, CONV_CH), jnp.float32)],
        scratch_shapes=[pltpu.VMEM((SUBLANES, CONV_CH), jnp.float32)],
        compiler_params=_params(("arbitrary",)),
        name="proj_prompt",
    )(x, w_in_b, cos_t, sin_t, conv_w)


def _proj_sample(x, w_in_b, cos_t, sin_t, conv_w, c0, c1):
    n = x.shape[0]
    full = lambda a: pl.BlockSpec(a.shape, lambda i: (0,) * a.ndim)
    out = lambda w, dt: jax.ShapeDtypeStruct((n, w), dt)
    blk = lambda w: pl.BlockSpec((n, w), lambda i: (0, 0))
    return pl.pallas_call(
        _proj_sample_kernel,
        grid=(1,),
        in_specs=[full(x), full(w_in_b), full(cos_t), full(sin_t), full(conv_w), full(c0), full(c1)],
        out_specs=[blk(ATTN_WIDTH), blk(KV_WIDTH), blk(KV_WIDTH), blk(CONV_CH), blk(CONV_CH)],
        out_shape=[out(ATTN_WIDTH, jnp.float32), out(KV_WIDTH, jnp.float32), out(KV_WIDTH, jnp.float32),
                   out(CONV_CH, jnp.bfloat16), out(CONV_CH, jnp.float32)],
        compiler_params=_params(("arbitrary",)),
        name="proj_sample",
    )(x, w_in_b, cos_t, sin_t, conv_w, c0, c1)


def _sink_softmax_pv(s, valid, sink, vx):
    s = jnp.where(valid, s, -jnp.inf)
    m = jnp.maximum(jnp.max(s, axis=1, keepdims=True), sink)
    p = jnp.exp(s - m)
    den = jnp.sum(p, axis=1, keepdims=True) + jnp.exp(sink - m)
    return _dot(_bf16(p), vx) / den


SWA_QB = 2


def _swa_prompt_kernel(sinks_ref, q_ref, kc_ref, kp_ref, vc_ref, vp_ref, o_ref):
    step = pl.program_id(0)
    kall = jnp.concatenate([kp_ref[...], kc_ref[...]], axis=0)
    vall = jnp.concatenate([vp_ref[...], vc_ref[...]], axis=0)
    i = lax.broadcasted_iota(jnp.int32, (WINDOW, 2 * WINDOW), 0)
    j = lax.broadcasted_iota(jnp.int32, (WINDOW, 2 * WINDOW), 1)
    band = (j > i) & (j <= i + WINDOW)
    for sb in range(SWA_QB):
        rows = slice(sb * WINDOW, (sb + 1) * WINDOW)
        kcat = kall[sb * WINDOW:(sb + 2) * WINDOW]
        vcat = vall[sb * WINDOW:(sb + 2) * WINDOW]
        valid = band & ((step > 0) | (j >= WINDOW)) if sb == 0 else band
        for p in range(N_HEADS // 2):
            qs = q_ref[rows, p * LANES:(p + 1) * LANES]
            acc = None
            for e in range(2):
                hd = 2 * p + e
                slab = 2 * (hd // (N_HEADS // N_KV_HEADS)) + e
                kx = kcat[:, slab * LANES:(slab + 1) * LANES]
                vx = vcat[:, slab * LANES:(slab + 1) * LANES]
                o = _sink_softmax_pv(_dot_nt(qs, kx), valid, sinks_ref[hd], vx)
                acc = o if acc is None else acc + o
            o_ref[rows, p * LANES:(p + 1) * LANES] = _bf16(acc)


def _swa_prompt(sinks, q, kx, vx):
    n = q.shape[0]
    nb = n // (SWA_QB * WINDOW)
    cur = lambda w: pl.BlockSpec((SWA_QB * WINDOW, w), lambda i: (i, 0))
    prev = lambda w: pl.BlockSpec((WINDOW, w), lambda i: (jnp.maximum(SWA_QB * i - 1, 0), 0))
    return pl.pallas_call(
        _swa_prompt_kernel,
        grid=(nb,),
        in_specs=[pl.BlockSpec(memory_space=pltpu.SMEM), cur(ATTN_WIDTH),
                  cur(4 * LANES), prev(4 * LANES), cur(4 * LANES), prev(4 * LANES)],
        out_specs=cur(ATTN_WIDTH),
        out_shape=jax.ShapeDtypeStruct((n, ATTN_WIDTH), jnp.bfloat16),
        compiler_params=_params(("arbitrary",)),
        name="swa_prompt",
    )(sinks, q, kx, kx, vx, vx)


SWA_ROWS = N_HEADS * DEC_SEQ
NEW_ROWS = 2 * SUBLANES


def _swa_sample_kernel(q_ref, sink_ref, kn_ref, vn_ref, kt_ref, vt_ref, o_ref, okt_ref, ovt_ref):
    t = lax.broadcasted_iota(jnp.int32, (SWA_ROWS, WINDOW), 0) % DEC_SEQ
    j = lax.broadcasted_iota(jnp.int32, (SWA_ROWS, WINDOW), 1)
    valid_c = j > t
    valid_n = (lax.broadcasted_iota(jnp.int32, (SWA_ROWS, NEW_ROWS), 1)
               <= lax.broadcasted_iota(jnp.int32, (SWA_ROWS, NEW_ROWS), 0) % DEC_SEQ)
    lane = lax.broadcasted_iota(jnp.int32, (KV_WIDTH, WINDOW), 1)
    sink = sink_ref[:, 0:1]
    shift = WINDOW - DEC_SEQ
    zrows = jnp.zeros((KV_WIDTH - NEW_ROWS, KV_WIDTH), jnp.float32)
    for b in range(SAMPLE_BB):
        q = _bf16(q_ref[b])
        kt, vt = kt_ref[b], vt_ref[b]
        kn, vn = kn_ref[b], vn_ref[b]
        s_c = jnp.where(valid_c, _dot(q, _bf16(kt)), -jnp.inf)
        s_n = jnp.where(valid_n, _dot_nt(q, _bf16(kn)), -jnp.inf)
        m = jnp.maximum(jnp.maximum(jnp.max(s_c, axis=1, keepdims=True), jnp.max(s_n, axis=1, keepdims=True)), sink)
        p_c = jnp.exp(s_c - m)
        p_n = jnp.exp(s_n - m)
        den = jnp.sum(p_c, axis=1, keepdims=True) + jnp.sum(p_n, axis=1, keepdims=True) + jnp.exp(sink - m)
        o_ref[b] = (_dot_nt(_bf16(p_c), _bf16(vt)) + _dot(_bf16(p_n), _bf16(vn))) / den
        for old, new, dst in ((kt, kn, okt_ref), (vt, vn, ovt_ref)):
            new_cols = pltpu.roll(jnp.concatenate([new, zrows], axis=0).T, shift, axis=1)
            dst[b] = jnp.where(lane >= shift, new_cols, pltpu.roll(old, shift, axis=1))


def _swa_sample(sinks, q, kn, vn, cache_k, cache_v):
    nb = cache_k.shape[0]
    bb = SAMPLE_BB
    groups = N_HEADS // N_KV_HEADS
    qh = q.reshape(nb, DEC_SEQ, N_KV_HEADS, groups, HEAD_DIM).transpose(0, 2, 3, 1, 4)
    qh = qh.reshape(nb, N_KV_HEADS, groups * DEC_SEQ, HEAD_DIM)
    zeros = jnp.zeros_like(qh[:, 0])
    qbd = jnp.concatenate([jnp.concatenate([qh[:, 0], zeros], axis=-1),
                           jnp.concatenate([zeros, qh[:, 1]], axis=-1)], axis=1)
    sink_col = jnp.broadcast_to(jnp.repeat(sinks, DEC_SEQ).reshape(SWA_ROWS, 1), (SWA_ROWS, LANES))
    pad8 = lambda a: jnp.pad(a.reshape(nb, DEC_SEQ, KV_WIDTH), ((0, 0), (0, NEW_ROWS - DEC_SEQ), (0, 0)))
    to_t = lambda c: c.transpose(0, 2, 3, 1).reshape(nb, KV_WIDTH, WINDOW)
    blk = lambda r, w: pl.BlockSpec((bb, r, w), lambda i: (i, 0, 0))
    o, okt, ovt = pl.pallas_call(
        _swa_sample_kernel,
        grid=(nb // bb,),
        in_specs=[blk(SWA_ROWS, KV_WIDTH), pl.BlockSpec((SWA_ROWS, LANES), lambda i: (0, 0)),
                  blk(NEW_ROWS, KV_WIDTH), blk(NEW_ROWS, KV_WIDTH), blk(KV_WIDTH, WINDOW), blk(KV_WIDTH, WINDOW)],
        out_specs=[blk(SWA_ROWS, KV_WIDTH), blk(KV_WIDTH, WINDOW), blk(KV_WIDTH, WINDOW)],
        out_shape=[jax.ShapeDtypeStruct((nb, SWA_ROWS, KV_WIDTH), jnp.float32),
                   jax.ShapeDtypeStruct((nb, KV_WIDTH, WINDOW), jnp.float32),
                   jax.ShapeDtypeStruct((nb, KV_WIDTH, WINDOW), jnp.float32)],
        compiler_params=_params(("arbitrary",)),
        name="swa_sample",
    )(qbd, sink_col, pad8(kn), pad8(vn), to_t(cache_k), to_t(cache_v))
    o = o.reshape(nb, N_KV_HEADS, groups, DEC_SEQ, N_KV_HEADS, HEAD_DIM)
    attn = jnp.stack([o[:, h, :, :, h, :] for h in range(N_KV_HEADS)], axis=1)
    attn = attn.transpose(0, 3, 1, 2, 4).reshape(nb * DEC_SEQ, ATTN_WIDTH)
    from_t = lambda c: c.reshape(nb, N_KV_HEADS, HEAD_DIM, WINDOW).transpose(0, 3, 1, 2)
    return attn, from_t(okt), from_t(ovt)


def _mem_kv_kernel(mem_ref, wk_ref, wv_ref, mk_ref, mv_ref, mkb_ref, mvb_ref):
    mb = _bf16(mem_ref[...])
    mk = _dot(mb, wk_ref[...])
    mv = _dot(mb, wv_ref[...])
    mk_ref[...] = mk
    mv_ref[...] = mv
    mkb_ref[...] = _bf16(mk)
    mvb_ref[...] = _bf16(mv)


def _mem_kv(mem, wk_b, wv_b):
    full = lambda a: pl.BlockSpec(a.shape, lambda i: (0,) * a.ndim)
    blk = pl.BlockSpec((N_MEM, D_MODEL), lambda i: (0, 0))
    f32 = jax.ShapeDtypeStruct((N_MEM, D_MODEL), jnp.float32)
    b16 = jax.ShapeDtypeStruct((N_MEM, D_MODEL), jnp.bfloat16)
    return pl.pallas_call(
        _mem_kv_kernel,
        grid=(1,),
        in_specs=[full(mem), full(wk_b), full(wv_b)],
        out_specs=[blk, blk, blk, blk],
        out_shape=[f32, f32, b16, b16],
        compiler_params=_params(("arbitrary",)),
        name="mem_kv",
    )(mem, wk_b, wv_b)


def _mix_ln1(attn_ref, conv_ref, x_ref, wmix_ref, g1_ref, b1_ref):
    mix = _dot(_bf16(attn_ref[...]), wmix_ref[0:ATTN_WIDTH, :]) + _dot(conv_ref[...], wmix_ref[ATTN_WIDTH:, :])
    return _layer_norm(ALPHA * x_ref[...] + mix, g1_ref[...], b1_ref[...])


def _mem_q(h1, wq_ref):
    return _bf16(_dot(_bf16(h1), wq_ref[...]) * (MEM_HEAD_DIM ** -0.5))


def _route(h2, wrh_ref, wrl_ref, br_ref, carry):
    hi = _bf16(h2)
    lo = _bf16(h2 - hi.astype(jnp.float32))
    logits = _dot(hi, wrh_ref[...]) + _dot(lo, wrh_ref[...]) + _dot(hi, wrl_ref[...]) + br_ref[...]
    lane_i = lax.broadcasted_iota(jnp.int32, logits.shape, 1)
    lane = lane_i.astype(jnp.float32)
    big = jnp.float32(LANES)
    is_g = lane_i < N_GROUPS
    gl = jnp.where(is_g, logits, -jnp.inf)
    gmax = jnp.max(gl, axis=1, keepdims=True)
    gidx = jnp.min(jnp.where(is_g & (logits == gmax), lane, big), axis=1, keepdims=True)
    gsum = jnp.sum(jnp.exp(gl - gmax), axis=1, keepdims=True)
    gw = 1.0 / gsum
    eid = lane_i - N_GROUPS
    assert EXPERTS_PER_GROUP == 8
    grp = lax.shift_right_arithmetic(eid, jnp.full_like(eid, 3)).astype(jnp.float32)
    in_e = (lane_i >= N_GROUPS) & (lane_i < N_GROUPS + N_EXPERTS) & (grp == gidx)
    v1 = jnp.max(jnp.where(in_e, logits, -jnp.inf), axis=1, keepdims=True)
    i1 = jnp.min(jnp.where(in_e & (logits == v1), lane, big), axis=1, keepdims=True)
    rest = in_e & (lane != i1)
    v2 = jnp.max(jnp.where(rest, logits, -jnp.inf), axis=1, keepdims=True)
    i2 = jnp.min(jnp.where(rest & (logits == v2), lane, big), axis=1, keepdims=True)
    ex = jnp.exp(v2 - v1)
    den = 1.0 + ex
    w1 = gw / den
    w2 = gw * ex / den
    zero = jnp.zeros_like(logits)
    pick1 = lane == i1
    pick2 = lane == i2
    sel = jnp.where(pick1 | pick2, 1.0, 0.0)
    tm = logits.shape[0]
    earlier = lax.broadcasted_iota(jnp.int32, (tm, tm), 1) < lax.broadcasted_iota(jnp.int32, (tm, tm), 0)
    before = _dot(_bf16(jnp.where(earlier, 1.0, 0.0)), _bf16(sel)) + carry
    rank1 = jnp.sum(jnp.where(pick1, before, zero), axis=1, keepdims=True)
    rank2 = jnp.sum(jnp.where(pick2, before, zero), axis=1, keepdims=True)
    cols = (i1 - N_GROUPS, i2 - N_GROUPS, w1, w2, rank1, rank2)
    route = zero
    for k, col in enumerate(cols):
        route = jnp.where(lane_i == k, col, route)
    return route, carry + jnp.sum(sel, axis=0, keepdims=True)


def _post_prompt_kernel(attn_ref, conv_ref, x_ref, wmix_ref, g1_ref, b1_ref, wq_ref, mk_ref, mv_ref, wo_ref,
                        g2_ref, b2_ref, wrh_ref, wrl_ref, br_ref, h2s_ref, rts_ref, cnts_ref,
                        h2t_ref, rt_ref, cnt_ref, carry_ref):
    is_tail = pl.program_id(0) == N_PROMPT // TM_POST

    @pl.when(pl.program_id(0) == 0)
    def _():
        carry_ref[...] = cnts_ref[...]

    @pl.when(is_tail)
    def _():
        h2t_ref[...] = h2s_ref[...]
        rt_ref[...] = rts_ref[...]

    @pl.when(jnp.logical_not(is_tail))
    def _():
        h1 = _mix_ln1(attn_ref, conv_ref, x_ref, wmix_ref, g1_ref, b1_ref)
        qm = _mem_q(h1, wq_ref)
        outs = []
        for h in range(MEM_HEADS):
            sl = slice(h * MEM_HEAD_DIM, (h + 1) * MEM_HEAD_DIM)
            s = _dot_nt(qm[:, sl], mk_ref[:, sl])
            m = jnp.max(s, axis=1, keepdims=True)
            p = jnp.exp(s - m)
            den = jnp.sum(p, axis=1, keepdims=True)
            outs.append(_dot(_bf16(p), mv_ref[:, sl]) / den)
        o = _bf16(jnp.concatenate(outs, axis=1))
        h2 = _layer_norm(ALPHA * h1 + _dot(o, wo_ref[...]), g2_ref[...], b2_ref[...])
        _store_token_tiles(h2t_ref, h2)
        route, carry = _route(h2, wrh_ref, wrl_ref, br_ref, carry_ref[0:1, :])
        rt_ref[...] = route
        carry_ref[...] = jnp.broadcast_to(carry, carry_ref.shape)
        cnt_ref[...] = jnp.broadcast_to(carry, cnt_ref.shape)


def _post_prompt(attn, conv, x, wmix_b, g1, b1, wq_b, mk_b, mv_b, wo_b, g2, b2, wrh, wrl, br, h2t_s, rt_s, cnt_s):
    n = x.shape[0]
    tm = TM_POST
    steps = n // tm
    row = lambda w: pl.BlockSpec((tm, w), lambda i: (jnp.minimum(i, steps - 1), 0))
    full = lambda a: pl.BlockSpec(a.shape, lambda i: (0,) * a.ndim)
    weights = (wmix_b, g1, b1, wq_b, mk_b, mv_b, wo_b, g2, b2, wrh, wrl, br, h2t_s, rt_s, cnt_s)
    n_out = n + h2t_s.shape[0] // ROW_CHUNKS
    return pl.pallas_call(
        _post_prompt_kernel,
        grid=(steps + 1,),
        in_specs=[row(ATTN_WIDTH), row(CONV_CH), row(D_MODEL)] + [full(a) for a in weights],
        out_specs=[pl.BlockSpec((tm * ROW_CHUNKS, LANES), lambda i: (i, 0)),
                   pl.BlockSpec((tm, LANES), lambda i: (i, 0)),
                   pl.BlockSpec((SUBLANES, LANES), lambda i: (0, 0))],
        out_shape=[jax.ShapeDtypeStruct((n_out * ROW_CHUNKS, LANES), jnp.float32),
                   jax.ShapeDtypeStruct((n_out, LANES), jnp.float32),
                   jax.ShapeDtypeStruct((SUBLANES, LANES), jnp.float32)],
        scratch_shapes=[pltpu.VMEM((SUBLANES, LANES), jnp.float32)],
        compiler_params=_params(("arbitrary",)),
        name="post_prompt",
    )(attn, conv, x, *weights)


def _post_a_sample_kernel(attn_ref, conv_ref, x_ref, wmix_ref, g1_ref, b1_ref, wq_ref, h1_ref, qm_ref):
    h1 = _mix_ln1(attn_ref, conv_ref, x_ref, wmix_ref, g1_ref, b1_ref)
    h1_ref[...] = h1
    qm_ref[...] = _mem_q(h1, wq_ref).astype(jnp.float32)


def _post_a_sample(attn, conv, x, wmix_b, g1, b1, wq_b):
    n = x.shape[0]
    args = (attn, conv, x, wmix_b, g1, b1, wq_b)
    full = lambda a: pl.BlockSpec(a.shape, lambda i: (0,) * a.ndim)
    blk = pl.BlockSpec((n, D_MODEL), lambda i: (0, 0))
    return pl.pallas_call(
        _post_a_sample_kernel,
        grid=(1,),
        in_specs=[full(a) for a in args],
        out_specs=[blk, blk],
        out_shape=[jax.ShapeDtypeStruct((n, D_MODEL), jnp.float32),
                   jax.ShapeDtypeStruct((n, D_MODEL), jnp.float32)],
        compiler_params=_params(("arbitrary",)),
        name="post_a_sample",
    )(*args)


MEM_ROWS = MEM_HEADS * DEC_SEQ


def _mem_attn_sample_kernel(q_ref, mk_ref, mv_ref, o_ref):
    nk = N_MEM * MEM_HEADS
    row_h = lax.broadcasted_iota(jnp.int32, (MEM_ROWS, nk), 0) // DEC_SEQ
    key_h = lax.broadcasted_iota(jnp.int32, (MEM_ROWS, nk), 1) % MEM_HEADS
    own = row_h == key_h
    for b in range(SAMPLE_BB):
        k2 = _bf16(mk_ref[b].reshape(nk, MEM_HEAD_DIM))
        v2 = _bf16(mv_ref[b].reshape(nk, MEM_HEAD_DIM))
        s = jnp.where(own, _dot_nt(_bf16(q_ref[b]), k2), -jnp.inf)
        m = jnp.max(s, axis=1, keepdims=True)
        p = jnp.exp(s - m)
        den = jnp.sum(p, axis=1, keepdims=True)
        o_ref[b] = _dot(_bf16(p), v2) / den


def _mem_attn_sample(qm, mk, mv):
    nb = mk.shape[0]
    bb = SAMPLE_BB
    q = qm.reshape(nb, DEC_SEQ, MEM_HEADS, MEM_HEAD_DIM).transpose(0, 2, 1, 3).reshape(nb, MEM_ROWS, MEM_HEAD_DIM)
    rows = pl.BlockSpec((bb, MEM_ROWS, MEM_HEAD_DIM), lambda i: (i, 0, 0))
    kv = pl.BlockSpec((bb, N_MEM, MEM_HEADS, MEM_HEAD_DIM), lambda i: (i, 0, 0, 0))
    o = pl.pallas_call(
        _mem_attn_sample_kernel,
        grid=(nb // bb,),
        in_specs=[rows, kv, kv],
        out_specs=rows,
        out_shape=jax.ShapeDtypeStruct((nb, MEM_ROWS, MEM_HEAD_DIM), jnp.float32),
        compiler_params=_params(("arbitrary",)),
        name="mem_attn_sample",
    )(q, mk, mv)
    return o.reshape(nb, MEM_HEADS, DEC_SEQ, MEM_HEAD_DIM).transpose(0, 2, 1, 3).reshape(nb * DEC_SEQ, D_MODEL)


def _post_b_sample_kernel(o_ref, h1_ref, wo_ref, g2_ref, b2_ref, wrh_ref, wrl_ref, br_ref, h2t_ref, rt_ref, cnt_ref):
    h2 = _layer_norm(ALPHA * h1_ref[...] + _dot(_bf16(o_ref[...]), wo_ref[...]), g2_ref[...], b2_ref[...])
    _store_token_tiles(h2t_ref, h2)
    route, carry = _route(h2, wrh_ref, wrl_ref, br_ref, jnp.zeros((1, LANES), jnp.float32))
    rt_ref[...] = route
    cnt_ref[...] = jnp.broadcast_to(carry, cnt_ref.shape)


def _post_b_sample(o, h1, wo_b, g2, b2, wrh, wrl, br):
    n = h1.shape[0]
    args = (o, h1, wo_b, g2, b2, wrh, wrl, br)
    full = lambda a: pl.BlockSpec(a.shape, lambda i: (0,) * a.ndim)
    return pl.pallas_call(
        _post_b_sample_kernel,
        grid=(1,),
        in_specs=[full(a) for a in args],
        out_specs=[pl.BlockSpec((n * ROW_CHUNKS, LANES), lambda i: (0, 0)),
                   pl.BlockSpec((n, LANES), lambda i: (0, 0)),
                   pl.BlockSpec((SUBLANES, LANES), lambda i: (0, 0))],
        out_shape=[jax.ShapeDtypeStruct((n * ROW_CHUNKS, LANES), jnp.float32),
                   jax.ShapeDtypeStruct((n, LANES), jnp.float32),
                   jax.ShapeDtypeStruct((SUBLANES, LANES), jnp.float32)],
        compiler_params=_params(("arbitrary",)),
        name="post_b_sample",
    )(*args)


def _row_gather_copy(src_hbm, idx, dst, dst_row, sem):
    s0 = pl.multiple_of(idx * ROW_CHUNKS, ROW_CHUNKS)
    d0 = pl.multiple_of(dst_row * ROW_CHUNKS, ROW_CHUNKS)
    return pltpu.make_async_copy(src_hbm.at[pl.ds(s0, ROW_CHUNKS), :], dst.at[pl.ds(d0, ROW_CHUNKS), :], sem)


def _dispatch_kernel(pos_ref, h2t_ref, xs_hbm, sem):
    def body(r, c):
        src = h2t_ref.at[pl.ds(pl.multiple_of(r * ROW_CHUNKS, ROW_CHUNKS), ROW_CHUNKS), :]
        for k in range(2):
            d0 = pl.multiple_of(pos_ref[0, 0, k * TM_COMB + r] * ROW_CHUNKS, ROW_CHUNKS)
            pltpu.make_async_copy(src, xs_hbm.at[pl.ds(d0, ROW_CHUNKS), :], sem.at[0]).start(priority=k)
        return c
    lax.fori_loop(0, TM_COMB, body, 0, unroll=8)
    for _ in range(2):
        pltpu.make_async_copy(h2t_ref, xs_hbm.at[pl.ds(0, TM_COMB * ROW_CHUNKS), :], sem.at[0]).wait()


def _dispatch(pos3, h2t):
    nt = N_ALL // TM_COMB
    return pl.pallas_call(
        _dispatch_kernel,
        grid=(nt,),
        in_specs=[pl.BlockSpec((1, 1, 2 * TM_COMB), lambda i: (i, 0, 0), memory_space=pltpu.SMEM),
                  pl.BlockSpec((TM_COMB * ROW_CHUNKS, LANES), lambda i: (i, 0))],
        out_specs=pl.BlockSpec(memory_space=pl.ANY),
        out_shape=jax.ShapeDtypeStruct((N_ASSIGN * ROW_CHUNKS, LANES), jnp.float32),
        scratch_shapes=[pltpu.SemaphoreType.DMA((1,))],
        compiler_params=_params(("arbitrary",)),
        name="moe_dispatch",
    )(pos3, h2t)


def _moe_ffn_kernel(it_ref, ie_ref, lo_ref, hi_ref, x_ref, wg_ref, wu_ref, wd_ref, y_ref, wgb, wub, wdb, cur_e):
    i = pl.program_id(0)
    lo = lo_ref[i]
    hi = hi_ref[i]
    e = ie_ref[i]

    @pl.when(i == 0)
    def _():
        cur_e[0] = -1

    @pl.when((hi > lo) & (cur_e[0] != e))
    def _():
        wgb[...] = _bf16(wg_ref[0])
        wub[...] = _bf16(wu_ref[0])
        wdb[...] = _bf16(wd_ref[0])
        cur_e[0] = e

    def ffn():
        x = _bf16(_load_token_tiles(x_ref, 0, TM_MOE))
        hg = _dot(x, wgb[...])
        hu = _dot(x, wub[...])
        h = hg / (1.0 + jnp.exp(-hg)) * hu
        return _dot(_bf16(h), wdb[...])

    def rows_mask():
        row = lax.broadcasted_iota(jnp.int32, (TM_MOE, LANES), 0)
        return (row >= lo) & (row < hi)

    @pl.when((hi > lo) & (lo == 0))
    def _():
        y = ffn()
        mask = rows_mask()
        for c in range(ROW_CHUNKS):
            y_ref[pl.ds(c, TM_MOE, stride=ROW_CHUNKS), :] = jnp.where(mask, y[:, c * LANES:(c + 1) * LANES], 0.0)

    @pl.when((hi > lo) & (lo > 0))
    def _():
        y = ffn()
        mask = rows_mask()
        for c in range(ROW_CHUNKS):
            sl = pl.ds(c, TM_MOE, stride=ROW_CHUNKS)
            y_ref[sl, :] = jnp.where(mask, y[:, c * LANES:(c + 1) * LANES], y_ref[sl, :])


def _moe_ffn(item_tile, item_expert, item_lo, item_hi, x_sorted, w_gate, w_up, w_down):
    wspec = lambda shp: pl.BlockSpec((1,) + shp, lambda i, it, ie, lo, hi: (ie[i], 0, 0))
    tile = pl.BlockSpec((TM_MOE * ROW_CHUNKS, LANES), lambda i, it, ie, lo, hi: (it[i], 0))
    grid_spec = pltpu.PrefetchScalarGridSpec(
        num_scalar_prefetch=4,
        grid=(MOE_ITEMS,),
        in_specs=[tile, wspec((D_MODEL, EXPERT_FF)), wspec((D_MODEL, EXPERT_FF)), wspec((EXPERT_FF, D_MODEL))],
        out_specs=tile,
        scratch_shapes=[pltpu.VMEM((D_MODEL, EXPERT_FF), jnp.bfloat16),
                        pltpu.VMEM((D_MODEL, EXPERT_FF), jnp.bfloat16),
                        pltpu.VMEM((EXPERT_FF, D_MODEL), jnp.bfloat16),
                        pltpu.SMEM((1,), jnp.int32)],
    )
    return pl.pallas_call(
        _moe_ffn_kernel,
        grid_spec=grid_spec,
        out_shape=jax.ShapeDtypeStruct((N_ASSIGN * ROW_CHUNKS, LANES), jnp.float32),
        compiler_params=_params(("arbitrary",)),
        name="moe_ffn",
    )(item_tile, item_expert, item_lo, item_hi, x_sorted, w_gate, w_up, w_down)


def _combine_kernel(nt, pos_cur_ref, pos_nxt_ref, yt_hbm, h2t_ref, rt_ref, g3_ref, b3_ref, o_ref, abuf, sem):
    t = pl.program_id(0)
    slot = t % 2
    rows = 2 * TM_COMB

    def issue(pos_ref, s):
        def body(j, c):
            for k in range(2):
                r = 2 * j + k
                _row_gather_copy(yt_hbm, pos_ref[0, 0, r], abuf, s * rows + r, sem.at[s]).start(priority=k)
            return c
        lax.fori_loop(0, rows // 2, body, 0, unroll=4)

    @pl.when(t == 0)
    def _():
        issue(pos_cur_ref, 0)

    @pl.when(t + 1 < nt)
    def _():
        issue(pos_nxt_ref, 1 - slot)

    base = pl.multiple_of(slot * (rows * ROW_CHUNKS), rows * ROW_CHUNKS)
    pltpu.make_async_copy(yt_hbm.at[pl.ds(0, rows * ROW_CHUNKS), :],
                          abuf.at[pl.ds(base, rows * ROW_CHUNKS), :], sem.at[slot]).wait()
    ya = _load_token_tiles(abuf, base, TM_COMB)
    yb = _load_token_tiles(abuf, base + TM_COMB * ROW_CHUNKS, TM_COMB)
    rt = rt_ref[...]
    ff = rt[:, 2:3] * ya + rt[:, 3:4] * yb
    h2 = _load_token_tiles(h2t_ref, 0, TM_COMB)
    o_ref[...] = _layer_norm(ALPHA * h2 + ff, g3_ref[...], b3_ref[...])


def _combine(pos3, yt, h2t, rt, g3, b3, tile0, n_tiles):
    last = tile0 + n_tiles - 1
    smem_pos = lambda f: pl.BlockSpec((1, 1, 2 * TM_COMB), f, memory_space=pltpu.SMEM)
    full = lambda a: pl.BlockSpec(a.shape, lambda i: (0,) * a.ndim)
    return pl.pallas_call(
        functools.partial(_combine_kernel, n_tiles),
        grid=(n_tiles,),
        in_specs=[smem_pos(lambda i: (tile0 + i, 0, 0)),
                  smem_pos(lambda i: (jnp.minimum(tile0 + i + 1, last), 0, 0)),
                  pl.BlockSpec(memory_space=pl.ANY),
                  pl.BlockSpec((TM_COMB * ROW_CHUNKS, LANES), lambda i: (tile0 + i, 0)),
                  pl.BlockSpec((TM_COMB, LANES), lambda i: (tile0 + i, 0)),
                  full(g3), full(b3)],
        out_specs=pl.BlockSpec((TM_COMB, D_MODEL), lambda i: (i, 0)),
        out_shape=jax.ShapeDtypeStruct((n_tiles * TM_COMB, D_MODEL), jnp.float32),
        scratch_shapes=[pltpu.VMEM((2 * 2 * TM_COMB * ROW_CHUNKS, LANES), jnp.float32),
                        pltpu.SemaphoreType.DMA((2,))],
        compiler_params=_params(("arbitrary",)),
        name="moe_combine",
    )(pos3, pos3, yt, h2t, rt, g3, b3)


def _routing_plan(rt, cnt):
    i32 = jnp.int32
    ids = rt[:, 0:6].astype(i32)
    counts = cnt[0, N_GROUPS:N_GROUPS + N_EXPERTS].astype(i32)
    starts = jnp.cumsum(counts) - counts
    pos0 = starts[ids[:, 0]] + ids[:, 4]
    pos1 = starts[ids[:, 1]] + ids[:, 5]
    pos3 = jnp.concatenate([pos0.reshape(-1, 1, TM_COMB), pos1.reshape(-1, 1, TM_COMB)], axis=2)
    tiles = jnp.arange(MOE_TILES, dtype=i32) * TM_MOE
    rank_t = jnp.arange(MOE_TILES, dtype=i32) + jnp.sum((starts[None, :] < tiles[:, None]).astype(i32), axis=1)
    rank_s = jnp.arange(N_EXPERTS, dtype=i32) + jnp.sum((tiles[None, :] <= starts[:, None]).astype(i32), axis=1)
    vals = jnp.concatenate([tiles, starts])
    ranks = jnp.concatenate([rank_t, rank_s])
    slot = jnp.arange(MOE_ITEMS, dtype=i32)
    lo = jnp.sum(jnp.where(ranks[None, :] == slot[:, None], vals[None, :], 0), axis=1)
    hi = jnp.concatenate([lo[1:], jnp.full((1,), N_ASSIGN, i32)])
    item_tile = jnp.minimum(lo // TM_MOE, MOE_TILES - 1)
    item_expert = jnp.clip(jnp.sum((starts[None, :] <= lo[:, None]).astype(i32), axis=1) - 1, 0, N_EXPERTS - 1)
    base = item_tile * TM_MOE
    return item_tile, item_expert, lo - base, hi - base, pos3


def kernel(x_prompt, x_sample, mem_prompt, cache_swa_k, cache_swa_v, cache_conv, cache_mem_k, cache_mem_v,
           w_in, sinks, conv_w, w_mix_out, ln1_g, ln1_b, w_q_mem, w_k_mem, w_v_mem, w_o_mem, ln2_g, ln2_b,
           w_router_group, b_router_group, w_router_expert, b_router_expert, w_gate, w_up, w_down,
           ln3_g, ln3_b):
    f32 = jnp.float32
    row = lambda a: a.reshape(1, -1).astype(f32)
    w_in_b, wmix_b, wq_b, wk_b, wv_b, wo_b = (_bf16(w) for w in (w_in, w_mix_out, w_q_mem, w_k_mem, w_v_mem, w_o_mem))
    g1, b1, g2, b2, g3, b3 = (row(a) for a in (ln1_g, ln1_b, ln2_g, ln2_b, ln3_g, ln3_b))
    pad = LANES - N_GROUPS - N_EXPERTS
    wr = jnp.concatenate([w_router_group, w_router_expert, jnp.zeros((D_MODEL, pad), f32)], axis=1)
    wrh = _bf16(wr)
    wrl = _bf16(wr - wrh.astype(f32))
    br = jnp.concatenate([b_router_group, b_router_expert, jnp.zeros((pad,), f32)]).reshape(1, LANES)

    xs = x_sample.reshape(N_SAMPLE, D_MODEL)
    cos_s, sin_s = _rope_tables(PAST_LEN + jnp.arange(DEC_SEQ))
    cos_s, sin_s = jnp.tile(cos_s, (DEC_BATCH, 1)), jnp.tile(sin_s, (DEC_BATCH, 1))
    c0 = jnp.repeat(cache_conv[:, 0], DEC_SEQ, axis=0)
    c1 = jnp.repeat(cache_conv[:, 1], DEC_SEQ, axis=0)
    q_s, k_s, v_s, conv_s, u_s = _proj_sample(xs, w_in_b, cos_s, sin_s, conv_w, c0, c1)
    attn_s, swa_k_s, swa_v_s = _swa_sample(sinks, q_s, k_s, v_s, cache_swa_k, cache_swa_v)
    h1_s, qm_s = _post_a_sample(attn_s, conv_s, xs, wmix_b, g1, b1, wq_b)
    o_s = _mem_attn_sample(qm_s, cache_mem_k, cache_mem_v)
    h2t_s, rt_s, cnt_s = _post_b_sample(o_s, h1_s, wo_b, g2, b2, wrh, wrl, br)

    xp = x_prompt.reshape(N_PROMPT, D_MODEL)
    cos_p, sin_p = _rope_tables(jnp.arange(N_PROMPT))
    q_p, kx_p, vx_p, conv_p, k_tail, v_tail, u_tail = _proj_prompt(xp, w_in_b, cos_p, sin_p, conv_w)
    attn_p = _swa_prompt(sinks, q_p, kx_p, vx_p)
    mk, mv, mk_b, mv_b = _mem_kv(mem_prompt.reshape(N_MEM, D_MODEL), wk_b, wv_b)
    h2t, rt, cnt = _post_prompt(attn_p, conv_p, xp, wmix_b, g1, b1, wq_b, mk_b, mv_b, wo_b, g2, b2,
                                wrh, wrl, br, h2t_s, rt_s, cnt_s)

    item_tile, item_expert, item_lo, item_hi, pos3 = _routing_plan(rt, cnt)
    x_sorted = _dispatch(pos3, h2t)
    yt = _moe_ffn(item_tile, item_expert, item_lo, item_hi, x_sorted, w_gate, w_up, w_down)
    y_p = _combine(pos3, yt, h2t, rt, g3, b3, 0, N_PROMPT // TM_COMB)
    y_s = _combine(pos3, yt, h2t, rt, g3, b3, N_PROMPT // TM_COMB, N_SAMPLE // TM_COMB)

    return (y_p.reshape(1, SEQ, D_MODEL),
            y_s.reshape(DEC_BATCH, DEC_SEQ, D_MODEL),
            k_tail.reshape(1, WINDOW, N_KV_HEADS, HEAD_DIM),
            v_tail.reshape(1, WINDOW, N_KV_HEADS, HEAD_DIM),
            u_tail[SUBLANES - (CONV_K - 1):].reshape(1, CONV_K - 1, CONV_CH),
            mk.reshape(1, N_MEM, MEM_HEADS, MEM_HEAD_DIM),
            mv.reshape(1, N_MEM, MEM_HEADS, MEM_HEAD_DIM),
            swa_k_s.reshape(DEC_BATCH, WINDOW, N_KV_HEADS, HEAD_DIM),
            swa_v_s.reshape(DEC_BATCH, WINDOW, N_KV_HEADS, HEAD_DIM),
            u_s.reshape(DEC_BATCH, DEC_SEQ, CONV_CH)[:, DEC_SEQ - (CONV_K - 1):])
```

```python
import functools

import jax
import jax.numpy as jnp
from jax import lax
from jax.experimental import pallas as pl
from jax.experimental.pallas import tpu as pltpu

D_MODEL = 1024
SEQ = 16384
DEC_BATCH = 128
DEC_SEQ = 4
PAST_LEN = 16384
ATTN_WIDTH = 512
CONV_CH = 512
HEAD_DIM = 64
N_HEADS = 8
N_KV_HEADS = 2
KV_WIDTH = 128
WINDOW = 128
ROPE_THETA = 500000.0
ROPE_DIM = 16
CONV_K = 3
Q_END = ATTN_WIDTH
K_END = Q_END + KV_WIDTH
V_END = K_END + KV_WIDTH
B_END = V_END + CONV_CH
C_END = B_END + CONV_CH
IN_WIDTH = C_END + CONV_CH
N_MEM = 256
MEM_HEADS = 4
MEM_HEAD_DIM = 256
N_GROUPS = 4
EXPERTS_PER_GROUP = 8
N_EXPERTS = 32
EXPERT_FF = 256
ALPHA = 2.0 ** 0.25
LN_EPS = 1e-5

LANES = 128
SUBLANES = 8
ROW_CHUNKS = D_MODEL // LANES
VMEM_LIMIT = 56 * 1024 * 1024

N_PROMPT = SEQ
N_SAMPLE = DEC_BATCH * DEC_SEQ
N_ALL = N_PROMPT + N_SAMPLE
TM_PROJ = 512
TM_POST = 512
TM_MOE = 512
TM_COMB = 512
N_ASSIGN = 2 * N_ALL
MOE_TILES = N_ASSIGN // TM_MOE
MOE_ITEMS = MOE_TILES + N_EXPERTS
SAMPLE_BB = 4

assert ROW_CHUNKS == SUBLANES
assert N_SAMPLE == TM_POST
assert N_ASSIGN % TM_MOE == 0 and N_ALL % TM_COMB == 0


def _params(sem, vmem=VMEM_LIMIT):
    return pltpu.CompilerParams(dimension_semantics=sem, vmem_limit_bytes=vmem)


def _bf16(x):
    return x.astype(jnp.bfloat16)


def _dot(a, b):
    return jnp.dot(a, b, preferred_element_type=jnp.float32)


def _dot_nt(a, b):
    return lax.dot_general(a, b, (((1,), (1,)), ((), ())), preferred_element_type=jnp.float32)


def _layer_norm(x, g, b):
    mu = jnp.mean(x, axis=-1, keepdims=True)
    xc = x - mu
    var = jnp.mean(xc * xc, axis=-1, keepdims=True)
    return xc * lax.rsqrt(var + LN_EPS) * g + b


def _rope(x, cos_t, sin_t):
    lane = lax.broadcasted_iota(jnp.int32, x.shape, 1) % HEAD_DIM
    half = ROPE_DIM // 2
    partner = jnp.where(lane < half, pltpu.roll(x, LANES - half, axis=1), pltpu.roll(x, half, axis=1))
    return x * cos_t + partner * sin_t


def _head_slabs(x):
    lane = lax.broadcasted_iota(jnp.int32, x.shape, 1)
    lo = lane < HEAD_DIM
    sw = pltpu.roll(x, HEAD_DIM, axis=1)
    zero = jnp.zeros_like(x)
    slabs = [jnp.where(lo, x, zero), jnp.where(lo, zero, sw), jnp.where(lo, sw, zero), jnp.where(lo, zero, x)]
    return _bf16(jnp.concatenate(slabs, axis=1))


def _store_token_tiles(ref, val):
    rows = val.shape[0]
    for c in range(ROW_CHUNKS):
        ref[pl.ds(c, rows, stride=ROW_CHUNKS), :] = val[:, c * LANES:(c + 1) * LANES]


def _load_token_tiles(ref, base, rows):
    return jnp.concatenate(
        [ref[pl.ds(base + c, rows, stride=ROW_CHUNKS), :] for c in range(ROW_CHUNKS)], axis=1)


def _proj_common(x_ref, w_ref, cos_ref, sin_ref):
    xb = _bf16(x_ref[...])
    cos_t = cos_ref[...]
    sin_t = sin_ref[...]
    q = _dot(xb, w_ref[:, 0:Q_END])
    q_rot = jnp.concatenate(
        [_rope(q[:, p * LANES:(p + 1) * LANES], cos_t, sin_t) for p in range(ATTN_WIDTH // LANES)], axis=1)
    q_out = _bf16(q_rot * (HEAD_DIM ** -0.5))
    k = _rope(_dot(xb, w_ref[:, Q_END:K_END]), cos_t, sin_t)
    v = _dot(xb, w_ref[:, K_END:V_END])
    bg = _dot(xb, w_ref[:, V_END:B_END])
    u = _dot(xb, w_ref[:, B_END:C_END]) * _dot(xb, w_ref[:, C_END:IN_WIDTH])
    return q_out, k, v, bg, u


def _conv3(bg, u, u1, u2, cw_ref):
    cw = cw_ref[...]
    return bg * (cw[0:1, :] * u2 + cw[1:2, :] * u1 + cw[2:3, :] * u)


def _proj_prompt_kernel(x_ref, w_ref, cos_ref, sin_ref, cw_ref,
                        q_ref, kx_ref, vx_ref, conv_ref, ktail_ref, vtail_ref, utail_ref, carry_ref):
    @pl.when(pl.program_id(0) == 0)
    def _():
        carry_ref[...] = jnp.zeros_like(carry_ref)

    q_out, k, v, bg, u = _proj_common(x_ref, w_ref, cos_ref, sin_ref)
    tm = u.shape[0]
    ext = jnp.concatenate([carry_ref[...], u], axis=0)
    u1 = pltpu.roll(ext, 1, axis=0)[SUBLANES:SUBLANES + tm]
    u2 = pltpu.roll(ext, 2, axis=0)[SUBLANES:SUBLANES + tm]
    q_ref[...] = q_out
    kx_ref[...] = _head_slabs(k)
    vx_ref[...] = _head_slabs(v)
    conv_ref[...] = _bf16(_conv3(bg, u, u1, u2, cw_ref))
    ktail_ref[...] = k[tm - WINDOW:tm]
    vtail_ref[...] = v[tm - WINDOW:tm]
    utail_ref[...] = u[tm - SUBLANES:tm]
    carry_ref[...] = u[tm - SUBLANES:tm]


def _proj_sample_kernel(x_ref, w_ref, cos_ref, sin_ref, cw_ref, c0_ref, c1_ref,
                        q_ref, k_ref, v_ref, conv_ref, u_ref):
    q_out, k, v, bg, u = _proj_common(x_ref, w_ref, cos_ref, sin_ref)
    t = lax.broadcasted_iota(jnp.int32, u.shape, 0) % DEC_SEQ
    c0 = c0_ref[...]
    c1 = c1_ref[...]
    u1 = jnp.where(t >= 1, pltpu.roll(u, 1, axis=0), c1)
    u2 = jnp.where(t >= 2, pltpu.roll(u, 2, axis=0), jnp.where(t == 1, c1, c0))
    q_ref[...] = q_out.astype(jnp.float32)
    k_ref[...] = k
    v_ref[...] = v
    conv_ref[...] = _bf16(_conv3(bg, u, u1, u2, cw_ref))
    u_ref[...] = u


def _rope_tables(pos):
    half = ROPE_DIM // 2
    inv = ROPE_THETA ** (-jnp.arange(0, ROPE_DIM, 2, dtype=jnp.float32) / ROPE_DIM)
    zeros = jnp.zeros((HEAD_DIM - ROPE_DIM,), jnp.float32)
    inv_h = jnp.concatenate([inv, inv, zeros])
    sign_h = jnp.concatenate([-jnp.ones((half,), jnp.float32), jnp.ones((half,), jnp.float32), zeros])
    reps = LANES // HEAD_DIM
    ang = pos.astype(jnp.float32)[:, None] * jnp.tile(inv_h, reps)[None, :]
    return jnp.cos(ang), jnp.sin(ang) * jnp.tile(sign_h, reps)[None, :]


def _proj_prompt(x, w_in_b, cos_t, sin_t, conv_w):
    n = x.shape[0]
    tm = TM_PROJ
    row = lambda w: pl.BlockSpec((tm, w), lambda i: (i, 0))
    full = lambda a: pl.BlockSpec(a.shape, lambda i: (0,) * a.ndim)
    const = lambda r, w: pl.BlockSpec((r, w), lambda i: (0, 0))
    return pl.pallas_call(
        _proj_prompt_kernel,
        grid=(n // tm,),
        in_specs=[row(D_MODEL), full(w_in_b), row(LANES), row(LANES), full(conv_w)],
        out_specs=[row(ATTN_WIDTH), row(4 * LANES), row(4 * LANES), row(CONV_CH),
                   const(WINDOW, KV_WIDTH), const(WINDOW, KV_WIDTH), const(SUBLANES, CONV_CH)],
        out_shape=[jax.ShapeDtypeStruct((n, ATTN_WIDTH), jnp.bfloat16),
                   jax.ShapeDtypeStruct((n, 4 * LANES), jnp.bfloat16),
                   jax.ShapeDtypeStruct((n, 4 * LANES), jnp.bfloat16),
                   jax.ShapeDtypeStruct((n, CONV_CH), jnp.bfloat16),
                   jax.ShapeDtypeStruct((WINDOW, KV_WIDTH), jnp.float32),
                   jax.ShapeDtypeStruct((WINDOW, KV_WIDTH), jnp.float32),
                   jax.ShapeDtypeStruct((SUBLANES, CONV_CH), jnp.float32)],
        scratch_shapes=[pltpu.VMEM((SUBLANES, CONV_CH), jnp.float32)],
        compiler_params=_params(("arbitrary",)),
        name="proj_prompt",
    )(x, w_in_b, cos_t, sin_t, conv_w)


def _proj_sample(x, w_in_b, cos_t, sin_t, conv_w, c0, c1):
    n = x.shape[0]
    full = lambda a: pl.BlockSpec(a.shape, lambda i: (0,) * a.ndim)
    out = lambda w, dt: jax.ShapeDtypeStruct((n, w), dt)
    blk = lambda w: pl.BlockSpec((n, w), lambda i: (0, 0))
    return pl.pallas_call(
        _proj_sample_kernel,
        grid=(1,),
        in_specs=[full(x), full(w_in_b), full(cos_t), full(sin_t), full(conv_w), full(c0), full(c1)],
        out_specs=[blk(ATTN_WIDTH), blk(KV_WIDTH), blk(KV_WIDTH), blk(CONV_CH), blk(CONV_CH)],
        out_shape=[out(ATTN_WIDTH, jnp.float32), out(KV_WIDTH, jnp.float32), out(KV_WIDTH, jnp.float32),
                   out(CONV_CH, jnp.bfloat16), out(CONV_CH, jnp.float32)],
        compiler_params=_params(("arbitrary",)),
        name="proj_sample",
    )(x, w_in_b, cos_t, sin_t, conv_w, c0, c1)


def _sink_softmax_pv(s, valid, sink, vx):
    s = jnp.where(valid, s, -jnp.inf)
    m = jnp.maximum(jnp.max(s, axis=1, keepdims=True), sink)
    p = jnp.exp(s - m)
    den = jnp.sum(p, axis=1, keepdims=True) + jnp.exp(sink - m)
    return _dot(_bf16(p), vx) / den


SWA_QB = 2


def _swa_prompt_kernel(sinks_ref, q_ref, kc_ref, kp_ref, vc_ref, vp_ref, o_ref):
    step = pl.program_id(0)
    kall = jnp.concatenate([kp_ref[...], kc_ref[...]], axis=0)
    vall = jnp.concatenate([vp_ref[...], vc_ref[...]], axis=0)
    i = lax.broadcasted_iota(jnp.int32, (WINDOW, 2 * WINDOW), 0)
    j = lax.broadcasted_iota(jnp.int32, (WINDOW, 2 * WINDOW), 1)
    band = (j > i) & (j <= i + WINDOW)
    for sb in range(SWA_QB):
        rows = slice(sb * WINDOW, (sb + 1) * WINDOW)
        kcat = kall[sb * WINDOW:(sb + 2) * WINDOW]
        vcat = vall[sb * WINDOW:(sb + 2) * WINDOW]
        valid = band & ((step > 0) | (j >= WINDOW)) if sb == 0 else band
        for p in range(N_HEADS // 2):
            qs = q_ref[rows, p * LANES:(p + 1) * LANES]
            acc = None
            for e in range(2):
                hd = 2 * p + e
                slab = 2 * (hd // (N_HEADS // N_KV_HEADS)) + e
                kx = kcat[:, slab * LANES:(slab + 1) * LANES]
                vx = vcat[:, slab * LANES:(slab + 1) * LANES]
                o = _sink_softmax_pv(_dot_nt(qs, kx), valid, sinks_ref[hd], vx)
                acc = o if acc is None else acc + o
            o_ref[rows, p * LANES:(p + 1) * LANES] = _bf16(acc)


def _swa_prompt(sinks, q, kx, vx):
    n = q.shape[0]
    nb = n // (SWA_QB * WINDOW)
    cur = lambda w: pl.BlockSpec((SWA_QB * WINDOW, w), lambda i: (i, 0))
    prev = lambda w: pl.BlockSpec((WINDOW, w), lambda i: (jnp.maximum(SWA_QB * i - 1, 0), 0))
    return pl.pallas_call(
        _swa_prompt_kernel,
        grid=(nb,),
        in_specs=[pl.BlockSpec(memory_space=pltpu.SMEM), cur(ATTN_WIDTH),
                  cur(4 * LANES), prev(4 * LANES), cur(4 * LANES), prev(4 * LANES)],
        out_specs=cur(ATTN_WIDTH),
        out_shape=jax.ShapeDtypeStruct((n, ATTN_WIDTH), jnp.bfloat16),
        compiler_params=_params(("arbitrary",)),
        name="swa_prompt",
    )(sinks, q, kx, kx, vx, vx)


SWA_ROWS = N_HEADS * DEC_SEQ
NEW_ROWS = 2 * SUBLANES


def _swa_sample_kernel(q_ref, sink_ref, kn_ref, vn_ref, kt_ref, vt_ref, o_ref, okt_ref, ovt_ref):
    t = lax.broadcasted_iota(jnp.int32, (SWA_ROWS, WINDOW), 0) % DEC_SEQ
    j = lax.broadcasted_iota(jnp.int32, (SWA_ROWS, WINDOW), 1)
    valid_c = j > t
    valid_n = (lax.broadcasted_iota(jnp.int32, (SWA_ROWS, NEW_ROWS), 1)
               <= lax.broadcasted_iota(jnp.int32, (SWA_ROWS, NEW_ROWS), 0) % DEC_SEQ)
    lane = lax.broadcasted_iota(jnp.int32, (KV_WIDTH, WINDOW), 1)
    sink = sink_ref[:, 0:1]
    shift = WINDOW - DEC_SEQ
    zrows = jnp.zeros((KV_WIDTH - NEW_ROWS, KV_WIDTH), jnp.float32)
    for b in range(SAMPLE_BB):
        q = _bf16(q_ref[b])
        kt, vt = kt_ref[b], vt_ref[b]
        kn, vn = kn_ref[b], vn_ref[b]
        s_c = jnp.where(valid_c, _dot(q, _bf16(kt)), -jnp.inf)
        s_n = jnp.where(valid_n, _dot_nt(q, _bf16(kn)), -jnp.inf)
        m = jnp.maximum(jnp.maximum(jnp.max(s_c, axis=1, keepdims=True), jnp.max(s_n, axis=1, keepdims=True)), sink)
        p_c = jnp.exp(s_c - m)
        p_n = jnp.exp(s_n - m)
        den = jnp.sum(p_c, axis=1, keepdims=True) + jnp.sum(p_n, axis=1, keepdims=True) + jnp.exp(sink - m)
        o_ref[b] = (_dot_nt(_bf16(p_c), _bf16(vt)) + _dot(_bf16(p_n), _bf16(vn))) / den
        for old, new, dst in ((kt, kn, okt_ref), (vt, vn, ovt_ref)):
            new_cols = pltpu.roll(jnp.concatenate([new, zrows], axis=0).T, shift, axis=1)
            dst[b] = jnp.where(lane >= shift, new_cols, pltpu.roll(old, shift, axis=1))


def _swa_sample(sinks, q, kn, vn, cache_k, cache_v):
    nb = cache_k.shape[0]
    bb = SAMPLE_BB
    groups = N_HEADS // N_KV_HEADS
    qh = q.reshape(nb, DEC_SEQ, N_KV_HEADS, groups, HEAD_DIM).transpose(0, 2, 3, 1, 4)
    qh = qh.reshape(nb, N_KV_HEADS, groups * DEC_SEQ, HEAD_DIM)
    zeros = jnp.zeros_like(qh[:, 0])
    qbd = jnp.concatenate([jnp.concatenate([qh[:, 0], zeros], axis=-1),
                           jnp.concatenate([zeros, qh[:, 1]], axis=-1)], axis=1)
    sink_col = jnp.broadcast_to(jnp.repeat(sinks, DEC_SEQ).reshape(SWA_ROWS, 1), (SWA_ROWS, LANES))
    pad8 = lambda a: jnp.pad(a.reshape(nb, DEC_SEQ, KV_WIDTH), ((0, 0), (0, NEW_ROWS - DEC_SEQ), (0, 0)))
    to_t = lambda c: c.transpose(0, 2, 3, 1).reshape(nb, KV_WIDTH, WINDOW)
    blk = lambda r, w: pl.BlockSpec((bb, r, w), lambda i: (i, 0, 0))
    o, okt, ovt = pl.pallas_call(
        _swa_sample_kernel,
        grid=(nb // bb,),
        in_specs=[blk(SWA_ROWS, KV_WIDTH), pl.BlockSpec((SWA_ROWS, LANES), lambda i: (0, 0)),
                  blk(NEW_ROWS, KV_WIDTH), blk(NEW_ROWS, KV_WIDTH), blk(KV_WIDTH, WINDOW), blk(KV_WIDTH, WINDOW)],
        out_specs=[blk(SWA_ROWS, KV_WIDTH), blk(KV_WIDTH, WINDOW), blk(KV_WIDTH, WINDOW)],
        out_shape=[jax.ShapeDtypeStruct((nb, SWA_ROWS, KV_WIDTH), jnp.float32),
                   jax.ShapeDtypeStruct((nb, KV_WIDTH, WINDOW), jnp.float32),
                   jax.ShapeDtypeStruct((nb, KV_WIDTH, WINDOW), jnp.float32)],
        compiler_params=_params(("arbitrary",)),
        name="swa_sample",
    )(qbd, sink_col, pad8(kn), pad8(vn), to_t(cache_k), to_t(cache_v))
    o = o.reshape(nb, N_KV_HEADS, groups, DEC_SEQ, N_KV_HEADS, HEAD_DIM)
    attn = jnp.stack([o[:, h, :, :, h, :] for h in range(N_KV_HEADS)], axis=1)
    attn = attn.transpose(0, 3, 1, 2, 4).reshape(nb * DEC_SEQ, ATTN_WIDTH)
    from_t = lambda c: c.reshape(nb, N_KV_HEADS, HEAD_DIM, WINDOW).transpose(0, 3, 1, 2)
    return attn, from_t(okt), from_t(ovt)


def _mem_kv_kernel(mem_ref, wk_ref, wv_ref, mk_ref, mv_ref, mkb_ref, mvb_ref):
    mb = _bf16(mem_ref[...])
    mk = _dot(mb, wk_ref[...])
    mv = _dot(mb, wv_ref[...])
    mk_ref[...] = mk
    mv_ref[...] = mv
    mkb_ref[...] = _bf16(mk)
    mvb_ref[...] = _bf16(mv)


def _mem_kv(mem, wk_b, wv_b):
    full = lambda a: pl.BlockSpec(a.shape, lambda i: (0,) * a.ndim)
    blk = pl.BlockSpec((N_MEM, D_MODEL), lambda i: (0, 0))
    f32 = jax.ShapeDtypeStruct((N_MEM, D_MODEL), jnp.float32)
    b16 = jax.ShapeDtypeStruct((N_MEM, D_MODEL), jnp.bfloat16)
    return pl.pallas_call(
        _mem_kv_kernel,
        grid=(1,),
        in_specs=[full(mem), full(wk_b), full(wv_b)],
        out_specs=[blk, blk, blk, blk],
        out_shape=[f32, f32, b16, b16],
        compiler_params=_params(("arbitrary",)),
        name="mem_kv",
    )(mem, wk_b, wv_b)


def _mix_ln1(attn_ref, conv_ref, x_ref, wmix_ref, g1_ref, b1_ref):
    mix = _dot(_bf16(attn_ref[...]), wmix_ref[0:ATTN_WIDTH, :]) + _dot(conv_ref[...], wmix_ref[ATTN_WIDTH:, :])
    return _layer_norm(ALPHA * x_ref[...] + mix, g1_ref[...], b1_ref[...])


def _mem_q(h1, wq_ref):
    return _bf16(_dot(_bf16(h1), wq_ref[...]) * (MEM_HEAD_DIM ** -0.5))


def _route(h2, wrh_ref, wrl_ref, br_ref, tri_ref, carry):
    hi = _bf16(h2)
    lo = _bf16(h2 - hi.astype(jnp.float32))
    logits = _dot(hi, wrh_ref[...]) + _dot(lo, wrh_ref[...]) + _dot(hi, wrl_ref[...]) + br_ref[...]
    lane_i = lax.broadcasted_iota(jnp.int32, logits.shape, 1)
    lane = lane_i.astype(jnp.float32)
    big = jnp.float32(LANES)
    is_g = lane_i < N_GROUPS
    gl = jnp.where(is_g, logits, -jnp.inf)
    gmax = jnp.max(gl, axis=1, keepdims=True)
    gidx = jnp.min(jnp.where(is_g & (logits == gmax), lane, big), axis=1, keepdims=True)
    gsum = jnp.sum(jnp.exp(gl - gmax), axis=1, keepdims=True)
    gw = 1.0 / gsum
    eid = lane_i - N_GROUPS
    assert EXPERTS_PER_GROUP == 8
    grp = lax.shift_right_arithmetic(eid, jnp.full_like(eid, 3)).astype(jnp.float32)
    in_e = (lane_i >= N_GROUPS) & (lane_i < N_GROUPS + N_EXPERTS) & (grp == gidx)
    v1 = jnp.max(jnp.where(in_e, logits, -jnp.inf), axis=1, keepdims=True)
    i1 = jnp.min(jnp.where(in_e & (logits == v1), lane, big), axis=1, keepdims=True)
    rest = in_e & (lane != i1)
    v2 = jnp.max(jnp.where(rest, logits, -jnp.inf), axis=1, keepdims=True)
    i2 = jnp.min(jnp.where(rest & (logits == v2), lane, big), axis=1, keepdims=True)
    ex = jnp.exp(v2 - v1)
    den = 1.0 + ex
    w1 = gw / den
    w2 = gw * ex / den
    zero = jnp.zeros_like(logits)
    pick1 = lane == i1
    pick2 = lane == i2
    sel = jnp.where(pick1 | pick2, 1.0, 0.0)
    before = _dot(tri_ref[...], _bf16(sel)) + carry
    rank1 = jnp.sum(jnp.where(pick1, before, zero), axis=1, keepdims=True)
    rank2 = jnp.sum(jnp.where(pick2, before, zero), axis=1, keepdims=True)
    cols = (i1 - N_GROUPS, i2 - N_GROUPS, w1, w2, rank1, rank2)
    route = zero
    for k, col in enumerate(cols):
        route = jnp.where(lane_i == k, col, route)
    return route, carry + jnp.sum(sel, axis=0, keepdims=True)


def _post_prompt_kernel(attn_ref, conv_ref, x_ref, wmix_ref, g1_ref, b1_ref, wq_ref, mk_ref, mv_ref, wo_ref,
                        g2_ref, b2_ref, wrh_ref, wrl_ref, br_ref, tri_ref, h2s_ref, rts_ref, cnts_ref,
                        h2t_ref, rt_ref, cnt_ref, carry_ref):
    is_tail = pl.program_id(0) == N_PROMPT // TM_POST

    @pl.when(pl.program_id(0) == 0)
    def _():
        carry_ref[...] = cnts_ref[...]

    @pl.when(is_tail)
    def _():
        h2t_ref[...] = h2s_ref[...]
        rt_ref[...] = rts_ref[...]

    @pl.when(jnp.logical_not(is_tail))
    def _():
        h1 = _mix_ln1(attn_ref, conv_ref, x_ref, wmix_ref, g1_ref, b1_ref)
        qm = _mem_q(h1, wq_ref)
        outs = []
        for h in range(MEM_HEADS):
            sl = slice(h * MEM_HEAD_DIM, (h + 1) * MEM_HEAD_DIM)
            s = _dot_nt(qm[:, sl], mk_ref[:, sl])
            m = jnp.max(s, axis=1, keepdims=True)
            p = jnp.exp(s - m)
            den = jnp.sum(p, axis=1, keepdims=True)
            outs.append(_dot(_bf16(p), mv_ref[:, sl]) / den)
        o = _bf16(jnp.concatenate(outs, axis=1))
        h2 = _layer_norm(ALPHA * h1 + _dot(o, wo_ref[...]), g2_ref[...], b2_ref[...])
        _store_token_tiles(h2t_ref, h2)
        route, carry = _route(h2, wrh_ref, wrl_ref, br_ref, tri_ref, carry_ref[0:1, :])
        rt_ref[...] = route
        carry_ref[...] = jnp.broadcast_to(carry, carry_ref.shape)
        cnt_ref[...] = jnp.broadcast_to(carry, cnt_ref.shape)


def _post_prompt(attn, conv, x, wmix_b, g1, b1, wq_b, mk_b, mv_b, wo_b, g2, b2, wrh, wrl, br, tri, h2t_s, rt_s, cnt_s):
    n = x.shape[0]
    tm = TM_POST
    steps = n // tm
    row = lambda w: pl.BlockSpec((tm, w), lambda i: (jnp.minimum(i, steps - 1), 0))
    full = lambda a: pl.BlockSpec(a.shape, lambda i: (0,) * a.ndim)
    weights = (wmix_b, g1, b1, wq_b, mk_b, mv_b, wo_b, g2, b2, wrh, wrl, br, tri, h2t_s, rt_s, cnt_s)
    n_out = n + h2t_s.shape[0] // ROW_CHUNKS
    return pl.pallas_call(
        _post_prompt_kernel,
        grid=(steps + 1,),
        in_specs=[row(ATTN_WIDTH), row(CONV_CH), row(D_MODEL)] + [full(a) for a in weights],
        out_specs=[pl.BlockSpec((tm * ROW_CHUNKS, LANES), lambda i: (i, 0)),
                   pl.BlockSpec((tm, LANES), lambda i: (i, 0)),
                   pl.BlockSpec((SUBLANES, LANES), lambda i: (0, 0))],
        out_shape=[jax.ShapeDtypeStruct((n_out * ROW_CHUNKS, LANES), jnp.float32),
                   jax.ShapeDtypeStruct((n_out, LANES), jnp.float32),
                   jax.ShapeDtypeStruct((SUBLANES, LANES), jnp.float32)],
        scratch_shapes=[pltpu.VMEM((SUBLANES, LANES), jnp.float32)],
        compiler_params=_params(("arbitrary",)),
        name="post_prompt",
    )(attn, conv, x, *weights)


def _post_a_sample_kernel(attn_ref, conv_ref, x_ref, wmix_ref, g1_ref, b1_ref, wq_ref, h1_ref, qm_ref):
    h1 = _mix_ln1(attn_ref, conv_ref, x_ref, wmix_ref, g1_ref, b1_ref)
    h1_ref[...] = h1
    qm_ref[...] = _mem_q(h1, wq_ref).astype(jnp.float32)


def _post_a_sample(attn, conv, x, wmix_b, g1, b1, wq_b):
    n = x.shape[0]
    args = (attn, conv, x, wmix_b, g1, b1, wq_b)
    full = lambda a: pl.BlockSpec(a.shape, lambda i: (0,) * a.ndim)
    blk = pl.BlockSpec((n, D_MODEL), lambda i: (0, 0))
    return pl.pallas_call(
        _post_a_sample_kernel,
        grid=(1,),
        in_specs=[full(a) for a in args],
        out_specs=[blk, blk],
        out_shape=[jax.ShapeDtypeStruct((n, D_MODEL), jnp.float32),
                   jax.ShapeDtypeStruct((n, D_MODEL), jnp.float32)],
        compiler_params=_params(("arbitrary",)),
        name="post_a_sample",
    )(*args)


MEM_ROWS = MEM_HEADS * DEC_SEQ


def _mem_attn_sample_kernel(q_ref, mk_ref, mv_ref, o_ref):
    nk = N_MEM * MEM_HEADS
    row_h = lax.broadcasted_iota(jnp.int32, (MEM_ROWS, nk), 0) // DEC_SEQ
    key_h = lax.broadcasted_iota(jnp.int32, (MEM_ROWS, nk), 1) % MEM_HEADS
    own = row_h == key_h
    for b in range(SAMPLE_BB):
        k2 = _bf16(mk_ref[b].reshape(nk, MEM_HEAD_DIM))
        v2 = _bf16(mv_ref[b].reshape(nk, MEM_HEAD_DIM))
        s = jnp.where(own, _dot_nt(_bf16(q_ref[b]), k2), -jnp.inf)
        m = jnp.max(s, axis=1, keepdims=True)
        p = jnp.exp(s - m)
        den = jnp.sum(p, axis=1, keepdims=True)
        o_ref[b] = _dot(_bf16(p), v2) / den


def _mem_attn_sample(qm, mk, mv):
    nb = mk.shape[0]
    bb = SAMPLE_BB
    q = qm.reshape(nb, DEC_SEQ, MEM_HEADS, MEM_HEAD_DIM).transpose(0, 2, 1, 3).reshape(nb, MEM_ROWS, MEM_HEAD_DIM)
    rows = pl.BlockSpec((bb, MEM_ROWS, MEM_HEAD_DIM), lambda i: (i, 0, 0))
    kv = pl.BlockSpec((bb, N_MEM, MEM_HEADS, MEM_HEAD_DIM), lambda i: (i, 0, 0, 0))
    o = pl.pallas_call(
        _mem_attn_sample_kernel,
        grid=(nb // bb,),
        in_specs=[rows, kv, kv],
        out_specs=rows,
        out_shape=jax.ShapeDtypeStruct((nb, MEM_ROWS, MEM_HEAD_DIM), jnp.float32),
        compiler_params=_params(("arbitrary",)),
        name="mem_attn_sample",
    )(q, mk, mv)
    return o.reshape(nb, MEM_HEADS, DEC_SEQ, MEM_HEAD_DIM).transpose(0, 2, 1, 3).reshape(nb * DEC_SEQ, D_MODEL)


def _post_b_sample_kernel(o_ref, h1_ref, wo_ref, g2_ref, b2_ref, wrh_ref, wrl_ref, br_ref, tri_ref,
                          h2t_ref, rt_ref, cnt_ref):
    h2 = _layer_norm(ALPHA * h1_ref[...] + _dot(_bf16(o_ref[...]), wo_ref[...]), g2_ref[...], b2_ref[...])
    _store_token_tiles(h2t_ref, h2)
    route, carry = _route(h2, wrh_ref, wrl_ref, br_ref, tri_ref, jnp.zeros((1, LANES), jnp.float32))
    rt_ref[...] = route
    cnt_ref[...] = jnp.broadcast_to(carry, cnt_ref.shape)


def _post_b_sample(o, h1, wo_b, g2, b2, wrh, wrl, br, tri):
    n = h1.shape[0]
    args = (o, h1, wo_b, g2, b2, wrh, wrl, br, tri)
    full = lambda a: pl.BlockSpec(a.shape, lambda i: (0,) * a.ndim)
    return pl.pallas_call(
        _post_b_sample_kernel,
        grid=(1,),
        in_specs=[full(a) for a in args],
        out_specs=[pl.BlockSpec((n * ROW_CHUNKS, LANES), lambda i: (0, 0)),
                   pl.BlockSpec((n, LANES), lambda i: (0, 0)),
                   pl.BlockSpec((SUBLANES, LANES), lambda i: (0, 0))],
        out_shape=[jax.ShapeDtypeStruct((n * ROW_CHUNKS, LANES), jnp.float32),
                   jax.ShapeDtypeStruct((n, LANES), jnp.float32),
                   jax.ShapeDtypeStruct((SUBLANES, LANES), jnp.float32)],
        compiler_params=_params(("arbitrary",)),
        name="post_b_sample",
    )(*args)


def _row_gather_copy(src_hbm, idx, dst, dst_row, sem):
    s0 = pl.multiple_of(idx * ROW_CHUNKS, ROW_CHUNKS)
    d0 = pl.multiple_of(dst_row * ROW_CHUNKS, ROW_CHUNKS)
    return pltpu.make_async_copy(src_hbm.at[pl.ds(s0, ROW_CHUNKS), :], dst.at[pl.ds(d0, ROW_CHUNKS), :], sem)


def _dispatch_kernel(pos_ref, h2t_ref, xs_hbm, sem):
    def body(r, c):
        src = h2t_ref.at[pl.ds(pl.multiple_of(r * ROW_CHUNKS, ROW_CHUNKS), ROW_CHUNKS), :]
        for k in range(2):
            d0 = pl.multiple_of(pos_ref[0, 0, k * TM_COMB + r] * ROW_CHUNKS, ROW_CHUNKS)
            pltpu.make_async_copy(src, xs_hbm.at[pl.ds(d0, ROW_CHUNKS), :], sem.at[0]).start(priority=k)
        return c
    lax.fori_loop(0, TM_COMB, body, 0, unroll=8)
    for _ in range(2):
        pltpu.make_async_copy(h2t_ref, xs_hbm.at[pl.ds(0, TM_COMB * ROW_CHUNKS), :], sem.at[0]).wait()


def _dispatch(pos3, h2t):
    nt = N_ALL // TM_COMB
    return pl.pallas_call(
        _dispatch_kernel,
        grid=(nt,),
        in_specs=[pl.BlockSpec((1, 1, 2 * TM_COMB), lambda i: (i, 0, 0), memory_space=pltpu.SMEM),
                  pl.BlockSpec((TM_COMB * ROW_CHUNKS, LANES), lambda i: (i, 0))],
        out_specs=pl.BlockSpec(memory_space=pl.ANY),
        out_shape=jax.ShapeDtypeStruct((N_ASSIGN * ROW_CHUNKS, LANES), jnp.float32),
        scratch_shapes=[pltpu.SemaphoreType.DMA((1,))],
        compiler_params=_params(("arbitrary",)),
        name="moe_dispatch",
    )(pos3, h2t)


def _moe_ffn_kernel(it_ref, ie_ref, lo_ref, hi_ref, x_ref, wg_ref, wu_ref, wd_ref, y_ref, wgb, wub, wdb, cur_e):
    i = pl.program_id(0)
    lo = lo_ref[i]
    hi = hi_ref[i]
    e = ie_ref[i]

    @pl.when(i == 0)
    def _():
        cur_e[0] = -1

    @pl.when((hi > lo) & (cur_e[0] != e))
    def _():
        wgb[...] = _bf16(wg_ref[0])
        wub[...] = _bf16(wu_ref[0])
        wdb[...] = _bf16(wd_ref[0])
        cur_e[0] = e

    def ffn():
        x = _bf16(_load_token_tiles(x_ref, 0, TM_MOE))
        hg = _dot(x, wgb[...])
        hu = _dot(x, wub[...])
        h = hg / (1.0 + jnp.exp(-hg)) * hu
        return _dot(_bf16(h), wdb[...])

    def rows_mask():
        row = lax.broadcasted_iota(jnp.int32, (TM_MOE, LANES), 0)
        return (row >= lo) & (row < hi)

    @pl.when((hi > lo) & (lo == 0))
    def _():
        y = ffn()
        mask = rows_mask()
        for c in range(ROW_CHUNKS):
            y_ref[pl.ds(c, TM_MOE, stride=ROW_CHUNKS), :] = jnp.where(mask, y[:, c * LANES:(c + 1) * LANES], 0.0)

    @pl.when((hi > lo) & (lo > 0))
    def _():
        y = ffn()
        mask = rows_mask()
        for c in range(ROW_CHUNKS):
            sl = pl.ds(c, TM_MOE, stride=ROW_CHUNKS)
            y_ref[sl, :] = jnp.where(mask, y[:, c * LANES:(c + 1) * LANES], y_ref[sl, :])


def _moe_ffn(item_tile, item_expert, item_lo, item_hi, x_sorted, w_gate, w_up, w_down):
    wspec = lambda shp: pl.BlockSpec((1,) + shp, lambda i, it, ie, lo, hi: (ie[i], 0, 0))
    tile = pl.BlockSpec((TM_MOE * ROW_CHUNKS, LANES), lambda i, it, ie, lo, hi: (it[i], 0))
    grid_spec = pltpu.PrefetchScalarGridSpec(
        num_scalar_prefetch=4,
        grid=(MOE_ITEMS,),
        in_specs=[tile, wspec((D_MODEL, EXPERT_FF)), wspec((D_MODEL, EXPERT_FF)), wspec((EXPERT_FF, D_MODEL))],
        out_specs=tile,
        scratch_shapes=[pltpu.VMEM((D_MODEL, EXPERT_FF), jnp.bfloat16),
                        pltpu.VMEM((D_MODEL, EXPERT_FF), jnp.bfloat16),
                        pltpu.VMEM((EXPERT_FF, D_MODEL), jnp.bfloat16),
                        pltpu.SMEM((1,), jnp.int32)],
    )
    return pl.pallas_call(
        _moe_ffn_kernel,
        grid_spec=grid_spec,
        out_shape=jax.ShapeDtypeStruct((N_ASSIGN * ROW_CHUNKS, LANES), jnp.float32),
        compiler_params=_params(("arbitrary",)),
        name="moe_ffn",
    )(item_tile, item_expert, item_lo, item_hi, x_sorted, w_gate, w_up, w_down)


def _combine_kernel(nt, pos_cur_ref, pos_nxt_ref, yt_hbm, h2t_ref, rt_ref, g3_ref, b3_ref, o_ref, abuf, sem):
    t = pl.program_id(0)
    slot = t % 2
    rows = 2 * TM_COMB

    def issue(pos_ref, s):
        def body(j, c):
            for k in range(2):
                r = 2 * j + k
                _row_gather_copy(yt_hbm, pos_ref[0, 0, r], abuf, s * rows + r, sem.at[s]).start(priority=k)
            return c
        lax.fori_loop(0, rows // 2, body, 0, unroll=4)

    @pl.when(t == 0)
    def _():
        issue(pos_cur_ref, 0)

    @pl.when(t + 1 < nt)
    def _():
        issue(pos_nxt_ref, 1 - slot)

    base = pl.multiple_of(slot * (rows * ROW_CHUNKS), rows * ROW_CHUNKS)
    pltpu.make_async_copy(yt_hbm.at[pl.ds(0, rows * ROW_CHUNKS), :],
                          abuf.at[pl.ds(base, rows * ROW_CHUNKS), :], sem.at[slot]).wait()
    ya = _load_token_tiles(abuf, base, TM_COMB)
    yb = _load_token_tiles(abuf, base + TM_COMB * ROW_CHUNKS, TM_COMB)
    rt = rt_ref[...]
    ff = rt[:, 2:3] * ya + rt[:, 3:4] * yb
    h2 = _load_token_tiles(h2t_ref, 0, TM_COMB)
    o_ref[...] = _layer_norm(ALPHA * h2 + ff, g3_ref[...], b3_ref[...])


def _combine(pos3, yt, h2t, rt, g3, b3, tile0, n_tiles):
    last = tile0 + n_tiles - 1
    smem_pos = lambda f: pl.BlockSpec((1, 1, 2 * TM_COMB), f, memory_space=pltpu.SMEM)
    full = lambda a: pl.BlockSpec(a.shape, lambda i: (0,) * a.ndim)
    return pl.pallas_call(
        functools.partial(_combine_kernel, n_tiles),
        grid=(n_tiles,),
        in_specs=[smem_pos(lambda i: (tile0 + i, 0, 0)),
                  smem_pos(lambda i: (jnp.minimum(tile0 + i + 1, last), 0, 0)),
                  pl.BlockSpec(memory_space=pl.ANY),
                  pl.BlockSpec((TM_COMB * ROW_CHUNKS, LANES), lambda i: (tile0 + i, 0)),
                  pl.BlockSpec((TM_COMB, LANES), lambda i: (tile0 + i, 0)),
                  full(g3), full(b3)],
        out_specs=pl.BlockSpec((TM_COMB, D_MODEL), lambda i: (i, 0)),
        out_shape=jax.ShapeDtypeStruct((n_tiles * TM_COMB, D_MODEL), jnp.float32),
        scratch_shapes=[pltpu.VMEM((2 * 2 * TM_COMB * ROW_CHUNKS, LANES), jnp.float32),
                        pltpu.SemaphoreType.DMA((2,))],
        compiler_params=_params(("arbitrary",)),
        name="moe_combine",
    )(pos3, pos3, yt, h2t, rt, g3, b3)


def _positions_kernel(rt_ref, starts_ref, pos_ref):
    rt = rt_ref[...]
    lane = lax.broadcasted_iota(jnp.int32, rt.shape, 1)
    lane_f = lane.astype(jnp.float32)
    starts = starts_ref[0:1, :]
    cols = []
    for k in range(2):
        seg = jnp.sum(jnp.where(lane_f == rt[:, k:k + 1] + N_GROUPS, starts, 0.0), axis=1, keepdims=True)
        cols.append(seg + rt[:, 4 + k:5 + k])
    packed = jnp.where(lane == 0, cols[0], jnp.where(lane == 1, cols[1], 0.0))
    rows = packed.T
    pos_ref[0] = jnp.concatenate([rows[0:1, :], rows[1:2, :]], axis=1).astype(jnp.int32)


def _positions(rt, starts_row):
    nt = N_ALL // TM_COMB
    return pl.pallas_call(
        _positions_kernel,
        grid=(nt,),
        in_specs=[pl.BlockSpec((TM_COMB, LANES), lambda i: (i, 0)),
                  pl.BlockSpec((SUBLANES, LANES), lambda i: (0, 0))],
        out_specs=pl.BlockSpec((1, 1, 2 * TM_COMB), lambda i: (i, 0, 0)),
        out_shape=jax.ShapeDtypeStruct((nt, 1, 2 * TM_COMB), jnp.int32),
        compiler_params=_params(("arbitrary",)),
        name="moe_positions",
    )(rt, starts_row)


def _routing_plan(rt, cnt):
    i32 = jnp.int32
    counts_f = cnt[0, N_GROUPS:N_GROUPS + N_EXPERTS]
    starts_f = jnp.cumsum(counts_f) - counts_f
    starts_row = jnp.broadcast_to(
        jnp.pad(starts_f, (N_GROUPS, LANES - N_GROUPS - N_EXPERTS))[None, :], (SUBLANES, LANES))
    pos3 = _positions(rt, starts_row)
    starts = starts_f.astype(i32)
    tiles = jnp.arange(MOE_TILES, dtype=i32) * TM_MOE
    rank_t = jnp.arange(MOE_TILES, dtype=i32) + jnp.sum((starts[None, :] < tiles[:, None]).astype(i32), axis=1)
    rank_s = jnp.arange(N_EXPERTS, dtype=i32) + jnp.sum((tiles[None, :] <= starts[:, None]).astype(i32), axis=1)
    vals = jnp.concatenate([tiles, starts])
    ranks = jnp.concatenate([rank_t, rank_s])
    slot = jnp.arange(MOE_ITEMS, dtype=i32)
    lo = jnp.sum(jnp.where(ranks[None, :] == slot[:, None], vals[None, :], 0), axis=1)
    hi = jnp.concatenate([lo[1:], jnp.full((1,), N_ASSIGN, i32)])
    item_tile = jnp.minimum(lo // TM_MOE, MOE_TILES - 1)
    item_expert = jnp.clip(jnp.sum((starts[None, :] <= lo[:, None]).astype(i32), axis=1) - 1, 0, N_EXPERTS - 1)
    base = item_tile * TM_MOE
    return item_tile, item_expert, lo - base, hi - base, pos3


def kernel(x_prompt, x_sample, mem_prompt, cache_swa_k, cache_swa_v, cache_conv, cache_mem_k, cache_mem_v,
           w_in, sinks, conv_w, w_mix_out, ln1_g, ln1_b, w_q_mem, w_k_mem, w_v_mem, w_o_mem, ln2_g, ln2_b,
           w_router_group, b_router_group, w_router_expert, b_router_expert, w_gate, w_up, w_down,
           ln3_g, ln3_b):
    f32 = jnp.float32
    row = lambda a: a.reshape(1, -1).astype(f32)
    w_in_b, wmix_b, wq_b, wk_b, wv_b, wo_b = (_bf16(w) for w in (w_in, w_mix_out, w_q_mem, w_k_mem, w_v_mem, w_o_mem))
    g1, b1, g2, b2, g3, b3 = (row(a) for a in (ln1_g, ln1_b, ln2_g, ln2_b, ln3_g, ln3_b))
    pad = LANES - N_GROUPS - N_EXPERTS
    wr = jnp.concatenate([w_router_group, w_router_expert, jnp.zeros((D_MODEL, pad), f32)], axis=1)
    wrh = _bf16(wr)
    wrl = _bf16(wr - wrh.astype(f32))
    br = jnp.concatenate([b_router_group, b_router_expert, jnp.zeros((pad,), f32)]).reshape(1, LANES)

    xs = x_sample.reshape(N_SAMPLE, D_MODEL)
    cos_s, sin_s = _rope_tables(PAST_LEN + jnp.arange(DEC_SEQ))
    cos_s, sin_s = jnp.tile(cos_s, (DEC_BATCH, 1)), jnp.tile(sin_s, (DEC_BATCH, 1))
    c0 = jnp.repeat(cache_conv[:, 0], DEC_SEQ, axis=0)
    c1 = jnp.repeat(cache_conv[:, 1], DEC_SEQ, axis=0)
    q_s, k_s, v_s, conv_s, u_s = _proj_sample(xs, w_in_b, cos_s, sin_s, conv_w, c0, c1)
    attn_s, swa_k_s, swa_v_s = _swa_sample(sinks, q_s, k_s, v_s, cache_swa_k, cache_swa_v)
    h1_s, qm_s = _post_a_sample(attn_s, conv_s, xs, wmix_b, g1, b1, wq_b)
    o_s = _mem_attn_sample(qm_s, cache_mem_k, cache_mem_v)
    tri = _bf16(jnp.tril(jnp.ones((TM_POST, TM_POST), f32), -1))
    h2t_s, rt_s, cnt_s = _post_b_sample(o_s, h1_s, wo_b, g2, b2, wrh, wrl, br, tri)

    xp = x_prompt.reshape(N_PROMPT, D_MODEL)
    cos_p, sin_p = _rope_tables(jnp.arange(N_PROMPT))
    q_p, kx_p, vx_p, conv_p, k_tail, v_tail, u_tail = _proj_prompt(xp, w_in_b, cos_p, sin_p, conv_w)
    attn_p = _swa_prompt(sinks, q_p, kx_p, vx_p)
    mk, mv, mk_b, mv_b = _mem_kv(mem_prompt.reshape(N_MEM, D_MODEL), wk_b, wv_b)
    h2t, rt, cnt = _post_prompt(attn_p, conv_p, xp, wmix_b, g1, b1, wq_b, mk_b, mv_b, wo_b, g2, b2,
                                wrh, wrl, br, tri, h2t_s, rt_s, cnt_s)

    item_tile, item_expert, item_lo, item_hi, pos3 = _routing_plan(rt, cnt)
    x_sorted = _dispatch(pos3, h2t)
    yt = _moe_ffn(item_tile, item_expert, item_lo, item_hi, x_sorted, w_gate, w_up, w_down)
    y_p = _combine(pos3, yt, h2t, rt, g3, b3, 0, N_PROMPT // TM_COMB)
    y_s = _combine(pos3, yt, h2t, rt, g3, b3, N_PROMPT // TM_COMB, N_SAMPLE // TM_COMB)

    return (y_p.reshape(1, SEQ, D_MODEL),
            y_s.reshape(DEC_BATCH, DEC_SEQ, D_MODEL),
            k_tail.reshape(1, WINDOW, N_KV_HEADS, HEAD_DIM),
            v_tail.reshape(1, WINDOW, N_KV_HEADS, HEAD_DIM),
            u_tail[SUBLANES - (CONV_K - 1):].reshape(1, CONV_K - 1, CONV_CH),
            mk.reshape(1, N_MEM, MEM_HEADS, MEM_HEAD_DIM),
            mv.reshape(1, N_MEM, MEM_HEADS, MEM_HEAD_DIM),
            swa_k_s.reshape(DEC_BATCH, WINDOW, N_KV_HEADS, HEAD_DIM),
            swa_v_s.reshape(DEC_BATCH, WINDOW, N_KV_HEADS, HEAD_DIM),
            u_s.reshape(DEC_BATCH, DEC_SEQ, CONV_CH)[:, DEC_SEQ - (CONV_K - 1):])
```

```python
import functools

import jax
import jax.numpy as jnp
from jax import lax
from jax.experimental import pallas as pl
from jax.experimental.pallas import tpu as pltpu

D_MODEL = 1024
SEQ = 16384
DEC_BATCH = 128
DEC_SEQ = 4
PAST_LEN = 16384
ATTN_WIDTH = 512
CONV_CH = 512
HEAD_DIM = 64
N_HEADS = 8
N_KV_HEADS = 2
KV_WIDTH = 128
WINDOW = 128
ROPE_THETA = 500000.0
ROPE_DIM = 16
CONV_K = 3
Q_END = ATTN_WIDTH
K_END = Q_END + KV_WIDTH
V_END = K_END + KV_WIDTH
B_END = V_END + CONV_CH
C_END = B_END + CONV_CH
IN_WIDTH = C_END + CONV_CH
N_MEM = 256
MEM_HEADS = 4
MEM_HEAD_DIM = 256
N_GROUPS = 4
EXPERTS_PER_GROUP = 8
N_EXPERTS = 32
EXPERT_FF = 256
ALPHA = 2.0 ** 0.25
LN_EPS = 1e-5

LANES = 128
SUBLANES = 8
ROW_CHUNKS = D_MODEL // LANES
VMEM_LIMIT = 56 * 1024 * 1024

N_PROMPT = SEQ
N_SAMPLE = DEC_BATCH * DEC_SEQ
N_ALL = N_PROMPT + N_SAMPLE
TM_PROJ = 512
TM_POST = 512
TM_MOE = 512
TM_COMB = 512
N_ASSIGN = 2 * N_ALL
MOE_TILES = N_ASSIGN // TM_MOE
MOE_ITEMS = MOE_TILES + N_EXPERTS
SAMPLE_BB = 4
SWA_BB = 8

assert ROW_CHUNKS == SUBLANES
assert N_SAMPLE == TM_POST
assert N_ASSIGN % TM_MOE == 0 and N_ALL % TM_COMB == 0


def _params(sem, vmem=VMEM_LIMIT):
    return pltpu.CompilerParams(dimension_semantics=sem, vmem_limit_bytes=vmem)


def _bf16(x):
    return x.astype(jnp.bfloat16)


def _dot(a, b):
    return jnp.dot(a, b, preferred_element_type=jnp.float32)


def _dot_nt(a, b):
    return lax.dot_general(a, b, (((1,), (1,)), ((), ())), preferred_element_type=jnp.float32)


def _layer_norm(x, g, b):
    mu = jnp.mean(x, axis=-1, keepdims=True)
    xc = x - mu
    var = jnp.mean(xc * xc, axis=-1, keepdims=True)
    return xc * lax.rsqrt(var + LN_EPS) * g + b


def _rope(x, cos_t, sin_t):
    lane = lax.broadcasted_iota(jnp.int32, x.shape, 1) % HEAD_DIM
    half = ROPE_DIM // 2
    partner = jnp.where(lane < half, pltpu.roll(x, LANES - half, axis=1), pltpu.roll(x, half, axis=1))
    return x * cos_t + partner * sin_t


def _head_slabs(x):
    lane = lax.broadcasted_iota(jnp.int32, x.shape, 1)
    lo = lane < HEAD_DIM
    sw = pltpu.roll(x, HEAD_DIM, axis=1)
    zero = jnp.zeros_like(x)
    slabs = [jnp.where(lo, x, zero), jnp.where(lo, zero, sw), jnp.where(lo, sw, zero), jnp.where(lo, zero, x)]
    return _bf16(jnp.concatenate(slabs, axis=1))


def _store_token_tiles(ref, val):
    rows = val.shape[0]
    for c in range(ROW_CHUNKS):
        ref[pl.ds(c, rows, stride=ROW_CHUNKS), :] = val[:, c * LANES:(c + 1) * LANES]


def _load_token_tiles(ref, base, rows):
    return jnp.concatenate(
        [ref[pl.ds(base + c, rows, stride=ROW_CHUNKS), :] for c in range(ROW_CHUNKS)], axis=1)


ROPE_ONE = 3 * (ROPE_DIM // 2)


def _rope_patterns(tab):
    half = ROPE_DIM // 2
    m = lax.broadcasted_iota(jnp.int32, tab.shape, 1) % HEAD_DIM
    idx_c = jnp.where(m < ROPE_DIM, m % half, ROPE_ONE)
    idx_s = jnp.where(m < half, 2 * half + m, jnp.where(m < ROPE_DIM, m, ROPE_ONE + 1))
    return jnp.take_along_axis(tab, idx_c, axis=1), jnp.take_along_axis(tab, idx_s, axis=1)


def _proj_common(x_ref, w_ref, tab_ref):
    xb = _bf16(x_ref[...])
    cos_t, sin_t = _rope_patterns(tab_ref[...])
    q = _dot(xb, w_ref[:, 0:Q_END])
    q_rot = jnp.concatenate(
        [_rope(q[:, p * LANES:(p + 1) * LANES], cos_t, sin_t) for p in range(ATTN_WIDTH // LANES)], axis=1)
    q_out = _bf16(q_rot * (HEAD_DIM ** -0.5))
    k = _rope(_dot(xb, w_ref[:, Q_END:K_END]), cos_t, sin_t)
    v = _dot(xb, w_ref[:, K_END:V_END])
    bg = _dot(xb, w_ref[:, V_END:B_END])
    u = _dot(xb, w_ref[:, B_END:C_END]) * _dot(xb, w_ref[:, C_END:IN_WIDTH])
    return q_out, k, v, bg, u


def _conv3(bg, u, u1, u2, cw_ref):
    cw = cw_ref[...]
    return bg * (cw[0:1, :] * u2 + cw[1:2, :] * u1 + cw[2:3, :] * u)


def _proj_prompt_kernel(x_ref, w_ref, tab_ref, cw_ref,
                        q_ref, kx_ref, vx_ref, conv_ref, ktail_ref, vtail_ref, utail_ref, carry_ref):
    @pl.when(pl.program_id(0) == 0)
    def _():
        carry_ref[...] = jnp.zeros_like(carry_ref)

    q_out, k, v, bg, u = _proj_common(x_ref, w_ref, tab_ref)
    tm = u.shape[0]
    ext = jnp.concatenate([carry_ref[...], u], axis=0)
    u1 = pltpu.roll(ext, 1, axis=0)[SUBLANES:SUBLANES + tm]
    u2 = pltpu.roll(ext, 2, axis=0)[SUBLANES:SUBLANES + tm]
    q_ref[...] = q_out
    kx_ref[...] = _head_slabs(k)
    vx_ref[...] = _head_slabs(v)
    conv_ref[...] = _bf16(_conv3(bg, u, u1, u2, cw_ref))
    ktail_ref[...] = k[tm - WINDOW:tm]
    vtail_ref[...] = v[tm - WINDOW:tm]
    utail_ref[...] = u[tm - SUBLANES:tm]
    carry_ref[...] = u[tm - SUBLANES:tm]


def _proj_sample_kernel(x_ref, w_ref, tab_ref, cw_ref, c0_ref, c1_ref,
                        q_ref, k_ref, v_ref, conv_ref, u_ref):
    q_out, k, v, bg, u = _proj_common(x_ref, w_ref, tab_ref)
    t = lax.broadcasted_iota(jnp.int32, u.shape, 0) % DEC_SEQ
    c0 = c0_ref[...]
    c1 = c1_ref[...]
    u1 = jnp.where(t >= 1, pltpu.roll(u, 1, axis=0), c1)
    u2 = jnp.where(t >= 2, pltpu.roll(u, 2, axis=0), jnp.where(t == 1, c1, c0))
    q_ref[...] = q_out.astype(jnp.float32)
    k_ref[...] = k
    v_ref[...] = v
    conv_ref[...] = _bf16(_conv3(bg, u, u1, u2, cw_ref))
    u_ref[...] = u


def _rope_table(pos):
    half = ROPE_DIM // 2
    inv = ROPE_THETA ** (-jnp.arange(0, ROPE_DIM, 2, dtype=jnp.float32) / ROPE_DIM)
    ang = pos.astype(jnp.float32)[:, None] * inv
    cos, sin = jnp.cos(ang), jnp.sin(ang)
    n = pos.shape[0]
    assert ROPE_ONE == 3 * half
    return jnp.concatenate([cos, sin, -sin, jnp.ones((n, 1), jnp.float32),
                            jnp.zeros((n, LANES - ROPE_ONE - 1), jnp.float32)], axis=1)


def _proj_prompt(x, w_in_b, tab, conv_w):
    n = x.shape[0]
    tm = TM_PROJ
    row = lambda w: pl.BlockSpec((tm, w), lambda i: (i, 0))
    full = lambda a: pl.BlockSpec(a.shape, lambda i: (0,) * a.ndim)
    const = lambda r, w: pl.BlockSpec((r, w), lambda i: (0, 0))
    return pl.pallas_call(
        _proj_prompt_kernel,
        grid=(n // tm,),
        in_specs=[row(D_MODEL), full(w_in_b), row(LANES), full(conv_w)],
        out_specs=[row(ATTN_WIDTH), row(4 * LANES), row(4 * LANES), row(CONV_CH),
                   const(WINDOW, KV_WIDTH), const(WINDOW, KV_WIDTH), const(SUBLANES, CONV_CH)],
        out_shape=[jax.ShapeDtypeStruct((n, ATTN_WIDTH), jnp.bfloat16),
                   jax.ShapeDtypeStruct((n, 4 * LANES), jnp.bfloat16),
                   jax.ShapeDtypeStruct((n, 4 * LANES), jnp.bfloat16),
                   jax.ShapeDtypeStruct((n, CONV_CH), jnp.bfloat16),
                   jax.ShapeDtypeStruct((WINDOW, KV_WIDTH), jnp.float32),
                   jax.ShapeDtypeStruct((WINDOW, KV_WIDTH), jnp.float32),
                   jax.ShapeDtypeStruct((SUBLANES, CONV_CH), jnp.float32)],
        scratch_shapes=[pltpu.VMEM((SUBLANES, CONV_CH), jnp.float32)],
        compiler_params=_params(("arbitrary",)),
        name="proj_prompt",
    )(x, w_in_b, tab, conv_w)


def _proj_sample(x, w_in_b, tab, conv_w, c0, c1):
    n = x.shape[0]
    full = lambda a: pl.BlockSpec(a.shape, lambda i: (0,) * a.ndim)
    out = lambda w, dt: jax.ShapeDtypeStruct((n, w), dt)
    blk = lambda w: pl.BlockSpec((n, w), lambda i: (0, 0))
    return pl.pallas_call(
        _proj_sample_kernel,
        grid=(1,),
        in_specs=[full(x), full(w_in_b), full(tab), full(conv_w), full(c0), full(c1)],
        out_specs=[blk(ATTN_WIDTH), blk(KV_WIDTH), blk(KV_WIDTH), blk(CONV_CH), blk(CONV_CH)],
        out_shape=[out(ATTN_WIDTH, jnp.float32), out(KV_WIDTH, jnp.float32), out(KV_WIDTH, jnp.float32),
                   out(CONV_CH, jnp.bfloat16), out(CONV_CH, jnp.float32)],
        compiler_params=_params(("arbitrary",)),
        name="proj_sample",
    )(x, w_in_b, tab, conv_w, c0, c1)


def _sink_softmax_pv(s, valid, sink, vx):
    s = jnp.where(valid, s, -jnp.inf)
    m = jnp.maximum(jnp.max(s, axis=1, keepdims=True), sink)
    p = jnp.exp(s - m)
    den = jnp.sum(p, axis=1, keepdims=True) + jnp.exp(sink - m)
    return _dot(_bf16(p), vx) / den


SWA_QB = 2


def _swa_prompt_kernel(sinks_ref, q_ref, kc_ref, kp_ref, vc_ref, vp_ref, o_ref):
    step = pl.program_id(0)
    kall = jnp.concatenate([kp_ref[...], kc_ref[...]], axis=0)
    vall = jnp.concatenate([vp_ref[...], vc_ref[...]], axis=0)
    i = lax.broadcasted_iota(jnp.int32, (WINDOW, 2 * WINDOW), 0)
    j = lax.broadcasted_iota(jnp.int32, (WINDOW, 2 * WINDOW), 1)
    band = (j > i) & (j <= i + WINDOW)
    for sb in range(SWA_QB):
        rows = slice(sb * WINDOW, (sb + 1) * WINDOW)
        kcat = kall[sb * WINDOW:(sb + 2) * WINDOW]
        vcat = vall[sb * WINDOW:(sb + 2) * WINDOW]
        valid = band & ((step > 0) | (j >= WINDOW)) if sb == 0 else band
        for p in range(N_HEADS // 2):
            qs = q_ref[rows, p * LANES:(p + 1) * LANES]
            acc = None
            for e in range(2):
                hd = 2 * p + e
                slab = 2 * (hd // (N_HEADS // N_KV_HEADS)) + e
                kx = kcat[:, slab * LANES:(slab + 1) * LANES]
                vx = vcat[:, slab * LANES:(slab + 1) * LANES]
                o = _sink_softmax_pv(_dot_nt(qs, kx), valid, sinks_ref[hd], vx)
                acc = o if acc is None else acc + o
            o_ref[rows, p * LANES:(p + 1) * LANES] = _bf16(acc)


def _swa_prompt(sinks, q, kx, vx):
    n = q.shape[0]
    nb = n // (SWA_QB * WINDOW)
    cur = lambda w: pl.BlockSpec((SWA_QB * WINDOW, w), lambda i: (i, 0))
    prev = lambda w: pl.BlockSpec((WINDOW, w), lambda i: (jnp.maximum(SWA_QB * i - 1, 0), 0))
    return pl.pallas_call(
        _swa_prompt_kernel,
        grid=(nb,),
        in_specs=[pl.BlockSpec(memory_space=pltpu.SMEM), cur(ATTN_WIDTH),
                  cur(4 * LANES), prev(4 * LANES), cur(4 * LANES), prev(4 * LANES)],
        out_specs=cur(ATTN_WIDTH),
        out_shape=jax.ShapeDtypeStruct((n, ATTN_WIDTH), jnp.bfloat16),
        compiler_params=_params(("arbitrary",)),
        name="swa_prompt",
    )(sinks, q, kx, kx, vx, vx)


SWA_ROWS = N_HEADS * DEC_SEQ
NEW_ROWS = 2 * SUBLANES


def _swa_sample_kernel(q_ref, sink_ref, kn_ref, vn_ref, kt_ref, vt_ref, o_ref, okt_ref, ovt_ref):
    t = lax.broadcasted_iota(jnp.int32, (SWA_ROWS, WINDOW), 0) % DEC_SEQ
    j = lax.broadcasted_iota(jnp.int32, (SWA_ROWS, WINDOW), 1)
    valid_c = j > t
    valid_n = (lax.broadcasted_iota(jnp.int32, (SWA_ROWS, NEW_ROWS), 1)
               <= lax.broadcasted_iota(jnp.int32, (SWA_ROWS, NEW_ROWS), 0) % DEC_SEQ)
    lane = lax.broadcasted_iota(jnp.int32, (KV_WIDTH, WINDOW), 1)
    sink = sink_ref[:, 0:1]
    shift = WINDOW - DEC_SEQ
    zrows = jnp.zeros((KV_WIDTH - NEW_ROWS, KV_WIDTH), jnp.float32)
    for b in range(SWA_BB):
        q = _bf16(q_ref[b])
        kt, vt = kt_ref[b], vt_ref[b]
        kn, vn = kn_ref[b], vn_ref[b]
        s_c = jnp.where(valid_c, _dot(q, _bf16(kt)), -jnp.inf)
        s_n = jnp.where(valid_n, _dot_nt(q, _bf16(kn)), -jnp.inf)
        m = jnp.maximum(jnp.maximum(jnp.max(s_c, axis=1, keepdims=True), jnp.max(s_n, axis=1, keepdims=True)), sink)
        p_c = jnp.exp(s_c - m)
        p_n = jnp.exp(s_n - m)
        den = jnp.sum(p_c, axis=1, keepdims=True) + jnp.sum(p_n, axis=1, keepdims=True) + jnp.exp(sink - m)
        o_ref[b] = (_dot_nt(_bf16(p_c), _bf16(vt)) + _dot(_bf16(p_n), _bf16(vn))) / den
        for old, new, dst in ((kt, kn, okt_ref), (vt, vn, ovt_ref)):
            new_cols = pltpu.roll(jnp.concatenate([new, zrows], axis=0).T, shift, axis=1)
            dst[b] = jnp.where(lane >= shift, new_cols, pltpu.roll(old, shift, axis=1))


def _swa_sample(sinks, q, kn, vn, cache_k, cache_v):
    nb = cache_k.shape[0]
    bb = SWA_BB
    groups = N_HEADS // N_KV_HEADS
    qh = q.reshape(nb, DEC_SEQ, N_KV_HEADS, groups, HEAD_DIM).transpose(0, 2, 3, 1, 4)
    qh = qh.reshape(nb, N_KV_HEADS, groups * DEC_SEQ, HEAD_DIM)
    zeros = jnp.zeros_like(qh[:, 0])
    qbd = jnp.concatenate([jnp.concatenate([qh[:, 0], zeros], axis=-1),
                           jnp.concatenate([zeros, qh[:, 1]], axis=-1)], axis=1)
    sink_col = jnp.broadcast_to(jnp.repeat(sinks, DEC_SEQ).reshape(SWA_ROWS, 1), (SWA_ROWS, LANES))
    pad8 = lambda a: jnp.pad(a.reshape(nb, DEC_SEQ, KV_WIDTH), ((0, 0), (0, NEW_ROWS - DEC_SEQ), (0, 0)))
    to_t = lambda c: c.transpose(0, 2, 3, 1).reshape(nb, KV_WIDTH, WINDOW)
    blk = lambda r, w: pl.BlockSpec((bb, r, w), lambda i: (i, 0, 0))
    o, okt, ovt = pl.pallas_call(
        _swa_sample_kernel,
        grid=(nb // bb,),
        in_specs=[blk(SWA_ROWS, KV_WIDTH), pl.BlockSpec((SWA_ROWS, LANES), lambda i: (0, 0)),
                  blk(NEW_ROWS, KV_WIDTH), blk(NEW_ROWS, KV_WIDTH), blk(KV_WIDTH, WINDOW), blk(KV_WIDTH, WINDOW)],
        out_specs=[blk(SWA_ROWS, KV_WIDTH), blk(KV_WIDTH, WINDOW), blk(KV_WIDTH, WINDOW)],
        out_shape=[jax.ShapeDtypeStruct((nb, SWA_ROWS, KV_WIDTH), jnp.float32),
                   jax.ShapeDtypeStruct((nb, KV_WIDTH, WINDOW), jnp.float32),
                   jax.ShapeDtypeStruct((nb, KV_WIDTH, WINDOW), jnp.float32)],
        compiler_params=_params(("arbitrary",)),
        name="swa_sample",
    )(qbd, sink_col, pad8(kn), pad8(vn), to_t(cache_k), to_t(cache_v))
    o = o.reshape(nb, N_KV_HEADS, groups, DEC_SEQ, N_KV_HEADS, HEAD_DIM)
    attn = jnp.stack([o[:, h, :, :, h, :] for h in range(N_KV_HEADS)], axis=1)
    attn = attn.transpose(0, 3, 1, 2, 4).reshape(nb * DEC_SEQ, ATTN_WIDTH)
    from_t = lambda c: c.reshape(nb, N_KV_HEADS, HEAD_DIM, WINDOW).transpose(0, 3, 1, 2)
    return attn, from_t(okt), from_t(ovt)


def _mem_kv_kernel(mem_ref, wk_ref, wv_ref, mk_ref, mv_ref, mkb_ref, mvb_ref):
    mb = _bf16(mem_ref[...])
    mk = _dot(mb, wk_ref[...])
    mv = _dot(mb, wv_ref[...])
    mk_ref[...] = mk
    mv_ref[...] = mv
    mkb_ref[...] = _bf16(mk)
    mvb_ref[...] = _bf16(mv)


def _mem_kv(mem, wk_b, wv_b):
    full = lambda a: pl.BlockSpec(a.shape, lambda i: (0,) * a.ndim)
    blk = pl.BlockSpec((N_MEM, D_MODEL), lambda i: (0, 0))
    f32 = jax.ShapeDtypeStruct((N_MEM, D_MODEL), jnp.float32)
    b16 = jax.ShapeDtypeStruct((N_MEM, D_MODEL), jnp.bfloat16)
    return pl.pallas_call(
        _mem_kv_kernel,
        grid=(1,),
        in_specs=[full(mem), full(wk_b), full(wv_b)],
        out_specs=[blk, blk, blk, blk],
        out_shape=[f32, f32, b16, b16],
        compiler_params=_params(("arbitrary",)),
        name="mem_kv",
    )(mem, wk_b, wv_b)


def _mix_ln1(attn_ref, conv_ref, x_ref, wmix_ref, g1_ref, b1_ref):
    mix = _dot(_bf16(attn_ref[...]), wmix_ref[0:ATTN_WIDTH, :]) + _dot(conv_ref[...], wmix_ref[ATTN_WIDTH:, :])
    return _layer_norm(ALPHA * x_ref[...] + mix, g1_ref[...], b1_ref[...])


def _mem_q(h1, wq_ref):
    return _bf16(_dot(_bf16(h1), wq_ref[...]) * (MEM_HEAD_DIM ** -0.5))


def _route(h2, wrhl_ref, br_ref, tri_ref, carry):
    hi = _bf16(h2)
    lo = _bf16(h2 - hi.astype(jnp.float32))
    hh = _dot(hi, wrhl_ref[...])
    logits = hh[:, 0:LANES] + hh[:, LANES:] + _dot(lo, wrhl_ref[:, 0:LANES]) + br_ref[...]
    lane_i = lax.broadcasted_iota(jnp.int32, logits.shape, 1)
    lane = lane_i.astype(jnp.float32)
    big = jnp.float32(LANES)
    is_g = lane_i < N_GROUPS
    gl = jnp.where(is_g, logits, -jnp.inf)
    gmax = jnp.max(gl, axis=1, keepdims=True)
    gidx = jnp.min(jnp.where(is_g & (logits == gmax), lane, big), axis=1, keepdims=True)
    gsum = jnp.sum(jnp.exp(gl - gmax), axis=1, keepdims=True)
    gw = 1.0 / gsum
    eid = lane_i - N_GROUPS
    assert EXPERTS_PER_GROUP == 8
    grp = lax.shift_right_arithmetic(eid, jnp.full_like(eid, 3)).astype(jnp.float32)
    in_e = (lane_i >= N_GROUPS) & (lane_i < N_GROUPS + N_EXPERTS) & (grp == gidx)
    v1 = jnp.max(jnp.where(in_e, logits, -jnp.inf), axis=1, keepdims=True)
    i1 = jnp.min(jnp.where(in_e & (logits == v1), lane, big), axis=1, keepdims=True)
    rest = in_e & (lane != i1)
    v2 = jnp.max(jnp.where(rest, logits, -jnp.inf), axis=1, keepdims=True)
    i2 = jnp.min(jnp.where(rest & (logits == v2), lane, big), axis=1, keepdims=True)
    ex = jnp.exp(v2 - v1)
    den = 1.0 + ex
    w1 = gw / den
    w2 = gw * ex / den
    zero = jnp.zeros_like(logits)
    pick1 = lane == i1
    pick2 = lane == i2
    sel = jnp.where(pick1 | pick2, 1.0, 0.0)
    before = _dot(tri_ref[...], _bf16(sel)) + carry
    rank1 = jnp.sum(jnp.where(pick1, before, zero), axis=1, keepdims=True)
    rank2 = jnp.sum(jnp.where(pick2, before, zero), axis=1, keepdims=True)
    cols = (i1 - N_GROUPS, i2 - N_GROUPS, w1, w2, rank1, rank2)
    route = zero
    for k, col in enumerate(cols):
        route = jnp.where(lane_i == k, col, route)
    return route, carry + jnp.sum(sel, axis=0, keepdims=True)


def _post_prompt_kernel(attn_ref, conv_ref, x_ref, wmix_ref, g1_ref, b1_ref, wq_ref, mk_ref, mv_ref, wo_ref,
                        g2_ref, b2_ref, wrhl_ref, br_ref, tri_ref, h2s_ref, rts_ref, cnts_ref,
                        h2t_ref, rt_ref, cnt_ref, carry_ref):
    is_tail = pl.program_id(0) == N_PROMPT // TM_POST

    @pl.when(pl.program_id(0) == 0)
    def _():
        carry_ref[...] = cnts_ref[...]

    @pl.when(is_tail)
    def _():
        h2t_ref[...] = h2s_ref[...]
        rt_ref[...] = rts_ref[...]

    @pl.when(jnp.logical_not(is_tail))
    def _():
        h1 = _mix_ln1(attn_ref, conv_ref, x_ref, wmix_ref, g1_ref, b1_ref)
        qm = _mem_q(h1, wq_ref)
        outs = []
        for h in range(MEM_HEADS):
            sl = slice(h * MEM_HEAD_DIM, (h + 1) * MEM_HEAD_DIM)
            s = _dot_nt(qm[:, sl], mk_ref[:, sl])
            m = jnp.max(s, axis=1, keepdims=True)
            p = jnp.exp(s - m)
            den = jnp.sum(p, axis=1, keepdims=True)
            outs.append(_dot(_bf16(p), mv_ref[:, sl]) / den)
        o = _bf16(jnp.concatenate(outs, axis=1))
        h2 = _layer_norm(ALPHA * h1 + _dot(o, wo_ref[...]), g2_ref[...], b2_ref[...])
        _store_token_tiles(h2t_ref, h2)
        route, carry = _route(h2, wrhl_ref, br_ref, tri_ref, carry_ref[0:1, :])
        rt_ref[...] = route
        carry_ref[...] = jnp.broadcast_to(carry, carry_ref.shape)
        cnt_ref[...] = jnp.broadcast_to(carry, cnt_ref.shape)


def _post_prompt(attn, conv, x, wmix_b, g1, b1, wq_b, mk_b, mv_b, wo_b, g2, b2, wrhl, br, tri, h2t_s, rt_s, cnt_s):
    n = x.shape[0]
    tm = TM_POST
    steps = n // tm
    row = lambda w: pl.BlockSpec((tm, w), lambda i: (jnp.minimum(i, steps - 1), 0))
    full = lambda a: pl.BlockSpec(a.shape, lambda i: (0,) * a.ndim)
    weights = (wmix_b, g1, b1, wq_b, mk_b, mv_b, wo_b, g2, b2, wrhl, br, tri, h2t_s, rt_s, cnt_s)
    n_out = n + h2t_s.shape[0] // ROW_CHUNKS
    return pl.pallas_call(
        _post_prompt_kernel,
        grid=(steps + 1,),
        in_specs=[row(ATTN_WIDTH), row(CONV_CH), row(D_MODEL)] + [full(a) for a in weights],
        out_specs=[pl.BlockSpec((tm * ROW_CHUNKS, LANES), lambda i: (i, 0)),
                   pl.BlockSpec((tm, LANES), lambda i: (i, 0)),
                   pl.BlockSpec((SUBLANES, LANES), lambda i: (0, 0))],
        out_shape=[jax.ShapeDtypeStruct((n_out * ROW_CHUNKS, LANES), jnp.float32),
                   jax.ShapeDtypeStruct((n_out, LANES), jnp.float32),
                   jax.ShapeDtypeStruct((SUBLANES, LANES), jnp.float32)],
        scratch_shapes=[pltpu.VMEM((SUBLANES, LANES), jnp.float32)],
        compiler_params=_params(("arbitrary",)),
        name="post_prompt",
    )(attn, conv, x, *weights)


def _post_a_sample_kernel(attn_ref, conv_ref, x_ref, wmix_ref, g1_ref, b1_ref, wq_ref, h1_ref, qm_ref):
    h1 = _mix_ln1(attn_ref, conv_ref, x_ref, wmix_ref, g1_ref, b1_ref)
    h1_ref[...] = h1
    qm_ref[...] = _mem_q(h1, wq_ref).astype(jnp.float32)


def _post_a_sample(attn, conv, x, wmix_b, g1, b1, wq_b):
    n = x.shape[0]
    args = (attn, conv, x, wmix_b, g1, b1, wq_b)
    full = lambda a: pl.BlockSpec(a.shape, lambda i: (0,) * a.ndim)
    blk = pl.BlockSpec((n, D_MODEL), lambda i: (0, 0))
    return pl.pallas_call(
        _post_a_sample_kernel,
        grid=(1,),
        in_specs=[full(a) for a in args],
        out_specs=[blk, blk],
        out_shape=[jax.ShapeDtypeStruct((n, D_MODEL), jnp.float32),
                   jax.ShapeDtypeStruct((n, D_MODEL), jnp.float32)],
        compiler_params=_params(("arbitrary",)),
        name="post_a_sample",
    )(*args)


MEM_ROWS = MEM_HEADS * DEC_SEQ


def _mem_attn_sample_kernel(q_ref, mk_ref, mv_ref, o_ref):
    nk = N_MEM * MEM_HEADS
    row_h = lax.broadcasted_iota(jnp.int32, (MEM_ROWS, nk), 0) // DEC_SEQ
    key_h = lax.broadcasted_iota(jnp.int32, (MEM_ROWS, nk), 1) % MEM_HEADS
    own = row_h == key_h
    for b in range(SAMPLE_BB):
        k2 = _bf16(mk_ref[b].reshape(nk, MEM_HEAD_DIM))
        v2 = _bf16(mv_ref[b].reshape(nk, MEM_HEAD_DIM))
        s = jnp.where(own, _dot_nt(_bf16(q_ref[b]), k2), -jnp.inf)
        m = jnp.max(s, axis=1, keepdims=True)
        p = jnp.exp(s - m)
        den = jnp.sum(p, axis=1, keepdims=True)
        o_ref[b] = _dot(_bf16(p), v2) / den


def _mem_attn_sample(qm, mk, mv):
    nb = mk.shape[0]
    bb = SAMPLE_BB
    q = qm.reshape(nb, DEC_SEQ, MEM_HEADS, MEM_HEAD_DIM).transpose(0, 2, 1, 3).reshape(nb, MEM_ROWS, MEM_HEAD_DIM)
    rows = pl.BlockSpec((bb, MEM_ROWS, MEM_HEAD_DIM), lambda i: (i, 0, 0))
    kv = pl.BlockSpec((bb, N_MEM, MEM_HEADS, MEM_HEAD_DIM), lambda i: (i, 0, 0, 0))
    o = pl.pallas_call(
        _mem_attn_sample_kernel,
        grid=(nb // bb,),
        in_specs=[rows, kv, kv],
        out_specs=rows,
        out_shape=jax.ShapeDtypeStruct((nb, MEM_ROWS, MEM_HEAD_DIM), jnp.float32),
        compiler_params=_params(("arbitrary",)),
        name="mem_attn_sample",
    )(q, mk, mv)
    return o.reshape(nb, MEM_HEADS, DEC_SEQ, MEM_HEAD_DIM).transpose(0, 2, 1, 3).reshape(nb * DEC_SEQ, D_MODEL)


def _post_b_sample_kernel(o_ref, h1_ref, wo_ref, g2_ref, b2_ref, wrhl_ref, br_ref, tri_ref,
                          h2t_ref, rt_ref, cnt_ref):
    h2 = _layer_norm(ALPHA * h1_ref[...] + _dot(_bf16(o_ref[...]), wo_ref[...]), g2_ref[...], b2_ref[...])
    _store_token_tiles(h2t_ref, h2)
    route, carry = _route(h2, wrhl_ref, br_ref, tri_ref, jnp.zeros((1, LANES), jnp.float32))
    rt_ref[...] = route
    cnt_ref[...] = jnp.broadcast_to(carry, cnt_ref.shape)


def _post_b_sample(o, h1, wo_b, g2, b2, wrhl, br, tri):
    n = h1.shape[0]
    args = (o, h1, wo_b, g2, b2, wrhl, br, tri)
    full = lambda a: pl.BlockSpec(a.shape, lambda i: (0,) * a.ndim)
    return pl.pallas_call(
        _post_b_sample_kernel,
        grid=(1,),
        in_specs=[full(a) for a in args],
        out_specs=[pl.BlockSpec((n * ROW_CHUNKS, LANES), lambda i: (0, 0)),
                   pl.BlockSpec((n, LANES), lambda i: (0, 0)),
                   pl.BlockSpec((SUBLANES, LANES), lambda i: (0, 0))],
        out_shape=[jax.ShapeDtypeStruct((n * ROW_CHUNKS, LANES), jnp.float32),
                   jax.ShapeDtypeStruct((n, LANES), jnp.float32),
                   jax.ShapeDtypeStruct((SUBLANES, LANES), jnp.float32)],
        compiler_params=_params(("arbitrary",)),
        name="post_b_sample",
    )(*args)


def _row_gather_copy(src_hbm, idx, dst, dst_row, sem):
    s0 = pl.multiple_of(idx * ROW_CHUNKS, ROW_CHUNKS)
    d0 = pl.multiple_of(dst_row * ROW_CHUNKS, ROW_CHUNKS)
    return pltpu.make_async_copy(src_hbm.at[pl.ds(s0, ROW_CHUNKS), :], dst.at[pl.ds(d0, ROW_CHUNKS), :], sem)


def _dispatch_kernel(pos_ref, h2t_ref, xs_hbm, sem):
    def body(r, c):
        src = h2t_ref.at[pl.ds(pl.multiple_of(r * ROW_CHUNKS, ROW_CHUNKS), ROW_CHUNKS), :]
        for k in range(2):
            d0 = pl.multiple_of(pos_ref[0, 0, k * TM_COMB + r] * ROW_CHUNKS, ROW_CHUNKS)
            pltpu.make_async_copy(src, xs_hbm.at[pl.ds(d0, ROW_CHUNKS), :], sem.at[0]).start(priority=k)
        return c
    lax.fori_loop(0, TM_COMB, body, 0, unroll=8)
    for _ in range(2):
        pltpu.make_async_copy(h2t_ref, xs_hbm.at[pl.ds(0, TM_COMB * ROW_CHUNKS), :], sem.at[0]).wait()


def _dispatch(pos3, h2t):
    nt = N_ALL // TM_COMB
    return pl.pallas_call(
        _dispatch_kernel,
        grid=(nt,),
        in_specs=[pl.BlockSpec((1, 1, 2 * TM_COMB), lambda i: (i, 0, 0), memory_space=pltpu.SMEM),
                  pl.BlockSpec((TM_COMB * ROW_CHUNKS, LANES), lambda i: (i, 0))],
        out_specs=pl.BlockSpec(memory_space=pl.ANY),
        out_shape=jax.ShapeDtypeStruct((N_ASSIGN * ROW_CHUNKS, LANES), jnp.float32),
        scratch_shapes=[pltpu.SemaphoreType.DMA((1,))],
        compiler_params=_params(("arbitrary",)),
        name="moe_dispatch",
    )(pos3, h2t)


def _moe_ffn_kernel(it_ref, ie_ref, lo_ref, hi_ref, x_ref, wg_ref, wu_ref, wd_ref, y_ref, wgb, wub, wdb, cur_e):
    i = pl.program_id(0)
    lo = lo_ref[i]
    hi = hi_ref[i]
    e = ie_ref[i]

    @pl.when(i == 0)
    def _():
        cur_e[0] = -1

    @pl.when((hi > lo) & (cur_e[0] != e))
    def _():
        wgb[...] = _bf16(wg_ref[0])
        wub[...] = _bf16(wu_ref[0])
        wdb[...] = _bf16(wd_ref[0])
        cur_e[0] = e

    def ffn():
        x = _bf16(_load_token_tiles(x_ref, 0, TM_MOE))
        hg = _dot(x, wgb[...])
        hu = _dot(x, wub[...])
        h = hg / (1.0 + jnp.exp(-hg)) * hu
        return _dot(_bf16(h), wdb[...])

    def rows_mask():
        row = lax.broadcasted_iota(jnp.int32, (TM_MOE, LANES), 0)
        return (row >= lo) & (row < hi)

    @pl.when((hi > lo) & (lo == 0))
    def _():
        y = ffn()
        mask = rows_mask()
        for c in range(ROW_CHUNKS):
            y_ref[pl.ds(c, TM_MOE, stride=ROW_CHUNKS), :] = jnp.where(mask, y[:, c * LANES:(c + 1) * LANES], 0.0)

    @pl.when((hi > lo) & (lo > 0))
    def _():
        y = ffn()
        mask = rows_mask()
        for c in range(ROW_CHUNKS):
            sl = pl.ds(c, TM_MOE, stride=ROW_CHUNKS)
            y_ref[sl, :] = jnp.where(mask, y[:, c * LANES:(c + 1) * LANES], y_ref[sl, :])


def _moe_ffn(item_tile, item_expert, item_lo, item_hi, x_sorted, w_gate, w_up, w_down):
    wspec = lambda shp: pl.BlockSpec((1,) + shp, lambda i, it, ie, lo, hi: (ie[i], 0, 0))
    tile = pl.BlockSpec((TM_MOE * ROW_CHUNKS, LANES), lambda i, it, ie, lo, hi: (it[i], 0))
    grid_spec = pltpu.PrefetchScalarGridSpec(
        num_scalar_prefetch=4,
        grid=(MOE_ITEMS,),
        in_specs=[tile, wspec((D_MODEL, EXPERT_FF)), wspec((D_MODEL, EXPERT_FF)), wspec((EXPERT_FF, D_MODEL))],
        out_specs=tile,
        scratch_shapes=[pltpu.VMEM((D_MODEL, EXPERT_FF), jnp.bfloat16),
                        pltpu.VMEM((D_MODEL, EXPERT_FF), jnp.bfloat16),
                        pltpu.VMEM((EXPERT_FF, D_MODEL), jnp.bfloat16),
                        pltpu.SMEM((1,), jnp.int32)],
    )
    return pl.pallas_call(
        _moe_ffn_kernel,
        grid_spec=grid_spec,
        out_shape=jax.ShapeDtypeStruct((N_ASSIGN * ROW_CHUNKS, LANES), jnp.float32),
        compiler_params=_params(("arbitrary",)),
        name="moe_ffn",
    )(item_tile, item_expert, item_lo, item_hi, x_sorted, w_gate, w_up, w_down)


def _combine_kernel(nt, pos_cur_ref, pos_nxt_ref, yt_hbm, h2t_ref, rt_ref, g3_ref, b3_ref, o_ref, abuf, sem):
    t = pl.program_id(0)
    slot = t % 2
    rows = 2 * TM_COMB

    def issue(pos_ref, s):
        def body(j, c):
            for k in range(2):
                r = 2 * j + k
                _row_gather_copy(yt_hbm, pos_ref[0, 0, r], abuf, s * rows + r, sem.at[s]).start(priority=k)
            return c
        lax.fori_loop(0, rows // 2, body, 0, unroll=4)

    @pl.when(t == 0)
    def _():
        issue(pos_cur_ref, 0)

    @pl.when(t + 1 < nt)
    def _():
        issue(pos_nxt_ref, 1 - slot)

    base = pl.multiple_of(slot * (rows * ROW_CHUNKS), rows * ROW_CHUNKS)
    pltpu.make_async_copy(yt_hbm.at[pl.ds(0, rows * ROW_CHUNKS), :],
                          abuf.at[pl.ds(base, rows * ROW_CHUNKS), :], sem.at[slot]).wait()
    ya = _load_token_tiles(abuf, base, TM_COMB)
    yb = _load_token_tiles(abuf, base + TM_COMB * ROW_CHUNKS, TM_COMB)
    rt = rt_ref[...]
    ff = rt[:, 2:3] * ya + rt[:, 3:4] * yb
    h2 = _load_token_tiles(h2t_ref, 0, TM_COMB)
    o_ref[...] = _layer_norm(ALPHA * h2 + ff, g3_ref[...], b3_ref[...])


def _combine(pos3, yt, h2t, rt, g3, b3, tile0, n_tiles):
    last = tile0 + n_tiles - 1
    smem_pos = lambda f: pl.BlockSpec((1, 1, 2 * TM_COMB), f, memory_space=pltpu.SMEM)
    full = lambda a: pl.BlockSpec(a.shape, lambda i: (0,) * a.ndim)
    return pl.pallas_call(
        functools.partial(_combine_kernel, n_tiles),
        grid=(n_tiles,),
        in_specs=[smem_pos(lambda i: (tile0 + i, 0, 0)),
                  smem_pos(lambda i: (jnp.minimum(tile0 + i + 1, last), 0, 0)),
                  pl.BlockSpec(memory_space=pl.ANY),
                  pl.BlockSpec((TM_COMB * ROW_CHUNKS, LANES), lambda i: (tile0 + i, 0)),
                  pl.BlockSpec((TM_COMB, LANES), lambda i: (tile0 + i, 0)),
                  full(g3), full(b3)],
        out_specs=pl.BlockSpec((TM_COMB, D_MODEL), lambda i: (i, 0)),
        out_shape=jax.ShapeDtypeStruct((n_tiles * TM_COMB, D_MODEL), jnp.float32),
        scratch_shapes=[pltpu.VMEM((2 * 2 * TM_COMB * ROW_CHUNKS, LANES), jnp.float32),
                        pltpu.SemaphoreType.DMA((2,))],
        compiler_params=_params(("arbitrary",)),
        name="moe_combine",
    )(pos3, pos3, yt, h2t, rt, g3, b3)


POS_TILES = 3


def _positions_kernel(rt_ref, starts_ref, pos_ref):
    lane = lax.broadcasted_iota(jnp.int32, (TM_COMB, LANES), 1)
    lane_f = lane.astype(jnp.float32)
    starts = starts_ref[0:1, :]
    for j in range(POS_TILES):
        rt = rt_ref[j * TM_COMB:(j + 1) * TM_COMB, :]
        cols = []
        for k in range(2):
            seg = jnp.sum(jnp.where(lane_f == rt[:, k:k + 1] + N_GROUPS, starts, 0.0), axis=1, keepdims=True)
            cols.append(seg + rt[:, 4 + k:5 + k])
        packed = jnp.where(lane == 0, cols[0], jnp.where(lane == 1, cols[1], 0.0))
        rows = packed.T
        pos_ref[j] = jnp.concatenate([rows[0:1, :], rows[1:2, :]], axis=1).astype(jnp.int32)


def _positions(rt, starts_row):
    nt = N_ALL // TM_COMB
    assert nt % POS_TILES == 0
    return pl.pallas_call(
        _positions_kernel,
        grid=(nt // POS_TILES,),
        in_specs=[pl.BlockSpec((POS_TILES * TM_COMB, LANES), lambda i: (i, 0)),
                  pl.BlockSpec((SUBLANES, LANES), lambda i: (0, 0))],
        out_specs=pl.BlockSpec((POS_TILES, 1, 2 * TM_COMB), lambda i: (i, 0, 0)),
        out_shape=jax.ShapeDtypeStruct((nt, 1, 2 * TM_COMB), jnp.int32),
        compiler_params=_params(("arbitrary",)),
        name="moe_positions",
    )(rt, starts_row)


def _routing_plan(rt, cnt):
    i32 = jnp.int32
    counts_f = cnt[0, N_GROUPS:N_GROUPS + N_EXPERTS]
    starts_f = jnp.cumsum(counts_f) - counts_f
    starts_row = jnp.broadcast_to(
        jnp.pad(starts_f, (N_GROUPS, LANES - N_GROUPS - N_EXPERTS))[None, :], (SUBLANES, LANES))
    pos3 = _positions(rt, starts_row)
    starts = starts_f.astype(i32)
    tiles = jnp.arange(MOE_TILES, dtype=i32) * TM_MOE
    rank_t = jnp.arange(MOE_TILES, dtype=i32) + jnp.sum((starts[None, :] < tiles[:, None]).astype(i32), axis=1)
    rank_s = jnp.arange(N_EXPERTS, dtype=i32) + jnp.sum((tiles[None, :] <= starts[:, None]).astype(i32), axis=1)
    vals = jnp.concatenate([tiles, starts])
    ranks = jnp.concatenate([rank_t, rank_s])
    slot = jnp.arange(MOE_ITEMS, dtype=i32)
    lo = jnp.sum(jnp.where(ranks[None, :] == slot[:, None], vals[None, :], 0), axis=1)
    hi = jnp.concatenate([lo[1:], jnp.full((1,), N_ASSIGN, i32)])
    item_tile = jnp.minimum(lo // TM_MOE, MOE_TILES - 1)
    item_expert = jnp.clip(jnp.sum((starts[None, :] <= lo[:, None]).astype(i32), axis=1) - 1, 0, N_EXPERTS - 1)
    base = item_tile * TM_MOE
    return item_tile, item_expert, lo - base, hi - base, pos3


def kernel(x_prompt, x_sample, mem_prompt, cache_swa_k, cache_swa_v, cache_conv, cache_mem_k, cache_mem_v,
           w_in, sinks, conv_w, w_mix_out, ln1_g, ln1_b, w_q_mem, w_k_mem, w_v_mem, w_o_mem, ln2_g, ln2_b,
           w_router_group, b_router_group, w_router_expert, b_router_expert, w_gate, w_up, w_down,
           ln3_g, ln3_b):
    f32 = jnp.float32
    row = lambda a: a.reshape(1, -1).astype(f32)
    w_in_b, wmix_b, wq_b, wk_b, wv_b, wo_b = (_bf16(w) for w in (w_in, w_mix_out, w_q_mem, w_k_mem, w_v_mem, w_o_mem))
    g1, b1, g2, b2, g3, b3 = (row(a) for a in (ln1_g, ln1_b, ln2_g, ln2_b, ln3_g, ln3_b))
    pad = LANES - N_GROUPS - N_EXPERTS
    wr = jnp.concatenate([w_router_group, w_router_expert, jnp.zeros((D_MODEL, pad), f32)], axis=1)
    wrh = _bf16(wr)
    wrhl = jnp.concatenate([wrh, _bf16(wr - wrh.astype(f32))], axis=1)
    br = jnp.concatenate([b_router_group, b_router_expert, jnp.zeros((pad,), f32)]).reshape(1, LANES)

    xs = x_sample.reshape(N_SAMPLE, D_MODEL)
    tab_s = jnp.tile(_rope_table(PAST_LEN + jnp.arange(DEC_SEQ)), (DEC_BATCH, 1))
    c0 = jnp.repeat(cache_conv[:, 0], DEC_SEQ, axis=0)
    c1 = jnp.repeat(cache_conv[:, 1], DEC_SEQ, axis=0)
    q_s, k_s, v_s, conv_s, u_s = _proj_sample(xs, w_in_b, tab_s, conv_w, c0, c1)
    attn_s, swa_k_s, swa_v_s = _swa_sample(sinks, q_s, k_s, v_s, cache_swa_k, cache_swa_v)
    h1_s, qm_s = _post_a_sample(attn_s, conv_s, xs, wmix_b, g1, b1, wq_b)
    o_s = _mem_attn_sample(qm_s, cache_mem_k, cache_mem_v)
    tri = _bf16(jnp.tril(jnp.ones((TM_POST, TM_POST), f32), -1))
    h2t_s, rt_s, cnt_s = _post_b_sample(o_s, h1_s, wo_b, g2, b2, wrhl, br, tri)

    xp = x_prompt.reshape(N_PROMPT, D_MODEL)
    tab_p = _rope_table(jnp.arange(N_PROMPT))
    q_p, kx_p, vx_p, conv_p, k_tail, v_tail, u_tail = _proj_prompt(xp, w_in_b, tab_p, conv_w)
    attn_p = _swa_prompt(sinks, q_p, kx_p, vx_p)
    mk, mv, mk_b, mv_b = _mem_kv(mem_prompt.reshape(N_MEM, D_MODEL), wk_b, wv_b)
    h2t, rt, cnt = _post_prompt(attn_p, conv_p, xp, wmix_b, g1, b1, wq_b, mk_b, mv_b, wo_b, g2, b2,
                                wrhl, br, tri, h2t_s, rt_s, cnt_s)

    item_tile, item_expert, item_lo, item_hi, pos3 = _routing_plan(rt, cnt)
    x_sorted = _dispatch(pos3, h2t)
    yt = _moe_ffn(item_tile, item_expert, item_lo, item_hi, x_sorted, w_gate, w_up, w_down)
    y_p = _combine(pos3, yt, h2t, rt, g3, b3, 0, N_PROMPT // TM_COMB)
    y_s = _combine(pos3, yt, h2t, rt, g3, b3, N_PROMPT // TM_COMB, N_SAMPLE // TM_COMB)

    return (y_p.reshape(1, SEQ, D_MODEL),
            y_s.reshape(DEC_BATCH, DEC_SEQ, D_MODEL),
            k_tail.reshape(1, WINDOW, N_KV_HEADS, HEAD_DIM),
            v_tail.reshape(1, WINDOW, N_KV_HEADS, HEAD_DIM),
            u_tail[SUBLANES - (CONV_K - 1):].reshape(1, CONV_K - 1, CONV_CH),
            mk.reshape(1, N_MEM, MEM_HEADS, MEM_HEAD_DIM),
            mv.reshape(1, N_MEM, MEM_HEADS, MEM_HEAD_DIM),
            swa_k_s.reshape(DEC_BATCH, WINDOW, N_KV_HEADS, HEAD_DIM),
            swa_v_s.reshape(DEC_BATCH, WINDOW, N_KV_HEADS, HEAD_DIM),
            u_s.reshape(DEC_BATCH, DEC_SEQ, CONV_CH)[:, DEC_SEQ - (CONV_K - 1):])
```

```python
import functools

import jax
import jax.numpy as jnp
from jax import lax
from jax.experimental import pallas as pl
from jax.experimental.pallas import tpu as pltpu

D_MODEL = 1024
SEQ = 16384
DEC_BATCH = 128
DEC_SEQ = 4
PAST_LEN = 16384
ATTN_WIDTH = 512
CONV_CH = 512
HEAD_DIM = 64
N_HEADS = 8
N_KV_HEADS = 2
KV_WIDTH = 128
WINDOW = 128
ROPE_THETA = 500000.0
ROPE_DIM = 16
CONV_K = 3
Q_END = ATTN_WIDTH
K_END = Q_END + KV_WIDTH
V_END = K_END + KV_WIDTH
B_END = V_END + CONV_CH
C_END = B_END + CONV_CH
IN_WIDTH = C_END + CONV_CH
N_MEM = 256
MEM_HEADS = 4
MEM_HEAD_DIM = 256
N_GROUPS = 4
EXPERTS_PER_GROUP = 8
N_EXPERTS = 32
EXPERT_FF = 256
ALPHA = 2.0 ** 0.25
LN_EPS = 1e-5

LANES = 128
SUBLANES = 8
ROW_CHUNKS = D_MODEL // LANES
VMEM_LIMIT = 56 * 1024 * 1024

N_PROMPT = SEQ
N_SAMPLE = DEC_BATCH * DEC_SEQ
N_ALL = N_PROMPT + N_SAMPLE
TM_PROJ = 512
TM_POST = 512
TM_MOE = 512
TM_COMB = 512
N_ASSIGN = 2 * N_ALL
MOE_TILES = N_ASSIGN // TM_MOE
MOE_ITEMS = MOE_TILES + N_EXPERTS
SAMPLE_BB = 4
SWA_BB = 8

assert ROW_CHUNKS == SUBLANES
assert N_SAMPLE == TM_POST
assert N_ASSIGN % TM_MOE == 0 and N_ALL % TM_COMB == 0


def _params(sem, vmem=VMEM_LIMIT):
    return pltpu.CompilerParams(dimension_semantics=sem, vmem_limit_bytes=vmem)


def _bf16(x):
    return x.astype(jnp.bfloat16)


def _dot(a, b):
    return jnp.dot(a, b, preferred_element_type=jnp.float32)


def _dot_nt(a, b):
    return lax.dot_general(a, b, (((1,), (1,)), ((), ())), preferred_element_type=jnp.float32)


def _layer_norm(x, g, b):
    mu = jnp.mean(x, axis=-1, keepdims=True)
    xc = x - mu
    var = jnp.mean(xc * xc, axis=-1, keepdims=True)
    return xc * lax.rsqrt(var + LN_EPS) * g + b


def _rope(x, cos_t, sin_t):
    lane = lax.broadcasted_iota(jnp.int32, x.shape, 1) % HEAD_DIM
    half = ROPE_DIM // 2
    partner = jnp.where(lane < half, pltpu.roll(x, LANES - half, axis=1), pltpu.roll(x, half, axis=1))
    return x * cos_t + partner * sin_t


def _head_slabs(x):
    lane = lax.broadcasted_iota(jnp.int32, x.shape, 1)
    lo = lane < HEAD_DIM
    sw = pltpu.roll(x, HEAD_DIM, axis=1)
    zero = jnp.zeros_like(x)
    slabs = [jnp.where(lo, x, zero), jnp.where(lo, zero, sw), jnp.where(lo, sw, zero), jnp.where(lo, zero, x)]
    return _bf16(jnp.concatenate(slabs, axis=1))


def _store_token_tiles(ref, val):
    rows = val.shape[0]
    for c in range(ROW_CHUNKS):
        ref[pl.ds(c, rows, stride=ROW_CHUNKS), :] = val[:, c * LANES:(c + 1) * LANES]


def _load_token_tiles(ref, base, rows):
    return jnp.concatenate(
        [ref[pl.ds(base + c, rows, stride=ROW_CHUNKS), :] for c in range(ROW_CHUNKS)], axis=1)


ROPE_ONE = 3 * (ROPE_DIM // 2)


def _rope_patterns(tab):
    half = ROPE_DIM // 2
    m = lax.broadcasted_iota(jnp.int32, tab.shape, 1) % HEAD_DIM
    idx_c = jnp.where(m < ROPE_DIM, m % half, ROPE_ONE)
    idx_s = jnp.where(m < half, 2 * half + m, jnp.where(m < ROPE_DIM, m, ROPE_ONE + 1))
    return jnp.take_along_axis(tab, idx_c, axis=1), jnp.take_along_axis(tab, idx_s, axis=1)


def _proj_common(x_ref, w_ref, tab_ref):
    xb = _bf16(x_ref[...])
    cos_t, sin_t = _rope_patterns(tab_ref[...].T)
    q = _dot(xb, w_ref[:, 0:Q_END])
    q_rot = jnp.concatenate(
        [_rope(q[:, p * LANES:(p + 1) * LANES], cos_t, sin_t) for p in range(ATTN_WIDTH // LANES)], axis=1)
    q_out = _bf16(q_rot * (HEAD_DIM ** -0.5))
    k = _rope(_dot(xb, w_ref[:, Q_END:K_END]), cos_t, sin_t)
    v = _dot(xb, w_ref[:, K_END:V_END])
    bg = _dot(xb, w_ref[:, V_END:B_END])
    u = _dot(xb, w_ref[:, B_END:C_END]) * _dot(xb, w_ref[:, C_END:IN_WIDTH])
    return q_out, k, v, bg, u


def _conv3(bg, u, u1, u2, cw_ref):
    cw = cw_ref[...]
    return bg * (cw[0:1, :] * u2 + cw[1:2, :] * u1 + cw[2:3, :] * u)


def _proj_prompt_kernel(x_ref, w_ref, tab_ref, cw_ref,
                        q_ref, kx_ref, vx_ref, conv_ref, ktail_ref, vtail_ref, utail_ref, carry_ref):
    @pl.when(pl.program_id(0) == 0)
    def _():
        carry_ref[...] = jnp.zeros_like(carry_ref)

    q_out, k, v, bg, u = _proj_common(x_ref, w_ref, tab_ref)
    tm = u.shape[0]
    ext = jnp.concatenate([carry_ref[...], u], axis=0)
    u1 = pltpu.roll(ext, 1, axis=0)[SUBLANES:SUBLANES + tm]
    u2 = pltpu.roll(ext, 2, axis=0)[SUBLANES:SUBLANES + tm]
    q_ref[...] = q_out
    kx_ref[...] = _head_slabs(k)
    vx_ref[...] = _head_slabs(v)
    conv_ref[...] = _bf16(_conv3(bg, u, u1, u2, cw_ref))
    ktail_ref[...] = k[tm - WINDOW:tm]
    vtail_ref[...] = v[tm - WINDOW:tm]
    utail_ref[...] = u[tm - SUBLANES:tm]
    carry_ref[...] = u[tm - SUBLANES:tm]


def _proj_sample_kernel(x_ref, w_ref, tab_ref, cw_ref, c0_ref, c1_ref,
                        q_ref, k_ref, v_ref, conv_ref, u_ref):
    q_out, k, v, bg, u = _proj_common(x_ref, w_ref, tab_ref)
    t = lax.broadcasted_iota(jnp.int32, u.shape, 0) % DEC_SEQ
    c0 = c0_ref[...]
    c1 = c1_ref[...]
    u1 = jnp.where(t >= 1, pltpu.roll(u, 1, axis=0), c1)
    u2 = jnp.where(t >= 2, pltpu.roll(u, 2, axis=0), jnp.where(t == 1, c1, c0))
    q_ref[...] = q_out.astype(jnp.float32)
    k_ref[...] = k
    v_ref[...] = v
    conv_ref[...] = _bf16(_conv3(bg, u, u1, u2, cw_ref))
    u_ref[...] = u


def _rope_table(pos):
    half = ROPE_DIM // 2
    inv = ROPE_THETA ** (-jnp.arange(0, ROPE_DIM, 2, dtype=jnp.float32) / ROPE_DIM)
    ang = pos.astype(jnp.float32)[None, :] * inv[:, None]
    cos, sin = jnp.cos(ang), jnp.sin(ang)
    n = pos.shape[0]
    assert ROPE_ONE == 3 * half
    return jnp.concatenate([cos, sin, -sin, jnp.ones((1, n), jnp.float32),
                            jnp.zeros((LANES - ROPE_ONE - 1, n), jnp.float32)], axis=0)


def _proj_prompt(x, w_in_b, tab, conv_w):
    n = x.shape[0]
    tm = TM_PROJ
    row = lambda w: pl.BlockSpec((tm, w), lambda i: (i, 0))
    full = lambda a: pl.BlockSpec(a.shape, lambda i: (0,) * a.ndim)
    const = lambda r, w: pl.BlockSpec((r, w), lambda i: (0, 0))
    return pl.pallas_call(
        _proj_prompt_kernel,
        grid=(n // tm,),
        in_specs=[row(D_MODEL), full(w_in_b), pl.BlockSpec((LANES, tm), lambda i: (0, i)), full(conv_w)],
        out_specs=[row(ATTN_WIDTH), row(4 * LANES), row(4 * LANES), row(CONV_CH),
                   const(WINDOW, KV_WIDTH), const(WINDOW, KV_WIDTH), const(SUBLANES, CONV_CH)],
        out_shape=[jax.ShapeDtypeStruct((n, ATTN_WIDTH), jnp.bfloat16),
                   jax.ShapeDtypeStruct((n, 4 * LANES), jnp.bfloat16),
                   jax.ShapeDtypeStruct((n, 4 * LANES), jnp.bfloat16),
                   jax.ShapeDtypeStruct((n, CONV_CH), jnp.bfloat16),
                   jax.ShapeDtypeStruct((WINDOW, KV_WIDTH), jnp.float32),
                   jax.ShapeDtypeStruct((WINDOW, KV_WIDTH), jnp.float32),
                   jax.ShapeDtypeStruct((SUBLANES, CONV_CH), jnp.float32)],
        scratch_shapes=[pltpu.VMEM((SUBLANES, CONV_CH), jnp.float32)],
        compiler_params=_params(("arbitrary",)),
        name="proj_prompt",
    )(x, w_in_b, tab, conv_w)


def _proj_sample(x, w_in_b, tab, conv_w, c0, c1):
    n = x.shape[0]
    full = lambda a: pl.BlockSpec(a.shape, lambda i: (0,) * a.ndim)
    out = lambda w, dt: jax.ShapeDtypeStruct((n, w), dt)
    blk = lambda w: pl.BlockSpec((n, w), lambda i: (0, 0))
    return pl.pallas_call(
        _proj_sample_kernel,
        grid=(1,),
        in_specs=[full(x), full(w_in_b), full(tab), full(conv_w), full(c0), full(c1)],
        out_specs=[blk(ATTN_WIDTH), blk(KV_WIDTH), blk(KV_WIDTH), blk(CONV_CH), blk(CONV_CH)],
        out_shape=[out(ATTN_WIDTH, jnp.float32), out(KV_WIDTH, jnp.float32), out(KV_WIDTH, jnp.float32),
                   out(CONV_CH, jnp.bfloat16), out(CONV_CH, jnp.float32)],
        compiler_params=_params(("arbitrary",)),
        name="proj_sample",
    )(x, w_in_b, tab, conv_w, c0, c1)


SWA_QB = 2


def _swa_prompt_kernel(sinks_ref, q_ref, kc_ref, kp_ref, vc_ref, vp_ref, o_ref):
    step = pl.program_id(0)
    kall = jnp.concatenate([kp_ref[...], kc_ref[...]], axis=0)
    vall = jnp.concatenate([vp_ref[...], vc_ref[...]], axis=0)
    i = lax.broadcasted_iota(jnp.int32, (WINDOW, WINDOW), 0)
    j = lax.broadcasted_iota(jnp.int32, (WINDOW, WINDOW), 1)
    from_prev = j > i
    for sb in range(SWA_QB):
        rows = slice(sb * WINDOW, (sb + 1) * WINDOW)
        kprev, kcur = kall[sb * WINDOW:(sb + 1) * WINDOW], kall[(sb + 1) * WINDOW:(sb + 2) * WINDOW]
        vprev, vcur = vall[sb * WINDOW:(sb + 1) * WINDOW], vall[(sb + 1) * WINDOW:(sb + 2) * WINDOW]
        for p in range(N_HEADS // 2):
            qs = q_ref[rows, p * LANES:(p + 1) * LANES]
            acc = None
            for e in range(2):
                hd = 2 * p + e
                slab = 2 * (hd // (N_HEADS // N_KV_HEADS)) + e
                sl = slice(slab * LANES, (slab + 1) * LANES)
                s = jnp.where(from_prev, _dot_nt(qs, kprev[:, sl]), _dot_nt(qs, kcur[:, sl]))
                if sb == 0:
                    s = jnp.where(from_prev & (step == 0), -jnp.inf, s)
                sink = sinks_ref[hd]
                m = jnp.maximum(jnp.max(s, axis=1, keepdims=True), sink)
                pe = jnp.exp(s - m)
                den = jnp.sum(pe, axis=1, keepdims=True) + jnp.exp(sink - m)
                p_prev = jnp.where(from_prev, pe, 0.0)
                o = (_dot(_bf16(p_prev), vprev[:, sl]) + _dot(_bf16(pe - p_prev), vcur[:, sl])) / den
                acc = o if acc is None else acc + o
            o_ref[rows, p * LANES:(p + 1) * LANES] = _bf16(acc)


def _swa_prompt(sinks, q, kx, vx):
    n = q.shape[0]
    nb = n // (SWA_QB * WINDOW)
    cur = lambda w: pl.BlockSpec((SWA_QB * WINDOW, w), lambda i: (i, 0))
    prev = lambda w: pl.BlockSpec((WINDOW, w), lambda i: (jnp.maximum(SWA_QB * i - 1, 0), 0))
    return pl.pallas_call(
        _swa_prompt_kernel,
        grid=(nb,),
        in_specs=[pl.BlockSpec(memory_space=pltpu.SMEM), cur(ATTN_WIDTH),
                  cur(4 * LANES), prev(4 * LANES), cur(4 * LANES), prev(4 * LANES)],
        out_specs=cur(ATTN_WIDTH),
        out_shape=jax.ShapeDtypeStruct((n, ATTN_WIDTH), jnp.bfloat16),
        compiler_params=_params(("arbitrary",)),
        name="swa_prompt",
    )(sinks, q, kx, kx, vx, vx)


SWA_ROWS = N_HEADS * DEC_SEQ
NEW_ROWS = 2 * SUBLANES


def _swa_sample_kernel(q_ref, sink_ref, kn_ref, vn_ref, kt_ref, vt_ref, o_ref, okt_ref, ovt_ref):
    t = lax.broadcasted_iota(jnp.int32, (SWA_ROWS, WINDOW), 0) % DEC_SEQ
    j = lax.broadcasted_iota(jnp.int32, (SWA_ROWS, WINDOW), 1)
    valid_c = j > t
    valid_n = (lax.broadcasted_iota(jnp.int32, (SWA_ROWS, NEW_ROWS), 1)
               <= lax.broadcasted_iota(jnp.int32, (SWA_ROWS, NEW_ROWS), 0) % DEC_SEQ)
    lane = lax.broadcasted_iota(jnp.int32, (KV_WIDTH, WINDOW), 1)
    sink = sink_ref[:, 0:1]
    shift = WINDOW - DEC_SEQ
    zrows = jnp.zeros((KV_WIDTH - NEW_ROWS, KV_WIDTH), jnp.float32)
    for b in range(SWA_BB):
        q = _bf16(q_ref[b])
        kt, vt = kt_ref[b], vt_ref[b]
        kn, vn = kn_ref[b], vn_ref[b]
        s_c = jnp.where(valid_c, _dot(q, _bf16(kt)), -jnp.inf)
        s_n = jnp.where(valid_n, _dot_nt(q, _bf16(kn)), -jnp.inf)
        m = jnp.maximum(jnp.maximum(jnp.max(s_c, axis=1, keepdims=True), jnp.max(s_n, axis=1, keepdims=True)), sink)
        p_c = jnp.exp(s_c - m)
        p_n = jnp.exp(s_n - m)
        den = jnp.sum(p_c, axis=1, keepdims=True) + jnp.sum(p_n, axis=1, keepdims=True) + jnp.exp(sink - m)
        o_ref[b] = (_dot_nt(_bf16(p_c), _bf16(vt)) + _dot(_bf16(p_n), _bf16(vn))) / den
        for old, new, dst in ((kt, kn, okt_ref), (vt, vn, ovt_ref)):
            new_cols = pltpu.roll(jnp.concatenate([new, zrows], axis=0).T, shift, axis=1)
            dst[b] = jnp.where(lane >= shift, new_cols, pltpu.roll(old, shift, axis=1))


def _swa_sample(sinks, q, kn, vn, cache_k, cache_v):
    nb = cache_k.shape[0]
    bb = SWA_BB
    groups = N_HEADS // N_KV_HEADS
    qh = q.reshape(nb, DEC_SEQ, N_KV_HEADS, groups, HEAD_DIM).transpose(0, 2, 3, 1, 4)
    qh = qh.reshape(nb, N_KV_HEADS, groups * DEC_SEQ, HEAD_DIM)
    zeros = jnp.zeros_like(qh[:, 0])
    qbd = jnp.concatenate([jnp.concatenate([qh[:, 0], zeros], axis=-1),
                           jnp.concatenate([zeros, qh[:, 1]], axis=-1)], axis=1)
    sink_col = jnp.broadcast_to(jnp.repeat(sinks, DEC_SEQ).reshape(SWA_ROWS, 1), (SWA_ROWS, LANES))
    pad8 = lambda a: jnp.pad(a.reshape(nb, DEC_SEQ, KV_WIDTH), ((0, 0), (0, NEW_ROWS - DEC_SEQ), (0, 0)))
    to_t = lambda c: c.transpose(0, 2, 3, 1).reshape(nb, KV_WIDTH, WINDOW)
    blk = lambda r, w: pl.BlockSpec((bb, r, w), lambda i: (i, 0, 0))
    o, okt, ovt = pl.pallas_call(
        _swa_sample_kernel,
        grid=(nb // bb,),
        in_specs=[blk(SWA_ROWS, KV_WIDTH), pl.BlockSpec((SWA_ROWS, LANES), lambda i: (0, 0)),
                  blk(NEW_ROWS, KV_WIDTH), blk(NEW_ROWS, KV_WIDTH), blk(KV_WIDTH, WINDOW), blk(KV_WIDTH, WINDOW)],
        out_specs=[blk(SWA_ROWS, KV_WIDTH), blk(KV_WIDTH, WINDOW), blk(KV_WIDTH, WINDOW)],
        out_shape=[jax.ShapeDtypeStruct((nb, SWA_ROWS, KV_WIDTH), jnp.float32),
                   jax.ShapeDtypeStruct((nb, KV_WIDTH, WINDOW), jnp.float32),
                   jax.ShapeDtypeStruct((nb, KV_WIDTH, WINDOW), jnp.float32)],
        compiler_params=_params(("arbitrary",)),
        name="swa_sample",
    )(qbd, sink_col, pad8(kn), pad8(vn), to_t(cache_k), to_t(cache_v))
    o = o.reshape(nb, N_KV_HEADS, groups, DEC_SEQ, N_KV_HEADS, HEAD_DIM)
    attn = jnp.stack([o[:, h, :, :, h, :] for h in range(N_KV_HEADS)], axis=1)
    attn = attn.transpose(0, 3, 1, 2, 4).reshape(nb * DEC_SEQ, ATTN_WIDTH)
    from_t = lambda c: c.reshape(nb, N_KV_HEADS, HEAD_DIM, WINDOW).transpose(0, 3, 1, 2)
    return attn, from_t(okt), from_t(ovt)


def _mem_kv_kernel(mem_ref, wk_ref, wv_ref, mk_ref, mv_ref, mkb_ref, mvb_ref):
    mb = _bf16(mem_ref[...])
    mk = _dot(mb, wk_ref[...])
    mv = _dot(mb, wv_ref[...])
    mk_ref[...] = mk
    mv_ref[...] = mv
    mkb_ref[...] = _bf16(mk)
    mvb_ref[...] = _bf16(mv)


def _mem_kv(mem, wk_b, wv_b):
    full = lambda a: pl.BlockSpec(a.shape, lambda i: (0,) * a.ndim)
    blk = pl.BlockSpec((N_MEM, D_MODEL), lambda i: (0, 0))
    f32 = jax.ShapeDtypeStruct((N_MEM, D_MODEL), jnp.float32)
    b16 = jax.ShapeDtypeStruct((N_MEM, D_MODEL), jnp.bfloat16)
    return pl.pallas_call(
        _mem_kv_kernel,
        grid=(1,),
        in_specs=[full(mem), full(wk_b), full(wv_b)],
        out_specs=[blk, blk, blk, blk],
        out_shape=[f32, f32, b16, b16],
        compiler_params=_params(("arbitrary",)),
        name="mem_kv",
    )(mem, wk_b, wv_b)


def _mix_ln1(attn_ref, conv_ref, x_ref, wmix_ref, g1_ref, b1_ref):
    mix = _dot(_bf16(attn_ref[...]), wmix_ref[0:ATTN_WIDTH, :]) + _dot(conv_ref[...], wmix_ref[ATTN_WIDTH:, :])
    return _layer_norm(ALPHA * x_ref[...] + mix, g1_ref[...], b1_ref[...])


def _mem_q(h1, wq_ref):
    return _bf16(_dot(_bf16(h1), wq_ref[...]) * (MEM_HEAD_DIM ** -0.5))


def _route(h2, wrhl_ref, br_ref, tri_ref, carry):
    hi = _bf16(h2)
    lo = _bf16(h2 - hi.astype(jnp.float32))
    hh = _dot(hi, wrhl_ref[...])
    logits = hh[:, 0:LANES] + hh[:, LANES:] + _dot(lo, wrhl_ref[:, 0:LANES]) + br_ref[...]
    lane_i = lax.broadcasted_iota(jnp.int32, logits.shape, 1)
    lane = lane_i.astype(jnp.float32)
    big = jnp.float32(LANES)
    is_g = lane_i < N_GROUPS
    gl = jnp.where(is_g, logits, -jnp.inf)
    gmax = jnp.max(gl, axis=1, keepdims=True)
    gidx = jnp.min(jnp.where(is_g & (logits == gmax), lane, big), axis=1, keepdims=True)
    gsum = jnp.sum(jnp.exp(gl - gmax), axis=1, keepdims=True)
    gw = 1.0 / gsum
    eid = lane_i - N_GROUPS
    assert EXPERTS_PER_GROUP == 8
    grp = lax.shift_right_arithmetic(eid, jnp.full_like(eid, 3)).astype(jnp.float32)
    in_e = (lane_i >= N_GROUPS) & (lane_i < N_GROUPS + N_EXPERTS) & (grp == gidx)
    v1 = jnp.max(jnp.where(in_e, logits, -jnp.inf), axis=1, keepdims=True)
    i1 = jnp.min(jnp.where(in_e & (logits == v1), lane, big), axis=1, keepdims=True)
    rest = in_e & (lane != i1)
    v2 = jnp.max(jnp.where(rest, logits, -jnp.inf), axis=1, keepdims=True)
    i2 = jnp.min(jnp.where(rest & (logits == v2), lane, big), axis=1, keepdims=True)
    ex = jnp.exp(v2 - v1)
    den = 1.0 + ex
    w1 = gw / den
    w2 = gw * ex / den
    zero = jnp.zeros_like(logits)
    pick1 = lane == i1
    pick2 = lane == i2
    sel = jnp.where(pick1 | pick2, 1.0, 0.0)
    before = _dot(tri_ref[...], _bf16(sel)) + carry
    rank1 = jnp.sum(jnp.where(pick1, before, zero), axis=1, keepdims=True)
    rank2 = jnp.sum(jnp.where(pick2, before, zero), axis=1, keepdims=True)
    cols = (i1 - N_GROUPS, i2 - N_GROUPS, w1, w2, rank1, rank2)
    route = zero
    for k, col in enumerate(cols):
        route = jnp.where(lane_i == k, col, route)
    return route, carry + jnp.sum(sel, axis=0, keepdims=True)


def _post_prompt_kernel(attn_ref, conv_ref, x_ref, wmix_ref, g1_ref, b1_ref, wq_ref, mk_ref, mv_ref, wo_ref,
                        g2_ref, b2_ref, wrhl_ref, br_ref, tri_ref, h2s_ref, rts_ref, cnts_ref,
                        h2t_ref, rt_ref, cnt_ref, carry_ref):
    is_tail = pl.program_id(0) == N_PROMPT // TM_POST

    @pl.when(pl.program_id(0) == 0)
    def _():
        carry_ref[...] = cnts_ref[...]

    @pl.when(is_tail)
    def _():
        h2t_ref[...] = h2s_ref[...]
        rt_ref[...] = rts_ref[...]

    @pl.when(jnp.logical_not(is_tail))
    def _():
        h1 = _mix_ln1(attn_ref, conv_ref, x_ref, wmix_ref, g1_ref, b1_ref)
        qm = _mem_q(h1, wq_ref)
        outs = []
        for h in range(MEM_HEADS):
            sl = slice(h * MEM_HEAD_DIM, (h + 1) * MEM_HEAD_DIM)
            s = _dot_nt(qm[:, sl], mk_ref[:, sl])
            m = jnp.max(s, axis=1, keepdims=True)
            p = jnp.exp(s - m)
            den = jnp.sum(p, axis=1, keepdims=True)
            outs.append(_dot(_bf16(p), mv_ref[:, sl]) / den)
        o = _bf16(jnp.concatenate(outs, axis=1))
        h2 = _layer_norm(ALPHA * h1 + _dot(o, wo_ref[...]), g2_ref[...], b2_ref[...])
        _store_token_tiles(h2t_ref, h2)
        route, carry = _route(h2, wrhl_ref, br_ref, tri_ref, carry_ref[0:1, :])
        rt_ref[...] = route
        carry_ref[...] = jnp.broadcast_to(carry, carry_ref.shape)
        cnt_ref[...] = jnp.broadcast_to(carry, cnt_ref.shape)


def _post_prompt(attn, conv, x, wmix_b, g1, b1, wq_b, mk_b, mv_b, wo_b, g2, b2, wrhl, br, tri, h2t_s, rt_s, cnt_s):
    n = x.shape[0]
    tm = TM_POST
    steps = n // tm
    row = lambda w: pl.BlockSpec((tm, w), lambda i: (jnp.minimum(i, steps - 1), 0))
    full = lambda a: pl.BlockSpec(a.shape, lambda i: (0,) * a.ndim)
    weights = (wmix_b, g1, b1, wq_b, mk_b, mv_b, wo_b, g2, b2, wrhl, br, tri, h2t_s, rt_s, cnt_s)
    n_out = n + h2t_s.shape[0] // ROW_CHUNKS
    return pl.pallas_call(
        _post_prompt_kernel,
        grid=(steps + 1,),
        in_specs=[row(ATTN_WIDTH), row(CONV_CH), row(D_MODEL)] + [full(a) for a in weights],
        out_specs=[pl.BlockSpec((tm * ROW_CHUNKS, LANES), lambda i: (i, 0)),
                   pl.BlockSpec((tm, LANES), lambda i: (i, 0)),
                   pl.BlockSpec((SUBLANES, LANES), lambda i: (0, 0))],
        out_shape=[jax.ShapeDtypeStruct((n_out * ROW_CHUNKS, LANES), jnp.float32),
                   jax.ShapeDtypeStruct((n_out, LANES), jnp.float32),
                   jax.ShapeDtypeStruct((SUBLANES, LANES), jnp.float32)],
        scratch_shapes=[pltpu.VMEM((SUBLANES, LANES), jnp.float32)],
        compiler_params=_params(("arbitrary",)),
        name="post_prompt",
    )(attn, conv, x, *weights)


def _post_a_sample_kernel(attn_ref, conv_ref, x_ref, wmix_ref, g1_ref, b1_ref, wq_ref, h1_ref, qm_ref):
    h1 = _mix_ln1(attn_ref, conv_ref, x_ref, wmix_ref, g1_ref, b1_ref)
    h1_ref[...] = h1
    qm_ref[...] = _mem_q(h1, wq_ref).astype(jnp.float32)


def _post_a_sample(attn, conv, x, wmix_b, g1, b1, wq_b):
    n = x.shape[0]
    args = (attn, conv, x, wmix_b, g1, b1, wq_b)
    full = lambda a: pl.BlockSpec(a.shape, lambda i: (0,) * a.ndim)
    blk = pl.BlockSpec((n, D_MODEL), lambda i: (0, 0))
    return pl.pallas_call(
        _post_a_sample_kernel,
        grid=(1,),
        in_specs=[full(a) for a in args],
        out_specs=[blk, blk],
        out_shape=[jax.ShapeDtypeStruct((n, D_MODEL), jnp.float32),
                   jax.ShapeDtypeStruct((n, D_MODEL), jnp.float32)],
        compiler_params=_params(("arbitrary",)),
        name="post_a_sample",
    )(*args)


MEM_ROWS = MEM_HEADS * DEC_SEQ


def _mem_attn_sample_kernel(q_ref, mk_ref, mv_ref, o_ref):
    nk = N_MEM * MEM_HEADS
    row_h = lax.broadcasted_iota(jnp.int32, (MEM_ROWS, nk), 0) // DEC_SEQ
    key_h = lax.broadcasted_iota(jnp.int32, (MEM_ROWS, nk), 1) % MEM_HEADS
    own = row_h == key_h
    for b in range(SAMPLE_BB):
        k2 = _bf16(mk_ref[b].reshape(nk, MEM_HEAD_DIM))
        v2 = _bf16(mv_ref[b].reshape(nk, MEM_HEAD_DIM))
        s = jnp.where(own, _dot_nt(_bf16(q_ref[b]), k2), -jnp.inf)
        m = jnp.max(s, axis=1, keepdims=True)
        p = jnp.exp(s - m)
        den = jnp.sum(p, axis=1, keepdims=True)
        o_ref[b] = _dot(_bf16(p), v2) / den


def _mem_attn_sample(qm, mk, mv):
    nb = mk.shape[0]
    bb = SAMPLE_BB
    q = qm.reshape(nb, DEC_SEQ, MEM_HEADS, MEM_HEAD_DIM).transpose(0, 2, 1, 3).reshape(nb, MEM_ROWS, MEM_HEAD_DIM)
    rows = pl.BlockSpec((bb, MEM_ROWS, MEM_HEAD_DIM), lambda i: (i, 0, 0))
    kv = pl.BlockSpec((bb, N_MEM, MEM_HEADS, MEM_HEAD_DIM), lambda i: (i, 0, 0, 0))
    o = pl.pallas_call(
        _mem_attn_sample_kernel,
        grid=(nb // bb,),
        in_specs=[rows, kv, kv],
        out_specs=rows,
        out_shape=jax.ShapeDtypeStruct((nb, MEM_ROWS, MEM_HEAD_DIM), jnp.float32),
        compiler_params=_params(("arbitrary",)),
        name="mem_attn_sample",
    )(q, mk, mv)
    return o.reshape(nb, MEM_HEADS, DEC_SEQ, MEM_HEAD_DIM).transpose(0, 2, 1, 3).reshape(nb * DEC_SEQ, D_MODEL)


def _post_b_sample_kernel(o_ref, h1_ref, wo_ref, g2_ref, b2_ref, wrhl_ref, br_ref, tri_ref,
                          h2t_ref, rt_ref, cnt_ref):
    h2 = _layer_norm(ALPHA * h1_ref[...] + _dot(_bf16(o_ref[...]), wo_ref[...]), g2_ref[...], b2_ref[...])
    _store_token_tiles(h2t_ref, h2)
    route, carry = _route(h2, wrhl_ref, br_ref, tri_ref, jnp.zeros((1, LANES), jnp.float32))
    rt_ref[...] = route
    cnt_ref[...] = jnp.broadcast_to(carry, cnt_ref.shape)


def _post_b_sample(o, h1, wo_b, g2, b2, wrhl, br, tri):
    n = h1.shape[0]
    args = (o, h1, wo_b, g2, b2, wrhl, br, tri)
    full = lambda a: pl.BlockSpec(a.shape, lambda i: (0,) * a.ndim)
    return pl.pallas_call(
        _post_b_sample_kernel,
        grid=(1,),
        in_specs=[full(a) for a in args],
        out_specs=[pl.BlockSpec((n * ROW_CHUNKS, LANES), lambda i: (0, 0)),
                   pl.BlockSpec((n, LANES), lambda i: (0, 0)),
                   pl.BlockSpec((SUBLANES, LANES), lambda i: (0, 0))],
        out_shape=[jax.ShapeDtypeStruct((n * ROW_CHUNKS, LANES), jnp.float32),
                   jax.ShapeDtypeStruct((n, LANES), jnp.float32),
                   jax.ShapeDtypeStruct((SUBLANES, LANES), jnp.float32)],
        compiler_params=_params(("arbitrary",)),
        name="post_b_sample",
    )(*args)


def _row_gather_copy(src_hbm, idx, dst, dst_row, sem):
    s0 = pl.multiple_of(idx * ROW_CHUNKS, ROW_CHUNKS)
    d0 = pl.multiple_of(dst_row * ROW_CHUNKS, ROW_CHUNKS)
    return pltpu.make_async_copy(src_hbm.at[pl.ds(s0, ROW_CHUNKS), :], dst.at[pl.ds(d0, ROW_CHUNKS), :], sem)


def _dispatch_kernel(pos_ref, h2t_ref, xs_hbm, sem):
    def body(r, c):
        src = h2t_ref.at[pl.ds(pl.multiple_of(r * ROW_CHUNKS, ROW_CHUNKS), ROW_CHUNKS), :]
        for k in range(2):
            d0 = pl.multiple_of(pos_ref[0, 0, k * TM_COMB + r] * ROW_CHUNKS, ROW_CHUNKS)
            pltpu.make_async_copy(src, xs_hbm.at[pl.ds(d0, ROW_CHUNKS), :], sem.at[0]).start(priority=k)
        return c
    lax.fori_loop(0, TM_COMB, body, 0, unroll=8)
    for _ in range(2):
        pltpu.make_async_copy(h2t_ref, xs_hbm.at[pl.ds(0, TM_COMB * ROW_CHUNKS), :], sem.at[0]).wait()


def _dispatch(pos3, h2t):
    nt = N_ALL // TM_COMB
    return pl.pallas_call(
        _dispatch_kernel,
        grid=(nt,),
        in_specs=[pl.BlockSpec((1, 1, 2 * TM_COMB), lambda i: (i, 0, 0), memory_space=pltpu.SMEM),
                  pl.BlockSpec((TM_COMB * ROW_CHUNKS, LANES), lambda i: (i, 0))],
        out_specs=pl.BlockSpec(memory_space=pl.ANY),
        out_shape=jax.ShapeDtypeStruct((N_ASSIGN * ROW_CHUNKS, LANES), jnp.float32),
        scratch_shapes=[pltpu.SemaphoreType.DMA((1,))],
        compiler_params=_params(("arbitrary",)),
        name="moe_dispatch",
    )(pos3, h2t)


def _moe_ffn_kernel(it_ref, ie_ref, lo_ref, hi_ref, x_ref, wg_ref, wu_ref, wd_ref, y_ref, wgb, wub, wdb, cur_e):
    i = pl.program_id(0)
    lo = lo_ref[i]
    hi = hi_ref[i]
    e = ie_ref[i]

    @pl.when(i == 0)
    def _():
        cur_e[0] = -1

    @pl.when((hi > lo) & (cur_e[0] != e))
    def _():
        wgb[...] = _bf16(wg_ref[0])
        wub[...] = _bf16(wu_ref[0])
        wdb[...] = _bf16(wd_ref[0])
        cur_e[0] = e

    def ffn():
        x = _bf16(_load_token_tiles(x_ref, 0, TM_MOE))
        hg = _dot(x, wgb[...])
        hu = _dot(x, wub[...])
        h = hg / (1.0 + jnp.exp(-hg)) * hu
        return _dot(_bf16(h), wdb[...])

    def rows_mask():
        row = lax.broadcasted_iota(jnp.int32, (TM_MOE, LANES), 0)
        return (row >= lo) & (row < hi)

    @pl.when((hi > lo) & (lo == 0))
    def _():
        y = ffn()
        mask = rows_mask()
        for c in range(ROW_CHUNKS):
            y_ref[pl.ds(c, TM_MOE, stride=ROW_CHUNKS), :] = jnp.where(mask, y[:, c * LANES:(c + 1) * LANES], 0.0)

    @pl.when((hi > lo) & (lo > 0))
    def _():
        y = ffn()
        mask = rows_mask()
        for c in range(ROW_CHUNKS):
            sl = pl.ds(c, TM_MOE, stride=ROW_CHUNKS)
            y_ref[sl, :] = jnp.where(mask, y[:, c * LANES:(c + 1) * LANES], y_ref[sl, :])


def _moe_ffn(item_tile, item_expert, item_lo, item_hi, x_sorted, w_gate, w_up, w_down):
    wspec = lambda shp: pl.BlockSpec((1,) + shp, lambda i, it, ie, lo, hi: (ie[i], 0, 0))
    tile = pl.BlockSpec((TM_MOE * ROW_CHUNKS, LANES), lambda i, it, ie, lo, hi: (it[i], 0))
    grid_spec = pltpu.PrefetchScalarGridSpec(
        num_scalar_prefetch=4,
        grid=(MOE_ITEMS,),
        in_specs=[tile, wspec((D_MODEL, EXPERT_FF)), wspec((D_MODEL, EXPERT_FF)), wspec((EXPERT_FF, D_MODEL))],
        out_specs=tile,
        scratch_shapes=[pltpu.VMEM((D_MODEL, EXPERT_FF), jnp.bfloat16),
                        pltpu.VMEM((D_MODEL, EXPERT_FF), jnp.bfloat16),
                        pltpu.VMEM((EXPERT_FF, D_MODEL), jnp.bfloat16),
                        pltpu.SMEM((1,), jnp.int32)],
    )
    return pl.pallas_call(
        _moe_ffn_kernel,
        grid_spec=grid_spec,
        out_shape=jax.ShapeDtypeStruct((N_ASSIGN * ROW_CHUNKS, LANES), jnp.float32),
        compiler_params=_params(("arbitrary",)),
        name="moe_ffn",
    )(item_tile, item_expert, item_lo, item_hi, x_sorted, w_gate, w_up, w_down)


def _combine_kernel(nt, pos_cur_ref, pos_nxt_ref, yt_hbm, h2t_ref, rt_ref, g3_ref, b3_ref, o_ref, abuf, sem):
    t = pl.program_id(0)
    slot = t % 2
    rows = 2 * TM_COMB

    def issue(pos_ref, s):
        def body(j, c):
            for k in range(2):
                r = 2 * j + k
                _row_gather_copy(yt_hbm, pos_ref[0, 0, r], abuf, s * rows + r, sem.at[s]).start(priority=k)
            return c
        lax.fori_loop(0, rows // 2, body, 0, unroll=4)

    @pl.when(t == 0)
    def _():
        issue(pos_cur_ref, 0)

    @pl.when(t + 1 < nt)
    def _():
        issue(pos_nxt_ref, 1 - slot)

    base = pl.multiple_of(slot * (rows * ROW_CHUNKS), rows * ROW_CHUNKS)
    pltpu.make_async_copy(yt_hbm.at[pl.ds(0, rows * ROW_CHUNKS), :],
                          abuf.at[pl.ds(base, rows * ROW_CHUNKS), :], sem.at[slot]).wait()
    ya = _load_token_tiles(abuf, base, TM_COMB)
    yb = _load_token_tiles(abuf, base + TM_COMB * ROW_CHUNKS, TM_COMB)
    rt = rt_ref[...]
    ff = rt[:, 2:3] * ya + rt[:, 3:4] * yb
    h2 = _load_token_tiles(h2t_ref, 0, TM_COMB)
    o_ref[...] = _layer_norm(ALPHA * h2 + ff, g3_ref[...], b3_ref[...])


def _combine(pos3, yt, h2t, rt, g3, b3, tile0, n_tiles):
    last = tile0 + n_tiles - 1
    smem_pos = lambda f: pl.BlockSpec((1, 1, 2 * TM_COMB), f, memory_space=pltpu.SMEM)
    full = lambda a: pl.BlockSpec(a.shape, lambda i: (0,) * a.ndim)
    return pl.pallas_call(
        functools.partial(_combine_kernel, n_tiles),
        grid=(n_tiles,),
        in_specs=[smem_pos(lambda i: (tile0 + i, 0, 0)),
                  smem_pos(lambda i: (jnp.minimum(tile0 + i + 1, last), 0, 0)),
                  pl.BlockSpec(memory_space=pl.ANY),
                  pl.BlockSpec((TM_COMB * ROW_CHUNKS, LANES), lambda i: (tile0 + i, 0)),
                  pl.BlockSpec((TM_COMB, LANES), lambda i: (tile0 + i, 0)),
                  full(g3), full(b3)],
        out_specs=pl.BlockSpec((TM_COMB, D_MODEL), lambda i: (i, 0)),
        out_shape=jax.ShapeDtypeStruct((n_tiles * TM_COMB, D_MODEL), jnp.float32),
        scratch_shapes=[pltpu.VMEM((2 * 2 * TM_COMB * ROW_CHUNKS, LANES), jnp.float32),
                        pltpu.SemaphoreType.DMA((2,))],
        compiler_params=_params(("arbitrary",)),
        name="moe_combine",
    )(pos3, pos3, yt, h2t, rt, g3, b3)


POS_TILES = 3


def _positions_kernel(rt_ref, starts_ref, pos_ref):
    lane = lax.broadcasted_iota(jnp.int32, (TM_COMB, LANES), 1)
    lane_f = lane.astype(jnp.float32)
    starts = starts_ref[0:1, :]
    for j in range(POS_TILES):
        rt = rt_ref[j * TM_COMB:(j + 1) * TM_COMB, :]
        cols = []
        for k in range(2):
            seg = jnp.sum(jnp.where(lane_f == rt[:, k:k + 1] + N_GROUPS, starts, 0.0), axis=1, keepdims=True)
            cols.append(seg + rt[:, 4 + k:5 + k])
        packed = jnp.where(lane == 0, cols[0], jnp.where(lane == 1, cols[1], 0.0))
        rows = packed.T
        pos_ref[j] = jnp.concatenate([rows[0:1, :], rows[1:2, :]], axis=1).astype(jnp.int32)


def _positions(rt, starts_row):
    nt = N_ALL // TM_COMB
    assert nt % POS_TILES == 0
    return pl.pallas_call(
        _positions_kernel,
        grid=(nt // POS_TILES,),
        in_specs=[pl.BlockSpec((POS_TILES * TM_COMB, LANES), lambda i: (i, 0)),
                  pl.BlockSpec((SUBLANES, LANES), lambda i: (0, 0))],
        out_specs=pl.BlockSpec((POS_TILES, 1, 2 * TM_COMB), lambda i: (i, 0, 0)),
        out_shape=jax.ShapeDtypeStruct((nt, 1, 2 * TM_COMB), jnp.int32),
        compiler_params=_params(("arbitrary",)),
        name="moe_positions",
    )(rt, starts_row)


def _routing_plan(rt, cnt):
    i32 = jnp.int32
    counts_f = cnt[0, N_GROUPS:N_GROUPS + N_EXPERTS]
    starts_f = jnp.cumsum(counts_f) - counts_f
    starts_row = jnp.broadcast_to(
        jnp.pad(starts_f, (N_GROUPS, LANES - N_GROUPS - N_EXPERTS))[None, :], (SUBLANES, LANES))
    pos3 = _positions(rt, starts_row)
    starts = starts_f.astype(i32)
    tiles = jnp.arange(MOE_TILES, dtype=i32) * TM_MOE
    rank_t = jnp.arange(MOE_TILES, dtype=i32) + jnp.sum((starts[None, :] < tiles[:, None]).astype(i32), axis=1)
    rank_s = jnp.arange(N_EXPERTS, dtype=i32) + jnp.sum((tiles[None, :] <= starts[:, None]).astype(i32), axis=1)
    vals = jnp.concatenate([tiles, starts])
    ranks = jnp.concatenate([rank_t, rank_s])
    slot = jnp.arange(MOE_ITEMS, dtype=i32)
    lo = jnp.sum(jnp.where(ranks[None, :] == slot[:, None], vals[None, :], 0), axis=1)
    hi = jnp.concatenate([lo[1:], jnp.full((1,), N_ASSIGN, i32)])
    item_tile = jnp.minimum(lo // TM_MOE, MOE_TILES - 1)
    item_expert = jnp.clip(jnp.sum((starts[None, :] <= lo[:, None]).astype(i32), axis=1) - 1, 0, N_EXPERTS - 1)
    base = item_tile * TM_MOE
    return item_tile, item_expert, lo - base, hi - base, pos3


def kernel(x_prompt, x_sample, mem_prompt, cache_swa_k, cache_swa_v, cache_conv, cache_mem_k, cache_mem_v,
           w_in, sinks, conv_w, w_mix_out, ln1_g, ln1_b, w_q_mem, w_k_mem, w_v_mem, w_o_mem, ln2_g, ln2_b,
           w_router_group, b_router_group, w_router_expert, b_router_expert, w_gate, w_up, w_down,
           ln3_g, ln3_b):
    f32 = jnp.float32
    row = lambda a: a.reshape(1, -1).astype(f32)
    w_in_b, wmix_b, wq_b, wk_b, wv_b, wo_b = (_bf16(w) for w in (w_in, w_mix_out, w_q_mem, w_k_mem, w_v_mem, w_o_mem))
    g1, b1, g2, b2, g3, b3 = (row(a) for a in (ln1_g, ln1_b, ln2_g, ln2_b, ln3_g, ln3_b))
    pad = LANES - N_GROUPS - N_EXPERTS
    wr = jnp.concatenate([w_router_group, w_router_expert, jnp.zeros((D_MODEL, pad), f32)], axis=1)
    wrh = _bf16(wr)
    wrhl = jnp.concatenate([wrh, _bf16(wr - wrh.astype(f32))], axis=1)
    br = jnp.concatenate([b_router_group, b_router_expert, jnp.zeros((pad,), f32)]).reshape(1, LANES)

    xs = x_sample.reshape(N_SAMPLE, D_MODEL)
    tab_s = jnp.tile(_rope_table(PAST_LEN + jnp.arange(DEC_SEQ)), (1, DEC_BATCH))
    c0 = jnp.repeat(cache_conv[:, 0], DEC_SEQ, axis=0)
    c1 = jnp.repeat(cache_conv[:, 1], DEC_SEQ, axis=0)
    q_s, k_s, v_s, conv_s, u_s = _proj_sample(xs, w_in_b, tab_s, conv_w, c0, c1)
    attn_s, swa_k_s, swa_v_s = _swa_sample(sinks, q_s, k_s, v_s, cache_swa_k, cache_swa_v)
    h1_s, qm_s = _post_a_sample(attn_s, conv_s, xs, wmix_b, g1, b1, wq_b)
    o_s = _mem_attn_sample(qm_s, cache_mem_k, cache_mem_v)
    tri = _bf16(jnp.tril(jnp.ones((TM_POST, TM_POST), f32), -1))
    h2t_s, rt_s, cnt_s = _post_b_sample(o_s, h1_s, wo_b, g2, b2, wrhl, br, tri)

    xp = x_prompt.reshape(N_PROMPT, D_MODEL)
    tab_p = _rope_table(jnp.arange(N_PROMPT))
    q_p, kx_p, vx_p, conv_p, k_tail, v_tail, u_tail = _proj_prompt(xp, w_in_b, tab_p, conv_w)
    attn_p = _swa_prompt(sinks, q_p, kx_p, vx_p)
    mk, mv, mk_b, mv_b = _mem_kv(mem_prompt.reshape(N_MEM, D_MODEL), wk_b, wv_b)
    h2t, rt, cnt = _post_prompt(attn_p, conv_p, xp, wmix_b, g1, b1, wq_b, mk_b, mv_b, wo_b, g2, b2,
                                wrhl, br, tri, h2t_s, rt_s, cnt_s)

    item_tile, item_expert, item_lo, item_hi, pos3 = _routing_plan(rt, cnt)
    x_sorted = _dispatch(pos3, h2t)
    yt = _moe_ffn(item_tile, item_expert, item_lo, item_hi, x_sorted, w_gate, w_up, w_down)
    y_p = _combine(pos3, yt, h2t, rt, g3, b3, 0, N_PROMPT // TM_COMB)
    y_s = _combine(pos3, yt, h2t, rt, g3, b3, N_PROMPT // TM_COMB, N_SAMPLE // TM_COMB)

    return (y_p.reshape(1, SEQ, D_MODEL),
            y_s.reshape(DEC_BATCH, DEC_SEQ, D_MODEL),
            k_tail.reshape(1, WINDOW, N_KV_HEADS, HEAD_DIM),
            v_tail.reshape(1, WINDOW, N_KV_HEADS, HEAD_DIM),
            u_tail[SUBLANES - (CONV_K - 1):].reshape(1, CONV_K - 1, CONV_CH),
            mk.reshape(1, N_MEM, MEM_HEADS, MEM_HEAD_DIM),
            mv.reshape(1, N_MEM, MEM_HEADS, MEM_HEAD_DIM),
            swa_k_s.reshape(DEC_BATCH, WINDOW, N_KV_HEADS, HEAD_DIM),
            swa_v_s.reshape(DEC_BATCH, WINDOW, N_KV_HEADS, HEAD_DIM),
            u_s.reshape(DEC_BATCH, DEC_SEQ, CONV_CH)[:, DEC_SEQ - (CONV_K - 1):])
```

```python
import functools

import jax
import jax.numpy as jnp
from jax import lax
from jax.experimental import pallas as pl
from jax.experimental.pallas import tpu as pltpu

D_MODEL = 1024
SEQ = 16384
DEC_BATCH = 128
DEC_SEQ = 4
PAST_LEN = 16384
ATTN_WIDTH = 512
CONV_CH = 512
HEAD_DIM = 64
N_HEADS = 8
N_KV_HEADS = 2
KV_WIDTH = 128
WINDOW = 128
ROPE_THETA = 500000.0
ROPE_DIM = 16
CONV_K = 3
Q_END = ATTN_WIDTH
K_END = Q_END + KV_WIDTH
V_END = K_END + KV_WIDTH
B_END = V_END + CONV_CH
C_END = B_END + CONV_CH
IN_WIDTH = C_END + CONV_CH
N_MEM = 256
MEM_HEADS = 4
MEM_HEAD_DIM = 256
N_GROUPS = 4
EXPERTS_PER_GROUP = 8
N_EXPERTS = 32
EXPERT_FF = 256
ALPHA = 2.0 ** 0.25
LN_EPS = 1e-5

LANES = 128
SUBLANES = 8
ROW_CHUNKS = D_MODEL // LANES
VMEM_LIMIT = 56 * 1024 * 1024

N_PROMPT = SEQ
N_SAMPLE = DEC_BATCH * DEC_SEQ
N_ALL = N_PROMPT + N_SAMPLE
TM_PROJ = 512
TM_POST = 512
TM_MOE = 512
TM_COMB = 512
N_ASSIGN = 2 * N_ALL
MOE_TILES = N_ASSIGN // TM_MOE
MOE_ITEMS = MOE_TILES + N_EXPERTS
SAMPLE_BB = 4
SWA_BB = 8

assert ROW_CHUNKS == SUBLANES
assert N_SAMPLE == TM_POST
assert N_ASSIGN % TM_MOE == 0 and N_ALL % TM_COMB == 0


def _params(sem, vmem=VMEM_LIMIT):
    return pltpu.CompilerParams(dimension_semantics=sem, vmem_limit_bytes=vmem)


def _bf16(x):
    return x.astype(jnp.bfloat16)


def _dot(a, b):
    return jnp.dot(a, b, preferred_element_type=jnp.float32)


def _dot_nt(a, b):
    return lax.dot_general(a, b, (((1,), (1,)), ((), ())), preferred_element_type=jnp.float32)


def _layer_norm(x, g, b):
    mu = jnp.mean(x, axis=-1, keepdims=True)
    xc = x - mu
    var = jnp.mean(xc * xc, axis=-1, keepdims=True)
    return xc * lax.rsqrt(var + LN_EPS) * g + b


def _rope(x, cos_t, sin_t):
    lane = lax.broadcasted_iota(jnp.int32, x.shape, 1) % HEAD_DIM
    half = ROPE_DIM // 2
    partner = jnp.where(lane < half, pltpu.roll(x, LANES - half, axis=1), pltpu.roll(x, half, axis=1))
    return x * cos_t + partner * sin_t


def _head_slabs(x):
    lane = lax.broadcasted_iota(jnp.int32, x.shape, 1)
    lo = lane < HEAD_DIM
    sw = pltpu.roll(x, HEAD_DIM, axis=1)
    zero = jnp.zeros_like(x)
    slabs = [jnp.where(lo, x, zero), jnp.where(lo, zero, sw), jnp.where(lo, sw, zero), jnp.where(lo, zero, x)]
    return _bf16(jnp.concatenate(slabs, axis=1))


def _store_token_tiles(ref, val):
    rows = val.shape[0]
    for c in range(ROW_CHUNKS):
        ref[pl.ds(c, rows, stride=ROW_CHUNKS), :] = val[:, c * LANES:(c + 1) * LANES]


def _load_token_tiles(ref, base, rows):
    return jnp.concatenate(
        [ref[pl.ds(base + c, rows, stride=ROW_CHUNKS), :] for c in range(ROW_CHUNKS)], axis=1)


ROPE_ONE = 3 * (ROPE_DIM // 2)
ROPE_ROWS = 32


def _rope_patterns(tab):
    half = ROPE_DIM // 2
    m = lax.broadcasted_iota(jnp.int32, tab.shape, 1) % HEAD_DIM
    idx_c = jnp.where(m < ROPE_DIM, m % half, ROPE_ONE)
    idx_s = jnp.where(m < half, 2 * half + m, jnp.where(m < ROPE_DIM, m, ROPE_ONE + 1))
    return jnp.take_along_axis(tab, idx_c, axis=1), jnp.take_along_axis(tab, idx_s, axis=1)


def _proj_common(x_ref, w_ref, tab_ref):
    xb = _bf16(x_ref[...])
    tab = tab_ref[...]
    pad = jnp.zeros((LANES - tab.shape[0], tab.shape[1]), jnp.float32)
    cos_t, sin_t = _rope_patterns(jnp.concatenate([tab, pad], axis=0).T)
    q = _dot(xb, w_ref[:, 0:Q_END])
    q_rot = jnp.concatenate(
        [_rope(q[:, p * LANES:(p + 1) * LANES], cos_t, sin_t) for p in range(ATTN_WIDTH // LANES)], axis=1)
    q_out = _bf16(q_rot * (HEAD_DIM ** -0.5))
    k = _rope(_dot(xb, w_ref[:, Q_END:K_END]), cos_t, sin_t)
    v = _dot(xb, w_ref[:, K_END:V_END])
    bg = _dot(xb, w_ref[:, V_END:B_END])
    u = _dot(xb, w_ref[:, B_END:C_END]) * _dot(xb, w_ref[:, C_END:IN_WIDTH])
    return q_out, k, v, bg, u


def _conv3(bg, u, u1, u2, cw_ref):
    cw = cw_ref[...]
    return bg * (cw[0:1, :] * u2 + cw[1:2, :] * u1 + cw[2:3, :] * u)


def _proj_prompt_kernel(x_ref, w_ref, tab_ref, cw_ref,
                        q_ref, kx_ref, vx_ref, conv_ref, ktail_ref, vtail_ref, utail_ref, carry_ref):
    @pl.when(pl.program_id(0) == 0)
    def _():
        carry_ref[...] = jnp.zeros_like(carry_ref)

    q_out, k, v, bg, u = _proj_common(x_ref, w_ref, tab_ref)
    tm = u.shape[0]
    ext = jnp.concatenate([carry_ref[...], u], axis=0)
    u1 = pltpu.roll(ext, 1, axis=0)[SUBLANES:SUBLANES + tm]
    u2 = pltpu.roll(ext, 2, axis=0)[SUBLANES:SUBLANES + tm]
    q_ref[...] = q_out
    kx_ref[...] = _head_slabs(k)
    vx_ref[...] = _head_slabs(v)
    conv_ref[...] = _bf16(_conv3(bg, u, u1, u2, cw_ref))
    ktail_ref[...] = k[tm - WINDOW:tm]
    vtail_ref[...] = v[tm - WINDOW:tm]
    utail_ref[...] = u[tm - SUBLANES:tm]
    carry_ref[...] = u[tm - SUBLANES:tm]


def _proj_sample_kernel(x_ref, w_ref, tab_ref, cw_ref, c0_ref, c1_ref,
                        q_ref, k_ref, v_ref, conv_ref, u_ref):
    q_out, k, v, bg, u = _proj_common(x_ref, w_ref, tab_ref)
    t = lax.broadcasted_iota(jnp.int32, u.shape, 0) % DEC_SEQ
    c0 = c0_ref[...]
    c1 = c1_ref[...]
    u1 = jnp.where(t >= 1, pltpu.roll(u, 1, axis=0), c1)
    u2 = jnp.where(t >= 2, pltpu.roll(u, 2, axis=0), jnp.where(t == 1, c1, c0))
    q_ref[...] = q_out.astype(jnp.float32)
    k_ref[...] = k
    v_ref[...] = v
    conv_ref[...] = _bf16(_conv3(bg, u, u1, u2, cw_ref))
    u_ref[...] = u


def _rope_table(pos):
    half = ROPE_DIM // 2
    inv = ROPE_THETA ** (-jnp.arange(0, ROPE_DIM, 2, dtype=jnp.float32) / ROPE_DIM)
    ang = pos.astype(jnp.float32)[None, :] * inv[:, None]
    cos, sin = jnp.cos(ang), jnp.sin(ang)
    n = pos.shape[0]
    assert ROPE_ONE == 3 * half
    return jnp.concatenate([cos, sin, -sin, jnp.ones((1, n), jnp.float32),
                            jnp.zeros((ROPE_ROWS - ROPE_ONE - 1, n), jnp.float32)], axis=0)


def _proj_prompt(x, w_in_b, tab, conv_w):
    n = x.shape[0]
    tm = TM_PROJ
    row = lambda w: pl.BlockSpec((tm, w), lambda i: (i, 0))
    full = lambda a: pl.BlockSpec(a.shape, lambda i: (0,) * a.ndim)
    const = lambda r, w: pl.BlockSpec((r, w), lambda i: (0, 0))
    return pl.pallas_call(
        _proj_prompt_kernel,
        grid=(n // tm,),
        in_specs=[row(D_MODEL), full(w_in_b), pl.BlockSpec((ROPE_ROWS, tm), lambda i: (0, i)), full(conv_w)],
        out_specs=[row(ATTN_WIDTH), row(4 * LANES), row(4 * LANES), row(CONV_CH),
                   const(WINDOW, KV_WIDTH), const(WINDOW, KV_WIDTH), const(SUBLANES, CONV_CH)],
        out_shape=[jax.ShapeDtypeStruct((n, ATTN_WIDTH), jnp.bfloat16),
                   jax.ShapeDtypeStruct((n, 4 * LANES), jnp.bfloat16),
                   jax.ShapeDtypeStruct((n, 4 * LANES), jnp.bfloat16),
                   jax.ShapeDtypeStruct((n, CONV_CH), jnp.bfloat16),
                   jax.ShapeDtypeStruct((WINDOW, KV_WIDTH), jnp.float32),
                   jax.ShapeDtypeStruct((WINDOW, KV_WIDTH), jnp.float32),
                   jax.ShapeDtypeStruct((SUBLANES, CONV_CH), jnp.float32)],
        scratch_shapes=[pltpu.VMEM((SUBLANES, CONV_CH), jnp.float32)],
        compiler_params=_params(("arbitrary",)),
        name="proj_prompt",
    )(x, w_in_b, tab, conv_w)


def _proj_sample(x, w_in_b, tab, conv_w, c0, c1):
    n = x.shape[0]
    full = lambda a: pl.BlockSpec(a.shape, lambda i: (0,) * a.ndim)
    out = lambda w, dt: jax.ShapeDtypeStruct((n, w), dt)
    blk = lambda w: pl.BlockSpec((n, w), lambda i: (0, 0))
    return pl.pallas_call(
        _proj_sample_kernel,
        grid=(1,),
        in_specs=[full(x), full(w_in_b), full(tab), full(conv_w), full(c0), full(c1)],
        out_specs=[blk(ATTN_WIDTH), blk(KV_WIDTH), blk(KV_WIDTH), blk(CONV_CH), blk(CONV_CH)],
        out_shape=[out(ATTN_WIDTH, jnp.float32), out(KV_WIDTH, jnp.float32), out(KV_WIDTH, jnp.float32),
                   out(CONV_CH, jnp.bfloat16), out(CONV_CH, jnp.float32)],
        compiler_params=_params(("arbitrary",)),
        name="proj_sample",
    )(x, w_in_b, tab, conv_w, c0, c1)


def _sink_softmax_pv(s, valid, sink, vx):
    s = jnp.where(valid, s, -jnp.inf)
    m = jnp.maximum(jnp.max(s, axis=1, keepdims=True), sink)
    p = jnp.exp(s - m)
    den = jnp.sum(p, axis=1, keepdims=True) + jnp.exp(sink - m)
    return _dot(_bf16(p), vx) / den


SWA_QB = 2


def _swa_prompt_kernel(sinks_ref, q_ref, kc_ref, kp_ref, vc_ref, vp_ref, o_ref):
    step = pl.program_id(0)
    kall = jnp.concatenate([kp_ref[...], kc_ref[...]], axis=0)
    vall = jnp.concatenate([vp_ref[...], vc_ref[...]], axis=0)
    i = lax.broadcasted_iota(jnp.int32, (WINDOW, 2 * WINDOW), 0)
    j = lax.broadcasted_iota(jnp.int32, (WINDOW, 2 * WINDOW), 1)
    band = (j > i) & (j <= i + WINDOW)
    for sb in range(SWA_QB):
        rows = slice(sb * WINDOW, (sb + 1) * WINDOW)
        kcat = kall[sb * WINDOW:(sb + 2) * WINDOW]
        vcat = vall[sb * WINDOW:(sb + 2) * WINDOW]
        valid = band & ((step > 0) | (j >= WINDOW)) if sb == 0 else band
        for p in range(N_HEADS // 2):
            qs = q_ref[rows, p * LANES:(p + 1) * LANES]
            acc = None
            for e in range(2):
                hd = 2 * p + e
                slab = 2 * (hd // (N_HEADS // N_KV_HEADS)) + e
                kx = kcat[:, slab * LANES:(slab + 1) * LANES]
                vx = vcat[:, slab * LANES:(slab + 1) * LANES]
                o = _sink_softmax_pv(_dot_nt(qs, kx), valid, sinks_ref[hd], vx)
                acc = o if acc is None else acc + o
            o_ref[rows, p * LANES:(p + 1) * LANES] = _bf16(acc)


def _swa_prompt(sinks, q, kx, vx):
    n = q.shape[0]
    nb = n // (SWA_QB * WINDOW)
    cur = lambda w: pl.BlockSpec((SWA_QB * WINDOW, w), lambda i: (i, 0))
    prev = lambda w: pl.BlockSpec((WINDOW, w), lambda i: (jnp.maximum(SWA_QB * i - 1, 0), 0))
    return pl.pallas_call(
        _swa_prompt_kernel,
        grid=(nb,),
        in_specs=[pl.BlockSpec(memory_space=pltpu.SMEM), cur(ATTN_WIDTH),
                  cur(4 * LANES), prev(4 * LANES), cur(4 * LANES), prev(4 * LANES)],
        out_specs=cur(ATTN_WIDTH),
        out_shape=jax.ShapeDtypeStruct((n, ATTN_WIDTH), jnp.bfloat16),
        compiler_params=_params(("arbitrary",)),
        name="swa_prompt",
    )(sinks, q, kx, kx, vx, vx)


SWA_ROWS = N_HEADS * DEC_SEQ
NEW_ROWS = 2 * SUBLANES


def _swa_sample_kernel(q_ref, sink_ref, kn_ref, vn_ref, kt_ref, vt_ref, o_ref, okt_ref, ovt_ref):
    t = lax.broadcasted_iota(jnp.int32, (SWA_ROWS, WINDOW), 0) % DEC_SEQ
    j = lax.broadcasted_iota(jnp.int32, (SWA_ROWS, WINDOW), 1)
    valid_c = j > t
    valid_n = (lax.broadcasted_iota(jnp.int32, (SWA_ROWS, NEW_ROWS), 1)
               <= lax.broadcasted_iota(jnp.int32, (SWA_ROWS, NEW_ROWS), 0) % DEC_SEQ)
    lane = lax.broadcasted_iota(jnp.int32, (KV_WIDTH, WINDOW), 1)
    sink = sink_ref[:, 0:1]
    shift = WINDOW - DEC_SEQ
    zrows = jnp.zeros((KV_WIDTH - NEW_ROWS, KV_WIDTH), jnp.float32)
    for b in range(SWA_BB):
        q = _bf16(q_ref[b])
        kt, vt = kt_ref[b], vt_ref[b]
        kn, vn = kn_ref[b], vn_ref[b]
        s_c = jnp.where(valid_c, _dot(q, _bf16(kt)), -jnp.inf)
        s_n = jnp.where(valid_n, _dot_nt(q, _bf16(kn)), -jnp.inf)
        m = jnp.maximum(jnp.maximum(jnp.max(s_c, axis=1, keepdims=True), jnp.max(s_n, axis=1, keepdims=True)), sink)
        p_c = jnp.exp(s_c - m)
        p_n = jnp.exp(s_n - m)
        den = jnp.sum(p_c, axis=1, keepdims=True) + jnp.sum(p_n, axis=1, keepdims=True) + jnp.exp(sink - m)
        o_ref[b] = (_dot_nt(_bf16(p_c), _bf16(vt)) + _dot(_bf16(p_n), _bf16(vn))) / den
        for old, new, dst in ((kt, kn, okt_ref), (vt, vn, ovt_ref)):
            new_cols = pltpu.roll(jnp.concatenate([new, zrows], axis=0).T, shift, axis=1)
            dst[b] = jnp.where(lane >= shift, new_cols, pltpu.roll(old, shift, axis=1))


def _swa_sample(sinks, q, kn, vn, cache_k, cache_v):
    nb = cache_k.shape[0]
    bb = SWA_BB
    groups = N_HEADS // N_KV_HEADS
    qh = q.reshape(nb, DEC_SEQ, N_KV_HEADS, groups, HEAD_DIM).transpose(0, 2, 3, 1, 4)
    qh = qh.reshape(nb, N_KV_HEADS, groups * DEC_SEQ, HEAD_DIM)
    zeros = jnp.zeros_like(qh[:, 0])
    qbd = jnp.concatenate([jnp.concatenate([qh[:, 0], zeros], axis=-1),
                           jnp.concatenate([zeros, qh[:, 1]], axis=-1)], axis=1)
    sink_col = jnp.broadcast_to(jnp.repeat(sinks, DEC_SEQ).reshape(SWA_ROWS, 1), (SWA_ROWS, LANES))
    pad8 = lambda a: jnp.pad(a.reshape(nb, DEC_SEQ, KV_WIDTH), ((0, 0), (0, NEW_ROWS - DEC_SEQ), (0, 0)))
    to_t = lambda c: c.transpose(0, 2, 3, 1).reshape(nb, KV_WIDTH, WINDOW)
    blk = lambda r, w: pl.BlockSpec((bb, r, w), lambda i: (i, 0, 0))
    o, okt, ovt = pl.pallas_call(
        _swa_sample_kernel,
        grid=(nb // bb,),
        in_specs=[blk(SWA_ROWS, KV_WIDTH), pl.BlockSpec((SWA_ROWS, LANES), lambda i: (0, 0)),
                  blk(NEW_ROWS, KV_WIDTH), blk(NEW_ROWS, KV_WIDTH), blk(KV_WIDTH, WINDOW), blk(KV_WIDTH, WINDOW)],
        out_specs=[blk(SWA_ROWS, KV_WIDTH), blk(KV_WIDTH, WINDOW), blk(KV_WIDTH, WINDOW)],
        out_shape=[jax.ShapeDtypeStruct((nb, SWA_ROWS, KV_WIDTH), jnp.float32),
                   jax.ShapeDtypeStruct((nb, KV_WIDTH, WINDOW), jnp.float32),
                   jax.ShapeDtypeStruct((nb, KV_WIDTH, WINDOW), jnp.float32)],
        compiler_params=_params(("arbitrary",)),
        name="swa_sample",
    )(qbd, sink_col, pad8(kn), pad8(vn), to_t(cache_k), to_t(cache_v))
    o = o.reshape(nb, N_KV_HEADS, groups, DEC_SEQ, N_KV_HEADS, HEAD_DIM)
    attn = jnp.stack([o[:, h, :, :, h, :] for h in range(N_KV_HEADS)], axis=1)
    attn = attn.transpose(0, 3, 1, 2, 4).reshape(nb * DEC_SEQ, ATTN_WIDTH)
    from_t = lambda c: c.reshape(nb, N_KV_HEADS, HEAD_DIM, WINDOW).transpose(0, 3, 1, 2)
    return attn, from_t(okt), from_t(ovt)


def _mem_kv_kernel(mem_ref, wk_ref, wv_ref, mk_ref, mv_ref, mkb_ref, mvb_ref):
    mb = _bf16(mem_ref[...])
    mk = _dot(mb, wk_ref[...])
    mv = _dot(mb, wv_ref[...])
    mk_ref[...] = mk
    mv_ref[...] = mv
    mkb_ref[...] = _bf16(mk)
    mvb_ref[...] = _bf16(mv)


def _mem_kv(mem, wk_b, wv_b):
    full = lambda a: pl.BlockSpec(a.shape, lambda i: (0,) * a.ndim)
    blk = pl.BlockSpec((N_MEM, D_MODEL), lambda i: (0, 0))
    f32 = jax.ShapeDtypeStruct((N_MEM, D_MODEL), jnp.float32)
    b16 = jax.ShapeDtypeStruct((N_MEM, D_MODEL), jnp.bfloat16)
    return pl.pallas_call(
        _mem_kv_kernel,
        grid=(1,),
        in_specs=[full(mem), full(wk_b), full(wv_b)],
        out_specs=[blk, blk, blk, blk],
        out_shape=[f32, f32, b16, b16],
        compiler_params=_params(("arbitrary",)),
        name="mem_kv",
    )(mem, wk_b, wv_b)


def _mix_ln1(attn_ref, conv_ref, x_ref, wmix_ref, g1_ref, b1_ref):
    mix = _dot(_bf16(attn_ref[...]), wmix_ref[0:ATTN_WIDTH, :]) + _dot(conv_ref[...], wmix_ref[ATTN_WIDTH:, :])
    return _layer_norm(ALPHA * x_ref[...] + mix, g1_ref[...], b1_ref[...])


def _mem_q(h1, wq_ref):
    return _bf16(_dot(_bf16(h1), wq_ref[...]) * (MEM_HEAD_DIM ** -0.5))


def _route(h2, wrhl_ref, br_ref, tri_ref, carry):
    hi = _bf16(h2)
    lo = _bf16(h2 - hi.astype(jnp.float32))
    hh = _dot(hi, wrhl_ref[...])
    logits = hh[:, 0:LANES] + hh[:, LANES:] + _dot(lo, wrhl_ref[:, 0:LANES]) + br_ref[...]
    lane_i = lax.broadcasted_iota(jnp.int32, logits.shape, 1)
    lane = lane_i.astype(jnp.float32)
    big = jnp.float32(LANES)
    is_g = lane_i < N_GROUPS
    gl = jnp.where(is_g, logits, -jnp.inf)
    gmax = jnp.max(gl, axis=1, keepdims=True)
    gidx = jnp.min(jnp.where(is_g & (logits == gmax), lane, big), axis=1, keepdims=True)
    gsum = jnp.sum(jnp.exp(gl - gmax), axis=1, keepdims=True)
    gw = 1.0 / gsum
    eid = lane_i - N_GROUPS
    assert EXPERTS_PER_GROUP == 8
    grp = lax.shift_right_arithmetic(eid, jnp.full_like(eid, 3)).astype(jnp.float32)
    in_e = (lane_i >= N_GROUPS) & (lane_i < N_GROUPS + N_EXPERTS) & (grp == gidx)
    v1 = jnp.max(jnp.where(in_e, logits, -jnp.inf), axis=1, keepdims=True)
    i1 = jnp.min(jnp.where(in_e & (logits == v1), lane, big), axis=1, keepdims=True)
    rest = in_e & (lane != i1)
    v2 = jnp.max(jnp.where(rest, logits, -jnp.inf), axis=1, keepdims=True)
    i2 = jnp.min(jnp.where(rest & (logits == v2), lane, big), axis=1, keepdims=True)
    ex = jnp.exp(v2 - v1)
    den = 1.0 + ex
    w1 = gw / den
    w2 = gw * ex / den
    zero = jnp.zeros_like(logits)
    pick1 = lane == i1
    pick2 = lane == i2
    sel = jnp.where(pick1 | pick2, 1.0, 0.0)
    before = _dot(tri_ref[...], _bf16(sel)) + carry
    rank1 = jnp.sum(jnp.where(pick1, before, zero), axis=1, keepdims=True)
    rank2 = jnp.sum(jnp.where(pick2, before, zero), axis=1, keepdims=True)
    cols = (i1 - N_GROUPS, i2 - N_GROUPS, w1, w2, rank1, rank2)
    route = zero
    for k, col in enumerate(cols):
        route = jnp.where(lane_i == k, col, route)
    return route, carry + jnp.sum(sel, axis=0, keepdims=True)


def _post_prompt_kernel(attn_ref, conv_ref, x_ref, wmix_ref, g1_ref, b1_ref, wq_ref, mk_ref, mv_ref, wo_ref,
                        g2_ref, b2_ref, wrhl_ref, br_ref, tri_ref, h2s_ref, rts_ref, cnts_ref,
                        h2t_ref, rt_ref, cnt_ref, carry_ref):
    is_tail = pl.program_id(0) == N_PROMPT // TM_POST

    @pl.when(pl.program_id(0) == 0)
    def _():
        carry_ref[...] = cnts_ref[...]

    @pl.when(is_tail)
    def _():
        h2t_ref[...] = h2s_ref[...]
        rt_ref[...] = rts_ref[...]

    @pl.when(jnp.logical_not(is_tail))
    def _():
        h1 = _mix_ln1(attn_ref, conv_ref, x_ref, wmix_ref, g1_ref, b1_ref)
        qm = _mem_q(h1, wq_ref)
        outs = []
        for h in range(MEM_HEADS):
            sl = slice(h * MEM_HEAD_DIM, (h + 1) * MEM_HEAD_DIM)
            s = _dot_nt(qm[:, sl], mk_ref[:, sl])
            m = jnp.max(s, axis=1, keepdims=True)
            p = jnp.exp(s - m)
            den = jnp.sum(p, axis=1, keepdims=True)
            outs.append(_dot(_bf16(p), mv_ref[:, sl]) / den)
        o = _bf16(jnp.concatenate(outs, axis=1))
        h2 = _layer_norm(ALPHA * h1 + _dot(o, wo_ref[...]), g2_ref[...], b2_ref[...])
        _store_token_tiles(h2t_ref, h2)
        route, carry = _route(h2, wrhl_ref, br_ref, tri_ref, carry_ref[0:1, :])
        rt_ref[...] = route
        carry_ref[...] = jnp.broadcast_to(carry, carry_ref.shape)
        cnt_ref[...] = jnp.broadcast_to(carry, cnt_ref.shape)


def _post_prompt(attn, conv, x, wmix_b, g1, b1, wq_b, mk_b, mv_b, wo_b, g2, b2, wrhl, br, tri, h2t_s, rt_s, cnt_s):
    n = x.shape[0]
    tm = TM_POST
    steps = n // tm
    row = lambda w: pl.BlockSpec((tm, w), lambda i: (jnp.minimum(i, steps - 1), 0))
    full = lambda a: pl.BlockSpec(a.shape, lambda i: (0,) * a.ndim)
    weights = (wmix_b, g1, b1, wq_b, mk_b, mv_b, wo_b, g2, b2, wrhl, br, tri, h2t_s, rt_s, cnt_s)
    n_out = n + h2t_s.shape[0] // ROW_CHUNKS
    return pl.pallas_call(
        _post_prompt_kernel,
        grid=(steps + 1,),
        in_specs=[row(ATTN_WIDTH), row(CONV_CH), row(D_MODEL)] + [full(a) for a in weights],
        out_specs=[pl.BlockSpec((tm * ROW_CHUNKS, LANES), lambda i: (i, 0)),
                   pl.BlockSpec((tm, LANES), lambda i: (i, 0)),
                   pl.BlockSpec((SUBLANES, LANES), lambda i: (0, 0))],
        out_shape=[jax.ShapeDtypeStruct((n_out * ROW_CHUNKS, LANES), jnp.float32),
                   jax.ShapeDtypeStruct((n_out, LANES), jnp.float32),
                   jax.ShapeDtypeStruct((SUBLANES, LANES), jnp.float32)],
        scratch_shapes=[pltpu.VMEM((SUBLANES, LANES), jnp.float32)],
        compiler_params=_params(("arbitrary",)),
        name="post_prompt",
    )(attn, conv, x, *weights)


def _post_a_sample_kernel(attn_ref, conv_ref, x_ref, wmix_ref, g1_ref, b1_ref, wq_ref, h1_ref, qm_ref):
    h1 = _mix_ln1(attn_ref, conv_ref, x_ref, wmix_ref, g1_ref, b1_ref)
    h1_ref[...] = h1
    qm_ref[...] = _mem_q(h1, wq_ref).astype(jnp.float32)


def _post_a_sample(attn, conv, x, wmix_b, g1, b1, wq_b):
    n = x.shape[0]
    args = (attn, conv, x, wmix_b, g1, b1, wq_b)
    full = lambda a: pl.BlockSpec(a.shape, lambda i: (0,) * a.ndim)
    blk = pl.BlockSpec((n, D_MODEL), lambda i: (0, 0))
    return pl.pallas_call(
        _post_a_sample_kernel,
        grid=(1,),
        in_specs=[full(a) for a in args],
        out_specs=[blk, blk],
        out_shape=[jax.ShapeDtypeStruct((n, D_MODEL), jnp.float32),
                   jax.ShapeDtypeStruct((n, D_MODEL), jnp.float32)],
        compiler_params=_params(("arbitrary",)),
        name="post_a_sample",
    )(*args)


MEM_ROWS = MEM_HEADS * DEC_SEQ


def _mem_attn_sample_kernel(q_ref, mk_ref, mv_ref, o_ref):
    nk = N_MEM * MEM_HEADS
    row_h = lax.broadcasted_iota(jnp.int32, (MEM_ROWS, nk), 0) // DEC_SEQ
    key_h = lax.broadcasted_iota(jnp.int32, (MEM_ROWS, nk), 1) % MEM_HEADS
    own = row_h == key_h
    for b in range(SAMPLE_BB):
        k2 = _bf16(mk_ref[b].reshape(nk, MEM_HEAD_DIM))
        v2 = _bf16(mv_ref[b].reshape(nk, MEM_HEAD_DIM))
        s = jnp.where(own, _dot_nt(_bf16(q_ref[b]), k2), -jnp.inf)
        m = jnp.max(s, axis=1, keepdims=True)
        p = jnp.exp(s - m)
        den = jnp.sum(p, axis=1, keepdims=True)
        o_ref[b] = _dot(_bf16(p), v2) / den


def _mem_attn_sample(qm, mk, mv):
    nb = mk.shape[0]
    bb = SAMPLE_BB
    q = qm.reshape(nb, DEC_SEQ, MEM_HEADS, MEM_HEAD_DIM).transpose(0, 2, 1, 3).reshape(nb, MEM_ROWS, MEM_HEAD_DIM)
    rows = pl.BlockSpec((bb, MEM_ROWS, MEM_HEAD_DIM), lambda i: (i, 0, 0))
    kv = pl.BlockSpec((bb, N_MEM, MEM_HEADS, MEM_HEAD_DIM), lambda i: (i, 0, 0, 0))
    o = pl.pallas_call(
        _mem_attn_sample_kernel,
        grid=(nb // bb,),
        in_specs=[rows, kv, kv],
        out_specs=rows,
        out_shape=jax.ShapeDtypeStruct((nb, MEM_ROWS, MEM_HEAD_DIM), jnp.float32),
        compiler_params=_params(("arbitrary",)),
        name="mem_attn_sample",
    )(q, mk, mv)
    return o.reshape(nb, MEM_HEADS, DEC_SEQ, MEM_HEAD_DIM).transpose(0, 2, 1, 3).reshape(nb * DEC_SEQ, D_MODEL)


def _post_b_sample_kernel(o_ref, h1_ref, wo_ref, g2_ref, b2_ref, wrhl_ref, br_ref, tri_ref,
                          h2t_ref, rt_ref, cnt_ref):
    h2 = _layer_norm(ALPHA * h1_ref[...] + _dot(_bf16(o_ref[...]), wo_ref[...]), g2_ref[...], b2_ref[...])
    _store_token_tiles(h2t_ref, h2)
    route, carry = _route(h2, wrhl_ref, br_ref, tri_ref, jnp.zeros((1, LANES), jnp.float32))
    rt_ref[...] = route
    cnt_ref[...] = jnp.broadcast_to(carry, cnt_ref.shape)


def _post_b_sample(o, h1, wo_b, g2, b2, wrhl, br, tri):
    n = h1.shape[0]
    args = (o, h1, wo_b, g2, b2, wrhl, br, tri)
    full = lambda a: pl.BlockSpec(a.shape, lambda i: (0,) * a.ndim)
    return pl.pallas_call(
        _post_b_sample_kernel,
        grid=(1,),
        in_specs=[full(a) for a in args],
        out_specs=[pl.BlockSpec((n * ROW_CHUNKS, LANES), lambda i: (0, 0)),
                   pl.BlockSpec((n, LANES), lambda i: (0, 0)),
                   pl.BlockSpec((SUBLANES, LANES), lambda i: (0, 0))],
        out_shape=[jax.ShapeDtypeStruct((n * ROW_CHUNKS, LANES), jnp.float32),
                   jax.ShapeDtypeStruct((n, LANES), jnp.float32),
                   jax.ShapeDtypeStruct((SUBLANES, LANES), jnp.float32)],
        compiler_params=_params(("arbitrary",)),
        name="post_b_sample",
    )(*args)


def _row_gather_copy(src_hbm, idx, dst, dst_row, sem):
    s0 = pl.multiple_of(idx * ROW_CHUNKS, ROW_CHUNKS)
    d0 = pl.multiple_of(dst_row * ROW_CHUNKS, ROW_CHUNKS)
    return pltpu.make_async_copy(src_hbm.at[pl.ds(s0, ROW_CHUNKS), :], dst.at[pl.ds(d0, ROW_CHUNKS), :], sem)


def _dispatch_kernel(pos_ref, h2t_ref, xs_hbm, sem):
    def body(r, c):
        src = h2t_ref.at[pl.ds(pl.multiple_of(r * ROW_CHUNKS, ROW_CHUNKS), ROW_CHUNKS), :]
        for k in range(2):
            d0 = pl.multiple_of(pos_ref[0, 0, k * TM_COMB + r] * ROW_CHUNKS, ROW_CHUNKS)
            pltpu.make_async_copy(src, xs_hbm.at[pl.ds(d0, ROW_CHUNKS), :], sem.at[0]).start(priority=k)
        return c
    lax.fori_loop(0, TM_COMB, body, 0, unroll=8)
    for _ in range(2):
        pltpu.make_async_copy(h2t_ref, xs_hbm.at[pl.ds(0, TM_COMB * ROW_CHUNKS), :], sem.at[0]).wait()


def _dispatch(pos3, h2t):
    nt = N_ALL // TM_COMB
    return pl.pallas_call(
        _dispatch_kernel,
        grid=(nt,),
        in_specs=[pl.BlockSpec((1, 1, 2 * TM_COMB), lambda i: (i, 0, 0), memory_space=pltpu.SMEM),
                  pl.BlockSpec((TM_COMB * ROW_CHUNKS, LANES), lambda i: (i, 0))],
        out_specs=pl.BlockSpec(memory_space=pl.ANY),
        out_shape=jax.ShapeDtypeStruct((N_ASSIGN * ROW_CHUNKS, LANES), jnp.float32),
        scratch_shapes=[pltpu.SemaphoreType.DMA((1,))],
        compiler_params=_params(("arbitrary",)),
        name="moe_dispatch",
    )(pos3, h2t)


def _moe_ffn_kernel(it_ref, ie_ref, lo_ref, hi_ref, x_ref, wg_ref, wu_ref, wd_ref, y_ref, wgb, wub, wdb, cur_e):
    i = pl.program_id(0)
    lo = lo_ref[i]
    hi = hi_ref[i]
    e = ie_ref[i]

    @pl.when(i == 0)
    def _():
        cur_e[0] = -1

    @pl.when((hi > lo) & (cur_e[0] != e))
    def _():
        wgb[...] = _bf16(wg_ref[0])
        wub[...] = _bf16(wu_ref[0])
        wdb[...] = _bf16(wd_ref[0])
        cur_e[0] = e

    def ffn():
        x = _bf16(_load_token_tiles(x_ref, 0, TM_MOE))
        hg = _dot(x, wgb[...])
        hu = _dot(x, wub[...])
        h = hg / (1.0 + jnp.exp(-hg)) * hu
        return _dot(_bf16(h), wdb[...])

    def rows_mask():
        row = lax.broadcasted_iota(jnp.int32, (TM_MOE, LANES), 0)
        return (row >= lo) & (row < hi)

    @pl.when((hi > lo) & (lo == 0))
    def _():
        y = ffn()
        mask = rows_mask()
        for c in range(ROW_CHUNKS):
            y_ref[pl.ds(c, TM_MOE, stride=ROW_CHUNKS), :] = jnp.where(mask, y[:, c * LANES:(c + 1) * LANES], 0.0)

    @pl.when((hi > lo) & (lo > 0))
    def _():
        y = ffn()
        mask = rows_mask()
        for c in range(ROW_CHUNKS):
            sl = pl.ds(c, TM_MOE, stride=ROW_CHUNKS)
            y_ref[sl, :] = jnp.where(mask, y[:, c * LANES:(c + 1) * LANES], y_ref[sl, :])


def _moe_ffn(item_tile, item_expert, item_lo, item_hi, x_sorted, w_gate, w_up, w_down):
    wspec = lambda shp: pl.BlockSpec((1,) + shp, lambda i, it, ie, lo, hi: (ie[i], 0, 0))
    tile = pl.BlockSpec((TM_MOE * ROW_CHUNKS, LANES), lambda i, it, ie, lo, hi: (it[i], 0))
    grid_spec = pltpu.PrefetchScalarGridSpec(
        num_scalar_prefetch=4,
        grid=(MOE_ITEMS,),
        in_specs=[tile, wspec((D_MODEL, EXPERT_FF)), wspec((D_MODEL, EXPERT_FF)), wspec((EXPERT_FF, D_MODEL))],
        out_specs=tile,
        scratch_shapes=[pltpu.VMEM((D_MODEL, EXPERT_FF), jnp.bfloat16),
                        pltpu.VMEM((D_MODEL, EXPERT_FF), jnp.bfloat16),
                        pltpu.VMEM((EXPERT_FF, D_MODEL), jnp.bfloat16),
                        pltpu.SMEM((1,), jnp.int32)],
    )
    return pl.pallas_call(
        _moe_ffn_kernel,
        grid_spec=grid_spec,
        out_shape=jax.ShapeDtypeStruct((N_ASSIGN * ROW_CHUNKS, LANES), jnp.float32),
        compiler_params=_params(("arbitrary",)),
        name="moe_ffn",
    )(item_tile, item_expert, item_lo, item_hi, x_sorted, w_gate, w_up, w_down)


def _combine_kernel(nt, pos_cur_ref, pos_nxt_ref, yt_hbm, h2t_ref, rt_ref, g3_ref, b3_ref, o_ref, abuf, sem):
    t = pl.program_id(0)
    slot = t % 2
    rows = 2 * TM_COMB

    def issue(pos_ref, s):
        def body(j, c):
            for k in range(2):
                r = 2 * j + k
                _row_gather_copy(yt_hbm, pos_ref[0, 0, r], abuf, s * rows + r, sem.at[s]).start(priority=k)
            return c
        lax.fori_loop(0, rows // 2, body, 0, unroll=4)

    @pl.when(t == 0)
    def _():
        issue(pos_cur_ref, 0)

    @pl.when(t + 1 < nt)
    def _():
        issue(pos_nxt_ref, 1 - slot)

    base = pl.multiple_of(slot * (rows * ROW_CHUNKS), rows * ROW_CHUNKS)
    pltpu.make_async_copy(yt_hbm.at[pl.ds(0, rows * ROW_CHUNKS), :],
                          abuf.at[pl.ds(base, rows * ROW_CHUNKS), :], sem.at[slot]).wait()
    ya = _load_token_tiles(abuf, base, TM_COMB)
    yb = _load_token_tiles(abuf, base + TM_COMB * ROW_CHUNKS, TM_COMB)
    rt = rt_ref[...]
    ff = rt[:, 2:3] * ya + rt[:, 3:4] * yb
    h2 = _load_token_tiles(h2t_ref, 0, TM_COMB)
    o_ref[...] = _layer_norm(ALPHA * h2 + ff, g3_ref[...], b3_ref[...])


def _combine(pos3, yt, h2t, rt, g3, b3, tile0, n_tiles):
    last = tile0 + n_tiles - 1
    smem_pos = lambda f: pl.BlockSpec((1, 1, 2 * TM_COMB), f, memory_space=pltpu.SMEM)
    full = lambda a: pl.BlockSpec(a.shape, lambda i: (0,) * a.ndim)
    return pl.pallas_call(
        functools.partial(_combine_kernel, n_tiles),
        grid=(n_tiles,),
        in_specs=[smem_pos(lambda i: (tile0 + i, 0, 0)),
                  smem_pos(lambda i: (jnp.minimum(tile0 + i + 1, last), 0, 0)),
                  pl.BlockSpec(memory_space=pl.ANY),
                  pl.BlockSpec((TM_COMB * ROW_CHUNKS, LANES), lambda i: (tile0 + i, 0)),
                  pl.BlockSpec((TM_COMB, LANES), lambda i: (tile0 + i, 0)),
                  full(g3), full(b3)],
        out_specs=pl.BlockSpec((TM_COMB, D_MODEL), lambda i: (i, 0)),
        out_shape=jax.ShapeDtypeStruct((n_tiles * TM_COMB, D_MODEL), jnp.float32),
        scratch_shapes=[pltpu.VMEM((2 * 2 * TM_COMB * ROW_CHUNKS, LANES), jnp.float32),
                        pltpu.SemaphoreType.DMA((2,))],
        compiler_params=_params(("arbitrary",)),
        name="moe_combine",
    )(pos3, pos3, yt, h2t, rt, g3, b3)


POS_TILES = 3


def _positions_kernel(rt_ref, starts_ref, pos_ref):
    lane = lax.broadcasted_iota(jnp.int32, (TM_COMB, LANES), 1)
    lane_f = lane.astype(jnp.float32)
    starts = starts_ref[0:1, :]
    for j in range(POS_TILES):
        rt = rt_ref[j * TM_COMB:(j + 1) * TM_COMB, :]
        cols = []
        for k in range(2):
            seg = jnp.sum(jnp.where(lane_f == rt[:, k:k + 1] + N_GROUPS, starts, 0.0), axis=1, keepdims=True)
            cols.append(seg + rt[:, 4 + k:5 + k])
        packed = jnp.where(lane == 0, cols[0], jnp.where(lane == 1, cols[1], 0.0))
        rows = packed.T
        pos_ref[j] = jnp.concatenate([rows[0:1, :], rows[1:2, :]], axis=1).astype(jnp.int32)


def _positions(rt, starts_row):
    nt = N_ALL // TM_COMB
    assert nt % POS_TILES == 0
    return pl.pallas_call(
        _positions_kernel,
        grid=(nt // POS_TILES,),
        in_specs=[pl.BlockSpec((POS_TILES * TM_COMB, LANES), lambda i: (i, 0)),
                  pl.BlockSpec((SUBLANES, LANES), lambda i: (0, 0))],
        out_specs=pl.BlockSpec((POS_TILES, 1, 2 * TM_COMB), lambda i: (i, 0, 0)),
        out_shape=jax.ShapeDtypeStruct((nt, 1, 2 * TM_COMB), jnp.int32),
        compiler_params=_params(("arbitrary",)),
        name="moe_positions",
    )(rt, starts_row)


def _routing_plan(rt, cnt):
    i32 = jnp.int32
    counts_f = cnt[0, N_GROUPS:N_GROUPS + N_EXPERTS]
    starts_f = jnp.cumsum(counts_f) - counts_f
    starts_row = jnp.broadcast_to(
        jnp.pad(starts_f, (N_GROUPS, LANES - N_GROUPS - N_EXPERTS))[None, :], (SUBLANES, LANES))
    pos3 = _positions(rt, starts_row)
    starts = starts_f.astype(i32)
    tiles = jnp.arange(MOE_TILES, dtype=i32) * TM_MOE
    rank_t = jnp.arange(MOE_TILES, dtype=i32) + jnp.sum((starts[None, :] < tiles[:, None]).astype(i32), axis=1)
    rank_s = jnp.arange(N_EXPERTS, dtype=i32) + jnp.sum((tiles[None, :] <= starts[:, None]).astype(i32), axis=1)
    vals = jnp.concatenate([tiles, starts])
    ranks = jnp.concatenate([rank_t, rank_s])
    slot = jnp.arange(MOE_ITEMS, dtype=i32)
    lo = jnp.sum(jnp.where(ranks[None, :] == slot[:, None], vals[None, :], 0), axis=1)
    hi = jnp.concatenate([lo[1:], jnp.full((1,), N_ASSIGN, i32)])
    item_tile = jnp.minimum(lo // TM_MOE, MOE_TILES - 1)
    item_expert = jnp.clip(jnp.sum((starts[None, :] <= lo[:, None]).astype(i32), axis=1) - 1, 0, N_EXPERTS - 1)
    base = item_tile * TM_MOE
    return item_tile, item_expert, lo - base, hi - base, pos3


def kernel(x_prompt, x_sample, mem_prompt, cache_swa_k, cache_swa_v, cache_conv, cache_mem_k, cache_mem_v,
           w_in, sinks, conv_w, w_mix_out, ln1_g, ln1_b, w_q_mem, w_k_mem, w_v_mem, w_o_mem, ln2_g, ln2_b,
           w_router_group, b_router_group, w_router_expert, b_router_expert, w_gate, w_up, w_down,
           ln3_g, ln3_b):
    f32 = jnp.float32
    row = lambda a: a.reshape(1, -1).astype(f32)
    w_in_b, wmix_b, wq_b, wk_b, wv_b, wo_b = (_bf16(w) for w in (w_in, w_mix_out, w_q_mem, w_k_mem, w_v_mem, w_o_mem))
    g1, b1, g2, b2, g3, b3 = (row(a) for a in (ln1_g, ln1_b, ln2_g, ln2_b, ln3_g, ln3_b))
    pad = LANES - N_GROUPS - N_EXPERTS
    wr = jnp.concatenate([w_router_group, w_router_expert, jnp.zeros((D_MODEL, pad), f32)], axis=1)
    wrh = _bf16(wr)
    wrhl = jnp.concatenate([wrh, _bf16(wr - wrh.astype(f32))], axis=1)
    br = jnp.concatenate([b_router_group, b_router_expert, jnp.zeros((pad,), f32)]).reshape(1, LANES)

    xs = x_sample.reshape(N_SAMPLE, D_MODEL)
    tab_s = jnp.tile(_rope_table(PAST_LEN + jnp.arange(DEC_SEQ)), (1, DEC_BATCH))
    c0 = jnp.repeat(cache_conv[:, 0], DEC_SEQ, axis=0)
    c1 = jnp.repeat(cache_conv[:, 1], DEC_SEQ, axis=0)
    q_s, k_s, v_s, conv_s, u_s = _proj_sample(xs, w_in_b, tab_s, conv_w, c0, c1)
    attn_s, swa_k_s, swa_v_s = _swa_sample(sinks, q_s, k_s, v_s, cache_swa_k, cache_swa_v)
    h1_s, qm_s = _post_a_sample(attn_s, conv_s, xs, wmix_b, g1, b1, wq_b)
    o_s = _mem_attn_sample(qm_s, cache_mem_k, cache_mem_v)
    tri = _bf16(jnp.tril(jnp.ones((TM_POST, TM_POST), f32), -1))
    h2t_s, rt_s, cnt_s = _post_b_sample(o_s, h1_s, wo_b, g2, b2, wrhl, br, tri)

    xp = x_prompt.reshape(N_PROMPT, D_MODEL)
    tab_p = _rope_table(jnp.arange(N_PROMPT))
    q_p, kx_p, vx_p, conv_p, k_tail, v_tail, u_tail = _proj_prompt(xp, w_in_b, tab_p, conv_w)
    attn_p = _swa_prompt(sinks, q_p, kx_p, vx_p)
    mk, mv, mk_b, mv_b = _mem_kv(mem_prompt.reshape(N_MEM, D_MODEL), wk_b, wv_b)
    h2t, rt, cnt = _post_prompt(attn_p, conv_p, xp, wmix_b, g1, b1, wq_b, mk_b, mv_b, wo_b, g2, b2,
                                wrhl, br, tri, h2t_s, rt_s, cnt_s)

    item_tile, item_expert, item_lo, item_hi, pos3 = _routing_plan(rt, cnt)
    x_sorted = _dispatch(pos3, h2t)
    yt = _moe_ffn(item_tile, item_expert, item_lo, item_hi, x_sorted, w_gate, w_up, w_down)
    y_p = _combine(pos3, yt, h2t, rt, g3, b3, 0, N_PROMPT // TM_COMB)
    y_s = _combine(pos3, yt, h2t, rt, g3, b3, N_PROMPT // TM_COMB, N_SAMPLE // TM_COMB)

    return (y_p.reshape(1, SEQ, D_MODEL),
            y_s.reshape(DEC_BATCH, DEC_SEQ, D_MODEL),
            k_tail.reshape(1, WINDOW, N_KV_HEADS, HEAD_DIM),
            v_tail.reshape(1, WINDOW, N_KV_HEADS, HEAD_DIM),
            u_tail[SUBLANES - (CONV_K - 1):].reshape(1, CONV_K - 1, CONV_CH),
            mk.reshape(1, N_MEM, MEM_HEADS, MEM_HEAD_DIM),
            mv.reshape(1, N_MEM, MEM_HEADS, MEM_HEAD_DIM),
            swa_k_s.reshape(DEC_BATCH, WINDOW, N_KV_HEADS, HEAD_DIM),
            swa_v_s.reshape(DEC_BATCH, WINDOW, N_KV_HEADS, HEAD_DIM),
            u_s.reshape(DEC_BATCH, DEC_SEQ, CONV_CH)[:, DEC_SEQ - (CONV_K - 1):])
```

```python
import functools

import jax
import jax.numpy as jnp
from jax import lax
from jax.experimental import pallas as pl
from jax.experimental.pallas import tpu as pltpu

D_MODEL = 1024
SEQ = 16384
DEC_BATCH = 128
DEC_SEQ = 4
PAST_LEN = 16384
ATTN_WIDTH = 512
CONV_CH = 512
HEAD_DIM = 64
N_HEADS = 8
N_KV_HEADS = 2
KV_WIDTH = 128
WINDOW = 128
ROPE_THETA = 500000.0
ROPE_DIM = 16
CONV_K = 3
Q_END = ATTN_WIDTH
K_END = Q_END + KV_WIDTH
V_END = K_END + KV_WIDTH
B_END = V_END + CONV_CH
C_END = B_END + CONV_CH
IN_WIDTH = C_END + CONV_CH
N_MEM = 256
MEM_HEADS = 4
MEM_HEAD_DIM = 256
N_GROUPS = 4
EXPERTS_PER_GROUP = 8
N_EXPERTS = 32
EXPERT_FF = 256
ALPHA = 2.0 ** 0.25
LN_EPS = 1e-5

LANES = 128
SUBLANES = 8
ROW_CHUNKS = D_MODEL // LANES
VMEM_LIMIT = 56 * 1024 * 1024

N_PROMPT = SEQ
N_SAMPLE = DEC_BATCH * DEC_SEQ
N_ALL = N_PROMPT + N_SAMPLE
TM_PROJ = 1024
TM_POST = 512
TM_MOE = 512
TM_COMB = 512
N_ASSIGN = 2 * N_ALL
MOE_TILES = N_ASSIGN // TM_MOE
MOE_ITEMS = MOE_TILES + N_EXPERTS
SAMPLE_BB = 4
SWA_BB = 8

assert ROW_CHUNKS == SUBLANES
assert N_SAMPLE == TM_POST
assert N_ASSIGN % TM_MOE == 0 and N_ALL % TM_COMB == 0


def _params(sem, vmem=VMEM_LIMIT):
    return pltpu.CompilerParams(dimension_semantics=sem, vmem_limit_bytes=vmem)


def _bf16(x):
    return x.astype(jnp.bfloat16)


def _dot(a, b):
    return jnp.dot(a, b, preferred_element_type=jnp.float32)


def _dot_nt(a, b):
    return lax.dot_general(a, b, (((1,), (1,)), ((), ())), preferred_element_type=jnp.float32)


def _layer_norm(x, g, b):
    mu = jnp.mean(x, axis=-1, keepdims=True)
    xc = x - mu
    var = jnp.mean(xc * xc, axis=-1, keepdims=True)
    return xc * lax.rsqrt(var + LN_EPS) * g + b


def _rope(x, cos_t, sin_t):
    lane = lax.broadcasted_iota(jnp.int32, x.shape, 1) % HEAD_DIM
    half = ROPE_DIM // 2
    partner = jnp.where(lane < half, pltpu.roll(x, LANES - half, axis=1), pltpu.roll(x, half, axis=1))
    return x * cos_t + partner * sin_t


def _head_slabs(x):
    lane = lax.broadcasted_iota(jnp.int32, x.shape, 1)
    lo = lane < HEAD_DIM
    sw = pltpu.roll(x, HEAD_DIM, axis=1)
    zero = jnp.zeros_like(x)
    slabs = [jnp.where(lo, x, zero), jnp.where(lo, zero, sw), jnp.where(lo, sw, zero), jnp.where(lo, zero, x)]
    return _bf16(jnp.concatenate(slabs, axis=1))


def _store_token_tiles(ref, val):
    rows = val.shape[0]
    for c in range(ROW_CHUNKS):
        ref[pl.ds(c, rows, stride=ROW_CHUNKS), :] = val[:, c * LANES:(c + 1) * LANES]


def _load_token_tiles(ref, base, rows):
    return jnp.concatenate(
        [ref[pl.ds(base + c, rows, stride=ROW_CHUNKS), :] for c in range(ROW_CHUNKS)], axis=1)


ROPE_ONE = 3 * (ROPE_DIM // 2)
ROPE_ROWS = 32


def _rope_patterns(tab):
    half = ROPE_DIM // 2
    m = lax.broadcasted_iota(jnp.int32, tab.shape, 1) % HEAD_DIM
    idx_c = jnp.where(m < ROPE_DIM, m % half, ROPE_ONE)
    idx_s = jnp.where(m < half, 2 * half + m, jnp.where(m < ROPE_DIM, m, ROPE_ONE + 1))
    return jnp.take_along_axis(tab, idx_c, axis=1), jnp.take_along_axis(tab, idx_s, axis=1)


def _proj_common(x_ref, w_ref, tab_ref):
    xb = _bf16(x_ref[...])
    tab = tab_ref[...]
    pad = jnp.zeros((LANES - tab.shape[0], tab.shape[1]), jnp.float32)
    cos_t, sin_t = _rope_patterns(jnp.concatenate([tab, pad], axis=0).T)
    q = _dot(xb, w_ref[:, 0:Q_END])
    q_rot = jnp.concatenate(
        [_rope(q[:, p * LANES:(p + 1) * LANES], cos_t, sin_t) for p in range(ATTN_WIDTH // LANES)], axis=1)
    q_out = _bf16(q_rot * (HEAD_DIM ** -0.5))
    kv = _dot(xb, w_ref[:, Q_END:V_END])
    k = _rope(kv[:, 0:KV_WIDTH], cos_t, sin_t)
    v = kv[:, KV_WIDTH:]
    bg = _dot(xb, w_ref[:, V_END:B_END])
    u = _dot(xb, w_ref[:, B_END:C_END]) * _dot(xb, w_ref[:, C_END:IN_WIDTH])
    return q_out, k, v, bg, u


def _conv3(bg, u, u1, u2, cw_ref):
    cw = cw_ref[...]
    return bg * (cw[0:1, :] * u2 + cw[1:2, :] * u1 + cw[2:3, :] * u)


def _proj_prompt_kernel(x_ref, w_ref, tab_ref, cw_ref,
                        q_ref, kx_ref, vx_ref, conv_ref, ktail_ref, vtail_ref, utail_ref, carry_ref):
    @pl.when(pl.program_id(0) == 0)
    def _():
        carry_ref[...] = jnp.zeros_like(carry_ref)

    q_out, k, v, bg, u = _proj_common(x_ref, w_ref, tab_ref)
    tm = u.shape[0]
    ext = jnp.concatenate([carry_ref[...], u], axis=0)
    u1 = pltpu.roll(ext, 1, axis=0)[SUBLANES:SUBLANES + tm]
    u2 = pltpu.roll(ext, 2, axis=0)[SUBLANES:SUBLANES + tm]
    q_ref[...] = q_out
    kx_ref[...] = _head_slabs(k)
    vx_ref[...] = _head_slabs(v)
    conv_ref[...] = _bf16(_conv3(bg, u, u1, u2, cw_ref))
    ktail_ref[...] = k[tm - WINDOW:tm]
    vtail_ref[...] = v[tm - WINDOW:tm]
    utail_ref[...] = u[tm - SUBLANES:tm]
    carry_ref[...] = u[tm - SUBLANES:tm]


def _proj_sample_kernel(x_ref, w_ref, tab_ref, cw_ref, c0_ref, c1_ref,
                        q_ref, k_ref, v_ref, conv_ref, u_ref):
    q_out, k, v, bg, u = _proj_common(x_ref, w_ref, tab_ref)
    t = lax.broadcasted_iota(jnp.int32, u.shape, 0) % DEC_SEQ
    c0 = c0_ref[...]
    c1 = c1_ref[...]
    u1 = jnp.where(t >= 1, pltpu.roll(u, 1, axis=0), c1)
    u2 = jnp.where(t >= 2, pltpu.roll(u, 2, axis=0), jnp.where(t == 1, c1, c0))
    q_ref[...] = q_out.astype(jnp.float32)
    k_ref[...] = k
    v_ref[...] = v
    conv_ref[...] = _bf16(_conv3(bg, u, u1, u2, cw_ref))
    u_ref[...] = u


def _rope_table(pos):
    half = ROPE_DIM // 2
    inv = ROPE_THETA ** (-jnp.arange(0, ROPE_DIM, 2, dtype=jnp.float32) / ROPE_DIM)
    ang = pos.astype(jnp.float32)[None, :] * inv[:, None]
    cos, sin = jnp.cos(ang), jnp.sin(ang)
    n = pos.shape[0]
    assert ROPE_ONE == 3 * half
    return jnp.concatenate([cos, sin, -sin, jnp.ones((1, n), jnp.float32),
                            jnp.zeros((ROPE_ROWS - ROPE_ONE - 1, n), jnp.float32)], axis=0)


def _proj_prompt(x, w_in_b, tab, conv_w):
    n = x.shape[0]
    tm = TM_PROJ
    row = lambda w: pl.BlockSpec((tm, w), lambda i: (i, 0))
    full = lambda a: pl.BlockSpec(a.shape, lambda i: (0,) * a.ndim)
    const = lambda r, w: pl.BlockSpec((r, w), lambda i: (0, 0))
    return pl.pallas_call(
        _proj_prompt_kernel,
        grid=(n // tm,),
        in_specs=[row(D_MODEL), full(w_in_b), pl.BlockSpec((ROPE_ROWS, tm), lambda i: (0, i)), full(conv_w)],
        out_specs=[row(ATTN_WIDTH), row(4 * LANES), row(4 * LANES), row(CONV_CH),
                   const(WINDOW, KV_WIDTH), const(WINDOW, KV_WIDTH), const(SUBLANES, CONV_CH)],
        out_shape=[jax.ShapeDtypeStruct((n, ATTN_WIDTH), jnp.bfloat16),
                   jax.ShapeDtypeStruct((n, 4 * LANES), jnp.bfloat16),
                   jax.ShapeDtypeStruct((n, 4 * LANES), jnp.bfloat16),
                   jax.ShapeDtypeStruct((n, CONV_CH), jnp.bfloat16),
                   jax.ShapeDtypeStruct((WINDOW, KV_WIDTH), jnp.float32),
                   jax.ShapeDtypeStruct((WINDOW, KV_WIDTH), jnp.float32),
                   jax.ShapeDtypeStruct((SUBLANES, CONV_CH), jnp.float32)],
        scratch_shapes=[pltpu.VMEM((SUBLANES, CONV_CH), jnp.float32)],
        compiler_params=_params(("arbitrary",)),
        name="proj_prompt",
    )(x, w_in_b, tab, conv_w)


def _proj_sample(x, w_in_b, tab, conv_w, c0, c1):
    n = x.shape[0]
    full = lambda a: pl.BlockSpec(a.shape, lambda i: (0,) * a.ndim)
    out = lambda w, dt: jax.ShapeDtypeStruct((n, w), dt)
    blk = lambda w: pl.BlockSpec((n, w), lambda i: (0, 0))
    return pl.pallas_call(
        _proj_sample_kernel,
        grid=(1,),
        in_specs=[full(x), full(w_in_b), full(tab), full(conv_w), full(c0), full(c1)],
        out_specs=[blk(ATTN_WIDTH), blk(KV_WIDTH), blk(KV_WIDTH), blk(CONV_CH), blk(CONV_CH)],
        out_shape=[out(ATTN_WIDTH, jnp.float32), out(KV_WIDTH, jnp.float32), out(KV_WIDTH, jnp.float32),
                   out(CONV_CH, jnp.bfloat16), out(CONV_CH, jnp.float32)],
        compiler_params=_params(("arbitrary",)),
        name="proj_sample",
    )(x, w_in_b, tab, conv_w, c0, c1)


def _sink_softmax_pv(s, valid, sink, vx):
    s = jnp.where(valid, s, -jnp.inf)
    m = jnp.maximum(jnp.max(s, axis=1, keepdims=True), sink)
    p = jnp.exp(s - m)
    den = jnp.sum(p, axis=1, keepdims=True) + jnp.exp(sink - m)
    return _dot(_bf16(p), vx) / den


SWA_QB = 4


def _swa_prompt_kernel(sinks_ref, q_ref, kc_ref, kp_ref, vc_ref, vp_ref, o_ref):
    step = pl.program_id(0)
    kall = jnp.concatenate([kp_ref[...], kc_ref[...]], axis=0)
    vall = jnp.concatenate([vp_ref[...], vc_ref[...]], axis=0)
    i = lax.broadcasted_iota(jnp.int32, (WINDOW, 2 * WINDOW), 0)
    j = lax.broadcasted_iota(jnp.int32, (WINDOW, 2 * WINDOW), 1)
    band = (j > i) & (j <= i + WINDOW)
    for sb in range(SWA_QB):
        rows = slice(sb * WINDOW, (sb + 1) * WINDOW)
        kcat = kall[sb * WINDOW:(sb + 2) * WINDOW]
        vcat = vall[sb * WINDOW:(sb + 2) * WINDOW]
        valid = band & ((step > 0) | (j >= WINDOW)) if sb == 0 else band
        for p in range(N_HEADS // 2):
            qs = q_ref[rows, p * LANES:(p + 1) * LANES]
            acc = None
            for e in range(2):
                hd = 2 * p + e
                slab = 2 * (hd // (N_HEADS // N_KV_HEADS)) + e
                kx = kcat[:, slab * LANES:(slab + 1) * LANES]
                vx = vcat[:, slab * LANES:(slab + 1) * LANES]
                o = _sink_softmax_pv(_dot_nt(qs, kx), valid, sinks_ref[hd], vx)
                acc = o if acc is None else acc + o
            o_ref[rows, p * LANES:(p + 1) * LANES] = _bf16(acc)


def _swa_prompt(sinks, q, kx, vx):
    n = q.shape[0]
    nb = n // (SWA_QB * WINDOW)
    cur = lambda w: pl.BlockSpec((SWA_QB * WINDOW, w), lambda i: (i, 0))
    prev = lambda w: pl.BlockSpec((WINDOW, w), lambda i: (jnp.maximum(SWA_QB * i - 1, 0), 0))
    return pl.pallas_call(
        _swa_prompt_kernel,
        grid=(nb,),
        in_specs=[pl.BlockSpec(memory_space=pltpu.SMEM), cur(ATTN_WIDTH),
                  cur(4 * LANES), prev(4 * LANES), cur(4 * LANES), prev(4 * LANES)],
        out_specs=cur(ATTN_WIDTH),
        out_shape=jax.ShapeDtypeStruct((n, ATTN_WIDTH), jnp.bfloat16),
        compiler_params=_params(("arbitrary",)),
        name="swa_prompt",
    )(sinks, q, kx, kx, vx, vx)


SWA_ROWS = N_HEADS * DEC_SEQ
NEW_ROWS = 2 * SUBLANES


def _swa_sample_kernel(q_ref, sink_ref, kn_ref, vn_ref, kt_ref, vt_ref, o_ref, okt_ref, ovt_ref):
    t = lax.broadcasted_iota(jnp.int32, (SWA_ROWS, WINDOW), 0) % DEC_SEQ
    j = lax.broadcasted_iota(jnp.int32, (SWA_ROWS, WINDOW), 1)
    valid_c = j > t
    valid_n = (lax.broadcasted_iota(jnp.int32, (SWA_ROWS, NEW_ROWS), 1)
               <= lax.broadcasted_iota(jnp.int32, (SWA_ROWS, NEW_ROWS), 0) % DEC_SEQ)
    lane = lax.broadcasted_iota(jnp.int32, (KV_WIDTH, WINDOW), 1)
    sink = sink_ref[:, 0:1]
    shift = WINDOW - DEC_SEQ
    zrows = jnp.zeros((KV_WIDTH - NEW_ROWS, KV_WIDTH), jnp.float32)
    for b in range(SWA_BB):
        q = _bf16(q_ref[b])
        kt, vt = kt_ref[b], vt_ref[b]
        kn, vn = kn_ref[b], vn_ref[b]
        s_c = jnp.where(valid_c, _dot(q, _bf16(kt)), -jnp.inf)
        s_n = jnp.where(valid_n, _dot_nt(q, _bf16(kn)), -jnp.inf)
        m = jnp.maximum(jnp.maximum(jnp.max(s_c, axis=1, keepdims=True), jnp.max(s_n, axis=1, keepdims=True)), sink)
        p_c = jnp.exp(s_c - m)
        p_n = jnp.exp(s_n - m)
        den = jnp.sum(p_c, axis=1, keepdims=True) + jnp.sum(p_n, axis=1, keepdims=True) + jnp.exp(sink - m)
        o_ref[b] = (_dot_nt(_bf16(p_c), _bf16(vt)) + _dot(_bf16(p_n), _bf16(vn))) / den
        for old, new, dst in ((kt, kn, okt_ref), (vt, vn, ovt_ref)):
            new_cols = pltpu.roll(jnp.concatenate([new, zrows], axis=0).T, shift, axis=1)
            dst[b] = jnp.where(lane >= shift, new_cols, pltpu.roll(old, shift, axis=1))


def _swa_sample(sinks, q, kn, vn, cache_k, cache_v):
    nb = cache_k.shape[0]
    bb = SWA_BB
    groups = N_HEADS // N_KV_HEADS
    qh = q.reshape(nb, DEC_SEQ, N_KV_HEADS, groups, HEAD_DIM).transpose(0, 2, 3, 1, 4)
    qh = qh.reshape(nb, N_KV_HEADS, groups * DEC_SEQ, HEAD_DIM)
    zeros = jnp.zeros_like(qh[:, 0])
    qbd = jnp.concatenate([jnp.concatenate([qh[:, 0], zeros], axis=-1),
                           jnp.concatenate([zeros, qh[:, 1]], axis=-1)], axis=1)
    sink_col = jnp.broadcast_to(jnp.repeat(sinks, DEC_SEQ).reshape(SWA_ROWS, 1), (SWA_ROWS, LANES))
    pad8 = lambda a: jnp.pad(a.reshape(nb, DEC_SEQ, KV_WIDTH), ((0, 0), (0, NEW_ROWS - DEC_SEQ), (0, 0)))
    to_t = lambda c: c.transpose(0, 2, 3, 1).reshape(nb, KV_WIDTH, WINDOW)
    blk = lambda r, w: pl.BlockSpec((bb, r, w), lambda i: (i, 0, 0))
    o, okt, ovt = pl.pallas_call(
        _swa_sample_kernel,
        grid=(nb // bb,),
        in_specs=[blk(SWA_ROWS, KV_WIDTH), pl.BlockSpec((SWA_ROWS, LANES), lambda i: (0, 0)),
                  blk(NEW_ROWS, KV_WIDTH), blk(NEW_ROWS, KV_WIDTH), blk(KV_WIDTH, WINDOW), blk(KV_WIDTH, WINDOW)],
        out_specs=[blk(SWA_ROWS, KV_WIDTH), blk(KV_WIDTH, WINDOW), blk(KV_WIDTH, WINDOW)],
        out_shape=[jax.ShapeDtypeStruct((nb, SWA_ROWS, KV_WIDTH), jnp.float32),
                   jax.ShapeDtypeStruct((nb, KV_WIDTH, WINDOW), jnp.float32),
                   jax.ShapeDtypeStruct((nb, KV_WIDTH, WINDOW), jnp.float32)],
        compiler_params=_params(("arbitrary",)),
        name="swa_sample",
    )(qbd, sink_col, pad8(kn), pad8(vn), to_t(cache_k), to_t(cache_v))
    o = o.reshape(nb, N_KV_HEADS, groups, DEC_SEQ, N_KV_HEADS, HEAD_DIM)
    attn = jnp.stack([o[:, h, :, :, h, :] for h in range(N_KV_HEADS)], axis=1)
    attn = attn.transpose(0, 3, 1, 2, 4).reshape(nb * DEC_SEQ, ATTN_WIDTH)
    from_t = lambda c: c.reshape(nb, N_KV_HEADS, HEAD_DIM, WINDOW).transpose(0, 3, 1, 2)
    return attn, from_t(okt), from_t(ovt)


def _mem_kv_kernel(mem_ref, wk_ref, wv_ref, mk_ref, mv_ref, mkb_ref, mvb_ref):
    mb = _bf16(mem_ref[...])
    mk = _dot(mb, wk_ref[...])
    mv = _dot(mb, wv_ref[...])
    mk_ref[...] = mk
    mv_ref[...] = mv
    mkb_ref[...] = _bf16(mk)
    mvb_ref[...] = _bf16(mv)


def _mem_kv(mem, wk_b, wv_b):
    full = lambda a: pl.BlockSpec(a.shape, lambda i: (0,) * a.ndim)
    blk = pl.BlockSpec((N_MEM, D_MODEL), lambda i: (0, 0))
    f32 = jax.ShapeDtypeStruct((N_MEM, D_MODEL), jnp.float32)
    b16 = jax.ShapeDtypeStruct((N_MEM, D_MODEL), jnp.bfloat16)
    return pl.pallas_call(
        _mem_kv_kernel,
        grid=(1,),
        in_specs=[full(mem), full(wk_b), full(wv_b)],
        out_specs=[blk, blk, blk, blk],
        out_shape=[f32, f32, b16, b16],
        compiler_params=_params(("arbitrary",)),
        name="mem_kv",
    )(mem, wk_b, wv_b)


def _mix_ln1(attn_ref, conv_ref, x_ref, wmix_ref, g1_ref, b1_ref):
    mix = _dot(_bf16(attn_ref[...]), wmix_ref[0:ATTN_WIDTH, :]) + _dot(conv_ref[...], wmix_ref[ATTN_WIDTH:, :])
    return _layer_norm(ALPHA * x_ref[...] + mix, g1_ref[...], b1_ref[...])


def _mem_q(h1, wq_ref):
    return _bf16(_dot(_bf16(h1), wq_ref[...]) * (MEM_HEAD_DIM ** -0.5))


def _route(h2, wrhl_ref, br_ref, tri_ref, carry):
    hi = _bf16(h2)
    lo = _bf16(h2 - hi.astype(jnp.float32))
    hh = _dot(hi, wrhl_ref[...])
    logits = hh[:, 0:LANES] + hh[:, LANES:] + _dot(lo, wrhl_ref[:, 0:LANES]) + br_ref[...]
    lane_i = lax.broadcasted_iota(jnp.int32, logits.shape, 1)
    lane = lane_i.astype(jnp.float32)
    big = jnp.float32(LANES)
    is_g = lane_i < N_GROUPS
    gl = jnp.where(is_g, logits, -jnp.inf)
    gmax = jnp.max(gl, axis=1, keepdims=True)
    gidx = jnp.min(jnp.where(is_g & (logits == gmax), lane, big), axis=1, keepdims=True)
    gsum = jnp.sum(jnp.exp(gl - gmax), axis=1, keepdims=True)
    gw = 1.0 / gsum
    eid = lane_i - N_GROUPS
    assert EXPERTS_PER_GROUP == 8
    grp = lax.shift_right_arithmetic(eid, jnp.full_like(eid, 3)).astype(jnp.float32)
    in_e = (lane_i >= N_GROUPS) & (lane_i < N_GROUPS + N_EXPERTS) & (grp == gidx)
    v1 = jnp.max(jnp.where(in_e, logits, -jnp.inf), axis=1, keepdims=True)
    i1 = jnp.min(jnp.where(in_e & (logits == v1), lane, big), axis=1, keepdims=True)
    rest = in_e & (lane != i1)
    v2 = jnp.max(jnp.where(rest, logits, -jnp.inf), axis=1, keepdims=True)
    i2 = jnp.min(jnp.where(rest & (logits == v2), lane, big), axis=1, keepdims=True)
    ex = jnp.exp(v2 - v1)
    den = 1.0 + ex
    w1 = gw / den
    w2 = gw * ex / den
    zero = jnp.zeros_like(logits)
    pick1 = lane == i1
    pick2 = lane == i2
    sel = jnp.where(pick1 | pick2, 1.0, 0.0)
    before = _dot(tri_ref[...], _bf16(sel)) + carry
    rank1 = jnp.sum(jnp.where(pick1, before, zero), axis=1, keepdims=True)
    rank2 = jnp.sum(jnp.where(pick2, before, zero), axis=1, keepdims=True)
    cols = (i1 - N_GROUPS, i2 - N_GROUPS, w1, w2, rank1, rank2)
    route = zero
    for k, col in enumerate(cols):
        route = jnp.where(lane_i == k, col, route)
    return route, carry + jnp.sum(sel, axis=0, keepdims=True)


def _post_prompt_kernel(attn_ref, conv_ref, x_ref, wmix_ref, g1_ref, b1_ref, wq_ref, mk_ref, mv_ref, wo_ref,
                        g2_ref, b2_ref, wrhl_ref, br_ref, tri_ref, h2s_ref, rts_ref, cnts_ref,
                        h2t_ref, rt_ref, cnt_ref, carry_ref):
    is_tail = pl.program_id(0) == N_PROMPT // TM_POST

    @pl.when(pl.program_id(0) == 0)
    def _():
        carry_ref[...] = cnts_ref[...]

    @pl.when(is_tail)
    def _():
        h2t_ref[...] = h2s_ref[...]
        rt_ref[...] = rts_ref[...]

    @pl.when(jnp.logical_not(is_tail))
    def _():
        h1 = _mix_ln1(attn_ref, conv_ref, x_ref, wmix_ref, g1_ref, b1_ref)
        qm = _mem_q(h1, wq_ref)
        outs = []
        for h in range(MEM_HEADS):
            sl = slice(h * MEM_HEAD_DIM, (h + 1) * MEM_HEAD_DIM)
            s = _dot_nt(qm[:, sl], mk_ref[:, sl])
            m = jnp.max(s, axis=1, keepdims=True)
            p = jnp.exp(s - m)
            den = jnp.sum(p, axis=1, keepdims=True)
            outs.append(_dot(_bf16(p), mv_ref[:, sl]) / den)
        o = _bf16(jnp.concatenate(outs, axis=1))
        h2 = _layer_norm(ALPHA * h1 + _dot(o, wo_ref[...]), g2_ref[...], b2_ref[...])
        _store_token_tiles(h2t_ref, h2)
        route, carry = _route(h2, wrhl_ref, br_ref, tri_ref, carry_ref[0:1, :])
        rt_ref[...] = route
        carry_ref[...] = jnp.broadcast_to(carry, carry_ref.shape)
        cnt_ref[...] = jnp.broadcast_to(carry, cnt_ref.shape)


def _post_prompt(attn, conv, x, wmix_b, g1, b1, wq_b, mk_b, mv_b, wo_b, g2, b2, wrhl, br, tri, h2t_s, rt_s, cnt_s):
    n = x.shape[0]
    tm = TM_POST
    steps = n // tm
    row = lambda w: pl.BlockSpec((tm, w), lambda i: (jnp.minimum(i, steps - 1), 0))
    full = lambda a: pl.BlockSpec(a.shape, lambda i: (0,) * a.ndim)
    weights = (wmix_b, g1, b1, wq_b, mk_b, mv_b, wo_b, g2, b2, wrhl, br, tri, h2t_s, rt_s, cnt_s)
    n_out = n + h2t_s.shape[0] // ROW_CHUNKS
    return pl.pallas_call(
        _post_prompt_kernel,
        grid=(steps + 1,),
        in_specs=[row(ATTN_WIDTH), row(CONV_CH), row(D_MODEL)] + [full(a) for a in weights],
        out_specs=[pl.BlockSpec((tm * ROW_CHUNKS, LANES), lambda i: (i, 0)),
                   pl.BlockSpec((tm, LANES), lambda i: (i, 0)),
                   pl.BlockSpec((SUBLANES, LANES), lambda i: (0, 0))],
        out_shape=[jax.ShapeDtypeStruct((n_out * ROW_CHUNKS, LANES), jnp.float32),
                   jax.ShapeDtypeStruct((n_out, LANES), jnp.float32),
                   jax.ShapeDtypeStruct((SUBLANES, LANES), jnp.float32)],
        scratch_shapes=[pltpu.VMEM((SUBLANES, LANES), jnp.float32)],
        compiler_params=_params(("arbitrary",)),
        name="post_prompt",
    )(attn, conv, x, *weights)


def _post_a_sample_kernel(attn_ref, conv_ref, x_ref, wmix_ref, g1_ref, b1_ref, wq_ref, h1_ref, qm_ref):
    h1 = _mix_ln1(attn_ref, conv_ref, x_ref, wmix_ref, g1_ref, b1_ref)
    h1_ref[...] = h1
    qm_ref[...] = _mem_q(h1, wq_ref).astype(jnp.float32)


def _post_a_sample(attn, conv, x, wmix_b, g1, b1, wq_b):
    n = x.shape[0]
    args = (attn, conv, x, wmix_b, g1, b1, wq_b)
    full = lambda a: pl.BlockSpec(a.shape, lambda i: (0,) * a.ndim)
    blk = pl.BlockSpec((n, D_MODEL), lambda i: (0, 0))
    return pl.pallas_call(
        _post_a_sample_kernel,
        grid=(1,),
        in_specs=[full(a) for a in args],
        out_specs=[blk, blk],
        out_shape=[jax.ShapeDtypeStruct((n, D_MODEL), jnp.float32),
                   jax.ShapeDtypeStruct((n, D_MODEL), jnp.float32)],
        compiler_params=_params(("arbitrary",)),
        name="post_a_sample",
    )(*args)


MEM_ROWS = MEM_HEADS * DEC_SEQ


def _mem_attn_sample_kernel(q_ref, mk_ref, mv_ref, o_ref):
    nk = N_MEM * MEM_HEADS
    row_h = lax.broadcasted_iota(jnp.int32, (MEM_ROWS, nk), 0) // DEC_SEQ
    key_h = lax.broadcasted_iota(jnp.int32, (MEM_ROWS, nk), 1) % MEM_HEADS
    own = row_h == key_h
    for b in range(SAMPLE_BB):
        k2 = _bf16(mk_ref[b].reshape(nk, MEM_HEAD_DIM))
        v2 = _bf16(mv_ref[b].reshape(nk, MEM_HEAD_DIM))
        s = jnp.where(own, _dot_nt(_bf16(q_ref[b]), k2), -jnp.inf)
        m = jnp.max(s, axis=1, keepdims=True)
        p = jnp.exp(s - m)
        den = jnp.sum(p, axis=1, keepdims=True)
        o_ref[b] = _dot(_bf16(p), v2) / den


def _mem_attn_sample(qm, mk, mv):
    nb = mk.shape[0]
    bb = SAMPLE_BB
    q = qm.reshape(nb, DEC_SEQ, MEM_HEADS, MEM_HEAD_DIM).transpose(0, 2, 1, 3).reshape(nb, MEM_ROWS, MEM_HEAD_DIM)
    rows = pl.BlockSpec((bb, MEM_ROWS, MEM_HEAD_DIM), lambda i: (i, 0, 0))
    kv = pl.BlockSpec((bb, N_MEM, MEM_HEADS, MEM_HEAD_DIM), lambda i: (i, 0, 0, 0))
    o = pl.pallas_call(
        _mem_attn_sample_kernel,
        grid=(nb // bb,),
        in_specs=[rows, kv, kv],
        out_specs=rows,
        out_shape=jax.ShapeDtypeStruct((nb, MEM_ROWS, MEM_HEAD_DIM), jnp.float32),
        compiler_params=_params(("arbitrary",)),
        name="mem_attn_sample",
    )(q, mk, mv)
    return o.reshape(nb, MEM_HEADS, DEC_SEQ, MEM_HEAD_DIM).transpose(0, 2, 1, 3).reshape(nb * DEC_SEQ, D_MODEL)


def _post_b_sample_kernel(o_ref, h1_ref, wo_ref, g2_ref, b2_ref, wrhl_ref, br_ref, tri_ref,
                          h2t_ref, rt_ref, cnt_ref):
    h2 = _layer_norm(ALPHA * h1_ref[...] + _dot(_bf16(o_ref[...]), wo_ref[...]), g2_ref[...], b2_ref[...])
    _store_token_tiles(h2t_ref, h2)
    route, carry = _route(h2, wrhl_ref, br_ref, tri_ref, jnp.zeros((1, LANES), jnp.float32))
    rt_ref[...] = route
    cnt_ref[...] = jnp.broadcast_to(carry, cnt_ref.shape)


def _post_b_sample(o, h1, wo_b, g2, b2, wrhl, br, tri):
    n = h1.shape[0]
    args = (o, h1, wo_b, g2, b2, wrhl, br, tri)
    full = lambda a: pl.BlockSpec(a.shape, lambda i: (0,) * a.ndim)
    return pl.pallas_call(
        _post_b_sample_kernel,
        grid=(1,),
        in_specs=[full(a) for a in args],
        out_specs=[pl.BlockSpec((n * ROW_CHUNKS, LANES), lambda i: (0, 0)),
                   pl.BlockSpec((n, LANES), lambda i: (0, 0)),
                   pl.BlockSpec((SUBLANES, LANES), lambda i: (0, 0))],
        out_shape=[jax.ShapeDtypeStruct((n * ROW_CHUNKS, LANES), jnp.float32),
                   jax.ShapeDtypeStruct((n, LANES), jnp.float32),
                   jax.ShapeDtypeStruct((SUBLANES, LANES), jnp.float32)],
        compiler_params=_params(("arbitrary",)),
        name="post_b_sample",
    )(*args)


def _row_gather_copy(src_hbm, idx, dst, dst_row, sem):
    s0 = pl.multiple_of(idx * ROW_CHUNKS, ROW_CHUNKS)
    d0 = pl.multiple_of(dst_row * ROW_CHUNKS, ROW_CHUNKS)
    return pltpu.make_async_copy(src_hbm.at[pl.ds(s0, ROW_CHUNKS), :], dst.at[pl.ds(d0, ROW_CHUNKS), :], sem)


def _dispatch_kernel(pos_ref, h2t_ref, xs_hbm, sem):
    def body(r, c):
        src = h2t_ref.at[pl.ds(pl.multiple_of(r * ROW_CHUNKS, ROW_CHUNKS), ROW_CHUNKS), :]
        for k in range(2):
            d0 = pl.multiple_of(pos_ref[0, 0, k * TM_COMB + r] * ROW_CHUNKS, ROW_CHUNKS)
            pltpu.make_async_copy(src, xs_hbm.at[pl.ds(d0, ROW_CHUNKS), :], sem.at[0]).start(priority=k)
        return c
    lax.fori_loop(0, TM_COMB, body, 0, unroll=8)
    for _ in range(2):
        pltpu.make_async_copy(h2t_ref, xs_hbm.at[pl.ds(0, TM_COMB * ROW_CHUNKS), :], sem.at[0]).wait()


def _dispatch(pos3, h2t):
    nt = N_ALL // TM_COMB
    return pl.pallas_call(
        _dispatch_kernel,
        grid=(nt,),
        in_specs=[pl.BlockSpec((1, 1, 2 * TM_COMB), lambda i: (i, 0, 0), memory_space=pltpu.SMEM),
                  pl.BlockSpec((TM_COMB * ROW_CHUNKS, LANES), lambda i: (i, 0))],
        out_specs=pl.BlockSpec(memory_space=pl.ANY),
        out_shape=jax.ShapeDtypeStruct((N_ASSIGN * ROW_CHUNKS, LANES), jnp.float32),
        scratch_shapes=[pltpu.SemaphoreType.DMA((1,))],
        compiler_params=_params(("arbitrary",)),
        name="moe_dispatch",
    )(pos3, h2t)


def _moe_ffn_kernel(it_ref, ie_ref, lo_ref, hi_ref, x_ref, wg_ref, wu_ref, wd_ref, y_ref, wgb, wub, wdb, cur_e):
    i = pl.program_id(0)
    lo = lo_ref[i]
    hi = hi_ref[i]
    e = ie_ref[i]

    @pl.when(i == 0)
    def _():
        cur_e[0] = -1

    @pl.when((hi > lo) & (cur_e[0] != e))
    def _():
        wgb[...] = _bf16(wg_ref[0])
        wub[...] = _bf16(wu_ref[0])
        wdb[...] = _bf16(wd_ref[0])
        cur_e[0] = e

    def ffn():
        x = _bf16(_load_token_tiles(x_ref, 0, TM_MOE))
        hg = _dot(x, wgb[...])
        hu = _dot(x, wub[...])
        h = hg / (1.0 + jnp.exp(-hg)) * hu
        return _dot(_bf16(h), wdb[...])

    def rows_mask():
        row = lax.broadcasted_iota(jnp.int32, (TM_MOE, LANES), 0)
        return (row >= lo) & (row < hi)

    @pl.when((hi > lo) & (lo == 0))
    def _():
        y = ffn()
        mask = rows_mask()
        for c in range(ROW_CHUNKS):
            y_ref[pl.ds(c, TM_MOE, stride=ROW_CHUNKS), :] = jnp.where(mask, y[:, c * LANES:(c + 1) * LANES], 0.0)

    @pl.when((hi > lo) & (lo > 0))
    def _():
        y = ffn()
        mask = rows_mask()
        for c in range(ROW_CHUNKS):
            sl = pl.ds(c, TM_MOE, stride=ROW_CHUNKS)
            y_ref[sl, :] = jnp.where(mask, y[:, c * LANES:(c + 1) * LANES], y_ref[sl, :])


def _moe_ffn(item_tile, item_expert, item_lo, item_hi, x_sorted, w_gate, w_up, w_down):
    wspec = lambda shp: pl.BlockSpec((1,) + shp, lambda i, it, ie, lo, hi: (ie[i], 0, 0))
    tile = pl.BlockSpec((TM_MOE * ROW_CHUNKS, LANES), lambda i, it, ie, lo, hi: (it[i], 0))
    grid_spec = pltpu.PrefetchScalarGridSpec(
        num_scalar_prefetch=4,
        grid=(MOE_ITEMS,),
        in_specs=[tile, wspec((D_MODEL, EXPERT_FF)), wspec((D_MODEL, EXPERT_FF)), wspec((EXPERT_FF, D_MODEL))],
        out_specs=tile,
        scratch_shapes=[pltpu.VMEM((D_MODEL, EXPERT_FF), jnp.bfloat16),
                        pltpu.VMEM((D_MODEL, EXPERT_FF), jnp.bfloat16),
                        pltpu.VMEM((EXPERT_FF, D_MODEL), jnp.bfloat16),
                        pltpu.SMEM((1,), jnp.int32)],
    )
    return pl.pallas_call(
        _moe_ffn_kernel,
        grid_spec=grid_spec,
        out_shape=jax.ShapeDtypeStruct((N_ASSIGN * ROW_CHUNKS, LANES), jnp.float32),
        compiler_params=_params(("arbitrary",)),
        name="moe_ffn",
    )(item_tile, item_expert, item_lo, item_hi, x_sorted, w_gate, w_up, w_down)


def _combine_kernel(nt, pos_cur_ref, pos_nxt_ref, yt_hbm, h2t_ref, rt_ref, g3_ref, b3_ref, o_ref, abuf, sem):
    t = pl.program_id(0)
    slot = t % 2
    rows = 2 * TM_COMB

    def issue(pos_ref, s):
        def body(j, c):
            for k in range(2):
                r = 2 * j + k
                _row_gather_copy(yt_hbm, pos_ref[0, 0, r], abuf, s * rows + r, sem.at[s]).start(priority=k)
            return c
        lax.fori_loop(0, rows // 2, body, 0, unroll=4)

    @pl.when(t == 0)
    def _():
        issue(pos_cur_ref, 0)

    @pl.when(t + 1 < nt)
    def _():
        issue(pos_nxt_ref, 1 - slot)

    base = pl.multiple_of(slot * (rows * ROW_CHUNKS), rows * ROW_CHUNKS)
    pltpu.make_async_copy(yt_hbm.at[pl.ds(0, rows * ROW_CHUNKS), :],
                          abuf.at[pl.ds(base, rows * ROW_CHUNKS), :], sem.at[slot]).wait()
    ya = _load_token_tiles(abuf, base, TM_COMB)
    yb = _load_token_tiles(abuf, base + TM_COMB * ROW_CHUNKS, TM_COMB)
    rt = rt_ref[...]
    ff = rt[:, 2:3] * ya + rt[:, 3:4] * yb
    h2 = _load_token_tiles(h2t_ref, 0, TM_COMB)
    o_ref[...] = _layer_norm(ALPHA * h2 + ff, g3_ref[...], b3_ref[...])


def _combine(pos3, yt, h2t, rt, g3, b3, tile0, n_tiles):
    last = tile0 + n_tiles - 1
    smem_pos = lambda f: pl.BlockSpec((1, 1, 2 * TM_COMB), f, memory_space=pltpu.SMEM)
    full = lambda a: pl.BlockSpec(a.shape, lambda i: (0,) * a.ndim)
    return pl.pallas_call(
        functools.partial(_combine_kernel, n_tiles),
        grid=(n_tiles,),
        in_specs=[smem_pos(lambda i: (tile0 + i, 0, 0)),
                  smem_pos(lambda i: (jnp.minimum(tile0 + i + 1, last), 0, 0)),
                  pl.BlockSpec(memory_space=pl.ANY),
                  pl.BlockSpec((TM_COMB * ROW_CHUNKS, LANES), lambda i: (tile0 + i, 0)),
                  pl.BlockSpec((TM_COMB, LANES), lambda i: (tile0 + i, 0)),
                  full(g3), full(b3)],
        out_specs=pl.BlockSpec((TM_COMB, D_MODEL), lambda i: (i, 0)),
        out_shape=jax.ShapeDtypeStruct((n_tiles * TM_COMB, D_MODEL), jnp.float32),
        scratch_shapes=[pltpu.VMEM((2 * 2 * TM_COMB * ROW_CHUNKS, LANES), jnp.float32),
                        pltpu.SemaphoreType.DMA((2,))],
        compiler_params=_params(("arbitrary",)),
        name="moe_combine",
    )(pos3, pos3, yt, h2t, rt, g3, b3)


POS_TILES = 3


def _positions_kernel(rt_ref, starts_ref, pos_ref):
    lane = lax.broadcasted_iota(jnp.int32, (TM_COMB, LANES), 1)
    lane_f = lane.astype(jnp.float32)
    starts = starts_ref[0:1, :]
    for j in range(POS_TILES):
        rt = rt_ref[j * TM_COMB:(j + 1) * TM_COMB, :]
        cols = []
        for k in range(2):
            seg = jnp.sum(jnp.where(lane_f == rt[:, k:k + 1] + N_GROUPS, starts, 0.0), axis=1, keepdims=True)
            cols.append(seg + rt[:, 4 + k:5 + k])
        packed = jnp.where(lane == 0, cols[0], jnp.where(lane == 1, cols[1], 0.0))
        rows = packed.T
        pos_ref[j] = jnp.concatenate([rows[0:1, :], rows[1:2, :]], axis=1).astype(jnp.int32)


def _positions(rt, starts_row):
    nt = N_ALL // TM_COMB
    assert nt % POS_TILES == 0
    return pl.pallas_call(
        _positions_kernel,
        grid=(nt // POS_TILES,),
        in_specs=[pl.BlockSpec((POS_TILES * TM_COMB, LANES), lambda i: (i, 0)),
                  pl.BlockSpec((SUBLANES, LANES), lambda i: (0, 0))],
        out_specs=pl.BlockSpec((POS_TILES, 1, 2 * TM_COMB), lambda i: (i, 0, 0)),
        out_shape=jax.ShapeDtypeStruct((nt, 1, 2 * TM_COMB), jnp.int32),
        compiler_params=_params(("arbitrary",)),
        name="moe_positions",
    )(rt, starts_row)


def _routing_plan(rt, cnt):
    i32 = jnp.int32
    counts_f = cnt[0, N_GROUPS:N_GROUPS + N_EXPERTS]
    starts_f = jnp.cumsum(counts_f) - counts_f
    starts_row = jnp.broadcast_to(
        jnp.pad(starts_f, (N_GROUPS, LANES - N_GROUPS - N_EXPERTS))[None, :], (SUBLANES, LANES))
    pos3 = _positions(rt, starts_row)
    starts = starts_f.astype(i32)
    tiles = jnp.arange(MOE_TILES, dtype=i32) * TM_MOE
    rank_t = jnp.arange(MOE_TILES, dtype=i32) + jnp.sum((starts[None, :] < tiles[:, None]).astype(i32), axis=1)
    rank_s = jnp.arange(N_EXPERTS, dtype=i32) + jnp.sum((tiles[None, :] <= starts[:, None]).astype(i32), axis=1)
    vals = jnp.concatenate([tiles, starts])
    ranks = jnp.concatenate([rank_t, rank_s])
    slot = jnp.arange(MOE_ITEMS, dtype=i32)
    lo = jnp.sum(jnp.where(ranks[None, :] == slot[:, None], vals[None, :], 0), axis=1)
    hi = jnp.concatenate([lo[1:], jnp.full((1,), N_ASSIGN, i32)])
    item_tile = jnp.minimum(lo // TM_MOE, MOE_TILES - 1)
    item_expert = jnp.clip(jnp.sum((starts[None, :] <= lo[:, None]).astype(i32), axis=1) - 1, 0, N_EXPERTS - 1)
    base = item_tile * TM_MOE
    return item_tile, item_expert, lo - base, hi - base, pos3


def kernel(x_prompt, x_sample, mem_prompt, cache_swa_k, cache_swa_v, cache_conv, cache_mem_k, cache_mem_v,
           w_in, sinks, conv_w, w_mix_out, ln1_g, ln1_b, w_q_mem, w_k_mem, w_v_mem, w_o_mem, ln2_g, ln2_b,
           w_router_group, b_router_group, w_router_expert, b_router_expert, w_gate, w_up, w_down,
           ln3_g, ln3_b):
    f32 = jnp.float32
    row = lambda a: a.reshape(1, -1).astype(f32)
    w_in_b, wmix_b, wq_b, wk_b, wv_b, wo_b = (_bf16(w) for w in (w_in, w_mix_out, w_q_mem, w_k_mem, w_v_mem, w_o_mem))
    g1, b1, g2, b2, g3, b3 = (row(a) for a in (ln1_g, ln1_b, ln2_g, ln2_b, ln3_g, ln3_b))
    pad = LANES - N_GROUPS - N_EXPERTS
    wr = jnp.concatenate([w_router_group, w_router_expert, jnp.zeros((D_MODEL, pad), f32)], axis=1)
    wrh = _bf16(wr)
    wrhl = jnp.concatenate([wrh, _bf16(wr - wrh.astype(f32))], axis=1)
    br = jnp.concatenate([b_router_group, b_router_expert, jnp.zeros((pad,), f32)]).reshape(1, LANES)

    xs = x_sample.reshape(N_SAMPLE, D_MODEL)
    tab_s = jnp.tile(_rope_table(PAST_LEN + jnp.arange(DEC_SEQ)), (1, DEC_BATCH))
    c0 = jnp.repeat(cache_conv[:, 0], DEC_SEQ, axis=0)
    c1 = jnp.repeat(cache_conv[:, 1], DEC_SEQ, axis=0)
    q_s, k_s, v_s, conv_s, u_s = _proj_sample(xs, w_in_b, tab_s, conv_w, c0, c1)
    attn_s, swa_k_s, swa_v_s = _swa_sample(sinks, q_s, k_s, v_s, cache_swa_k, cache_swa_v)
    h1_s, qm_s = _post_a_sample(attn_s, conv_s, xs, wmix_b, g1, b1, wq_b)
    o_s = _mem_attn_sample(qm_s, cache_mem_k, cache_mem_v)
    tri = _bf16(jnp.tril(jnp.ones((TM_POST, TM_POST), f32), -1))
    h2t_s, rt_s, cnt_s = _post_b_sample(o_s, h1_s, wo_b, g2, b2, wrhl, br, tri)

    xp = x_prompt.reshape(N_PROMPT, D_MODEL)
    tab_p = _rope_table(jnp.arange(N_PROMPT))
    q_p, kx_p, vx_p, conv_p, k_tail, v_tail, u_tail = _proj_prompt(xp, w_in_b, tab_p, conv_w)
    attn_p = _swa_prompt(sinks, q_p, kx_p, vx_p)
    mk, mv, mk_b, mv_b = _mem_kv(mem_prompt.reshape(N_MEM, D_MODEL), wk_b, wv_b)
    h2t, rt, cnt = _post_prompt(attn_p, conv_p, xp, wmix_b, g1, b1, wq_b, mk_b, mv_b, wo_b, g2, b2,
                                wrhl, br, tri, h2t_s, rt_s, cnt_s)

    item_tile, item_expert, item_lo, item_hi, pos3 = _routing_plan(rt, cnt)
    x_sorted = _dispatch(pos3, h2t)
    yt = _moe_ffn(item_tile, item_expert, item_lo, item_hi, x_sorted, w_gate, w_up, w_down)
    y_p = _combine(pos3, yt, h2t, rt, g3, b3, 0, N_PROMPT // TM_COMB)
    y_s = _combine(pos3, yt, h2t, rt, g3, b3, N_PROMPT // TM_COMB, N_SAMPLE // TM_COMB)

    return (y_p.reshape(1, SEQ, D_MODEL),
            y_s.reshape(DEC_BATCH, DEC_SEQ, D_MODEL),
            k_tail.reshape(1, WINDOW, N_KV_HEADS, HEAD_DIM),
            v_tail.reshape(1, WINDOW, N_KV_HEADS, HEAD_DIM),
            u_tail[SUBLANES - (CONV_K - 1):].reshape(1, CONV_K - 1, CONV_CH),
            mk.reshape(1, N_MEM, MEM_HEADS, MEM_HEAD_DIM),
            mv.reshape(1, N_MEM, MEM_HEADS, MEM_HEAD_DIM),
            swa_k_s.reshape(DEC_BATCH, WINDOW, N_KV_HEADS, HEAD_DIM),
            swa_v_s.reshape(DEC_BATCH, WINDOW, N_KV_HEADS, HEAD_DIM),
            u_s.reshape(DEC_BATCH, DEC_SEQ, CONV_CH)[:, DEC_SEQ - (CONV_K - 1):])
```

```python
import functools

import jax
import jax.numpy as jnp
from jax import lax
from jax.experimental import pallas as pl
from jax.experimental.pallas import tpu as pltpu

D_MODEL = 1024
SEQ = 16384
DEC_BATCH = 128
DEC_SEQ = 4
PAST_LEN = 16384
ATTN_WIDTH = 512
CONV_CH = 512
HEAD_DIM = 64
N_HEADS = 8
N_KV_HEADS = 2
KV_WIDTH = 128
WINDOW = 128
ROPE_THETA = 500000.0
ROPE_DIM = 16
CONV_K = 3
Q_END = ATTN_WIDTH
K_END = Q_END + KV_WIDTH
V_END = K_END + KV_WIDTH
B_END = V_END + CONV_CH
C_END = B_END + CONV_CH
IN_WIDTH = C_END + CONV_CH
N_MEM = 256
MEM_HEADS = 4
MEM_HEAD_DIM = 256
N_GROUPS = 4
EXPERTS_PER_GROUP = 8
N_EXPERTS = 32
EXPERT_FF = 256
ALPHA = 2.0 ** 0.25
LN_EPS = 1e-5

LANES = 128
SUBLANES = 8
ROW_CHUNKS = D_MODEL // LANES
VMEM_LIMIT = 56 * 1024 * 1024

N_PROMPT = SEQ
N_SAMPLE = DEC_BATCH * DEC_SEQ
N_ALL = N_PROMPT + N_SAMPLE
TM_PROJ = 1024
TM_POST = 512
TM_MOE = 512
TM_COMB = 512
N_ASSIGN = 2 * N_ALL
MOE_TILES = N_ASSIGN // TM_MOE
MOE_ITEMS = MOE_TILES + N_EXPERTS
SAMPLE_BB = 4
SWA_BB = 8

assert ROW_CHUNKS == SUBLANES
assert N_SAMPLE == TM_POST
assert N_ASSIGN % TM_MOE == 0 and N_ALL % TM_COMB == 0


def _params(sem, vmem=VMEM_LIMIT):
    return pltpu.CompilerParams(dimension_semantics=sem, vmem_limit_bytes=vmem)


def _bf16(x):
    return x.astype(jnp.bfloat16)


def _dot(a, b):
    return jnp.dot(a, b, preferred_element_type=jnp.float32)


def _dot_nt(a, b):
    return lax.dot_general(a, b, (((1,), (1,)), ((), ())), preferred_element_type=jnp.float32)


def _layer_norm(x, g, b):
    mu = jnp.mean(x, axis=-1, keepdims=True)
    xc = x - mu
    var = jnp.mean(xc * xc, axis=-1, keepdims=True)
    return xc * lax.rsqrt(var + LN_EPS) * g + b


def _rope(x, cos_t, sin_t):
    lane = lax.broadcasted_iota(jnp.int32, x.shape, 1) % HEAD_DIM
    half = ROPE_DIM // 2
    partner = jnp.where(lane < half, pltpu.roll(x, LANES - half, axis=1), pltpu.roll(x, half, axis=1))
    return x * cos_t + partner * sin_t


def _head_slabs(x):
    lane = lax.broadcasted_iota(jnp.int32, x.shape, 1)
    lo = lane < HEAD_DIM
    sw = pltpu.roll(x, HEAD_DIM, axis=1)
    zero = jnp.zeros_like(x)
    slabs = [jnp.where(lo, x, zero), jnp.where(lo, zero, sw), jnp.where(lo, sw, zero), jnp.where(lo, zero, x)]
    return _bf16(jnp.concatenate(slabs, axis=1))


def _store_token_tiles(ref, val):
    rows = val.shape[0]
    for c in range(ROW_CHUNKS):
        ref[pl.ds(c, rows, stride=ROW_CHUNKS), :] = val[:, c * LANES:(c + 1) * LANES]


def _load_token_tiles(ref, base, rows):
    return jnp.concatenate(
        [ref[pl.ds(base + c, rows, stride=ROW_CHUNKS), :] for c in range(ROW_CHUNKS)], axis=1)


ROPE_ONE = 3 * (ROPE_DIM // 2)
ROPE_ROWS = 32


def _rope_patterns(tab):
    half = ROPE_DIM // 2
    m = lax.broadcasted_iota(jnp.int32, tab.shape, 1) % HEAD_DIM
    idx_c = jnp.where(m < ROPE_DIM, m % half, ROPE_ONE)
    idx_s = jnp.where(m < half, 2 * half + m, jnp.where(m < ROPE_DIM, m, ROPE_ONE + 1))
    return jnp.take_along_axis(tab, idx_c, axis=1), jnp.take_along_axis(tab, idx_s, axis=1)


def _proj_common(x_ref, w_ref, tab_ref):
    xb = _bf16(x_ref[...])
    tab = tab_ref[...]
    pad = jnp.zeros((LANES - tab.shape[0], tab.shape[1]), jnp.float32)
    cos_t, sin_t = _rope_patterns(jnp.concatenate([tab, pad], axis=0).T)
    q = _dot(xb, w_ref[:, 0:Q_END])
    q_rot = jnp.concatenate(
        [_rope(q[:, p * LANES:(p + 1) * LANES], cos_t, sin_t) for p in range(ATTN_WIDTH // LANES)], axis=1)
    q_out = _bf16(q_rot * (HEAD_DIM ** -0.5))
    kv = _dot(xb, w_ref[:, Q_END:V_END])
    k = _rope(kv[:, 0:KV_WIDTH], cos_t, sin_t)
    v = kv[:, KV_WIDTH:]
    bg = _dot(xb, w_ref[:, V_END:B_END])
    u = _dot(xb, w_ref[:, B_END:C_END]) * _dot(xb, w_ref[:, C_END:IN_WIDTH])
    return q_out, k, v, bg, u


def _conv3(bg, u, u1, u2, cw_ref):
    cw = cw_ref[...]
    return bg * (cw[0:1, :] * u2 + cw[1:2, :] * u1 + cw[2:3, :] * u)


def _proj_prompt_kernel(x_ref, w_ref, tab_ref, cw_ref,
                        q_ref, kx_ref, vx_ref, conv_ref, ktail_ref, vtail_ref, utail_ref, carry_ref):
    @pl.when(pl.program_id(0) == 0)
    def _():
        carry_ref[...] = jnp.zeros_like(carry_ref)

    q_out, k, v, bg, u = _proj_common(x_ref, w_ref, tab_ref)
    tm = u.shape[0]
    ext = jnp.concatenate([carry_ref[...], u], axis=0)
    u1 = pltpu.roll(ext, 1, axis=0)[SUBLANES:SUBLANES + tm]
    u2 = pltpu.roll(ext, 2, axis=0)[SUBLANES:SUBLANES + tm]
    q_ref[...] = q_out
    kx_ref[...] = _head_slabs(k)
    vx_ref[...] = _head_slabs(v)
    conv_ref[...] = _bf16(_conv3(bg, u, u1, u2, cw_ref))
    ktail_ref[...] = k[tm - WINDOW:tm]
    vtail_ref[...] = v[tm - WINDOW:tm]
    utail_ref[...] = u[tm - SUBLANES:tm]
    carry_ref[...] = u[tm - SUBLANES:tm]


def _proj_sample_kernel(x_ref, w_ref, tab_ref, cw_ref, c0_ref, c1_ref,
                        q_ref, k_ref, v_ref, conv_ref, u_ref):
    q_out, k, v, bg, u = _proj_common(x_ref, w_ref, tab_ref)
    t = lax.broadcasted_iota(jnp.int32, u.shape, 0) % DEC_SEQ
    c0 = c0_ref[...]
    c1 = c1_ref[...]
    u1 = jnp.where(t >= 1, pltpu.roll(u, 1, axis=0), c1)
    u2 = jnp.where(t >= 2, pltpu.roll(u, 2, axis=0), jnp.where(t == 1, c1, c0))
    q_ref[...] = q_out.astype(jnp.float32)
    k_ref[...] = k
    v_ref[...] = v
    conv_ref[...] = _bf16(_conv3(bg, u, u1, u2, cw_ref))
    u_ref[...] = u


def _rope_table(pos):
    half = ROPE_DIM // 2
    inv = ROPE_THETA ** (-jnp.arange(0, ROPE_DIM, 2, dtype=jnp.float32) / ROPE_DIM)
    ang = pos.astype(jnp.float32)[None, :] * inv[:, None]
    cos, sin = jnp.cos(ang), jnp.sin(ang)
    n = pos.shape[0]
    assert ROPE_ONE == 3 * half
    return jnp.concatenate([cos, sin, -sin, jnp.ones((1, n), jnp.float32),
                            jnp.zeros((ROPE_ROWS - ROPE_ONE - 1, n), jnp.float32)], axis=0)


def _proj_prompt(x, w_in_b, tab, conv_w):
    n = x.shape[0]
    tm = TM_PROJ
    row = lambda w: pl.BlockSpec((tm, w), lambda i: (i, 0))
    full = lambda a: pl.BlockSpec(a.shape, lambda i: (0,) * a.ndim)
    const = lambda r, w: pl.BlockSpec((r, w), lambda i: (0, 0))
    return pl.pallas_call(
        _proj_prompt_kernel,
        grid=(n // tm,),
        in_specs=[row(D_MODEL), full(w_in_b), pl.BlockSpec((ROPE_ROWS, tm), lambda i: (0, i)), full(conv_w)],
        out_specs=[row(ATTN_WIDTH), row(4 * LANES), row(4 * LANES), row(CONV_CH),
                   const(WINDOW, KV_WIDTH), const(WINDOW, KV_WIDTH), const(SUBLANES, CONV_CH)],
        out_shape=[jax.ShapeDtypeStruct((n, ATTN_WIDTH), jnp.bfloat16),
                   jax.ShapeDtypeStruct((n, 4 * LANES), jnp.bfloat16),
                   jax.ShapeDtypeStruct((n, 4 * LANES), jnp.bfloat16),
                   jax.ShapeDtypeStruct((n, CONV_CH), jnp.bfloat16),
                   jax.ShapeDtypeStruct((WINDOW, KV_WIDTH), jnp.float32),
                   jax.ShapeDtypeStruct((WINDOW, KV_WIDTH), jnp.float32),
                   jax.ShapeDtypeStruct((SUBLANES, CONV_CH), jnp.float32)],
        scratch_shapes=[pltpu.VMEM((SUBLANES, CONV_CH), jnp.float32)],
        compiler_params=_params(("arbitrary",)),
        name="proj_prompt",
    )(x, w_in_b, tab, conv_w)


def _proj_sample(x, w_in_b, tab, conv_w, c0, c1):
    n = x.shape[0]
    full = lambda a: pl.BlockSpec(a.shape, lambda i: (0,) * a.ndim)
    out = lambda w, dt: jax.ShapeDtypeStruct((n, w), dt)
    blk = lambda w: pl.BlockSpec((n, w), lambda i: (0, 0))
    return pl.pallas_call(
        _proj_sample_kernel,
        grid=(1,),
        in_specs=[full(x), full(w_in_b), full(tab), full(conv_w), full(c0), full(c1)],
        out_specs=[blk(ATTN_WIDTH), blk(KV_WIDTH), blk(KV_WIDTH), blk(CONV_CH), blk(CONV_CH)],
        out_shape=[out(ATTN_WIDTH, jnp.float32), out(KV_WIDTH, jnp.float32), out(KV_WIDTH, jnp.float32),
                   out(CONV_CH, jnp.bfloat16), out(CONV_CH, jnp.float32)],
        compiler_params=_params(("arbitrary",)),
        name="proj_sample",
    )(x, w_in_b, tab, conv_w, c0, c1)


def _sink_softmax_pv(s, valid, sink, vx):
    s = jnp.where(valid, s, -jnp.inf)
    m = jnp.maximum(jnp.max(s, axis=1, keepdims=True), sink)
    p = jnp.exp(s - m)
    den = jnp.sum(p, axis=1, keepdims=True) + jnp.exp(sink - m)
    return _dot(_bf16(p), vx) / den


SWA_QB = 4


def _swa_prompt_kernel(sinks_ref, q_ref, kc_ref, kp_ref, vc_ref, vp_ref, o_ref):
    step = pl.program_id(0)
    kall = jnp.concatenate([kp_ref[...], kc_ref[...]], axis=0)
    vall = jnp.concatenate([vp_ref[...], vc_ref[...]], axis=0)
    i = lax.broadcasted_iota(jnp.int32, (WINDOW, 2 * WINDOW), 0)
    j = lax.broadcasted_iota(jnp.int32, (WINDOW, 2 * WINDOW), 1)
    band = (j > i) & (j <= i + WINDOW)
    for sb in range(SWA_QB):
        rows = slice(sb * WINDOW, (sb + 1) * WINDOW)
        kcat = kall[sb * WINDOW:(sb + 2) * WINDOW]
        vcat = vall[sb * WINDOW:(sb + 2) * WINDOW]
        valid = band & ((step > 0) | (j >= WINDOW)) if sb == 0 else band
        for p in range(N_HEADS // 2):
            qs = q_ref[rows, p * LANES:(p + 1) * LANES]
            acc = None
            for e in range(2):
                hd = 2 * p + e
                slab = 2 * (hd // (N_HEADS // N_KV_HEADS)) + e
                kx = kcat[:, slab * LANES:(slab + 1) * LANES]
                vx = vcat[:, slab * LANES:(slab + 1) * LANES]
                o = _sink_softmax_pv(_dot_nt(qs, kx), valid, sinks_ref[hd], vx)
                acc = o if acc is None else acc + o
            o_ref[rows, p * LANES:(p + 1) * LANES] = _bf16(acc)


def _swa_prompt(sinks, q, kx, vx):
    n = q.shape[0]
    nb = n // (SWA_QB * WINDOW)
    cur = lambda w: pl.BlockSpec((SWA_QB * WINDOW, w), lambda i: (i, 0))
    prev = lambda w: pl.BlockSpec((WINDOW, w), lambda i: (jnp.maximum(SWA_QB * i - 1, 0), 0))
    return pl.pallas_call(
        _swa_prompt_kernel,
        grid=(nb,),
        in_specs=[pl.BlockSpec(memory_space=pltpu.SMEM), cur(ATTN_WIDTH),
                  cur(4 * LANES), prev(4 * LANES), cur(4 * LANES), prev(4 * LANES)],
        out_specs=cur(ATTN_WIDTH),
        out_shape=jax.ShapeDtypeStruct((n, ATTN_WIDTH), jnp.bfloat16),
        compiler_params=_params(("arbitrary",)),
        name="swa_prompt",
    )(sinks, q, kx, kx, vx, vx)


SWA_ROWS = N_HEADS * DEC_SEQ
NEW_ROWS = 2 * SUBLANES


def _swa_sample_kernel(q_ref, sink_ref, kn_ref, vn_ref, kt_ref, vt_ref, o_ref, okt_ref, ovt_ref):
    t = lax.broadcasted_iota(jnp.int32, (SWA_ROWS, WINDOW), 0) % DEC_SEQ
    j = lax.broadcasted_iota(jnp.int32, (SWA_ROWS, WINDOW), 1)
    valid_c = j > t
    valid_n = (lax.broadcasted_iota(jnp.int32, (SWA_ROWS, NEW_ROWS), 1)
               <= lax.broadcasted_iota(jnp.int32, (SWA_ROWS, NEW_ROWS), 0) % DEC_SEQ)
    lane = lax.broadcasted_iota(jnp.int32, (KV_WIDTH, WINDOW), 1)
    sink = sink_ref[:, 0:1]
    shift = WINDOW - DEC_SEQ
    zrows = jnp.zeros((KV_WIDTH - NEW_ROWS, KV_WIDTH), jnp.float32)
    for b in range(SWA_BB):
        q = _bf16(q_ref[b])
        kt, vt = kt_ref[b], vt_ref[b]
        kn, vn = kn_ref[b], vn_ref[b]
        s_c = jnp.where(valid_c, _dot(q, _bf16(kt)), -jnp.inf)
        s_n = jnp.where(valid_n, _dot_nt(q, _bf16(kn)), -jnp.inf)
        m = jnp.maximum(jnp.maximum(jnp.max(s_c, axis=1, keepdims=True), jnp.max(s_n, axis=1, keepdims=True)), sink)
        p_c = jnp.exp(s_c - m)
        p_n = jnp.exp(s_n - m)
        den = jnp.sum(p_c, axis=1, keepdims=True) + jnp.sum(p_n, axis=1, keepdims=True) + jnp.exp(sink - m)
        o_ref[b] = (_dot_nt(_bf16(p_c), _bf16(vt)) + _dot(_bf16(p_n), _bf16(vn))) / den
        for old, new, dst in ((kt, kn, okt_ref), (vt, vn, ovt_ref)):
            new_cols = pltpu.roll(jnp.concatenate([new, zrows], axis=0).T, shift, axis=1)
            dst[b] = jnp.where(lane >= shift, new_cols, pltpu.roll(old, shift, axis=1))


def _swa_sample(sinks, q, kn, vn, cache_k, cache_v):
    nb = cache_k.shape[0]
    bb = SWA_BB
    groups = N_HEADS // N_KV_HEADS
    qh = q.reshape(nb, DEC_SEQ, N_KV_HEADS, groups, HEAD_DIM).transpose(0, 2, 3, 1, 4)
    qh = qh.reshape(nb, N_KV_HEADS, groups * DEC_SEQ, HEAD_DIM)
    zeros = jnp.zeros_like(qh[:, 0])
    qbd = jnp.concatenate([jnp.concatenate([qh[:, 0], zeros], axis=-1),
                           jnp.concatenate([zeros, qh[:, 1]], axis=-1)], axis=1)
    sink_col = jnp.broadcast_to(jnp.repeat(sinks, DEC_SEQ).reshape(SWA_ROWS, 1), (SWA_ROWS, LANES))
    pad8 = lambda a: jnp.pad(a.reshape(nb, DEC_SEQ, KV_WIDTH), ((0, 0), (0, NEW_ROWS - DEC_SEQ), (0, 0)))
    to_t = lambda c: c.transpose(0, 2, 3, 1).reshape(nb, KV_WIDTH, WINDOW)
    blk = lambda r, w: pl.BlockSpec((bb, r, w), lambda i: (i, 0, 0))
    o, okt, ovt = pl.pallas_call(
        _swa_sample_kernel,
        grid=(nb // bb,),
        in_specs=[blk(SWA_ROWS, KV_WIDTH), pl.BlockSpec((SWA_ROWS, LANES), lambda i: (0, 0)),
                  blk(NEW_ROWS, KV_WIDTH), blk(NEW_ROWS, KV_WIDTH), blk(KV_WIDTH, WINDOW), blk(KV_WIDTH, WINDOW)],
        out_specs=[blk(SWA_ROWS, KV_WIDTH), blk(KV_WIDTH, WINDOW), blk(KV_WIDTH, WINDOW)],
        out_shape=[jax.ShapeDtypeStruct((nb, SWA_ROWS, KV_WIDTH), jnp.float32),
                   jax.ShapeDtypeStruct((nb, KV_WIDTH, WINDOW), jnp.float32),
                   jax.ShapeDtypeStruct((nb, KV_WIDTH, WINDOW), jnp.float32)],
        compiler_params=_params(("arbitrary",)),
        name="swa_sample",
    )(qbd, sink_col, pad8(kn), pad8(vn), to_t(cache_k), to_t(cache_v))
    o = o.reshape(nb, N_KV_HEADS, groups, DEC_SEQ, N_KV_HEADS, HEAD_DIM)
    attn = jnp.stack([o[:, h, :, :, h, :] for h in range(N_KV_HEADS)], axis=1)
    attn = attn.transpose(0, 3, 1, 2, 4).reshape(nb * DEC_SEQ, ATTN_WIDTH)
    from_t = lambda c: c.reshape(nb, N_KV_HEADS, HEAD_DIM, WINDOW).transpose(0, 3, 1, 2)
    return attn, from_t(okt), from_t(ovt)


def _mem_kv_kernel(mem_ref, wk_ref, wv_ref, mk_ref, mv_ref, mkb_ref, mvb_ref):
    mb = _bf16(mem_ref[...])
    mk = _dot(mb, wk_ref[...])
    mv = _dot(mb, wv_ref[...])
    mk_ref[...] = mk
    mv_ref[...] = mv
    mkb_ref[...] = _bf16(mk)
    mvb_ref[...] = _bf16(mv)


def _mem_kv(mem, wk_b, wv_b):
    full = lambda a: pl.BlockSpec(a.shape, lambda i: (0,) * a.ndim)
    blk = pl.BlockSpec((N_MEM, D_MODEL), lambda i: (0, 0))
    f32 = jax.ShapeDtypeStruct((N_MEM, D_MODEL), jnp.float32)
    b16 = jax.ShapeDtypeStruct((N_MEM, D_MODEL), jnp.bfloat16)
    return pl.pallas_call(
        _mem_kv_kernel,
        grid=(1,),
        in_specs=[full(mem), full(wk_b), full(wv_b)],
        out_specs=[blk, blk, blk, blk],
        out_shape=[f32, f32, b16, b16],
        compiler_params=_params(("arbitrary",)),
        name="mem_kv",
    )(mem, wk_b, wv_b)


def _mix_ln1(attn_ref, conv_ref, x_ref, wmix_ref, g1_ref, b1_ref):
    mix = _dot(_bf16(attn_ref[...]), wmix_ref[0:ATTN_WIDTH, :]) + _dot(conv_ref[...], wmix_ref[ATTN_WIDTH:, :])
    return _layer_norm(ALPHA * x_ref[...] + mix, g1_ref[...], b1_ref[...])


def _mem_q(h1, wq_ref):
    return _bf16(_dot(_bf16(h1), wq_ref[...]) * (MEM_HEAD_DIM ** -0.5))


def _route(h2, wrhl_ref, br_ref, tri_ref, carry):
    hi = _bf16(h2)
    lo = _bf16(h2 - hi.astype(jnp.float32))
    hh = _dot(hi, wrhl_ref[...])
    logits = hh[:, 0:LANES] + hh[:, LANES:] + _dot(lo, wrhl_ref[:, 0:LANES]) + br_ref[...]
    lane_i = lax.broadcasted_iota(jnp.int32, logits.shape, 1)
    lane = lane_i.astype(jnp.float32)
    big = jnp.float32(LANES)
    is_g = lane_i < N_GROUPS
    gl = jnp.where(is_g, logits, -jnp.inf)
    gmax = jnp.max(gl, axis=1, keepdims=True)
    gidx = jnp.min(jnp.where(is_g & (logits == gmax), lane, big), axis=1, keepdims=True)
    gsum = jnp.sum(jnp.exp(gl - gmax), axis=1, keepdims=True)
    gw = 1.0 / gsum
    eid = lane_i - N_GROUPS
    assert EXPERTS_PER_GROUP == 8
    grp = lax.shift_right_arithmetic(eid, jnp.full_like(eid, 3)).astype(jnp.float32)
    in_e = (lane_i >= N_GROUPS) & (lane_i < N_GROUPS + N_EXPERTS) & (grp == gidx)
    v1 = jnp.max(jnp.where(in_e, logits, -jnp.inf), axis=1, keepdims=True)
    i1 = jnp.min(jnp.where(in_e & (logits == v1), lane, big), axis=1, keepdims=True)
    rest = in_e & (lane != i1)
    v2 = jnp.max(jnp.where(rest, logits, -jnp.inf), axis=1, keepdims=True)
    i2 = jnp.min(jnp.where(rest & (logits == v2), lane, big), axis=1, keepdims=True)
    ex = jnp.exp(v2 - v1)
    den = 1.0 + ex
    w1 = gw / den
    w2 = gw * ex / den
    zero = jnp.zeros_like(logits)
    pick1 = lane == i1
    pick2 = lane == i2
    sel = jnp.where(pick1 | pick2, 1.0, 0.0)
    before = _dot(tri_ref[...], _bf16(sel)) + carry
    rank1 = jnp.sum(jnp.where(pick1, before, zero), axis=1, keepdims=True)
    rank2 = jnp.sum(jnp.where(pick2, before, zero), axis=1, keepdims=True)
    cols = (i1 - N_GROUPS, i2 - N_GROUPS, w1, w2, rank1, rank2)
    route = zero
    for k, col in enumerate(cols):
        route = jnp.where(lane_i == k, col, route)
    return route, carry + jnp.sum(sel, axis=0, keepdims=True)


def _post_prompt_kernel(attn_ref, conv_ref, x_ref, wmix_ref, g1_ref, b1_ref, wq_ref, mk_ref, mv_ref, wo_ref,
                        g2_ref, b2_ref, wrhl_ref, br_ref, tri_ref, h2s_ref, rts_ref, cnts_ref,
                        h2t_ref, rt_ref, cnt_ref, carry_ref):
    is_tail = pl.program_id(0) == N_PROMPT // TM_POST

    @pl.when(pl.program_id(0) == 0)
    def _():
        carry_ref[...] = cnts_ref[...]

    @pl.when(is_tail)
    def _():
        h2t_ref[...] = h2s_ref[...]
        rt_ref[...] = rts_ref[...]

    @pl.when(jnp.logical_not(is_tail))
    def _():
        h1 = _mix_ln1(attn_ref, conv_ref, x_ref, wmix_ref, g1_ref, b1_ref)
        qm = _mem_q(h1, wq_ref)
        outs = []
        for h in range(MEM_HEADS):
            sl = slice(h * MEM_HEAD_DIM, (h + 1) * MEM_HEAD_DIM)
            s = _dot_nt(qm[:, sl], mk_ref[:, sl])
            m = jnp.max(s, axis=1, keepdims=True)
            p = jnp.exp(s - m)
            den = jnp.sum(p, axis=1, keepdims=True)
            outs.append(_dot(_bf16(p), mv_ref[:, sl]) / den)
        o = _bf16(jnp.concatenate(outs, axis=1))
        h2 = _layer_norm(ALPHA * h1 + _dot(o, wo_ref[...]), g2_ref[...], b2_ref[...])
        _store_token_tiles(h2t_ref, h2)
        route, carry = _route(h2, wrhl_ref, br_ref, tri_ref, carry_ref[0:1, :])
        rt_ref[...] = route
        carry_ref[...] = jnp.broadcast_to(carry, carry_ref.shape)
        cnt_ref[...] = jnp.broadcast_to(carry, cnt_ref.shape)


def _post_prompt(attn, conv, x, wmix_b, g1, b1, wq_b, mk_b, mv_b, wo_b, g2, b2, wrhl, br, tri, h2t_s, rt_s, cnt_s):
    n = x.shape[0]
    tm = TM_POST
    steps = n // tm
    row = lambda w: pl.BlockSpec((tm, w), lambda i: (jnp.minimum(i, steps - 1), 0))
    full = lambda a: pl.BlockSpec(a.shape, lambda i: (0,) * a.ndim)
    weights = (wmix_b, g1, b1, wq_b, mk_b, mv_b, wo_b, g2, b2, wrhl, br, tri, h2t_s, rt_s, cnt_s)
    n_out = n + h2t_s.shape[0] // ROW_CHUNKS
    return pl.pallas_call(
        _post_prompt_kernel,
        grid=(steps + 1,),
        in_specs=[row(ATTN_WIDTH), row(CONV_CH), row(D_MODEL)] + [full(a) for a in weights],
        out_specs=[pl.BlockSpec((tm * ROW_CHUNKS, LANES), lambda i: (i, 0)),
                   pl.BlockSpec((tm, LANES), lambda i: (i, 0)),
                   pl.BlockSpec((SUBLANES, LANES), lambda i: (0, 0))],
        out_shape=[jax.ShapeDtypeStruct((n_out * ROW_CHUNKS, LANES), jnp.float32),
                   jax.ShapeDtypeStruct((n_out, LANES), jnp.float32),
                   jax.ShapeDtypeStruct((SUBLANES, LANES), jnp.float32)],
        scratch_shapes=[pltpu.VMEM((SUBLANES, LANES), jnp.float32)],
        compiler_params=_params(("arbitrary",)),
        name="post_prompt",
    )(attn, conv, x, *weights)


def _post_a_sample_kernel(attn_ref, conv_ref, x_ref, wmix_ref, g1_ref, b1_ref, wq_ref, h1_ref, qm_ref):
    h1 = _mix_ln1(attn_ref, conv_ref, x_ref, wmix_ref, g1_ref, b1_ref)
    h1_ref[...] = h1
    qm_ref[...] = _mem_q(h1, wq_ref).astype(jnp.float32)


def _post_a_sample(attn, conv, x, wmix_b, g1, b1, wq_b):
    n = x.shape[0]
    args = (attn, conv, x, wmix_b, g1, b1, wq_b)
    full = lambda a: pl.BlockSpec(a.shape, lambda i: (0,) * a.ndim)
    blk = pl.BlockSpec((n, D_MODEL), lambda i: (0, 0))
    return pl.pallas_call(
        _post_a_sample_kernel,
        grid=(1,),
        in_specs=[full(a) for a in args],
        out_specs=[blk, blk],
        out_shape=[jax.ShapeDtypeStruct((n, D_MODEL), jnp.float32),
                   jax.ShapeDtypeStruct((n, D_MODEL), jnp.float32)],
        compiler_params=_params(("arbitrary",)),
        name="post_a_sample",
    )(*args)


MEM_ROWS = MEM_HEADS * DEC_SEQ


def _mem_attn_sample_kernel(q_ref, mk_ref, mv_ref, o_ref):
    nk = N_MEM * MEM_HEADS
    row_h = lax.broadcasted_iota(jnp.int32, (MEM_ROWS, nk), 0) // DEC_SEQ
    key_h = lax.broadcasted_iota(jnp.int32, (MEM_ROWS, nk), 1) % MEM_HEADS
    own = row_h == key_h
    for b in range(SAMPLE_BB):
        k2 = _bf16(mk_ref[b].reshape(nk, MEM_HEAD_DIM))
        v2 = _bf16(mv_ref[b].reshape(nk, MEM_HEAD_DIM))
        s = jnp.where(own, _dot_nt(_bf16(q_ref[b]), k2), -jnp.inf)
        m = jnp.max(s, axis=1, keepdims=True)
        p = jnp.exp(s - m)
        den = jnp.sum(p, axis=1, keepdims=True)
        o_ref[b] = _dot(_bf16(p), v2) / den


def _mem_attn_sample(qm, mk, mv):
    nb = mk.shape[0]
    bb = SAMPLE_BB
    q = qm.reshape(nb, DEC_SEQ, MEM_HEADS, MEM_HEAD_DIM).transpose(0, 2, 1, 3).reshape(nb, MEM_ROWS, MEM_HEAD_DIM)
    rows = pl.BlockSpec((bb, MEM_ROWS, MEM_HEAD_DIM), lambda i: (i, 0, 0))
    kv = pl.BlockSpec((bb, N_MEM, MEM_HEADS, MEM_HEAD_DIM), lambda i: (i, 0, 0, 0))
    o = pl.pallas_call(
        _mem_attn_sample_kernel,
        grid=(nb // bb,),
        in_specs=[rows, kv, kv],
        out_specs=rows,
        out_shape=jax.ShapeDtypeStruct((nb, MEM_ROWS, MEM_HEAD_DIM), jnp.float32),
        compiler_params=_params(("arbitrary",)),
        name="mem_attn_sample",
    )(q, mk, mv)
    return o.reshape(nb, MEM_HEADS, DEC_SEQ, MEM_HEAD_DIM).transpose(0, 2, 1, 3).reshape(nb * DEC_SEQ, D_MODEL)


def _post_b_sample_kernel(o_ref, h1_ref, wo_ref, g2_ref, b2_ref, wrhl_ref, br_ref, tri_ref,
                          h2t_ref, rt_ref, cnt_ref):
    h2 = _layer_norm(ALPHA * h1_ref[...] + _dot(_bf16(o_ref[...]), wo_ref[...]), g2_ref[...], b2_ref[...])
    _store_token_tiles(h2t_ref, h2)
    route, carry = _route(h2, wrhl_ref, br_ref, tri_ref, jnp.zeros((1, LANES), jnp.float32))
    rt_ref[...] = route
    cnt_ref[...] = jnp.broadcast_to(carry, cnt_ref.shape)


def _post_b_sample(o, h1, wo_b, g2, b2, wrhl, br, tri):
    n = h1.shape[0]
    args = (o, h1, wo_b, g2, b2, wrhl, br, tri)
    full = lambda a: pl.BlockSpec(a.shape, lambda i: (0,) * a.ndim)
    return pl.pallas_call(
        _post_b_sample_kernel,
        grid=(1,),
        in_specs=[full(a) for a in args],
        out_specs=[pl.BlockSpec((n * ROW_CHUNKS, LANES), lambda i: (0, 0)),
                   pl.BlockSpec((n, LANES), lambda i: (0, 0)),
                   pl.BlockSpec((SUBLANES, LANES), lambda i: (0, 0))],
        out_shape=[jax.ShapeDtypeStruct((n * ROW_CHUNKS, LANES), jnp.float32),
                   jax.ShapeDtypeStruct((n, LANES), jnp.float32),
                   jax.ShapeDtypeStruct((SUBLANES, LANES), jnp.float32)],
        compiler_params=_params(("arbitrary",)),
        name="post_b_sample",
    )(*args)


def _row_gather_copy(src_hbm, idx, dst, dst_row, sem):
    s0 = pl.multiple_of(idx * ROW_CHUNKS, ROW_CHUNKS)
    d0 = dst_row * ROW_CHUNKS
    if not isinstance(dst_row, int):
        d0 = pl.multiple_of(d0, ROW_CHUNKS)
    return pltpu.make_async_copy(src_hbm.at[pl.ds(s0, ROW_CHUNKS), :], dst.at[pl.ds(d0, ROW_CHUNKS), :], sem)


def _dispatch_kernel(pos_ref, h2t_ref, xs_hbm, sem):
    def body(r, c):
        src = h2t_ref.at[pl.ds(pl.multiple_of(r * ROW_CHUNKS, ROW_CHUNKS), ROW_CHUNKS), :]
        for k in range(2):
            d0 = pl.multiple_of(pos_ref[0, 0, k * TM_COMB + r] * ROW_CHUNKS, ROW_CHUNKS)
            pltpu.make_async_copy(src, xs_hbm.at[pl.ds(d0, ROW_CHUNKS), :], sem.at[0]).start(priority=k)
        return c
    lax.fori_loop(0, TM_COMB, body, 0, unroll=8)
    for _ in range(2):
        pltpu.make_async_copy(h2t_ref, xs_hbm.at[pl.ds(0, TM_COMB * ROW_CHUNKS), :], sem.at[0]).wait()


def _dispatch(pos3, h2t):
    nt = N_ALL // TM_COMB
    return pl.pallas_call(
        _dispatch_kernel,
        grid=(nt,),
        in_specs=[pl.BlockSpec((1, 1, 2 * TM_COMB), lambda i: (i, 0, 0), memory_space=pltpu.SMEM),
                  pl.BlockSpec((TM_COMB * ROW_CHUNKS, LANES), lambda i: (i, 0))],
        out_specs=pl.BlockSpec(memory_space=pl.ANY),
        out_shape=jax.ShapeDtypeStruct((N_ASSIGN * ROW_CHUNKS, LANES), jnp.float32),
        scratch_shapes=[pltpu.SemaphoreType.DMA((1,))],
        compiler_params=_params(("arbitrary",)),
        name="moe_dispatch",
    )(pos3, h2t)


def _moe_ffn_kernel(it_ref, ie_ref, lo_ref, hi_ref, x_ref, wg_ref, wu_ref, wd_ref, y_ref, wgb, wub, wdb, cur_e):
    i = pl.program_id(0)
    lo = lo_ref[i]
    hi = hi_ref[i]
    e = ie_ref[i]

    @pl.when(i == 0)
    def _():
        cur_e[0] = -1

    @pl.when((hi > lo) & (cur_e[0] != e))
    def _():
        wgb[...] = _bf16(wg_ref[0])
        wub[...] = _bf16(wu_ref[0])
        wdb[...] = _bf16(wd_ref[0])
        cur_e[0] = e

    def ffn():
        x = _bf16(_load_token_tiles(x_ref, 0, TM_MOE))
        hg = _dot(x, wgb[...])
        hu = _dot(x, wub[...])
        h = hg / (1.0 + jnp.exp(-hg)) * hu
        return _dot(_bf16(h), wdb[...])

    def rows_mask():
        row = lax.broadcasted_iota(jnp.int32, (TM_MOE, LANES), 0)
        return (row >= lo) & (row < hi)

    @pl.when((hi > lo) & (lo == 0))
    def _():
        y = ffn()
        mask = rows_mask()
        for c in range(ROW_CHUNKS):
            y_ref[pl.ds(c, TM_MOE, stride=ROW_CHUNKS), :] = jnp.where(mask, y[:, c * LANES:(c + 1) * LANES], 0.0)

    @pl.when((hi > lo) & (lo > 0))
    def _():
        y = ffn()
        mask = rows_mask()
        for c in range(ROW_CHUNKS):
            sl = pl.ds(c, TM_MOE, stride=ROW_CHUNKS)
            y_ref[sl, :] = jnp.where(mask, y[:, c * LANES:(c + 1) * LANES], y_ref[sl, :])


def _moe_ffn(item_tile, item_expert, item_lo, item_hi, x_sorted, w_gate, w_up, w_down):
    wspec = lambda shp: pl.BlockSpec((1,) + shp, lambda i, it, ie, lo, hi: (ie[i], 0, 0))
    tile = pl.BlockSpec((TM_MOE * ROW_CHUNKS, LANES), lambda i, it, ie, lo, hi: (it[i], 0))
    grid_spec = pltpu.PrefetchScalarGridSpec(
        num_scalar_prefetch=4,
        grid=(MOE_ITEMS,),
        in_specs=[tile, wspec((D_MODEL, EXPERT_FF)), wspec((D_MODEL, EXPERT_FF)), wspec((EXPERT_FF, D_MODEL))],
        out_specs=tile,
        scratch_shapes=[pltpu.VMEM((D_MODEL, EXPERT_FF), jnp.bfloat16),
                        pltpu.VMEM((D_MODEL, EXPERT_FF), jnp.bfloat16),
                        pltpu.VMEM((EXPERT_FF, D_MODEL), jnp.bfloat16),
                        pltpu.SMEM((1,), jnp.int32)],
    )
    return pl.pallas_call(
        _moe_ffn_kernel,
        grid_spec=grid_spec,
        out_shape=jax.ShapeDtypeStruct((N_ASSIGN * ROW_CHUNKS, LANES), jnp.float32),
        compiler_params=_params(("arbitrary",)),
        name="moe_ffn",
    )(item_tile, item_expert, item_lo, item_hi, x_sorted, w_gate, w_up, w_down)


def _combine_kernel(nt, pos_cur_ref, pos_nxt_ref, yt_hbm, h2t_ref, rt_ref, g3_ref, b3_ref, o_ref, abuf, sem):
    t = pl.program_id(0)
    slot = t % 2
    rows = 2 * TM_COMB

    def issue(pos_ref, s):
        def body(j, c):
            for k in range(2):
                r = 2 * j + k
                _row_gather_copy(yt_hbm, pos_ref[0, 0, r], abuf, s * rows + r, sem.at[s]).start(priority=k)
            return c
        lax.fori_loop(0, rows // 2, body, 0, unroll=4)

    @pl.when(t == 0)
    def _():
        issue(pos_cur_ref, 0)

    @pl.when(t + 1 < nt)
    def _():
        issue(pos_nxt_ref, 1 - slot)

    base = pl.multiple_of(slot * (rows * ROW_CHUNKS), rows * ROW_CHUNKS)
    pltpu.make_async_copy(yt_hbm.at[pl.ds(0, rows * ROW_CHUNKS), :],
                          abuf.at[pl.ds(base, rows * ROW_CHUNKS), :], sem.at[slot]).wait()
    ya = _load_token_tiles(abuf, base, TM_COMB)
    yb = _load_token_tiles(abuf, base + TM_COMB * ROW_CHUNKS, TM_COMB)
    rt = rt_ref[...]
    ff = rt[:, 2:3] * ya + rt[:, 3:4] * yb
    h2 = _load_token_tiles(h2t_ref, 0, TM_COMB)
    o_ref[...] = _layer_norm(ALPHA * h2 + ff, g3_ref[...], b3_ref[...])


def _combine(pos3, yt, h2t, rt, g3, b3, tile0, n_tiles):
    last = tile0 + n_tiles - 1
    smem_pos = lambda f: pl.BlockSpec((1, 1, 2 * TM_COMB), f, memory_space=pltpu.SMEM)
    full = lambda a: pl.BlockSpec(a.shape, lambda i: (0,) * a.ndim)
    return pl.pallas_call(
        functools.partial(_combine_kernel, n_tiles),
        grid=(n_tiles,),
        in_specs=[smem_pos(lambda i: (tile0 + i, 0, 0)),
                  smem_pos(lambda i: (jnp.minimum(tile0 + i + 1, last), 0, 0)),
                  pl.BlockSpec(memory_space=pl.ANY),
                  pl.BlockSpec((TM_COMB * ROW_CHUNKS, LANES), lambda i: (tile0 + i, 0)),
                  pl.BlockSpec((TM_COMB, LANES), lambda i: (tile0 + i, 0)),
                  full(g3), full(b3)],
        out_specs=pl.BlockSpec((TM_COMB, D_MODEL), lambda i: (i, 0)),
        out_shape=jax.ShapeDtypeStruct((n_tiles * TM_COMB, D_MODEL), jnp.float32),
        scratch_shapes=[pltpu.VMEM((2 * 2 * TM_COMB * ROW_CHUNKS, LANES), jnp.float32),
                        pltpu.SemaphoreType.DMA((2,))],
        compiler_params=_params(("arbitrary",)),
        name="moe_combine",
    )(pos3, pos3, yt, h2t, rt, g3, b3)


def _combine_pair_kernel(n_pairs, pos_a_ref, pos_b_ref, pos_c_ref, yt_hbm, h2t_ref, rt_ref, g3_ref, b3_ref,
                         o_ref, abuf, sem, vbuf):
    i = pl.program_id(0)
    rows = 2 * TM_COMB
    tile_rows = rows * ROW_CHUNKS
    per_chunk = rows // ROW_CHUNKS

    def gather(pos_ref, slot, r):
        return _row_gather_copy(yt_hbm, pos_ref[0, 0, r], abuf, slot * rows + r, sem.at[slot])

    def wait_tile(slot):
        pltpu.make_async_copy(yt_hbm.at[pl.ds(0, tile_rows), :],
                              abuf.at[pl.ds(slot * tile_rows, tile_rows), :], sem.at[slot]).wait()

    @pl.when(i == 0)
    def _():
        def body(r, c):
            gather(pos_a_ref, 0, r).start()
            return c
        lax.fori_loop(0, rows, body, 0, unroll=8)

    def tile(t, slot, next_pos_ref):
        wait_tile(slot)
        base = slot * tile_rows
        tok = slice(t * TM_COMB, (t + 1) * TM_COMB)
        rt = rt_ref[tok, :]
        w1, w2 = rt[:, 2:3], rt[:, 3:4]
        total = jnp.zeros((TM_COMB, 1), jnp.float32)
        for c in range(ROW_CHUNKS):
            ya = abuf[pl.ds(base + c, TM_COMB, stride=ROW_CHUNKS), :]
            yb = abuf[pl.ds(base + TM_COMB * ROW_CHUNKS + c, TM_COMB, stride=ROW_CHUNKS), :]
            h2 = h2t_ref[pl.ds(t * TM_COMB * ROW_CHUNKS + c, TM_COMB, stride=ROW_CHUNKS), :]
            v = ALPHA * h2 + (w1 * ya + w2 * yb)
            vbuf[:, c * LANES:(c + 1) * LANES] = v
            total = total + jnp.sum(v, axis=1, keepdims=True)
            for r in range(c * per_chunk, (c + 1) * per_chunk):
                gather(next_pos_ref, 1 - slot, r).start(priority=r % 2)
        mu = total * (1.0 / D_MODEL)
        ssq = jnp.zeros((TM_COMB, 1), jnp.float32)
        for c in range(ROW_CHUNKS):
            d = vbuf[:, c * LANES:(c + 1) * LANES] - mu
            ssq = ssq + jnp.sum(d * d, axis=1, keepdims=True)
        inv = lax.rsqrt(ssq * (1.0 / D_MODEL) + LN_EPS)
        for c in range(ROW_CHUNKS):
            cols = slice(c * LANES, (c + 1) * LANES)
            o_ref[tok, cols] = (vbuf[:, cols] - mu) * inv * g3_ref[:, cols] + b3_ref[:, cols]

    tile(0, 0, pos_b_ref)
    tile(1, 1, pos_c_ref)

    @pl.when(i == n_pairs - 1)
    def _():
        wait_tile(0)


def _combine_pairs(pos3, yt, h2t, rt, g3, b3, n_tiles):
    assert n_tiles % 2 == 0
    n_pairs = n_tiles // 2
    last = n_tiles - 1
    smem_pos = lambda f: pl.BlockSpec((1, 1, 2 * TM_COMB), f, memory_space=pltpu.SMEM)
    full = lambda a: pl.BlockSpec(a.shape, lambda i: (0,) * a.ndim)
    return pl.pallas_call(
        functools.partial(_combine_pair_kernel, n_pairs),
        grid=(n_pairs,),
        in_specs=[smem_pos(lambda i: (2 * i, 0, 0)),
                  smem_pos(lambda i: (2 * i + 1, 0, 0)),
                  smem_pos(lambda i: (jnp.minimum(2 * i + 2, last), 0, 0)),
                  pl.BlockSpec(memory_space=pl.ANY),
                  pl.BlockSpec((2 * TM_COMB * ROW_CHUNKS, LANES), lambda i: (i, 0)),
                  pl.BlockSpec((2 * TM_COMB, LANES), lambda i: (i, 0)),
                  full(g3), full(b3)],
        out_specs=pl.BlockSpec((2 * TM_COMB, D_MODEL), lambda i: (i, 0)),
        out_shape=jax.ShapeDtypeStruct((n_tiles * TM_COMB, D_MODEL), jnp.float32),
        scratch_shapes=[pltpu.VMEM((2 * 2 * TM_COMB * ROW_CHUNKS, LANES), jnp.float32),
                        pltpu.SemaphoreType.DMA((2,)),
                        pltpu.VMEM((TM_COMB, D_MODEL), jnp.float32)],
        compiler_params=_params(("arbitrary",)),
        name="moe_combine_pairs",
    )(pos3, pos3, pos3, yt, h2t, rt, g3, b3)


POS_TILES = 3


def _positions_kernel(rt_ref, starts_ref, pos_ref):
    lane = lax.broadcasted_iota(jnp.int32, (TM_COMB, LANES), 1)
    lane_f = lane.astype(jnp.float32)
    starts = starts_ref[0:1, :]
    for j in range(POS_TILES):
        rt = rt_ref[j * TM_COMB:(j + 1) * TM_COMB, :]
        cols = []
        for k in range(2):
            seg = jnp.sum(jnp.where(lane_f == rt[:, k:k + 1] + N_GROUPS, starts, 0.0), axis=1, keepdims=True)
            cols.append(seg + rt[:, 4 + k:5 + k])
        packed = jnp.where(lane == 0, cols[0], jnp.where(lane == 1, cols[1], 0.0))
        rows = packed.T
        pos_ref[j] = jnp.concatenate([rows[0:1, :], rows[1:2, :]], axis=1).astype(jnp.int32)


def _positions(rt, starts_row):
    nt = N_ALL // TM_COMB
    assert nt % POS_TILES == 0
    return pl.pallas_call(
        _positions_kernel,
        grid=(nt // POS_TILES,),
        in_specs=[pl.BlockSpec((POS_TILES * TM_COMB, LANES), lambda i: (i, 0)),
                  pl.BlockSpec((SUBLANES, LANES), lambda i: (0, 0))],
        out_specs=pl.BlockSpec((POS_TILES, 1, 2 * TM_COMB), lambda i: (i, 0, 0)),
        out_shape=jax.ShapeDtypeStruct((nt, 1, 2 * TM_COMB), jnp.int32),
        compiler_params=_params(("arbitrary",)),
        name="moe_positions",
    )(rt, starts_row)


def _routing_plan(rt, cnt):
    i32 = jnp.int32
    counts_f = cnt[0, N_GROUPS:N_GROUPS + N_EXPERTS]
    starts_f = jnp.cumsum(counts_f) - counts_f
    starts_row = jnp.broadcast_to(
        jnp.pad(starts_f, (N_GROUPS, LANES - N_GROUPS - N_EXPERTS))[None, :], (SUBLANES, LANES))
    pos3 = _positions(rt, starts_row)
    starts = starts_f.astype(i32)
    tiles = jnp.arange(MOE_TILES, dtype=i32) * TM_MOE
    rank_t = jnp.arange(MOE_TILES, dtype=i32) + jnp.sum((starts[None, :] < tiles[:, None]).astype(i32), axis=1)
    rank_s = jnp.arange(N_EXPERTS, dtype=i32) + jnp.sum((tiles[None, :] <= starts[:, None]).astype(i32), axis=1)
    vals = jnp.concatenate([tiles, starts])
    ranks = jnp.concatenate([rank_t, rank_s])
    slot = jnp.arange(MOE_ITEMS, dtype=i32)
    lo = jnp.sum(jnp.where(ranks[None, :] == slot[:, None], vals[None, :], 0), axis=1)
    hi = jnp.concatenate([lo[1:], jnp.full((1,), N_ASSIGN, i32)])
    item_tile = jnp.minimum(lo // TM_MOE, MOE_TILES - 1)
    item_expert = jnp.clip(jnp.sum((starts[None, :] <= lo[:, None]).astype(i32), axis=1) - 1, 0, N_EXPERTS - 1)
    base = item_tile * TM_MOE
    return item_tile, item_expert, lo - base, hi - base, pos3


def kernel(x_prompt, x_sample, mem_prompt, cache_swa_k, cache_swa_v, cache_conv, cache_mem_k, cache_mem_v,
           w_in, sinks, conv_w, w_mix_out, ln1_g, ln1_b, w_q_mem, w_k_mem, w_v_mem, w_o_mem, ln2_g, ln2_b,
           w_router_group, b_router_group, w_router_expert, b_router_expert, w_gate, w_up, w_down,
           ln3_g, ln3_b):
    f32 = jnp.float32
    row = lambda a: a.reshape(1, -1).astype(f32)
    w_in_b, wmix_b, wq_b, wk_b, wv_b, wo_b = (_bf16(w) for w in (w_in, w_mix_out, w_q_mem, w_k_mem, w_v_mem, w_o_mem))
    g1, b1, g2, b2, g3, b3 = (row(a) for a in (ln1_g, ln1_b, ln2_g, ln2_b, ln3_g, ln3_b))
    pad = LANES - N_GROUPS - N_EXPERTS
    wr = jnp.concatenate([w_router_group, w_router_expert, jnp.zeros((D_MODEL, pad), f32)], axis=1)
    wrh = _bf16(wr)
    wrhl = jnp.concatenate([wrh, _bf16(wr - wrh.astype(f32))], axis=1)
    br = jnp.concatenate([b_router_group, b_router_expert, jnp.zeros((pad,), f32)]).reshape(1, LANES)

    xs = x_sample.reshape(N_SAMPLE, D_MODEL)
    tab_s = jnp.tile(_rope_table(PAST_LEN + jnp.arange(DEC_SEQ)), (1, DEC_BATCH))
    c0 = jnp.repeat(cache_conv[:, 0], DEC_SEQ, axis=0)
    c1 = jnp.repeat(cache_conv[:, 1], DEC_SEQ, axis=0)
    q_s, k_s, v_s, conv_s, u_s = _proj_sample(xs, w_in_b, tab_s, conv_w, c0, c1)
    attn_s, swa_k_s, swa_v_s = _swa_sample(sinks, q_s, k_s, v_s, cache_swa_k, cache_swa_v)
    h1_s, qm_s = _post_a_sample(attn_s, conv_s, xs, wmix_b, g1, b1, wq_b)
    o_s = _mem_attn_sample(qm_s, cache_mem_k, cache_mem_v)
    tri = _bf16(jnp.tril(jnp.ones((TM_POST, TM_POST), f32), -1))
    h2t_s, rt_s, cnt_s = _post_b_sample(o_s, h1_s, wo_b, g2, b2, wrhl, br, tri)

    xp = x_prompt.reshape(N_PROMPT, D_MODEL)
    tab_p = _rope_table(jnp.arange(N_PROMPT))
    q_p, kx_p, vx_p, conv_p, k_tail, v_tail, u_tail = _proj_prompt(xp, w_in_b, tab_p, conv_w)
    attn_p = _swa_prompt(sinks, q_p, kx_p, vx_p)
    mk, mv, mk_b, mv_b = _mem_kv(mem_prompt.reshape(N_MEM, D_MODEL), wk_b, wv_b)
    h2t, rt, cnt = _post_prompt(attn_p, conv_p, xp, wmix_b, g1, b1, wq_b, mk_b, mv_b, wo_b, g2, b2,
                                wrhl, br, tri, h2t_s, rt_s, cnt_s)

    item_tile, item_expert, item_lo, item_hi, pos3 = _routing_plan(rt, cnt)
    x_sorted = _dispatch(pos3, h2t)
    yt = _moe_ffn(item_tile, item_expert, item_lo, item_hi, x_sorted, w_gate, w_up, w_down)
    y_p = _combine_pairs(pos3, yt, h2t, rt, g3, b3, N_PROMPT // TM_COMB)
    y_s = _combine(pos3, yt, h2t, rt, g3, b3, N_PROMPT // TM_COMB, N_SAMPLE // TM_COMB)

    return (y_p.reshape(1, SEQ, D_MODEL),
            y_s.reshape(DEC_BATCH, DEC_SEQ, D_MODEL),
            k_tail.reshape(1, WINDOW, N_KV_HEADS, HEAD_DIM),
            v_tail.reshape(1, WINDOW, N_KV_HEADS, HEAD_DIM),
            u_tail[SUBLANES - (CONV_K - 1):].reshape(1, CONV_K - 1, CONV_CH),
            mk.reshape(1, N_MEM, MEM_HEADS, MEM_HEAD_DIM),
            mv.reshape(1, N_MEM, MEM_HEADS, MEM_HEAD_DIM),
            swa_k_s.reshape(DEC_BATCH, WINDOW, N_KV_HEADS, HEAD_DIM),
            swa_v_s.reshape(DEC_BATCH, WINDOW, N_KV_HEADS, HEAD_DIM),
            u_s.reshape(DEC_BATCH, DEC_SEQ, CONV_CH)[:, DEC_SEQ - (CONV_K - 1):])
```

```python
import functools

import jax
import jax.numpy as jnp
from jax import lax
from jax.experimental import pallas as pl
from jax.experimental.pallas import tpu as pltpu

D_MODEL = 1024
SEQ = 16384
DEC_BATCH = 128
DEC_SEQ = 4
PAST_LEN = 16384
ATTN_WIDTH = 512
CONV_CH = 512
HEAD_DIM = 64
N_HEADS = 8
N_KV_HEADS = 2
KV_WIDTH = 128
WINDOW = 128
ROPE_THETA = 500000.0
ROPE_DIM = 16
CONV_K = 3
Q_END = ATTN_WIDTH
K_END = Q_END + KV_WIDTH
V_END = K_END + KV_WIDTH
B_END = V_END + CONV_CH
C_END = B_END + CONV_CH
IN_WIDTH = C_END + CONV_CH
N_MEM = 256
MEM_HEADS = 4
MEM_HEAD_DIM = 256
N_GROUPS = 4
EXPERTS_PER_GROUP = 8
N_EXPERTS = 32
EXPERT_FF = 256
ALPHA = 2.0 ** 0.25
LN_EPS = 1e-5

LANES = 128
SUBLANES = 8
ROW_CHUNKS = D_MODEL // LANES
VMEM_LIMIT = 56 * 1024 * 1024

N_PROMPT = SEQ
N_SAMPLE = DEC_BATCH * DEC_SEQ
N_ALL = N_PROMPT + N_SAMPLE
TM_PROJ = 1024
TM_POST = 512
TM_MOE = 256
TM_COMB = 512
N_ASSIGN = 2 * N_ALL
MOE_TILES = N_ASSIGN // TM_MOE
MOE_ITEMS = MOE_TILES + N_EXPERTS
SAMPLE_BB = 4
SWA_BB = 8

assert ROW_CHUNKS == SUBLANES
assert N_SAMPLE == TM_POST
assert N_ASSIGN % TM_MOE == 0 and N_ALL % TM_COMB == 0


def _params(sem, vmem=VMEM_LIMIT):
    return pltpu.CompilerParams(dimension_semantics=sem, vmem_limit_bytes=vmem)


def _bf16(x):
    return x.astype(jnp.bfloat16)


def _dot(a, b):
    return jnp.dot(a, b, preferred_element_type=jnp.float32)


def _dot_nt(a, b):
    return lax.dot_general(a, b, (((1,), (1,)), ((), ())), preferred_element_type=jnp.float32)


def _layer_norm(x, g, b):
    mu = jnp.mean(x, axis=-1, keepdims=True)
    xc = x - mu
    var = jnp.mean(xc * xc, axis=-1, keepdims=True)
    return xc * lax.rsqrt(var + LN_EPS) * g + b


def _rope(x, cos_t, sin_t):
    lane = lax.broadcasted_iota(jnp.int32, x.shape, 1) % HEAD_DIM
    half = ROPE_DIM // 2
    partner = jnp.where(lane < half, pltpu.roll(x, LANES - half, axis=1), pltpu.roll(x, half, axis=1))
    return x * cos_t + partner * sin_t


def _head_slabs(x):
    lane = lax.broadcasted_iota(jnp.int32, x.shape, 1)
    lo = lane < HEAD_DIM
    sw = pltpu.roll(x, HEAD_DIM, axis=1)
    zero = jnp.zeros_like(x)
    slabs = [jnp.where(lo, x, zero), jnp.where(lo, zero, sw), jnp.where(lo, sw, zero), jnp.where(lo, zero, x)]
    return _bf16(jnp.concatenate(slabs, axis=1))


def _store_token_tiles(ref, val):
    rows = val.shape[0]
    for c in range(ROW_CHUNKS):
        ref[pl.ds(c, rows, stride=ROW_CHUNKS), :] = val[:, c * LANES:(c + 1) * LANES]


def _load_token_tiles(ref, base, rows):
    return jnp.concatenate(
        [ref[pl.ds(base + c, rows, stride=ROW_CHUNKS), :] for c in range(ROW_CHUNKS)], axis=1)


ROPE_ONE = 3 * (ROPE_DIM // 2)
ROPE_ROWS = 32


def _rope_patterns(tab):
    half = ROPE_DIM // 2
    m = lax.broadcasted_iota(jnp.int32, tab.shape, 1) % HEAD_DIM
    idx_c = jnp.where(m < ROPE_DIM, m % half, ROPE_ONE)
    idx_s = jnp.where(m < half, 2 * half + m, jnp.where(m < ROPE_DIM, m, ROPE_ONE + 1))
    return jnp.take_along_axis(tab, idx_c, axis=1), jnp.take_along_axis(tab, idx_s, axis=1)


def _proj_common(x_ref, w_ref, tab_ref):
    xb = _bf16(x_ref[...])
    tab = tab_ref[...]
    pad = jnp.zeros((LANES - tab.shape[0], tab.shape[1]), jnp.float32)
    cos_t, sin_t = _rope_patterns(jnp.concatenate([tab, pad], axis=0).T)
    q = _dot(xb, w_ref[:, 0:Q_END])
    q_rot = jnp.concatenate(
        [_rope(q[:, p * LANES:(p + 1) * LANES], cos_t, sin_t) for p in range(ATTN_WIDTH // LANES)], axis=1)
    q_out = _bf16(q_rot * (HEAD_DIM ** -0.5))
    kv = _dot(xb, w_ref[:, Q_END:V_END])
    k = _rope(kv[:, 0:KV_WIDTH], cos_t, sin_t)
    v = kv[:, KV_WIDTH:]
    bg = _dot(xb, w_ref[:, V_END:B_END])
    u = _dot(xb, w_ref[:, B_END:C_END]) * _dot(xb, w_ref[:, C_END:IN_WIDTH])
    return q_out, k, v, bg, u


def _conv3(bg, u, u1, u2, cw_ref):
    cw = cw_ref[...]
    return bg * (cw[0:1, :] * u2 + cw[1:2, :] * u1 + cw[2:3, :] * u)


def _proj_prompt_kernel(x_ref, w_ref, tab_ref, cw_ref,
                        q_ref, kx_ref, vx_ref, conv_ref, ktail_ref, vtail_ref, utail_ref, carry_ref):
    @pl.when(pl.program_id(0) == 0)
    def _():
        carry_ref[...] = jnp.zeros_like(carry_ref)

    q_out, k, v, bg, u = _proj_common(x_ref, w_ref, tab_ref)
    tm = u.shape[0]
    ext = jnp.concatenate([carry_ref[...], u], axis=0)
    u1 = pltpu.roll(ext, 1, axis=0)[SUBLANES:SUBLANES + tm]
    u2 = pltpu.roll(ext, 2, axis=0)[SUBLANES:SUBLANES + tm]
    q_ref[...] = q_out
    kx_ref[...] = _head_slabs(k)
    vx_ref[...] = _head_slabs(v)
    conv_ref[...] = _bf16(_conv3(bg, u, u1, u2, cw_ref))
    ktail_ref[...] = k[tm - WINDOW:tm]
    vtail_ref[...] = v[tm - WINDOW:tm]
    utail_ref[...] = u[tm - SUBLANES:tm]
    carry_ref[...] = u[tm - SUBLANES:tm]


def _proj_sample_kernel(x_ref, w_ref, tab_ref, cw_ref, c0_ref, c1_ref,
                        q_ref, k_ref, v_ref, conv_ref, u_ref):
    q_out, k, v, bg, u = _proj_common(x_ref, w_ref, tab_ref)
    t = lax.broadcasted_iota(jnp.int32, u.shape, 0) % DEC_SEQ
    c0 = c0_ref[...]
    c1 = c1_ref[...]
    u1 = jnp.where(t >= 1, pltpu.roll(u, 1, axis=0), c1)
    u2 = jnp.where(t >= 2, pltpu.roll(u, 2, axis=0), jnp.where(t == 1, c1, c0))
    q_ref[...] = q_out.astype(jnp.float32)
    k_ref[...] = k
    v_ref[...] = v
    conv_ref[...] = _bf16(_conv3(bg, u, u1, u2, cw_ref))
    u_ref[...] = u


def _rope_table(pos):
    half = ROPE_DIM // 2
    inv = ROPE_THETA ** (-jnp.arange(0, ROPE_DIM, 2, dtype=jnp.float32) / ROPE_DIM)
    ang = pos.astype(jnp.float32)[None, :] * inv[:, None]
    cos, sin = jnp.cos(ang), jnp.sin(ang)
    n = pos.shape[0]
    assert ROPE_ONE == 3 * half
    return jnp.concatenate([cos, sin, -sin, jnp.ones((1, n), jnp.float32),
                            jnp.zeros((ROPE_ROWS - ROPE_ONE - 1, n), jnp.float32)], axis=0)


def _proj_prompt(x, w_in_b, tab, conv_w):
    n = x.shape[0]
    tm = TM_PROJ
    row = lambda w: pl.BlockSpec((tm, w), lambda i: (i, 0))
    full = lambda a: pl.BlockSpec(a.shape, lambda i: (0,) * a.ndim)
    const = lambda r, w: pl.BlockSpec((r, w), lambda i: (0, 0))
    return pl.pallas_call(
        _proj_prompt_kernel,
        grid=(n // tm,),
        in_specs=[row(D_MODEL), full(w_in_b), pl.BlockSpec((ROPE_ROWS, tm), lambda i: (0, i)), full(conv_w)],
        out_specs=[row(ATTN_WIDTH), row(4 * LANES), row(4 * LANES), row(CONV_CH),
                   const(WINDOW, KV_WIDTH), const(WINDOW, KV_WIDTH), const(SUBLANES, CONV_CH)],
        out_shape=[jax.ShapeDtypeStruct((n, ATTN_WIDTH), jnp.bfloat16),
                   jax.ShapeDtypeStruct((n, 4 * LANES), jnp.bfloat16),
                   jax.ShapeDtypeStruct((n, 4 * LANES), jnp.bfloat16),
                   jax.ShapeDtypeStruct((n, CONV_CH), jnp.bfloat16),
                   jax.ShapeDtypeStruct((WINDOW, KV_WIDTH), jnp.float32),
                   jax.ShapeDtypeStruct((WINDOW, KV_WIDTH), jnp.float32),
                   jax.ShapeDtypeStruct((SUBLANES, CONV_CH), jnp.float32)],
        scratch_shapes=[pltpu.VMEM((SUBLANES, CONV_CH), jnp.float32)],
        compiler_params=_params(("arbitrary",)),
        name="proj_prompt",
    )(x, w_in_b, tab, conv_w)


def _proj_sample(x, w_in_b, tab, conv_w, c0, c1):
    n = x.shape[0]
    full = lambda a: pl.BlockSpec(a.shape, lambda i: (0,) * a.ndim)
    out = lambda w, dt: jax.ShapeDtypeStruct((n, w), dt)
    blk = lambda w: pl.BlockSpec((n, w), lambda i: (0, 0))
    return pl.pallas_call(
        _proj_sample_kernel,
        grid=(1,),
        in_specs=[full(x), full(w_in_b), full(tab), full(conv_w), full(c0), full(c1)],
        out_specs=[blk(ATTN_WIDTH), blk(KV_WIDTH), blk(KV_WIDTH), blk(CONV_CH), blk(CONV_CH)],
        out_shape=[out(ATTN_WIDTH, jnp.float32), out(KV_WIDTH, jnp.float32), out(KV_WIDTH, jnp.float32),
                   out(CONV_CH, jnp.bfloat16), out(CONV_CH, jnp.float32)],
        compiler_params=_params(("arbitrary",)),
        name="proj_sample",
    )(x, w_in_b, tab, conv_w, c0, c1)


def _sink_softmax_pv(s, valid, sink, vx):
    s = jnp.where(valid, s, -jnp.inf)
    m = jnp.maximum(jnp.max(s, axis=1, keepdims=True), sink)
    p = jnp.exp(s - m)
    den = jnp.sum(p, axis=1, keepdims=True) + jnp.exp(sink - m)
    return _dot(_bf16(p), vx) / den


SWA_QB = 4


def _swa_prompt_kernel(sinks_ref, q_ref, kc_ref, kp_ref, vc_ref, vp_ref, o_ref):
    step = pl.program_id(0)
    kall = jnp.concatenate([kp_ref[...], kc_ref[...]], axis=0)
    vall = jnp.concatenate([vp_ref[...], vc_ref[...]], axis=0)
    i = lax.broadcasted_iota(jnp.int32, (WINDOW, 2 * WINDOW), 0)
    j = lax.broadcasted_iota(jnp.int32, (WINDOW, 2 * WINDOW), 1)
    band = (j > i) & (j <= i + WINDOW)
    for sb in range(SWA_QB):
        rows = slice(sb * WINDOW, (sb + 1) * WINDOW)
        kcat = kall[sb * WINDOW:(sb + 2) * WINDOW]
        vcat = vall[sb * WINDOW:(sb + 2) * WINDOW]
        valid = band & ((step > 0) | (j >= WINDOW)) if sb == 0 else band
        for p in range(N_HEADS // 2):
            qs = q_ref[rows, p * LANES:(p + 1) * LANES]
            acc = None
            for e in range(2):
                hd = 2 * p + e
                slab = 2 * (hd // (N_HEADS // N_KV_HEADS)) + e
                kx = kcat[:, slab * LANES:(slab + 1) * LANES]
                vx = vcat[:, slab * LANES:(slab + 1) * LANES]
                o = _sink_softmax_pv(_dot_nt(qs, kx), valid, sinks_ref[hd], vx)
                acc = o if acc is None else acc + o
            o_ref[rows, p * LANES:(p + 1) * LANES] = _bf16(acc)


def _swa_prompt(sinks, q, kx, vx):
    n = q.shape[0]
    nb = n // (SWA_QB * WINDOW)
    cur = lambda w: pl.BlockSpec((SWA_QB * WINDOW, w), lambda i: (i, 0))
    prev = lambda w: pl.BlockSpec((WINDOW, w), lambda i: (jnp.maximum(SWA_QB * i - 1, 0), 0))
    return pl.pallas_call(
        _swa_prompt_kernel,
        grid=(nb,),
        in_specs=[pl.BlockSpec(memory_space=pltpu.SMEM), cur(ATTN_WIDTH),
                  cur(4 * LANES), prev(4 * LANES), cur(4 * LANES), prev(4 * LANES)],
        out_specs=cur(ATTN_WIDTH),
        out_shape=jax.ShapeDtypeStruct((n, ATTN_WIDTH), jnp.bfloat16),
        compiler_params=_params(("arbitrary",)),
        name="swa_prompt",
    )(sinks, q, kx, kx, vx, vx)


SWA_ROWS = N_HEADS * DEC_SEQ
NEW_ROWS = 2 * SUBLANES


def _swa_sample_kernel(q_ref, sink_ref, kn_ref, vn_ref, kt_ref, vt_ref, o_ref, okt_ref, ovt_ref):
    t = lax.broadcasted_iota(jnp.int32, (SWA_ROWS, WINDOW), 0) % DEC_SEQ
    j = lax.broadcasted_iota(jnp.int32, (SWA_ROWS, WINDOW), 1)
    valid_c = j > t
    valid_n = (lax.broadcasted_iota(jnp.int32, (SWA_ROWS, NEW_ROWS), 1)
               <= lax.broadcasted_iota(jnp.int32, (SWA_ROWS, NEW_ROWS), 0) % DEC_SEQ)
    lane = lax.broadcasted_iota(jnp.int32, (KV_WIDTH, WINDOW), 1)
    sink = sink_ref[:, 0:1]
    shift = WINDOW - DEC_SEQ
    zrows = jnp.zeros((KV_WIDTH - NEW_ROWS, KV_WIDTH), jnp.float32)
    for b in range(SWA_BB):
        q = _bf16(q_ref[b])
        kt, vt = kt_ref[b], vt_ref[b]
        kn, vn = kn_ref[b], vn_ref[b]
        s_c = jnp.where(valid_c, _dot(q, _bf16(kt)), -jnp.inf)
        s_n = jnp.where(valid_n, _dot_nt(q, _bf16(kn)), -jnp.inf)
        m = jnp.maximum(jnp.maximum(jnp.max(s_c, axis=1, keepdims=True), jnp.max(s_n, axis=1, keepdims=True)), sink)
        p_c = jnp.exp(s_c - m)
        p_n = jnp.exp(s_n - m)
        den = jnp.sum(p_c, axis=1, keepdims=True) + jnp.sum(p_n, axis=1, keepdims=True) + jnp.exp(sink - m)
        o_ref[b] = (_dot_nt(_bf16(p_c), _bf16(vt)) + _dot(_bf16(p_n), _bf16(vn))) / den
        for old, new, dst in ((kt, kn, okt_ref), (vt, vn, ovt_ref)):
            new_cols = pltpu.roll(jnp.concatenate([new, zrows], axis=0).T, shift, axis=1)
            dst[b] = jnp.where(lane >= shift, new_cols, pltpu.roll(old, shift, axis=1))


def _swa_sample(sinks, q, kn, vn, cache_k, cache_v):
    nb = cache_k.shape[0]
    bb = SWA_BB
    groups = N_HEADS // N_KV_HEADS
    qh = q.reshape(nb, DEC_SEQ, N_KV_HEADS, groups, HEAD_DIM).transpose(0, 2, 3, 1, 4)
    qh = qh.reshape(nb, N_KV_HEADS, groups * DEC_SEQ, HEAD_DIM)
    zeros = jnp.zeros_like(qh[:, 0])
    qbd = jnp.concatenate([jnp.concatenate([qh[:, 0], zeros], axis=-1),
                           jnp.concatenate([zeros, qh[:, 1]], axis=-1)], axis=1)
    sink_col = jnp.broadcast_to(jnp.repeat(sinks, DEC_SEQ).reshape(SWA_ROWS, 1), (SWA_ROWS, LANES))
    pad8 = lambda a: jnp.pad(a.reshape(nb, DEC_SEQ, KV_WIDTH), ((0, 0), (0, NEW_ROWS - DEC_SEQ), (0, 0)))
    to_t = lambda c: c.transpose(0, 2, 3, 1).reshape(nb, KV_WIDTH, WINDOW)
    blk = lambda r, w: pl.BlockSpec((bb, r, w), lambda i: (i, 0, 0))
    o, okt, ovt = pl.pallas_call(
        _swa_sample_kernel,
        grid=(nb // bb,),
        in_specs=[blk(SWA_ROWS, KV_WIDTH), pl.BlockSpec((SWA_ROWS, LANES), lambda i: (0, 0)),
                  blk(NEW_ROWS, KV_WIDTH), blk(NEW_ROWS, KV_WIDTH), blk(KV_WIDTH, WINDOW), blk(KV_WIDTH, WINDOW)],
        out_specs=[blk(SWA_ROWS, KV_WIDTH), blk(KV_WIDTH, WINDOW), blk(KV_WIDTH, WINDOW)],
        out_shape=[jax.ShapeDtypeStruct((nb, SWA_ROWS, KV_WIDTH), jnp.float32),
                   jax.ShapeDtypeStruct((nb, KV_WIDTH, WINDOW), jnp.float32),
                   jax.ShapeDtypeStruct((nb, KV_WIDTH, WINDOW), jnp.float32)],
        compiler_params=_params(("arbitrary",)),
        name="swa_sample",
    )(qbd, sink_col, pad8(kn), pad8(vn), to_t(cache_k), to_t(cache_v))
    o = o.reshape(nb, N_KV_HEADS, groups, DEC_SEQ, N_KV_HEADS, HEAD_DIM)
    attn = jnp.stack([o[:, h, :, :, h, :] for h in range(N_KV_HEADS)], axis=1)
    attn = attn.transpose(0, 3, 1, 2, 4).reshape(nb * DEC_SEQ, ATTN_WIDTH)
    from_t = lambda c: c.reshape(nb, N_KV_HEADS, HEAD_DIM, WINDOW).transpose(0, 3, 1, 2)
    return attn, from_t(okt), from_t(ovt)


def _mem_kv_kernel(mem_ref, wk_ref, wv_ref, mk_ref, mv_ref, mkb_ref, mvb_ref):
    mb = _bf16(mem_ref[...])
    mk = _dot(mb, wk_ref[...])
    mv = _dot(mb, wv_ref[...])
    mk_ref[...] = mk
    mv_ref[...] = mv
    mkb_ref[...] = _bf16(mk)
    mvb_ref[...] = _bf16(mv)


def _mem_kv(mem, wk_b, wv_b):
    full = lambda a: pl.BlockSpec(a.shape, lambda i: (0,) * a.ndim)
    blk = pl.BlockSpec((N_MEM, D_MODEL), lambda i: (0, 0))
    f32 = jax.ShapeDtypeStruct((N_MEM, D_MODEL), jnp.float32)
    b16 = jax.ShapeDtypeStruct((N_MEM, D_MODEL), jnp.bfloat16)
    return pl.pallas_call(
        _mem_kv_kernel,
        grid=(1,),
        in_specs=[full(mem), full(wk_b), full(wv_b)],
        out_specs=[blk, blk, blk, blk],
        out_shape=[f32, f32, b16, b16],
        compiler_params=_params(("arbitrary",)),
        name="mem_kv",
    )(mem, wk_b, wv_b)


def _mix_ln1(attn_ref, conv_ref, x_ref, wmix_ref, g1_ref, b1_ref):
    mix = _dot(_bf16(attn_ref[...]), wmix_ref[0:ATTN_WIDTH, :]) + _dot(conv_ref[...], wmix_ref[ATTN_WIDTH:, :])
    return _layer_norm(ALPHA * x_ref[...] + mix, g1_ref[...], b1_ref[...])


def _mem_q(h1, wq_ref):
    return _bf16(_dot(_bf16(h1), wq_ref[...]) * (MEM_HEAD_DIM ** -0.5))


def _route(h2, wrhl_ref, br_ref, tri_ref, carry):
    hi = _bf16(h2)
    lo = _bf16(h2 - hi.astype(jnp.float32))
    hh = _dot(hi, wrhl_ref[...])
    logits = hh[:, 0:LANES] + hh[:, LANES:] + _dot(lo, wrhl_ref[:, 0:LANES]) + br_ref[...]
    lane_i = lax.broadcasted_iota(jnp.int32, logits.shape, 1)
    lane = lane_i.astype(jnp.float32)
    big = jnp.float32(LANES)
    is_g = lane_i < N_GROUPS
    gl = jnp.where(is_g, logits, -jnp.inf)
    gmax = jnp.max(gl, axis=1, keepdims=True)
    gidx = jnp.min(jnp.where(is_g & (logits == gmax), lane, big), axis=1, keepdims=True)
    gsum = jnp.sum(jnp.exp(gl - gmax), axis=1, keepdims=True)
    gw = 1.0 / gsum
    eid = lane_i - N_GROUPS
    assert EXPERTS_PER_GROUP == 8
    grp = lax.shift_right_arithmetic(eid, jnp.full_like(eid, 3)).astype(jnp.float32)
    in_e = (lane_i >= N_GROUPS) & (lane_i < N_GROUPS + N_EXPERTS) & (grp == gidx)
    v1 = jnp.max(jnp.where(in_e, logits, -jnp.inf), axis=1, keepdims=True)
    i1 = jnp.min(jnp.where(in_e & (logits == v1), lane, big), axis=1, keepdims=True)
    rest = in_e & (lane != i1)
    v2 = jnp.max(jnp.where(rest, logits, -jnp.inf), axis=1, keepdims=True)
    i2 = jnp.min(jnp.where(rest & (logits == v2), lane, big), axis=1, keepdims=True)
    ex = jnp.exp(v2 - v1)
    den = 1.0 + ex
    w1 = gw / den
    w2 = gw * ex / den
    zero = jnp.zeros_like(logits)
    pick1 = lane == i1
    pick2 = lane == i2
    sel = jnp.where(pick1 | pick2, 1.0, 0.0)
    before = _dot(tri_ref[...], _bf16(sel)) + carry
    rank1 = jnp.sum(jnp.where(pick1, before, zero), axis=1, keepdims=True)
    rank2 = jnp.sum(jnp.where(pick2, before, zero), axis=1, keepdims=True)
    cols = (i1 - N_GROUPS, i2 - N_GROUPS, w1, w2, rank1, rank2)
    route = zero
    for k, col in enumerate(cols):
        route = jnp.where(lane_i == k, col, route)
    return route, carry + jnp.sum(sel, axis=0, keepdims=True)


def _post_prompt_kernel(attn_ref, conv_ref, x_ref, wmix_ref, g1_ref, b1_ref, wq_ref, mk_ref, mv_ref, wo_ref,
                        g2_ref, b2_ref, wrhl_ref, br_ref, tri_ref, h2s_ref, rts_ref, cnts_ref,
                        h2t_ref, rt_ref, cnt_ref, carry_ref):
    is_tail = pl.program_id(0) == N_PROMPT // TM_POST

    @pl.when(pl.program_id(0) == 0)
    def _():
        carry_ref[...] = cnts_ref[...]

    @pl.when(is_tail)
    def _():
        h2t_ref[...] = h2s_ref[...]
        rt_ref[...] = rts_ref[...]

    @pl.when(jnp.logical_not(is_tail))
    def _():
        h1 = _mix_ln1(attn_ref, conv_ref, x_ref, wmix_ref, g1_ref, b1_ref)
        qm = _mem_q(h1, wq_ref)
        outs = []
        for h in range(MEM_HEADS):
            sl = slice(h * MEM_HEAD_DIM, (h + 1) * MEM_HEAD_DIM)
            s = _dot_nt(qm[:, sl], mk_ref[:, sl])
            m = jnp.max(s, axis=1, keepdims=True)
            p = jnp.exp(s - m)
            den = jnp.sum(p, axis=1, keepdims=True)
            outs.append(_dot(_bf16(p), mv_ref[:, sl]) / den)
        o = _bf16(jnp.concatenate(outs, axis=1))
        h2 = _layer_norm(ALPHA * h1 + _dot(o, wo_ref[...]), g2_ref[...], b2_ref[...])
        _store_token_tiles(h2t_ref, h2)
        route, carry = _route(h2, wrhl_ref, br_ref, tri_ref, carry_ref[0:1, :])
        rt_ref[...] = route
        carry_ref[...] = jnp.broadcast_to(carry, carry_ref.shape)
        cnt_ref[...] = jnp.broadcast_to(carry, cnt_ref.shape)


def _post_prompt(attn, conv, x, wmix_b, g1, b1, wq_b, mk_b, mv_b, wo_b, g2, b2, wrhl, br, tri, h2t_s, rt_s, cnt_s):
    n = x.shape[0]
    tm = TM_POST
    steps = n // tm
    row = lambda w: pl.BlockSpec((tm, w), lambda i: (jnp.minimum(i, steps - 1), 0))
    full = lambda a: pl.BlockSpec(a.shape, lambda i: (0,) * a.ndim)
    weights = (wmix_b, g1, b1, wq_b, mk_b, mv_b, wo_b, g2, b2, wrhl, br, tri, h2t_s, rt_s, cnt_s)
    n_out = n + h2t_s.shape[0] // ROW_CHUNKS
    return pl.pallas_call(
        _post_prompt_kernel,
        grid=(steps + 1,),
        in_specs=[row(ATTN_WIDTH), row(CONV_CH), row(D_MODEL)] + [full(a) for a in weights],
        out_specs=[pl.BlockSpec((tm * ROW_CHUNKS, LANES), lambda i: (i, 0)),
                   pl.BlockSpec((tm, LANES), lambda i: (i, 0)),
                   pl.BlockSpec((SUBLANES, LANES), lambda i: (0, 0))],
        out_shape=[jax.ShapeDtypeStruct((n_out * ROW_CHUNKS, LANES), jnp.float32),
                   jax.ShapeDtypeStruct((n_out, LANES), jnp.float32),
                   jax.ShapeDtypeStruct((SUBLANES, LANES), jnp.float32)],
        scratch_shapes=[pltpu.VMEM((SUBLANES, LANES), jnp.float32)],
        compiler_params=_params(("arbitrary",)),
        name="post_prompt",
    )(attn, conv, x, *weights)


def _post_a_sample_kernel(attn_ref, conv_ref, x_ref, wmix_ref, g1_ref, b1_ref, wq_ref, h1_ref, qm_ref):
    h1 = _mix_ln1(attn_ref, conv_ref, x_ref, wmix_ref, g1_ref, b1_ref)
    h1_ref[...] = h1
    qm_ref[...] = _mem_q(h1, wq_ref).astype(jnp.float32)


def _post_a_sample(attn, conv, x, wmix_b, g1, b1, wq_b):
    n = x.shape[0]
    args = (attn, conv, x, wmix_b, g1, b1, wq_b)
    full = lambda a: pl.BlockSpec(a.shape, lambda i: (0,) * a.ndim)
    blk = pl.BlockSpec((n, D_MODEL), lambda i: (0, 0))
    return pl.pallas_call(
        _post_a_sample_kernel,
        grid=(1,),
        in_specs=[full(a) for a in args],
        out_specs=[blk, blk],
        out_shape=[jax.ShapeDtypeStruct((n, D_MODEL), jnp.float32),
                   jax.ShapeDtypeStruct((n, D_MODEL), jnp.float32)],
        compiler_params=_params(("arbitrary",)),
        name="post_a_sample",
    )(*args)


MEM_ROWS = MEM_HEADS * DEC_SEQ


def _mem_attn_sample_kernel(q_ref, mk_ref, mv_ref, o_ref):
    nk = N_MEM * MEM_HEADS
    row_h = lax.broadcasted_iota(jnp.int32, (MEM_ROWS, nk), 0) // DEC_SEQ
    key_h = lax.broadcasted_iota(jnp.int32, (MEM_ROWS, nk), 1) % MEM_HEADS
    own = row_h == key_h
    for b in range(SAMPLE_BB):
        k2 = _bf16(mk_ref[b].reshape(nk, MEM_HEAD_DIM))
        v2 = _bf16(mv_ref[b].reshape(nk, MEM_HEAD_DIM))
        s = jnp.where(own, _dot_nt(_bf16(q_ref[b]), k2), -jnp.inf)
        m = jnp.max(s, axis=1, keepdims=True)
        p = jnp.exp(s - m)
        den = jnp.sum(p, axis=1, keepdims=True)
        o_ref[b] = _dot(_bf16(p), v2) / den


def _mem_attn_sample(qm, mk, mv):
    nb = mk.shape[0]
    bb = SAMPLE_BB
    q = qm.reshape(nb, DEC_SEQ, MEM_HEADS, MEM_HEAD_DIM).transpose(0, 2, 1, 3).reshape(nb, MEM_ROWS, MEM_HEAD_DIM)
    rows = pl.BlockSpec((bb, MEM_ROWS, MEM_HEAD_DIM), lambda i: (i, 0, 0))
    kv = pl.BlockSpec((bb, N_MEM, MEM_HEADS, MEM_HEAD_DIM), lambda i: (i, 0, 0, 0))
    o = pl.pallas_call(
        _mem_attn_sample_kernel,
        grid=(nb // bb,),
        in_specs=[rows, kv, kv],
        out_specs=rows,
        out_shape=jax.ShapeDtypeStruct((nb, MEM_ROWS, MEM_HEAD_DIM), jnp.float32),
        compiler_params=_params(("arbitrary",)),
        name="mem_attn_sample",
    )(q, mk, mv)
    return o.reshape(nb, MEM_HEADS, DEC_SEQ, MEM_HEAD_DIM).transpose(0, 2, 1, 3).reshape(nb * DEC_SEQ, D_MODEL)


def _post_b_sample_kernel(o_ref, h1_ref, wo_ref, g2_ref, b2_ref, wrhl_ref, br_ref, tri_ref,
                          h2t_ref, rt_ref, cnt_ref):
    h2 = _layer_norm(ALPHA * h1_ref[...] + _dot(_bf16(o_ref[...]), wo_ref[...]), g2_ref[...], b2_ref[...])
    _store_token_tiles(h2t_ref, h2)
    route, carry = _route(h2, wrhl_ref, br_ref, tri_ref, jnp.zeros((1, LANES), jnp.float32))
    rt_ref[...] = route
    cnt_ref[...] = jnp.broadcast_to(carry, cnt_ref.shape)


def _post_b_sample(o, h1, wo_b, g2, b2, wrhl, br, tri):
    n = h1.shape[0]
    args = (o, h1, wo_b, g2, b2, wrhl, br, tri)
    full = lambda a: pl.BlockSpec(a.shape, lambda i: (0,) * a.ndim)
    return pl.pallas_call(
        _post_b_sample_kernel,
        grid=(1,),
        in_specs=[full(a) for a in args],
        out_specs=[pl.BlockSpec((n * ROW_CHUNKS, LANES), lambda i: (0, 0)),
                   pl.BlockSpec((n, LANES), lambda i: (0, 0)),
                   pl.BlockSpec((SUBLANES, LANES), lambda i: (0, 0))],
        out_shape=[jax.ShapeDtypeStruct((n * ROW_CHUNKS, LANES), jnp.float32),
                   jax.ShapeDtypeStruct((n, LANES), jnp.float32),
                   jax.ShapeDtypeStruct((SUBLANES, LANES), jnp.float32)],
        compiler_params=_params(("arbitrary",)),
        name="post_b_sample",
    )(*args)


def _row_gather_copy(src_hbm, idx, dst, dst_row, sem):
    s0 = pl.multiple_of(idx * ROW_CHUNKS, ROW_CHUNKS)
    d0 = pl.multiple_of(dst_row * ROW_CHUNKS, ROW_CHUNKS)
    return pltpu.make_async_copy(src_hbm.at[pl.ds(s0, ROW_CHUNKS), :], dst.at[pl.ds(d0, ROW_CHUNKS), :], sem)


def _dispatch_kernel(pos_ref, h2t_ref, xs_hbm, sem):
    def body(r, c):
        src = h2t_ref.at[pl.ds(pl.multiple_of(r * ROW_CHUNKS, ROW_CHUNKS), ROW_CHUNKS), :]
        for k in range(2):
            d0 = pl.multiple_of(pos_ref[0, 0, k * TM_COMB + r] * ROW_CHUNKS, ROW_CHUNKS)
            pltpu.make_async_copy(src, xs_hbm.at[pl.ds(d0, ROW_CHUNKS), :], sem.at[0]).start(priority=k)
        return c
    lax.fori_loop(0, TM_COMB, body, 0, unroll=8)
    for _ in range(2):
        pltpu.make_async_copy(h2t_ref, xs_hbm.at[pl.ds(0, TM_COMB * ROW_CHUNKS), :], sem.at[0]).wait()


def _dispatch(pos3, h2t):
    nt = N_ALL // TM_COMB
    return pl.pallas_call(
        _dispatch_kernel,
        grid=(nt,),
        in_specs=[pl.BlockSpec((1, 1, 2 * TM_COMB), lambda i: (i, 0, 0), memory_space=pltpu.SMEM),
                  pl.BlockSpec((TM_COMB * ROW_CHUNKS, LANES), lambda i: (i, 0))],
        out_specs=pl.BlockSpec(memory_space=pl.ANY),
        out_shape=jax.ShapeDtypeStruct((N_ASSIGN * ROW_CHUNKS, LANES), jnp.float32),
        scratch_shapes=[pltpu.SemaphoreType.DMA((1,))],
        compiler_params=_params(("arbitrary",)),
        name="moe_dispatch",
    )(pos3, h2t)


def _moe_ffn_kernel(it_ref, ie_ref, lo_ref, hi_ref, x_ref, wg_ref, wu_ref, wd_ref, y_ref, wgb, wub, wdb, cur_e):
    i = pl.program_id(0)
    lo = lo_ref[i]
    hi = hi_ref[i]
    e = ie_ref[i]

    @pl.when(i == 0)
    def _():
        cur_e[0] = -1

    @pl.when((hi > lo) & (cur_e[0] != e))
    def _():
        wgb[...] = _bf16(wg_ref[0])
        wub[...] = _bf16(wu_ref[0])
        wdb[...] = _bf16(wd_ref[0])
        cur_e[0] = e

    def ffn():
        x = _bf16(_load_token_tiles(x_ref, 0, TM_MOE))
        hg = _dot(x, wgb[...])
        hu = _dot(x, wub[...])
        h = hg / (1.0 + jnp.exp(-hg)) * hu
        return _dot(_bf16(h), wdb[...])

    def rows_mask():
        row = lax.broadcasted_iota(jnp.int32, (TM_MOE, LANES), 0)
        return (row >= lo) & (row < hi)

    @pl.when(hi - lo == TM_MOE)
    def _():
        _store_token_tiles(y_ref, ffn())

    @pl.when((hi > lo) & (lo == 0) & (hi < TM_MOE))
    def _():
        y = ffn()
        mask = rows_mask()
        for c in range(ROW_CHUNKS):
            y_ref[pl.ds(c, TM_MOE, stride=ROW_CHUNKS), :] = jnp.where(mask, y[:, c * LANES:(c + 1) * LANES], 0.0)

    @pl.when((hi > lo) & (lo > 0))
    def _():
        y = ffn()
        mask = rows_mask()
        for c in range(ROW_CHUNKS):
            sl = pl.ds(c, TM_MOE, stride=ROW_CHUNKS)
            y_ref[sl, :] = jnp.where(mask, y[:, c * LANES:(c + 1) * LANES], y_ref[sl, :])


def _moe_ffn(item_tile, item_expert, item_lo, item_hi, x_sorted, w_gate, w_up, w_down):
    wspec = lambda shp: pl.BlockSpec((1,) + shp, lambda i, it, ie, lo, hi: (ie[i], 0, 0))
    tile = pl.BlockSpec((TM_MOE * ROW_CHUNKS, LANES), lambda i, it, ie, lo, hi: (it[i], 0))
    grid_spec = pltpu.PrefetchScalarGridSpec(
        num_scalar_prefetch=4,
        grid=(MOE_ITEMS,),
        in_specs=[tile, wspec((D_MODEL, EXPERT_FF)), wspec((D_MODEL, EXPERT_FF)), wspec((EXPERT_FF, D_MODEL))],
        out_specs=tile,
        scratch_shapes=[pltpu.VMEM((D_MODEL, EXPERT_FF), jnp.bfloat16),
                        pltpu.VMEM((D_MODEL, EXPERT_FF), jnp.bfloat16),
                        pltpu.VMEM((EXPERT_FF, D_MODEL), jnp.bfloat16),
                        pltpu.SMEM((1,), jnp.int32)],
    )
    return pl.pallas_call(
        _moe_ffn_kernel,
        grid_spec=grid_spec,
        out_shape=jax.ShapeDtypeStruct((N_ASSIGN * ROW_CHUNKS, LANES), jnp.float32),
        compiler_params=_params(("arbitrary",)),
        name="moe_ffn",
    )(item_tile, item_expert, item_lo, item_hi, x_sorted, w_gate, w_up, w_down)


def _combine_kernel(nt, pos_cur_ref, pos_nxt_ref, yt_hbm, h2t_ref, rt_ref, g3_ref, b3_ref, o_ref, abuf, sem):
    t = pl.program_id(0)
    slot = t % 2
    rows = 2 * TM_COMB

    def issue(pos_ref, s):
        def body(j, c):
            for k in range(2):
                r = 2 * j + k
                _row_gather_copy(yt_hbm, pos_ref[0, 0, r], abuf, s * rows + r, sem.at[s]).start(priority=k)
            return c
        lax.fori_loop(0, rows // 2, body, 0, unroll=4)

    @pl.when(t == 0)
    def _():
        issue(pos_cur_ref, 0)

    @pl.when(t + 1 < nt)
    def _():
        issue(pos_nxt_ref, 1 - slot)

    base = pl.multiple_of(slot * (rows * ROW_CHUNKS), rows * ROW_CHUNKS)
    pltpu.make_async_copy(yt_hbm.at[pl.ds(0, rows * ROW_CHUNKS), :],
                          abuf.at[pl.ds(base, rows * ROW_CHUNKS), :], sem.at[slot]).wait()
    ya = _load_token_tiles(abuf, base, TM_COMB)
    yb = _load_token_tiles(abuf, base + TM_COMB * ROW_CHUNKS, TM_COMB)
    rt = rt_ref[...]
    ff = rt[:, 2:3] * ya + rt[:, 3:4] * yb
    h2 = _load_token_tiles(h2t_ref, 0, TM_COMB)
    o_ref[...] = _layer_norm(ALPHA * h2 + ff, g3_ref[...], b3_ref[...])


def _combine(pos3, yt, h2t, rt, g3, b3, tile0, n_tiles):
    last = tile0 + n_tiles - 1
    smem_pos = lambda f: pl.BlockSpec((1, 1, 2 * TM_COMB), f, memory_space=pltpu.SMEM)
    full = lambda a: pl.BlockSpec(a.shape, lambda i: (0,) * a.ndim)
    return pl.pallas_call(
        functools.partial(_combine_kernel, n_tiles),
        grid=(n_tiles,),
        in_specs=[smem_pos(lambda i: (tile0 + i, 0, 0)),
                  smem_pos(lambda i: (jnp.minimum(tile0 + i + 1, last), 0, 0)),
                  pl.BlockSpec(memory_space=pl.ANY),
                  pl.BlockSpec((TM_COMB * ROW_CHUNKS, LANES), lambda i: (tile0 + i, 0)),
                  pl.BlockSpec((TM_COMB, LANES), lambda i: (tile0 + i, 0)),
                  full(g3), full(b3)],
        out_specs=pl.BlockSpec((TM_COMB, D_MODEL), lambda i: (i, 0)),
        out_shape=jax.ShapeDtypeStruct((n_tiles * TM_COMB, D_MODEL), jnp.float32),
        scratch_shapes=[pltpu.VMEM((2 * 2 * TM_COMB * ROW_CHUNKS, LANES), jnp.float32),
                        pltpu.SemaphoreType.DMA((2,))],
        compiler_params=_params(("arbitrary",)),
        name="moe_combine",
    )(pos3, pos3, yt, h2t, rt, g3, b3)


POS_TILES = 3


def _positions_kernel(rt_ref, starts_ref, pos_ref):
    lane = lax.broadcasted_iota(jnp.int32, (TM_COMB, LANES), 1)
    lane_f = lane.astype(jnp.float32)
    starts = starts_ref[0:1, :]
    for j in range(POS_TILES):
        rt = rt_ref[j * TM_COMB:(j + 1) * TM_COMB, :]
        cols = []
        for k in range(2):
            seg = jnp.sum(jnp.where(lane_f == rt[:, k:k + 1] + N_GROUPS, starts, 0.0), axis=1, keepdims=True)
            cols.append(seg + rt[:, 4 + k:5 + k])
        packed = jnp.where(lane == 0, cols[0], jnp.where(lane == 1, cols[1], 0.0))
        rows = packed.T
        pos_ref[j] = jnp.concatenate([rows[0:1, :], rows[1:2, :]], axis=1).astype(jnp.int32)


def _positions(rt, starts_row):
    nt = N_ALL // TM_COMB
    assert nt % POS_TILES == 0
    return pl.pallas_call(
        _positions_kernel,
        grid=(nt // POS_TILES,),
        in_specs=[pl.BlockSpec((POS_TILES * TM_COMB, LANES), lambda i: (i, 0)),
                  pl.BlockSpec((SUBLANES, LANES), lambda i: (0, 0))],
        out_specs=pl.BlockSpec((POS_TILES, 1, 2 * TM_COMB), lambda i: (i, 0, 0)),
        out_shape=jax.ShapeDtypeStruct((nt, 1, 2 * TM_COMB), jnp.int32),
        compiler_params=_params(("arbitrary",)),
        name="moe_positions",
    )(rt, starts_row)


def _routing_plan(rt, cnt):
    i32 = jnp.int32
    counts_f = cnt[0, N_GROUPS:N_GROUPS + N_EXPERTS]
    starts_f = jnp.cumsum(counts_f) - counts_f
    starts_row = jnp.broadcast_to(
        jnp.pad(starts_f, (N_GROUPS, LANES - N_GROUPS - N_EXPERTS))[None, :], (SUBLANES, LANES))
    pos3 = _positions(rt, starts_row)
    starts = starts_f.astype(i32)
    tiles = jnp.arange(MOE_TILES, dtype=i32) * TM_MOE
    rank_t = jnp.arange(MOE_TILES, dtype=i32) + jnp.sum((starts[None, :] < tiles[:, None]).astype(i32), axis=1)
    rank_s = jnp.arange(N_EXPERTS, dtype=i32) + jnp.sum((tiles[None, :] <= starts[:, None]).astype(i32), axis=1)
    vals = jnp.concatenate([tiles, starts])
    ranks = jnp.concatenate([rank_t, rank_s])
    slot = jnp.arange(MOE_ITEMS, dtype=i32)
    lo = jnp.sum(jnp.where(ranks[None, :] == slot[:, None], vals[None, :], 0), axis=1)
    hi = jnp.concatenate([lo[1:], jnp.full((1,), N_ASSIGN, i32)])
    item_tile = jnp.minimum(lo // TM_MOE, MOE_TILES - 1)
    item_expert = jnp.clip(jnp.sum((starts[None, :] <= lo[:, None]).astype(i32), axis=1) - 1, 0, N_EXPERTS - 1)
    base = item_tile * TM_MOE
    return item_tile, item_expert, lo - base, hi - base, pos3


def kernel(x_prompt, x_sample, mem_prompt, cache_swa_k, cache_swa_v, cache_conv, cache_mem_k, cache_mem_v,
           w_in, sinks, conv_w, w_mix_out, ln1_g, ln1_b, w_q_mem, w_k_mem, w_v_mem, w_o_mem, ln2_g, ln2_b,
           w_router_group, b_router_group, w_router_expert, b_router_expert, w_gate, w_up, w_down,
           ln3_g, ln3_b):
    f32 = jnp.float32
    row = lambda a: a.reshape(1, -1).astype(f32)
    w_in_b, wmix_b, wq_b, wk_b, wv_b, wo_b = (_bf16(w) for w in (w_in, w_mix_out, w_q_mem, w_k_mem, w_v_mem, w_o_mem))
    g1, b1, g2, b2, g3, b3 = (row(a) for a in (ln1_g, ln1_b, ln2_g, ln2_b, ln3_g, ln3_b))
    pad = LANES - N_GROUPS - N_EXPERTS
    wr = jnp.concatenate([w_router_group, w_router_expert, jnp.zeros((D_MODEL, pad), f32)], axis=1)
    wrh = _bf16(wr)
    wrhl = jnp.concatenate([wrh, _bf16(wr - wrh.astype(f32))], axis=1)
    br = jnp.concatenate([b_router_group, b_router_expert, jnp.zeros((pad,), f32)]).reshape(1, LANES)

    xs = x_sample.reshape(N_SAMPLE, D_MODEL)
    tab_s = jnp.tile(_rope_table(PAST_LEN + jnp.arange(DEC_SEQ)), (1, DEC_BATCH))
    c0 = jnp.repeat(cache_conv[:, 0], DEC_SEQ, axis=0)
    c1 = jnp.repeat(cache_conv[:, 1], DEC_SEQ, axis=0)
    q_s, k_s, v_s, conv_s, u_s = _proj_sample(xs, w_in_b, tab_s, conv_w, c0, c1)
    attn_s, swa_k_s, swa_v_s = _swa_sample(sinks, q_s, k_s, v_s, cache_swa_k, cache_swa_v)
    h1_s, qm_s = _post_a_sample(attn_s, conv_s, xs, wmix_b, g1, b1, wq_b)
    o_s = _mem_attn_sample(qm_s, cache_mem_k, cache_mem_v)
    tri = _bf16(jnp.tril(jnp.ones((TM_POST, TM_POST), f32), -1))
    h2t_s, rt_s, cnt_s = _post_b_sample(o_s, h1_s, wo_b, g2, b2, wrhl, br, tri)

    xp = x_prompt.reshape(N_PROMPT, D_MODEL)
    tab_p = _rope_table(jnp.arange(N_PROMPT))
    q_p, kx_p, vx_p, conv_p, k_tail, v_tail, u_tail = _proj_prompt(xp, w_in_b, tab_p, conv_w)
    attn_p = _swa_prompt(sinks, q_p, kx_p, vx_p)
    mk, mv, mk_b, mv_b = _mem_kv(mem_prompt.reshape(N_MEM, D_MODEL), wk_b, wv_b)
    h2t, rt, cnt = _post_prompt(attn_p, conv_p, xp, wmix_b, g1, b1, wq_b, mk_b, mv_b, wo_b, g2, b2,
                                wrhl, br, tri, h2t_s, rt_s, cnt_s)

    item_tile, item_expert, item_lo, item_hi, pos3 = _routing_plan(rt, cnt)
    x_sorted = _dispatch(pos3, h2t)
    yt = _moe_ffn(item_tile, item_expert, item_lo, item_hi, x_sorted, w_gate, w_up, w_down)
    y_p = _combine(pos3, yt, h2t, rt, g3, b3, 0, N_PROMPT // TM_COMB)
    y_s = _combine(pos3, yt, h2t, rt, g3, b3, N_PROMPT // TM_COMB, N_SAMPLE // TM_COMB)

    return (y_p.reshape(1, SEQ, D_MODEL),
            y_s.reshape(DEC_BATCH, DEC_SEQ, D_MODEL),
            k_tail.reshape(1, WINDOW, N_KV_HEADS, HEAD_DIM),
            v_tail.reshape(1, WINDOW, N_KV_HEADS, HEAD_DIM),
            u_tail[SUBLANES - (CONV_K - 1):].reshape(1, CONV_K - 1, CONV_CH),
            mk.reshape(1, N_MEM, MEM_HEADS, MEM_HEAD_DIM),
            mv.reshape(1, N_MEM, MEM_HEADS, MEM_HEAD_DIM),
            swa_k_s.reshape(DEC_BATCH, WINDOW, N_KV_HEADS, HEAD_DIM),
            swa_v_s.reshape(DEC_BATCH, WINDOW, N_KV_HEADS, HEAD_DIM),
            u_s.reshape(DEC_BATCH, DEC_SEQ, CONV_CH)[:, DEC_SEQ - (CONV_K - 1):])
```

```python
import functools

import jax
import jax.numpy as jnp
from jax import lax
from jax.experimental import pallas as pl
from jax.experimental.pallas import tpu as pltpu

D_MODEL = 1024
SEQ = 16384
DEC_BATCH = 128
DEC_SEQ = 4
PAST_LEN = 16384
ATTN_WIDTH = 512
CONV_CH = 512
HEAD_DIM = 64
N_HEADS = 8
N_KV_HEADS = 2
KV_WIDTH = 128
WINDOW = 128
ROPE_THETA = 500000.0
ROPE_DIM = 16
CONV_K = 3
Q_END = ATTN_WIDTH
K_END = Q_END + KV_WIDTH
V_END = K_END + KV_WIDTH
B_END = V_END + CONV_CH
C_END = B_END + CONV_CH
IN_WIDTH = C_END + CONV_CH
N_MEM = 256
MEM_HEADS = 4
MEM_HEAD_DIM = 256
N_GROUPS = 4
EXPERTS_PER_GROUP = 8
N_EXPERTS = 32
EXPERT_FF = 256
ALPHA = 2.0 ** 0.25
LN_EPS = 1e-5

LANES = 128
SUBLANES = 8
ROW_CHUNKS = D_MODEL // LANES
VMEM_LIMIT = 56 * 1024 * 1024

N_PROMPT = SEQ
N_SAMPLE = DEC_BATCH * DEC_SEQ
N_ALL = N_PROMPT + N_SAMPLE
TM_PROJ = 1024
TM_POST = 512
TM_MOE = 512
TM_COMB = 512
N_ASSIGN = 2 * N_ALL
MOE_TILES = N_ASSIGN // TM_MOE
MOE_ITEMS = MOE_TILES + N_EXPERTS
SAMPLE_BB = 4
SWA_BB = 8

assert ROW_CHUNKS == SUBLANES
assert N_SAMPLE == TM_POST
assert N_ASSIGN % TM_MOE == 0 and N_ALL % TM_COMB == 0


def _params(sem, vmem=VMEM_LIMIT):
    return pltpu.CompilerParams(dimension_semantics=sem, vmem_limit_bytes=vmem)


def _bf16(x):
    return x.astype(jnp.bfloat16)


def _dot(a, b):
    return jnp.dot(a, b, preferred_element_type=jnp.float32)


def _dot_nt(a, b):
    return lax.dot_general(a, b, (((1,), (1,)), ((), ())), preferred_element_type=jnp.float32)


def _layer_norm(x, g, b):
    mu = jnp.mean(x, axis=-1, keepdims=True)
    xc = x - mu
    var = jnp.mean(xc * xc, axis=-1, keepdims=True)
    return xc * lax.rsqrt(var + LN_EPS) * g + b


def _rope(x, cos_t, sin_t):
    lane = lax.broadcasted_iota(jnp.int32, x.shape, 1) % HEAD_DIM
    half = ROPE_DIM // 2
    partner = jnp.where(lane < half, pltpu.roll(x, LANES - half, axis=1), pltpu.roll(x, half, axis=1))
    return x * cos_t + partner * sin_t


def _head_slabs(x):
    lane = lax.broadcasted_iota(jnp.int32, x.shape, 1)
    lo = lane < HEAD_DIM
    sw = pltpu.roll(x, HEAD_DIM, axis=1)
    zero = jnp.zeros_like(x)
    slabs = [jnp.where(lo, x, zero), jnp.where(lo, zero, sw), jnp.where(lo, sw, zero), jnp.where(lo, zero, x)]
    return _bf16(jnp.concatenate(slabs, axis=1))


def _store_token_tiles(ref, val):
    rows = val.shape[0]
    for c in range(ROW_CHUNKS):
        ref[pl.ds(c, rows, stride=ROW_CHUNKS), :] = val[:, c * LANES:(c + 1) * LANES]


def _load_token_tiles(ref, base, rows):
    return jnp.concatenate(
        [ref[pl.ds(base + c, rows, stride=ROW_CHUNKS), :] for c in range(ROW_CHUNKS)], axis=1)


ROPE_ONE = 3 * (ROPE_DIM // 2)
ROPE_ROWS = 32


def _rope_patterns(tab):
    half = ROPE_DIM // 2
    m = lax.broadcasted_iota(jnp.int32, tab.shape, 1) % HEAD_DIM
    idx_c = jnp.where(m < ROPE_DIM, m % half, ROPE_ONE)
    idx_s = jnp.where(m < half, 2 * half + m, jnp.where(m < ROPE_DIM, m, ROPE_ONE + 1))
    return jnp.take_along_axis(tab, idx_c, axis=1), jnp.take_along_axis(tab, idx_s, axis=1)


def _proj_common(x_ref, w_ref, tab_ref):
    xb = _bf16(x_ref[...])
    tab = tab_ref[...]
    pad = jnp.zeros((LANES - tab.shape[0], tab.shape[1]), jnp.float32)
    cos_t, sin_t = _rope_patterns(jnp.concatenate([tab, pad], axis=0).T)
    q = _dot(xb, w_ref[:, 0:Q_END])
    q_rot = jnp.concatenate(
        [_rope(q[:, p * LANES:(p + 1) * LANES], cos_t, sin_t) for p in range(ATTN_WIDTH // LANES)], axis=1)
    q_out = _bf16(q_rot * (HEAD_DIM ** -0.5))
    kv = _dot(xb, w_ref[:, Q_END:V_END])
    k = _rope(kv[:, 0:KV_WIDTH], cos_t, sin_t)
    v = kv[:, KV_WIDTH:]
    bg = _dot(xb, w_ref[:, V_END:B_END])
    u = _dot(xb, w_ref[:, B_END:C_END]) * _dot(xb, w_ref[:, C_END:IN_WIDTH])
    return q_out, k, v, bg, u


def _conv3(bg, u, u1, u2, cw_ref):
    cw = cw_ref[...]
    return bg * (cw[0:1, :] * u2 + cw[1:2, :] * u1 + cw[2:3, :] * u)


def _proj_prompt_kernel(x_ref, w_ref, tab_ref, cw_ref,
                        q_ref, kx_ref, vx_ref, conv_ref, ktail_ref, vtail_ref, utail_ref, carry_ref):
    @pl.when(pl.program_id(0) == 0)
    def _():
        carry_ref[...] = jnp.zeros_like(carry_ref)

    q_out, k, v, bg, u = _proj_common(x_ref, w_ref, tab_ref)
    tm = u.shape[0]
    ext = jnp.concatenate([carry_ref[...], u], axis=0)
    u1 = pltpu.roll(ext, 1, axis=0)[SUBLANES:SUBLANES + tm]
    u2 = pltpu.roll(ext, 2, axis=0)[SUBLANES:SUBLANES + tm]
    q_ref[...] = q_out
    kx_ref[...] = _head_slabs(k)
    vx_ref[...] = _head_slabs(v)
    conv_ref[...] = _bf16(_conv3(bg, u, u1, u2, cw_ref))
    ktail_ref[...] = k[tm - WINDOW:tm]
    vtail_ref[...] = v[tm - WINDOW:tm]
    utail_ref[...] = u[tm - SUBLANES:tm]
    carry_ref[...] = u[tm - SUBLANES:tm]


def _proj_sample_kernel(x_ref, w_ref, tab_ref, cw_ref, c0_ref, c1_ref,
                        q_ref, k_ref, v_ref, conv_ref, u_ref):
    q_out, k, v, bg, u = _proj_common(x_ref, w_ref, tab_ref)
    t = lax.broadcasted_iota(jnp.int32, u.shape, 0) % DEC_SEQ
    c0 = c0_ref[...]
    c1 = c1_ref[...]
    u1 = jnp.where(t >= 1, pltpu.roll(u, 1, axis=0), c1)
    u2 = jnp.where(t >= 2, pltpu.roll(u, 2, axis=0), jnp.where(t == 1, c1, c0))
    q_ref[...] = q_out.astype(jnp.float32)
    k_ref[...] = k
    v_ref[...] = v
    conv_ref[...] = _bf16(_conv3(bg, u, u1, u2, cw_ref))
    u_ref[...] = u


def _rope_table(pos):
    half = ROPE_DIM // 2
    inv = ROPE_THETA ** (-jnp.arange(0, ROPE_DIM, 2, dtype=jnp.float32) / ROPE_DIM)
    ang = pos.astype(jnp.float32)[None, :] * inv[:, None]
    cos, sin = jnp.cos(ang), jnp.sin(ang)
    n = pos.shape[0]
    assert ROPE_ONE == 3 * half
    return jnp.concatenate([cos, sin, -sin, jnp.ones((1, n), jnp.float32),
                            jnp.zeros((ROPE_ROWS - ROPE_ONE - 1, n), jnp.float32)], axis=0)


def _proj_prompt(x, w_in_b, tab, conv_w):
    n = x.shape[0]
    tm = TM_PROJ
    row = lambda w: pl.BlockSpec((tm, w), lambda i: (i, 0))
    full = lambda a: pl.BlockSpec(a.shape, lambda i: (0,) * a.ndim)
    const = lambda r, w: pl.BlockSpec((r, w), lambda i: (0, 0))
    return pl.pallas_call(
        _proj_prompt_kernel,
        grid=(n // tm,),
        in_specs=[row(D_MODEL), full(w_in_b), pl.BlockSpec((ROPE_ROWS, tm), lambda i: (0, i)), full(conv_w)],
        out_specs=[row(ATTN_WIDTH), row(4 * LANES), row(4 * LANES), row(CONV_CH),
                   const(WINDOW, KV_WIDTH), const(WINDOW, KV_WIDTH), const(SUBLANES, CONV_CH)],
        out_shape=[jax.ShapeDtypeStruct((n, ATTN_WIDTH), jnp.bfloat16),
                   jax.ShapeDtypeStruct((n, 4 * LANES), jnp.bfloat16),
                   jax.ShapeDtypeStruct((n, 4 * LANES), jnp.bfloat16),
                   jax.ShapeDtypeStruct((n, CONV_CH), jnp.bfloat16),
                   jax.ShapeDtypeStruct((WINDOW, KV_WIDTH), jnp.float32),
                   jax.ShapeDtypeStruct((WINDOW, KV_WIDTH), jnp.float32),
                   jax.ShapeDtypeStruct((SUBLANES, CONV_CH), jnp.float32)],
        scratch_shapes=[pltpu.VMEM((SUBLANES, CONV_CH), jnp.float32)],
        compiler_params=_params(("arbitrary",)),
        name="proj_prompt",
    )(x, w_in_b, tab, conv_w)


def _proj_sample(x, w_in_b, tab, conv_w, c0, c1):
    n = x.shape[0]
    full = lambda a: pl.BlockSpec(a.shape, lambda i: (0,) * a.ndim)
    out = lambda w, dt: jax.ShapeDtypeStruct((n, w), dt)
    blk = lambda w: pl.BlockSpec((n, w), lambda i: (0, 0))
    return pl.pallas_call(
        _proj_sample_kernel,
        grid=(1,),
        in_specs=[full(x), full(w_in_b), full(tab), full(conv_w), full(c0), full(c1)],
        out_specs=[blk(ATTN_WIDTH), blk(KV_WIDTH), blk(KV_WIDTH), blk(CONV_CH), blk(CONV_CH)],
        out_shape=[out(ATTN_WIDTH, jnp.float32), out(KV_WIDTH, jnp.float32), out(KV_WIDTH, jnp.float32),
                   out(CONV_CH, jnp.bfloat16), out(CONV_CH, jnp.float32)],
        compiler_params=_params(("arbitrary",)),
        name="proj_sample",
    )(x, w_in_b, tab, conv_w, c0, c1)


def _sink_softmax_pv(s, valid, sink, vx):
    s = jnp.where(valid, s, -jnp.inf)
    m = jnp.maximum(jnp.max(s, axis=1, keepdims=True), sink)
    p = jnp.exp(s - m)
    den = jnp.sum(p, axis=1, keepdims=True) + jnp.exp(sink - m)
    return _dot(_bf16(p), vx) / den


SWA_QB = 4


def _swa_prompt_kernel(sinks_ref, q_ref, kc_ref, kp_ref, vc_ref, vp_ref, o_ref):
    step = pl.program_id(0)
    kall = jnp.concatenate([kp_ref[...], kc_ref[...]], axis=0)
    vall = jnp.concatenate([vp_ref[...], vc_ref[...]], axis=0)
    i = lax.broadcasted_iota(jnp.int32, (WINDOW, 2 * WINDOW), 0)
    j = lax.broadcasted_iota(jnp.int32, (WINDOW, 2 * WINDOW), 1)
    band = (j > i) & (j <= i + WINDOW)
    for sb in range(SWA_QB):
        rows = slice(sb * WINDOW, (sb + 1) * WINDOW)
        kcat = kall[sb * WINDOW:(sb + 2) * WINDOW]
        vcat = vall[sb * WINDOW:(sb + 2) * WINDOW]
        valid = band & ((step > 0) | (j >= WINDOW)) if sb == 0 else band
        for p in range(N_HEADS // 2):
            qs = q_ref[rows, p * LANES:(p + 1) * LANES]
            acc = None
            for e in range(2):
                hd = 2 * p + e
                slab = 2 * (hd // (N_HEADS // N_KV_HEADS)) + e
                kx = kcat[:, slab * LANES:(slab + 1) * LANES]
                vx = vcat[:, slab * LANES:(slab + 1) * LANES]
                o = _sink_softmax_pv(_dot_nt(qs, kx), valid, sinks_ref[hd], vx)
                acc = o if acc is None else acc + o
            o_ref[rows, p * LANES:(p + 1) * LANES] = _bf16(acc)


def _swa_prompt(sinks, q, kx, vx):
    n = q.shape[0]
    nb = n // (SWA_QB * WINDOW)
    cur = lambda w: pl.BlockSpec((SWA_QB * WINDOW, w), lambda i: (i, 0))
    prev = lambda w: pl.BlockSpec((WINDOW, w), lambda i: (jnp.maximum(SWA_QB * i - 1, 0), 0))
    return pl.pallas_call(
        _swa_prompt_kernel,
        grid=(nb,),
        in_specs=[pl.BlockSpec(memory_space=pltpu.SMEM), cur(ATTN_WIDTH),
                  cur(4 * LANES), prev(4 * LANES), cur(4 * LANES), prev(4 * LANES)],
        out_specs=cur(ATTN_WIDTH),
        out_shape=jax.ShapeDtypeStruct((n, ATTN_WIDTH), jnp.bfloat16),
        compiler_params=_params(("arbitrary",)),
        name="swa_prompt",
    )(sinks, q, kx, kx, vx, vx)


SWA_ROWS = N_HEADS * DEC_SEQ
NEW_ROWS = 2 * SUBLANES


def _swa_sample_kernel(q_ref, sink_ref, kn_ref, vn_ref, kt_ref, vt_ref, o_ref, okt_ref, ovt_ref):
    t = lax.broadcasted_iota(jnp.int32, (SWA_ROWS, WINDOW), 0) % DEC_SEQ
    j = lax.broadcasted_iota(jnp.int32, (SWA_ROWS, WINDOW), 1)
    valid_c = j > t
    valid_n = (lax.broadcasted_iota(jnp.int32, (SWA_ROWS, NEW_ROWS), 1)
               <= lax.broadcasted_iota(jnp.int32, (SWA_ROWS, NEW_ROWS), 0) % DEC_SEQ)
    lane = lax.broadcasted_iota(jnp.int32, (KV_WIDTH, WINDOW), 1)
    sink = sink_ref[:, 0:1]
    shift = WINDOW - DEC_SEQ
    zrows = jnp.zeros((KV_WIDTH - NEW_ROWS, KV_WIDTH), jnp.float32)
    for b in range(SWA_BB):
        q = _bf16(q_ref[b])
        kt, vt = kt_ref[b], vt_ref[b]
        kn, vn = kn_ref[b], vn_ref[b]
        s_c = jnp.where(valid_c, _dot(q, _bf16(kt)), -jnp.inf)
        s_n = jnp.where(valid_n, _dot_nt(q, _bf16(kn)), -jnp.inf)
        m = jnp.maximum(jnp.maximum(jnp.max(s_c, axis=1, keepdims=True), jnp.max(s_n, axis=1, keepdims=True)), sink)
        p_c = jnp.exp(s_c - m)
        p_n = jnp.exp(s_n - m)
        den = jnp.sum(p_c, axis=1, keepdims=True) + jnp.sum(p_n, axis=1, keepdims=True) + jnp.exp(sink - m)
        o_ref[b] = (_dot_nt(_bf16(p_c), _bf16(vt)) + _dot(_bf16(p_n), _bf16(vn))) / den
        for old, new, dst in ((kt, kn, okt_ref), (vt, vn, ovt_ref)):
            new_cols = pltpu.roll(jnp.concatenate([new, zrows], axis=0).T, shift, axis=1)
            dst[b] = jnp.where(lane >= shift, new_cols, pltpu.roll(old, shift, axis=1))


def _swa_sample(sinks, q, kn, vn, cache_k, cache_v):
    nb = cache_k.shape[0]
    bb = SWA_BB
    groups = N_HEADS // N_KV_HEADS
    qh = q.reshape(nb, DEC_SEQ, N_KV_HEADS, groups, HEAD_DIM).transpose(0, 2, 3, 1, 4)
    qh = qh.reshape(nb, N_KV_HEADS, groups * DEC_SEQ, HEAD_DIM)
    zeros = jnp.zeros_like(qh[:, 0])
    qbd = jnp.concatenate([jnp.concatenate([qh[:, 0], zeros], axis=-1),
                           jnp.concatenate([zeros, qh[:, 1]], axis=-1)], axis=1)
    sink_col = jnp.broadcast_to(jnp.repeat(sinks, DEC_SEQ).reshape(SWA_ROWS, 1), (SWA_ROWS, LANES))
    pad8 = lambda a: jnp.pad(a.reshape(nb, DEC_SEQ, KV_WIDTH), ((0, 0), (0, NEW_ROWS - DEC_SEQ), (0, 0)))
    to_t = lambda c: c.transpose(0, 2, 3, 1).reshape(nb, KV_WIDTH, WINDOW)
    blk = lambda r, w: pl.BlockSpec((bb, r, w), lambda i: (i, 0, 0))
    o, okt, ovt = pl.pallas_call(
        _swa_sample_kernel,
        grid=(nb // bb,),
        in_specs=[blk(SWA_ROWS, KV_WIDTH), pl.BlockSpec((SWA_ROWS, LANES), lambda i: (0, 0)),
                  blk(NEW_ROWS, KV_WIDTH), blk(NEW_ROWS, KV_WIDTH), blk(KV_WIDTH, WINDOW), blk(KV_WIDTH, WINDOW)],
        out_specs=[blk(SWA_ROWS, KV_WIDTH), blk(KV_WIDTH, WINDOW), blk(KV_WIDTH, WINDOW)],
        out_shape=[jax.ShapeDtypeStruct((nb, SWA_ROWS, KV_WIDTH), jnp.float32),
                   jax.ShapeDtypeStruct((nb, KV_WIDTH, WINDOW), jnp.float32),
                   jax.ShapeDtypeStruct((nb, KV_WIDTH, WINDOW), jnp.float32)],
        compiler_params=_params(("arbitrary",)),
        name="swa_sample",
    )(qbd, sink_col, pad8(kn), pad8(vn), to_t(cache_k), to_t(cache_v))
    o = o.reshape(nb, N_KV_HEADS, groups, DEC_SEQ, N_KV_HEADS, HEAD_DIM)
    attn = jnp.stack([o[:, h, :, :, h, :] for h in range(N_KV_HEADS)], axis=1)
    attn = attn.transpose(0, 3, 1, 2, 4).reshape(nb * DEC_SEQ, ATTN_WIDTH)
    from_t = lambda c: c.reshape(nb, N_KV_HEADS, HEAD_DIM, WINDOW).transpose(0, 3, 1, 2)
    return attn, from_t(okt), from_t(ovt)


def _mem_kv_kernel(mem_ref, wk_ref, wv_ref, wq_ref, wo_ref, mk_ref, mv_ref, wqk_ref, wvo_ref):
    mb = _bf16(mem_ref[...])
    mk = _dot(mb, wk_ref[...])
    mv = _dot(mb, wv_ref[...])
    mk_ref[...] = mk
    mv_ref[...] = mv
    mkb, mvb = _bf16(mk), _bf16(mv)
    for h in range(MEM_HEADS):
        sl = slice(h * MEM_HEAD_DIM, (h + 1) * MEM_HEAD_DIM)
        keys = slice(h * N_MEM, (h + 1) * N_MEM)
        wqk_ref[:, keys] = _bf16(_dot_nt(wq_ref[:, sl], mkb[:, sl]) * (MEM_HEAD_DIM ** -0.5))
        wvo_ref[keys, :] = _bf16(_dot(mvb[:, sl], wo_ref[sl, :]))


def _mem_kv(mem, wk_b, wv_b, wq_b, wo_b):
    full = lambda a: pl.BlockSpec(a.shape, lambda i: (0,) * a.ndim)
    blk = pl.BlockSpec((N_MEM, D_MODEL), lambda i: (0, 0))
    f32 = jax.ShapeDtypeStruct((N_MEM, D_MODEL), jnp.float32)
    fused = (D_MODEL, MEM_HEADS * N_MEM), (MEM_HEADS * N_MEM, D_MODEL)
    return pl.pallas_call(
        _mem_kv_kernel,
        grid=(1,),
        in_specs=[full(mem), full(wk_b), full(wv_b), full(wq_b), full(wo_b)],
        out_specs=[blk, blk] + [pl.BlockSpec(shp, lambda i: (0, 0)) for shp in fused],
        out_shape=[f32, f32] + [jax.ShapeDtypeStruct(shp, jnp.bfloat16) for shp in fused],
        compiler_params=_params(("arbitrary",)),
        name="mem_kv",
    )(mem, wk_b, wv_b, wq_b, wo_b)


def _mix_ln1(attn_ref, conv_ref, x_ref, wmix_ref, g1_ref, b1_ref):
    mix = _dot(_bf16(attn_ref[...]), wmix_ref[0:ATTN_WIDTH, :]) + _dot(conv_ref[...], wmix_ref[ATTN_WIDTH:, :])
    return _layer_norm(ALPHA * x_ref[...] + mix, g1_ref[...], b1_ref[...])


def _mem_q(h1, wq_ref):
    return _bf16(_dot(_bf16(h1), wq_ref[...]) * (MEM_HEAD_DIM ** -0.5))


def _route(h2, wrhl_ref, br_ref, tri_ref, carry):
    hi = _bf16(h2)
    lo = _bf16(h2 - hi.astype(jnp.float32))
    hh = _dot(hi, wrhl_ref[...])
    logits = hh[:, 0:LANES] + hh[:, LANES:] + _dot(lo, wrhl_ref[:, 0:LANES]) + br_ref[...]
    lane_i = lax.broadcasted_iota(jnp.int32, logits.shape, 1)
    lane = lane_i.astype(jnp.float32)
    big = jnp.float32(LANES)
    is_g = lane_i < N_GROUPS
    gl = jnp.where(is_g, logits, -jnp.inf)
    gmax = jnp.max(gl, axis=1, keepdims=True)
    gidx = jnp.min(jnp.where(is_g & (logits == gmax), lane, big), axis=1, keepdims=True)
    gsum = jnp.sum(jnp.exp(gl - gmax), axis=1, keepdims=True)
    gw = 1.0 / gsum
    eid = lane_i - N_GROUPS
    assert EXPERTS_PER_GROUP == 8
    grp = lax.shift_right_arithmetic(eid, jnp.full_like(eid, 3)).astype(jnp.float32)
    in_e = (lane_i >= N_GROUPS) & (lane_i < N_GROUPS + N_EXPERTS) & (grp == gidx)
    v1 = jnp.max(jnp.where(in_e, logits, -jnp.inf), axis=1, keepdims=True)
    i1 = jnp.min(jnp.where(in_e & (logits == v1), lane, big), axis=1, keepdims=True)
    rest = in_e & (lane != i1)
    v2 = jnp.max(jnp.where(rest, logits, -jnp.inf), axis=1, keepdims=True)
    i2 = jnp.min(jnp.where(rest & (logits == v2), lane, big), axis=1, keepdims=True)
    ex = jnp.exp(v2 - v1)
    den = 1.0 + ex
    w1 = gw / den
    w2 = gw * ex / den
    zero = jnp.zeros_like(logits)
    pick1 = lane == i1
    pick2 = lane == i2
    sel = jnp.where(pick1 | pick2, 1.0, 0.0)
    before = _dot(tri_ref[...], _bf16(sel)) + carry
    rank1 = jnp.sum(jnp.where(pick1, before, zero), axis=1, keepdims=True)
    rank2 = jnp.sum(jnp.where(pick2, before, zero), axis=1, keepdims=True)
    cols = (i1 - N_GROUPS, i2 - N_GROUPS, w1, w2, rank1, rank2)
    route = zero
    for k, col in enumerate(cols):
        route = jnp.where(lane_i == k, col, route)
    return route, carry + jnp.sum(sel, axis=0, keepdims=True)


def _post_prompt_kernel(attn_ref, conv_ref, x_ref, wmix_ref, g1_ref, b1_ref, wqk_ref, wvo_ref,
                        g2_ref, b2_ref, wrhl_ref, br_ref, tri_ref, h2s_ref, rts_ref, cnts_ref,
                        h2t_ref, rt_ref, cnt_ref, carry_ref):
    is_tail = pl.program_id(0) == N_PROMPT // TM_POST

    @pl.when(pl.program_id(0) == 0)
    def _():
        carry_ref[...] = cnts_ref[...]

    @pl.when(is_tail)
    def _():
        h2t_ref[...] = h2s_ref[...]
        rt_ref[...] = rts_ref[...]

    @pl.when(jnp.logical_not(is_tail))
    def _():
        h1 = _mix_ln1(attn_ref, conv_ref, x_ref, wmix_ref, g1_ref, b1_ref)
        scores = _dot(_bf16(h1), wqk_ref[...])
        probs = []
        for h in range(MEM_HEADS):
            s = scores[:, h * N_MEM:(h + 1) * N_MEM]
            p = jnp.exp(s - jnp.max(s, axis=1, keepdims=True))
            probs.append(_bf16(p / jnp.sum(p, axis=1, keepdims=True)))
        mem_out = _dot(jnp.concatenate(probs, axis=1), wvo_ref[...])
        h2 = _layer_norm(ALPHA * h1 + mem_out, g2_ref[...], b2_ref[...])
        _store_token_tiles(h2t_ref, h2)
        route, carry = _route(h2, wrhl_ref, br_ref, tri_ref, carry_ref[0:1, :])
        rt_ref[...] = route
        carry_ref[...] = jnp.broadcast_to(carry, carry_ref.shape)
        cnt_ref[...] = jnp.broadcast_to(carry, cnt_ref.shape)


def _post_prompt(attn, conv, x, wmix_b, g1, b1, wqk, wvo, g2, b2, wrhl, br, tri, h2t_s, rt_s, cnt_s):
    n = x.shape[0]
    tm = TM_POST
    steps = n // tm
    row = lambda w: pl.BlockSpec((tm, w), lambda i: (jnp.minimum(i, steps - 1), 0))
    full = lambda a: pl.BlockSpec(a.shape, lambda i: (0,) * a.ndim)
    weights = (wmix_b, g1, b1, wqk, wvo, g2, b2, wrhl, br, tri, h2t_s, rt_s, cnt_s)
    n_out = n + h2t_s.shape[0] // ROW_CHUNKS
    return pl.pallas_call(
        _post_prompt_kernel,
        grid=(steps + 1,),
        in_specs=[row(ATTN_WIDTH), row(CONV_CH), row(D_MODEL)] + [full(a) for a in weights],
        out_specs=[pl.BlockSpec((tm * ROW_CHUNKS, LANES), lambda i: (i, 0)),
                   pl.BlockSpec((tm, LANES), lambda i: (i, 0)),
                   pl.BlockSpec((SUBLANES, LANES), lambda i: (0, 0))],
        out_shape=[jax.ShapeDtypeStruct((n_out * ROW_CHUNKS, LANES), jnp.float32),
                   jax.ShapeDtypeStruct((n_out, LANES), jnp.float32),
                   jax.ShapeDtypeStruct((SUBLANES, LANES), jnp.float32)],
        scratch_shapes=[pltpu.VMEM((SUBLANES, LANES), jnp.float32)],
        compiler_params=_params(("arbitrary",)),
        name="post_prompt",
    )(attn, conv, x, *weights)


def _post_a_sample_kernel(attn_ref, conv_ref, x_ref, wmix_ref, g1_ref, b1_ref, wq_ref, h1_ref, qm_ref):
    h1 = _mix_ln1(attn_ref, conv_ref, x_ref, wmix_ref, g1_ref, b1_ref)
    h1_ref[...] = h1
    qm_ref[...] = _mem_q(h1, wq_ref).astype(jnp.float32)


def _post_a_sample(attn, conv, x, wmix_b, g1, b1, wq_b):
    n = x.shape[0]
    args = (attn, conv, x, wmix_b, g1, b1, wq_b)
    full = lambda a: pl.BlockSpec(a.shape, lambda i: (0,) * a.ndim)
    blk = pl.BlockSpec((n, D_MODEL), lambda i: (0, 0))
    return pl.pallas_call(
        _post_a_sample_kernel,
        grid=(1,),
        in_specs=[full(a) for a in args],
        out_specs=[blk, blk],
        out_shape=[jax.ShapeDtypeStruct((n, D_MODEL), jnp.float32),
                   jax.ShapeDtypeStruct((n, D_MODEL), jnp.float32)],
        compiler_params=_params(("arbitrary",)),
        name="post_a_sample",
    )(*args)


MEM_ROWS = MEM_HEADS * DEC_SEQ


def _mem_attn_sample_kernel(q_ref, mk_ref, mv_ref, o_ref):
    nk = N_MEM * MEM_HEADS
    row_h = lax.broadcasted_iota(jnp.int32, (MEM_ROWS, nk), 0) // DEC_SEQ
    key_h = lax.broadcasted_iota(jnp.int32, (MEM_ROWS, nk), 1) % MEM_HEADS
    own = row_h == key_h
    for b in range(SAMPLE_BB):
        k2 = _bf16(mk_ref[b].reshape(nk, MEM_HEAD_DIM))
        v2 = _bf16(mv_ref[b].reshape(nk, MEM_HEAD_DIM))
        s = jnp.where(own, _dot_nt(_bf16(q_ref[b]), k2), -jnp.inf)
        m = jnp.max(s, axis=1, keepdims=True)
        p = jnp.exp(s - m)
        den = jnp.sum(p, axis=1, keepdims=True)
        o_ref[b] = _dot(_bf16(p), v2) / den


def _mem_attn_sample(qm, mk, mv):
    nb = mk.shape[0]
    bb = SAMPLE_BB
    q = qm.reshape(nb, DEC_SEQ, MEM_HEADS, MEM_HEAD_DIM).transpose(0, 2, 1, 3).reshape(nb, MEM_ROWS, MEM_HEAD_DIM)
    rows = pl.BlockSpec((bb, MEM_ROWS, MEM_HEAD_DIM), lambda i: (i, 0, 0))
    kv = pl.BlockSpec((bb, N_MEM, MEM_HEADS, MEM_HEAD_DIM), lambda i: (i, 0, 0, 0))
    o = pl.pallas_call(
        _mem_attn_sample_kernel,
        grid=(nb // bb,),
        in_specs=[rows, kv, kv],
        out_specs=rows,
        out_shape=jax.ShapeDtypeStruct((nb, MEM_ROWS, MEM_HEAD_DIM), jnp.float32),
        compiler_params=_params(("arbitrary",)),
        name="mem_attn_sample",
    )(q, mk, mv)
    return o.reshape(nb, MEM_HEADS, DEC_SEQ, MEM_HEAD_DIM).transpose(0, 2, 1, 3).reshape(nb * DEC_SEQ, D_MODEL)


def _post_b_sample_kernel(o_ref, h1_ref, wo_ref, g2_ref, b2_ref, wrhl_ref, br_ref, tri_ref,
                          h2t_ref, rt_ref, cnt_ref):
    h2 = _layer_norm(ALPHA * h1_ref[...] + _dot(_bf16(o_ref[...]), wo_ref[...]), g2_ref[...], b2_ref[...])
    _store_token_tiles(h2t_ref, h2)
    route, carry = _route(h2, wrhl_ref, br_ref, tri_ref, jnp.zeros((1, LANES), jnp.float32))
    rt_ref[...] = route
    cnt_ref[...] = jnp.broadcast_to(carry, cnt_ref.shape)


def _post_b_sample(o, h1, wo_b, g2, b2, wrhl, br, tri):
    n = h1.shape[0]
    args = (o, h1, wo_b, g2, b2, wrhl, br, tri)
    full = lambda a: pl.BlockSpec(a.shape, lambda i: (0,) * a.ndim)
    return pl.pallas_call(
        _post_b_sample_kernel,
        grid=(1,),
        in_specs=[full(a) for a in args],
        out_specs=[pl.BlockSpec((n * ROW_CHUNKS, LANES), lambda i: (0, 0)),
                   pl.BlockSpec((n, LANES), lambda i: (0, 0)),
                   pl.BlockSpec((SUBLANES, LANES), lambda i: (0, 0))],
        out_shape=[jax.ShapeDtypeStruct((n * ROW_CHUNKS, LANES), jnp.float32),
                   jax.ShapeDtypeStruct((n, LANES), jnp.float32),
                   jax.ShapeDtypeStruct((SUBLANES, LANES), jnp.float32)],
        compiler_params=_params(("arbitrary",)),
        name="post_b_sample",
    )(*args)


def _row_gather_copy(src_hbm, idx, dst, dst_row, sem):
    s0 = pl.multiple_of(idx * ROW_CHUNKS, ROW_CHUNKS)
    d0 = pl.multiple_of(dst_row * ROW_CHUNKS, ROW_CHUNKS)
    return pltpu.make_async_copy(src_hbm.at[pl.ds(s0, ROW_CHUNKS), :], dst.at[pl.ds(d0, ROW_CHUNKS), :], sem)


def _dispatch_kernel(pos_ref, h2t_ref, xs_hbm, sem):
    def body(r, c):
        src = h2t_ref.at[pl.ds(pl.multiple_of(r * ROW_CHUNKS, ROW_CHUNKS), ROW_CHUNKS), :]
        for k in range(2):
            d0 = pl.multiple_of(pos_ref[0, 0, k * TM_COMB + r] * ROW_CHUNKS, ROW_CHUNKS)
            pltpu.make_async_copy(src, xs_hbm.at[pl.ds(d0, ROW_CHUNKS), :], sem.at[0]).start(priority=k)
        return c
    lax.fori_loop(0, TM_COMB, body, 0, unroll=8)
    for _ in range(2):
        pltpu.make_async_copy(h2t_ref, xs_hbm.at[pl.ds(0, TM_COMB * ROW_CHUNKS), :], sem.at[0]).wait()


def _dispatch(pos3, h2t):
    nt = N_ALL // TM_COMB
    return pl.pallas_call(
        _dispatch_kernel,
        grid=(nt,),
        in_specs=[pl.BlockSpec((1, 1, 2 * TM_COMB), lambda i: (i, 0, 0), memory_space=pltpu.SMEM),
                  pl.BlockSpec((TM_COMB * ROW_CHUNKS, LANES), lambda i: (i, 0))],
        out_specs=pl.BlockSpec(memory_space=pl.ANY),
        out_shape=jax.ShapeDtypeStruct((N_ASSIGN * ROW_CHUNKS, LANES), jnp.float32),
        scratch_shapes=[pltpu.SemaphoreType.DMA((1,))],
        compiler_params=_params(("arbitrary",)),
        name="moe_dispatch",
    )(pos3, h2t)


def _moe_ffn_kernel(it_ref, ie_ref, lo_ref, hi_ref, x_ref, wg_ref, wu_ref, wd_ref, y_ref, wgb, wub, wdb, cur_e):
    i = pl.program_id(0)
    lo = lo_ref[i]
    hi = hi_ref[i]
    e = ie_ref[i]

    @pl.when(i == 0)
    def _():
        cur_e[0] = -1

    @pl.when((hi > lo) & (cur_e[0] != e))
    def _():
        wgb[...] = _bf16(wg_ref[0])
        wub[...] = _bf16(wu_ref[0])
        wdb[...] = _bf16(wd_ref[0])
        cur_e[0] = e

    def ffn():
        x = _bf16(_load_token_tiles(x_ref, 0, TM_MOE))
        hg = _dot(x, wgb[...])
        hu = _dot(x, wub[...])
        h = hg / (1.0 + jnp.exp(-hg)) * hu
        return _dot(_bf16(h), wdb[...])

    def rows_mask():
        row = lax.broadcasted_iota(jnp.int32, (TM_MOE, LANES), 0)
        return (row >= lo) & (row < hi)

    @pl.when((hi > lo) & (lo == 0))
    def _():
        y = ffn()
        mask = rows_mask()
        for c in range(ROW_CHUNKS):
            y_ref[pl.ds(c, TM_MOE, stride=ROW_CHUNKS), :] = jnp.where(mask, y[:, c * LANES:(c + 1) * LANES], 0.0)

    @pl.when((hi > lo) & (lo > 0))
    def _():
        y = ffn()
        mask = rows_mask()
        for c in range(ROW_CHUNKS):
            sl = pl.ds(c, TM_MOE, stride=ROW_CHUNKS)
            y_ref[sl, :] = jnp.where(mask, y[:, c * LANES:(c + 1) * LANES], y_ref[sl, :])


def _moe_ffn(item_tile, item_expert, item_lo, item_hi, x_sorted, w_gate, w_up, w_down):
    wspec = lambda shp: pl.BlockSpec((1,) + shp, lambda i, it, ie, lo, hi: (ie[i], 0, 0))
    tile = pl.BlockSpec((TM_MOE * ROW_CHUNKS, LANES), lambda i, it, ie, lo, hi: (it[i], 0))
    grid_spec = pltpu.PrefetchScalarGridSpec(
        num_scalar_prefetch=4,
        grid=(MOE_ITEMS,),
        in_specs=[tile, wspec((D_MODEL, EXPERT_FF)), wspec((D_MODEL, EXPERT_FF)), wspec((EXPERT_FF, D_MODEL))],
        out_specs=tile,
        scratch_shapes=[pltpu.VMEM((D_MODEL, EXPERT_FF), jnp.bfloat16),
                        pltpu.VMEM((D_MODEL, EXPERT_FF), jnp.bfloat16),
                        pltpu.VMEM((EXPERT_FF, D_MODEL), jnp.bfloat16),
                        pltpu.SMEM((1,), jnp.int32)],
    )
    return pl.pallas_call(
        _moe_ffn_kernel,
        grid_spec=grid_spec,
        out_shape=jax.ShapeDtypeStruct((N_ASSIGN * ROW_CHUNKS, LANES), jnp.float32),
        compiler_params=_params(("arbitrary",)),
        name="moe_ffn",
    )(item_tile, item_expert, item_lo, item_hi, x_sorted, w_gate, w_up, w_down)


def _combine_kernel(nt, pos_cur_ref, pos_nxt_ref, yt_hbm, h2t_ref, rt_ref, g3_ref, b3_ref, o_ref, abuf, sem):
    t = pl.program_id(0)
    slot = t % 2
    rows = 2 * TM_COMB

    def issue(pos_ref, s):
        def body(j, c):
            for k in range(2):
                r = 2 * j + k
                _row_gather_copy(yt_hbm, pos_ref[0, 0, r], abuf, s * rows + r, sem.at[s]).start(priority=k)
            return c
        lax.fori_loop(0, rows // 2, body, 0, unroll=4)

    @pl.when(t == 0)
    def _():
        issue(pos_cur_ref, 0)

    @pl.when(t + 1 < nt)
    def _():
        issue(pos_nxt_ref, 1 - slot)

    base = pl.multiple_of(slot * (rows * ROW_CHUNKS), rows * ROW_CHUNKS)
    pltpu.make_async_copy(yt_hbm.at[pl.ds(0, rows * ROW_CHUNKS), :],
                          abuf.at[pl.ds(base, rows * ROW_CHUNKS), :], sem.at[slot]).wait()
    ya = _load_token_tiles(abuf, base, TM_COMB)
    yb = _load_token_tiles(abuf, base + TM_COMB * ROW_CHUNKS, TM_COMB)
    rt = rt_ref[...]
    ff = rt[:, 2:3] * ya + rt[:, 3:4] * yb
    h2 = _load_token_tiles(h2t_ref, 0, TM_COMB)
    o_ref[...] = _layer_norm(ALPHA * h2 + ff, g3_ref[...], b3_ref[...])


def _combine(pos3, yt, h2t, rt, g3, b3, tile0, n_tiles):
    last = tile0 + n_tiles - 1
    smem_pos = lambda f: pl.BlockSpec((1, 1, 2 * TM_COMB), f, memory_space=pltpu.SMEM)
    full = lambda a: pl.BlockSpec(a.shape, lambda i: (0,) * a.ndim)
    return pl.pallas_call(
        functools.partial(_combine_kernel, n_tiles),
        grid=(n_tiles,),
        in_specs=[smem_pos(lambda i: (tile0 + i, 0, 0)),
                  smem_pos(lambda i: (jnp.minimum(tile0 + i + 1, last), 0, 0)),
                  pl.BlockSpec(memory_space=pl.ANY),
                  pl.BlockSpec((TM_COMB * ROW_CHUNKS, LANES), lambda i: (tile0 + i, 0)),
                  pl.BlockSpec((TM_COMB, LANES), lambda i: (tile0 + i, 0)),
                  full(g3), full(b3)],
        out_specs=pl.BlockSpec((TM_COMB, D_MODEL), lambda i: (i, 0)),
        out_shape=jax.ShapeDtypeStruct((n_tiles * TM_COMB, D_MODEL), jnp.float32),
        scratch_shapes=[pltpu.VMEM((2 * 2 * TM_COMB * ROW_CHUNKS, LANES), jnp.float32),
                        pltpu.SemaphoreType.DMA((2,))],
        compiler_params=_params(("arbitrary",)),
        name="moe_combine",
    )(pos3, pos3, yt, h2t, rt, g3, b3)


POS_TILES = 3


def _positions_kernel(rt_ref, starts_ref, pos_ref):
    lane = lax.broadcasted_iota(jnp.int32, (TM_COMB, LANES), 1)
    lane_f = lane.astype(jnp.float32)
    starts = starts_ref[0:1, :]
    for j in range(POS_TILES):
        rt = rt_ref[j * TM_COMB:(j + 1) * TM_COMB, :]
        cols = []
        for k in range(2):
            seg = jnp.sum(jnp.where(lane_f == rt[:, k:k + 1] + N_GROUPS, starts, 0.0), axis=1, keepdims=True)
            cols.append(seg + rt[:, 4 + k:5 + k])
        packed = jnp.where(lane == 0, cols[0], jnp.where(lane == 1, cols[1], 0.0))
        rows = packed.T
        pos_ref[j] = jnp.concatenate([rows[0:1, :], rows[1:2, :]], axis=1).astype(jnp.int32)


def _positions(rt, starts_row):
    nt = N_ALL // TM_COMB
    assert nt % POS_TILES == 0
    return pl.pallas_call(
        _positions_kernel,
        grid=(nt // POS_TILES,),
        in_specs=[pl.BlockSpec((POS_TILES * TM_COMB, LANES), lambda i: (i, 0)),
                  pl.BlockSpec((SUBLANES, LANES), lambda i: (0, 0))],
        out_specs=pl.BlockSpec((POS_TILES, 1, 2 * TM_COMB), lambda i: (i, 0, 0)),
        out_shape=jax.ShapeDtypeStruct((nt, 1, 2 * TM_COMB), jnp.int32),
        compiler_params=_params(("arbitrary",)),
        name="moe_positions",
    )(rt, starts_row)


def _routing_plan(rt, cnt):
    i32 = jnp.int32
    counts_f = cnt[0, N_GROUPS:N_GROUPS + N_EXPERTS]
    starts_f = jnp.cumsum(counts_f) - counts_f
    starts_row = jnp.broadcast_to(
        jnp.pad(starts_f, (N_GROUPS, LANES - N_GROUPS - N_EXPERTS))[None, :], (SUBLANES, LANES))
    pos3 = _positions(rt, starts_row)
    starts = starts_f.astype(i32)
    tiles = jnp.arange(MOE_TILES, dtype=i32) * TM_MOE
    rank_t = jnp.arange(MOE_TILES, dtype=i32) + jnp.sum((starts[None, :] < tiles[:, None]).astype(i32), axis=1)
    rank_s = jnp.arange(N_EXPERTS, dtype=i32) + jnp.sum((tiles[None, :] <= starts[:, None]).astype(i32), axis=1)
    vals = jnp.concatenate([tiles, starts])
    ranks = jnp.concatenate([rank_t, rank_s])
    slot = jnp.arange(MOE_ITEMS, dtype=i32)
    lo = jnp.sum(jnp.where(ranks[None, :] == slot[:, None], vals[None, :], 0), axis=1)
    hi = jnp.concatenate([lo[1:], jnp.full((1,), N_ASSIGN, i32)])
    item_tile = jnp.minimum(lo // TM_MOE, MOE_TILES - 1)
    item_expert = jnp.clip(jnp.sum((starts[None, :] <= lo[:, None]).astype(i32), axis=1) - 1, 0, N_EXPERTS - 1)
    base = item_tile * TM_MOE
    return item_tile, item_expert, lo - base, hi - base, pos3


def kernel(x_prompt, x_sample, mem_prompt, cache_swa_k, cache_swa_v, cache_conv, cache_mem_k, cache_mem_v,
           w_in, sinks, conv_w, w_mix_out, ln1_g, ln1_b, w_q_mem, w_k_mem, w_v_mem, w_o_mem, ln2_g, ln2_b,
           w_router_group, b_router_group, w_router_expert, b_router_expert, w_gate, w_up, w_down,
           ln3_g, ln3_b):
    f32 = jnp.float32
    row = lambda a: a.reshape(1, -1).astype(f32)
    w_in_b, wmix_b, wq_b, wk_b, wv_b, wo_b = (_bf16(w) for w in (w_in, w_mix_out, w_q_mem, w_k_mem, w_v_mem, w_o_mem))
    g1, b1, g2, b2, g3, b3 = (row(a) for a in (ln1_g, ln1_b, ln2_g, ln2_b, ln3_g, ln3_b))
    pad = LANES - N_GROUPS - N_EXPERTS
    wr = jnp.concatenate([w_router_group, w_router_expert, jnp.zeros((D_MODEL, pad), f32)], axis=1)
    wrh = _bf16(wr)
    wrhl = jnp.concatenate([wrh, _bf16(wr - wrh.astype(f32))], axis=1)
    br = jnp.concatenate([b_router_group, b_router_expert, jnp.zeros((pad,), f32)]).reshape(1, LANES)

    xs = x_sample.reshape(N_SAMPLE, D_MODEL)
    tab_s = jnp.tile(_rope_table(PAST_LEN + jnp.arange(DEC_SEQ)), (1, DEC_BATCH))
    c0 = jnp.repeat(cache_conv[:, 0], DEC_SEQ, axis=0)
    c1 = jnp.repeat(cache_conv[:, 1], DEC_SEQ, axis=0)
    q_s, k_s, v_s, conv_s, u_s = _proj_sample(xs, w_in_b, tab_s, conv_w, c0, c1)
    attn_s, swa_k_s, swa_v_s = _swa_sample(sinks, q_s, k_s, v_s, cache_swa_k, cache_swa_v)
    h1_s, qm_s = _post_a_sample(attn_s, conv_s, xs, wmix_b, g1, b1, wq_b)
    o_s = _mem_attn_sample(qm_s, cache_mem_k, cache_mem_v)
    tri = _bf16(jnp.tril(jnp.ones((TM_POST, TM_POST), f32), -1))
    h2t_s, rt_s, cnt_s = _post_b_sample(o_s, h1_s, wo_b, g2, b2, wrhl, br, tri)

    xp = x_prompt.reshape(N_PROMPT, D_MODEL)
    tab_p = _rope_table(jnp.arange(N_PROMPT))
    q_p, kx_p, vx_p, conv_p, k_tail, v_tail, u_tail = _proj_prompt(xp, w_in_b, tab_p, conv_w)
    attn_p = _swa_prompt(sinks, q_p, kx_p, vx_p)
    mk, mv, wqk, wvo = _mem_kv(mem_prompt.reshape(N_MEM, D_MODEL), wk_b, wv_b, wq_b, wo_b)
    h2t, rt, cnt = _post_prompt(attn_p, conv_p, xp, wmix_b, g1, b1, wqk, wvo, g2, b2,
                                wrhl, br, tri, h2t_s, rt_s, cnt_s)

    item_tile, item_expert, item_lo, item_hi, pos3 = _routing_plan(rt, cnt)
    x_sorted = _dispatch(pos3, h2t)
    yt = _moe_ffn(item_tile, item_expert, item_lo, item_hi, x_sorted, w_gate, w_up, w_down)
    y_p = _combine(pos3, yt, h2t, rt, g3, b3, 0, N_PROMPT // TM_COMB)
    y_s = _combine(pos3, yt, h2t, rt, g3, b3, N_PROMPT // TM_COMB, N_SAMPLE // TM_COMB)

    return (y_p.reshape(1, SEQ, D_MODEL),
            y_s.reshape(DEC_BATCH, DEC_SEQ, D_MODEL),
            k_tail.reshape(1, WINDOW, N_KV_HEADS, HEAD_DIM),
            v_tail.reshape(1, WINDOW, N_KV_HEADS, HEAD_DIM),
            u_tail[SUBLANES - (CONV_K - 1):].reshape(1, CONV_K - 1, CONV_CH),
            mk.reshape(1, N_MEM, MEM_HEADS, MEM_HEAD_DIM),
            mv.reshape(1, N_MEM, MEM_HEADS, MEM_HEAD_DIM),
            swa_k_s.reshape(DEC_BATCH, WINDOW, N_KV_HEADS, HEAD_DIM),
            swa_v_s.reshape(DEC_BATCH, WINDOW, N_KV_HEADS, HEAD_DIM),
            u_s.reshape(DEC_BATCH, DEC_SEQ, CONV_CH)[:, DEC_SEQ - (CONV_K - 1):])
```

```python
import functools

import jax
import jax.numpy as jnp
from jax import lax
from jax.experimental import pallas as pl
from jax.experimental.pallas import tpu as pltpu

D_MODEL = 1024
SEQ = 16384
DEC_BATCH = 128
DEC_SEQ = 4
PAST_LEN = 16384
ATTN_WIDTH = 512
CONV_CH = 512
HEAD_DIM = 64
N_HEADS = 8
N_KV_HEADS = 2
KV_WIDTH = 128
WINDOW = 128
ROPE_THETA = 500000.0
ROPE_DIM = 16
CONV_K = 3
Q_END = ATTN_WIDTH
K_END = Q_END + KV_WIDTH
V_END = K_END + KV_WIDTH
B_END = V_END + CONV_CH
C_END = B_END + CONV_CH
IN_WIDTH = C_END + CONV_CH
N_MEM = 256
MEM_HEADS = 4
MEM_HEAD_DIM = 256
N_GROUPS = 4
EXPERTS_PER_GROUP = 8
N_EXPERTS = 32
EXPERT_FF = 256
ALPHA = 2.0 ** 0.25
LN_EPS = 1e-5

LANES = 128
SUBLANES = 8
ROW_CHUNKS = D_MODEL // LANES
VMEM_LIMIT = 56 * 1024 * 1024

N_PROMPT = SEQ
N_SAMPLE = DEC_BATCH * DEC_SEQ
N_ALL = N_PROMPT + N_SAMPLE
TM_PROJ = 1024
TM_POST = 512
TM_MOE = 512
TM_COMB = 512
N_ASSIGN = 2 * N_ALL
MOE_TILES = N_ASSIGN // TM_MOE
MOE_ITEMS = MOE_TILES + N_EXPERTS
SAMPLE_BB = 4
SWA_BB = 8

assert ROW_CHUNKS == SUBLANES
assert N_SAMPLE == TM_POST
assert N_ASSIGN % TM_MOE == 0 and N_ALL % TM_COMB == 0


def _params(sem, vmem=VMEM_LIMIT):
    return pltpu.CompilerParams(dimension_semantics=sem, vmem_limit_bytes=vmem)


def _bf16(x):
    return x.astype(jnp.bfloat16)


def _dot(a, b):
    return jnp.dot(a, b, preferred_element_type=jnp.float32)


def _dot_nt(a, b):
    return lax.dot_general(a, b, (((1,), (1,)), ((), ())), preferred_element_type=jnp.float32)


def _layer_norm(x, g, b):
    mu = jnp.mean(x, axis=-1, keepdims=True)
    xc = x - mu
    var = jnp.mean(xc * xc, axis=-1, keepdims=True)
    return xc * lax.rsqrt(var + LN_EPS) * g + b


def _rope(x, cos_t, sin_t):
    lane = lax.broadcasted_iota(jnp.int32, x.shape, 1) % HEAD_DIM
    half = ROPE_DIM // 2
    partner = jnp.where(lane < half, pltpu.roll(x, LANES - half, axis=1), pltpu.roll(x, half, axis=1))
    return x * cos_t + partner * sin_t


def _head_slabs(x):
    lane = lax.broadcasted_iota(jnp.int32, x.shape, 1)
    lo = lane < HEAD_DIM
    sw = pltpu.roll(x, HEAD_DIM, axis=1)
    zero = jnp.zeros_like(x)
    slabs = [jnp.where(lo, x, zero), jnp.where(lo, zero, sw), jnp.where(lo, sw, zero), jnp.where(lo, zero, x)]
    return _bf16(jnp.concatenate(slabs, axis=1))


def _store_token_tiles(ref, val):
    rows = val.shape[0]
    for c in range(ROW_CHUNKS):
        ref[pl.ds(c, rows, stride=ROW_CHUNKS), :] = val[:, c * LANES:(c + 1) * LANES]


def _load_token_tiles(ref, base, rows):
    return jnp.concatenate(
        [ref[pl.ds(base + c, rows, stride=ROW_CHUNKS), :] for c in range(ROW_CHUNKS)], axis=1)


ROPE_ONE = 3 * (ROPE_DIM // 2)
ROPE_ROWS = 32


def _rope_patterns(tab):
    half = ROPE_DIM // 2
    m = lax.broadcasted_iota(jnp.int32, tab.shape, 1) % HEAD_DIM
    idx_c = jnp.where(m < ROPE_DIM, m % half, ROPE_ONE)
    idx_s = jnp.where(m < half, 2 * half + m, jnp.where(m < ROPE_DIM, m, ROPE_ONE + 1))
    return jnp.take_along_axis(tab, idx_c, axis=1), jnp.take_along_axis(tab, idx_s, axis=1)


def _proj_common(x_ref, w_ref, tab_ref):
    xb = _bf16(x_ref[...])
    tab = tab_ref[...]
    pad = jnp.zeros((LANES - tab.shape[0], tab.shape[1]), jnp.float32)
    cos_t, sin_t = _rope_patterns(jnp.concatenate([tab, pad], axis=0).T)
    q = _dot(xb, w_ref[:, 0:Q_END])
    q_rot = jnp.concatenate(
        [_rope(q[:, p * LANES:(p + 1) * LANES], cos_t, sin_t) for p in range(ATTN_WIDTH // LANES)], axis=1)
    q_out = _bf16(q_rot * (HEAD_DIM ** -0.5))
    kv = _dot(xb, w_ref[:, Q_END:V_END])
    k = _rope(kv[:, 0:KV_WIDTH], cos_t, sin_t)
    v = kv[:, KV_WIDTH:]
    bg = _dot(xb, w_ref[:, V_END:B_END])
    u = _dot(xb, w_ref[:, B_END:C_END]) * _dot(xb, w_ref[:, C_END:IN_WIDTH])
    return q_out, k, v, bg, u


def _conv3(bg, u, u1, u2, cw_ref):
    cw = cw_ref[...]
    return bg * (cw[0:1, :] * u2 + cw[1:2, :] * u1 + cw[2:3, :] * u)


def _proj_prompt_kernel(x_ref, w_ref, tab_ref, cw_ref,
                        q_ref, kx_ref, vx_ref, conv_ref, ktail_ref, vtail_ref, utail_ref, carry_ref):
    @pl.when(pl.program_id(0) == 0)
    def _():
        carry_ref[...] = jnp.zeros_like(carry_ref)

    q_out, k, v, bg, u = _proj_common(x_ref, w_ref, tab_ref)
    tm = u.shape[0]
    ext = jnp.concatenate([carry_ref[...], u], axis=0)
    u1 = pltpu.roll(ext, 1, axis=0)[SUBLANES:SUBLANES + tm]
    u2 = pltpu.roll(ext, 2, axis=0)[SUBLANES:SUBLANES + tm]
    q_ref[...] = q_out
    kx_ref[...] = _head_slabs(k)
    vx_ref[...] = _head_slabs(v)
    conv_ref[...] = _bf16(_conv3(bg, u, u1, u2, cw_ref))
    ktail_ref[...] = k[tm - WINDOW:tm]
    vtail_ref[...] = v[tm - WINDOW:tm]
    utail_ref[...] = u[tm - SUBLANES:tm]
    carry_ref[...] = u[tm - SUBLANES:tm]


def _proj_sample_kernel(x_ref, w_ref, tab_ref, cw_ref, c0_ref, c1_ref,
                        q_ref, k_ref, v_ref, conv_ref, u_ref):
    q_out, k, v, bg, u = _proj_common(x_ref, w_ref, tab_ref)
    t = lax.broadcasted_iota(jnp.int32, u.shape, 0) % DEC_SEQ
    c0 = c0_ref[...]
    c1 = c1_ref[...]
    u1 = jnp.where(t >= 1, pltpu.roll(u, 1, axis=0), c1)
    u2 = jnp.where(t >= 2, pltpu.roll(u, 2, axis=0), jnp.where(t == 1, c1, c0))
    q_ref[...] = q_out.astype(jnp.float32)
    k_ref[...] = k
    v_ref[...] = v
    conv_ref[...] = _bf16(_conv3(bg, u, u1, u2, cw_ref))
    u_ref[...] = u


def _rope_table(pos):
    half = ROPE_DIM // 2
    inv = ROPE_THETA ** (-jnp.arange(0, ROPE_DIM, 2, dtype=jnp.float32) / ROPE_DIM)
    ang = pos.astype(jnp.float32)[None, :] * inv[:, None]
    cos, sin = jnp.cos(ang), jnp.sin(ang)
    n = pos.shape[0]
    assert ROPE_ONE == 3 * half
    return jnp.concatenate([cos, sin, -sin, jnp.ones((1, n), jnp.float32),
                            jnp.zeros((ROPE_ROWS - ROPE_ONE - 1, n), jnp.float32)], axis=0)


def _proj_prompt(x, w_in_b, tab, conv_w):
    n = x.shape[0]
    tm = TM_PROJ
    row = lambda w: pl.BlockSpec((tm, w), lambda i: (i, 0))
    full = lambda a: pl.BlockSpec(a.shape, lambda i: (0,) * a.ndim)
    const = lambda r, w: pl.BlockSpec((r, w), lambda i: (0, 0))
    return pl.pallas_call(
        _proj_prompt_kernel,
        grid=(n // tm,),
        in_specs=[row(D_MODEL), full(w_in_b), pl.BlockSpec((ROPE_ROWS, tm), lambda i: (0, i)), full(conv_w)],
        out_specs=[row(ATTN_WIDTH), row(4 * LANES), row(4 * LANES), row(CONV_CH),
                   const(WINDOW, KV_WIDTH), const(WINDOW, KV_WIDTH), const(SUBLANES, CONV_CH)],
        out_shape=[jax.ShapeDtypeStruct((n, ATTN_WIDTH), jnp.bfloat16),
                   jax.ShapeDtypeStruct((n, 4 * LANES), jnp.bfloat16),
                   jax.ShapeDtypeStruct((n, 4 * LANES), jnp.bfloat16),
                   jax.ShapeDtypeStruct((n, CONV_CH), jnp.bfloat16),
                   jax.ShapeDtypeStruct((WINDOW, KV_WIDTH), jnp.float32),
                   jax.ShapeDtypeStruct((WINDOW, KV_WIDTH), jnp.float32),
                   jax.ShapeDtypeStruct((SUBLANES, CONV_CH), jnp.float32)],
        scratch_shapes=[pltpu.VMEM((SUBLANES, CONV_CH), jnp.float32)],
        compiler_params=_params(("arbitrary",)),
        name="proj_prompt",
    )(x, w_in_b, tab, conv_w)


def _proj_sample(x, w_in_b, tab, conv_w, c0, c1):
    n = x.shape[0]
    full = lambda a: pl.BlockSpec(a.shape, lambda i: (0,) * a.ndim)
    out = lambda w, dt: jax.ShapeDtypeStruct((n, w), dt)
    blk = lambda w: pl.BlockSpec((n, w), lambda i: (0, 0))
    return pl.pallas_call(
        _proj_sample_kernel,
        grid=(1,),
        in_specs=[full(x), full(w_in_b), full(tab), full(conv_w), full(c0), full(c1)],
        out_specs=[blk(ATTN_WIDTH), blk(KV_WIDTH), blk(KV_WIDTH), blk(CONV_CH), blk(CONV_CH)],
        out_shape=[out(ATTN_WIDTH, jnp.float32), out(KV_WIDTH, jnp.float32), out(KV_WIDTH, jnp.float32),
                   out(CONV_CH, jnp.bfloat16), out(CONV_CH, jnp.float32)],
        compiler_params=_params(("arbitrary",)),
        name="proj_sample",
    )(x, w_in_b, tab, conv_w, c0, c1)


def _sink_softmax_pv(s, valid, sink, vx):
    s = jnp.where(valid, s, -jnp.inf)
    m = jnp.maximum(jnp.max(s, axis=1, keepdims=True), sink)
    p = jnp.exp(s - m)
    den = jnp.sum(p, axis=1, keepdims=True) + jnp.exp(sink - m)
    return _dot(_bf16(p), vx) / den


SWA_QB = 4


def _swa_tile(step, sinks_ref, q_ref, kc_ref, kp_ref, vc_ref, vp_ref, store):
    kall = jnp.concatenate([kp_ref[...], kc_ref[...]], axis=0)
    vall = jnp.concatenate([vp_ref[...], vc_ref[...]], axis=0)
    i = lax.broadcasted_iota(jnp.int32, (WINDOW, 2 * WINDOW), 0)
    j = lax.broadcasted_iota(jnp.int32, (WINDOW, 2 * WINDOW), 1)
    band = (j > i) & (j <= i + WINDOW)
    for sb in range(SWA_QB):
        rows = slice(sb * WINDOW, (sb + 1) * WINDOW)
        kcat = kall[sb * WINDOW:(sb + 2) * WINDOW]
        vcat = vall[sb * WINDOW:(sb + 2) * WINDOW]
        valid = band & ((step > 0) | (j >= WINDOW)) if sb == 0 else band
        for p in range(N_HEADS // 2):
            qs = q_ref[rows, p * LANES:(p + 1) * LANES]
            acc = None
            for e in range(2):
                hd = 2 * p + e
                slab = 2 * (hd // (N_HEADS // N_KV_HEADS)) + e
                kx = kcat[:, slab * LANES:(slab + 1) * LANES]
                vx = vcat[:, slab * LANES:(slab + 1) * LANES]
                o = _sink_softmax_pv(_dot_nt(qs, kx), valid, sinks_ref[hd], vx)
                acc = o if acc is None else acc + o
            store(rows, slice(p * LANES, (p + 1) * LANES), _bf16(acc))


SWA_ROWS = N_HEADS * DEC_SEQ
NEW_ROWS = 2 * SUBLANES


def _swa_sample_kernel(q_ref, sink_ref, kn_ref, vn_ref, kt_ref, vt_ref, o_ref, okt_ref, ovt_ref):
    t = lax.broadcasted_iota(jnp.int32, (SWA_ROWS, WINDOW), 0) % DEC_SEQ
    j = lax.broadcasted_iota(jnp.int32, (SWA_ROWS, WINDOW), 1)
    valid_c = j > t
    valid_n = (lax.broadcasted_iota(jnp.int32, (SWA_ROWS, NEW_ROWS), 1)
               <= lax.broadcasted_iota(jnp.int32, (SWA_ROWS, NEW_ROWS), 0) % DEC_SEQ)
    lane = lax.broadcasted_iota(jnp.int32, (KV_WIDTH, WINDOW), 1)
    sink = sink_ref[:, 0:1]
    shift = WINDOW - DEC_SEQ
    zrows = jnp.zeros((KV_WIDTH - NEW_ROWS, KV_WIDTH), jnp.float32)
    for b in range(SWA_BB):
        q = _bf16(q_ref[b])
        kt, vt = kt_ref[b], vt_ref[b]
        kn, vn = kn_ref[b], vn_ref[b]
        s_c = jnp.where(valid_c, _dot(q, _bf16(kt)), -jnp.inf)
        s_n = jnp.where(valid_n, _dot_nt(q, _bf16(kn)), -jnp.inf)
        m = jnp.maximum(jnp.maximum(jnp.max(s_c, axis=1, keepdims=True), jnp.max(s_n, axis=1, keepdims=True)), sink)
        p_c = jnp.exp(s_c - m)
        p_n = jnp.exp(s_n - m)
        den = jnp.sum(p_c, axis=1, keepdims=True) + jnp.sum(p_n, axis=1, keepdims=True) + jnp.exp(sink - m)
        o_ref[b] = (_dot_nt(_bf16(p_c), _bf16(vt)) + _dot(_bf16(p_n), _bf16(vn))) / den
        for old, new, dst in ((kt, kn, okt_ref), (vt, vn, ovt_ref)):
            new_cols = pltpu.roll(jnp.concatenate([new, zrows], axis=0).T, shift, axis=1)
            dst[b] = jnp.where(lane >= shift, new_cols, pltpu.roll(old, shift, axis=1))


def _swa_sample(sinks, q, kn, vn, cache_k, cache_v):
    nb = cache_k.shape[0]
    bb = SWA_BB
    groups = N_HEADS // N_KV_HEADS
    qh = q.reshape(nb, DEC_SEQ, N_KV_HEADS, groups, HEAD_DIM).transpose(0, 2, 3, 1, 4)
    qh = qh.reshape(nb, N_KV_HEADS, groups * DEC_SEQ, HEAD_DIM)
    zeros = jnp.zeros_like(qh[:, 0])
    qbd = jnp.concatenate([jnp.concatenate([qh[:, 0], zeros], axis=-1),
                           jnp.concatenate([zeros, qh[:, 1]], axis=-1)], axis=1)
    sink_col = jnp.broadcast_to(jnp.repeat(sinks, DEC_SEQ).reshape(SWA_ROWS, 1), (SWA_ROWS, LANES))
    pad8 = lambda a: jnp.pad(a.reshape(nb, DEC_SEQ, KV_WIDTH), ((0, 0), (0, NEW_ROWS - DEC_SEQ), (0, 0)))
    to_t = lambda c: c.transpose(0, 2, 3, 1).reshape(nb, KV_WIDTH, WINDOW)
    blk = lambda r, w: pl.BlockSpec((bb, r, w), lambda i: (i, 0, 0))
    o, okt, ovt = pl.pallas_call(
        _swa_sample_kernel,
        grid=(nb // bb,),
        in_specs=[blk(SWA_ROWS, KV_WIDTH), pl.BlockSpec((SWA_ROWS, LANES), lambda i: (0, 0)),
                  blk(NEW_ROWS, KV_WIDTH), blk(NEW_ROWS, KV_WIDTH), blk(KV_WIDTH, WINDOW), blk(KV_WIDTH, WINDOW)],
        out_specs=[blk(SWA_ROWS, KV_WIDTH), blk(KV_WIDTH, WINDOW), blk(KV_WIDTH, WINDOW)],
        out_shape=[jax.ShapeDtypeStruct((nb, SWA_ROWS, KV_WIDTH), jnp.float32),
                   jax.ShapeDtypeStruct((nb, KV_WIDTH, WINDOW), jnp.float32),
                   jax.ShapeDtypeStruct((nb, KV_WIDTH, WINDOW), jnp.float32)],
        compiler_params=_params(("arbitrary",)),
        name="swa_sample",
    )(qbd, sink_col, pad8(kn), pad8(vn), to_t(cache_k), to_t(cache_v))
    o = o.reshape(nb, N_KV_HEADS, groups, DEC_SEQ, N_KV_HEADS, HEAD_DIM)
    attn = jnp.stack([o[:, h, :, :, h, :] for h in range(N_KV_HEADS)], axis=1)
    attn = attn.transpose(0, 3, 1, 2, 4).reshape(nb * DEC_SEQ, ATTN_WIDTH)
    from_t = lambda c: c.reshape(nb, N_KV_HEADS, HEAD_DIM, WINDOW).transpose(0, 3, 1, 2)
    return attn, from_t(okt), from_t(ovt)


def _mem_kv_kernel(mem_ref, wk_ref, wv_ref, wq_ref, wo_ref, mk_ref, mv_ref, wqk_ref, wvo_ref):
    mb = _bf16(mem_ref[...])
    mk = _dot(mb, wk_ref[...])
    mv = _dot(mb, wv_ref[...])
    mk_ref[...] = mk
    mv_ref[...] = mv
    mkb, mvb = _bf16(mk), _bf16(mv)
    for h in range(MEM_HEADS):
        sl = slice(h * MEM_HEAD_DIM, (h + 1) * MEM_HEAD_DIM)
        keys = slice(h * N_MEM, (h + 1) * N_MEM)
        wqk_ref[:, keys] = _bf16(_dot_nt(wq_ref[:, sl], mkb[:, sl]) * (MEM_HEAD_DIM ** -0.5))
        wvo_ref[keys, :] = _bf16(_dot(mvb[:, sl], wo_ref[sl, :]))


def _mem_kv(mem, wk_b, wv_b, wq_b, wo_b):
    full = lambda a: pl.BlockSpec(a.shape, lambda i: (0,) * a.ndim)
    blk = pl.BlockSpec((N_MEM, D_MODEL), lambda i: (0, 0))
    f32 = jax.ShapeDtypeStruct((N_MEM, D_MODEL), jnp.float32)
    fused = (D_MODEL, MEM_HEADS * N_MEM), (MEM_HEADS * N_MEM, D_MODEL)
    return pl.pallas_call(
        _mem_kv_kernel,
        grid=(1,),
        in_specs=[full(mem), full(wk_b), full(wv_b), full(wq_b), full(wo_b)],
        out_specs=[blk, blk] + [pl.BlockSpec(shp, lambda i: (0, 0)) for shp in fused],
        out_shape=[f32, f32] + [jax.ShapeDtypeStruct(shp, jnp.bfloat16) for shp in fused],
        compiler_params=_params(("arbitrary",)),
        name="mem_kv",
    )(mem, wk_b, wv_b, wq_b, wo_b)


def _mix_ln1(attn_ref, conv_ref, x_ref, wmix_ref, g1_ref, b1_ref):
    mix = _dot(_bf16(attn_ref[...]), wmix_ref[0:ATTN_WIDTH, :]) + _dot(conv_ref[...], wmix_ref[ATTN_WIDTH:, :])
    return _layer_norm(ALPHA * x_ref[...] + mix, g1_ref[...], b1_ref[...])


def _mem_q(h1, wq_ref):
    return _bf16(_dot(_bf16(h1), wq_ref[...]) * (MEM_HEAD_DIM ** -0.5))


def _route(h2, wrhl_ref, br_ref, tri_ref, carry):
    hi = _bf16(h2)
    lo = _bf16(h2 - hi.astype(jnp.float32))
    hh = _dot(hi, wrhl_ref[...])
    logits = hh[:, 0:LANES] + hh[:, LANES:] + _dot(lo, wrhl_ref[:, 0:LANES]) + br_ref[...]
    lane_i = lax.broadcasted_iota(jnp.int32, logits.shape, 1)
    lane = lane_i.astype(jnp.float32)
    big = jnp.float32(LANES)
    is_g = lane_i < N_GROUPS
    gl = jnp.where(is_g, logits, -jnp.inf)
    gmax = jnp.max(gl, axis=1, keepdims=True)
    gidx = jnp.min(jnp.where(is_g & (logits == gmax), lane, big), axis=1, keepdims=True)
    gsum = jnp.sum(jnp.exp(gl - gmax), axis=1, keepdims=True)
    gw = 1.0 / gsum
    eid = lane_i - N_GROUPS
    assert EXPERTS_PER_GROUP == 8
    grp = lax.shift_right_arithmetic(eid, jnp.full_like(eid, 3)).astype(jnp.float32)
    in_e = (lane_i >= N_GROUPS) & (lane_i < N_GROUPS + N_EXPERTS) & (grp == gidx)
    v1 = jnp.max(jnp.where(in_e, logits, -jnp.inf), axis=1, keepdims=True)
    i1 = jnp.min(jnp.where(in_e & (logits == v1), lane, big), axis=1, keepdims=True)
    rest = in_e & (lane != i1)
    v2 = jnp.max(jnp.where(rest, logits, -jnp.inf), axis=1, keepdims=True)
    i2 = jnp.min(jnp.where(rest & (logits == v2), lane, big), axis=1, keepdims=True)
    ex = jnp.exp(v2 - v1)
    den = 1.0 + ex
    w1 = gw / den
    w2 = gw * ex / den
    zero = jnp.zeros_like(logits)
    pick1 = lane == i1
    pick2 = lane == i2
    sel = jnp.where(pick1 | pick2, 1.0, 0.0)
    before = _dot(tri_ref[...], _bf16(sel)) + carry
    rank1 = jnp.sum(jnp.where(pick1, before, zero), axis=1, keepdims=True)
    rank2 = jnp.sum(jnp.where(pick2, before, zero), axis=1, keepdims=True)
    cols = (i1 - N_GROUPS, i2 - N_GROUPS, w1, w2, rank1, rank2)
    route = zero
    for k, col in enumerate(cols):
        route = jnp.where(lane_i == k, col, route)
    return route, carry + jnp.sum(sel, axis=0, keepdims=True)


def _post_tile(attn, conv_ref, x_ref, wmix_ref, g1_ref, b1_ref, wqk_ref, wvo_ref, g2_ref, b2_ref,
               wrhl_ref, br_ref, tri_ref, h2t_ref, rt_ref, cnt_ref, carry_ref):
    mix = _dot(attn, wmix_ref[0:ATTN_WIDTH, :]) + _dot(conv_ref[...], wmix_ref[ATTN_WIDTH:, :])
    h1 = _layer_norm(ALPHA * x_ref[...] + mix, g1_ref[...], b1_ref[...])
    scores = _dot(_bf16(h1), wqk_ref[...])
    probs = []
    for h in range(MEM_HEADS):
        s = scores[:, h * N_MEM:(h + 1) * N_MEM]
        p = jnp.exp(s - jnp.max(s, axis=1, keepdims=True))
        probs.append(_bf16(p / jnp.sum(p, axis=1, keepdims=True)))
    mem_out = _dot(jnp.concatenate(probs, axis=1), wvo_ref[...])
    h2 = _layer_norm(ALPHA * h1 + mem_out, g2_ref[...], b2_ref[...])
    _store_token_tiles(h2t_ref, h2)
    route, carry = _route(h2, wrhl_ref, br_ref, tri_ref, carry_ref[0:1, :])
    rt_ref[...] = route
    carry_ref[...] = jnp.broadcast_to(carry, carry_ref.shape)
    cnt_ref[...] = jnp.broadcast_to(carry, cnt_ref.shape)


def _swa_post_kernel(sinks_ref, q_ref, kc_ref, kp_ref, vc_ref, vp_ref,
                     conv_ref, x_ref, wmix_ref, g1_ref, b1_ref, wqk_ref, wvo_ref, g2_ref, b2_ref,
                     wrhl_ref, br_ref, tri_ref, h2s_ref, rts_ref, cnts_ref,
                     h2t_ref, rt_ref, cnt_ref, carry_ref, attn_s):
    t = pl.program_id(0)
    steps = N_PROMPT // TM_POST
    par = t % 2

    def swa(slot):
        def store(rows, cols, val):
            attn_s[slot, rows, cols] = val
        _swa_tile(t, sinks_ref, q_ref, kc_ref, kp_ref, vc_ref, vp_ref, store)

    def post(slot):
        _post_tile(attn_s[slot], conv_ref, x_ref, wmix_ref, g1_ref, b1_ref, wqk_ref, wvo_ref, g2_ref, b2_ref,
                   wrhl_ref, br_ref, tri_ref, h2t_ref, rt_ref, cnt_ref, carry_ref)

    @pl.when(t == 0)
    def _():
        carry_ref[...] = cnts_ref[...]
        swa(0)

    @pl.when((t >= 1) & (t < steps))
    def _():
        swa(par)
        post(1 - par)

    @pl.when(t == steps)
    def _():
        post(1 - par)

    @pl.when(t == steps + 1)
    def _():
        h2t_ref[...] = h2s_ref[...]
        rt_ref[...] = rts_ref[...]


def _swa_post(sinks, q, kx, vx, conv, x, wmix_b, g1, b1, wqk, wvo, g2, b2, wrhl, br, tri, h2t_s, rt_s, cnt_s):
    n = x.shape[0]
    tm = TM_POST
    assert tm == SWA_QB * WINDOW
    steps = n // tm
    cur = lambda w: pl.BlockSpec((tm, w), lambda i: (jnp.minimum(i, steps - 1), 0))
    prev = lambda w: pl.BlockSpec((WINDOW, w), lambda i: (jnp.clip(SWA_QB * i - 1, 0, n // WINDOW - 1), 0))
    lag = lambda w: pl.BlockSpec((tm, w), lambda i: (jnp.clip(i - 1, 0, steps - 1), 0))
    full = lambda a: pl.BlockSpec(a.shape, lambda i: (0,) * a.ndim)
    weights = (wmix_b, g1, b1, wqk, wvo, g2, b2, wrhl, br, tri, h2t_s, rt_s, cnt_s)
    n_out = n + h2t_s.shape[0] // ROW_CHUNKS
    out_idx = lambda i: (jnp.where(i > steps, steps, jnp.clip(i - 1, 0, steps - 1)), 0)
    return pl.pallas_call(
        _swa_post_kernel,
        grid=(steps + 2,),
        in_specs=([pl.BlockSpec(memory_space=pltpu.SMEM), cur(ATTN_WIDTH),
                   cur(4 * LANES), prev(4 * LANES), cur(4 * LANES), prev(4 * LANES),
                   lag(CONV_CH), lag(D_MODEL)] + [full(a) for a in weights]),
        out_specs=[pl.BlockSpec((tm * ROW_CHUNKS, LANES), out_idx),
                   pl.BlockSpec((tm, LANES), out_idx),
                   pl.BlockSpec((SUBLANES, LANES), lambda i: (0, 0))],
        out_shape=[jax.ShapeDtypeStruct((n_out * ROW_CHUNKS, LANES), jnp.float32),
                   jax.ShapeDtypeStruct((n_out, LANES), jnp.float32),
                   jax.ShapeDtypeStruct((SUBLANES, LANES), jnp.float32)],
        scratch_shapes=[pltpu.VMEM((SUBLANES, LANES), jnp.float32),
                        pltpu.VMEM((2, tm, ATTN_WIDTH), jnp.bfloat16)],
        compiler_params=_params(("arbitrary",)),
        name="swa_post_prompt",
    )(sinks, q, kx, kx, vx, vx, conv, x, *weights)


def _post_a_sample_kernel(attn_ref, conv_ref, x_ref, wmix_ref, g1_ref, b1_ref, wq_ref, h1_ref, qm_ref):
    h1 = _mix_ln1(attn_ref, conv_ref, x_ref, wmix_ref, g1_ref, b1_ref)
    h1_ref[...] = h1
    qm_ref[...] = _mem_q(h1, wq_ref).astype(jnp.float32)


def _post_a_sample(attn, conv, x, wmix_b, g1, b1, wq_b):
    n = x.shape[0]
    args = (attn, conv, x, wmix_b, g1, b1, wq_b)
    full = lambda a: pl.BlockSpec(a.shape, lambda i: (0,) * a.ndim)
    blk = pl.BlockSpec((n, D_MODEL), lambda i: (0, 0))
    return pl.pallas_call(
        _post_a_sample_kernel,
        grid=(1,),
        in_specs=[full(a) for a in args],
        out_specs=[blk, blk],
        out_shape=[jax.ShapeDtypeStruct((n, D_MODEL), jnp.float32),
                   jax.ShapeDtypeStruct((n, D_MODEL), jnp.float32)],
        compiler_params=_params(("arbitrary",)),
        name="post_a_sample",
    )(*args)


MEM_ROWS = MEM_HEADS * DEC_SEQ


def _mem_attn_sample_kernel(q_ref, mk_ref, mv_ref, o_ref):
    nk = N_MEM * MEM_HEADS
    row_h = lax.broadcasted_iota(jnp.int32, (MEM_ROWS, nk), 0) // DEC_SEQ
    key_h = lax.broadcasted_iota(jnp.int32, (MEM_ROWS, nk), 1) % MEM_HEADS
    own = row_h == key_h
    for b in range(SAMPLE_BB):
        k2 = _bf16(mk_ref[b].reshape(nk, MEM_HEAD_DIM))
        v2 = _bf16(mv_ref[b].reshape(nk, MEM_HEAD_DIM))
        s = jnp.where(own, _dot_nt(_bf16(q_ref[b]), k2), -jnp.inf)
        m = jnp.max(s, axis=1, keepdims=True)
        p = jnp.exp(s - m)
        den = jnp.sum(p, axis=1, keepdims=True)
        o_ref[b] = _dot(_bf16(p), v2) / den


def _mem_attn_sample(qm, mk, mv):
    nb = mk.shape[0]
    bb = SAMPLE_BB
    q = qm.reshape(nb, DEC_SEQ, MEM_HEADS, MEM_HEAD_DIM).transpose(0, 2, 1, 3).reshape(nb, MEM_ROWS, MEM_HEAD_DIM)
    rows = pl.BlockSpec((bb, MEM_ROWS, MEM_HEAD_DIM), lambda i: (i, 0, 0))
    kv = pl.BlockSpec((bb, N_MEM, MEM_HEADS, MEM_HEAD_DIM), lambda i: (i, 0, 0, 0))
    o = pl.pallas_call(
        _mem_attn_sample_kernel,
        grid=(nb // bb,),
        in_specs=[rows, kv, kv],
        out_specs=rows,
        out_shape=jax.ShapeDtypeStruct((nb, MEM_ROWS, MEM_HEAD_DIM), jnp.float32),
        compiler_params=_params(("arbitrary",)),
        name="mem_attn_sample",
    )(q, mk, mv)
    return o.reshape(nb, MEM_HEADS, DEC_SEQ, MEM_HEAD_DIM).transpose(0, 2, 1, 3).reshape(nb * DEC_SEQ, D_MODEL)


def _post_b_sample_kernel(o_ref, h1_ref, wo_ref, g2_ref, b2_ref, wrhl_ref, br_ref, tri_ref,
                          h2t_ref, rt_ref, cnt_ref):
    h2 = _layer_norm(ALPHA * h1_ref[...] + _dot(_bf16(o_ref[...]), wo_ref[...]), g2_ref[...], b2_ref[...])
    _store_token_tiles(h2t_ref, h2)
    route, carry = _route(h2, wrhl_ref, br_ref, tri_ref, jnp.zeros((1, LANES), jnp.float32))
    rt_ref[...] = route
    cnt_ref[...] = jnp.broadcast_to(carry, cnt_ref.shape)


def _post_b_sample(o, h1, wo_b, g2, b2, wrhl, br, tri):
    n = h1.shape[0]
    args = (o, h1, wo_b, g2, b2, wrhl, br, tri)
    full = lambda a: pl.BlockSpec(a.shape, lambda i: (0,) * a.ndim)
    return pl.pallas_call(
        _post_b_sample_kernel,
        grid=(1,),
        in_specs=[full(a) for a in args],
        out_specs=[pl.BlockSpec((n * ROW_CHUNKS, LANES), lambda i: (0, 0)),
                   pl.BlockSpec((n, LANES), lambda i: (0, 0)),
                   pl.BlockSpec((SUBLANES, LANES), lambda i: (0, 0))],
        out_shape=[jax.ShapeDtypeStruct((n * ROW_CHUNKS, LANES), jnp.float32),
                   jax.ShapeDtypeStruct((n, LANES), jnp.float32),
                   jax.ShapeDtypeStruct((SUBLANES, LANES), jnp.float32)],
        compiler_params=_params(("arbitrary",)),
        name="post_b_sample",
    )(*args)


def _row_gather_copy(src_hbm, idx, dst, dst_row, sem):
    s0 = pl.multiple_of(idx * ROW_CHUNKS, ROW_CHUNKS)
    d0 = pl.multiple_of(dst_row * ROW_CHUNKS, ROW_CHUNKS)
    return pltpu.make_async_copy(src_hbm.at[pl.ds(s0, ROW_CHUNKS), :], dst.at[pl.ds(d0, ROW_CHUNKS), :], sem)


def _dispatch_kernel(pos_ref, h2t_ref, xs_hbm, sem):
    def body(r, c):
        src = h2t_ref.at[pl.ds(pl.multiple_of(r * ROW_CHUNKS, ROW_CHUNKS), ROW_CHUNKS), :]
        for k in range(2):
            d0 = pl.multiple_of(pos_ref[0, 0, k * TM_COMB + r] * ROW_CHUNKS, ROW_CHUNKS)
            pltpu.make_async_copy(src, xs_hbm.at[pl.ds(d0, ROW_CHUNKS), :], sem.at[0]).start(priority=k)
        return c
    lax.fori_loop(0, TM_COMB, body, 0, unroll=8)
    for _ in range(2):
        pltpu.make_async_copy(h2t_ref, xs_hbm.at[pl.ds(0, TM_COMB * ROW_CHUNKS), :], sem.at[0]).wait()


def _dispatch(pos3, h2t):
    nt = N_ALL // TM_COMB
    return pl.pallas_call(
        _dispatch_kernel,
        grid=(nt,),
        in_specs=[pl.BlockSpec((1, 1, 2 * TM_COMB), lambda i: (i, 0, 0), memory_space=pltpu.SMEM),
                  pl.BlockSpec((TM_COMB * ROW_CHUNKS, LANES), lambda i: (i, 0))],
        out_specs=pl.BlockSpec(memory_space=pl.ANY),
        out_shape=jax.ShapeDtypeStruct((N_ASSIGN * ROW_CHUNKS, LANES), jnp.float32),
        scratch_shapes=[pltpu.SemaphoreType.DMA((1,))],
        compiler_params=_params(("arbitrary",)),
        name="moe_dispatch",
    )(pos3, h2t)


def _moe_ffn_kernel(it_ref, ie_ref, lo_ref, hi_ref, x_ref, wg_ref, wu_ref, wd_ref, y_ref, wgb, wub, wdb, cur_e):
    i = pl.program_id(0)
    lo = lo_ref[i]
    hi = hi_ref[i]
    e = ie_ref[i]

    @pl.when(i == 0)
    def _():
        cur_e[0] = -1

    @pl.when((hi > lo) & (cur_e[0] != e))
    def _():
        wgb[...] = _bf16(wg_ref[0])
        wub[...] = _bf16(wu_ref[0])
        wdb[...] = _bf16(wd_ref[0])
        cur_e[0] = e

    def ffn():
        x = _bf16(_load_token_tiles(x_ref, 0, TM_MOE))
        hg = _dot(x, wgb[...])
        hu = _dot(x, wub[...])
        h = hg / (1.0 + jnp.exp(-hg)) * hu
        return _dot(_bf16(h), wdb[...])

    def rows_mask():
        row = lax.broadcasted_iota(jnp.int32, (TM_MOE, LANES), 0)
        return (row >= lo) & (row < hi)

    @pl.when((hi > lo) & (lo == 0))
    def _():
        y = ffn()
        mask = rows_mask()
        for c in range(ROW_CHUNKS):
            y_ref[pl.ds(c, TM_MOE, stride=ROW_CHUNKS), :] = jnp.where(mask, y[:, c * LANES:(c + 1) * LANES], 0.0)

    @pl.when((hi > lo) & (lo > 0))
    def _():
        y = ffn()
        mask = rows_mask()
        for c in range(ROW_CHUNKS):
            sl = pl.ds(c, TM_MOE, stride=ROW_CHUNKS)
            y_ref[sl, :] = jnp.where(mask, y[:, c * LANES:(c + 1) * LANES], y_ref[sl, :])


def _moe_ffn(item_tile, item_expert, item_lo, item_hi, x_sorted, w_gate, w_up, w_down):
    wspec = lambda shp: pl.BlockSpec((1,) + shp, lambda i, it, ie, lo, hi: (ie[i], 0, 0))
    tile = pl.BlockSpec((TM_MOE * ROW_CHUNKS, LANES), lambda i, it, ie, lo, hi: (it[i], 0))
    grid_spec = pltpu.PrefetchScalarGridSpec(
        num_scalar_prefetch=4,
        grid=(MOE_ITEMS,),
        in_specs=[tile, wspec((D_MODEL, EXPERT_FF)), wspec((D_MODEL, EXPERT_FF)), wspec((EXPERT_FF, D_MODEL))],
        out_specs=tile,
        scratch_shapes=[pltpu.VMEM((D_MODEL, EXPERT_FF), jnp.bfloat16),
                        pltpu.VMEM((D_MODEL, EXPERT_FF), jnp.bfloat16),
                        pltpu.VMEM((EXPERT_FF, D_MODEL), jnp.bfloat16),
                        pltpu.SMEM((1,), jnp.int32)],
    )
    return pl.pallas_call(
        _moe_ffn_kernel,
        grid_spec=grid_spec,
        out_shape=jax.ShapeDtypeStruct((N_ASSIGN * ROW_CHUNKS, LANES), jnp.float32),
        compiler_params=_params(("arbitrary",)),
        name="moe_ffn",
    )(item_tile, item_expert, item_lo, item_hi, x_sorted, w_gate, w_up, w_down)


def _combine_kernel(nt, pos_cur_ref, pos_nxt_ref, yt_hbm, h2t_ref, rt_ref, g3_ref, b3_ref, o_ref, abuf, sem):
    t = pl.program_id(0)
    slot = t % 2
    rows = 2 * TM_COMB

    def issue(pos_ref, s):
        def body(j, c):
            for k in range(2):
                r = 2 * j + k
                _row_gather_copy(yt_hbm, pos_ref[0, 0, r], abuf, s * rows + r, sem.at[s]).start(priority=k)
            return c
        lax.fori_loop(0, rows // 2, body, 0, unroll=4)

    @pl.when(t == 0)
    def _():
        issue(pos_cur_ref, 0)

    @pl.when(t + 1 < nt)
    def _():
        issue(pos_nxt_ref, 1 - slot)

    base = pl.multiple_of(slot * (rows * ROW_CHUNKS), rows * ROW_CHUNKS)
    pltpu.make_async_copy(yt_hbm.at[pl.ds(0, rows * ROW_CHUNKS), :],
                          abuf.at[pl.ds(base, rows * ROW_CHUNKS), :], sem.at[slot]).wait()
    ya = _load_token_tiles(abuf, base, TM_COMB)
    yb = _load_token_tiles(abuf, base + TM_COMB * ROW_CHUNKS, TM_COMB)
    rt = rt_ref[...]
    ff = rt[:, 2:3] * ya + rt[:, 3:4] * yb
    h2 = _load_token_tiles(h2t_ref, 0, TM_COMB)
    o_ref[...] = _layer_norm(ALPHA * h2 + ff, g3_ref[...], b3_ref[...])


def _combine(pos3, yt, h2t, rt, g3, b3, tile0, n_tiles):
    last = tile0 + n_tiles - 1
    smem_pos = lambda f: pl.BlockSpec((1, 1, 2 * TM_COMB), f, memory_space=pltpu.SMEM)
    full = lambda a: pl.BlockSpec(a.shape, lambda i: (0,) * a.ndim)
    return pl.pallas_call(
        functools.partial(_combine_kernel, n_tiles),
        grid=(n_tiles,),
        in_specs=[smem_pos(lambda i: (tile0 + i, 0, 0)),
                  smem_pos(lambda i: (jnp.minimum(tile0 + i + 1, last), 0, 0)),
                  pl.BlockSpec(memory_space=pl.ANY),
                  pl.BlockSpec((TM_COMB * ROW_CHUNKS, LANES), lambda i: (tile0 + i, 0)),
                  pl.BlockSpec((TM_COMB, LANES), lambda i: (tile0 + i, 0)),
                  full(g3), full(b3)],
        out_specs=pl.BlockSpec((TM_COMB, D_MODEL), lambda i: (i, 0)),
        out_shape=jax.ShapeDtypeStruct((n_tiles * TM_COMB, D_MODEL), jnp.float32),
        scratch_shapes=[pltpu.VMEM((2 * 2 * TM_COMB * ROW_CHUNKS, LANES), jnp.float32),
                        pltpu.SemaphoreType.DMA((2,))],
        compiler_params=_params(("arbitrary",)),
        name="moe_combine",
    )(pos3, pos3, yt, h2t, rt, g3, b3)


POS_TILES = 3


def _positions_kernel(rt_ref, starts_ref, pos_ref):
    lane = lax.broadcasted_iota(jnp.int32, (TM_COMB, LANES), 1)
    lane_f = lane.astype(jnp.float32)
    starts = starts_ref[0:1, :]
    for j in range(POS_TILES):
        rt = rt_ref[j * TM_COMB:(j + 1) * TM_COMB, :]
        cols = []
        for k in range(2):
            seg = jnp.sum(jnp.where(lane_f == rt[:, k:k + 1] + N_GROUPS, starts, 0.0), axis=1, keepdims=True)
            cols.append(seg + rt[:, 4 + k:5 + k])
        packed = jnp.where(lane == 0, cols[0], jnp.where(lane == 1, cols[1], 0.0))
        rows = packed.T
        pos_ref[j] = jnp.concatenate([rows[0:1, :], rows[1:2, :]], axis=1).astype(jnp.int32)


def _positions(rt, starts_row):
    nt = N_ALL // TM_COMB
    assert nt % POS_TILES == 0
    return pl.pallas_call(
        _positions_kernel,
        grid=(nt // POS_TILES,),
        in_specs=[pl.BlockSpec((POS_TILES * TM_COMB, LANES), lambda i: (i, 0)),
                  pl.BlockSpec((SUBLANES, LANES), lambda i: (0, 0))],
        out_specs=pl.BlockSpec((POS_TILES, 1, 2 * TM_COMB), lambda i: (i, 0, 0)),
        out_shape=jax.ShapeDtypeStruct((nt, 1, 2 * TM_COMB), jnp.int32),
        compiler_params=_params(("arbitrary",)),
        name="moe_positions",
    )(rt, starts_row)


def _routing_plan(rt, cnt):
    i32 = jnp.int32
    counts_f = cnt[0, N_GROUPS:N_GROUPS + N_EXPERTS]
    starts_f = jnp.cumsum(counts_f) - counts_f
    starts_row = jnp.broadcast_to(
        jnp.pad(starts_f, (N_GROUPS, LANES - N_GROUPS - N_EXPERTS))[None, :], (SUBLANES, LANES))
    pos3 = _positions(rt, starts_row)
    starts = starts_f.astype(i32)
    tiles = jnp.arange(MOE_TILES, dtype=i32) * TM_MOE
    rank_t = jnp.arange(MOE_TILES, dtype=i32) + jnp.sum((starts[None, :] < tiles[:, None]).astype(i32), axis=1)
    rank_s = jnp.arange(N_EXPERTS, dtype=i32) + jnp.sum((tiles[None, :] <= starts[:, None]).astype(i32), axis=1)
    vals = jnp.concatenate([tiles, starts])
    ranks = jnp.concatenate([rank_t, rank_s])
    slot = jnp.arange(MOE_ITEMS, dtype=i32)
    lo = jnp.sum(jnp.where(ranks[None, :] == slot[:, None], vals[None, :], 0), axis=1)
    hi = jnp.concatenate([lo[1:], jnp.full((1,), N_ASSIGN, i32)])
    item_tile = jnp.minimum(lo // TM_MOE, MOE_TILES - 1)
    item_expert = jnp.clip(jnp.sum((starts[None, :] <= lo[:, None]).astype(i32), axis=1) - 1, 0, N_EXPERTS - 1)
    base = item_tile * TM_MOE
    return item_tile, item_expert, lo - base, hi - base, pos3


def kernel(x_prompt, x_sample, mem_prompt, cache_swa_k, cache_swa_v, cache_conv, cache_mem_k, cache_mem_v,
           w_in, sinks, conv_w, w_mix_out, ln1_g, ln1_b, w_q_mem, w_k_mem, w_v_mem, w_o_mem, ln2_g, ln2_b,
           w_router_group, b_router_group, w_router_expert, b_router_expert, w_gate, w_up, w_down,
           ln3_g, ln3_b):
    f32 = jnp.float32
    row = lambda a: a.reshape(1, -1).astype(f32)
    w_in_b, wmix_b, wq_b, wk_b, wv_b, wo_b = (_bf16(w) for w in (w_in, w_mix_out, w_q_mem, w_k_mem, w_v_mem, w_o_mem))
    g1, b1, g2, b2, g3, b3 = (row(a) for a in (ln1_g, ln1_b, ln2_g, ln2_b, ln3_g, ln3_b))
    pad = LANES - N_GROUPS - N_EXPERTS
    wr = jnp.concatenate([w_router_group, w_router_expert, jnp.zeros((D_MODEL, pad), f32)], axis=1)
    wrh = _bf16(wr)
    wrhl = jnp.concatenate([wrh, _bf16(wr - wrh.astype(f32))], axis=1)
    br = jnp.concatenate([b_router_group, b_router_expert, jnp.zeros((pad,), f32)]).reshape(1, LANES)

    xs = x_sample.reshape(N_SAMPLE, D_MODEL)
    tab_s = jnp.tile(_rope_table(PAST_LEN + jnp.arange(DEC_SEQ)), (1, DEC_BATCH))
    c0 = jnp.repeat(cache_conv[:, 0], DEC_SEQ, axis=0)
    c1 = jnp.repeat(cache_conv[:, 1], DEC_SEQ, axis=0)
    q_s, k_s, v_s, conv_s, u_s = _proj_sample(xs, w_in_b, tab_s, conv_w, c0, c1)
    attn_s, swa_k_s, swa_v_s = _swa_sample(sinks, q_s, k_s, v_s, cache_swa_k, cache_swa_v)
    h1_s, qm_s = _post_a_sample(attn_s, conv_s, xs, wmix_b, g1, b1, wq_b)
    o_s = _mem_attn_sample(qm_s, cache_mem_k, cache_mem_v)
    tri = _bf16(jnp.tril(jnp.ones((TM_POST, TM_POST), f32), -1))
    h2t_s, rt_s, cnt_s = _post_b_sample(o_s, h1_s, wo_b, g2, b2, wrhl, br, tri)

    xp = x_prompt.reshape(N_PROMPT, D_MODEL)
    tab_p = _rope_table(jnp.arange(N_PROMPT))
    q_p, kx_p, vx_p, conv_p, k_tail, v_tail, u_tail = _proj_prompt(xp, w_in_b, tab_p, conv_w)
    mk, mv, wqk, wvo = _mem_kv(mem_prompt.reshape(N_MEM, D_MODEL), wk_b, wv_b, wq_b, wo_b)
    h2t, rt, cnt = _swa_post(sinks, q_p, kx_p, vx_p, conv_p, xp, wmix_b, g1, b1, wqk, wvo, g2, b2,
                             wrhl, br, tri, h2t_s, rt_s, cnt_s)

    item_tile, item_expert, item_lo, item_hi, pos3 = _routing_plan(rt, cnt)
    x_sorted = _dispatch(pos3, h2t)
    yt = _moe_ffn(item_tile, item_expert, item_lo, item_hi, x_sorted, w_gate, w_up, w_down)
    y_p = _combine(pos3, yt, h2t, rt, g3, b3, 0, N_PROMPT // TM_COMB)
    y_s = _combine(pos3, yt, h2t, rt, g3, b3, N_PROMPT // TM_COMB, N_SAMPLE // TM_COMB)

    return (y_p.reshape(1, SEQ, D_MODEL),
            y_s.reshape(DEC_BATCH, DEC_SEQ, D_MODEL),
            k_tail.reshape(1, WINDOW, N_KV_HEADS, HEAD_DIM),
            v_tail.reshape(1, WINDOW, N_KV_HEADS, HEAD_DIM),
            u_tail[SUBLANES - (CONV_K - 1):].reshape(1, CONV_K - 1, CONV_CH),
            mk.reshape(1, N_MEM, MEM_HEADS, MEM_HEAD_DIM),
            mv.reshape(1, N_MEM, MEM_HEADS, MEM_HEAD_DIM),
            swa_k_s.reshape(DEC_BATCH, WINDOW, N_KV_HEADS, HEAD_DIM),
            swa_v_s.reshape(DEC_BATCH, WINDOW, N_KV_HEADS, HEAD_DIM),
            u_s.reshape(DEC_BATCH, DEC_SEQ, CONV_CH)[:, DEC_SEQ - (CONV_K - 1):])
```

```python
import functools

import jax
import jax.numpy as jnp
from jax import lax
from jax.experimental import pallas as pl
from jax.experimental.pallas import tpu as pltpu

D_MODEL = 1024
SEQ = 16384
DEC_BATCH = 128
DEC_SEQ = 4
PAST_LEN = 16384
ATTN_WIDTH = 512
CONV_CH = 512
HEAD_DIM = 64
N_HEADS = 8
N_KV_HEADS = 2
KV_WIDTH = 128
WINDOW = 128
ROPE_THETA = 500000.0
ROPE_DIM = 16
CONV_K = 3
Q_END = ATTN_WIDTH
K_END = Q_END + KV_WIDTH
V_END = K_END + KV_WIDTH
B_END = V_END + CONV_CH
C_END = B_END + CONV_CH
IN_WIDTH = C_END + CONV_CH
N_MEM = 256
MEM_HEADS = 4
MEM_HEAD_DIM = 256
N_GROUPS = 4
EXPERTS_PER_GROUP = 8
N_EXPERTS = 32
EXPERT_FF = 256
ALPHA = 2.0 ** 0.25
LN_EPS = 1e-5

LANES = 128
SUBLANES = 8
ROW_CHUNKS = D_MODEL // LANES
VMEM_LIMIT = 56 * 1024 * 1024

N_PROMPT = SEQ
N_SAMPLE = DEC_BATCH * DEC_SEQ
N_ALL = N_PROMPT + N_SAMPLE
TM_POST = 512
TM_MOE = 512
TM_COMB = 512
N_ASSIGN = 2 * N_ALL
MOE_TILES = N_ASSIGN // TM_MOE
MOE_ITEMS = MOE_TILES + N_EXPERTS
SAMPLE_BB = 4
SWA_BB = 8

assert ROW_CHUNKS == SUBLANES
assert N_SAMPLE == TM_POST
assert N_ASSIGN % TM_MOE == 0 and N_ALL % TM_COMB == 0


def _params(sem, vmem=VMEM_LIMIT):
    return pltpu.CompilerParams(dimension_semantics=sem, vmem_limit_bytes=vmem)


def _bf16(x):
    return x.astype(jnp.bfloat16)


def _dot(a, b):
    return jnp.dot(a, b, preferred_element_type=jnp.float32)


def _dot_nt(a, b):
    return lax.dot_general(a, b, (((1,), (1,)), ((), ())), preferred_element_type=jnp.float32)


def _layer_norm(x, g, b):
    mu = jnp.mean(x, axis=-1, keepdims=True)
    xc = x - mu
    var = jnp.mean(xc * xc, axis=-1, keepdims=True)
    return xc * lax.rsqrt(var + LN_EPS) * g + b


def _rope(x, cos_t, sin_t):
    lane = lax.broadcasted_iota(jnp.int32, x.shape, 1) % HEAD_DIM
    half = ROPE_DIM // 2
    partner = jnp.where(lane < half, pltpu.roll(x, LANES - half, axis=1), pltpu.roll(x, half, axis=1))
    return x * cos_t + partner * sin_t


def _head_slabs(x):
    lane = lax.broadcasted_iota(jnp.int32, x.shape, 1)
    lo = lane < HEAD_DIM
    sw = pltpu.roll(x, HEAD_DIM, axis=1)
    zero = jnp.zeros_like(x)
    slabs = [jnp.where(lo, x, zero), jnp.where(lo, zero, sw), jnp.where(lo, sw, zero), jnp.where(lo, zero, x)]
    return _bf16(jnp.concatenate(slabs, axis=1))


def _store_token_tiles(ref, val):
    rows = val.shape[0]
    for c in range(ROW_CHUNKS):
        ref[pl.ds(c, rows, stride=ROW_CHUNKS), :] = val[:, c * LANES:(c + 1) * LANES]


def _load_token_tiles(ref, base, rows):
    return jnp.concatenate(
        [ref[pl.ds(base + c, rows, stride=ROW_CHUNKS), :] for c in range(ROW_CHUNKS)], axis=1)


ROPE_ONE = 3 * (ROPE_DIM // 2)
ROPE_ROWS = 32


def _rope_patterns(tab):
    half = ROPE_DIM // 2
    m = lax.broadcasted_iota(jnp.int32, tab.shape, 1) % HEAD_DIM
    idx_c = jnp.where(m < ROPE_DIM, m % half, ROPE_ONE)
    idx_s = jnp.where(m < half, 2 * half + m, jnp.where(m < ROPE_DIM, m, ROPE_ONE + 1))
    return jnp.take_along_axis(tab, idx_c, axis=1), jnp.take_along_axis(tab, idx_s, axis=1)


def _proj_common(x_ref, w_ref, tab_ref):
    xb = _bf16(x_ref[...])
    tab = tab_ref[...]
    pad = jnp.zeros((LANES - tab.shape[0], tab.shape[1]), jnp.float32)
    cos_t, sin_t = _rope_patterns(jnp.concatenate([tab, pad], axis=0).T)
    q = _dot(xb, w_ref[:, 0:Q_END])
    q_rot = jnp.concatenate(
        [_rope(q[:, p * LANES:(p + 1) * LANES], cos_t, sin_t) for p in range(ATTN_WIDTH // LANES)], axis=1)
    q_out = _bf16(q_rot * (HEAD_DIM ** -0.5))
    kv = _dot(xb, w_ref[:, Q_END:V_END])
    k = _rope(kv[:, 0:KV_WIDTH], cos_t, sin_t)
    v = kv[:, KV_WIDTH:]
    bg = _dot(xb, w_ref[:, V_END:B_END])
    u = _dot(xb, w_ref[:, B_END:C_END]) * _dot(xb, w_ref[:, C_END:IN_WIDTH])
    return q_out, k, v, bg, u


def _conv3(bg, u, u1, u2, cw_ref):
    cw = cw_ref[...]
    return bg * (cw[0:1, :] * u2 + cw[1:2, :] * u1 + cw[2:3, :] * u)


def _proj_prompt_body(x_ref, w_ref, tab_ref, cw_ref,
                      q_ref, kx_ref, vx_ref, conv_ref, ktail_ref, vtail_ref, utail_ref, carry_ref):
    @pl.when(pl.program_id(0) == 0)
    def _():
        carry_ref[...] = jnp.zeros_like(carry_ref)

    q_out, k, v, bg, u = _proj_common(x_ref, w_ref, tab_ref)
    tm = u.shape[0]
    ext = jnp.concatenate([carry_ref[...], u], axis=0)
    u1 = pltpu.roll(ext, 1, axis=0)[SUBLANES:SUBLANES + tm]
    u2 = pltpu.roll(ext, 2, axis=0)[SUBLANES:SUBLANES + tm]
    q_ref[...] = q_out
    kx_ref[...] = _head_slabs(k)
    vx_ref[...] = _head_slabs(v)
    conv_ref[...] = _bf16(_conv3(bg, u, u1, u2, cw_ref))
    ktail_ref[...] = k[tm - WINDOW:tm]
    vtail_ref[...] = v[tm - WINDOW:tm]
    utail_ref[...] = u[tm - SUBLANES:tm]
    carry_ref[...] = u[tm - SUBLANES:tm]


def _proj_sample_kernel(x_ref, w_ref, tab_ref, cw_ref, c0_ref, c1_ref,
                        q_ref, k_ref, v_ref, conv_ref, u_ref):
    q_out, k, v, bg, u = _proj_common(x_ref, w_ref, tab_ref)
    t = lax.broadcasted_iota(jnp.int32, u.shape, 0) % DEC_SEQ
    c0 = c0_ref[...]
    c1 = c1_ref[...]
    u1 = jnp.where(t >= 1, pltpu.roll(u, 1, axis=0), c1)
    u2 = jnp.where(t >= 2, pltpu.roll(u, 2, axis=0), jnp.where(t == 1, c1, c0))
    q_ref[...] = q_out.astype(jnp.float32)
    k_ref[...] = k
    v_ref[...] = v
    conv_ref[...] = _bf16(_conv3(bg, u, u1, u2, cw_ref))
    u_ref[...] = u


def _rope_table(pos):
    half = ROPE_DIM // 2
    inv = ROPE_THETA ** (-jnp.arange(0, ROPE_DIM, 2, dtype=jnp.float32) / ROPE_DIM)
    ang = pos.astype(jnp.float32)[None, :] * inv[:, None]
    cos, sin = jnp.cos(ang), jnp.sin(ang)
    n = pos.shape[0]
    assert ROPE_ONE == 3 * half
    return jnp.concatenate([cos, sin, -sin, jnp.ones((1, n), jnp.float32),
                            jnp.zeros((ROPE_ROWS - ROPE_ONE - 1, n), jnp.float32)], axis=0)


def _proj_sample(x, w_in_b, tab, conv_w, c0, c1):
    n = x.shape[0]
    full = lambda a: pl.BlockSpec(a.shape, lambda i: (0,) * a.ndim)
    out = lambda w, dt: jax.ShapeDtypeStruct((n, w), dt)
    blk = lambda w: pl.BlockSpec((n, w), lambda i: (0, 0))
    return pl.pallas_call(
        _proj_sample_kernel,
        grid=(1,),
        in_specs=[full(x), full(w_in_b), full(tab), full(conv_w), full(c0), full(c1)],
        out_specs=[blk(ATTN_WIDTH), blk(KV_WIDTH), blk(KV_WIDTH), blk(CONV_CH), blk(CONV_CH)],
        out_shape=[out(ATTN_WIDTH, jnp.float32), out(KV_WIDTH, jnp.float32), out(KV_WIDTH, jnp.float32),
                   out(CONV_CH, jnp.bfloat16), out(CONV_CH, jnp.float32)],
        compiler_params=_params(("arbitrary",)),
        name="proj_sample",
    )(x, w_in_b, tab, conv_w, c0, c1)


def _sink_softmax_pv(s, valid, sink, vx):
    s = jnp.where(valid, s, -jnp.inf)
    m = jnp.maximum(jnp.max(s, axis=1, keepdims=True), sink)
    p = jnp.exp(s - m)
    den = jnp.sum(p, axis=1, keepdims=True) + jnp.exp(sink - m)
    return _dot(_bf16(p), vx) / den


SWA_QB = 4


def _swa_tile(step, sinks_ref, q_ref, kc_ref, kp_ref, vc_ref, vp_ref, store):
    kall = jnp.concatenate([kp_ref[...], kc_ref[...]], axis=0)
    vall = jnp.concatenate([vp_ref[...], vc_ref[...]], axis=0)
    i = lax.broadcasted_iota(jnp.int32, (WINDOW, 2 * WINDOW), 0)
    j = lax.broadcasted_iota(jnp.int32, (WINDOW, 2 * WINDOW), 1)
    band = (j > i) & (j <= i + WINDOW)
    for sb in range(SWA_QB):
        rows = slice(sb * WINDOW, (sb + 1) * WINDOW)
        kcat = kall[sb * WINDOW:(sb + 2) * WINDOW]
        vcat = vall[sb * WINDOW:(sb + 2) * WINDOW]
        valid = band & ((step > 0) | (j >= WINDOW)) if sb == 0 else band
        for p in range(N_HEADS // 2):
            qs = q_ref[rows, p * LANES:(p + 1) * LANES]
            acc = None
            for e in range(2):
                hd = 2 * p + e
                slab = 2 * (hd // (N_HEADS // N_KV_HEADS)) + e
                kx = kcat[:, slab * LANES:(slab + 1) * LANES]
                vx = vcat[:, slab * LANES:(slab + 1) * LANES]
                o = _sink_softmax_pv(_dot_nt(qs, kx), valid, sinks_ref[hd], vx)
                acc = o if acc is None else acc + o
            store(rows, slice(p * LANES, (p + 1) * LANES), _bf16(acc))


SWA_ROWS = N_HEADS * DEC_SEQ
NEW_ROWS = 2 * SUBLANES


def _swa_sample_kernel(q_ref, sink_ref, kn_ref, vn_ref, kt_ref, vt_ref, o_ref, okt_ref, ovt_ref):
    t = lax.broadcasted_iota(jnp.int32, (SWA_ROWS, WINDOW), 0) % DEC_SEQ
    j = lax.broadcasted_iota(jnp.int32, (SWA_ROWS, WINDOW), 1)
    valid_c = j > t
    valid_n = (lax.broadcasted_iota(jnp.int32, (SWA_ROWS, NEW_ROWS), 1)
               <= lax.broadcasted_iota(jnp.int32, (SWA_ROWS, NEW_ROWS), 0) % DEC_SEQ)
    lane = lax.broadcasted_iota(jnp.int32, (KV_WIDTH, WINDOW), 1)
    sink = sink_ref[:, 0:1]
    shift = WINDOW - DEC_SEQ
    zrows = jnp.zeros((KV_WIDTH - NEW_ROWS, KV_WIDTH), jnp.float32)
    for b in range(SWA_BB):
        q = _bf16(q_ref[b])
        kt, vt = kt_ref[b], vt_ref[b]
        kn, vn = kn_ref[b], vn_ref[b]
        s_c = jnp.where(valid_c, _dot(q, _bf16(kt)), -jnp.inf)
        s_n = jnp.where(valid_n, _dot_nt(q, _bf16(kn)), -jnp.inf)
        m = jnp.maximum(jnp.maximum(jnp.max(s_c, axis=1, keepdims=True), jnp.max(s_n, axis=1, keepdims=True)), sink)
        p_c = jnp.exp(s_c - m)
        p_n = jnp.exp(s_n - m)
        den = jnp.sum(p_c, axis=1, keepdims=True) + jnp.sum(p_n, axis=1, keepdims=True) + jnp.exp(sink - m)
        o_ref[b] = (_dot_nt(_bf16(p_c), _bf16(vt)) + _dot(_bf16(p_n), _bf16(vn))) / den
        for old, new, dst in ((kt, kn, okt_ref), (vt, vn, ovt_ref)):
            new_cols = pltpu.roll(jnp.concatenate([new, zrows], axis=0).T, shift, axis=1)
            dst[b] = jnp.where(lane >= shift, new_cols, pltpu.roll(old, shift, axis=1))


def _swa_sample(sinks, q, kn, vn, cache_k, cache_v):
    nb = cache_k.shape[0]
    bb = SWA_BB
    groups = N_HEADS // N_KV_HEADS
    qh = q.reshape(nb, DEC_SEQ, N_KV_HEADS, groups, HEAD_DIM).transpose(0, 2, 3, 1, 4)
    qh = qh.reshape(nb, N_KV_HEADS, groups * DEC_SEQ, HEAD_DIM)
    zeros = jnp.zeros_like(qh[:, 0])
    qbd = jnp.concatenate([jnp.concatenate([qh[:, 0], zeros], axis=-1),
                           jnp.concatenate([zeros, qh[:, 1]], axis=-1)], axis=1)
    sink_col = jnp.broadcast_to(jnp.repeat(sinks, DEC_SEQ).reshape(SWA_ROWS, 1), (SWA_ROWS, LANES))
    pad8 = lambda a: jnp.pad(a.reshape(nb, DEC_SEQ, KV_WIDTH), ((0, 0), (0, NEW_ROWS - DEC_SEQ), (0, 0)))
    to_t = lambda c: c.transpose(0, 2, 3, 1).reshape(nb, KV_WIDTH, WINDOW)
    blk = lambda r, w: pl.BlockSpec((bb, r, w), lambda i: (i, 0, 0))
    o, okt, ovt = pl.pallas_call(
        _swa_sample_kernel,
        grid=(nb // bb,),
        in_specs=[blk(SWA_ROWS, KV_WIDTH), pl.BlockSpec((SWA_ROWS, LANES), lambda i: (0, 0)),
                  blk(NEW_ROWS, KV_WIDTH), blk(NEW_ROWS, KV_WIDTH), blk(KV_WIDTH, WINDOW), blk(KV_WIDTH, WINDOW)],
        out_specs=[blk(SWA_ROWS, KV_WIDTH), blk(KV_WIDTH, WINDOW), blk(KV_WIDTH, WINDOW)],
        out_shape=[jax.ShapeDtypeStruct((nb, SWA_ROWS, KV_WIDTH), jnp.float32),
                   jax.ShapeDtypeStruct((nb, KV_WIDTH, WINDOW), jnp.float32),
                   jax.ShapeDtypeStruct((nb, KV_WIDTH, WINDOW), jnp.float32)],
        compiler_params=_params(("arbitrary",)),
        name="swa_sample",
    )(qbd, sink_col, pad8(kn), pad8(vn), to_t(cache_k), to_t(cache_v))
    o = o.reshape(nb, N_KV_HEADS, groups, DEC_SEQ, N_KV_HEADS, HEAD_DIM)
    attn = jnp.stack([o[:, h, :, :, h, :] for h in range(N_KV_HEADS)], axis=1)
    attn = attn.transpose(0, 3, 1, 2, 4).reshape(nb * DEC_SEQ, ATTN_WIDTH)
    from_t = lambda c: c.reshape(nb, N_KV_HEADS, HEAD_DIM, WINDOW).transpose(0, 3, 1, 2)
    return attn, from_t(okt), from_t(ovt)


def _mem_kv_kernel(mem_ref, wk_ref, wv_ref, wq_ref, wo_ref, mk_ref, mv_ref, wqk_ref, wvo_ref):
    mb = _bf16(mem_ref[...])
    mk = _dot(mb, wk_ref[...])
    mv = _dot(mb, wv_ref[...])
    mk_ref[...] = mk
    mv_ref[...] = mv
    mkb, mvb = _bf16(mk), _bf16(mv)
    for h in range(MEM_HEADS):
        sl = slice(h * MEM_HEAD_DIM, (h + 1) * MEM_HEAD_DIM)
        keys = slice(h * N_MEM, (h + 1) * N_MEM)
        wqk_ref[:, keys] = _bf16(_dot_nt(wq_ref[:, sl], mkb[:, sl]) * (MEM_HEAD_DIM ** -0.5))
        wvo_ref[keys, :] = _bf16(_dot(mvb[:, sl], wo_ref[sl, :]))


def _mem_kv(mem, wk_b, wv_b, wq_b, wo_b):
    full = lambda a: pl.BlockSpec(a.shape, lambda i: (0,) * a.ndim)
    blk = pl.BlockSpec((N_MEM, D_MODEL), lambda i: (0, 0))
    f32 = jax.ShapeDtypeStruct((N_MEM, D_MODEL), jnp.float32)
    fused = (D_MODEL, MEM_HEADS * N_MEM), (MEM_HEADS * N_MEM, D_MODEL)
    return pl.pallas_call(
        _mem_kv_kernel,
        grid=(1,),
        in_specs=[full(mem), full(wk_b), full(wv_b), full(wq_b), full(wo_b)],
        out_specs=[blk, blk] + [pl.BlockSpec(shp, lambda i: (0, 0)) for shp in fused],
        out_shape=[f32, f32] + [jax.ShapeDtypeStruct(shp, jnp.bfloat16) for shp in fused],
        compiler_params=_params(("arbitrary",)),
        name="mem_kv",
    )(mem, wk_b, wv_b, wq_b, wo_b)


def _mix_ln1(attn_ref, conv_ref, x_ref, wmix_ref, g1_ref, b1_ref):
    mix = _dot(_bf16(attn_ref[...]), wmix_ref[0:ATTN_WIDTH, :]) + _dot(conv_ref[...], wmix_ref[ATTN_WIDTH:, :])
    return _layer_norm(ALPHA * x_ref[...] + mix, g1_ref[...], b1_ref[...])


def _mem_q(h1, wq_ref):
    return _bf16(_dot(_bf16(h1), wq_ref[...]) * (MEM_HEAD_DIM ** -0.5))


def _route(h2, wrhl_ref, br_ref, tri_ref, carry):
    hi = _bf16(h2)
    lo = _bf16(h2 - hi.astype(jnp.float32))
    hh = _dot(hi, wrhl_ref[...])
    logits = hh[:, 0:LANES] + hh[:, LANES:] + _dot(lo, wrhl_ref[:, 0:LANES]) + br_ref[...]
    lane_i = lax.broadcasted_iota(jnp.int32, logits.shape, 1)
    lane = lane_i.astype(jnp.float32)
    big = jnp.float32(LANES)
    is_g = lane_i < N_GROUPS
    gl = jnp.where(is_g, logits, -jnp.inf)
    gmax = jnp.max(gl, axis=1, keepdims=True)
    gidx = jnp.min(jnp.where(is_g & (logits == gmax), lane, big), axis=1, keepdims=True)
    gsum = jnp.sum(jnp.exp(gl - gmax), axis=1, keepdims=True)
    gw = 1.0 / gsum
    eid = lane_i - N_GROUPS
    assert EXPERTS_PER_GROUP == 8
    grp = lax.shift_right_arithmetic(eid, jnp.full_like(eid, 3)).astype(jnp.float32)
    in_e = (lane_i >= N_GROUPS) & (lane_i < N_GROUPS + N_EXPERTS) & (grp == gidx)
    v1 = jnp.max(jnp.where(in_e, logits, -jnp.inf), axis=1, keepdims=True)
    i1 = jnp.min(jnp.where(in_e & (logits == v1), lane, big), axis=1, keepdims=True)
    rest = in_e & (lane != i1)
    v2 = jnp.max(jnp.where(rest, logits, -jnp.inf), axis=1, keepdims=True)
    i2 = jnp.min(jnp.where(rest & (logits == v2), lane, big), axis=1, keepdims=True)
    ex = jnp.exp(v2 - v1)
    den = 1.0 + ex
    w1 = gw / den
    w2 = gw * ex / den
    zero = jnp.zeros_like(logits)
    pick1 = lane == i1
    pick2 = lane == i2
    sel = jnp.where(pick1 | pick2, 1.0, 0.0)
    before = _dot(tri_ref[...], _bf16(sel)) + carry
    rank1 = jnp.sum(jnp.where(pick1, before, zero), axis=1, keepdims=True)
    rank2 = jnp.sum(jnp.where(pick2, before, zero), axis=1, keepdims=True)
    cols = (i1 - N_GROUPS, i2 - N_GROUPS, w1, w2, rank1, rank2)
    route = zero
    for k, col in enumerate(cols):
        route = jnp.where(lane_i == k, col, route)
    return route, carry + jnp.sum(sel, axis=0, keepdims=True)


def _post_tile(attn, conv_ref, x_ref, wmix_ref, g1_ref, b1_ref, wqk_ref, wvo_ref, g2_ref, b2_ref,
               wrhl_ref, br_ref, tri_ref, h2t_ref, rt_ref, cnt_ref, carry_ref):
    mix = _dot(attn, wmix_ref[0:ATTN_WIDTH, :]) + _dot(conv_ref[...], wmix_ref[ATTN_WIDTH:, :])
    h1 = _layer_norm(ALPHA * x_ref[...] + mix, g1_ref[...], b1_ref[...])
    scores = _dot(_bf16(h1), wqk_ref[...])
    probs = []
    for h in range(MEM_HEADS):
        s = scores[:, h * N_MEM:(h + 1) * N_MEM]
        p = jnp.exp(s - jnp.max(s, axis=1, keepdims=True))
        probs.append(_bf16(p / jnp.sum(p, axis=1, keepdims=True)))
    mem_out = _dot(jnp.concatenate(probs, axis=1), wvo_ref[...])
    h2 = _layer_norm(ALPHA * h1 + mem_out, g2_ref[...], b2_ref[...])
    _store_token_tiles(h2t_ref, h2)
    route, carry = _route(h2, wrhl_ref, br_ref, tri_ref, carry_ref[0:1, :])
    rt_ref[...] = route
    carry_ref[...] = jnp.broadcast_to(carry, carry_ref.shape)
    cnt_ref[...] = jnp.broadcast_to(carry, cnt_ref.shape)


def _swa_post_kernel(sinks_ref, q_ref, kc_ref, kp_ref, vc_ref, vp_ref,
                     conv_ref, x_ref, wmix_ref, g1_ref, b1_ref, wqk_ref, wvo_ref, g2_ref, b2_ref,
                     wrhl_ref, br_ref, tri_ref, h2s_ref, rts_ref, cnts_ref,
                     h2t_ref, rt_ref, cnt_ref, carry_ref, attn_s):
    t = pl.program_id(0)
    steps = N_PROMPT // TM_POST
    par = t % 2

    def swa(slot):
        def store(rows, cols, val):
            attn_s[slot, rows, cols] = val
        _swa_tile(t, sinks_ref, q_ref, kc_ref, kp_ref, vc_ref, vp_ref, store)

    def post(slot):
        _post_tile(attn_s[slot], conv_ref, x_ref, wmix_ref, g1_ref, b1_ref, wqk_ref, wvo_ref, g2_ref, b2_ref,
                   wrhl_ref, br_ref, tri_ref, h2t_ref, rt_ref, cnt_ref, carry_ref)

    @pl.when(t == 0)
    def _():
        carry_ref[...] = cnts_ref[...]
        swa(0)

    @pl.when((t >= 1) & (t < steps))
    def _():
        swa(par)
        post(1 - par)

    @pl.when(t == steps)
    def _():
        post(1 - par)

    @pl.when(t == steps + 1)
    def _():
        h2t_ref[...] = h2s_ref[...]
        rt_ref[...] = rts_ref[...]


def _swa_post(sinks, q, kx, vx, conv, x, wmix_b, g1, b1, wqk, wvo, g2, b2, wrhl, br, tri, h2t_s, rt_s, cnt_s):
    n = x.shape[0]
    tm = TM_POST
    assert tm == SWA_QB * WINDOW
    steps = n // tm
    cur = lambda w: pl.BlockSpec((tm, w), lambda i: (jnp.minimum(i, steps - 1), 0))
    prev = lambda w: pl.BlockSpec((WINDOW, w), lambda i: (jnp.clip(SWA_QB * i - 1, 0, n // WINDOW - 1), 0))
    lag = lambda w: pl.BlockSpec((tm, w), lambda i: (jnp.clip(i - 1, 0, steps - 1), 0))
    full = lambda a: pl.BlockSpec(a.shape, lambda i: (0,) * a.ndim)
    weights = (wmix_b, g1, b1, wqk, wvo, g2, b2, wrhl, br, tri, h2t_s, rt_s, cnt_s)
    n_out = n + h2t_s.shape[0] // ROW_CHUNKS
    out_idx = lambda i: (jnp.where(i > steps, steps, jnp.clip(i - 1, 0, steps - 1)), 0)
    return pl.pallas_call(
        _swa_post_kernel,
        grid=(steps + 2,),
        in_specs=([pl.BlockSpec(memory_space=pltpu.SMEM), cur(ATTN_WIDTH),
                   cur(4 * LANES), prev(4 * LANES), cur(4 * LANES), prev(4 * LANES),
                   lag(CONV_CH), lag(D_MODEL)] + [full(a) for a in weights]),
        out_specs=[pl.BlockSpec((tm * ROW_CHUNKS, LANES), out_idx),
                   pl.BlockSpec((tm, LANES), out_idx),
                   pl.BlockSpec((SUBLANES, LANES), lambda i: (0, 0))],
        out_shape=[jax.ShapeDtypeStruct((n_out * ROW_CHUNKS, LANES), jnp.float32),
                   jax.ShapeDtypeStruct((n_out, LANES), jnp.float32),
                   jax.ShapeDtypeStruct((SUBLANES, LANES), jnp.float32)],
        scratch_shapes=[pltpu.VMEM((SUBLANES, LANES), jnp.float32),
                        pltpu.VMEM((2, tm, ATTN_WIDTH), jnp.bfloat16)],
        compiler_params=_params(("arbitrary",)),
        name="swa_post_prompt",
    )(sinks, q, kx, kx, vx, vx, conv, x, *weights)


def _post_a_sample_kernel(attn_ref, conv_ref, x_ref, wmix_ref, g1_ref, b1_ref, wq_ref, h1_ref, qm_ref):
    h1 = _mix_ln1(attn_ref, conv_ref, x_ref, wmix_ref, g1_ref, b1_ref)
    h1_ref[...] = h1
    qm_ref[...] = _mem_q(h1, wq_ref).astype(jnp.float32)


def _post_a_sample(attn, conv, x, wmix_b, g1, b1, wq_b):
    n = x.shape[0]
    args = (attn, conv, x, wmix_b, g1, b1, wq_b)
    full = lambda a: pl.BlockSpec(a.shape, lambda i: (0,) * a.ndim)
    blk = pl.BlockSpec((n, D_MODEL), lambda i: (0, 0))
    return pl.pallas_call(
        _post_a_sample_kernel,
        grid=(1,),
        in_specs=[full(a) for a in args],
        out_specs=[blk, blk],
        out_shape=[jax.ShapeDtypeStruct((n, D_MODEL), jnp.float32),
                   jax.ShapeDtypeStruct((n, D_MODEL), jnp.float32)],
        compiler_params=_params(("arbitrary",)),
        name="post_a_sample",
    )(*args)


MEM_ROWS = MEM_HEADS * DEC_SEQ


def _mem_attn_sample_body(q_ref, mk_ref, mv_ref, o_ref):
    nk = N_MEM * MEM_HEADS
    row_h = lax.broadcasted_iota(jnp.int32, (MEM_ROWS, nk), 0) // DEC_SEQ
    key_h = lax.broadcasted_iota(jnp.int32, (MEM_ROWS, nk), 1) % MEM_HEADS
    own = row_h == key_h
    for b in range(SAMPLE_BB):
        k2 = _bf16(mk_ref[b].reshape(nk, MEM_HEAD_DIM))
        v2 = _bf16(mv_ref[b].reshape(nk, MEM_HEAD_DIM))
        s = jnp.where(own, _dot_nt(_bf16(q_ref[b]), k2), -jnp.inf)
        m = jnp.max(s, axis=1, keepdims=True)
        p = jnp.exp(s - m)
        den = jnp.sum(p, axis=1, keepdims=True)
        o_ref[b] = _dot(_bf16(p), v2) / den


def _proj_mem_kernel(x_ref, w_ref, tab_ref, cw_ref, mq_ref, mk_ref, mv_ref,
                     q_ref, kx_ref, vx_ref, conv_ref, ktail_ref, vtail_ref, utail_ref, mo_ref, carry_ref):
    _mem_attn_sample_body(mq_ref, mk_ref, mv_ref, mo_ref)
    _proj_prompt_body(x_ref, w_ref, tab_ref, cw_ref,
                      q_ref, kx_ref, vx_ref, conv_ref, ktail_ref, vtail_ref, utail_ref, carry_ref)


def _proj_prompt_mem_sample(x, w_in_b, tab, conv_w, qm, mk, mv):
    n = x.shape[0]
    nb = mk.shape[0]
    bb = SAMPLE_BB
    steps = nb // bb
    tm = n // steps
    row = lambda w: pl.BlockSpec((tm, w), lambda i: (i, 0))
    full = lambda a: pl.BlockSpec(a.shape, lambda i: (0,) * a.ndim)
    const = lambda r, w: pl.BlockSpec((r, w), lambda i: (0, 0))
    mq = qm.reshape(nb, DEC_SEQ, MEM_HEADS, MEM_HEAD_DIM).transpose(0, 2, 1, 3).reshape(nb, MEM_ROWS, MEM_HEAD_DIM)
    mrows = pl.BlockSpec((bb, MEM_ROWS, MEM_HEAD_DIM), lambda i: (i, 0, 0))
    kv = pl.BlockSpec((bb, N_MEM, MEM_HEADS, MEM_HEAD_DIM), lambda i: (i, 0, 0, 0))
    outs = pl.pallas_call(
        _proj_mem_kernel,
        grid=(steps,),
        in_specs=[row(D_MODEL), full(w_in_b), pl.BlockSpec((ROPE_ROWS, tm), lambda i: (0, i)), full(conv_w),
                  mrows, kv, kv],
        out_specs=[row(ATTN_WIDTH), row(4 * LANES), row(4 * LANES), row(CONV_CH),
                   const(WINDOW, KV_WIDTH), const(WINDOW, KV_WIDTH), const(SUBLANES, CONV_CH), mrows],
        out_shape=[jax.ShapeDtypeStruct((n, ATTN_WIDTH), jnp.bfloat16),
                   jax.ShapeDtypeStruct((n, 4 * LANES), jnp.bfloat16),
                   jax.ShapeDtypeStruct((n, 4 * LANES), jnp.bfloat16),
                   jax.ShapeDtypeStruct((n, CONV_CH), jnp.bfloat16),
                   jax.ShapeDtypeStruct((WINDOW, KV_WIDTH), jnp.float32),
                   jax.ShapeDtypeStruct((WINDOW, KV_WIDTH), jnp.float32),
                   jax.ShapeDtypeStruct((SUBLANES, CONV_CH), jnp.float32),
                   jax.ShapeDtypeStruct((nb, MEM_ROWS, MEM_HEAD_DIM), jnp.float32)],
        scratch_shapes=[pltpu.VMEM((SUBLANES, CONV_CH), jnp.float32)],
        compiler_params=_params(("arbitrary",)),
        name="proj_prompt_mem_sample",
    )(x, w_in_b, tab, conv_w, mq, mk, mv)
    o = outs[7].reshape(nb, MEM_HEADS, DEC_SEQ, MEM_HEAD_DIM).transpose(0, 2, 1, 3).reshape(nb * DEC_SEQ, D_MODEL)
    return outs[:7], o


def _post_b_sample_kernel(o_ref, h1_ref, wo_ref, g2_ref, b2_ref, wrhl_ref, br_ref, tri_ref,
                          h2t_ref, rt_ref, cnt_ref):
    h2 = _layer_norm(ALPHA * h1_ref[...] + _dot(_bf16(o_ref[...]), wo_ref[...]), g2_ref[...], b2_ref[...])
    _store_token_tiles(h2t_ref, h2)
    route, carry = _route(h2, wrhl_ref, br_ref, tri_ref, jnp.zeros((1, LANES), jnp.float32))
    rt_ref[...] = route
    cnt_ref[...] = jnp.broadcast_to(carry, cnt_ref.shape)


def _post_b_sample(o, h1, wo_b, g2, b2, wrhl, br, tri):
    n = h1.shape[0]
    args = (o, h1, wo_b, g2, b2, wrhl, br, tri)
    full = lambda a: pl.BlockSpec(a.shape, lambda i: (0,) * a.ndim)
    return pl.pallas_call(
        _post_b_sample_kernel,
        grid=(1,),
        in_specs=[full(a) for a in args],
        out_specs=[pl.BlockSpec((n * ROW_CHUNKS, LANES), lambda i: (0, 0)),
                   pl.BlockSpec((n, LANES), lambda i: (0, 0)),
                   pl.BlockSpec((SUBLANES, LANES), lambda i: (0, 0))],
        out_shape=[jax.ShapeDtypeStruct((n * ROW_CHUNKS, LANES), jnp.float32),
                   jax.ShapeDtypeStruct((n, LANES), jnp.float32),
                   jax.ShapeDtypeStruct((SUBLANES, LANES), jnp.float32)],
        compiler_params=_params(("arbitrary",)),
        name="post_b_sample",
    )(*args)


def _row_gather_copy(src_hbm, idx, dst, dst_row, sem):
    s0 = pl.multiple_of(idx * ROW_CHUNKS, ROW_CHUNKS)
    d0 = pl.multiple_of(dst_row * ROW_CHUNKS, ROW_CHUNKS)
    return pltpu.make_async_copy(src_hbm.at[pl.ds(s0, ROW_CHUNKS), :], dst.at[pl.ds(d0, ROW_CHUNKS), :], sem)


def _dispatch_kernel(pos_ref, h2t_ref, xs_hbm, sem):
    def body(r, c):
        src = h2t_ref.at[pl.ds(pl.multiple_of(r * ROW_CHUNKS, ROW_CHUNKS), ROW_CHUNKS), :]
        for k in range(2):
            d0 = pl.multiple_of(pos_ref[0, 0, k * TM_COMB + r] * ROW_CHUNKS, ROW_CHUNKS)
            pltpu.make_async_copy(src, xs_hbm.at[pl.ds(d0, ROW_CHUNKS), :], sem.at[0]).start(priority=k)
        return c
    lax.fori_loop(0, TM_COMB, body, 0, unroll=8)
    for _ in range(2):
        pltpu.make_async_copy(h2t_ref, xs_hbm.at[pl.ds(0, TM_COMB * ROW_CHUNKS), :], sem.at[0]).wait()


def _dispatch(pos3, h2t):
    nt = N_ALL // TM_COMB
    return pl.pallas_call(
        _dispatch_kernel,
        grid=(nt,),
        in_specs=[pl.BlockSpec((1, 1, 2 * TM_COMB), lambda i: (i, 0, 0), memory_space=pltpu.SMEM),
                  pl.BlockSpec((TM_COMB * ROW_CHUNKS, LANES), lambda i: (i, 0))],
        out_specs=pl.BlockSpec(memory_space=pl.ANY),
        out_shape=jax.ShapeDtypeStruct((N_ASSIGN * ROW_CHUNKS, LANES), jnp.float32),
        scratch_shapes=[pltpu.SemaphoreType.DMA((1,))],
        compiler_params=_params(("arbitrary",)),
        name="moe_dispatch",
    )(pos3, h2t)


def _moe_ffn_kernel(it_ref, ie_ref, lo_ref, hi_ref, x_ref, wg_ref, wu_ref, wd_ref, y_ref, wgb, wub, wdb, cur_e):
    i = pl.program_id(0)
    lo = lo_ref[i]
    hi = hi_ref[i]
    e = ie_ref[i]

    @pl.when(i == 0)
    def _():
        cur_e[0] = -1

    @pl.when((hi > lo) & (cur_e[0] != e))
    def _():
        wgb[...] = _bf16(wg_ref[0])
        wub[...] = _bf16(wu_ref[0])
        wdb[...] = _bf16(wd_ref[0])
        cur_e[0] = e

    def ffn():
        x = _bf16(_load_token_tiles(x_ref, 0, TM_MOE))
        hg = _dot(x, wgb[...])
        hu = _dot(x, wub[...])
        h = hg / (1.0 + jnp.exp(-hg)) * hu
        return _dot(_bf16(h), wdb[...])

    def rows_mask():
        row = lax.broadcasted_iota(jnp.int32, (TM_MOE, LANES), 0)
        return (row >= lo) & (row < hi)

    @pl.when((hi > lo) & (lo == 0))
    def _():
        y = ffn()
        mask = rows_mask()
        for c in range(ROW_CHUNKS):
            y_ref[pl.ds(c, TM_MOE, stride=ROW_CHUNKS), :] = jnp.where(mask, y[:, c * LANES:(c + 1) * LANES], 0.0)

    @pl.when((hi > lo) & (lo > 0))
    def _():
        y = ffn()
        mask = rows_mask()
        for c in range(ROW_CHUNKS):
            sl = pl.ds(c, TM_MOE, stride=ROW_CHUNKS)
            y_ref[sl, :] = jnp.where(mask, y[:, c * LANES:(c + 1) * LANES], y_ref[sl, :])


def _moe_ffn(item_tile, item_expert, item_lo, item_hi, x_sorted, w_gate, w_up, w_down):
    wspec = lambda shp: pl.BlockSpec((1,) + shp, lambda i, it, ie, lo, hi: (ie[i], 0, 0))
    tile = pl.BlockSpec((TM_MOE * ROW_CHUNKS, LANES), lambda i, it, ie, lo, hi: (it[i], 0))
    grid_spec = pltpu.PrefetchScalarGridSpec(
        num_scalar_prefetch=4,
        grid=(MOE_ITEMS,),
        in_specs=[tile, wspec((D_MODEL, EXPERT_FF)), wspec((D_MODEL, EXPERT_FF)), wspec((EXPERT_FF, D_MODEL))],
        out_specs=tile,
        scratch_shapes=[pltpu.VMEM((D_MODEL, EXPERT_FF), jnp.bfloat16),
                        pltpu.VMEM((D_MODEL, EXPERT_FF), jnp.bfloat16),
                        pltpu.VMEM((EXPERT_FF, D_MODEL), jnp.bfloat16),
                        pltpu.SMEM((1,), jnp.int32)],
    )
    return pl.pallas_call(
        _moe_ffn_kernel,
        grid_spec=grid_spec,
        out_shape=jax.ShapeDtypeStruct((N_ASSIGN * ROW_CHUNKS, LANES), jnp.float32),
        compiler_params=_params(("arbitrary",)),
        name="moe_ffn",
    )(item_tile, item_expert, item_lo, item_hi, x_sorted, w_gate, w_up, w_down)


def _combine_kernel(nt, pos_cur_ref, pos_nxt_ref, yt_hbm, h2t_ref, rt_ref, g3_ref, b3_ref, o_ref, abuf, sem):
    t = pl.program_id(0)
    slot = t % 2
    rows = 2 * TM_COMB

    def issue(pos_ref, s):
        def body(j, c):
            for k in range(2):
                r = 2 * j + k
                _row_gather_copy(yt_hbm, pos_ref[0, 0, r], abuf, s * rows + r, sem.at[s]).start(priority=k)
            return c
        lax.fori_loop(0, rows // 2, body, 0, unroll=4)

    @pl.when(t == 0)
    def _():
        issue(pos_cur_ref, 0)

    @pl.when(t + 1 < nt)
    def _():
        issue(pos_nxt_ref, 1 - slot)

    base = pl.multiple_of(slot * (rows * ROW_CHUNKS), rows * ROW_CHUNKS)
    pltpu.make_async_copy(yt_hbm.at[pl.ds(0, rows * ROW_CHUNKS), :],
                          abuf.at[pl.ds(base, rows * ROW_CHUNKS), :], sem.at[slot]).wait()
    ya = _load_token_tiles(abuf, base, TM_COMB)
    yb = _load_token_tiles(abuf, base + TM_COMB * ROW_CHUNKS, TM_COMB)
    rt = rt_ref[...]
    ff = rt[:, 2:3] * ya + rt[:, 3:4] * yb
    h2 = _load_token_tiles(h2t_ref, 0, TM_COMB)
    o_ref[...] = _layer_norm(ALPHA * h2 + ff, g3_ref[...], b3_ref[...])


def _combine(pos3, yt, h2t, rt, g3, b3, tile0, n_tiles):
    last = tile0 + n_tiles - 1
    smem_pos = lambda f: pl.BlockSpec((1, 1, 2 * TM_COMB), f, memory_space=pltpu.SMEM)
    full = lambda a: pl.BlockSpec(a.shape, lambda i: (0,) * a.ndim)
    return pl.pallas_call(
        functools.partial(_combine_kernel, n_tiles),
        grid=(n_tiles,),
        in_specs=[smem_pos(lambda i: (tile0 + i, 0, 0)),
                  smem_pos(lambda i: (jnp.minimum(tile0 + i + 1, last), 0, 0)),
                  pl.BlockSpec(memory_space=pl.ANY),
                  pl.BlockSpec((TM_COMB * ROW_CHUNKS, LANES), lambda i: (tile0 + i, 0)),
                  pl.BlockSpec((TM_COMB, LANES), lambda i: (tile0 + i, 0)),
                  full(g3), full(b3)],
        out_specs=pl.BlockSpec((TM_COMB, D_MODEL), lambda i: (i, 0)),
        out_shape=jax.ShapeDtypeStruct((n_tiles * TM_COMB, D_MODEL), jnp.float32),
        scratch_shapes=[pltpu.VMEM((2 * 2 * TM_COMB * ROW_CHUNKS, LANES), jnp.float32),
                        pltpu.SemaphoreType.DMA((2,))],
        compiler_params=_params(("arbitrary",)),
        name="moe_combine",
    )(pos3, pos3, yt, h2t, rt, g3, b3)


POS_TILES = 3


def _positions_kernel(rt_ref, starts_ref, pos_ref):
    lane = lax.broadcasted_iota(jnp.int32, (TM_COMB, LANES), 1)
    lane_f = lane.astype(jnp.float32)
    starts = starts_ref[0:1, :]
    for j in range(POS_TILES):
        rt = rt_ref[j * TM_COMB:(j + 1) * TM_COMB, :]
        cols = []
        for k in range(2):
            seg = jnp.sum(jnp.where(lane_f == rt[:, k:k + 1] + N_GROUPS, starts, 0.0), axis=1, keepdims=True)
            cols.append(seg + rt[:, 4 + k:5 + k])
        packed = jnp.where(lane == 0, cols[0], jnp.where(lane == 1, cols[1], 0.0))
        rows = packed.T
        pos_ref[j] = jnp.concatenate([rows[0:1, :], rows[1:2, :]], axis=1).astype(jnp.int32)


def _positions(rt, starts_row):
    nt = N_ALL // TM_COMB
    assert nt % POS_TILES == 0
    return pl.pallas_call(
        _positions_kernel,
        grid=(nt // POS_TILES,),
        in_specs=[pl.BlockSpec((POS_TILES * TM_COMB, LANES), lambda i: (i, 0)),
                  pl.BlockSpec((SUBLANES, LANES), lambda i: (0, 0))],
        out_specs=pl.BlockSpec((POS_TILES, 1, 2 * TM_COMB), lambda i: (i, 0, 0)),
        out_shape=jax.ShapeDtypeStruct((nt, 1, 2 * TM_COMB), jnp.int32),
        compiler_params=_params(("arbitrary",)),
        name="moe_positions",
    )(rt, starts_row)


def _routing_plan(rt, cnt):
    i32 = jnp.int32
    counts_f = cnt[0, N_GROUPS:N_GROUPS + N_EXPERTS]
    starts_f = jnp.cumsum(counts_f) - counts_f
    starts_row = jnp.broadcast_to(
        jnp.pad(starts_f, (N_GROUPS, LANES - N_GROUPS - N_EXPERTS))[None, :], (SUBLANES, LANES))
    pos3 = _positions(rt, starts_row)
    starts = starts_f.astype(i32)
    tiles = jnp.arange(MOE_TILES, dtype=i32) * TM_MOE
    rank_t = jnp.arange(MOE_TILES, dtype=i32) + jnp.sum((starts[None, :] < tiles[:, None]).astype(i32), axis=1)
    rank_s = jnp.arange(N_EXPERTS, dtype=i32) + jnp.sum((tiles[None, :] <= starts[:, None]).astype(i32), axis=1)
    vals = jnp.concatenate([tiles, starts])
    ranks = jnp.concatenate([rank_t, rank_s])
    slot = jnp.arange(MOE_ITEMS, dtype=i32)
    lo = jnp.sum(jnp.where(ranks[None, :] == slot[:, None], vals[None, :], 0), axis=1)
    hi = jnp.concatenate([lo[1:], jnp.full((1,), N_ASSIGN, i32)])
    item_tile = jnp.minimum(lo // TM_MOE, MOE_TILES - 1)
    item_expert = jnp.clip(jnp.sum((starts[None, :] <= lo[:, None]).astype(i32), axis=1) - 1, 0, N_EXPERTS - 1)
    base = item_tile * TM_MOE
    return item_tile, item_expert, lo - base, hi - base, pos3


def kernel(x_prompt, x_sample, mem_prompt, cache_swa_k, cache_swa_v, cache_conv, cache_mem_k, cache_mem_v,
           w_in, sinks, conv_w, w_mix_out, ln1_g, ln1_b, w_q_mem, w_k_mem, w_v_mem, w_o_mem, ln2_g, ln2_b,
           w_router_group, b_router_group, w_router_expert, b_router_expert, w_gate, w_up, w_down,
           ln3_g, ln3_b):
    f32 = jnp.float32
    row = lambda a: a.reshape(1, -1).astype(f32)
    w_in_b, wmix_b, wq_b, wk_b, wv_b, wo_b = (_bf16(w) for w in (w_in, w_mix_out, w_q_mem, w_k_mem, w_v_mem, w_o_mem))
    g1, b1, g2, b2, g3, b3 = (row(a) for a in (ln1_g, ln1_b, ln2_g, ln2_b, ln3_g, ln3_b))
    pad = LANES - N_GROUPS - N_EXPERTS
    wr = jnp.concatenate([w_router_group, w_router_expert, jnp.zeros((D_MODEL, pad), f32)], axis=1)
    wrh = _bf16(wr)
    wrhl = jnp.concatenate([wrh, _bf16(wr - wrh.astype(f32))], axis=1)
    br = jnp.concatenate([b_router_group, b_router_expert, jnp.zeros((pad,), f32)]).reshape(1, LANES)

    xs = x_sample.reshape(N_SAMPLE, D_MODEL)
    tab_s = jnp.tile(_rope_table(PAST_LEN + jnp.arange(DEC_SEQ)), (1, DEC_BATCH))
    c0 = jnp.repeat(cache_conv[:, 0], DEC_SEQ, axis=0)
    c1 = jnp.repeat(cache_conv[:, 1], DEC_SEQ, axis=0)
    q_s, k_s, v_s, conv_s, u_s = _proj_sample(xs, w_in_b, tab_s, conv_w, c0, c1)
    attn_s, swa_k_s, swa_v_s = _swa_sample(sinks, q_s, k_s, v_s, cache_swa_k, cache_swa_v)
    h1_s, qm_s = _post_a_sample(attn_s, conv_s, xs, wmix_b, g1, b1, wq_b)
    xp = x_prompt.reshape(N_PROMPT, D_MODEL)
    tab_p = _rope_table(jnp.arange(N_PROMPT))
    (q_p, kx_p, vx_p, conv_p, k_tail, v_tail, u_tail), o_s = _proj_prompt_mem_sample(
        xp, w_in_b, tab_p, conv_w, qm_s, cache_mem_k, cache_mem_v)
    tri =_bf16(jnp.tril(jnp.ones((TM_POST, TM_POST), f32), -1))
    h2t_s, rt_s, cnt_s = _post_b_sample(o_s, h1_s, wo_b, g2, b2, wrhl, br, tri)

    mk, mv, wqk, wvo = _mem_kv(mem_prompt.reshape(N_MEM, D_MODEL), wk_b, wv_b, wq_b, wo_b)
    h2t, rt, cnt = _swa_post(sinks, q_p, kx_p, vx_p, conv_p, xp, wmix_b, g1, b1, wqk, wvo, g2, b2,
                             wrhl, br, tri, h2t_s, rt_s, cnt_s)

    item_tile, item_expert, item_lo, item_hi, pos3 = _routing_plan(rt, cnt)
    x_sorted = _dispatch(pos3, h2t)
    yt = _moe_ffn(item_tile, item_expert, item_lo, item_hi, x_sorted, w_gate, w_up, w_down)
    y_p = _combine(pos3, yt, h2t, rt, g3, b3, 0, N_PROMPT // TM_COMB)
    y_s = _combine(pos3, yt, h2t, rt, g3, b3, N_PROMPT // TM_COMB, N_SAMPLE // TM_COMB)

    return (y_p.reshape(1, SEQ, D_MODEL),
            y_s.reshape(DEC_BATCH, DEC_SEQ, D_MODEL),
            k_tail.reshape(1, WINDOW, N_KV_HEADS, HEAD_DIM),
            v_tail.reshape(1, WINDOW, N_KV_HEADS, HEAD_DIM),
            u_tail[SUBLANES - (CONV_K - 1):].reshape(1, CONV_K - 1, CONV_CH),
            mk.reshape(1, N_MEM, MEM_HEADS, MEM_HEAD_DIM),
            mv.reshape(1, N_MEM, MEM_HEADS, MEM_HEAD_DIM),
            swa_k_s.reshape(DEC_BATCH, WINDOW, N_KV_HEADS, HEAD_DIM),
            swa_v_s.reshape(DEC_BATCH, WINDOW, N_KV_HEADS, HEAD_DIM),
            u_s.reshape(DEC_BATCH, DEC_SEQ, CONV_CH)[:, DEC_SEQ - (CONV_K - 1):])
```

```python
import functools

import jax
import jax.numpy as jnp
from jax import lax
from jax.experimental import pallas as pl
from jax.experimental.pallas import tpu as pltpu

D_MODEL = 1024
SEQ = 16384
DEC_BATCH = 128
DEC_SEQ = 4
PAST_LEN = 16384
ATTN_WIDTH = 512
CONV_CH = 512
HEAD_DIM = 64
N_HEADS = 8
N_KV_HEADS = 2
KV_WIDTH = 128
WINDOW = 128
ROPE_THETA = 500000.0
ROPE_DIM = 16
CONV_K = 3
Q_END = ATTN_WIDTH
K_END = Q_END + KV_WIDTH
V_END = K_END + KV_WIDTH
B_END = V_END + CONV_CH
C_END = B_END + CONV_CH
IN_WIDTH = C_END + CONV_CH
N_MEM = 256
MEM_HEADS = 4
MEM_HEAD_DIM = 256
N_GROUPS = 4
EXPERTS_PER_GROUP = 8
N_EXPERTS = 32
EXPERT_FF = 256
ALPHA = 2.0 ** 0.25
LN_EPS = 1e-5

LANES = 128
SUBLANES = 8
ROW_CHUNKS = D_MODEL // LANES
VMEM_LIMIT = 56 * 1024 * 1024

N_PROMPT = SEQ
N_SAMPLE = DEC_BATCH * DEC_SEQ
N_ALL = N_PROMPT + N_SAMPLE
TM_POST = 512
TM_MOE = 512
TM_COMB = 512
N_ASSIGN = 2 * N_ALL
MOE_TILES = N_ASSIGN // TM_MOE
MOE_ITEMS = MOE_TILES + N_EXPERTS
SAMPLE_BB = 4
SWA_BB = 16

assert ROW_CHUNKS == SUBLANES
assert N_SAMPLE == TM_POST
assert N_ASSIGN % TM_MOE == 0 and N_ALL % TM_COMB == 0


def _params(sem, vmem=VMEM_LIMIT):
    return pltpu.CompilerParams(dimension_semantics=sem, vmem_limit_bytes=vmem)


def _bf16(x):
    return x.astype(jnp.bfloat16)


def _dot(a, b):
    return jnp.dot(a, b, preferred_element_type=jnp.float32)


def _dot_nt(a, b):
    return lax.dot_general(a, b, (((1,), (1,)), ((), ())), preferred_element_type=jnp.float32)


def _layer_norm(x, g, b):
    mu = jnp.mean(x, axis=-1, keepdims=True)
    xc = x - mu
    var = jnp.mean(xc * xc, axis=-1, keepdims=True)
    return xc * lax.rsqrt(var + LN_EPS) * g + b


def _rope(x, cos_t, sin_t):
    lane = lax.broadcasted_iota(jnp.int32, x.shape, 1) % HEAD_DIM
    half = ROPE_DIM // 2
    partner = jnp.where(lane < half, pltpu.roll(x, LANES - half, axis=1), pltpu.roll(x, half, axis=1))
    return x * cos_t + partner * sin_t


def _head_slabs(x):
    lane = lax.broadcasted_iota(jnp.int32, x.shape, 1)
    lo = lane < HEAD_DIM
    sw = pltpu.roll(x, HEAD_DIM, axis=1)
    zero = jnp.zeros_like(x)
    slabs = [jnp.where(lo, x, zero), jnp.where(lo, zero, sw), jnp.where(lo, sw, zero), jnp.where(lo, zero, x)]
    return _bf16(jnp.concatenate(slabs, axis=1))


def _store_token_tiles(ref, val):
    rows = val.shape[0]
    for c in range(ROW_CHUNKS):
        ref[pl.ds(c, rows, stride=ROW_CHUNKS), :] = val[:, c * LANES:(c + 1) * LANES]


def _load_token_tiles(ref, base, rows):
    return jnp.concatenate(
        [ref[pl.ds(base + c, rows, stride=ROW_CHUNKS), :] for c in range(ROW_CHUNKS)], axis=1)


ROPE_ONE = 3 * (ROPE_DIM // 2)
ROPE_ROWS = 32


def _rope_patterns(tab):
    half = ROPE_DIM // 2
    m = lax.broadcasted_iota(jnp.int32, tab.shape, 1) % HEAD_DIM
    idx_c = jnp.where(m < ROPE_DIM, m % half, ROPE_ONE)
    idx_s = jnp.where(m < half, 2 * half + m, jnp.where(m < ROPE_DIM, m, ROPE_ONE + 1))
    return jnp.take_along_axis(tab, idx_c, axis=1), jnp.take_along_axis(tab, idx_s, axis=1)


def _proj_common(x_ref, w_ref, tab_ref):
    xb = _bf16(x_ref[...])
    tab = tab_ref[...]
    pad = jnp.zeros((LANES - tab.shape[0], tab.shape[1]), jnp.float32)
    cos_t, sin_t = _rope_patterns(jnp.concatenate([tab, pad], axis=0).T)
    q = _dot(xb, w_ref[:, 0:Q_END])
    q_rot = jnp.concatenate(
        [_rope(q[:, p * LANES:(p + 1) * LANES], cos_t, sin_t) for p in range(ATTN_WIDTH // LANES)], axis=1)
    q_out = _bf16(q_rot * (HEAD_DIM ** -0.5))
    kv = _dot(xb, w_ref[:, Q_END:V_END])
    k = _rope(kv[:, 0:KV_WIDTH], cos_t, sin_t)
    v = kv[:, KV_WIDTH:]
    bg = _dot(xb, w_ref[:, V_END:B_END])
    u = _dot(xb, w_ref[:, B_END:C_END]) * _dot(xb, w_ref[:, C_END:IN_WIDTH])
    return q_out, k, v, bg, u


def _conv3(bg, u, u1, u2, cw_ref):
    cw = cw_ref[...]
    return bg * (cw[0:1, :] * u2 + cw[1:2, :] * u1 + cw[2:3, :] * u)


def _proj_prompt_body(x_ref, w_ref, tab_ref, cw_ref,
                      q_ref, kx_ref, vx_ref, conv_ref, ktail_ref, vtail_ref, utail_ref, carry_ref):
    @pl.when(pl.program_id(0) == 0)
    def _():
        carry_ref[...] = jnp.zeros_like(carry_ref)

    q_out, k, v, bg, u = _proj_common(x_ref, w_ref, tab_ref)
    tm = u.shape[0]
    ext = jnp.concatenate([carry_ref[...], u], axis=0)
    u1 = pltpu.roll(ext, 1, axis=0)[SUBLANES:SUBLANES + tm]
    u2 = pltpu.roll(ext, 2, axis=0)[SUBLANES:SUBLANES + tm]
    q_ref[...] = q_out
    kx_ref[...] = _head_slabs(k)
    vx_ref[...] = _head_slabs(v)
    conv_ref[...] = _bf16(_conv3(bg, u, u1, u2, cw_ref))
    ktail_ref[...] = k[tm - WINDOW:tm]
    vtail_ref[...] = v[tm - WINDOW:tm]
    utail_ref[...] = u[tm - SUBLANES:tm]
    carry_ref[...] = u[tm - SUBLANES:tm]


def _proj_sample_kernel(x_ref, w_ref, tab_ref, cw_ref, c0_ref, c1_ref,
                        q_ref, k_ref, v_ref, conv_ref, u_ref):
    q_out, k, v, bg, u = _proj_common(x_ref, w_ref, tab_ref)
    t = lax.broadcasted_iota(jnp.int32, u.shape, 0) % DEC_SEQ
    c0 = c0_ref[...]
    c1 = c1_ref[...]
    u1 = jnp.where(t >= 1, pltpu.roll(u, 1, axis=0), c1)
    u2 = jnp.where(t >= 2, pltpu.roll(u, 2, axis=0), jnp.where(t == 1, c1, c0))
    q_ref[...] = q_out.astype(jnp.float32)
    k_ref[...] = k
    v_ref[...] = v
    conv_ref[...] = _bf16(_conv3(bg, u, u1, u2, cw_ref))
    u_ref[...] = u


def _rope_table(pos):
    half = ROPE_DIM // 2
    inv = ROPE_THETA ** (-jnp.arange(0, ROPE_DIM, 2, dtype=jnp.float32) / ROPE_DIM)
    ang = pos.astype(jnp.float32)[None, :] * inv[:, None]
    cos, sin = jnp.cos(ang), jnp.sin(ang)
    n = pos.shape[0]
    assert ROPE_ONE == 3 * half
    return jnp.concatenate([cos, sin, -sin, jnp.ones((1, n), jnp.float32),
                            jnp.zeros((ROPE_ROWS - ROPE_ONE - 1, n), jnp.float32)], axis=0)


def _proj_sample(x, w_in_b, tab, conv_w, c0, c1):
    n = x.shape[0]
    full = lambda a: pl.BlockSpec(a.shape, lambda i: (0,) * a.ndim)
    out = lambda w, dt: jax.ShapeDtypeStruct((n, w), dt)
    blk = lambda w: pl.BlockSpec((n, w), lambda i: (0, 0))
    return pl.pallas_call(
        _proj_sample_kernel,
        grid=(1,),
        in_specs=[full(x), full(w_in_b), full(tab), full(conv_w), full(c0), full(c1)],
        out_specs=[blk(ATTN_WIDTH), blk(KV_WIDTH), blk(KV_WIDTH), blk(CONV_CH), blk(CONV_CH)],
        out_shape=[out(ATTN_WIDTH, jnp.float32), out(KV_WIDTH, jnp.float32), out(KV_WIDTH, jnp.float32),
                   out(CONV_CH, jnp.bfloat16), out(CONV_CH, jnp.float32)],
        compiler_params=_params(("arbitrary",)),
        name="proj_sample",
    )(x, w_in_b, tab, conv_w, c0, c1)


def _sink_softmax_pv(s, valid, sink, vx):
    s = jnp.where(valid, s, -jnp.inf)
    m = jnp.maximum(jnp.max(s, axis=1, keepdims=True), sink)
    p = jnp.exp(s - m)
    den = jnp.sum(p, axis=1, keepdims=True) + jnp.exp(sink - m)
    return _dot(_bf16(p), vx) / den


SWA_QB = 4


def _swa_tile(step, sinks_ref, q_ref, kc_ref, kp_ref, vc_ref, vp_ref, store):
    kall = jnp.concatenate([kp_ref[...], kc_ref[...]], axis=0)
    vall = jnp.concatenate([vp_ref[...], vc_ref[...]], axis=0)
    i = lax.broadcasted_iota(jnp.int32, (WINDOW, 2 * WINDOW), 0)
    j = lax.broadcasted_iota(jnp.int32, (WINDOW, 2 * WINDOW), 1)
    band = (j > i) & (j <= i + WINDOW)
    for sb in range(SWA_QB):
        rows = slice(sb * WINDOW, (sb + 1) * WINDOW)
        kcat = kall[sb * WINDOW:(sb + 2) * WINDOW]
        vcat = vall[sb * WINDOW:(sb + 2) * WINDOW]
        valid = band & ((step > 0) | (j >= WINDOW)) if sb == 0 else band
        for p in range(N_HEADS // 2):
            qs = q_ref[rows, p * LANES:(p + 1) * LANES]
            acc = None
            for e in range(2):
                hd = 2 * p + e
                slab = 2 * (hd // (N_HEADS // N_KV_HEADS)) + e
                kx = kcat[:, slab * LANES:(slab + 1) * LANES]
                vx = vcat[:, slab * LANES:(slab + 1) * LANES]
                o = _sink_softmax_pv(_dot_nt(qs, kx), valid, sinks_ref[hd], vx)
                acc = o if acc is None else acc + o
            store(rows, slice(p * LANES, (p + 1) * LANES), _bf16(acc))


SWA_ROWS = N_HEADS * DEC_SEQ
NEW_ROWS = 2 * SUBLANES


def _swa_sample_kernel(q_ref, sink_ref, kn_ref, vn_ref, kt_ref, vt_ref, o_ref, okt_ref, ovt_ref):
    nb = SWA_BB
    rows = nb * SWA_ROWS
    t = lax.broadcasted_iota(jnp.int32, (rows, WINDOW), 0) % DEC_SEQ
    valid_c = lax.broadcasted_iota(jnp.int32, (rows, WINDOW), 1) > t
    valid_n = (lax.broadcasted_iota(jnp.int32, (rows, NEW_ROWS), 1)
               <= lax.broadcasted_iota(jnp.int32, (rows, NEW_ROWS), 0) % DEC_SEQ)
    sink = jnp.concatenate([sink_ref[:, 0:1]] * nb, axis=0)
    qs = [_bf16(q_ref[b]) for b in range(nb)]
    s_c = jnp.concatenate([_dot(qs[b], _bf16(kt_ref[b])) for b in range(nb)], axis=0)
    s_n = jnp.concatenate([_dot_nt(qs[b], _bf16(kn_ref[b])) for b in range(nb)], axis=0)
    s_c = jnp.where(valid_c, s_c, -jnp.inf)
    s_n = jnp.where(valid_n, s_n, -jnp.inf)
    m = jnp.maximum(jnp.maximum(jnp.max(s_c, axis=1, keepdims=True), jnp.max(s_n, axis=1, keepdims=True)), sink)
    p_c = jnp.exp(s_c - m)
    p_n = jnp.exp(s_n - m)
    rden = 1.0 / (jnp.sum(p_c, axis=1, keepdims=True) + jnp.sum(p_n, axis=1, keepdims=True) + jnp.exp(sink - m))
    p_c, p_n = _bf16(p_c), _bf16(p_n)
    lane = lax.broadcasted_iota(jnp.int32, (KV_WIDTH, WINDOW), 1)
    shift = WINDOW - DEC_SEQ
    zrows = jnp.zeros((KV_WIDTH - NEW_ROWS, KV_WIDTH), jnp.float32)
    for b in range(nb):
        r = slice(b * SWA_ROWS, (b + 1) * SWA_ROWS)
        kt, vt = kt_ref[b], vt_ref[b]
        kn, vn = kn_ref[b], vn_ref[b]
        o_ref[b] = (_dot_nt(p_c[r], _bf16(vt)) + _dot(p_n[r], _bf16(vn))) * rden[r]
        for old, new, dst in ((kt, kn, okt_ref), (vt, vn, ovt_ref)):
            new_cols = pltpu.roll(jnp.concatenate([new, zrows], axis=0).T, shift, axis=1)
            dst[b] = jnp.where(lane >= shift, new_cols, pltpu.roll(old, shift, axis=1))


def _swa_sample(sinks, q, kn, vn, cache_k, cache_v):
    nb = cache_k.shape[0]
    bb = SWA_BB
    groups = N_HEADS // N_KV_HEADS
    qh = q.reshape(nb, DEC_SEQ, N_KV_HEADS, groups, HEAD_DIM).transpose(0, 2, 3, 1, 4)
    qh = qh.reshape(nb, N_KV_HEADS, groups * DEC_SEQ, HEAD_DIM)
    zeros = jnp.zeros_like(qh[:, 0])
    qbd = jnp.concatenate([jnp.concatenate([qh[:, 0], zeros], axis=-1),
                           jnp.concatenate([zeros, qh[:, 1]], axis=-1)], axis=1)
    sink_col = jnp.broadcast_to(jnp.repeat(sinks, DEC_SEQ).reshape(SWA_ROWS, 1), (SWA_ROWS, LANES))
    pad8 = lambda a: jnp.pad(a.reshape(nb, DEC_SEQ, KV_WIDTH), ((0, 0), (0, NEW_ROWS - DEC_SEQ), (0, 0)))
    to_t = lambda c: c.transpose(0, 2, 3, 1).reshape(nb, KV_WIDTH, WINDOW)
    blk = lambda r, w: pl.BlockSpec((bb, r, w), lambda i: (i, 0, 0))
    o, okt, ovt = pl.pallas_call(
        _swa_sample_kernel,
        grid=(nb // bb,),
        in_specs=[blk(SWA_ROWS, KV_WIDTH), pl.BlockSpec((SWA_ROWS, LANES), lambda i: (0, 0)),
                  blk(NEW_ROWS, KV_WIDTH), blk(NEW_ROWS, KV_WIDTH), blk(KV_WIDTH, WINDOW), blk(KV_WIDTH, WINDOW)],
        out_specs=[blk(SWA_ROWS, KV_WIDTH), blk(KV_WIDTH, WINDOW), blk(KV_WIDTH, WINDOW)],
        out_shape=[jax.ShapeDtypeStruct((nb, SWA_ROWS, KV_WIDTH), jnp.float32),
                   jax.ShapeDtypeStruct((nb, KV_WIDTH, WINDOW), jnp.float32),
                   jax.ShapeDtypeStruct((nb, KV_WIDTH, WINDOW), jnp.float32)],
        compiler_params=_params(("arbitrary",)),
        name="swa_sample",
    )(qbd, sink_col, pad8(kn), pad8(vn), to_t(cache_k), to_t(cache_v))
    o = o.reshape(nb, N_KV_HEADS, groups, DEC_SEQ, N_KV_HEADS, HEAD_DIM)
    attn = jnp.stack([o[:, h, :, :, h, :] for h in range(N_KV_HEADS)], axis=1)
    attn = attn.transpose(0, 3, 1, 2, 4).reshape(nb * DEC_SEQ, ATTN_WIDTH)
    from_t = lambda c: c.reshape(nb, N_KV_HEADS, HEAD_DIM, WINDOW).transpose(0, 3, 1, 2)
    return attn, from_t(okt), from_t(ovt)


def _mem_kv_kernel(mem_ref, wk_ref, wv_ref, wq_ref, wo_ref, mk_ref, mv_ref, wqk_ref, wvo_ref):
    mb = _bf16(mem_ref[...])
    mk = _dot(mb, wk_ref[...])
    mv = _dot(mb, wv_ref[...])
    mk_ref[...] = mk
    mv_ref[...] = mv
    mkb, mvb = _bf16(mk), _bf16(mv)
    for h in range(MEM_HEADS):
        sl = slice(h * MEM_HEAD_DIM, (h + 1) * MEM_HEAD_DIM)
        keys = slice(h * N_MEM, (h + 1) * N_MEM)
        wqk_ref[:, keys] = _bf16(_dot_nt(wq_ref[:, sl], mkb[:, sl]) * (MEM_HEAD_DIM ** -0.5))
        wvo_ref[keys, :] = _bf16(_dot(mvb[:, sl], wo_ref[sl, :]))


def _mem_kv(mem, wk_b, wv_b, wq_b, wo_b):
    full = lambda a: pl.BlockSpec(a.shape, lambda i: (0,) * a.ndim)
    blk = pl.BlockSpec((N_MEM, D_MODEL), lambda i: (0, 0))
    f32 = jax.ShapeDtypeStruct((N_MEM, D_MODEL), jnp.float32)
    fused = (D_MODEL, MEM_HEADS * N_MEM), (MEM_HEADS * N_MEM, D_MODEL)
    return pl.pallas_call(
        _mem_kv_kernel,
        grid=(1,),
        in_specs=[full(mem), full(wk_b), full(wv_b), full(wq_b), full(wo_b)],
        out_specs=[blk, blk] + [pl.BlockSpec(shp, lambda i: (0, 0)) for shp in fused],
        out_shape=[f32, f32] + [jax.ShapeDtypeStruct(shp, jnp.bfloat16) for shp in fused],
        compiler_params=_params(("arbitrary",)),
        name="mem_kv",
    )(mem, wk_b, wv_b, wq_b, wo_b)


def _mix_ln1(attn_ref, conv_ref, x_ref, wmix_ref, g1_ref, b1_ref):
    mix = _dot(_bf16(attn_ref[...]), wmix_ref[0:ATTN_WIDTH, :]) + _dot(conv_ref[...], wmix_ref[ATTN_WIDTH:, :])
    return _layer_norm(ALPHA * x_ref[...] + mix, g1_ref[...], b1_ref[...])


def _mem_q(h1, wq_ref):
    return _bf16(_dot(_bf16(h1), wq_ref[...]) * (MEM_HEAD_DIM ** -0.5))


def _route(h2, wrhl_ref, br_ref, tri_ref, carry):
    hi = _bf16(h2)
    lo = _bf16(h2 - hi.astype(jnp.float32))
    hh = _dot(hi, wrhl_ref[...])
    logits = hh[:, 0:LANES] + hh[:, LANES:] + _dot(lo, wrhl_ref[:, 0:LANES]) + br_ref[...]
    lane_i = lax.broadcasted_iota(jnp.int32, logits.shape, 1)
    lane = lane_i.astype(jnp.float32)
    big = jnp.float32(LANES)
    is_g = lane_i < N_GROUPS
    gl = jnp.where(is_g, logits, -jnp.inf)
    gmax = jnp.max(gl, axis=1, keepdims=True)
    gidx = jnp.min(jnp.where(is_g & (logits == gmax), lane, big), axis=1, keepdims=True)
    gsum = jnp.sum(jnp.exp(gl - gmax), axis=1, keepdims=True)
    gw = 1.0 / gsum
    eid = lane_i - N_GROUPS
    assert EXPERTS_PER_GROUP == 8
    grp = lax.shift_right_arithmetic(eid, jnp.full_like(eid, 3)).astype(jnp.float32)
    in_e = (lane_i >= N_GROUPS) & (lane_i < N_GROUPS + N_EXPERTS) & (grp == gidx)
    v1 = jnp.max(jnp.where(in_e, logits, -jnp.inf), axis=1, keepdims=True)
    i1 = jnp.min(jnp.where(in_e & (logits == v1), lane, big), axis=1, keepdims=True)
    rest = in_e & (lane != i1)
    v2 = jnp.max(jnp.where(rest, logits, -jnp.inf), axis=1, keepdims=True)
    i2 = jnp.min(jnp.where(rest & (logits == v2), lane, big), axis=1, keepdims=True)
    ex = jnp.exp(v2 - v1)
    den = 1.0 + ex
    w1 = gw / den
    w2 = gw * ex / den
    zero = jnp.zeros_like(logits)
    pick1 = lane == i1
    pick2 = lane == i2
    sel = jnp.where(pick1 | pick2, 1.0, 0.0)
    before = _dot(tri_ref[...], _bf16(sel)) + carry
    rank1 = jnp.sum(jnp.where(pick1, before, zero), axis=1, keepdims=True)
    rank2 = jnp.sum(jnp.where(pick2, before, zero), axis=1, keepdims=True)
    cols = (i1 - N_GROUPS, i2 - N_GROUPS, w1, w2, rank1, rank2)
    route = zero
    for k, col in enumerate(cols):
        route = jnp.where(lane_i == k, col, route)
    return route, carry + jnp.sum(sel, axis=0, keepdims=True)


def _post_tile(attn, conv_ref, x_ref, wmix_ref, g1_ref, b1_ref, wqk_ref, wvo_ref, g2_ref, b2_ref,
               wrhl_ref, br_ref, tri_ref, h2t_ref, rt_ref, cnt_ref, carry_ref):
    mix = _dot(attn, wmix_ref[0:ATTN_WIDTH, :]) + _dot(conv_ref[...], wmix_ref[ATTN_WIDTH:, :])
    h1 = _layer_norm(ALPHA * x_ref[...] + mix, g1_ref[...], b1_ref[...])
    scores = _dot(_bf16(h1), wqk_ref[...])
    probs = []
    for h in range(MEM_HEADS):
        s = scores[:, h * N_MEM:(h + 1) * N_MEM]
        p = jnp.exp(s - jnp.max(s, axis=1, keepdims=True))
        probs.append(_bf16(p / jnp.sum(p, axis=1, keepdims=True)))
    mem_out = _dot(jnp.concatenate(probs, axis=1), wvo_ref[...])
    h2 = _layer_norm(ALPHA * h1 + mem_out, g2_ref[...], b2_ref[...])
    _store_token_tiles(h2t_ref, h2)
    route, carry = _route(h2, wrhl_ref, br_ref, tri_ref, carry_ref[0:1, :])
    rt_ref[...] = route
    carry_ref[...] = jnp.broadcast_to(carry, carry_ref.shape)
    cnt_ref[...] = jnp.broadcast_to(carry, cnt_ref.shape)


def _swa_post_kernel(sinks_ref, q_ref, kc_ref, kp_ref, vc_ref, vp_ref,
                     conv_ref, x_ref, wmix_ref, g1_ref, b1_ref, wqk_ref, wvo_ref, g2_ref, b2_ref,
                     wrhl_ref, br_ref, tri_ref, h2s_ref, rts_ref, cnts_ref,
                     h2t_ref, rt_ref, cnt_ref, carry_ref, attn_s):
    t = pl.program_id(0)
    steps = N_PROMPT // TM_POST
    par = t % 2

    def swa(slot):
        def store(rows, cols, val):
            attn_s[slot, rows, cols] = val
        _swa_tile(t, sinks_ref, q_ref, kc_ref, kp_ref, vc_ref, vp_ref, store)

    def post(slot):
        _post_tile(attn_s[slot], conv_ref, x_ref, wmix_ref, g1_ref, b1_ref, wqk_ref, wvo_ref, g2_ref, b2_ref,
                   wrhl_ref, br_ref, tri_ref, h2t_ref, rt_ref, cnt_ref, carry_ref)

    @pl.when(t == 0)
    def _():
        carry_ref[...] = cnts_ref[...]
        swa(0)

    @pl.when((t >= 1) & (t < steps))
    def _():
        swa(par)
        post(1 - par)

    @pl.when(t == steps)
    def _():
        post(1 - par)

    @pl.when(t == steps + 1)
    def _():
        h2t_ref[...] = h2s_ref[...]
        rt_ref[...] = rts_ref[...]


def _swa_post(sinks, q, kx, vx, conv, x, wmix_b, g1, b1, wqk, wvo, g2, b2, wrhl, br, tri, h2t_s, rt_s, cnt_s):
    n = x.shape[0]
    tm = TM_POST
    assert tm == SWA_QB * WINDOW
    steps = n // tm
    cur = lambda w: pl.BlockSpec((tm, w), lambda i: (jnp.minimum(i, steps - 1), 0))
    prev = lambda w: pl.BlockSpec((WINDOW, w), lambda i: (jnp.clip(SWA_QB * i - 1, 0, n // WINDOW - 1), 0))
    lag = lambda w: pl.BlockSpec((tm, w), lambda i: (jnp.clip(i - 1, 0, steps - 1), 0))
    full = lambda a: pl.BlockSpec(a.shape, lambda i: (0,) * a.ndim)
    weights = (wmix_b, g1, b1, wqk, wvo, g2, b2, wrhl, br, tri, h2t_s, rt_s, cnt_s)
    n_out = n + h2t_s.shape[0] // ROW_CHUNKS
    out_idx = lambda i: (jnp.where(i > steps, steps, jnp.clip(i - 1, 0, steps - 1)), 0)
    return pl.pallas_call(
        _swa_post_kernel,
        grid=(steps + 2,),
        in_specs=([pl.BlockSpec(memory_space=pltpu.SMEM), cur(ATTN_WIDTH),
                   cur(4 * LANES), prev(4 * LANES), cur(4 * LANES), prev(4 * LANES),
                   lag(CONV_CH), lag(D_MODEL)] + [full(a) for a in weights]),
        out_specs=[pl.BlockSpec((tm * ROW_CHUNKS, LANES), out_idx),
                   pl.BlockSpec((tm, LANES), out_idx),
                   pl.BlockSpec((SUBLANES, LANES), lambda i: (0, 0))],
        out_shape=[jax.ShapeDtypeStruct((n_out * ROW_CHUNKS, LANES), jnp.float32),
                   jax.ShapeDtypeStruct((n_out, LANES), jnp.float32),
                   jax.ShapeDtypeStruct((SUBLANES, LANES), jnp.float32)],
        scratch_shapes=[pltpu.VMEM((SUBLANES, LANES), jnp.float32),
                        pltpu.VMEM((2, tm, ATTN_WIDTH), jnp.bfloat16)],
        compiler_params=_params(("arbitrary",)),
        name="swa_post_prompt",
    )(sinks, q, kx, kx, vx, vx, conv, x, *weights)


def _post_a_sample_kernel(attn_ref, conv_ref, x_ref, wmix_ref, g1_ref, b1_ref, wq_ref, h1_ref, qm_ref):
    h1 = _mix_ln1(attn_ref, conv_ref, x_ref, wmix_ref, g1_ref, b1_ref)
    h1_ref[...] = h1
    qm_ref[...] = _mem_q(h1, wq_ref).astype(jnp.float32)


def _post_a_sample(attn, conv, x, wmix_b, g1, b1, wq_b):
    n = x.shape[0]
    args = (attn, conv, x, wmix_b, g1, b1, wq_b)
    full = lambda a: pl.BlockSpec(a.shape, lambda i: (0,) * a.ndim)
    blk = pl.BlockSpec((n, D_MODEL), lambda i: (0, 0))
    return pl.pallas_call(
        _post_a_sample_kernel,
        grid=(1,),
        in_specs=[full(a) for a in args],
        out_specs=[blk, blk],
        out_shape=[jax.ShapeDtypeStruct((n, D_MODEL), jnp.float32),
                   jax.ShapeDtypeStruct((n, D_MODEL), jnp.float32)],
        compiler_params=_params(("arbitrary",)),
        name="post_a_sample",
    )(*args)


MEM_ROWS = MEM_HEADS * DEC_SEQ


def _mem_attn_sample_body(q_ref, mk_ref, mv_ref, o_ref):
    nk = N_MEM * MEM_HEADS
    nb = SAMPLE_BB
    rows = nb * MEM_ROWS
    row_h = (lax.broadcasted_iota(jnp.int32, (rows, nk), 0) % MEM_ROWS) // DEC_SEQ
    key_h = lax.broadcasted_iota(jnp.int32, (rows, nk), 1) % MEM_HEADS
    s = jnp.concatenate(
        [_dot_nt(_bf16(q_ref[b]), _bf16(mk_ref[b].reshape(nk, MEM_HEAD_DIM))) for b in range(nb)], axis=0)
    s = jnp.where(row_h == key_h, s, -jnp.inf)
    p = jnp.exp(s - jnp.max(s, axis=1, keepdims=True))
    rden = 1.0 / jnp.sum(p, axis=1, keepdims=True)
    p = _bf16(p)
    for b in range(nb):
        r = slice(b * MEM_ROWS, (b + 1) * MEM_ROWS)
        o_ref[b] = _dot(p[r], _bf16(mv_ref[b].reshape(nk, MEM_HEAD_DIM))) * rden[r]


def _proj_mem_kernel(x_ref, w_ref, tab_ref, cw_ref, mq_ref, mk_ref, mv_ref,
                     q_ref, kx_ref, vx_ref, conv_ref, ktail_ref, vtail_ref, utail_ref, mo_ref, carry_ref):
    _mem_attn_sample_body(mq_ref, mk_ref, mv_ref, mo_ref)
    _proj_prompt_body(x_ref, w_ref, tab_ref, cw_ref,
                      q_ref, kx_ref, vx_ref, conv_ref, ktail_ref, vtail_ref, utail_ref, carry_ref)


def _proj_prompt_mem_sample(x, w_in_b, tab, conv_w, qm, mk, mv):
    n = x.shape[0]
    nb = mk.shape[0]
    bb = SAMPLE_BB
    steps = nb // bb
    tm = n // steps
    row = lambda w: pl.BlockSpec((tm, w), lambda i: (i, 0))
    full = lambda a: pl.BlockSpec(a.shape, lambda i: (0,) * a.ndim)
    const = lambda r, w: pl.BlockSpec((r, w), lambda i: (0, 0))
    mq = qm.reshape(nb, DEC_SEQ, MEM_HEADS, MEM_HEAD_DIM).transpose(0, 2, 1, 3).reshape(nb, MEM_ROWS, MEM_HEAD_DIM)
    mrows = pl.BlockSpec((bb, MEM_ROWS, MEM_HEAD_DIM), lambda i: (i, 0, 0))
    kv = pl.BlockSpec((bb, N_MEM, MEM_HEADS, MEM_HEAD_DIM), lambda i: (i, 0, 0, 0))
    outs = pl.pallas_call(
        _proj_mem_kernel,
        grid=(steps,),
        in_specs=[row(D_MODEL), full(w_in_b), pl.BlockSpec((ROPE_ROWS, tm), lambda i: (0, i)), full(conv_w),
                  mrows, kv, kv],
        out_specs=[row(ATTN_WIDTH), row(4 * LANES), row(4 * LANES), row(CONV_CH),
                   const(WINDOW, KV_WIDTH), const(WINDOW, KV_WIDTH), const(SUBLANES, CONV_CH), mrows],
        out_shape=[jax.ShapeDtypeStruct((n, ATTN_WIDTH), jnp.bfloat16),
                   jax.ShapeDtypeStruct((n, 4 * LANES), jnp.bfloat16),
                   jax.ShapeDtypeStruct((n, 4 * LANES), jnp.bfloat16),
                   jax.ShapeDtypeStruct((n, CONV_CH), jnp.bfloat16),
                   jax.ShapeDtypeStruct((WINDOW, KV_WIDTH), jnp.float32),
                   jax.ShapeDtypeStruct((WINDOW, KV_WIDTH), jnp.float32),
                   jax.ShapeDtypeStruct((SUBLANES, CONV_CH), jnp.float32),
                   jax.ShapeDtypeStruct((nb, MEM_ROWS, MEM_HEAD_DIM), jnp.float32)],
        scratch_shapes=[pltpu.VMEM((SUBLANES, CONV_CH), jnp.float32)],
        compiler_params=_params(("arbitrary",)),
        name="proj_prompt_mem_sample",
    )(x, w_in_b, tab, conv_w, mq, mk, mv)
    o = outs[7].reshape(nb, MEM_HEADS, DEC_SEQ, MEM_HEAD_DIM).transpose(0, 2, 1, 3).reshape(nb * DEC_SEQ, D_MODEL)
    return outs[:7], o


def _post_b_sample_kernel(o_ref, h1_ref, wo_ref, g2_ref, b2_ref, wrhl_ref, br_ref, tri_ref,
                          h2t_ref, rt_ref, cnt_ref):
    h2 = _layer_norm(ALPHA * h1_ref[...] + _dot(_bf16(o_ref[...]), wo_ref[...]), g2_ref[...], b2_ref[...])
    _store_token_tiles(h2t_ref, h2)
    route, carry = _route(h2, wrhl_ref, br_ref, tri_ref, jnp.zeros((1, LANES), jnp.float32))
    rt_ref[...] = route
    cnt_ref[...] = jnp.broadcast_to(carry, cnt_ref.shape)


def _post_b_sample(o, h1, wo_b, g2, b2, wrhl, br, tri):
    n = h1.shape[0]
    args = (o, h1, wo_b, g2, b2, wrhl, br, tri)
    full = lambda a: pl.BlockSpec(a.shape, lambda i: (0,) * a.ndim)
    return pl.pallas_call(
        _post_b_sample_kernel,
        grid=(1,),
        in_specs=[full(a) for a in args],
        out_specs=[pl.BlockSpec((n * ROW_CHUNKS, LANES), lambda i: (0, 0)),
                   pl.BlockSpec((n, LANES), lambda i: (0, 0)),
                   pl.BlockSpec((SUBLANES, LANES), lambda i: (0, 0))],
        out_shape=[jax.ShapeDtypeStruct((n * ROW_CHUNKS, LANES), jnp.float32),
                   jax.ShapeDtypeStruct((n, LANES), jnp.float32),
                   jax.ShapeDtypeStruct((SUBLANES, LANES), jnp.float32)],
        compiler_params=_params(("arbitrary",)),
        name="post_b_sample",
    )(*args)


def _row_gather_copy(src_hbm, idx, dst, dst_row, sem):
    s0 = pl.multiple_of(idx * ROW_CHUNKS, ROW_CHUNKS)
    d0 = pl.multiple_of(dst_row * ROW_CHUNKS, ROW_CHUNKS)
    return pltpu.make_async_copy(src_hbm.at[pl.ds(s0, ROW_CHUNKS), :], dst.at[pl.ds(d0, ROW_CHUNKS), :], sem)


def _dispatch_kernel(pos_ref, h2t_ref, xs_hbm, sem):
    def body(r, c):
        src = h2t_ref.at[pl.ds(pl.multiple_of(r * ROW_CHUNKS, ROW_CHUNKS), ROW_CHUNKS), :]
        for k in range(2):
            d0 = pl.multiple_of(pos_ref[0, 0, k * TM_COMB + r] * ROW_CHUNKS, ROW_CHUNKS)
            pltpu.make_async_copy(src, xs_hbm.at[pl.ds(d0, ROW_CHUNKS), :], sem.at[0]).start(priority=k)
        return c
    lax.fori_loop(0, TM_COMB, body, 0, unroll=8)
    for _ in range(2):
        pltpu.make_async_copy(h2t_ref, xs_hbm.at[pl.ds(0, TM_COMB * ROW_CHUNKS), :], sem.at[0]).wait()


def _dispatch(pos3, h2t):
    nt = N_ALL // TM_COMB
    return pl.pallas_call(
        _dispatch_kernel,
        grid=(nt,),
        in_specs=[pl.BlockSpec((1, 1, 2 * TM_COMB), lambda i: (i, 0, 0), memory_space=pltpu.SMEM),
                  pl.BlockSpec((TM_COMB * ROW_CHUNKS, LANES), lambda i: (i, 0))],
        out_specs=pl.BlockSpec(memory_space=pl.ANY),
        out_shape=jax.ShapeDtypeStruct((N_ASSIGN * ROW_CHUNKS, LANES), jnp.float32),
        scratch_shapes=[pltpu.SemaphoreType.DMA((1,))],
        compiler_params=_params(("arbitrary",)),
        name="moe_dispatch",
    )(pos3, h2t)


def _moe_ffn_kernel(it_ref, ie_ref, lo_ref, hi_ref, x_ref, wg_ref, wu_ref, wd_ref, y_ref, wgb, wub, wdb, cur_e):
    i = pl.program_id(0)
    lo = lo_ref[i]
    hi = hi_ref[i]
    e = ie_ref[i]

    @pl.when(i == 0)
    def _():
        cur_e[0] = -1

    @pl.when((hi > lo) & (cur_e[0] != e))
    def _():
        wgb[...] = _bf16(wg_ref[0])
        wub[...] = _bf16(wu_ref[0])
        wdb[...] = _bf16(wd_ref[0])
        cur_e[0] = e

    def ffn():
        x = _bf16(_load_token_tiles(x_ref, 0, TM_MOE))
        hg = _dot(x, wgb[...])
        hu = _dot(x, wub[...])
        h = hg / (1.0 + jnp.exp(-hg)) * hu
        return _dot(_bf16(h), wdb[...])

    def rows_mask():
        row = lax.broadcasted_iota(jnp.int32, (TM_MOE, LANES), 0)
        return (row >= lo) & (row < hi)

    @pl.when((hi > lo) & (lo == 0))
    def _():
        y = ffn()
        mask = rows_mask()
        for c in range(ROW_CHUNKS):
            y_ref[pl.ds(c, TM_MOE, stride=ROW_CHUNKS), :] = jnp.where(mask, y[:, c * LANES:(c + 1) * LANES], 0.0)

    @pl.when((hi > lo) & (lo > 0))
    def _():
        y = ffn()
        mask = rows_mask()
        for c in range(ROW_CHUNKS):
            sl = pl.ds(c, TM_MOE, stride=ROW_CHUNKS)
            y_ref[sl, :] = jnp.where(mask, y[:, c * LANES:(c + 1) * LANES], y_ref[sl, :])


def _moe_ffn(item_tile, item_expert, item_lo, item_hi, x_sorted, w_gate, w_up, w_down):
    wspec = lambda shp: pl.BlockSpec((1,) + shp, lambda i, it, ie, lo, hi: (ie[i], 0, 0))
    tile = pl.BlockSpec((TM_MOE * ROW_CHUNKS, LANES), lambda i, it, ie, lo, hi: (it[i], 0))
    grid_spec = pltpu.PrefetchScalarGridSpec(
        num_scalar_prefetch=4,
        grid=(MOE_ITEMS,),
        in_specs=[tile, wspec((D_MODEL, EXPERT_FF)), wspec((D_MODEL, EXPERT_FF)), wspec((EXPERT_FF, D_MODEL))],
        out_specs=tile,
        scratch_shapes=[pltpu.VMEM((D_MODEL, EXPERT_FF), jnp.bfloat16),
                        pltpu.VMEM((D_MODEL, EXPERT_FF), jnp.bfloat16),
                        pltpu.VMEM((EXPERT_FF, D_MODEL), jnp.bfloat16),
                        pltpu.SMEM((1,), jnp.int32)],
    )
    return pl.pallas_call(
        _moe_ffn_kernel,
        grid_spec=grid_spec,
        out_shape=jax.ShapeDtypeStruct((N_ASSIGN * ROW_CHUNKS, LANES), jnp.float32),
        compiler_params=_params(("arbitrary",)),
        name="moe_ffn",
    )(item_tile, item_expert, item_lo, item_hi, x_sorted, w_gate, w_up, w_down)


def _combine_kernel(nt, pos_cur_ref, pos_nxt_ref, yt_hbm, h2t_ref, rt_ref, g3_ref, b3_ref, o_ref, abuf, sem):
    t = pl.program_id(0)
    slot = t % 2
    rows = 2 * TM_COMB

    def issue(pos_ref, s):
        def body(j, c):
            for k in range(2):
                r = 2 * j + k
                _row_gather_copy(yt_hbm, pos_ref[0, 0, r], abuf, s * rows + r, sem.at[s]).start(priority=k)
            return c
        lax.fori_loop(0, rows // 2, body, 0, unroll=4)

    @pl.when(t == 0)
    def _():
        issue(pos_cur_ref, 0)

    @pl.when(t + 1 < nt)
    def _():
        issue(pos_nxt_ref, 1 - slot)

    base = pl.multiple_of(slot * (rows * ROW_CHUNKS), rows * ROW_CHUNKS)
    pltpu.make_async_copy(yt_hbm.at[pl.ds(0, rows * ROW_CHUNKS), :],
                          abuf.at[pl.ds(base, rows * ROW_CHUNKS), :], sem.at[slot]).wait()
    ya = _load_token_tiles(abuf, base, TM_COMB)
    yb = _load_token_tiles(abuf, base + TM_COMB * ROW_CHUNKS, TM_COMB)
    rt = rt_ref[...]
    ff = rt[:, 2:3] * ya + rt[:, 3:4] * yb
    h2 = _load_token_tiles(h2t_ref, 0, TM_COMB)
    o_ref[...] = _layer_norm(ALPHA * h2 + ff, g3_ref[...], b3_ref[...])


def _combine(pos3, yt, h2t, rt, g3, b3, tile0, n_tiles):
    last = tile0 + n_tiles - 1
    smem_pos = lambda f: pl.BlockSpec((1, 1, 2 * TM_COMB), f, memory_space=pltpu.SMEM)
    full = lambda a: pl.BlockSpec(a.shape, lambda i: (0,) * a.ndim)
    return pl.pallas_call(
        functools.partial(_combine_kernel, n_tiles),
        grid=(n_tiles,),
        in_specs=[smem_pos(lambda i: (tile0 + i, 0, 0)),
                  smem_pos(lambda i: (jnp.minimum(tile0 + i + 1, last), 0, 0)),
                  pl.BlockSpec(memory_space=pl.ANY),
                  pl.BlockSpec((TM_COMB * ROW_CHUNKS, LANES), lambda i: (tile0 + i, 0)),
                  pl.BlockSpec((TM_COMB, LANES), lambda i: (tile0 + i, 0)),
                  full(g3), full(b3)],
        out_specs=pl.BlockSpec((TM_COMB, D_MODEL), lambda i: (i, 0)),
        out_shape=jax.ShapeDtypeStruct((n_tiles * TM_COMB, D_MODEL), jnp.float32),
        scratch_shapes=[pltpu.VMEM((2 * 2 * TM_COMB * ROW_CHUNKS, LANES), jnp.float32),
                        pltpu.SemaphoreType.DMA((2,))],
        compiler_params=_params(("arbitrary",)),
        name="moe_combine",
    )(pos3, pos3, yt, h2t, rt, g3, b3)


POS_TILES = 3


def _positions_kernel(rt_ref, starts_ref, pos_ref):
    lane = lax.broadcasted_iota(jnp.int32, (TM_COMB, LANES), 1)
    lane_f = lane.astype(jnp.float32)
    starts = starts_ref[0:1, :]
    for j in range(POS_TILES):
        rt = rt_ref[j * TM_COMB:(j + 1) * TM_COMB, :]
        cols = []
        for k in range(2):
            seg = jnp.sum(jnp.where(lane_f == rt[:, k:k + 1] + N_GROUPS, starts, 0.0), axis=1, keepdims=True)
            cols.append(seg + rt[:, 4 + k:5 + k])
        packed = jnp.where(lane == 0, cols[0], jnp.where(lane == 1, cols[1], 0.0))
        rows = packed.T
        pos_ref[j] = jnp.concatenate([rows[0:1, :], rows[1:2, :]], axis=1).astype(jnp.int32)


def _positions(rt, starts_row):
    nt = N_ALL // TM_COMB
    assert nt % POS_TILES == 0
    return pl.pallas_call(
        _positions_kernel,
        grid=(nt // POS_TILES,),
        in_specs=[pl.BlockSpec((POS_TILES * TM_COMB, LANES), lambda i: (i, 0)),
                  pl.BlockSpec((SUBLANES, LANES), lambda i: (0, 0))],
        out_specs=pl.BlockSpec((POS_TILES, 1, 2 * TM_COMB), lambda i: (i, 0, 0)),
        out_shape=jax.ShapeDtypeStruct((nt, 1, 2 * TM_COMB), jnp.int32),
        compiler_params=_params(("arbitrary",)),
        name="moe_positions",
    )(rt, starts_row)


def _routing_plan(rt, cnt):
    i32 = jnp.int32
    counts_f = cnt[0, N_GROUPS:N_GROUPS + N_EXPERTS]
    starts_f = jnp.cumsum(counts_f) - counts_f
    starts_row = jnp.broadcast_to(
        jnp.pad(starts_f, (N_GROUPS, LANES - N_GROUPS - N_EXPERTS))[None, :], (SUBLANES, LANES))
    pos3 = _positions(rt, starts_row)
    starts = starts_f.astype(i32)
    tiles = jnp.arange(MOE_TILES, dtype=i32) * TM_MOE
    rank_t = jnp.arange(MOE_TILES, dtype=i32) + jnp.sum((starts[None, :] < tiles[:, None]).astype(i32), axis=1)
    rank_s = jnp.arange(N_EXPERTS, dtype=i32) + jnp.sum((tiles[None, :] <= starts[:, None]).astype(i32), axis=1)
    vals = jnp.concatenate([tiles, starts])
    ranks = jnp.concatenate([rank_t, rank_s])
    slot = jnp.arange(MOE_ITEMS, dtype=i32)
    lo = jnp.sum(jnp.where(ranks[None, :] == slot[:, None], vals[None, :], 0), axis=1)
    hi = jnp.concatenate([lo[1:], jnp.full((1,), N_ASSIGN, i32)])
    item_tile = jnp.minimum(lo // TM_MOE, MOE_TILES - 1)
    item_expert = jnp.clip(jnp.sum((starts[None, :] <= lo[:, None]).astype(i32), axis=1) - 1, 0, N_EXPERTS - 1)
    base = item_tile * TM_MOE
    return item_tile, item_expert, lo - base, hi - base, pos3


def kernel(x_prompt, x_sample, mem_prompt, cache_swa_k, cache_swa_v, cache_conv, cache_mem_k, cache_mem_v,
           w_in, sinks, conv_w, w_mix_out, ln1_g, ln1_b, w_q_mem, w_k_mem, w_v_mem, w_o_mem, ln2_g, ln2_b,
           w_router_group, b_router_group, w_router_expert, b_router_expert, w_gate, w_up, w_down,
           ln3_g, ln3_b):
    f32 = jnp.float32
    row = lambda a: a.reshape(1, -1).astype(f32)
    w_in_b, wmix_b, wq_b, wk_b, wv_b, wo_b = (_bf16(w) for w in (w_in, w_mix_out, w_q_mem, w_k_mem, w_v_mem, w_o_mem))
    g1, b1, g2, b2, g3, b3 = (row(a) for a in (ln1_g, ln1_b, ln2_g, ln2_b, ln3_g, ln3_b))
    pad = LANES - N_GROUPS - N_EXPERTS
    wr = jnp.concatenate([w_router_group, w_router_expert, jnp.zeros((D_MODEL, pad), f32)], axis=1)
    wrh = _bf16(wr)
    wrhl = jnp.concatenate([wrh, _bf16(wr - wrh.astype(f32))], axis=1)
    br = jnp.concatenate([b_router_group, b_router_expert, jnp.zeros((pad,), f32)]).reshape(1, LANES)

    xs = x_sample.reshape(N_SAMPLE, D_MODEL)
    tab_s = jnp.tile(_rope_table(PAST_LEN + jnp.arange(DEC_SEQ)), (1, DEC_BATCH))
    c0 = jnp.repeat(cache_conv[:, 0], DEC_SEQ, axis=0)
    c1 = jnp.repeat(cache_conv[:, 1], DEC_SEQ, axis=0)
    q_s, k_s, v_s, conv_s, u_s = _proj_sample(xs, w_in_b, tab_s, conv_w, c0, c1)
    attn_s, swa_k_s, swa_v_s = _swa_sample(sinks, q_s, k_s, v_s, cache_swa_k, cache_swa_v)
    h1_s, qm_s = _post_a_sample(attn_s, conv_s, xs, wmix_b, g1, b1, wq_b)
    xp = x_prompt.reshape(N_PROMPT, D_MODEL)
    tab_p = _rope_table(jnp.arange(N_PROMPT))
    (q_p, kx_p, vx_p, conv_p, k_tail, v_tail, u_tail), o_s = _proj_prompt_mem_sample(
        xp, w_in_b, tab_p, conv_w, qm_s, cache_mem_k, cache_mem_v)
    tri = _bf16(jnp.tril(jnp.ones((TM_POST, TM_POST), f32), -1))
    h2t_s, rt_s, cnt_s = _post_b_sample(o_s, h1_s, wo_b, g2, b2, wrhl, br, tri)

    mk, mv, wqk, wvo = _mem_kv(mem_prompt.reshape(N_MEM, D_MODEL), wk_b, wv_b, wq_b, wo_b)
    h2t, rt, cnt = _swa_post(sinks, q_p, kx_p, vx_p, conv_p, xp, wmix_b, g1, b1, wqk, wvo, g2, b2,
                             wrhl, br, tri, h2t_s, rt_s, cnt_s)

    item_tile, item_expert, item_lo, item_hi, pos3 = _routing_plan(rt, cnt)
    x_sorted = _dispatch(pos3, h2t)
    yt = _moe_ffn(item_tile, item_expert, item_lo, item_hi, x_sorted, w_gate, w_up, w_down)
    y_p = _combine(pos3, yt, h2t, rt, g3, b3, 0, N_PROMPT // TM_COMB)
    y_s = _combine(pos3, yt, h2t, rt, g3, b3, N_PROMPT // TM_COMB, N_SAMPLE // TM_COMB)

    return (y_p.reshape(1, SEQ, D_MODEL),
            y_s.reshape(DEC_BATCH, DEC_SEQ, D_MODEL),
            k_tail.reshape(1, WINDOW, N_KV_HEADS, HEAD_DIM),
            v_tail.reshape(1, WINDOW, N_KV_HEADS, HEAD_DIM),
            u_tail[SUBLANES - (CONV_K - 1):].reshape(1, CONV_K - 1, CONV_CH),
            mk.reshape(1, N_MEM, MEM_HEADS, MEM_HEAD_DIM),
            mv.reshape(1, N_MEM, MEM_HEADS, MEM_HEAD_DIM),
            swa_k_s.reshape(DEC_BATCH, WINDOW, N_KV_HEADS, HEAD_DIM),
            swa_v_s.reshape(DEC_BATCH, WINDOW, N_KV_HEADS, HEAD_DIM),
            u_s.reshape(DEC_BATCH, DEC_SEQ, CONV_CH)[:, DEC_SEQ - (CONV_K - 1):])
```

```python
import functools

import jax
import jax.numpy as jnp
from jax import lax
from jax.experimental import pallas as pl
from jax.experimental.pallas import tpu as pltpu

D_MODEL = 1024
SEQ = 16384
DEC_BATCH = 128
DEC_SEQ = 4
PAST_LEN = 16384
ATTN_WIDTH = 512
CONV_CH = 512
HEAD_DIM = 64
N_HEADS = 8
N_KV_HEADS = 2
KV_WIDTH = 128
WINDOW = 128
ROPE_THETA = 500000.0
ROPE_DIM = 16
CONV_K = 3
Q_END = ATTN_WIDTH
K_END = Q_END + KV_WIDTH
V_END = K_END + KV_WIDTH
B_END = V_END + CONV_CH
C_END = B_END + CONV_CH
IN_WIDTH = C_END + CONV_CH
N_MEM = 256
MEM_HEADS = 4
MEM_HEAD_DIM = 256
N_GROUPS = 4
EXPERTS_PER_GROUP = 8
N_EXPERTS = 32
EXPERT_FF = 256
ALPHA = 2.0 ** 0.25
LN_EPS = 1e-5

LANES = 128
SUBLANES = 8
ROW_CHUNKS = D_MODEL // LANES
VMEM_LIMIT = 56 * 1024 * 1024

N_PROMPT = SEQ
N_SAMPLE = DEC_BATCH * DEC_SEQ
N_ALL = N_PROMPT + N_SAMPLE
TM_POST = 512
TM_MOE = 512
TM_COMB = 512
N_ASSIGN = 2 * N_ALL
MOE_TILES = N_ASSIGN // TM_MOE
MOE_ITEMS = MOE_TILES + N_EXPERTS
SAMPLE_BB = 4
SWA_BB = 16

assert ROW_CHUNKS == SUBLANES
assert N_SAMPLE == TM_POST
assert N_ASSIGN % TM_MOE == 0 and N_ALL % TM_COMB == 0


def _params(sem, vmem=VMEM_LIMIT):
    return pltpu.CompilerParams(dimension_semantics=sem, vmem_limit_bytes=vmem)


def _bf16(x):
    return x.astype(jnp.bfloat16)


def _dot(a, b):
    return jnp.dot(a, b, preferred_element_type=jnp.float32)


def _dot_nt(a, b):
    return lax.dot_general(a, b, (((1,), (1,)), ((), ())), preferred_element_type=jnp.float32)


def _layer_norm(x, g, b):
    mu = jnp.mean(x, axis=-1, keepdims=True)
    xc = x - mu
    var = jnp.mean(xc * xc, axis=-1, keepdims=True)
    return xc * lax.rsqrt(var + LN_EPS) * g + b


def _rope(x, cos_t, sin_t):
    lane = lax.broadcasted_iota(jnp.int32, x.shape, 1) % HEAD_DIM
    half = ROPE_DIM // 2
    partner = jnp.where(lane < half, pltpu.roll(x, LANES - half, axis=1), pltpu.roll(x, half, axis=1))
    return x * cos_t + partner * sin_t


def _head_slabs(x):
    lane = lax.broadcasted_iota(jnp.int32, x.shape, 1)
    lo = lane < HEAD_DIM
    sw = pltpu.roll(x, HEAD_DIM, axis=1)
    zero = jnp.zeros_like(x)
    slabs = [jnp.where(lo, x, zero), jnp.where(lo, zero, sw), jnp.where(lo, sw, zero), jnp.where(lo, zero, x)]
    return _bf16(jnp.concatenate(slabs, axis=1))


def _store_token_tiles(ref, val):
    rows = val.shape[0]
    for c in range(ROW_CHUNKS):
        ref[pl.ds(c, rows, stride=ROW_CHUNKS), :] = val[:, c * LANES:(c + 1) * LANES]


def _load_token_tiles(ref, base, rows):
    return jnp.concatenate(
        [ref[pl.ds(base + c, rows, stride=ROW_CHUNKS), :] for c in range(ROW_CHUNKS)], axis=1)


ROPE_ONE = 3 * (ROPE_DIM // 2)
ROPE_ROWS = 32


def _rope_patterns(tab):
    half = ROPE_DIM // 2
    m = lax.broadcasted_iota(jnp.int32, tab.shape, 1) % HEAD_DIM
    idx_c = jnp.where(m < ROPE_DIM, m % half, ROPE_ONE)
    idx_s = jnp.where(m < half, 2 * half + m, jnp.where(m < ROPE_DIM, m, ROPE_ONE + 1))
    return jnp.take_along_axis(tab, idx_c, axis=1), jnp.take_along_axis(tab, idx_s, axis=1)


def _proj_common(x_ref, w_ref, tab_ref):
    xb = _bf16(x_ref[...])
    tab = tab_ref[...]
    pad = jnp.zeros((LANES - tab.shape[0], tab.shape[1]), jnp.float32)
    cos_t, sin_t = _rope_patterns(jnp.concatenate([tab, pad], axis=0).T)
    q = _dot(xb, w_ref[:, 0:Q_END])
    q_rot = jnp.concatenate(
        [_rope(q[:, p * LANES:(p + 1) * LANES], cos_t, sin_t) for p in range(ATTN_WIDTH // LANES)], axis=1)
    q_out = _bf16(q_rot * (HEAD_DIM ** -0.5))
    kv = _dot(xb, w_ref[:, Q_END:V_END])
    k = _rope(kv[:, 0:KV_WIDTH], cos_t, sin_t)
    v = kv[:, KV_WIDTH:]
    bg = _dot(xb, w_ref[:, V_END:B_END])
    u = _dot(xb, w_ref[:, B_END:C_END]) * _dot(xb, w_ref[:, C_END:IN_WIDTH])
    return q_out, k, v, bg, u


def _conv3(bg, u, u1, u2, cw_ref):
    cw = cw_ref[...]
    return bg * (cw[0:1, :] * u2 + cw[1:2, :] * u1 + cw[2:3, :] * u)


def _proj_prompt_body(x_ref, w_ref, tab_ref, cw_ref,
                      q_ref, kx_ref, vx_ref, conv_ref, ktail_ref, vtail_ref, utail_ref, carry_ref):
    @pl.when(pl.program_id(0) == 0)
    def _():
        carry_ref[...] = jnp.zeros_like(carry_ref)

    q_out, k, v, bg, u = _proj_common(x_ref, w_ref, tab_ref)
    tm = u.shape[0]
    ext = jnp.concatenate([carry_ref[...], u], axis=0)
    u1 = pltpu.roll(ext, 1, axis=0)[SUBLANES:SUBLANES + tm]
    u2 = pltpu.roll(ext, 2, axis=0)[SUBLANES:SUBLANES + tm]
    q_ref[...] = q_out
    kx_ref[...] = _head_slabs(k)
    vx_ref[...] = _head_slabs(v)
    conv_ref[...] = _bf16(_conv3(bg, u, u1, u2, cw_ref))
    ktail_ref[...] = k[tm - WINDOW:tm]
    vtail_ref[...] = v[tm - WINDOW:tm]
    utail_ref[...] = u[tm - SUBLANES:tm]
    carry_ref[...] = u[tm - SUBLANES:tm]


def _proj_sample_kernel(x_ref, w_ref, tab_ref, cw_ref, c0_ref, c1_ref,
                        q_ref, k_ref, v_ref, conv_ref, u_ref):
    q_out, k, v, bg, u = _proj_common(x_ref, w_ref, tab_ref)
    t = lax.broadcasted_iota(jnp.int32, u.shape, 0) % DEC_SEQ
    c0 = c0_ref[...]
    c1 = c1_ref[...]
    u1 = jnp.where(t >= 1, pltpu.roll(u, 1, axis=0), c1)
    u2 = jnp.where(t >= 2, pltpu.roll(u, 2, axis=0), jnp.where(t == 1, c1, c0))
    q_ref[...] = q_out.astype(jnp.float32)
    k_ref[...] = k
    v_ref[...] = v
    conv_ref[...] = _bf16(_conv3(bg, u, u1, u2, cw_ref))
    u_ref[...] = u


def _rope_table(pos):
    half = ROPE_DIM // 2
    inv = ROPE_THETA ** (-jnp.arange(0, ROPE_DIM, 2, dtype=jnp.float32) / ROPE_DIM)
    ang = pos.astype(jnp.float32)[None, :] * inv[:, None]
    cos, sin = jnp.cos(ang), jnp.sin(ang)
    n = pos.shape[0]
    assert ROPE_ONE == 3 * half
    return jnp.concatenate([cos, sin, -sin, jnp.ones((1, n), jnp.float32),
                            jnp.zeros((ROPE_ROWS - ROPE_ONE - 1, n), jnp.float32)], axis=0)


def _proj_sample(x, w_in_b, tab, conv_w, c0, c1):
    n = x.shape[0]
    full = lambda a: pl.BlockSpec(a.shape, lambda i: (0,) * a.ndim)
    out = lambda w, dt: jax.ShapeDtypeStruct((n, w), dt)
    blk = lambda w: pl.BlockSpec((n, w), lambda i: (0, 0))
    return pl.pallas_call(
        _proj_sample_kernel,
        grid=(1,),
        in_specs=[full(x), full(w_in_b), full(tab), full(conv_w), full(c0), full(c1)],
        out_specs=[blk(ATTN_WIDTH), blk(KV_WIDTH), blk(KV_WIDTH), blk(CONV_CH), blk(CONV_CH)],
        out_shape=[out(ATTN_WIDTH, jnp.float32), out(KV_WIDTH, jnp.float32), out(KV_WIDTH, jnp.float32),
                   out(CONV_CH, jnp.bfloat16), out(CONV_CH, jnp.float32)],
        compiler_params=_params(("arbitrary",)),
        name="proj_sample",
    )(x, w_in_b, tab, conv_w, c0, c1)


def _sink_softmax_pv(s, valid, sink, vx):
    s = jnp.where(valid, s, -jnp.inf)
    m = jnp.maximum(jnp.max(s, axis=1, keepdims=True), sink)
    p = jnp.exp(s - m)
    den = jnp.sum(p, axis=1, keepdims=True) + jnp.exp(sink - m)
    return _dot(_bf16(p), vx) / den


SWA_QB = 4


def _swa_tile(step, sinks_ref, q_ref, kc_ref, kp_ref, vc_ref, vp_ref, store):
    kall = jnp.concatenate([kp_ref[...], kc_ref[...]], axis=0)
    vall = jnp.concatenate([vp_ref[...], vc_ref[...]], axis=0)
    i = lax.broadcasted_iota(jnp.int32, (WINDOW, 2 * WINDOW), 0)
    j = lax.broadcasted_iota(jnp.int32, (WINDOW, 2 * WINDOW), 1)
    band = (j > i) & (j <= i + WINDOW)
    for sb in range(SWA_QB):
        rows = slice(sb * WINDOW, (sb + 1) * WINDOW)
        kcat = kall[sb * WINDOW:(sb + 2) * WINDOW]
        vcat = vall[sb * WINDOW:(sb + 2) * WINDOW]
        valid = band & ((step > 0) | (j >= WINDOW)) if sb == 0 else band
        for p in range(N_HEADS // 2):
            qs = q_ref[rows, p * LANES:(p + 1) * LANES]
            acc = None
            for e in range(2):
                hd = 2 * p + e
                slab = 2 * (hd // (N_HEADS // N_KV_HEADS)) + e
                kx = kcat[:, slab * LANES:(slab + 1) * LANES]
                vx = vcat[:, slab * LANES:(slab + 1) * LANES]
                o = _sink_softmax_pv(_dot_nt(qs, kx), valid, sinks_ref[hd], vx)
                acc = o if acc is None else acc + o
            store(rows, slice(p * LANES, (p + 1) * LANES), _bf16(acc))


SWA_ROWS = N_HEADS * DEC_SEQ
NEW_ROWS = 2 * SUBLANES


def _swa_sample_kernel(q_ref, sink_ref, kn_ref, vn_ref, kt_ref, vt_ref, o_ref, okt_ref, ovt_ref):
    nb = SWA_BB
    rows = nb * SWA_ROWS
    t = lax.broadcasted_iota(jnp.int32, (rows, WINDOW), 0) % DEC_SEQ
    valid_c = lax.broadcasted_iota(jnp.int32, (rows, WINDOW), 1) > t
    valid_n = (lax.broadcasted_iota(jnp.int32, (rows, NEW_ROWS), 1)
               <= lax.broadcasted_iota(jnp.int32, (rows, NEW_ROWS), 0) % DEC_SEQ)
    sink = jnp.concatenate([sink_ref[:, 0:1]] * nb, axis=0)
    qs = [_bf16(q_ref[b]) for b in range(nb)]
    s_c = jnp.concatenate([_dot(qs[b], _bf16(kt_ref[b])) for b in range(nb)], axis=0)
    s_n = jnp.concatenate([_dot_nt(qs[b], _bf16(kn_ref[b])) for b in range(nb)], axis=0)
    s_c = jnp.where(valid_c, s_c, -jnp.inf)
    s_n = jnp.where(valid_n, s_n, -jnp.inf)
    m = jnp.maximum(jnp.maximum(jnp.max(s_c, axis=1, keepdims=True), jnp.max(s_n, axis=1, keepdims=True)), sink)
    p_c = jnp.exp(s_c - m)
    p_n = jnp.exp(s_n - m)
    rden = 1.0 / (jnp.sum(p_c, axis=1, keepdims=True) + jnp.sum(p_n, axis=1, keepdims=True) + jnp.exp(sink - m))
    p_c, p_n = _bf16(p_c), _bf16(p_n)
    lane = lax.broadcasted_iota(jnp.int32, (KV_WIDTH, WINDOW), 1)
    shift = WINDOW - DEC_SEQ
    zrows = jnp.zeros((KV_WIDTH - NEW_ROWS, KV_WIDTH), jnp.float32)
    for b in range(nb):
        r = slice(b * SWA_ROWS, (b + 1) * SWA_ROWS)
        kt, vt = kt_ref[b], vt_ref[b]
        kn, vn = kn_ref[b], vn_ref[b]
        o_ref[b] = (_dot_nt(p_c[r], _bf16(vt)) + _dot(p_n[r], _bf16(vn))) * rden[r]
        for old, new, dst in ((kt, kn, okt_ref), (vt, vn, ovt_ref)):
            new_cols = pltpu.roll(jnp.concatenate([new, zrows], axis=0).T, shift, axis=1)
            dst[b] = jnp.where(lane >= shift, new_cols, pltpu.roll(old, shift, axis=1))


def _swa_sample(sinks, q, kn, vn, cache_k, cache_v):
    nb = cache_k.shape[0]
    bb = SWA_BB
    groups = N_HEADS // N_KV_HEADS
    qh = q.reshape(nb, DEC_SEQ, N_KV_HEADS, groups, HEAD_DIM).transpose(0, 2, 3, 1, 4)
    qh = qh.reshape(nb, N_KV_HEADS, groups * DEC_SEQ, HEAD_DIM)
    zeros = jnp.zeros_like(qh[:, 0])
    qbd = jnp.concatenate([jnp.concatenate([qh[:, 0], zeros], axis=-1),
                           jnp.concatenate([zeros, qh[:, 1]], axis=-1)], axis=1)
    sink_col = jnp.broadcast_to(jnp.repeat(sinks, DEC_SEQ).reshape(SWA_ROWS, 1), (SWA_ROWS, LANES))
    pad8 = lambda a: jnp.pad(a.reshape(nb, DEC_SEQ, KV_WIDTH), ((0, 0), (0, NEW_ROWS - DEC_SEQ), (0, 0)))
    to_t = lambda c: c.transpose(0, 2, 3, 1).reshape(nb, KV_WIDTH, WINDOW)
    blk = lambda r, w: pl.BlockSpec((bb, r, w), lambda i: (i, 0, 0))
    o, okt, ovt = pl.pallas_call(
        _swa_sample_kernel,
        grid=(nb // bb,),
        in_specs=[blk(SWA_ROWS, KV_WIDTH), pl.BlockSpec((SWA_ROWS, LANES), lambda i: (0, 0)),
                  blk(NEW_ROWS, KV_WIDTH), blk(NEW_ROWS, KV_WIDTH), blk(KV_WIDTH, WINDOW), blk(KV_WIDTH, WINDOW)],
        out_specs=[blk(SWA_ROWS, KV_WIDTH), blk(KV_WIDTH, WINDOW), blk(KV_WIDTH, WINDOW)],
        out_shape=[jax.ShapeDtypeStruct((nb, SWA_ROWS, KV_WIDTH), jnp.float32),
                   jax.ShapeDtypeStruct((nb, KV_WIDTH, WINDOW), jnp.float32),
                   jax.ShapeDtypeStruct((nb, KV_WIDTH, WINDOW), jnp.float32)],
        compiler_params=_params(("arbitrary",)),
        name="swa_sample",
    )(qbd, sink_col, pad8(kn), pad8(vn), to_t(cache_k), to_t(cache_v))
    o = o.reshape(nb, N_KV_HEADS, groups, DEC_SEQ, N_KV_HEADS, HEAD_DIM)
    attn = jnp.stack([o[:, h, :, :, h, :] for h in range(N_KV_HEADS)], axis=1)
    attn = attn.transpose(0, 3, 1, 2, 4).reshape(nb * DEC_SEQ, ATTN_WIDTH)
    from_t = lambda c: c.reshape(nb, N_KV_HEADS, HEAD_DIM, WINDOW).transpose(0, 3, 1, 2)
    return attn, from_t(okt), from_t(ovt)


def _mem_kv_kernel(mem_ref, wk_ref, wv_ref, wq_ref, wo_ref, mk_ref, mv_ref, wqk_ref, wvo_ref):
    mb = _bf16(mem_ref[...])
    mk = _dot(mb, wk_ref[...])
    mv = _dot(mb, wv_ref[...])
    mk_ref[...] = mk
    mv_ref[...] = mv
    mkb, mvb = _bf16(mk), _bf16(mv)
    for h in range(MEM_HEADS):
        sl = slice(h * MEM_HEAD_DIM, (h + 1) * MEM_HEAD_DIM)
        keys = slice(h * N_MEM, (h + 1) * N_MEM)
        wqk_ref[:, keys] = _bf16(_dot_nt(wq_ref[:, sl], mkb[:, sl]) * (MEM_HEAD_DIM ** -0.5))
        wvo_ref[keys, :] = _bf16(_dot(mvb[:, sl], wo_ref[sl, :]))


def _mem_kv(mem, wk_b, wv_b, wq_b, wo_b):
    full = lambda a: pl.BlockSpec(a.shape, lambda i: (0,) * a.ndim)
    blk = pl.BlockSpec((N_MEM, D_MODEL), lambda i: (0, 0))
    f32 = jax.ShapeDtypeStruct((N_MEM, D_MODEL), jnp.float32)
    fused = (D_MODEL, MEM_HEADS * N_MEM), (MEM_HEADS * N_MEM, D_MODEL)
    return pl.pallas_call(
        _mem_kv_kernel,
        grid=(1,),
        in_specs=[full(mem), full(wk_b), full(wv_b), full(wq_b), full(wo_b)],
        out_specs=[blk, blk] + [pl.BlockSpec(shp, lambda i: (0, 0)) for shp in fused],
        out_shape=[f32, f32] + [jax.ShapeDtypeStruct(shp, jnp.bfloat16) for shp in fused],
        compiler_params=_params(("arbitrary",)),
        name="mem_kv",
    )(mem, wk_b, wv_b, wq_b, wo_b)


def _mix_ln1(attn_ref, conv_ref, x_ref, wmix_ref, g1_ref, b1_ref):
    mix = _dot(_bf16(attn_ref[...]), wmix_ref[0:ATTN_WIDTH, :]) + _dot(conv_ref[...], wmix_ref[ATTN_WIDTH:, :])
    return _layer_norm(ALPHA * x_ref[...] + mix, g1_ref[...], b1_ref[...])


def _mem_q(h1, wq_ref):
    return _bf16(_dot(_bf16(h1), wq_ref[...]) * (MEM_HEAD_DIM ** -0.5))


def _route(h2, wrhl_ref, br_ref, tri_ref, carry):
    hi = _bf16(h2)
    lo = _bf16(h2 - hi.astype(jnp.float32))
    hh = _dot(hi, wrhl_ref[...])
    logits = hh[:, 0:LANES] + hh[:, LANES:] + _dot(lo, wrhl_ref[:, 0:LANES]) + br_ref[...]
    lane_i = lax.broadcasted_iota(jnp.int32, logits.shape, 1)
    lane = lane_i.astype(jnp.float32)
    big = jnp.float32(LANES)
    is_g = lane_i < N_GROUPS
    gl = jnp.where(is_g, logits, -jnp.inf)
    gmax = jnp.max(gl, axis=1, keepdims=True)
    gidx = jnp.min(jnp.where(is_g & (logits == gmax), lane, big), axis=1, keepdims=True)
    gsum = jnp.sum(jnp.exp(gl - gmax), axis=1, keepdims=True)
    gw = 1.0 / gsum
    eid = lane_i - N_GROUPS
    assert EXPERTS_PER_GROUP == 8
    grp = lax.shift_right_arithmetic(eid, jnp.full_like(eid, 3)).astype(jnp.float32)
    in_e = (lane_i >= N_GROUPS) & (lane_i < N_GROUPS + N_EXPERTS) & (grp == gidx)
    v1 = jnp.max(jnp.where(in_e, logits, -jnp.inf), axis=1, keepdims=True)
    i1 = jnp.min(jnp.where(in_e & (logits == v1), lane, big), axis=1, keepdims=True)
    rest = in_e & (lane != i1)
    v2 = jnp.max(jnp.where(rest, logits, -jnp.inf), axis=1, keepdims=True)
    i2 = jnp.min(jnp.where(rest & (logits == v2), lane, big), axis=1, keepdims=True)
    ex = jnp.exp(v2 - v1)
    den = 1.0 + ex
    w1 = gw / den
    w2 = gw * ex / den
    zero = jnp.zeros_like(logits)
    pick1 = lane == i1
    pick2 = lane == i2
    sel = jnp.where(pick1 | pick2, 1.0, 0.0)
    before = _dot(tri_ref[...], _bf16(sel)) + carry
    rank1 = jnp.sum(jnp.where(pick1, before, zero), axis=1, keepdims=True)
    rank2 = jnp.sum(jnp.where(pick2, before, zero), axis=1, keepdims=True)
    cols = (i1 - N_GROUPS, i2 - N_GROUPS, w1, w2, rank1, rank2)
    route = zero
    for k, col in enumerate(cols):
        route = jnp.where(lane_i == k, col, route)
    return route, carry + jnp.sum(sel, axis=0, keepdims=True)


def _post_tile(attn, conv_ref, x_ref, wmix_ref, g1_ref, b1_ref, wqk_ref, wvo_ref, g2_ref, b2_ref,
               wrhl_ref, br_ref, tri_ref, h2t_ref, rt_ref, cnt_ref, carry_ref):
    mix = _dot(attn, wmix_ref[0:ATTN_WIDTH, :]) + _dot(conv_ref[...], wmix_ref[ATTN_WIDTH:, :])
    h1 = _layer_norm(ALPHA * x_ref[...] + mix, g1_ref[...], b1_ref[...])
    scores = _dot(_bf16(h1), wqk_ref[...])
    probs = []
    for h in range(MEM_HEADS):
        s = scores[:, h * N_MEM:(h + 1) * N_MEM]
        p = jnp.exp(s - jnp.max(s, axis=1, keepdims=True))
        probs.append(_bf16(p / jnp.sum(p, axis=1, keepdims=True)))
    mem_out = _dot(jnp.concatenate(probs, axis=1), wvo_ref[...])
    h2 = _layer_norm(ALPHA * h1 + mem_out, g2_ref[...], b2_ref[...])
    _store_token_tiles(h2t_ref, h2)
    route, carry = _route(h2, wrhl_ref, br_ref, tri_ref, carry_ref[0:1, :])
    rt_ref[...] = route
    carry_ref[...] = jnp.broadcast_to(carry, carry_ref.shape)
    cnt_ref[...] = jnp.broadcast_to(carry, cnt_ref.shape)


def _swa_post_kernel(sinks_ref, q_ref, kc_ref, kp_ref, vc_ref, vp_ref,
                     conv_ref, x_ref, wmix_ref, g1_ref, b1_ref, wqk_ref, wvo_ref, g2_ref, b2_ref,
                     wrhl_ref, br_ref, tri_ref, h2s_ref, rts_ref, cnts_ref,
                     h2t_ref, rt_ref, cnt_ref, carry_ref, attn_s):
    t = pl.program_id(0)
    steps = N_PROMPT // TM_POST
    par = t % 2

    def swa(slot):
        def store(rows, cols, val):
            attn_s[slot, rows, cols] = val
        _swa_tile(t, sinks_ref, q_ref, kc_ref, kp_ref, vc_ref, vp_ref, store)

    def post(slot):
        _post_tile(attn_s[slot], conv_ref, x_ref, wmix_ref, g1_ref, b1_ref, wqk_ref, wvo_ref, g2_ref, b2_ref,
                   wrhl_ref, br_ref, tri_ref, h2t_ref, rt_ref, cnt_ref, carry_ref)

    @pl.when(t == 0)
    def _():
        carry_ref[...] = cnts_ref[...]
        swa(0)

    @pl.when((t >= 1) & (t < steps))
    def _():
        swa(par)
        post(1 - par)

    @pl.when(t == steps)
    def _():
        post(1 - par)

    @pl.when(t == steps + 1)
    def _():
        h2t_ref[...] = h2s_ref[...]
        rt_ref[...] = rts_ref[...]


def _swa_post(sinks, q, kx, vx, conv, x, wmix_b, g1, b1, wqk, wvo, g2, b2, wrhl, br, tri, h2t_s, rt_s, cnt_s):
    n = x.shape[0]
    tm = TM_POST
    assert tm == SWA_QB * WINDOW
    steps = n // tm
    cur = lambda w: pl.BlockSpec((tm, w), lambda i: (jnp.minimum(i, steps - 1), 0))
    prev = lambda w: pl.BlockSpec((WINDOW, w), lambda i: (jnp.clip(SWA_QB * i - 1, 0, n // WINDOW - 1), 0))
    lag = lambda w: pl.BlockSpec((tm, w), lambda i: (jnp.clip(i - 1, 0, steps - 1), 0))
    full = lambda a: pl.BlockSpec(a.shape, lambda i: (0,) * a.ndim)
    weights = (wmix_b, g1, b1, wqk, wvo, g2, b2, wrhl, br, tri, h2t_s, rt_s, cnt_s)
    n_out = n + h2t_s.shape[0] // ROW_CHUNKS
    out_idx = lambda i: (jnp.where(i > steps, steps, jnp.clip(i - 1, 0, steps - 1)), 0)
    return pl.pallas_call(
        _swa_post_kernel,
        grid=(steps + 2,),
        in_specs=([pl.BlockSpec(memory_space=pltpu.SMEM), cur(ATTN_WIDTH),
                   cur(4 * LANES), prev(4 * LANES), cur(4 * LANES), prev(4 * LANES),
                   lag(CONV_CH), lag(D_MODEL)] + [full(a) for a in weights]),
        out_specs=[pl.BlockSpec((tm * ROW_CHUNKS, LANES), out_idx),
                   pl.BlockSpec((tm, LANES), out_idx),
                   pl.BlockSpec((SUBLANES, LANES), lambda i: (0, 0))],
        out_shape=[jax.ShapeDtypeStruct((n_out * ROW_CHUNKS, LANES), jnp.float32),
                   jax.ShapeDtypeStruct((n_out, LANES), jnp.float32),
                   jax.ShapeDtypeStruct((SUBLANES, LANES), jnp.float32)],
        scratch_shapes=[pltpu.VMEM((SUBLANES, LANES), jnp.float32),
                        pltpu.VMEM((2, tm, ATTN_WIDTH), jnp.bfloat16)],
        compiler_params=_params(("arbitrary",)),
        name="swa_post_prompt",
    )(sinks, q, kx, kx, vx, vx, conv, x, *weights)


def _post_a_sample_kernel(attn_ref, conv_ref, x_ref, wmix_ref, g1_ref, b1_ref, wq_ref, h1_ref, qm_ref):
    h1 = _mix_ln1(attn_ref, conv_ref, x_ref, wmix_ref, g1_ref, b1_ref)
    h1_ref[...] = h1
    qm_ref[...] = _mem_q(h1, wq_ref).astype(jnp.float32)


def _post_a_sample(attn, conv, x, wmix_b, g1, b1, wq_b):
    n = x.shape[0]
    args = (attn, conv, x, wmix_b, g1, b1, wq_b)
    full = lambda a: pl.BlockSpec(a.shape, lambda i: (0,) * a.ndim)
    blk = pl.BlockSpec((n, D_MODEL), lambda i: (0, 0))
    return pl.pallas_call(
        _post_a_sample_kernel,
        grid=(1,),
        in_specs=[full(a) for a in args],
        out_specs=[blk, blk],
        out_shape=[jax.ShapeDtypeStruct((n, D_MODEL), jnp.float32),
                   jax.ShapeDtypeStruct((n, D_MODEL), jnp.float32)],
        compiler_params=_params(("arbitrary",)),
        name="post_a_sample",
    )(*args)


MEM_ROWS = MEM_HEADS * DEC_SEQ


def _mem_attn_sample_body(q_ref, mk_ref, mv_ref, o_ref):
    nk = N_MEM * MEM_HEADS
    nb = SAMPLE_BB
    rows = nb * MEM_ROWS
    row_h = (lax.broadcasted_iota(jnp.int32, (rows, nk), 0) % MEM_ROWS) // DEC_SEQ
    key_h = lax.broadcasted_iota(jnp.int32, (rows, nk), 1) % MEM_HEADS
    s = jnp.concatenate(
        [_dot_nt(_bf16(q_ref[b]), _bf16(mk_ref[b].reshape(nk, MEM_HEAD_DIM))) for b in range(nb)], axis=0)
    s = jnp.where(row_h == key_h, s, -jnp.inf)
    p = jnp.exp(s - jnp.max(s, axis=1, keepdims=True))
    rden = 1.0 / jnp.sum(p, axis=1, keepdims=True)
    p = _bf16(p)
    for b in range(nb):
        r = slice(b * MEM_ROWS, (b + 1) * MEM_ROWS)
        o_ref[b] = _dot(p[r], _bf16(mv_ref[b].reshape(nk, MEM_HEAD_DIM))) * rden[r]


def _proj_mem_kernel(x_ref, w_ref, tab_ref, cw_ref, mq_ref, mk_ref, mv_ref,
                     q_ref, kx_ref, vx_ref, conv_ref, ktail_ref, vtail_ref, utail_ref, mo_ref, carry_ref):
    _mem_attn_sample_body(mq_ref, mk_ref, mv_ref, mo_ref)
    _proj_prompt_body(x_ref, w_ref, tab_ref, cw_ref,
                      q_ref, kx_ref, vx_ref, conv_ref, ktail_ref, vtail_ref, utail_ref, carry_ref)


def _proj_prompt_mem_sample(x, w_in_b, tab, conv_w, qm, mk, mv):
    n = x.shape[0]
    nb = mk.shape[0]
    bb = SAMPLE_BB
    steps = nb // bb
    tm = n // steps
    row = lambda w: pl.BlockSpec((tm, w), lambda i: (i, 0))
    full = lambda a: pl.BlockSpec(a.shape, lambda i: (0,) * a.ndim)
    const = lambda r, w: pl.BlockSpec((r, w), lambda i: (0, 0))
    mq = qm.reshape(nb, DEC_SEQ, MEM_HEADS, MEM_HEAD_DIM).transpose(0, 2, 1, 3).reshape(nb, MEM_ROWS, MEM_HEAD_DIM)
    mrows = pl.BlockSpec((bb, MEM_ROWS, MEM_HEAD_DIM), lambda i: (i, 0, 0))
    kv = pl.BlockSpec((bb, N_MEM, MEM_HEADS, MEM_HEAD_DIM), lambda i: (i, 0, 0, 0))
    outs = pl.pallas_call(
        _proj_mem_kernel,
        grid=(steps,),
        in_specs=[row(D_MODEL), full(w_in_b), pl.BlockSpec((ROPE_ROWS, tm), lambda i: (0, i)), full(conv_w),
                  mrows, kv, kv],
        out_specs=[row(ATTN_WIDTH), row(4 * LANES), row(4 * LANES), row(CONV_CH),
                   const(WINDOW, KV_WIDTH), const(WINDOW, KV_WIDTH), const(SUBLANES, CONV_CH), mrows],
        out_shape=[jax.ShapeDtypeStruct((n, ATTN_WIDTH), jnp.bfloat16),
                   jax.ShapeDtypeStruct((n, 4 * LANES), jnp.bfloat16),
                   jax.ShapeDtypeStruct((n, 4 * LANES), jnp.bfloat16),
                   jax.ShapeDtypeStruct((n, CONV_CH), jnp.bfloat16),
                   jax.ShapeDtypeStruct((WINDOW, KV_WIDTH), jnp.float32),
                   jax.ShapeDtypeStruct((WINDOW, KV_WIDTH), jnp.float32),
                   jax.ShapeDtypeStruct((SUBLANES, CONV_CH), jnp.float32),
                   jax.ShapeDtypeStruct((nb, MEM_ROWS, MEM_HEAD_DIM), jnp.float32)],
        scratch_shapes=[pltpu.VMEM((SUBLANES, CONV_CH), jnp.float32)],
        compiler_params=_params(("arbitrary",)),
        name="proj_prompt_mem_sample",
    )(x, w_in_b, tab, conv_w, mq, mk, mv)
    o = outs[7].reshape(nb, MEM_HEADS, DEC_SEQ, MEM_HEAD_DIM).transpose(0, 2, 1, 3).reshape(nb * DEC_SEQ, D_MODEL)
    return outs[:7], o


def _post_b_sample_kernel(o_ref, h1_ref, wo_ref, g2_ref, b2_ref, wrhl_ref, br_ref, tri_ref,
                          h2t_ref, rt_ref, cnt_ref):
    h2 = _layer_norm(ALPHA * h1_ref[...] + _dot(_bf16(o_ref[...]), wo_ref[...]), g2_ref[...], b2_ref[...])
    _store_token_tiles(h2t_ref, h2)
    route, carry = _route(h2, wrhl_ref, br_ref, tri_ref, jnp.zeros((1, LANES), jnp.float32))
    rt_ref[...] = route
    cnt_ref[...] = jnp.broadcast_to(carry, cnt_ref.shape)


def _post_b_sample(o, h1, wo_b, g2, b2, wrhl, br, tri):
    n = h1.shape[0]
    args = (o, h1, wo_b, g2, b2, wrhl, br, tri)
    full = lambda a: pl.BlockSpec(a.shape, lambda i: (0,) * a.ndim)
    return pl.pallas_call(
        _post_b_sample_kernel,
        grid=(1,),
        in_specs=[full(a) for a in args],
        out_specs=[pl.BlockSpec((n * ROW_CHUNKS, LANES), lambda i: (0, 0)),
                   pl.BlockSpec((n, LANES), lambda i: (0, 0)),
                   pl.BlockSpec((SUBLANES, LANES), lambda i: (0, 0))],
        out_shape=[jax.ShapeDtypeStruct((n * ROW_CHUNKS, LANES), jnp.float32),
                   jax.ShapeDtypeStruct((n, LANES), jnp.float32),
                   jax.ShapeDtypeStruct((SUBLANES, LANES), jnp.float32)],
        compiler_params=_params(("arbitrary",)),
        name="post_b_sample",
    )(*args)


def _row_gather_copy(src_hbm, idx, dst, dst_row, sem):
    s0 = pl.multiple_of(idx * ROW_CHUNKS, ROW_CHUNKS)
    d0 = pl.multiple_of(dst_row * ROW_CHUNKS, ROW_CHUNKS)
    return pltpu.make_async_copy(src_hbm.at[pl.ds(s0, ROW_CHUNKS), :], dst.at[pl.ds(d0, ROW_CHUNKS), :], sem)


def _dispatch_kernel(pos_ref, h2t_ref, xs_hbm, sem):
    def body(r, c):
        src = h2t_ref.at[pl.ds(pl.multiple_of(r * ROW_CHUNKS, ROW_CHUNKS), ROW_CHUNKS), :]
        for k in range(2):
            d0 = pl.multiple_of(pos_ref[0, 0, k * TM_COMB + r] * ROW_CHUNKS, ROW_CHUNKS)
            pltpu.make_async_copy(src, xs_hbm.at[pl.ds(d0, ROW_CHUNKS), :], sem.at[0]).start(priority=k)
        return c
    lax.fori_loop(0, TM_COMB, body, 0, unroll=8)
    for _ in range(2):
        pltpu.make_async_copy(h2t_ref, xs_hbm.at[pl.ds(0, TM_COMB * ROW_CHUNKS), :], sem.at[0]).wait()


def _dispatch(pos3, h2t):
    nt = N_ALL // TM_COMB
    return pl.pallas_call(
        _dispatch_kernel,
        grid=(nt,),
        in_specs=[pl.BlockSpec((1, 1, 2 * TM_COMB), lambda i: (i, 0, 0), memory_space=pltpu.SMEM),
                  pl.BlockSpec((TM_COMB * ROW_CHUNKS, LANES), lambda i: (i, 0))],
        out_specs=pl.BlockSpec(memory_space=pl.ANY),
        out_shape=jax.ShapeDtypeStruct((N_ASSIGN * ROW_CHUNKS, LANES), jnp.float32),
        scratch_shapes=[pltpu.SemaphoreType.DMA((1,))],
        compiler_params=_params(("arbitrary",)),
        name="moe_dispatch",
    )(pos3, h2t)


def _moe_ffn_kernel(it_ref, ie_ref, lo_ref, hi_ref, x_ref, wg_ref, wu_ref, wd_ref, y_ref, wgb, wub, wdb, cur_e):
    i = pl.program_id(0)
    lo = lo_ref[i]
    hi = hi_ref[i]
    e = ie_ref[i]

    @pl.when(i == 0)
    def _():
        cur_e[0] = -1

    @pl.when((hi > lo) & (cur_e[0] != e))
    def _():
        wgb[...] = _bf16(wg_ref[0])
        wub[...] = _bf16(wu_ref[0])
        wdb[...] = _bf16(wd_ref[0])
        cur_e[0] = e

    def ffn():
        x = _bf16(_load_token_tiles(x_ref, 0, TM_MOE))
        hg = _dot(x, wgb[...])
        hu = _dot(x, wub[...])
        h = hg / (1.0 + jnp.exp(-hg)) * hu
        return _dot(_bf16(h), wdb[...])

    def rows_mask():
        row = lax.broadcasted_iota(jnp.int32, (TM_MOE, LANES), 0)
        return (row >= lo) & (row < hi)

    @pl.when((hi > lo) & (lo == 0))
    def _():
        y = ffn()
        mask = rows_mask()
        for c in range(ROW_CHUNKS):
            y_ref[pl.ds(c, TM_MOE, stride=ROW_CHUNKS), :] = jnp.where(mask, y[:, c * LANES:(c + 1) * LANES], 0.0)

    @pl.when((hi > lo) & (lo > 0))
    def _():
        y = ffn()
        mask = rows_mask()
        for c in range(ROW_CHUNKS):
            sl = pl.ds(c, TM_MOE, stride=ROW_CHUNKS)
            y_ref[sl, :] = jnp.where(mask, y[:, c * LANES:(c + 1) * LANES], y_ref[sl, :])


def _moe_ffn(item_tile, item_expert, item_lo, item_hi, x_sorted, w_gate, w_up, w_down):
    wspec = lambda shp: pl.BlockSpec((1,) + shp, lambda i, it, ie, lo, hi: (ie[i], 0, 0))
    tile = pl.BlockSpec((TM_MOE * ROW_CHUNKS, LANES), lambda i, it, ie, lo, hi: (it[i], 0))
    grid_spec = pltpu.PrefetchScalarGridSpec(
        num_scalar_prefetch=4,
        grid=(MOE_ITEMS,),
        in_specs=[tile, wspec((D_MODEL, EXPERT_FF)), wspec((D_MODEL, EXPERT_FF)), wspec((EXPERT_FF, D_MODEL))],
        out_specs=tile,
        scratch_shapes=[pltpu.VMEM((D_MODEL, EXPERT_FF), jnp.bfloat16),
                        pltpu.VMEM((D_MODEL, EXPERT_FF), jnp.bfloat16),
                        pltpu.VMEM((EXPERT_FF, D_MODEL), jnp.bfloat16),
                        pltpu.SMEM((1,), jnp.int32)],
    )
    return pl.pallas_call(
        _moe_ffn_kernel,
        grid_spec=grid_spec,
        out_shape=jax.ShapeDtypeStruct((N_ASSIGN * ROW_CHUNKS, LANES), jnp.float32),
        compiler_params=_params(("arbitrary",)),
        name="moe_ffn",
    )(item_tile, item_expert, item_lo, item_hi, x_sorted, w_gate, w_up, w_down)


def _combine_kernel(nt, pos_cur_ref, pos_nxt_ref, yt_hbm, h2t_ref, rt_ref, g3_ref, b3_ref, o_ref, abuf, sem):
    t = pl.program_id(0)
    slot = t % 2
    rows = 2 * TM_COMB

    def issue(pos_ref, s):
        def body(j, c):
            for k in range(2):
                r = 2 * j + k
                _row_gather_copy(yt_hbm, pos_ref[0, 0, r], abuf, s * rows + r, sem.at[s]).start(priority=k)
            return c
        lax.fori_loop(0, rows // 2, body, 0, unroll=4)

    @pl.when(t == 0)
    def _():
        issue(pos_cur_ref, 0)

    @pl.when(t + 1 < nt)
    def _():
        issue(pos_nxt_ref, 1 - slot)

    base = pl.multiple_of(slot * (rows * ROW_CHUNKS), rows * ROW_CHUNKS)
    pltpu.make_async_copy(yt_hbm.at[pl.ds(0, rows * ROW_CHUNKS), :],
                          abuf.at[pl.ds(base, rows * ROW_CHUNKS), :], sem.at[slot]).wait()
    ya = _load_token_tiles(abuf, base, TM_COMB)
    yb = _load_token_tiles(abuf, base + TM_COMB * ROW_CHUNKS, TM_COMB)
    rt = rt_ref[...]
    ff = rt[:, 2:3] * ya + rt[:, 3:4] * yb
    h2 = _load_token_tiles(h2t_ref, 0, TM_COMB)
    o_ref[...] = _layer_norm(ALPHA * h2 + ff, g3_ref[...], b3_ref[...])


def _combine(pos3, yt, h2t, rt, g3, b3, tile0, n_tiles):
    last = tile0 + n_tiles - 1
    smem_pos = lambda f: pl.BlockSpec((1, 1, 2 * TM_COMB), f, memory_space=pltpu.SMEM)
    full = lambda a: pl.BlockSpec(a.shape, lambda i: (0,) * a.ndim)
    return pl.pallas_call(
        functools.partial(_combine_kernel, n_tiles),
        grid=(n_tiles,),
        in_specs=[smem_pos(lambda i: (tile0 + i, 0, 0)),
                  smem_pos(lambda i: (jnp.minimum(tile0 + i + 1, last), 0, 0)),
                  pl.BlockSpec(memory_space=pl.ANY),
                  pl.BlockSpec((TM_COMB * ROW_CHUNKS, LANES), lambda i: (tile0 + i, 0)),
                  pl.BlockSpec((TM_COMB, LANES), lambda i: (tile0 + i, 0)),
                  full(g3), full(b3)],
        out_specs=pl.BlockSpec((TM_COMB, D_MODEL), lambda i: (i, 0)),
        out_shape=jax.ShapeDtypeStruct((n_tiles * TM_COMB, D_MODEL), jnp.float32),
        scratch_shapes=[pltpu.VMEM((2 * 2 * TM_COMB * ROW_CHUNKS, LANES), jnp.float32),
                        pltpu.SemaphoreType.DMA((2,))],
        compiler_params=_params(("arbitrary",)),
        name="moe_combine",
    )(pos3, pos3, yt, h2t, rt, g3, b3)


POS_TILES = 3


def _positions_kernel(rt_ref, starts_ref, pos_ref):
    expert = lax.broadcasted_iota(jnp.int32, (N_EXPERTS, TM_COMB), 0).astype(jnp.float32)
    starts = jnp.concatenate([starts_ref[...]] * (TM_COMB // LANES), axis=1)
    for j in range(POS_TILES):
        cols = rt_ref[j * TM_COMB:(j + 1) * TM_COMB, :].T
        out = []
        for k in range(2):
            seg = jnp.sum(jnp.where(expert == cols[k:k + 1, :], starts, 0.0), axis=0, keepdims=True)
            out.append(seg + cols[4 + k:5 + k, :])
        pos_ref[j] = jnp.concatenate(out, axis=1).astype(jnp.int32)


def _positions(rt, starts_rep):
    nt = N_ALL // TM_COMB
    assert nt % POS_TILES == 0
    return pl.pallas_call(
        _positions_kernel,
        grid=(nt // POS_TILES,),
        in_specs=[pl.BlockSpec((POS_TILES * TM_COMB, LANES), lambda i: (i, 0)),
                  pl.BlockSpec((N_EXPERTS, LANES), lambda i: (0, 0))],
        out_specs=pl.BlockSpec((POS_TILES, 1, 2 * TM_COMB), lambda i: (i, 0, 0)),
        out_shape=jax.ShapeDtypeStruct((nt, 1, 2 * TM_COMB), jnp.int32),
        compiler_params=_params(("arbitrary",)),
        name="moe_positions",
    )(rt, starts_rep)


def _routing_plan(rt, cnt):
    i32 = jnp.int32
    counts_f = cnt[0, N_GROUPS:N_GROUPS + N_EXPERTS]
    starts_f = jnp.cumsum(counts_f) - counts_f
    pos3 = _positions(rt, jnp.broadcast_to(starts_f[:, None], (N_EXPERTS, LANES)))
    starts = starts_f.astype(i32)
    tiles = jnp.arange(MOE_TILES, dtype=i32) * TM_MOE
    rank_t = jnp.arange(MOE_TILES, dtype=i32) + jnp.sum((starts[None, :] < tiles[:, None]).astype(i32), axis=1)
    rank_s = jnp.arange(N_EXPERTS, dtype=i32) + jnp.sum((tiles[None, :] <= starts[:, None]).astype(i32), axis=1)
    vals = jnp.concatenate([tiles, starts])
    ranks = jnp.concatenate([rank_t, rank_s])
    slot = jnp.arange(MOE_ITEMS, dtype=i32)
    lo = jnp.sum(jnp.where(ranks[None, :] == slot[:, None], vals[None, :], 0), axis=1)
    hi = jnp.concatenate([lo[1:], jnp.full((1,), N_ASSIGN, i32)])
    item_tile = jnp.minimum(lo // TM_MOE, MOE_TILES - 1)
    item_expert = jnp.clip(jnp.sum((starts[None, :] <= lo[:, None]).astype(i32), axis=1) - 1, 0, N_EXPERTS - 1)
    base = item_tile * TM_MOE
    return item_tile, item_expert, lo - base, hi - base, pos3


def kernel(x_prompt, x_sample, mem_prompt, cache_swa_k, cache_swa_v, cache_conv, cache_mem_k, cache_mem_v,
           w_in, sinks, conv_w, w_mix_out, ln1_g, ln1_b, w_q_mem, w_k_mem, w_v_mem, w_o_mem, ln2_g, ln2_b,
           w_router_group, b_router_group, w_router_expert, b_router_expert, w_gate, w_up, w_down,
           ln3_g, ln3_b):
    f32 = jnp.float32
    row = lambda a: a.reshape(1, -1).astype(f32)
    w_in_b, wmix_b, wq_b, wk_b, wv_b, wo_b = (_bf16(w) for w in (w_in, w_mix_out, w_q_mem, w_k_mem, w_v_mem, w_o_mem))
    g1, b1, g2, b2, g3, b3 = (row(a) for a in (ln1_g, ln1_b, ln2_g, ln2_b, ln3_g, ln3_b))
    pad = LANES - N_GROUPS - N_EXPERTS
    wr = jnp.concatenate([w_router_group, w_router_expert, jnp.zeros((D_MODEL, pad), f32)], axis=1)
    wrh = _bf16(wr)
    wrhl = jnp.concatenate([wrh, _bf16(wr - wrh.astype(f32))], axis=1)
    br = jnp.concatenate([b_router_group, b_router_expert, jnp.zeros((pad,), f32)]).reshape(1, LANES)

    xs = x_sample.reshape(N_SAMPLE, D_MODEL)
    tab_s = jnp.tile(_rope_table(PAST_LEN + jnp.arange(DEC_SEQ)), (1, DEC_BATCH))
    c0 = jnp.repeat(cache_conv[:, 0], DEC_SEQ, axis=0)
    c1 = jnp.repeat(cache_conv[:, 1], DEC_SEQ, axis=0)
    q_s, k_s, v_s, conv_s, u_s = _proj_sample(xs, w_in_b, tab_s, conv_w, c0, c1)
    attn_s, swa_k_s, swa_v_s = _swa_sample(sinks, q_s, k_s, v_s, cache_swa_k, cache_swa_v)
    h1_s, qm_s = _post_a_sample(attn_s, conv_s, xs, wmix_b, g1, b1, wq_b)
    xp = x_prompt.reshape(N_PROMPT, D_MODEL)
    tab_p = _rope_table(jnp.arange(N_PROMPT))
    (q_p, kx_p, vx_p, conv_p, k_tail, v_tail, u_tail), o_s = _proj_prompt_mem_sample(
        xp, w_in_b, tab_p, conv_w, qm_s, cache_mem_k, cache_mem_v)
    tri = _bf16(jnp.tril(jnp.ones((TM_POST, TM_POST), f32), -1))
    h2t_s, rt_s, cnt_s = _post_b_sample(o_s, h1_s, wo_b, g2, b2, wrhl, br, tri)

    mk, mv, wqk, wvo = _mem_kv(mem_prompt.reshape(N_MEM, D_MODEL), wk_b, wv_b, wq_b, wo_b)
    h2t, rt, cnt = _swa_post(sinks, q_p, kx_p, vx_p, conv_p, xp, wmix_b, g1, b1, wqk, wvo, g2, b2,
                             wrhl, br, tri, h2t_s, rt_s, cnt_s)

    item_tile, item_expert, item_lo, item_hi, pos3 = _routing_plan(rt, cnt)
    x_sorted = _dispatch(pos3, h2t)
    yt = _moe_ffn(item_tile, item_expert, item_lo, item_hi, x_sorted, w_gate, w_up, w_down)
    y_p = _combine(pos3, yt, h2t, rt, g3, b3, 0, N_PROMPT // TM_COMB)
    y_s = _combine(pos3, yt, h2t, rt, g3, b3, N_PROMPT // TM_COMB, N_SAMPLE // TM_COMB)

    return (y_p.reshape(1, SEQ, D_MODEL),
            y_s.reshape(DEC_BATCH, DEC_SEQ, D_MODEL),
            k_tail.reshape(1, WINDOW, N_KV_HEADS, HEAD_DIM),
            v_tail.reshape(1, WINDOW, N_KV_HEADS, HEAD_DIM),
            u_tail[SUBLANES - (CONV_K - 1):].reshape(1, CONV_K - 1, CONV_CH),
            mk.reshape(1, N_MEM, MEM_HEADS, MEM_HEAD_DIM),
            mv.reshape(1, N_MEM, MEM_HEADS, MEM_HEAD_DIM),
            swa_k_s.reshape(DEC_BATCH, WINDOW, N_KV_HEADS, HEAD_DIM),
            swa_v_s.reshape(DEC_BATCH, WINDOW, N_KV_HEADS, HEAD_DIM),
            u_s.reshape(DEC_BATCH, DEC_SEQ, CONV_CH)[:, DEC_SEQ - (CONV_K - 1):])
```

```python
import functools

import jax
import jax.numpy as jnp
from jax import lax
from jax.experimental import pallas as pl
from jax.experimental.pallas import tpu as pltpu

D_MODEL = 1024
SEQ = 16384
DEC_BATCH = 128
DEC_SEQ = 4
PAST_LEN = 16384
ATTN_WIDTH = 512
CONV_CH = 512
HEAD_DIM = 64
N_HEADS = 8
N_KV_HEADS = 2
KV_WIDTH = 128
WINDOW = 128
ROPE_THETA = 500000.0
ROPE_DIM = 16
CONV_K = 3
Q_END = ATTN_WIDTH
K_END = Q_END + KV_WIDTH
V_END = K_END + KV_WIDTH
B_END = V_END + CONV_CH
C_END = B_END + CONV_CH
IN_WIDTH = C_END + CONV_CH
N_MEM = 256
MEM_HEADS = 4
MEM_HEAD_DIM = 256
N_GROUPS = 4
EXPERTS_PER_GROUP = 8
N_EXPERTS = 32
EXPERT_FF = 256
ALPHA = 2.0 ** 0.25
LN_EPS = 1e-5

LANES = 128
SUBLANES = 8
ROW_CHUNKS = D_MODEL // LANES
VMEM_LIMIT = 56 * 1024 * 1024

N_PROMPT = SEQ
N_SAMPLE = DEC_BATCH * DEC_SEQ
N_ALL = N_PROMPT + N_SAMPLE
TM_POST = 512
TM_MOE = 512
TM_COMB = 512
N_ASSIGN = 2 * N_ALL
MOE_TILES = N_ASSIGN // TM_MOE
MOE_ITEMS = MOE_TILES + N_EXPERTS
SAMPLE_BB = 4
SWA_BB = 16

assert ROW_CHUNKS == SUBLANES
assert N_SAMPLE == TM_POST
assert N_ASSIGN % TM_MOE == 0 and N_ALL % TM_COMB == 0


def _params(sem, vmem=VMEM_LIMIT):
    return pltpu.CompilerParams(dimension_semantics=sem, vmem_limit_bytes=vmem)


def _bf16(x):
    return x.astype(jnp.bfloat16)


def _dot(a, b):
    return jnp.dot(a, b, preferred_element_type=jnp.float32)


def _dot_nt(a, b):
    return lax.dot_general(a, b, (((1,), (1,)), ((), ())), preferred_element_type=jnp.float32)


def _layer_norm(x, g, b):
    mu = jnp.mean(x, axis=-1, keepdims=True)
    xc = x - mu
    var = jnp.mean(xc * xc, axis=-1, keepdims=True)
    return xc * lax.rsqrt(var + LN_EPS) * g + b


def _rope(x, cos_t, sin_t):
    lane = lax.broadcasted_iota(jnp.int32, x.shape, 1) % HEAD_DIM
    half = ROPE_DIM // 2
    partner = jnp.where(lane < half, pltpu.roll(x, LANES - half, axis=1), pltpu.roll(x, half, axis=1))
    return x * cos_t + partner * sin_t


def _head_slabs(x):
    lane = lax.broadcasted_iota(jnp.int32, x.shape, 1)
    lo = lane < HEAD_DIM
    sw = pltpu.roll(x, HEAD_DIM, axis=1)
    zero = jnp.zeros_like(x)
    slabs = [jnp.where(lo, x, zero), jnp.where(lo, zero, sw), jnp.where(lo, sw, zero), jnp.where(lo, zero, x)]
    return _bf16(jnp.concatenate(slabs, axis=1))


def _store_token_tiles(ref, val):
    rows = val.shape[0]
    for c in range(ROW_CHUNKS):
        ref[pl.ds(c, rows, stride=ROW_CHUNKS), :] = val[:, c * LANES:(c + 1) * LANES]


def _load_token_tiles(ref, base, rows):
    return jnp.concatenate(
        [ref[pl.ds(base + c, rows, stride=ROW_CHUNKS), :] for c in range(ROW_CHUNKS)], axis=1)


ROPE_ONE = 3 * (ROPE_DIM // 2)
ROPE_ROWS = 32


def _rope_patterns(tab):
    half = ROPE_DIM // 2
    m = lax.broadcasted_iota(jnp.int32, tab.shape, 1) % HEAD_DIM
    idx_c = jnp.where(m < ROPE_DIM, m % half, ROPE_ONE)
    idx_s = jnp.where(m < half, 2 * half + m, jnp.where(m < ROPE_DIM, m, ROPE_ONE + 1))
    return jnp.take_along_axis(tab, idx_c, axis=1), jnp.take_along_axis(tab, idx_s, axis=1)


def _proj_common(x_ref, w_ref, tab_ref):
    xb = _bf16(x_ref[...])
    tab = tab_ref[...]
    pad = jnp.zeros((LANES - tab.shape[0], tab.shape[1]), jnp.float32)
    cos_t, sin_t = _rope_patterns(jnp.concatenate([tab, pad], axis=0).T)
    q = _dot(xb, w_ref[:, 0:Q_END])
    q_rot = jnp.concatenate(
        [_rope(q[:, p * LANES:(p + 1) * LANES], cos_t, sin_t) for p in range(ATTN_WIDTH // LANES)], axis=1)
    q_out = _bf16(q_rot * (HEAD_DIM ** -0.5))
    kv = _dot(xb, w_ref[:, Q_END:V_END])
    k = _rope(kv[:, 0:KV_WIDTH], cos_t, sin_t)
    v = kv[:, KV_WIDTH:]
    bg = _dot(xb, w_ref[:, V_END:B_END])
    u = _dot(xb, w_ref[:, B_END:C_END]) * _dot(xb, w_ref[:, C_END:IN_WIDTH])
    return q_out, k, v, bg, u


def _conv3(bg, u, u1, u2, cw_ref):
    cw = cw_ref[...]
    return bg * (cw[0:1, :] * u2 + cw[1:2, :] * u1 + cw[2:3, :] * u)


def _proj_prompt_body(x_ref, w_ref, tab_ref, cw_ref,
                      q_ref, kx_ref, vx_ref, conv_ref, ktail_ref, vtail_ref, utail_ref, carry_ref):
    @pl.when(pl.program_id(0) == 0)
    def _():
        carry_ref[...] = jnp.zeros_like(carry_ref)

    q_out, k, v, bg, u = _proj_common(x_ref, w_ref, tab_ref)
    tm = u.shape[0]
    ext = jnp.concatenate([carry_ref[...], u], axis=0)
    u1 = pltpu.roll(ext, 1, axis=0)[SUBLANES:SUBLANES + tm]
    u2 = pltpu.roll(ext, 2, axis=0)[SUBLANES:SUBLANES + tm]
    q_ref[...] = q_out
    kx_ref[...] = _head_slabs(k)
    vx_ref[...] = _head_slabs(v)
    conv_ref[...] = _bf16(_conv3(bg, u, u1, u2, cw_ref))
    ktail_ref[...] = k[tm - WINDOW:tm]
    vtail_ref[...] = v[tm - WINDOW:tm]
    utail_ref[...] = u[tm - SUBLANES:tm]
    carry_ref[...] = u[tm - SUBLANES:tm]


def _proj_sample_kernel(x_ref, w_ref, tab_ref, cw_ref, c0_ref, c1_ref,
                        q_ref, k_ref, v_ref, conv_ref, u_ref):
    q_out, k, v, bg, u = _proj_common(x_ref, w_ref, tab_ref)
    t = lax.broadcasted_iota(jnp.int32, u.shape, 0) % DEC_SEQ
    c0 = c0_ref[...]
    c1 = c1_ref[...]
    u1 = jnp.where(t >= 1, pltpu.roll(u, 1, axis=0), c1)
    u2 = jnp.where(t >= 2, pltpu.roll(u, 2, axis=0), jnp.where(t == 1, c1, c0))
    q_ref[...] = q_out.astype(jnp.float32)
    k_ref[...] = k
    v_ref[...] = v
    conv_ref[...] = _bf16(_conv3(bg, u, u1, u2, cw_ref))
    u_ref[...] = u


def _rope_table(pos):
    half = ROPE_DIM // 2
    inv = ROPE_THETA ** (-jnp.arange(0, ROPE_DIM, 2, dtype=jnp.float32) / ROPE_DIM)
    ang = pos.astype(jnp.float32)[None, :] * inv[:, None]
    cos, sin = jnp.cos(ang), jnp.sin(ang)
    n = pos.shape[0]
    assert ROPE_ONE == 3 * half
    return jnp.concatenate([cos, sin, -sin, jnp.ones((1, n), jnp.float32),
                            jnp.zeros((ROPE_ROWS - ROPE_ONE - 1, n), jnp.float32)], axis=0)


def _proj_sample(x, w_in_b, tab, conv_w, c0, c1):
    n = x.shape[0]
    full = lambda a: pl.BlockSpec(a.shape, lambda i: (0,) * a.ndim)
    out = lambda w, dt: jax.ShapeDtypeStruct((n, w), dt)
    blk = lambda w: pl.BlockSpec((n, w), lambda i: (0, 0))
    return pl.pallas_call(
        _proj_sample_kernel,
        grid=(1,),
        in_specs=[full(x), full(w_in_b), full(tab), full(conv_w), full(c0), full(c1)],
        out_specs=[blk(ATTN_WIDTH), blk(KV_WIDTH), blk(KV_WIDTH), blk(CONV_CH), blk(CONV_CH)],
        out_shape=[out(ATTN_WIDTH, jnp.float32), out(KV_WIDTH, jnp.float32), out(KV_WIDTH, jnp.float32),
                   out(CONV_CH, jnp.bfloat16), out(CONV_CH, jnp.float32)],
        compiler_params=_params(("arbitrary",)),
        name="proj_sample",
    )(x, w_in_b, tab, conv_w, c0, c1)


def _sink_softmax_pv(s, valid, sink, vx):
    s = jnp.where(valid, s, -jnp.inf)
    m = jnp.maximum(jnp.max(s, axis=1, keepdims=True), sink)
    p = jnp.exp(s - m)
    den = jnp.sum(p, axis=1, keepdims=True) + jnp.exp(sink - m)
    return _dot(_bf16(p), vx) / den


SWA_QB = 4


def _swa_tile(step, sinks_ref, q_ref, kc_ref, kp_ref, vc_ref, vp_ref, store):
    kall = jnp.concatenate([kp_ref[...], kc_ref[...]], axis=0)
    vall = jnp.concatenate([vp_ref[...], vc_ref[...]], axis=0)
    i = lax.broadcasted_iota(jnp.int32, (WINDOW, 2 * WINDOW), 0)
    j = lax.broadcasted_iota(jnp.int32, (WINDOW, 2 * WINDOW), 1)
    band = (j > i) & (j <= i + WINDOW)
    for sb in range(SWA_QB):
        rows = slice(sb * WINDOW, (sb + 1) * WINDOW)
        kcat = kall[sb * WINDOW:(sb + 2) * WINDOW]
        vcat = vall[sb * WINDOW:(sb + 2) * WINDOW]
        valid = band & ((step > 0) | (j >= WINDOW)) if sb == 0 else band
        for p in range(N_HEADS // 2):
            qs = q_ref[rows, p * LANES:(p + 1) * LANES]
            acc = None
            for e in range(2):
                hd = 2 * p + e
                slab = 2 * (hd // (N_HEADS // N_KV_HEADS)) + e
                kx = kcat[:, slab * LANES:(slab + 1) * LANES]
                vx = vcat[:, slab * LANES:(slab + 1) * LANES]
                o = _sink_softmax_pv(_dot_nt(qs, kx), valid, sinks_ref[hd], vx)
                acc = o if acc is None else acc + o
            store(rows, slice(p * LANES, (p + 1) * LANES), _bf16(acc))


SWA_ROWS = N_HEADS * DEC_SEQ
NEW_ROWS = 2 * SUBLANES


def _swa_sample_kernel(q_ref, sink_ref, kn_ref, vn_ref, kt_ref, vt_ref, o_ref, okt_ref, ovt_ref):
    nb = SWA_BB
    rows = nb * SWA_ROWS
    t = lax.broadcasted_iota(jnp.int32, (rows, WINDOW), 0) % DEC_SEQ
    valid_c = lax.broadcasted_iota(jnp.int32, (rows, WINDOW), 1) > t
    valid_n = (lax.broadcasted_iota(jnp.int32, (rows, NEW_ROWS), 1)
               <= lax.broadcasted_iota(jnp.int32, (rows, NEW_ROWS), 0) % DEC_SEQ)
    sink = jnp.concatenate([sink_ref[:, 0:1]] * nb, axis=0)
    qs = [_bf16(q_ref[b]) for b in range(nb)]
    s_c = jnp.concatenate([_dot(qs[b], _bf16(kt_ref[b])) for b in range(nb)], axis=0)
    s_n = jnp.concatenate([_dot_nt(qs[b], _bf16(kn_ref[b])) for b in range(nb)], axis=0)
    s_c = jnp.where(valid_c, s_c, -jnp.inf)
    s_n = jnp.where(valid_n, s_n, -jnp.inf)
    m = jnp.maximum(jnp.maximum(jnp.max(s_c, axis=1, keepdims=True), jnp.max(s_n, axis=1, keepdims=True)), sink)
    p_c = jnp.exp(s_c - m)
    p_n = jnp.exp(s_n - m)
    rden = 1.0 / (jnp.sum(p_c, axis=1, keepdims=True) + jnp.sum(p_n, axis=1, keepdims=True) + jnp.exp(sink - m))
    p_c, p_n = _bf16(p_c), _bf16(p_n)
    lane = lax.broadcasted_iota(jnp.int32, (KV_WIDTH, WINDOW), 1)
    shift = WINDOW - DEC_SEQ
    zrows = jnp.zeros((KV_WIDTH - NEW_ROWS, KV_WIDTH), jnp.float32)
    for b in range(nb):
        r = slice(b * SWA_ROWS, (b + 1) * SWA_ROWS)
        kt, vt = kt_ref[b], vt_ref[b]
        kn, vn = kn_ref[b], vn_ref[b]
        o_ref[b] = (_dot_nt(p_c[r], _bf16(vt)) + _dot(p_n[r], _bf16(vn))) * rden[r]
        for old, new, dst in ((kt, kn, okt_ref), (vt, vn, ovt_ref)):
            new_cols = pltpu.roll(jnp.concatenate([new, zrows], axis=0).T, shift, axis=1)
            dst[b] = jnp.where(lane >= shift, new_cols, pltpu.roll(old, shift, axis=1))


def _swa_sample(sinks, q, kn, vn, cache_k, cache_v):
    nb = cache_k.shape[0]
    bb = SWA_BB
    groups = N_HEADS // N_KV_HEADS
    qh = q.reshape(nb, DEC_SEQ, N_KV_HEADS, groups, HEAD_DIM).transpose(0, 2, 3, 1, 4)
    qh = qh.reshape(nb, N_KV_HEADS, groups * DEC_SEQ, HEAD_DIM)
    zeros = jnp.zeros_like(qh[:, 0])
    qbd = jnp.concatenate([jnp.concatenate([qh[:, 0], zeros], axis=-1),
                           jnp.concatenate([zeros, qh[:, 1]], axis=-1)], axis=1)
    sink_col = jnp.broadcast_to(jnp.repeat(sinks, DEC_SEQ).reshape(SWA_ROWS, 1), (SWA_ROWS, LANES))
    pad8 = lambda a: jnp.pad(a.reshape(nb, DEC_SEQ, KV_WIDTH), ((0, 0), (0, NEW_ROWS - DEC_SEQ), (0, 0)))
    to_t = lambda c: c.transpose(0, 2, 3, 1).reshape(nb, KV_WIDTH, WINDOW)
    blk = lambda r, w: pl.BlockSpec((bb, r, w), lambda i: (i, 0, 0))
    o, okt, ovt = pl.pallas_call(
        _swa_sample_kernel,
        grid=(nb // bb,),
        in_specs=[blk(SWA_ROWS, KV_WIDTH), pl.BlockSpec((SWA_ROWS, LANES), lambda i: (0, 0)),
                  blk(NEW_ROWS, KV_WIDTH), blk(NEW_ROWS, KV_WIDTH), blk(KV_WIDTH, WINDOW), blk(KV_WIDTH, WINDOW)],
        out_specs=[blk(SWA_ROWS, KV_WIDTH), blk(KV_WIDTH, WINDOW), blk(KV_WIDTH, WINDOW)],
        out_shape=[jax.ShapeDtypeStruct((nb, SWA_ROWS, KV_WIDTH), jnp.float32),
                   jax.ShapeDtypeStruct((nb, KV_WIDTH, WINDOW), jnp.float32),
                   jax.ShapeDtypeStruct((nb, KV_WIDTH, WINDOW), jnp.float32)],
        compiler_params=_params(("arbitrary",)),
        name="swa_sample",
    )(qbd, sink_col, pad8(kn), pad8(vn), to_t(cache_k), to_t(cache_v))
    o = o.reshape(nb, N_KV_HEADS, groups, DEC_SEQ, N_KV_HEADS, HEAD_DIM)
    attn = jnp.stack([o[:, h, :, :, h, :] for h in range(N_KV_HEADS)], axis=1)
    attn = attn.transpose(0, 3, 1, 2, 4).reshape(nb * DEC_SEQ, ATTN_WIDTH)
    from_t = lambda c: c.reshape(nb, N_KV_HEADS, HEAD_DIM, WINDOW).transpose(0, 3, 1, 2)
    return attn, from_t(okt), from_t(ovt)


def _mem_kv_kernel(mem_ref, wk_ref, wv_ref, wq_ref, wo_ref, mk_ref, mv_ref, wqk_ref, wvo_ref):
    mb = _bf16(mem_ref[...])
    mk = _dot(mb, wk_ref[...])
    mv = _dot(mb, wv_ref[...])
    mk_ref[...] = mk
    mv_ref[...] = mv
    mkb, mvb = _bf16(mk), _bf16(mv)
    for h in range(MEM_HEADS):
        sl = slice(h * MEM_HEAD_DIM, (h + 1) * MEM_HEAD_DIM)
        keys = slice(h * N_MEM, (h + 1) * N_MEM)
        wqk_ref[:, keys] = _bf16(_dot_nt(wq_ref[:, sl], mkb[:, sl]) * (MEM_HEAD_DIM ** -0.5))
        wvo_ref[keys, :] = _bf16(_dot(mvb[:, sl], wo_ref[sl, :]))


def _mem_kv(mem, wk_b, wv_b, wq_b, wo_b):
    full = lambda a: pl.BlockSpec(a.shape, lambda i: (0,) * a.ndim)
    blk = pl.BlockSpec((N_MEM, D_MODEL), lambda i: (0, 0))
    f32 = jax.ShapeDtypeStruct((N_MEM, D_MODEL), jnp.float32)
    fused = (D_MODEL, MEM_HEADS * N_MEM), (MEM_HEADS * N_MEM, D_MODEL)
    return pl.pallas_call(
        _mem_kv_kernel,
        grid=(1,),
        in_specs=[full(mem), full(wk_b), full(wv_b), full(wq_b), full(wo_b)],
        out_specs=[blk, blk] + [pl.BlockSpec(shp, lambda i: (0, 0)) for shp in fused],
        out_shape=[f32, f32] + [jax.ShapeDtypeStruct(shp, jnp.bfloat16) for shp in fused],
        compiler_params=_params(("arbitrary",)),
        name="mem_kv",
    )(mem, wk_b, wv_b, wq_b, wo_b)


def _mix_ln1(attn_ref, conv_ref, x_ref, wmix_ref, g1_ref, b1_ref):
    mix = _dot(_bf16(attn_ref[...]), wmix_ref[0:ATTN_WIDTH, :]) + _dot(conv_ref[...], wmix_ref[ATTN_WIDTH:, :])
    return _layer_norm(ALPHA * x_ref[...] + mix, g1_ref[...], b1_ref[...])


def _mem_q(h1, wq_ref):
    return _bf16(_dot(_bf16(h1), wq_ref[...]) * (MEM_HEAD_DIM ** -0.5))


def _route(h2, wrhl_ref, br_ref, tri_ref, carry):
    hi = _bf16(h2)
    lo = _bf16(h2 - hi.astype(jnp.float32))
    hh = _dot(hi, wrhl_ref[...])
    logits = hh[:, 0:LANES] + hh[:, LANES:] + _dot(lo, wrhl_ref[:, 0:LANES]) + br_ref[...]
    lane_i = lax.broadcasted_iota(jnp.int32, logits.shape, 1)
    lane = lane_i.astype(jnp.float32)
    big = jnp.float32(LANES)
    is_g = lane_i < N_GROUPS
    gl = jnp.where(is_g, logits, -jnp.inf)
    gmax = jnp.max(gl, axis=1, keepdims=True)
    gidx = jnp.min(jnp.where(is_g & (logits == gmax), lane, big), axis=1, keepdims=True)
    gsum = jnp.sum(jnp.exp(gl - gmax), axis=1, keepdims=True)
    gw = 1.0 / gsum
    eid = lane_i - N_GROUPS
    assert EXPERTS_PER_GROUP == 8
    grp = lax.shift_right_arithmetic(eid, jnp.full_like(eid, 3)).astype(jnp.float32)
    in_e = (lane_i >= N_GROUPS) & (lane_i < N_GROUPS + N_EXPERTS) & (grp == gidx)
    v1 = jnp.max(jnp.where(in_e, logits, -jnp.inf), axis=1, keepdims=True)
    i1 = jnp.min(jnp.where(in_e & (logits == v1), lane, big), axis=1, keepdims=True)
    rest = in_e & (lane != i1)
    v2 = jnp.max(jnp.where(rest, logits, -jnp.inf), axis=1, keepdims=True)
    i2 = jnp.min(jnp.where(rest & (logits == v2), lane, big), axis=1, keepdims=True)
    ex = jnp.exp(v2 - v1)
    den = 1.0 + ex
    w1 = gw / den
    w2 = gw * ex / den
    zero = jnp.zeros_like(logits)
    pick1 = lane == i1
    pick2 = lane == i2
    sel = jnp.where(pick1 | pick2, 1.0, 0.0)
    before = _dot(tri_ref[...], _bf16(sel)) + carry
    rank1 = jnp.sum(jnp.where(pick1, before, zero), axis=1, keepdims=True)
    rank2 = jnp.sum(jnp.where(pick2, before, zero), axis=1, keepdims=True)
    cols = (i1 - N_GROUPS, i2 - N_GROUPS, w1, w2, rank1, rank2)
    route = zero
    for k, col in enumerate(cols):
        route = jnp.where(lane_i == k, col, route)
    return route, carry + jnp.sum(sel, axis=0, keepdims=True)


def _post_tile(attn, conv_ref, x_ref, wmix_ref, g1_ref, b1_ref, wqk_ref, wvo_ref, g2_ref, b2_ref,
               wrhl_ref, br_ref, tri_ref, h2t_ref, rt_ref, cnt_ref, carry_ref):
    mix = _dot(attn, wmix_ref[0:ATTN_WIDTH, :]) + _dot(conv_ref[...], wmix_ref[ATTN_WIDTH:, :])
    h1 = _layer_norm(ALPHA * x_ref[...] + mix, g1_ref[...], b1_ref[...])
    scores = _dot(_bf16(h1), wqk_ref[...])
    probs = []
    for h in range(MEM_HEADS):
        s = scores[:, h * N_MEM:(h + 1) * N_MEM]
        p = jnp.exp(s - jnp.max(s, axis=1, keepdims=True))
        probs.append(_bf16(p / jnp.sum(p, axis=1, keepdims=True)))
    mem_out = _dot(jnp.concatenate(probs, axis=1), wvo_ref[...])
    h2 = _layer_norm(ALPHA * h1 + mem_out, g2_ref[...], b2_ref[...])
    _store_token_tiles(h2t_ref, h2)
    route, carry = _route(h2, wrhl_ref, br_ref, tri_ref, carry_ref[0:1, :])
    rt_ref[...] = route
    carry_ref[...] = jnp.broadcast_to(carry, carry_ref.shape)
    cnt_ref[...] = jnp.broadcast_to(carry, cnt_ref.shape)


def _swa_post_kernel(sinks_ref, q_ref, kc_ref, kp_ref, vc_ref, vp_ref,
                     conv_ref, x_ref, wmix_ref, g1_ref, b1_ref, wqk_ref, wvo_ref, g2_ref, b2_ref,
                     wrhl_ref, br_ref, tri_ref, h2s_ref, rts_ref, cnts_ref,
                     h2t_ref, rt_ref, cnt_ref, carry_ref, attn_s):
    t = pl.program_id(0)
    steps = N_PROMPT // TM_POST
    par = t % 2

    def swa(slot):
        def store(rows, cols, val):
            attn_s[slot, rows, cols] = val
        _swa_tile(t, sinks_ref, q_ref, kc_ref, kp_ref, vc_ref, vp_ref, store)

    def post(slot):
        _post_tile(attn_s[slot], conv_ref, x_ref, wmix_ref, g1_ref, b1_ref, wqk_ref, wvo_ref, g2_ref, b2_ref,
                   wrhl_ref, br_ref, tri_ref, h2t_ref, rt_ref, cnt_ref, carry_ref)

    @pl.when(t == 0)
    def _():
        carry_ref[...] = cnts_ref[...]
        swa(0)

    @pl.when((t >= 1) & (t < steps))
    def _():
        swa(par)
        post(1 - par)

    @pl.when(t == steps)
    def _():
        post(1 - par)

    @pl.when(t == steps + 1)
    def _():
        h2t_ref[...] = h2s_ref[...]
        rt_ref[...] = rts_ref[...]


def _swa_post(sinks, q, kx, vx, conv, x, wmix_b, g1, b1, wqk, wvo, g2, b2, wrhl, br, tri, h2t_s, rt_s, cnt_s):
    n = x.shape[0]
    tm = TM_POST
    assert tm == SWA_QB * WINDOW
    steps = n // tm
    cur = lambda w: pl.BlockSpec((tm, w), lambda i: (jnp.minimum(i, steps - 1), 0))
    prev = lambda w: pl.BlockSpec((WINDOW, w), lambda i: (jnp.clip(SWA_QB * i - 1, 0, n // WINDOW - 1), 0))
    lag = lambda w: pl.BlockSpec((tm, w), lambda i: (jnp.clip(i - 1, 0, steps - 1), 0))
    full = lambda a: pl.BlockSpec(a.shape, lambda i: (0,) * a.ndim)
    weights = (wmix_b, g1, b1, wqk, wvo, g2, b2, wrhl, br, tri, h2t_s, rt_s, cnt_s)
    n_out = n + h2t_s.shape[0] // ROW_CHUNKS
    out_idx = lambda i: (jnp.where(i > steps, steps, jnp.clip(i - 1, 0, steps - 1)), 0)
    return pl.pallas_call(
        _swa_post_kernel,
        grid=(steps + 2,),
        in_specs=([pl.BlockSpec(memory_space=pltpu.SMEM), cur(ATTN_WIDTH),
                   cur(4 * LANES), prev(4 * LANES), cur(4 * LANES), prev(4 * LANES),
                   lag(CONV_CH), lag(D_MODEL)] + [full(a) for a in weights]),
        out_specs=[pl.BlockSpec((tm * ROW_CHUNKS, LANES), out_idx),
                   pl.BlockSpec((tm, LANES), out_idx),
                   pl.BlockSpec((SUBLANES, LANES), lambda i: (0, 0))],
        out_shape=[jax.ShapeDtypeStruct((n_out * ROW_CHUNKS, LANES), jnp.float32),
                   jax.ShapeDtypeStruct((n_out, LANES), jnp.float32),
                   jax.ShapeDtypeStruct((SUBLANES, LANES), jnp.float32)],
        scratch_shapes=[pltpu.VMEM((SUBLANES, LANES), jnp.float32),
                        pltpu.VMEM((2, tm, ATTN_WIDTH), jnp.bfloat16)],
        compiler_params=_params(("arbitrary",)),
        name="swa_post_prompt",
    )(sinks, q, kx, kx, vx, vx, conv, x, *weights)


def _post_a_sample_kernel(attn_ref, conv_ref, x_ref, wmix_ref, g1_ref, b1_ref, wq_ref, h1_ref, qm_ref):
    h1 = _mix_ln1(attn_ref, conv_ref, x_ref, wmix_ref, g1_ref, b1_ref)
    h1_ref[...] = h1
    qm_ref[...] = _mem_q(h1, wq_ref).astype(jnp.float32)


def _post_a_sample(attn, conv, x, wmix_b, g1, b1, wq_b):
    n = x.shape[0]
    args = (attn, conv, x, wmix_b, g1, b1, wq_b)
    full = lambda a: pl.BlockSpec(a.shape, lambda i: (0,) * a.ndim)
    blk = pl.BlockSpec((n, D_MODEL), lambda i: (0, 0))
    return pl.pallas_call(
        _post_a_sample_kernel,
        grid=(1,),
        in_specs=[full(a) for a in args],
        out_specs=[blk, blk],
        out_shape=[jax.ShapeDtypeStruct((n, D_MODEL), jnp.float32),
                   jax.ShapeDtypeStruct((n, D_MODEL), jnp.float32)],
        compiler_params=_params(("arbitrary",)),
        name="post_a_sample",
    )(*args)


MEM_ROWS = MEM_HEADS * DEC_SEQ


def _mem_attn_sample_body(q_ref, mk_ref, mv_ref, o_ref):
    nk = N_MEM * MEM_HEADS
    nb = SAMPLE_BB
    rows = nb * MEM_ROWS
    row_h = (lax.broadcasted_iota(jnp.int32, (rows, nk), 0) % MEM_ROWS) // DEC_SEQ
    key_h = lax.broadcasted_iota(jnp.int32, (rows, nk), 1) % MEM_HEADS
    s = jnp.concatenate(
        [_dot_nt(_bf16(q_ref[b]), _bf16(mk_ref[b].reshape(nk, MEM_HEAD_DIM))) for b in range(nb)], axis=0)
    s = jnp.where(row_h == key_h, s, -jnp.inf)
    p = jnp.exp(s - jnp.max(s, axis=1, keepdims=True))
    rden = 1.0 / jnp.sum(p, axis=1, keepdims=True)
    p = _bf16(p)
    for b in range(nb):
        r = slice(b * MEM_ROWS, (b + 1) * MEM_ROWS)
        o_ref[b] = _dot(p[r], _bf16(mv_ref[b].reshape(nk, MEM_HEAD_DIM))) * rden[r]


def _proj_mem_kernel(x_ref, w_ref, tab_ref, cw_ref, mq_ref, mk_ref, mv_ref,
                     q_ref, kx_ref, vx_ref, conv_ref, ktail_ref, vtail_ref, utail_ref, mo_ref, carry_ref):
    _mem_attn_sample_body(mq_ref, mk_ref, mv_ref, mo_ref)
    _proj_prompt_body(x_ref, w_ref, tab_ref, cw_ref,
                      q_ref, kx_ref, vx_ref, conv_ref, ktail_ref, vtail_ref, utail_ref, carry_ref)


def _proj_prompt_mem_sample(x, w_in_b, tab, conv_w, qm, mk, mv):
    n = x.shape[0]
    nb = mk.shape[0]
    bb = SAMPLE_BB
    steps = nb // bb
    tm = n // steps
    row = lambda w: pl.BlockSpec((tm, w), lambda i: (i, 0))
    full = lambda a: pl.BlockSpec(a.shape, lambda i: (0,) * a.ndim)
    const = lambda r, w: pl.BlockSpec((r, w), lambda i: (0, 0))
    mq = qm.reshape(nb, DEC_SEQ, MEM_HEADS, MEM_HEAD_DIM).transpose(0, 2, 1, 3).reshape(nb, MEM_ROWS, MEM_HEAD_DIM)
    mrows = pl.BlockSpec((bb, MEM_ROWS, MEM_HEAD_DIM), lambda i: (i, 0, 0))
    kv = pl.BlockSpec((bb, N_MEM, MEM_HEADS, MEM_HEAD_DIM), lambda i: (i, 0, 0, 0))
    outs = pl.pallas_call(
        _proj_mem_kernel,
        grid=(steps,),
        in_specs=[row(D_MODEL), full(w_in_b), pl.BlockSpec((ROPE_ROWS, tm), lambda i: (0, i)), full(conv_w),
                  mrows, kv, kv],
        out_specs=[row(ATTN_WIDTH), row(4 * LANES), row(4 * LANES), row(CONV_CH),
                   const(WINDOW, KV_WIDTH), const(WINDOW, KV_WIDTH), const(SUBLANES, CONV_CH), mrows],
        out_shape=[jax.ShapeDtypeStruct((n, ATTN_WIDTH), jnp.bfloat16),
                   jax.ShapeDtypeStruct((n, 4 * LANES), jnp.bfloat16),
                   jax.ShapeDtypeStruct((n, 4 * LANES), jnp.bfloat16),
                   jax.ShapeDtypeStruct((n, CONV_CH), jnp.bfloat16),
                   jax.ShapeDtypeStruct((WINDOW, KV_WIDTH), jnp.float32),
                   jax.ShapeDtypeStruct((WINDOW, KV_WIDTH), jnp.float32),
                   jax.ShapeDtypeStruct((SUBLANES, CONV_CH), jnp.float32),
                   jax.ShapeDtypeStruct((nb, MEM_ROWS, MEM_HEAD_DIM), jnp.float32)],
        scratch_shapes=[pltpu.VMEM((SUBLANES, CONV_CH), jnp.float32)],
        compiler_params=_params(("arbitrary",)),
        name="proj_prompt_mem_sample",
    )(x, w_in_b, tab, conv_w, mq, mk, mv)
    o = outs[7].reshape(nb, MEM_HEADS, DEC_SEQ, MEM_HEAD_DIM).transpose(0, 2, 1, 3).reshape(nb * DEC_SEQ, D_MODEL)
    return outs[:7], o


def _post_b_sample_kernel(o_ref, h1_ref, wo_ref, g2_ref, b2_ref, wrhl_ref, br_ref, tri_ref,
                          h2t_ref, rt_ref, cnt_ref):
    h2 = _layer_norm(ALPHA * h1_ref[...] + _dot(_bf16(o_ref[...]), wo_ref[...]), g2_ref[...], b2_ref[...])
    _store_token_tiles(h2t_ref, h2)
    route, carry = _route(h2, wrhl_ref, br_ref, tri_ref, jnp.zeros((1, LANES), jnp.float32))
    rt_ref[...] = route
    cnt_ref[...] = jnp.broadcast_to(carry, cnt_ref.shape)


def _post_b_sample(o, h1, wo_b, g2, b2, wrhl, br, tri):
    n = h1.shape[0]
    args = (o, h1, wo_b, g2, b2, wrhl, br, tri)
    full = lambda a: pl.BlockSpec(a.shape, lambda i: (0,) * a.ndim)
    return pl.pallas_call(
        _post_b_sample_kernel,
        grid=(1,),
        in_specs=[full(a) for a in args],
        out_specs=[pl.BlockSpec((n * ROW_CHUNKS, LANES), lambda i: (0, 0)),
                   pl.BlockSpec((n, LANES), lambda i: (0, 0)),
                   pl.BlockSpec((SUBLANES, LANES), lambda i: (0, 0))],
        out_shape=[jax.ShapeDtypeStruct((n * ROW_CHUNKS, LANES), jnp.float32),
                   jax.ShapeDtypeStruct((n, LANES), jnp.float32),
                   jax.ShapeDtypeStruct((SUBLANES, LANES), jnp.float32)],
        compiler_params=_params(("arbitrary",)),
        name="post_b_sample",
    )(*args)


def _row_gather_copy(src_hbm, idx, dst, dst_row, sem):
    s0 = pl.multiple_of(idx * ROW_CHUNKS, ROW_CHUNKS)
    d0 = pl.multiple_of(dst_row * ROW_CHUNKS, ROW_CHUNKS)
    return pltpu.make_async_copy(src_hbm.at[pl.ds(s0, ROW_CHUNKS), :], dst.at[pl.ds(d0, ROW_CHUNKS), :], sem)


def _dispatch_kernel(pos_ref, h2t_ref, xs_hbm, sem):
    def body(r, c):
        src = h2t_ref.at[pl.ds(pl.multiple_of(r * ROW_CHUNKS, ROW_CHUNKS), ROW_CHUNKS), :]
        for k in range(2):
            d0 = pl.multiple_of(pos_ref[0, 0, k * TM_COMB + r] * ROW_CHUNKS, ROW_CHUNKS)
            pltpu.make_async_copy(src, xs_hbm.at[pl.ds(d0, ROW_CHUNKS), :], sem.at[0]).start(priority=k)
        return c
    lax.fori_loop(0, TM_COMB, body, 0, unroll=8)
    for _ in range(2):
        pltpu.make_async_copy(h2t_ref, xs_hbm.at[pl.ds(0, TM_COMB * ROW_CHUNKS), :], sem.at[0]).wait()


def _dispatch(pos3, h2t):
    nt = N_ALL // TM_COMB
    return pl.pallas_call(
        _dispatch_kernel,
        grid=(nt,),
        in_specs=[pl.BlockSpec((1, 1, 2 * TM_COMB), lambda i: (i, 0, 0), memory_space=pltpu.SMEM),
                  pl.BlockSpec((TM_COMB * ROW_CHUNKS, LANES), lambda i: (i, 0))],
        out_specs=pl.BlockSpec(memory_space=pl.ANY),
        out_shape=jax.ShapeDtypeStruct((N_ASSIGN * ROW_CHUNKS, LANES), jnp.float32),
        scratch_shapes=[pltpu.SemaphoreType.DMA((1,))],
        compiler_params=_params(("arbitrary",)),
        name="moe_dispatch",
    )(pos3, h2t)


def _moe_ffn_kernel(it_ref, ie_ref, lo_ref, hi_ref, x_ref, wg_ref, wu_ref, wd_ref, y_ref, wgb, wub, wdb, cur_e):
    i = pl.program_id(0)
    lo = lo_ref[i]
    hi = hi_ref[i]
    e = ie_ref[i]

    @pl.when(i == 0)
    def _():
        cur_e[0] = -1

    @pl.when((hi > lo) & (cur_e[0] != e))
    def _():
        wgb[...] = _bf16(wg_ref[0])
        wub[...] = _bf16(wu_ref[0])
        wdb[...] = _bf16(wd_ref[0])
        cur_e[0] = e

    def ffn():
        x = _bf16(_load_token_tiles(x_ref, 0, TM_MOE))
        hg = _dot(x, wgb[...])
        hu = _dot(x, wub[...])
        h = hg / (1.0 + jnp.exp(-hg)) * hu
        return _dot(_bf16(h), wdb[...])

    def rows_mask():
        row = lax.broadcasted_iota(jnp.int32, (TM_MOE, LANES), 0)
        return (row >= lo) & (row < hi)

    @pl.when((hi > lo) & (lo == 0))
    def _():
        y = ffn()
        mask = rows_mask()
        for c in range(ROW_CHUNKS):
            y_ref[pl.ds(c, TM_MOE, stride=ROW_CHUNKS), :] = jnp.where(mask, y[:, c * LANES:(c + 1) * LANES], 0.0)

    @pl.when((hi > lo) & (lo > 0))
    def _():
        y = ffn()
        mask = rows_mask()
        for c in range(ROW_CHUNKS):
            sl = pl.ds(c, TM_MOE, stride=ROW_CHUNKS)
            y_ref[sl, :] = jnp.where(mask, y[:, c * LANES:(c + 1) * LANES], y_ref[sl, :])


def _moe_ffn(item_tile, item_expert, item_lo, item_hi, x_sorted, w_gate, w_up, w_down):
    wspec = lambda shp: pl.BlockSpec((1,) + shp, lambda i, it, ie, lo, hi: (ie[i], 0, 0))
    tile = pl.BlockSpec((TM_MOE * ROW_CHUNKS, LANES), lambda i, it, ie, lo, hi: (it[i], 0))
    grid_spec = pltpu.PrefetchScalarGridSpec(
        num_scalar_prefetch=4,
        grid=(MOE_ITEMS,),
        in_specs=[tile, wspec((D_MODEL, EXPERT_FF)), wspec((D_MODEL, EXPERT_FF)), wspec((EXPERT_FF, D_MODEL))],
        out_specs=tile,
        scratch_shapes=[pltpu.VMEM((D_MODEL, EXPERT_FF), jnp.bfloat16),
                        pltpu.VMEM((D_MODEL, EXPERT_FF), jnp.bfloat16),
                        pltpu.VMEM((EXPERT_FF, D_MODEL), jnp.bfloat16),
                        pltpu.SMEM((1,), jnp.int32)],
    )
    return pl.pallas_call(
        _moe_ffn_kernel,
        grid_spec=grid_spec,
        out_shape=jax.ShapeDtypeStruct((N_ASSIGN * ROW_CHUNKS, LANES), jnp.float32),
        compiler_params=_params(("arbitrary",)),
        name="moe_ffn",
    )(item_tile, item_expert, item_lo, item_hi, x_sorted, w_gate, w_up, w_down)


def _combine_kernel(nt, pos_cur_ref, pos_nxt_ref, yt_hbm, h2t_ref, rt_ref, g3_ref, b3_ref, o_ref, abuf, sem):
    t = pl.program_id(0)
    slot = t % 2
    rows = 2 * TM_COMB

    def issue(pos_ref, s):
        def body(j, c):
            for k in range(2):
                r = 2 * j + k
                _row_gather_copy(yt_hbm, pos_ref[0, 0, r], abuf, s * rows + r, sem.at[s]).start(priority=k)
            return c
        lax.fori_loop(0, rows // 2, body, 0, unroll=16)

    @pl.when(t == 0)
    def _():
        issue(pos_cur_ref, 0)

    @pl.when(t + 1 < nt)
    def _():
        issue(pos_nxt_ref, 1 - slot)

    base = pl.multiple_of(slot * (rows * ROW_CHUNKS), rows * ROW_CHUNKS)
    pltpu.make_async_copy(yt_hbm.at[pl.ds(0, rows * ROW_CHUNKS), :],
                          abuf.at[pl.ds(base, rows * ROW_CHUNKS), :], sem.at[slot]).wait()
    ya = _load_token_tiles(abuf, base, TM_COMB)
    yb = _load_token_tiles(abuf, base + TM_COMB * ROW_CHUNKS, TM_COMB)
    rt = rt_ref[...]
    ff = rt[:, 2:3] * ya + rt[:, 3:4] * yb
    h2 = _load_token_tiles(h2t_ref, 0, TM_COMB)
    o_ref[...] = _layer_norm(ALPHA * h2 + ff, g3_ref[...], b3_ref[...])


def _combine(pos3, yt, h2t, rt, g3, b3, tile0, n_tiles):
    last = tile0 + n_tiles - 1
    smem_pos = lambda f: pl.BlockSpec((1, 1, 2 * TM_COMB), f, memory_space=pltpu.SMEM)
    full = lambda a: pl.BlockSpec(a.shape, lambda i: (0,) * a.ndim)
    return pl.pallas_call(
        functools.partial(_combine_kernel, n_tiles),
        grid=(n_tiles,),
        in_specs=[smem_pos(lambda i: (tile0 + i, 0, 0)),
                  smem_pos(lambda i: (jnp.minimum(tile0 + i + 1, last), 0, 0)),
                  pl.BlockSpec(memory_space=pl.ANY),
                  pl.BlockSpec((TM_COMB * ROW_CHUNKS, LANES), lambda i: (tile0 + i, 0)),
                  pl.BlockSpec((TM_COMB, LANES), lambda i: (tile0 + i, 0)),
                  full(g3), full(b3)],
        out_specs=pl.BlockSpec((TM_COMB, D_MODEL), lambda i: (i, 0)),
        out_shape=jax.ShapeDtypeStruct((n_tiles * TM_COMB, D_MODEL), jnp.float32),
        scratch_shapes=[pltpu.VMEM((2 * 2 * TM_COMB * ROW_CHUNKS, LANES), jnp.float32),
                        pltpu.SemaphoreType.DMA((2,))],
        compiler_params=_params(("arbitrary",)),
        name="moe_combine",
    )(pos3, pos3, yt, h2t, rt, g3, b3)


POS_TILES = 11


def _positions_kernel(rt_ref, starts_ref, pos_ref):
    expert = lax.broadcasted_iota(jnp.int32, (N_EXPERTS, TM_COMB), 0).astype(jnp.float32)
    starts = jnp.concatenate([starts_ref[...]] * (TM_COMB // LANES), axis=1)
    for j in range(POS_TILES):
        cols = rt_ref[j * TM_COMB:(j + 1) * TM_COMB, :].T
        out = []
        for k in range(2):
            seg = jnp.sum(jnp.where(expert == cols[k:k + 1, :], starts, 0.0), axis=0, keepdims=True)
            out.append(seg + cols[4 + k:5 + k, :])
        pos_ref[j] = jnp.concatenate(out, axis=1).astype(jnp.int32)


def _positions(rt, starts_rep):
    nt = N_ALL // TM_COMB
    assert nt % POS_TILES == 0
    return pl.pallas_call(
        _positions_kernel,
        grid=(nt // POS_TILES,),
        in_specs=[pl.BlockSpec((POS_TILES * TM_COMB, LANES), lambda i: (i, 0)),
                  pl.BlockSpec((N_EXPERTS, LANES), lambda i: (0, 0))],
        out_specs=pl.BlockSpec((POS_TILES, 1, 2 * TM_COMB), lambda i: (i, 0, 0)),
        out_shape=jax.ShapeDtypeStruct((nt, 1, 2 * TM_COMB), jnp.int32),
        compiler_params=_params(("arbitrary",)),
        name="moe_positions",
    )(rt, starts_rep)


def _routing_plan(rt, cnt):
    i32 = jnp.int32
    counts_f = cnt[0, N_GROUPS:N_GROUPS + N_EXPERTS]
    starts_f = jnp.cumsum(counts_f) - counts_f
    pos3 = _positions(rt, jnp.broadcast_to(starts_f[:, None], (N_EXPERTS, LANES)))
    starts = starts_f.astype(i32)
    tiles = jnp.arange(MOE_TILES, dtype=i32) * TM_MOE
    rank_t = jnp.arange(MOE_TILES, dtype=i32) + jnp.sum((starts[None, :] < tiles[:, None]).astype(i32), axis=1)
    rank_s = jnp.arange(N_EXPERTS, dtype=i32) + jnp.sum((tiles[None, :] <= starts[:, None]).astype(i32), axis=1)
    vals = jnp.concatenate([tiles, starts])
    ranks = jnp.concatenate([rank_t, rank_s])
    slot = jnp.arange(MOE_ITEMS, dtype=i32)
    lo = jnp.sum(jnp.where(ranks[None, :] == slot[:, None], vals[None, :], 0), axis=1)
    hi = jnp.concatenate([lo[1:], jnp.full((1,), N_ASSIGN, i32)])
    item_tile = jnp.minimum(lo // TM_MOE, MOE_TILES - 1)
    item_expert = jnp.clip(jnp.sum((starts[None, :] <= lo[:, None]).astype(i32), axis=1) - 1, 0, N_EXPERTS - 1)
    base = item_tile * TM_MOE
    return item_tile, item_expert, lo - base, hi - base, pos3


def kernel(x_prompt, x_sample, mem_prompt, cache_swa_k, cache_swa_v, cache_conv, cache_mem_k, cache_mem_v,
           w_in, sinks, conv_w, w_mix_out, ln1_g, ln1_b, w_q_mem, w_k_mem, w_v_mem, w_o_mem, ln2_g, ln2_b,
           w_router_group, b_router_group, w_router_expert, b_router_expert, w_gate, w_up, w_down,
           ln3_g, ln3_b):
    f32 = jnp.float32
    row = lambda a: a.reshape(1, -1).astype(f32)
    w_in_b, wmix_b, wq_b, wk_b, wv_b, wo_b = (_bf16(w) for w in (w_in, w_mix_out, w_q_mem, w_k_mem, w_v_mem, w_o_mem))
    g1, b1, g2, b2, g3, b3 = (row(a) for a in (ln1_g, ln1_b, ln2_g, ln2_b, ln3_g, ln3_b))
    pad = LANES - N_GROUPS - N_EXPERTS
    wr = jnp.concatenate([w_router_group, w_router_expert, jnp.zeros((D_MODEL, pad), f32)], axis=1)
    wrh = _bf16(wr)
    wrhl = jnp.concatenate([wrh, _bf16(wr - wrh.astype(f32))], axis=1)
    br = jnp.concatenate([b_router_group, b_router_expert, jnp.zeros((pad,), f32)]).reshape(1, LANES)

    xs = x_sample.reshape(N_SAMPLE, D_MODEL)
    tab_s = jnp.tile(_rope_table(PAST_LEN + jnp.arange(DEC_SEQ)), (1, DEC_BATCH))
    c0 = jnp.repeat(cache_conv[:, 0], DEC_SEQ, axis=0)
    c1 = jnp.repeat(cache_conv[:, 1], DEC_SEQ, axis=0)
    q_s, k_s, v_s, conv_s, u_s = _proj_sample(xs, w_in_b, tab_s, conv_w, c0, c1)
    attn_s, swa_k_s, swa_v_s = _swa_sample(sinks, q_s, k_s, v_s, cache_swa_k, cache_swa_v)
    h1_s, qm_s = _post_a_sample(attn_s, conv_s, xs, wmix_b, g1, b1, wq_b)
    xp = x_prompt.reshape(N_PROMPT, D_MODEL)
    tab_p = _rope_table(jnp.arange(N_PROMPT))
    (q_p, kx_p, vx_p, conv_p, k_tail, v_tail, u_tail), o_s = _proj_prompt_mem_sample(
        xp, w_in_b, tab_p, conv_w, qm_s, cache_mem_k, cache_mem_v)
    tri = _bf16(jnp.tril(jnp.ones((TM_POST, TM_POST), f32), -1))
    h2t_s, rt_s, cnt_s = _post_b_sample(o_s, h1_s, wo_b, g2, b2, wrhl, br, tri)

    mk, mv, wqk, wvo = _mem_kv(mem_prompt.reshape(N_MEM, D_MODEL), wk_b, wv_b, wq_b, wo_b)
    h2t, rt, cnt = _swa_post(sinks, q_p, kx_p, vx_p, conv_p, xp, wmix_b, g1, b1, wqk, wvo, g2, b2,
                             wrhl, br, tri, h2t_s, rt_s, cnt_s)

    item_tile, item_expert, item_lo, item_hi, pos3 = _routing_plan(rt, cnt)
    x_sorted = _dispatch(pos3, h2t)
    yt = _moe_ffn(item_tile, item_expert, item_lo, item_hi, x_sorted, w_gate, w_up, w_down)
    y_p = _combine(pos3, yt, h2t, rt, g3, b3, 0, N_PROMPT // TM_COMB)
    y_s = _combine(pos3, yt, h2t, rt, g3, b3, N_PROMPT // TM_COMB, N_SAMPLE // TM_COMB)

    return (y_p.reshape(1, SEQ, D_MODEL),
            y_s.reshape(DEC_BATCH, DEC_SEQ, D_MODEL),
            k_tail.reshape(1, WINDOW, N_KV_HEADS, HEAD_DIM),
            v_tail.reshape(1, WINDOW, N_KV_HEADS, HEAD_DIM),
            u_tail[SUBLANES - (CONV_K - 1):].reshape(1, CONV_K - 1, CONV_CH),
            mk.reshape(1, N_MEM, MEM_HEADS, MEM_HEAD_DIM),
            mv.reshape(1, N_MEM, MEM_HEADS, MEM_HEAD_DIM),
            swa_k_s.reshape(DEC_BATCH, WINDOW, N_KV_HEADS, HEAD_DIM),
            swa_v_s.reshape(DEC_BATCH, WINDOW, N_KV_HEADS, HEAD_DIM),
            u_s.reshape(DEC_BATCH, DEC_SEQ, CONV_CH)[:, DEC_SEQ - (CONV_K - 1):])
```

```python
import functools

import jax
import jax.numpy as jnp
from jax import lax
from jax.experimental import pallas as pl
from jax.experimental.pallas import tpu as pltpu

D_MODEL = 1024
SEQ = 16384
DEC_BATCH = 128
DEC_SEQ = 4
PAST_LEN = 16384
ATTN_WIDTH = 512
CONV_CH = 512
HEAD_DIM = 64
N_HEADS = 8
N_KV_HEADS = 2
KV_WIDTH = 128
WINDOW = 128
ROPE_THETA = 500000.0
ROPE_DIM = 16
CONV_K = 3
Q_END = ATTN_WIDTH
K_END = Q_END + KV_WIDTH
V_END = K_END + KV_WIDTH
B_END = V_END + CONV_CH
C_END = B_END + CONV_CH
IN_WIDTH = C_END + CONV_CH
N_MEM = 256
MEM_HEADS = 4
MEM_HEAD_DIM = 256
N_GROUPS = 4
EXPERTS_PER_GROUP = 8
N_EXPERTS = 32
EXPERT_FF = 256
ALPHA = 2.0 ** 0.25
LN_EPS = 1e-5

LANES = 128
SUBLANES = 8
ROW_CHUNKS = D_MODEL // LANES
VMEM_LIMIT = 56 * 1024 * 1024

N_PROMPT = SEQ
N_SAMPLE = DEC_BATCH * DEC_SEQ
N_ALL = N_PROMPT + N_SAMPLE
TM_POST = 512
TM_MOE = 512
TM_COMB = 512
MOE_BLOCK = 256
N_ASSIGN = 2 * N_ALL
MOE_TILES = N_ASSIGN // TM_MOE
MOE_ITEMS = MOE_TILES + N_EXPERTS
SAMPLE_BB = 4
SWA_BB = 16

assert ROW_CHUNKS == SUBLANES
assert N_SAMPLE == TM_POST
assert N_ASSIGN % TM_MOE == 0 and N_ALL % TM_COMB == 0


def _params(sem, vmem=VMEM_LIMIT):
    return pltpu.CompilerParams(dimension_semantics=sem, vmem_limit_bytes=vmem)


def _bf16(x):
    return x.astype(jnp.bfloat16)


def _dot(a, b):
    return jnp.dot(a, b, preferred_element_type=jnp.float32)


def _dot_nt(a, b):
    return lax.dot_general(a, b, (((1,), (1,)), ((), ())), preferred_element_type=jnp.float32)


def _layer_norm(x, g, b):
    mu = jnp.mean(x, axis=-1, keepdims=True)
    xc = x - mu
    var = jnp.mean(xc * xc, axis=-1, keepdims=True)
    return xc * lax.rsqrt(var + LN_EPS) * g + b


def _rope(x, cos_t, sin_t):
    lane = lax.broadcasted_iota(jnp.int32, x.shape, 1) % HEAD_DIM
    half = ROPE_DIM // 2
    partner = jnp.where(lane < half, pltpu.roll(x, LANES - half, axis=1), pltpu.roll(x, half, axis=1))
    return x * cos_t + partner * sin_t


def _head_slabs(x):
    lane = lax.broadcasted_iota(jnp.int32, x.shape, 1)
    lo = lane < HEAD_DIM
    sw = pltpu.roll(x, HEAD_DIM, axis=1)
    zero = jnp.zeros_like(x)
    slabs = [jnp.where(lo, x, zero), jnp.where(lo, zero, sw), jnp.where(lo, sw, zero), jnp.where(lo, zero, x)]
    return _bf16(jnp.concatenate(slabs, axis=1))


def _store_token_tiles(ref, val):
    rows = val.shape[0]
    for c in range(ROW_CHUNKS):
        ref[pl.ds(c, rows, stride=ROW_CHUNKS), :] = val[:, c * LANES:(c + 1) * LANES]


def _load_token_tiles(ref, base, rows):
    return jnp.concatenate(
        [ref[pl.ds(base + c, rows, stride=ROW_CHUNKS), :] for c in range(ROW_CHUNKS)], axis=1)


ROPE_ONE = 3 * (ROPE_DIM // 2)
ROPE_ROWS = 32


def _rope_patterns(tab):
    half = ROPE_DIM // 2
    m = lax.broadcasted_iota(jnp.int32, tab.shape, 1) % HEAD_DIM
    idx_c = jnp.where(m < ROPE_DIM, m % half, ROPE_ONE)
    idx_s = jnp.where(m < half, 2 * half + m, jnp.where(m < ROPE_DIM, m, ROPE_ONE + 1))
    return jnp.take_along_axis(tab, idx_c, axis=1), jnp.take_along_axis(tab, idx_s, axis=1)


def _proj_common(x_ref, w_ref, tab_ref):
    xb = _bf16(x_ref[...])
    tab = tab_ref[...]
    pad = jnp.zeros((LANES - tab.shape[0], tab.shape[1]), jnp.float32)
    cos_t, sin_t = _rope_patterns(jnp.concatenate([tab, pad], axis=0).T)
    q = _dot(xb, w_ref[:, 0:Q_END])
    q_rot = jnp.concatenate(
        [_rope(q[:, p * LANES:(p + 1) * LANES], cos_t, sin_t) for p in range(ATTN_WIDTH // LANES)], axis=1)
    q_out = _bf16(q_rot * (HEAD_DIM ** -0.5))
    kv = _dot(xb, w_ref[:, Q_END:V_END])
    k = _rope(kv[:, 0:KV_WIDTH], cos_t, sin_t)
    v = kv[:, KV_WIDTH:]
    bg = _dot(xb, w_ref[:, V_END:B_END])
    u = _dot(xb, w_ref[:, B_END:C_END]) * _dot(xb, w_ref[:, C_END:IN_WIDTH])
    return q_out, k, v, bg, u


def _conv3(bg, u, u1, u2, cw_ref):
    cw = cw_ref[...]
    return bg * (cw[0:1, :] * u2 + cw[1:2, :] * u1 + cw[2:3, :] * u)


def _proj_prompt_body(x_ref, w_ref, tab_ref, cw_ref,
                      q_ref, kx_ref, vx_ref, conv_ref, ktail_ref, vtail_ref, utail_ref, carry_ref):
    @pl.when(pl.program_id(0) == 0)
    def _():
        carry_ref[...] = jnp.zeros_like(carry_ref)

    q_out, k, v, bg, u = _proj_common(x_ref, w_ref, tab_ref)
    tm = u.shape[0]
    ext = jnp.concatenate([carry_ref[...], u], axis=0)
    u1 = pltpu.roll(ext, 1, axis=0)[SUBLANES:SUBLANES + tm]
    u2 = pltpu.roll(ext, 2, axis=0)[SUBLANES:SUBLANES + tm]
    q_ref[...] = q_out
    kx_ref[...] = _head_slabs(k)
    vx_ref[...] = _head_slabs(v)
    conv_ref[...] = _bf16(_conv3(bg, u, u1, u2, cw_ref))
    ktail_ref[...] = k[tm - WINDOW:tm]
    vtail_ref[...] = v[tm - WINDOW:tm]
    utail_ref[...] = u[tm - SUBLANES:tm]
    carry_ref[...] = u[tm - SUBLANES:tm]


def _proj_sample_kernel(x_ref, w_ref, tab_ref, cw_ref, c0_ref, c1_ref,
                        q_ref, k_ref, v_ref, conv_ref, u_ref):
    q_out, k, v, bg, u = _proj_common(x_ref, w_ref, tab_ref)
    t = lax.broadcasted_iota(jnp.int32, u.shape, 0) % DEC_SEQ
    c0 = c0_ref[...]
    c1 = c1_ref[...]
    u1 = jnp.where(t >= 1, pltpu.roll(u, 1, axis=0), c1)
    u2 = jnp.where(t >= 2, pltpu.roll(u, 2, axis=0), jnp.where(t == 1, c1, c0))
    q_ref[...] = q_out.astype(jnp.float32)
    k_ref[...] = k
    v_ref[...] = v
    conv_ref[...] = _bf16(_conv3(bg, u, u1, u2, cw_ref))
    u_ref[...] = u


def _rope_table(pos):
    half = ROPE_DIM // 2
    inv = ROPE_THETA ** (-jnp.arange(0, ROPE_DIM, 2, dtype=jnp.float32) / ROPE_DIM)
    ang = pos.astype(jnp.float32)[None, :] * inv[:, None]
    cos, sin = jnp.cos(ang), jnp.sin(ang)
    n = pos.shape[0]
    assert ROPE_ONE == 3 * half
    return jnp.concatenate([cos, sin, -sin, jnp.ones((1, n), jnp.float32),
                            jnp.zeros((ROPE_ROWS - ROPE_ONE - 1, n), jnp.float32)], axis=0)


def _proj_sample(x, w_in_b, tab, conv_w, c0, c1):
    n = x.shape[0]
    full = lambda a: pl.BlockSpec(a.shape, lambda i: (0,) * a.ndim)
    out = lambda w, dt: jax.ShapeDtypeStruct((n, w), dt)
    blk = lambda w: pl.BlockSpec((n, w), lambda i: (0, 0))
    return pl.pallas_call(
        _proj_sample_kernel,
        grid=(1,),
        in_specs=[full(x), full(w_in_b), full(tab), full(conv_w), full(c0), full(c1)],
        out_specs=[blk(ATTN_WIDTH), blk(KV_WIDTH), blk(KV_WIDTH), blk(CONV_CH), blk(CONV_CH)],
        out_shape=[out(ATTN_WIDTH, jnp.float32), out(KV_WIDTH, jnp.float32), out(KV_WIDTH, jnp.float32),
                   out(CONV_CH, jnp.bfloat16), out(CONV_CH, jnp.float32)],
        compiler_params=_params(("arbitrary",)),
        name="proj_sample",
    )(x, w_in_b, tab, conv_w, c0, c1)


def _sink_softmax_pv(s, valid, sink, vx):
    s = jnp.where(valid, s, -jnp.inf)
    m = jnp.maximum(jnp.max(s, axis=1, keepdims=True), sink)
    p = jnp.exp(s - m)
    den = jnp.sum(p, axis=1, keepdims=True) + jnp.exp(sink - m)
    return _dot(_bf16(p), vx) / den


SWA_QB = 4


def _swa_tile(step, sinks_ref, q_ref, kc_ref, kp_ref, vc_ref, vp_ref, store):
    kall = jnp.concatenate([kp_ref[...], kc_ref[...]], axis=0)
    vall = jnp.concatenate([vp_ref[...], vc_ref[...]], axis=0)
    i = lax.broadcasted_iota(jnp.int32, (WINDOW, 2 * WINDOW), 0)
    j = lax.broadcasted_iota(jnp.int32, (WINDOW, 2 * WINDOW), 1)
    band = (j > i) & (j <= i + WINDOW)
    for sb in range(SWA_QB):
        rows = slice(sb * WINDOW, (sb + 1) * WINDOW)
        kcat = kall[sb * WINDOW:(sb + 2) * WINDOW]
        vcat = vall[sb * WINDOW:(sb + 2) * WINDOW]
        valid = band & ((step > 0) | (j >= WINDOW)) if sb == 0 else band
        for p in range(N_HEADS // 2):
            qs = q_ref[rows, p * LANES:(p + 1) * LANES]
            acc = None
            for e in range(2):
                hd = 2 * p + e
                slab = 2 * (hd // (N_HEADS // N_KV_HEADS)) + e
                kx = kcat[:, slab * LANES:(slab + 1) * LANES]
                vx = vcat[:, slab * LANES:(slab + 1) * LANES]
                o = _sink_softmax_pv(_dot_nt(qs, kx), valid, sinks_ref[hd], vx)
                acc = o if acc is None else acc + o
            store(rows, slice(p * LANES, (p + 1) * LANES), _bf16(acc))


SWA_ROWS = N_HEADS * DEC_SEQ
NEW_ROWS = 2 * SUBLANES


def _swa_sample_kernel(q_ref, sink_ref, kn_ref, vn_ref, kt_ref, vt_ref, o_ref, okt_ref, ovt_ref):
    nb = SWA_BB
    rows = nb * SWA_ROWS
    t = lax.broadcasted_iota(jnp.int32, (rows, WINDOW), 0) % DEC_SEQ
    valid_c = lax.broadcasted_iota(jnp.int32, (rows, WINDOW), 1) > t
    valid_n = (lax.broadcasted_iota(jnp.int32, (rows, NEW_ROWS), 1)
               <= lax.broadcasted_iota(jnp.int32, (rows, NEW_ROWS), 0) % DEC_SEQ)
    sink = jnp.concatenate([sink_ref[:, 0:1]] * nb, axis=0)
    qs = [_bf16(q_ref[b]) for b in range(nb)]
    s_c = jnp.concatenate([_dot(qs[b], _bf16(kt_ref[b])) for b in range(nb)], axis=0)
    s_n = jnp.concatenate([_dot_nt(qs[b], _bf16(kn_ref[b])) for b in range(nb)], axis=0)
    s_c = jnp.where(valid_c, s_c, -jnp.inf)
    s_n = jnp.where(valid_n, s_n, -jnp.inf)
    m = jnp.maximum(jnp.maximum(jnp.max(s_c, axis=1, keepdims=True), jnp.max(s_n, axis=1, keepdims=True)), sink)
    p_c = jnp.exp(s_c - m)
    p_n = jnp.exp(s_n - m)
    rden = 1.0 / (jnp.sum(p_c, axis=1, keepdims=True) + jnp.sum(p_n, axis=1, keepdims=True) + jnp.exp(sink - m))
    p_c, p_n = _bf16(p_c), _bf16(p_n)
    lane = lax.broadcasted_iota(jnp.int32, (KV_WIDTH, WINDOW), 1)
    shift = WINDOW - DEC_SEQ
    zrows = jnp.zeros((KV_WIDTH - NEW_ROWS, KV_WIDTH), jnp.float32)
    for b in range(nb):
        r = slice(b * SWA_ROWS, (b + 1) * SWA_ROWS)
        kt, vt = kt_ref[b], vt_ref[b]
        kn, vn = kn_ref[b], vn_ref[b]
        o_ref[b] = (_dot_nt(p_c[r], _bf16(vt)) + _dot(p_n[r], _bf16(vn))) * rden[r]
        for old, new, dst in ((kt, kn, okt_ref), (vt, vn, ovt_ref)):
            new_cols = pltpu.roll(jnp.concatenate([new, zrows], axis=0).T, shift, axis=1)
            dst[b] = jnp.where(lane >= shift, new_cols, pltpu.roll(old, shift, axis=1))


def _swa_sample(sinks, q, kn, vn, cache_k, cache_v):
    nb = cache_k.shape[0]
    bb = SWA_BB
    groups = N_HEADS // N_KV_HEADS
    qh = q.reshape(nb, DEC_SEQ, N_KV_HEADS, groups, HEAD_DIM).transpose(0, 2, 3, 1, 4)
    qh = qh.reshape(nb, N_KV_HEADS, groups * DEC_SEQ, HEAD_DIM)
    zeros = jnp.zeros_like(qh[:, 0])
    qbd = jnp.concatenate([jnp.concatenate([qh[:, 0], zeros], axis=-1),
                           jnp.concatenate([zeros, qh[:, 1]], axis=-1)], axis=1)
    sink_col = jnp.broadcast_to(jnp.repeat(sinks, DEC_SEQ).reshape(SWA_ROWS, 1), (SWA_ROWS, LANES))
    pad8 = lambda a: jnp.pad(a.reshape(nb, DEC_SEQ, KV_WIDTH), ((0, 0), (0, NEW_ROWS - DEC_SEQ), (0, 0)))
    to_t = lambda c: c.transpose(0, 2, 3, 1).reshape(nb, KV_WIDTH, WINDOW)
    blk = lambda r, w: pl.BlockSpec((bb, r, w), lambda i: (i, 0, 0))
    o, okt, ovt = pl.pallas_call(
        _swa_sample_kernel,
        grid=(nb // bb,),
        in_specs=[blk(SWA_ROWS, KV_WIDTH), pl.BlockSpec((SWA_ROWS, LANES), lambda i: (0, 0)),
                  blk(NEW_ROWS, KV_WIDTH), blk(NEW_ROWS, KV_WIDTH), blk(KV_WIDTH, WINDOW), blk(KV_WIDTH, WINDOW)],
        out_specs=[blk(SWA_ROWS, KV_WIDTH), blk(KV_WIDTH, WINDOW), blk(KV_WIDTH, WINDOW)],
        out_shape=[jax.ShapeDtypeStruct((nb, SWA_ROWS, KV_WIDTH), jnp.float32),
                   jax.ShapeDtypeStruct((nb, KV_WIDTH, WINDOW), jnp.float32),
                   jax.ShapeDtypeStruct((nb, KV_WIDTH, WINDOW), jnp.float32)],
        compiler_params=_params(("arbitrary",)),
        name="swa_sample",
    )(qbd, sink_col, pad8(kn), pad8(vn), to_t(cache_k), to_t(cache_v))
    o = o.reshape(nb, N_KV_HEADS, groups, DEC_SEQ, N_KV_HEADS, HEAD_DIM)
    attn = jnp.stack([o[:, h, :, :, h, :] for h in range(N_KV_HEADS)], axis=1)
    attn = attn.transpose(0, 3, 1, 2, 4).reshape(nb * DEC_SEQ, ATTN_WIDTH)
    from_t = lambda c: c.reshape(nb, N_KV_HEADS, HEAD_DIM, WINDOW).transpose(0, 3, 1, 2)
    return attn, from_t(okt), from_t(ovt)


def _mem_kv_kernel(mem_ref, wk_ref, wv_ref, wq_ref, wo_ref, mk_ref, mv_ref, wqk_ref, wvo_ref):
    mb = _bf16(mem_ref[...])
    mk = _dot(mb, wk_ref[...])
    mv = _dot(mb, wv_ref[...])
    mk_ref[...] = mk
    mv_ref[...] = mv
    mkb, mvb = _bf16(mk), _bf16(mv)
    for h in range(MEM_HEADS):
        sl = slice(h * MEM_HEAD_DIM, (h + 1) * MEM_HEAD_DIM)
        keys = slice(h * N_MEM, (h + 1) * N_MEM)
        wqk_ref[:, keys] = _bf16(_dot_nt(wq_ref[:, sl], mkb[:, sl]) * (MEM_HEAD_DIM ** -0.5))
        wvo_ref[keys, :] = _bf16(_dot(mvb[:, sl], wo_ref[sl, :]))


def _mem_kv(mem, wk_b, wv_b, wq_b, wo_b):
    full = lambda a: pl.BlockSpec(a.shape, lambda i: (0,) * a.ndim)
    blk = pl.BlockSpec((N_MEM, D_MODEL), lambda i: (0, 0))
    f32 = jax.ShapeDtypeStruct((N_MEM, D_MODEL), jnp.float32)
    fused = (D_MODEL, MEM_HEADS * N_MEM), (MEM_HEADS * N_MEM, D_MODEL)
    return pl.pallas_call(
        _mem_kv_kernel,
        grid=(1,),
        in_specs=[full(mem), full(wk_b), full(wv_b), full(wq_b), full(wo_b)],
        out_specs=[blk, blk] + [pl.BlockSpec(shp, lambda i: (0, 0)) for shp in fused],
        out_shape=[f32, f32] + [jax.ShapeDtypeStruct(shp, jnp.bfloat16) for shp in fused],
        compiler_params=_params(("arbitrary",)),
        name="mem_kv",
    )(mem, wk_b, wv_b, wq_b, wo_b)


def _mix_ln1(attn_ref, conv_ref, x_ref, wmix_ref, g1_ref, b1_ref):
    mix = _dot(_bf16(attn_ref[...]), wmix_ref[0:ATTN_WIDTH, :]) + _dot(conv_ref[...], wmix_ref[ATTN_WIDTH:, :])
    return _layer_norm(ALPHA * x_ref[...] + mix, g1_ref[...], b1_ref[...])


def _mem_q(h1, wq_ref):
    return _bf16(_dot(_bf16(h1), wq_ref[...]) * (MEM_HEAD_DIM ** -0.5))


def _route(h2, wrhl_ref, br_ref, tri_ref, carry):
    hi = _bf16(h2)
    lo = _bf16(h2 - hi.astype(jnp.float32))
    hh = _dot(hi, wrhl_ref[...])
    logits = hh[:, 0:LANES] + hh[:, LANES:] + _dot(lo, wrhl_ref[:, 0:LANES]) + br_ref[...]
    lane_i = lax.broadcasted_iota(jnp.int32, logits.shape, 1)
    lane = lane_i.astype(jnp.float32)
    big = jnp.float32(LANES)
    is_g = lane_i < N_GROUPS
    gl = jnp.where(is_g, logits, -jnp.inf)
    gmax = jnp.max(gl, axis=1, keepdims=True)
    gidx = jnp.min(jnp.where(is_g & (logits == gmax), lane, big), axis=1, keepdims=True)
    gsum = jnp.sum(jnp.exp(gl - gmax), axis=1, keepdims=True)
    gw = 1.0 / gsum
    eid = lane_i - N_GROUPS
    assert EXPERTS_PER_GROUP == 8
    grp = lax.shift_right_arithmetic(eid, jnp.full_like(eid, 3)).astype(jnp.float32)
    in_e = (lane_i >= N_GROUPS) & (lane_i < N_GROUPS + N_EXPERTS) & (grp == gidx)
    v1 = jnp.max(jnp.where(in_e, logits, -jnp.inf), axis=1, keepdims=True)
    i1 = jnp.min(jnp.where(in_e & (logits == v1), lane, big), axis=1, keepdims=True)
    rest = in_e & (lane != i1)
    v2 = jnp.max(jnp.where(rest, logits, -jnp.inf), axis=1, keepdims=True)
    i2 = jnp.min(jnp.where(rest & (logits == v2), lane, big), axis=1, keepdims=True)
    ex = jnp.exp(v2 - v1)
    den = 1.0 + ex
    w1 = gw / den
    w2 = gw * ex / den
    zero = jnp.zeros_like(logits)
    pick1 = lane == i1
    pick2 = lane == i2
    sel = jnp.where(pick1 | pick2, 1.0, 0.0)
    before = _dot(tri_ref[...], _bf16(sel)) + carry
    rank1 = jnp.sum(jnp.where(pick1, before, zero), axis=1, keepdims=True)
    rank2 = jnp.sum(jnp.where(pick2, before, zero), axis=1, keepdims=True)
    cols = (i1 - N_GROUPS, i2 - N_GROUPS, w1, w2, rank1, rank2)
    route = zero
    for k, col in enumerate(cols):
        route = jnp.where(lane_i == k, col, route)
    return route, carry + jnp.sum(sel, axis=0, keepdims=True)


def _post_tile(attn, conv_ref, x_ref, wmix_ref, g1_ref, b1_ref, wqk_ref, wvo_ref, g2_ref, b2_ref,
               wrhl_ref, br_ref, tri_ref, h2t_ref, rt_ref, cnt_ref, carry_ref):
    mix = _dot(attn, wmix_ref[0:ATTN_WIDTH, :]) + _dot(conv_ref[...], wmix_ref[ATTN_WIDTH:, :])
    h1 = _layer_norm(ALPHA * x_ref[...] + mix, g1_ref[...], b1_ref[...])
    scores = _dot(_bf16(h1), wqk_ref[...])
    probs = []
    for h in range(MEM_HEADS):
        s = scores[:, h * N_MEM:(h + 1) * N_MEM]
        p = jnp.exp(s - jnp.max(s, axis=1, keepdims=True))
        probs.append(_bf16(p / jnp.sum(p, axis=1, keepdims=True)))
    mem_out = _dot(jnp.concatenate(probs, axis=1), wvo_ref[...])
    h2 = _layer_norm(ALPHA * h1 + mem_out, g2_ref[...], b2_ref[...])
    _store_token_tiles(h2t_ref, h2)
    route, carry = _route(h2, wrhl_ref, br_ref, tri_ref, carry_ref[0:1, :])
    rt_ref[...] = route
    carry_ref[...] = jnp.broadcast_to(carry, carry_ref.shape)
    cnt_ref[...] = jnp.broadcast_to(carry, cnt_ref.shape)


def _swa_post_kernel(sinks_ref, q_ref, kc_ref, kp_ref, vc_ref, vp_ref,
                     conv_ref, x_ref, wmix_ref, g1_ref, b1_ref, wqk_ref, wvo_ref, g2_ref, b2_ref,
                     wrhl_ref, br_ref, tri_ref, h2s_ref, rts_ref, cnts_ref,
                     h2t_ref, rt_ref, cnt_ref, carry_ref, attn_s):
    t = pl.program_id(0)
    steps = N_PROMPT // TM_POST
    par = t % 2

    def swa(slot):
        def store(rows, cols, val):
            attn_s[slot, rows, cols] = val
        _swa_tile(t, sinks_ref, q_ref, kc_ref, kp_ref, vc_ref, vp_ref, store)

    def post(slot):
        _post_tile(attn_s[slot], conv_ref, x_ref, wmix_ref, g1_ref, b1_ref, wqk_ref, wvo_ref, g2_ref, b2_ref,
                   wrhl_ref, br_ref, tri_ref, h2t_ref, rt_ref, cnt_ref, carry_ref)

    @pl.when(t == 0)
    def _():
        carry_ref[...] = cnts_ref[...]
        swa(0)

    @pl.when((t >= 1) & (t < steps))
    def _():
        swa(par)
        post(1 - par)

    @pl.when(t == steps)
    def _():
        post(1 - par)

    @pl.when(t == steps + 1)
    def _():
        h2t_ref[...] = h2s_ref[...]
        rt_ref[...] = rts_ref[...]


def _swa_post(sinks, q, kx, vx, conv, x, wmix_b, g1, b1, wqk, wvo, g2, b2, wrhl, br, tri, h2t_s, rt_s, cnt_s):
    n = x.shape[0]
    tm = TM_POST
    assert tm == SWA_QB * WINDOW
    steps = n // tm
    cur = lambda w: pl.BlockSpec((tm, w), lambda i: (jnp.minimum(i, steps - 1), 0))
    prev = lambda w: pl.BlockSpec((WINDOW, w), lambda i: (jnp.clip(SWA_QB * i - 1, 0, n // WINDOW - 1), 0))
    lag = lambda w: pl.BlockSpec((tm, w), lambda i: (jnp.clip(i - 1, 0, steps - 1), 0))
    full = lambda a: pl.BlockSpec(a.shape, lambda i: (0,) * a.ndim)
    weights = (wmix_b, g1, b1, wqk, wvo, g2, b2, wrhl, br, tri, h2t_s, rt_s, cnt_s)
    n_out = n + h2t_s.shape[0] // ROW_CHUNKS
    out_idx = lambda i: (jnp.where(i > steps, steps, jnp.clip(i - 1, 0, steps - 1)), 0)
    return pl.pallas_call(
        _swa_post_kernel,
        grid=(steps + 2,),
        in_specs=([pl.BlockSpec(memory_space=pltpu.SMEM), cur(ATTN_WIDTH),
                   cur(4 * LANES), prev(4 * LANES), cur(4 * LANES), prev(4 * LANES),
                   lag(CONV_CH), lag(D_MODEL)] + [full(a) for a in weights]),
        out_specs=[pl.BlockSpec((tm * ROW_CHUNKS, LANES), out_idx),
                   pl.BlockSpec((tm, LANES), out_idx),
                   pl.BlockSpec((SUBLANES, LANES), lambda i: (0, 0))],
        out_shape=[jax.ShapeDtypeStruct((n_out * ROW_CHUNKS, LANES), jnp.float32),
                   jax.ShapeDtypeStruct((n_out, LANES), jnp.float32),
                   jax.ShapeDtypeStruct((SUBLANES, LANES), jnp.float32)],
        scratch_shapes=[pltpu.VMEM((SUBLANES, LANES), jnp.float32),
                        pltpu.VMEM((2, tm, ATTN_WIDTH), jnp.bfloat16)],
        compiler_params=_params(("arbitrary",)),
        name="swa_post_prompt",
    )(sinks, q, kx, kx, vx, vx, conv, x, *weights)


def _post_a_sample_kernel(attn_ref, conv_ref, x_ref, wmix_ref, g1_ref, b1_ref, wq_ref, h1_ref, qm_ref):
    h1 = _mix_ln1(attn_ref, conv_ref, x_ref, wmix_ref, g1_ref, b1_ref)
    h1_ref[...] = h1
    qm_ref[...] = _mem_q(h1, wq_ref).astype(jnp.float32)


def _post_a_sample(attn, conv, x, wmix_b, g1, b1, wq_b):
    n = x.shape[0]
    args = (attn, conv, x, wmix_b, g1, b1, wq_b)
    full = lambda a: pl.BlockSpec(a.shape, lambda i: (0,) * a.ndim)
    blk = pl.BlockSpec((n, D_MODEL), lambda i: (0, 0))
    return pl.pallas_call(
        _post_a_sample_kernel,
        grid=(1,),
        in_specs=[full(a) for a in args],
        out_specs=[blk, blk],
        out_shape=[jax.ShapeDtypeStruct((n, D_MODEL), jnp.float32),
                   jax.ShapeDtypeStruct((n, D_MODEL), jnp.float32)],
        compiler_params=_params(("arbitrary",)),
        name="post_a_sample",
    )(*args)


MEM_ROWS = MEM_HEADS * DEC_SEQ


def _mem_attn_sample_body(q_ref, mk_ref, mv_ref, o_ref):
    nk = N_MEM * MEM_HEADS
    nb = SAMPLE_BB
    rows = nb * MEM_ROWS
    row_h = (lax.broadcasted_iota(jnp.int32, (rows, nk), 0) % MEM_ROWS) // DEC_SEQ
    key_h = lax.broadcasted_iota(jnp.int32, (rows, nk), 1) % MEM_HEADS
    s = jnp.concatenate(
        [_dot_nt(_bf16(q_ref[b]), _bf16(mk_ref[b].reshape(nk, MEM_HEAD_DIM))) for b in range(nb)], axis=0)
    s = jnp.where(row_h == key_h, s, -jnp.inf)
    p = jnp.exp(s - jnp.max(s, axis=1, keepdims=True))
    rden = 1.0 / jnp.sum(p, axis=1, keepdims=True)
    p = _bf16(p)
    for b in range(nb):
        r = slice(b * MEM_ROWS, (b + 1) * MEM_ROWS)
        o_ref[b] = _dot(p[r], _bf16(mv_ref[b].reshape(nk, MEM_HEAD_DIM))) * rden[r]


def _proj_mem_kernel(x_ref, w_ref, tab_ref, cw_ref, mq_ref, mk_ref, mv_ref,
                     q_ref, kx_ref, vx_ref, conv_ref, ktail_ref, vtail_ref, utail_ref, mo_ref, carry_ref):
    _mem_attn_sample_body(mq_ref, mk_ref, mv_ref, mo_ref)
    _proj_prompt_body(x_ref, w_ref, tab_ref, cw_ref,
                      q_ref, kx_ref, vx_ref, conv_ref, ktail_ref, vtail_ref, utail_ref, carry_ref)


def _proj_prompt_mem_sample(x, w_in_b, tab, conv_w, qm, mk, mv):
    n = x.shape[0]
    nb = mk.shape[0]
    bb = SAMPLE_BB
    steps = nb // bb
    tm = n // steps
    row = lambda w: pl.BlockSpec((tm, w), lambda i: (i, 0))
    full = lambda a: pl.BlockSpec(a.shape, lambda i: (0,) * a.ndim)
    const = lambda r, w: pl.BlockSpec((r, w), lambda i: (0, 0))
    mq = qm.reshape(nb, DEC_SEQ, MEM_HEADS, MEM_HEAD_DIM).transpose(0, 2, 1, 3).reshape(nb, MEM_ROWS, MEM_HEAD_DIM)
    mrows = pl.BlockSpec((bb, MEM_ROWS, MEM_HEAD_DIM), lambda i: (i, 0, 0))
    kv = pl.BlockSpec((bb, N_MEM, MEM_HEADS, MEM_HEAD_DIM), lambda i: (i, 0, 0, 0))
    outs = pl.pallas_call(
        _proj_mem_kernel,
        grid=(steps,),
        in_specs=[row(D_MODEL), full(w_in_b), pl.BlockSpec((ROPE_ROWS, tm), lambda i: (0, i)), full(conv_w),
                  mrows, kv, kv],
        out_specs=[row(ATTN_WIDTH), row(4 * LANES), row(4 * LANES), row(CONV_CH),
                   const(WINDOW, KV_WIDTH), const(WINDOW, KV_WIDTH), const(SUBLANES, CONV_CH), mrows],
        out_shape=[jax.ShapeDtypeStruct((n, ATTN_WIDTH), jnp.bfloat16),
                   jax.ShapeDtypeStruct((n, 4 * LANES), jnp.bfloat16),
                   jax.ShapeDtypeStruct((n, 4 * LANES), jnp.bfloat16),
                   jax.ShapeDtypeStruct((n, CONV_CH), jnp.bfloat16),
                   jax.ShapeDtypeStruct((WINDOW, KV_WIDTH), jnp.float32),
                   jax.ShapeDtypeStruct((WINDOW, KV_WIDTH), jnp.float32),
                   jax.ShapeDtypeStruct((SUBLANES, CONV_CH), jnp.float32),
                   jax.ShapeDtypeStruct((nb, MEM_ROWS, MEM_HEAD_DIM), jnp.float32)],
        scratch_shapes=[pltpu.VMEM((SUBLANES, CONV_CH), jnp.float32)],
        compiler_params=_params(("arbitrary",)),
        name="proj_prompt_mem_sample",
    )(x, w_in_b, tab, conv_w, mq, mk, mv)
    o = outs[7].reshape(nb, MEM_HEADS, DEC_SEQ, MEM_HEAD_DIM).transpose(0, 2, 1, 3).reshape(nb * DEC_SEQ, D_MODEL)
    return outs[:7], o


def _post_b_sample_kernel(o_ref, h1_ref, wo_ref, g2_ref, b2_ref, wrhl_ref, br_ref, tri_ref,
                          h2t_ref, rt_ref, cnt_ref):
    h2 = _layer_norm(ALPHA * h1_ref[...] + _dot(_bf16(o_ref[...]), wo_ref[...]), g2_ref[...], b2_ref[...])
    _store_token_tiles(h2t_ref, h2)
    route, carry = _route(h2, wrhl_ref, br_ref, tri_ref, jnp.zeros((1, LANES), jnp.float32))
    rt_ref[...] = route
    cnt_ref[...] = jnp.broadcast_to(carry, cnt_ref.shape)


def _post_b_sample(o, h1, wo_b, g2, b2, wrhl, br, tri):
    n = h1.shape[0]
    args = (o, h1, wo_b, g2, b2, wrhl, br, tri)
    full = lambda a: pl.BlockSpec(a.shape, lambda i: (0,) * a.ndim)
    return pl.pallas_call(
        _post_b_sample_kernel,
        grid=(1,),
        in_specs=[full(a) for a in args],
        out_specs=[pl.BlockSpec((n * ROW_CHUNKS, LANES), lambda i: (0, 0)),
                   pl.BlockSpec((n, LANES), lambda i: (0, 0)),
                   pl.BlockSpec((SUBLANES, LANES), lambda i: (0, 0))],
        out_shape=[jax.ShapeDtypeStruct((n * ROW_CHUNKS, LANES), jnp.float32),
                   jax.ShapeDtypeStruct((n, LANES), jnp.float32),
                   jax.ShapeDtypeStruct((SUBLANES, LANES), jnp.float32)],
        compiler_params=_params(("arbitrary",)),
        name="post_b_sample",
    )(*args)


def _row_gather_copy(src_hbm, idx, dst, dst_row, sem):
    s0 = pl.multiple_of(idx * ROW_CHUNKS, ROW_CHUNKS)
    d0 = pl.multiple_of(dst_row * ROW_CHUNKS, ROW_CHUNKS)
    return pltpu.make_async_copy(src_hbm.at[pl.ds(s0, ROW_CHUNKS), :], dst.at[pl.ds(d0, ROW_CHUNKS), :], sem)


def _dispatch_kernel(pos_ref, h2t_ref, xs_hbm, sem):
    def body(r, c):
        src = h2t_ref.at[pl.ds(pl.multiple_of(r * ROW_CHUNKS, ROW_CHUNKS), ROW_CHUNKS), :]
        for k in range(2):
            d0 = pl.multiple_of(pos_ref[0, 0, k * TM_COMB + r] * ROW_CHUNKS, ROW_CHUNKS)
            pltpu.make_async_copy(src, xs_hbm.at[pl.ds(d0, ROW_CHUNKS), :], sem.at[0]).start(priority=k)
        return c
    lax.fori_loop(0, TM_COMB, body, 0, unroll=8)
    for _ in range(2):
        pltpu.make_async_copy(h2t_ref, xs_hbm.at[pl.ds(0, TM_COMB * ROW_CHUNKS), :], sem.at[0]).wait()


def _dispatch(pos3, h2t):
    nt = N_ALL // TM_COMB
    return pl.pallas_call(
        _dispatch_kernel,
        grid=(nt,),
        in_specs=[pl.BlockSpec((1, 1, 2 * TM_COMB), lambda i: (i, 0, 0), memory_space=pltpu.SMEM),
                  pl.BlockSpec((TM_COMB * ROW_CHUNKS, LANES), lambda i: (i, 0))],
        out_specs=pl.BlockSpec(memory_space=pl.ANY),
        out_shape=jax.ShapeDtypeStruct((N_ASSIGN * ROW_CHUNKS, LANES), jnp.float32),
        scratch_shapes=[pltpu.SemaphoreType.DMA((1,))],
        compiler_params=_params(("arbitrary",)),
        name="moe_dispatch",
    )(pos3, h2t)


def _moe_ffn_kernel(it_ref, ie_ref, lo_ref, hi_ref, x_ref, wg_ref, wu_ref, wd_ref, y_ref, wgb, wub, wdb, cur_e):
    i = pl.program_id(0)
    lo = lo_ref[i]
    hi = hi_ref[i]
    e = ie_ref[i]

    @pl.when(i == 0)
    def _():
        cur_e[0] = -1

    @pl.when((hi > lo) & (cur_e[0] != e))
    def _():
        wgb[...] = _bf16(wg_ref[0])
        wub[...] = _bf16(wu_ref[0])
        wdb[...] = _bf16(wd_ref[0])
        cur_e[0] = e

    def ffn(r0, rows):
        x = _bf16(_load_token_tiles(x_ref, r0 * ROW_CHUNKS, rows))
        hg = _dot(x, wgb[...])
        hu = _dot(x, wub[...])
        h = hg / (1.0 + jnp.exp(-hg)) * hu
        return _dot(_bf16(h), wdb[...])

    whole = (lo == 0) & (hi == TM_MOE)

    @pl.when(whole)
    def _():
        _store_token_tiles(y_ref, ffn(0, TM_MOE))

    for r0 in range(0, TM_MOE, MOE_BLOCK):
        live = (hi > lo) & jnp.logical_not(whole) & (hi > r0) & (lo < r0 + MOE_BLOCK)

        def store(merge, r0=r0):
            y = ffn(r0, MOE_BLOCK)
            row = r0 + lax.broadcasted_iota(jnp.int32, (MOE_BLOCK, LANES), 0)
            mask = (row >= lo) & (row < hi)
            for c in range(ROW_CHUNKS):
                sl = pl.ds(r0 * ROW_CHUNKS + c, MOE_BLOCK, stride=ROW_CHUNKS)
                y_ref[sl, :] = jnp.where(mask, y[:, c * LANES:(c + 1) * LANES], y_ref[sl, :] if merge else 0.0)

        pl.when(live & (lo <= r0))(functools.partial(store, False))
        pl.when(live & (lo > r0))(functools.partial(store, True))


def _moe_ffn(item_tile, item_expert, item_lo, item_hi, x_sorted, w_gate, w_up, w_down):
    wspec = lambda shp: pl.BlockSpec((1,) + shp, lambda i, it, ie, lo, hi: (ie[i], 0, 0))
    tile = pl.BlockSpec((TM_MOE * ROW_CHUNKS, LANES), lambda i, it, ie, lo, hi: (it[i], 0))
    grid_spec = pltpu.PrefetchScalarGridSpec(
        num_scalar_prefetch=4,
        grid=(MOE_ITEMS,),
        in_specs=[tile, wspec((D_MODEL, EXPERT_FF)), wspec((D_MODEL, EXPERT_FF)), wspec((EXPERT_FF, D_MODEL))],
        out_specs=tile,
        scratch_shapes=[pltpu.VMEM((D_MODEL, EXPERT_FF), jnp.bfloat16),
                        pltpu.VMEM((D_MODEL, EXPERT_FF), jnp.bfloat16),
                        pltpu.VMEM((EXPERT_FF, D_MODEL), jnp.bfloat16),
                        pltpu.SMEM((1,), jnp.int32)],
    )
    return pl.pallas_call(
        _moe_ffn_kernel,
        grid_spec=grid_spec,
        out_shape=jax.ShapeDtypeStruct((N_ASSIGN * ROW_CHUNKS, LANES), jnp.float32),
        compiler_params=_params(("arbitrary",)),
        name="moe_ffn",
    )(item_tile, item_expert, item_lo, item_hi, x_sorted, w_gate, w_up, w_down)


def _combine_kernel(nt, pos_cur_ref, pos_nxt_ref, yt_hbm, h2t_ref, rt_ref, g3_ref, b3_ref, o_ref, abuf, sem):
    t = pl.program_id(0)
    slot = t % 2
    rows = 2 * TM_COMB

    def issue(pos_ref, s):
        def body(j, c):
            for k in range(2):
                r = 2 * j + k
                _row_gather_copy(yt_hbm, pos_ref[0, 0, r], abuf, s * rows + r, sem.at[s]).start(priority=k)
            return c
        lax.fori_loop(0, rows // 2, body, 0, unroll=16)

    @pl.when(t == 0)
    def _():
        issue(pos_cur_ref, 0)

    @pl.when(t + 1 < nt)
    def _():
        issue(pos_nxt_ref, 1 - slot)

    base = pl.multiple_of(slot * (rows * ROW_CHUNKS), rows * ROW_CHUNKS)
    pltpu.make_async_copy(yt_hbm.at[pl.ds(0, rows * ROW_CHUNKS), :],
                          abuf.at[pl.ds(base, rows * ROW_CHUNKS), :], sem.at[slot]).wait()
    ya = _load_token_tiles(abuf, base, TM_COMB)
    yb = _load_token_tiles(abuf, base + TM_COMB * ROW_CHUNKS, TM_COMB)
    rt = rt_ref[...]
    ff = rt[:, 2:3] * ya + rt[:, 3:4] * yb
    h2 = _load_token_tiles(h2t_ref, 0, TM_COMB)
    o_ref[...] = _layer_norm(ALPHA * h2 + ff, g3_ref[...], b3_ref[...])


def _combine(pos3, yt, h2t, rt, g3, b3, tile0, n_tiles):
    last = tile0 + n_tiles - 1
    smem_pos = lambda f: pl.BlockSpec((1, 1, 2 * TM_COMB), f, memory_space=pltpu.SMEM)
    full = lambda a: pl.BlockSpec(a.shape, lambda i: (0,) * a.ndim)
    return pl.pallas_call(
        functools.partial(_combine_kernel, n_tiles),
        grid=(n_tiles,),
        in_specs=[smem_pos(lambda i: (tile0 + i, 0, 0)),
                  smem_pos(lambda i: (jnp.minimum(tile0 + i + 1, last), 0, 0)),
                  pl.BlockSpec(memory_space=pl.ANY),
                  pl.BlockSpec((TM_COMB * ROW_CHUNKS, LANES), lambda i: (tile0 + i, 0)),
                  pl.BlockSpec((TM_COMB, LANES), lambda i: (tile0 + i, 0)),
                  full(g3), full(b3)],
        out_specs=pl.BlockSpec((TM_COMB, D_MODEL), lambda i: (i, 0)),
        out_shape=jax.ShapeDtypeStruct((n_tiles * TM_COMB, D_MODEL), jnp.float32),
        scratch_shapes=[pltpu.VMEM((2 * 2 * TM_COMB * ROW_CHUNKS, LANES), jnp.float32),
                        pltpu.SemaphoreType.DMA((2,))],
        compiler_params=_params(("arbitrary",)),
        name="moe_combine",
    )(pos3, pos3, yt, h2t, rt, g3, b3)


POS_TILES = 11


def _positions_kernel(rt_ref, starts_ref, pos_ref):
    expert = lax.broadcasted_iota(jnp.int32, (N_EXPERTS, TM_COMB), 0).astype(jnp.float32)
    starts = jnp.concatenate([starts_ref[...]] * (TM_COMB // LANES), axis=1)
    for j in range(POS_TILES):
        cols = rt_ref[j * TM_COMB:(j + 1) * TM_COMB, :].T
        out = []
        for k in range(2):
            seg = jnp.sum(jnp.where(expert == cols[k:k + 1, :], starts, 0.0), axis=0, keepdims=True)
            out.append(seg + cols[4 + k:5 + k, :])
        pos_ref[j] = jnp.concatenate(out, axis=1).astype(jnp.int32)


def _positions(rt, starts_rep):
    nt = N_ALL // TM_COMB
    assert nt % POS_TILES == 0
    return pl.pallas_call(
        _positions_kernel,
        grid=(nt // POS_TILES,),
        in_specs=[pl.BlockSpec((POS_TILES * TM_COMB, LANES), lambda i: (i, 0)),
                  pl.BlockSpec((N_EXPERTS, LANES), lambda i: (0, 0))],
        out_specs=pl.BlockSpec((POS_TILES, 1, 2 * TM_COMB), lambda i: (i, 0, 0)),
        out_shape=jax.ShapeDtypeStruct((nt, 1, 2 * TM_COMB), jnp.int32),
        compiler_params=_params(("arbitrary",)),
        name="moe_positions",
    )(rt, starts_rep)


def _routing_plan(rt, cnt):
    i32 = jnp.int32
    counts_f = cnt[0, N_GROUPS:N_GROUPS + N_EXPERTS]
    starts_f = jnp.cumsum(counts_f) - counts_f
    pos3 = _positions(rt, jnp.broadcast_to(starts_f[:, None], (N_EXPERTS, LANES)))
    starts = starts_f.astype(i32)
    tiles = jnp.arange(MOE_TILES, dtype=i32) * TM_MOE
    rank_t = jnp.arange(MOE_TILES, dtype=i32) + jnp.sum((starts[None, :] < tiles[:, None]).astype(i32), axis=1)
    rank_s = jnp.arange(N_EXPERTS, dtype=i32) + jnp.sum((tiles[None, :] <= starts[:, None]).astype(i32), axis=1)
    vals = jnp.concatenate([tiles, starts])
    ranks = jnp.concatenate([rank_t, rank_s])
    slot = jnp.arange(MOE_ITEMS, dtype=i32)
    lo = jnp.sum(jnp.where(ranks[None, :] == slot[:, None], vals[None, :], 0), axis=1)
    hi = jnp.concatenate([lo[1:], jnp.full((1,), N_ASSIGN, i32)])
    item_tile = jnp.minimum(lo // TM_MOE, MOE_TILES - 1)
    item_expert = jnp.clip(jnp.sum((starts[None, :] <= lo[:, None]).astype(i32), axis=1) - 1, 0, N_EXPERTS - 1)
    base = item_tile * TM_MOE
    return item_tile, item_expert, lo - base, hi - base, pos3


def kernel(x_prompt, x_sample, mem_prompt, cache_swa_k, cache_swa_v, cache_conv, cache_mem_k, cache_mem_v,
           w_in, sinks, conv_w, w_mix_out, ln1_g, ln1_b, w_q_mem, w_k_mem, w_v_mem, w_o_mem, ln2_g, ln2_b,
           w_router_group, b_router_group, w_router_expert, b_router_expert, w_gate, w_up, w_down,
           ln3_g, ln3_b):
    f32 = jnp.float32
    row = lambda a: a.reshape(1, -1).astype(f32)
    w_in_b, wmix_b, wq_b, wk_b, wv_b, wo_b = (_bf16(w) for w in (w_in, w_mix_out, w_q_mem, w_k_mem, w_v_mem, w_o_mem))
    g1, b1, g2, b2, g3, b3 = (row(a) for a in (ln1_g, ln1_b, ln2_g, ln2_b, ln3_g, ln3_b))
    pad = LANES - N_GROUPS - N_EXPERTS
    wr = jnp.concatenate([w_router_group, w_router_expert, jnp.zeros((D_MODEL, pad), f32)], axis=1)
    wrh = _bf16(wr)
    wrhl = jnp.concatenate([wrh, _bf16(wr - wrh.astype(f32))], axis=1)
    br = jnp.concatenate([b_router_group, b_router_expert, jnp.zeros((pad,), f32)]).reshape(1, LANES)

    xs = x_sample.reshape(N_SAMPLE, D_MODEL)
    tab_s = jnp.tile(_rope_table(PAST_LEN + jnp.arange(DEC_SEQ)), (1, DEC_BATCH))
    c0 = jnp.repeat(cache_conv[:, 0], DEC_SEQ, axis=0)
    c1 = jnp.repeat(cache_conv[:, 1], DEC_SEQ, axis=0)
    q_s, k_s, v_s, conv_s, u_s = _proj_sample(xs, w_in_b, tab_s, conv_w, c0, c1)
    attn_s, swa_k_s, swa_v_s = _swa_sample(sinks, q_s, k_s, v_s, cache_swa_k, cache_swa_v)
    h1_s, qm_s = _post_a_sample(attn_s, conv_s, xs, wmix_b, g1, b1, wq_b)
    xp = x_prompt.reshape(N_PROMPT, D_MODEL)
    tab_p = _rope_table(jnp.arange(N_PROMPT))
    (q_p, kx_p, vx_p, conv_p, k_tail, v_tail, u_tail), o_s = _proj_prompt_mem_sample(
        xp, w_in_b, tab_p, conv_w, qm_s, cache_mem_k, cache_mem_v)
    tri = _bf16(jnp.tril(jnp.ones((TM_POST, TM_POST), f32), -1))
    h2t_s, rt_s, cnt_s = _post_b_sample(o_s, h1_s, wo_b, g2, b2, wrhl, br, tri)

    mk, mv, wqk, wvo = _mem_kv(mem_prompt.reshape(N_MEM, D_MODEL), wk_b, wv_b, wq_b, wo_b)
    h2t, rt, cnt = _swa_post(sinks, q_p, kx_p, vx_p, conv_p, xp, wmix_b, g1, b1, wqk, wvo, g2, b2,
                             wrhl, br, tri, h2t_s, rt_s, cnt_s)

    item_tile, item_expert, item_lo, item_hi, pos3 = _routing_plan(rt, cnt)
    x_sorted = _dispatch(pos3, h2t)
    yt = _moe_ffn(item_tile, item_expert, item_lo, item_hi, x_sorted, w_gate, w_up, w_down)
    y_p = _combine(pos3, yt, h2t, rt, g3, b3, 0, N_PROMPT // TM_COMB)
    y_s = _combine(pos3, yt, h2t, rt, g3, b3, N_PROMPT // TM_COMB, N_SAMPLE // TM_COMB)

    return (y_p.reshape(1, SEQ, D_MODEL),
            y_s.reshape(DEC_BATCH, DEC_SEQ, D_MODEL),
            k_tail.reshape(1, WINDOW, N_KV_HEADS, HEAD_DIM),
            v_tail.reshape(1, WINDOW, N_KV_HEADS, HEAD_DIM),
            u_tail[SUBLANES - (CONV_K - 1):].reshape(1, CONV_K - 1, CONV_CH),
            mk.reshape(1, N_MEM, MEM_HEADS, MEM_HEAD_DIM),
            mv.reshape(1, N_MEM, MEM_HEADS, MEM_HEAD_DIM),
            swa_k_s.reshape(DEC_BATCH, WINDOW, N_KV_HEADS, HEAD_DIM),
            swa_v_s.reshape(DEC_BATCH, WINDOW, N_KV_HEADS, HEAD_DIM),
            u_s.reshape(DEC_BATCH, DEC_SEQ, CONV_CH)[:, DEC_SEQ - (CONV_K - 1):])
```

```python
import functools

import jax
import jax.numpy as jnp
from jax import lax
from jax.experimental import pallas as pl
from jax.experimental.pallas import tpu as pltpu

D_MODEL = 1024
SEQ = 16384
DEC_BATCH = 128
DEC_SEQ = 4
PAST_LEN = 16384
ATTN_WIDTH = 512
CONV_CH = 512
HEAD_DIM = 64
N_HEADS = 8
N_KV_HEADS = 2
KV_WIDTH = 128
WINDOW = 128
ROPE_THETA = 500000.0
ROPE_DIM = 16
CONV_K = 3
Q_END = ATTN_WIDTH
K_END = Q_END + KV_WIDTH
V_END = K_END + KV_WIDTH
B_END = V_END + CONV_CH
C_END = B_END + CONV_CH
IN_WIDTH = C_END + CONV_CH
N_MEM = 256
MEM_HEADS = 4
MEM_HEAD_DIM = 256
N_GROUPS = 4
EXPERTS_PER_GROUP = 8
N_EXPERTS = 32
EXPERT_FF = 256
ALPHA = 2.0 ** 0.25
LN_EPS = 1e-5

LANES = 128
SUBLANES = 8
ROW_CHUNKS = D_MODEL // LANES
VMEM_LIMIT = 56 * 1024 * 1024

N_PROMPT = SEQ
N_SAMPLE = DEC_BATCH * DEC_SEQ
N_ALL = N_PROMPT + N_SAMPLE
TM_POST = 512
TM_MOE = 1024
TM_COMB = 512
MOE_BLOCK = 256
N_ASSIGN = 2 * N_ALL
MOE_TILES = N_ASSIGN // TM_MOE
MOE_ITEMS = MOE_TILES + N_EXPERTS
SAMPLE_BB = 4
SWA_BB = 16

assert ROW_CHUNKS == SUBLANES
assert N_SAMPLE == TM_POST
assert N_ASSIGN % TM_MOE == 0 and N_ALL % TM_COMB == 0


def _params(sem, vmem=VMEM_LIMIT):
    return pltpu.CompilerParams(dimension_semantics=sem, vmem_limit_bytes=vmem)


def _bf16(x):
    return x.astype(jnp.bfloat16)


def _dot(a, b):
    return jnp.dot(a, b, preferred_element_type=jnp.float32)


def _dot_nt(a, b):
    return lax.dot_general(a, b, (((1,), (1,)), ((), ())), preferred_element_type=jnp.float32)


def _layer_norm(x, g, b):
    mu = jnp.mean(x, axis=-1, keepdims=True)
    xc = x - mu
    var = jnp.mean(xc * xc, axis=-1, keepdims=True)
    return xc * lax.rsqrt(var + LN_EPS) * g + b


def _rope(x, cos_t, sin_t):
    lane = lax.broadcasted_iota(jnp.int32, x.shape, 1) % HEAD_DIM
    half = ROPE_DIM // 2
    partner = jnp.where(lane < half, pltpu.roll(x, LANES - half, axis=1), pltpu.roll(x, half, axis=1))
    return x * cos_t + partner * sin_t


def _head_slabs(x):
    lane = lax.broadcasted_iota(jnp.int32, x.shape, 1)
    lo = lane < HEAD_DIM
    sw = pltpu.roll(x, HEAD_DIM, axis=1)
    zero = jnp.zeros_like(x)
    slabs = [jnp.where(lo, x, zero), jnp.where(lo, zero, sw), jnp.where(lo, sw, zero), jnp.where(lo, zero, x)]
    return _bf16(jnp.concatenate(slabs, axis=1))


def _store_token_tiles(ref, val):
    rows = val.shape[0]
    for c in range(ROW_CHUNKS):
        ref[pl.ds(c, rows, stride=ROW_CHUNKS), :] = val[:, c * LANES:(c + 1) * LANES]


def _load_token_tiles(ref, base, rows):
    return jnp.concatenate(
        [ref[pl.ds(base + c, rows, stride=ROW_CHUNKS), :] for c in range(ROW_CHUNKS)], axis=1)


ROPE_ONE = 3 * (ROPE_DIM // 2)
ROPE_ROWS = 32


def _rope_patterns(tab):
    half = ROPE_DIM // 2
    m = lax.broadcasted_iota(jnp.int32, tab.shape, 1) % HEAD_DIM
    idx_c = jnp.where(m < ROPE_DIM, m % half, ROPE_ONE)
    idx_s = jnp.where(m < half, 2 * half + m, jnp.where(m < ROPE_DIM, m, ROPE_ONE + 1))
    return jnp.take_along_axis(tab, idx_c, axis=1), jnp.take_along_axis(tab, idx_s, axis=1)


def _proj_common(x_ref, w_ref, tab_ref):
    xb = _bf16(x_ref[...])
    tab = tab_ref[...]
    pad = jnp.zeros((LANES - tab.shape[0], tab.shape[1]), jnp.float32)
    cos_t, sin_t = _rope_patterns(jnp.concatenate([tab, pad], axis=0).T)
    q = _dot(xb, w_ref[:, 0:Q_END])
    q_rot = jnp.concatenate(
        [_rope(q[:, p * LANES:(p + 1) * LANES], cos_t, sin_t) for p in range(ATTN_WIDTH // LANES)], axis=1)
    q_out = _bf16(q_rot * (HEAD_DIM ** -0.5))
    kv = _dot(xb, w_ref[:, Q_END:V_END])
    k = _rope(kv[:, 0:KV_WIDTH], cos_t, sin_t)
    v = kv[:, KV_WIDTH:]
    bg = _dot(xb, w_ref[:, V_END:B_END])
    u = _dot(xb, w_ref[:, B_END:C_END]) * _dot(xb, w_ref[:, C_END:IN_WIDTH])
    return q_out, k, v, bg, u


def _conv3(bg, u, u1, u2, cw_ref):
    cw = cw_ref[...]
    return bg * (cw[0:1, :] * u2 + cw[1:2, :] * u1 + cw[2:3, :] * u)


def _proj_prompt_body(x_ref, w_ref, tab_ref, cw_ref,
                      q_ref, kx_ref, vx_ref, conv_ref, ktail_ref, vtail_ref, utail_ref, carry_ref):
    @pl.when(pl.program_id(0) == 0)
    def _():
        carry_ref[...] = jnp.zeros_like(carry_ref)

    q_out, k, v, bg, u = _proj_common(x_ref, w_ref, tab_ref)
    tm = u.shape[0]
    ext = jnp.concatenate([carry_ref[...], u], axis=0)
    u1 = pltpu.roll(ext, 1, axis=0)[SUBLANES:SUBLANES + tm]
    u2 = pltpu.roll(ext, 2, axis=0)[SUBLANES:SUBLANES + tm]
    q_ref[...] = q_out
    kx_ref[...] = _head_slabs(k)
    vx_ref[...] = _head_slabs(v)
    conv_ref[...] = _bf16(_conv3(bg, u, u1, u2, cw_ref))
    ktail_ref[...] = k[tm - WINDOW:tm]
    vtail_ref[...] = v[tm - WINDOW:tm]
    utail_ref[...] = u[tm - SUBLANES:tm]
    carry_ref[...] = u[tm - SUBLANES:tm]


def _proj_sample_kernel(x_ref, w_ref, tab_ref, cw_ref, c0_ref, c1_ref,
                        q_ref, k_ref, v_ref, conv_ref, u_ref):
    q_out, k, v, bg, u = _proj_common(x_ref, w_ref, tab_ref)
    t = lax.broadcasted_iota(jnp.int32, u.shape, 0) % DEC_SEQ
    c0 = c0_ref[...]
    c1 = c1_ref[...]
    u1 = jnp.where(t >= 1, pltpu.roll(u, 1, axis=0), c1)
    u2 = jnp.where(t >= 2, pltpu.roll(u, 2, axis=0), jnp.where(t == 1, c1, c0))
    q_ref[...] = q_out.astype(jnp.float32)
    k_ref[...] = k
    v_ref[...] = v
    conv_ref[...] = _bf16(_conv3(bg, u, u1, u2, cw_ref))
    u_ref[...] = u


def _rope_table(pos):
    half = ROPE_DIM // 2
    inv = ROPE_THETA ** (-jnp.arange(0, ROPE_DIM, 2, dtype=jnp.float32) / ROPE_DIM)
    ang = pos.astype(jnp.float32)[None, :] * inv[:, None]
    cos, sin = jnp.cos(ang), jnp.sin(ang)
    n = pos.shape[0]
    assert ROPE_ONE == 3 * half
    return jnp.concatenate([cos, sin, -sin, jnp.ones((1, n), jnp.float32),
                            jnp.zeros((ROPE_ROWS - ROPE_ONE - 1, n), jnp.float32)], axis=0)


def _proj_sample(x, w_in_b, tab, conv_w, c0, c1):
    n = x.shape[0]
    full = lambda a: pl.BlockSpec(a.shape, lambda i: (0,) * a.ndim)
    out = lambda w, dt: jax.ShapeDtypeStruct((n, w), dt)
    blk = lambda w: pl.BlockSpec((n, w), lambda i: (0, 0))
    return pl.pallas_call(
        _proj_sample_kernel,
        grid=(1,),
        in_specs=[full(x), full(w_in_b), full(tab), full(conv_w), full(c0), full(c1)],
        out_specs=[blk(ATTN_WIDTH), blk(KV_WIDTH), blk(KV_WIDTH), blk(CONV_CH), blk(CONV_CH)],
        out_shape=[out(ATTN_WIDTH, jnp.float32), out(KV_WIDTH, jnp.float32), out(KV_WIDTH, jnp.float32),
                   out(CONV_CH, jnp.bfloat16), out(CONV_CH, jnp.float32)],
        compiler_params=_params(("arbitrary",)),
        name="proj_sample",
    )(x, w_in_b, tab, conv_w, c0, c1)


def _sink_softmax_pv(s, valid, sink, vx):
    s = jnp.where(valid, s, -jnp.inf)
    m = jnp.maximum(jnp.max(s, axis=1, keepdims=True), sink)
    p = jnp.exp(s - m)
    den = jnp.sum(p, axis=1, keepdims=True) + jnp.exp(sink - m)
    return _dot(_bf16(p), vx) / den


SWA_QB = 4


def _swa_tile(step, sinks_ref, q_ref, kc_ref, kp_ref, vc_ref, vp_ref, store):
    kall = jnp.concatenate([kp_ref[...], kc_ref[...]], axis=0)
    vall = jnp.concatenate([vp_ref[...], vc_ref[...]], axis=0)
    i = lax.broadcasted_iota(jnp.int32, (WINDOW, 2 * WINDOW), 0)
    j = lax.broadcasted_iota(jnp.int32, (WINDOW, 2 * WINDOW), 1)
    band = (j > i) & (j <= i + WINDOW)
    for sb in range(SWA_QB):
        rows = slice(sb * WINDOW, (sb + 1) * WINDOW)
        kcat = kall[sb * WINDOW:(sb + 2) * WINDOW]
        vcat = vall[sb * WINDOW:(sb + 2) * WINDOW]
        valid = band & ((step > 0) | (j >= WINDOW)) if sb == 0 else band
        for p in range(N_HEADS // 2):
            qs = q_ref[rows, p * LANES:(p + 1) * LANES]
            acc = None
            for e in range(2):
                hd = 2 * p + e
                slab = 2 * (hd // (N_HEADS // N_KV_HEADS)) + e
                kx = kcat[:, slab * LANES:(slab + 1) * LANES]
                vx = vcat[:, slab * LANES:(slab + 1) * LANES]
                o = _sink_softmax_pv(_dot_nt(qs, kx), valid, sinks_ref[hd], vx)
                acc = o if acc is None else acc + o
            store(rows, slice(p * LANES, (p + 1) * LANES), _bf16(acc))


SWA_ROWS = N_HEADS * DEC_SEQ
NEW_ROWS = 2 * SUBLANES


def _swa_sample_kernel(q_ref, sink_ref, kn_ref, vn_ref, kt_ref, vt_ref, o_ref, okt_ref, ovt_ref):
    nb = SWA_BB
    rows = nb * SWA_ROWS
    t = lax.broadcasted_iota(jnp.int32, (rows, WINDOW), 0) % DEC_SEQ
    valid_c = lax.broadcasted_iota(jnp.int32, (rows, WINDOW), 1) > t
    valid_n = (lax.broadcasted_iota(jnp.int32, (rows, NEW_ROWS), 1)
               <= lax.broadcasted_iota(jnp.int32, (rows, NEW_ROWS), 0) % DEC_SEQ)
    sink = jnp.concatenate([sink_ref[:, 0:1]] * nb, axis=0)
    qs = [_bf16(q_ref[b]) for b in range(nb)]
    s_c = jnp.concatenate([_dot(qs[b], _bf16(kt_ref[b])) for b in range(nb)], axis=0)
    s_n = jnp.concatenate([_dot_nt(qs[b], _bf16(kn_ref[b])) for b in range(nb)], axis=0)
    s_c = jnp.where(valid_c, s_c, -jnp.inf)
    s_n = jnp.where(valid_n, s_n, -jnp.inf)
    m = jnp.maximum(jnp.maximum(jnp.max(s_c, axis=1, keepdims=True), jnp.max(s_n, axis=1, keepdims=True)), sink)
    p_c = jnp.exp(s_c - m)
    p_n = jnp.exp(s_n - m)
    rden = 1.0 / (jnp.sum(p_c, axis=1, keepdims=True) + jnp.sum(p_n, axis=1, keepdims=True) + jnp.exp(sink - m))
    p_c, p_n = _bf16(p_c), _bf16(p_n)
    lane = lax.broadcasted_iota(jnp.int32, (KV_WIDTH, WINDOW), 1)
    shift = WINDOW - DEC_SEQ
    zrows = jnp.zeros((KV_WIDTH - NEW_ROWS, KV_WIDTH), jnp.float32)
    for b in range(nb):
        r = slice(b * SWA_ROWS, (b + 1) * SWA_ROWS)
        kt, vt = kt_ref[b], vt_ref[b]
        kn, vn = kn_ref[b], vn_ref[b]
        o_ref[b] = (_dot_nt(p_c[r], _bf16(vt)) + _dot(p_n[r], _bf16(vn))) * rden[r]
        for old, new, dst in ((kt, kn, okt_ref), (vt, vn, ovt_ref)):
            new_cols = pltpu.roll(jnp.concatenate([new, zrows], axis=0).T, shift, axis=1)
            dst[b] = jnp.where(lane >= shift, new_cols, pltpu.roll(old, shift, axis=1))


def _swa_sample(sinks, q, kn, vn, cache_k, cache_v):
    nb = cache_k.shape[0]
    bb = SWA_BB
    groups = N_HEADS // N_KV_HEADS
    qh = q.reshape(nb, DEC_SEQ, N_KV_HEADS, groups, HEAD_DIM).transpose(0, 2, 3, 1, 4)
    qh = qh.reshape(nb, N_KV_HEADS, groups * DEC_SEQ, HEAD_DIM)
    zeros = jnp.zeros_like(qh[:, 0])
    qbd = jnp.concatenate([jnp.concatenate([qh[:, 0], zeros], axis=-1),
                           jnp.concatenate([zeros, qh[:, 1]], axis=-1)], axis=1)
    sink_col = jnp.broadcast_to(jnp.repeat(sinks, DEC_SEQ).reshape(SWA_ROWS, 1), (SWA_ROWS, LANES))
    pad8 = lambda a: jnp.pad(a.reshape(nb, DEC_SEQ, KV_WIDTH), ((0, 0), (0, NEW_ROWS - DEC_SEQ), (0, 0)))
    to_t = lambda c: c.transpose(0, 2, 3, 1).reshape(nb, KV_WIDTH, WINDOW)
    blk = lambda r, w: pl.BlockSpec((bb, r, w), lambda i: (i, 0, 0))
    o, okt, ovt = pl.pallas_call(
        _swa_sample_kernel,
        grid=(nb // bb,),
        in_specs=[blk(SWA_ROWS, KV_WIDTH), pl.BlockSpec((SWA_ROWS, LANES), lambda i: (0, 0)),
                  blk(NEW_ROWS, KV_WIDTH), blk(NEW_ROWS, KV_WIDTH), blk(KV_WIDTH, WINDOW), blk(KV_WIDTH, WINDOW)],
        out_specs=[blk(SWA_ROWS, KV_WIDTH), blk(KV_WIDTH, WINDOW), blk(KV_WIDTH, WINDOW)],
        out_shape=[jax.ShapeDtypeStruct((nb, SWA_ROWS, KV_WIDTH), jnp.float32),
                   jax.ShapeDtypeStruct((nb, KV_WIDTH, WINDOW), jnp.float32),
                   jax.ShapeDtypeStruct((nb, KV_WIDTH, WINDOW), jnp.float32)],
        compiler_params=_params(("arbitrary",)),
        name="swa_sample",
    )(qbd, sink_col, pad8(kn), pad8(vn), to_t(cache_k), to_t(cache_v))
    o = o.reshape(nb, N_KV_HEADS, groups, DEC_SEQ, N_KV_HEADS, HEAD_DIM)
    attn = jnp.stack([o[:, h, :, :, h, :] for h in range(N_KV_HEADS)], axis=1)
    attn = attn.transpose(0, 3, 1, 2, 4).reshape(nb * DEC_SEQ, ATTN_WIDTH)
    from_t = lambda c: c.reshape(nb, N_KV_HEADS, HEAD_DIM, WINDOW).transpose(0, 3, 1, 2)
    return attn, from_t(okt), from_t(ovt)


def _mem_kv_kernel(mem_ref, wk_ref, wv_ref, wq_ref, wo_ref, mk_ref, mv_ref, wqk_ref, wvo_ref):
    mb = _bf16(mem_ref[...])
    mk = _dot(mb, wk_ref[...])
    mv = _dot(mb, wv_ref[...])
    mk_ref[...] = mk
    mv_ref[...] = mv
    mkb, mvb = _bf16(mk), _bf16(mv)
    for h in range(MEM_HEADS):
        sl = slice(h * MEM_HEAD_DIM, (h + 1) * MEM_HEAD_DIM)
        keys = slice(h * N_MEM, (h + 1) * N_MEM)
        wqk_ref[:, keys] = _bf16(_dot_nt(wq_ref[:, sl], mkb[:, sl]) * (MEM_HEAD_DIM ** -0.5))
        wvo_ref[keys, :] = _bf16(_dot(mvb[:, sl], wo_ref[sl, :]))


def _mem_kv(mem, wk_b, wv_b, wq_b, wo_b):
    full = lambda a: pl.BlockSpec(a.shape, lambda i: (0,) * a.ndim)
    blk = pl.BlockSpec((N_MEM, D_MODEL), lambda i: (0, 0))
    f32 = jax.ShapeDtypeStruct((N_MEM, D_MODEL), jnp.float32)
    fused = (D_MODEL, MEM_HEADS * N_MEM), (MEM_HEADS * N_MEM, D_MODEL)
    return pl.pallas_call(
        _mem_kv_kernel,
        grid=(1,),
        in_specs=[full(mem), full(wk_b), full(wv_b), full(wq_b), full(wo_b)],
        out_specs=[blk, blk] + [pl.BlockSpec(shp, lambda i: (0, 0)) for shp in fused],
        out_shape=[f32, f32] + [jax.ShapeDtypeStruct(shp, jnp.bfloat16) for shp in fused],
        compiler_params=_params(("arbitrary",)),
        name="mem_kv",
    )(mem, wk_b, wv_b, wq_b, wo_b)


def _mix_ln1(attn_ref, conv_ref, x_ref, wmix_ref, g1_ref, b1_ref):
    mix = _dot(_bf16(attn_ref[...]), wmix_ref[0:ATTN_WIDTH, :]) + _dot(conv_ref[...], wmix_ref[ATTN_WIDTH:, :])
    return _layer_norm(ALPHA * x_ref[...] + mix, g1_ref[...], b1_ref[...])


def _mem_q(h1, wq_ref):
    return _bf16(_dot(_bf16(h1), wq_ref[...]) * (MEM_HEAD_DIM ** -0.5))


def _route(h2, wrhl_ref, br_ref, tri_ref, carry):
    hi = _bf16(h2)
    lo = _bf16(h2 - hi.astype(jnp.float32))
    hh = _dot(hi, wrhl_ref[...])
    logits = hh[:, 0:LANES] + hh[:, LANES:] + _dot(lo, wrhl_ref[:, 0:LANES]) + br_ref[...]
    lane_i = lax.broadcasted_iota(jnp.int32, logits.shape, 1)
    lane = lane_i.astype(jnp.float32)
    big = jnp.float32(LANES)
    is_g = lane_i < N_GROUPS
    gl = jnp.where(is_g, logits, -jnp.inf)
    gmax = jnp.max(gl, axis=1, keepdims=True)
    gidx = jnp.min(jnp.where(is_g & (logits == gmax), lane, big), axis=1, keepdims=True)
    gsum = jnp.sum(jnp.exp(gl - gmax), axis=1, keepdims=True)
    gw = 1.0 / gsum
    eid = lane_i - N_GROUPS
    assert EXPERTS_PER_GROUP == 8
    grp = lax.shift_right_arithmetic(eid, jnp.full_like(eid, 3)).astype(jnp.float32)
    in_e = (lane_i >= N_GROUPS) & (lane_i < N_GROUPS + N_EXPERTS) & (grp == gidx)
    v1 = jnp.max(jnp.where(in_e, logits, -jnp.inf), axis=1, keepdims=True)
    i1 = jnp.min(jnp.where(in_e & (logits == v1), lane, big), axis=1, keepdims=True)
    rest = in_e & (lane != i1)
    v2 = jnp.max(jnp.where(rest, logits, -jnp.inf), axis=1, keepdims=True)
    i2 = jnp.min(jnp.where(rest & (logits == v2), lane, big), axis=1, keepdims=True)
    ex = jnp.exp(v2 - v1)
    den = 1.0 + ex
    w1 = gw / den
    w2 = gw * ex / den
    zero = jnp.zeros_like(logits)
    pick1 = lane == i1
    pick2 = lane == i2
    sel = jnp.where(pick1 | pick2, 1.0, 0.0)
    before = _dot(tri_ref[...], _bf16(sel)) + carry
    rank1 = jnp.sum(jnp.where(pick1, before, zero), axis=1, keepdims=True)
    rank2 = jnp.sum(jnp.where(pick2, before, zero), axis=1, keepdims=True)
    cols = (i1 - N_GROUPS, i2 - N_GROUPS, w1, w2, rank1, rank2)
    route = zero
    for k, col in enumerate(cols):
        route = jnp.where(lane_i == k, col, route)
    return route, carry + jnp.sum(sel, axis=0, keepdims=True)


def _post_tile(attn, conv_ref, x_ref, wmix_ref, g1_ref, b1_ref, wqk_ref, wvo_ref, g2_ref, b2_ref,
               wrhl_ref, br_ref, tri_ref, h2t_ref, rt_ref, cnt_ref, carry_ref):
    mix = _dot(attn, wmix_ref[0:ATTN_WIDTH, :]) + _dot(conv_ref[...], wmix_ref[ATTN_WIDTH:, :])
    h1 = _layer_norm(ALPHA * x_ref[...] + mix, g1_ref[...], b1_ref[...])
    scores = _dot(_bf16(h1), wqk_ref[...])
    probs = []
    for h in range(MEM_HEADS):
        s = scores[:, h * N_MEM:(h + 1) * N_MEM]
        p = jnp.exp(s - jnp.max(s, axis=1, keepdims=True))
        probs.append(_bf16(p / jnp.sum(p, axis=1, keepdims=True)))
    mem_out = _dot(jnp.concatenate(probs, axis=1), wvo_ref[...])
    h2 = _layer_norm(ALPHA * h1 + mem_out, g2_ref[...], b2_ref[...])
    _store_token_tiles(h2t_ref, h2)
    route, carry = _route(h2, wrhl_ref, br_ref, tri_ref, carry_ref[0:1, :])
    rt_ref[...] = route
    carry_ref[...] = jnp.broadcast_to(carry, carry_ref.shape)
    cnt_ref[...] = jnp.broadcast_to(carry, cnt_ref.shape)


def _swa_post_kernel(sinks_ref, q_ref, kc_ref, kp_ref, vc_ref, vp_ref,
                     conv_ref, x_ref, wmix_ref, g1_ref, b1_ref, wqk_ref, wvo_ref, g2_ref, b2_ref,
                     wrhl_ref, br_ref, tri_ref, h2s_ref, rts_ref, cnts_ref,
                     h2t_ref, rt_ref, cnt_ref, carry_ref, attn_s):
    t = pl.program_id(0)
    steps = N_PROMPT // TM_POST
    par = t % 2

    def swa(slot):
        def store(rows, cols, val):
            attn_s[slot, rows, cols] = val
        _swa_tile(t, sinks_ref, q_ref, kc_ref, kp_ref, vc_ref, vp_ref, store)

    def post(slot):
        _post_tile(attn_s[slot], conv_ref, x_ref, wmix_ref, g1_ref, b1_ref, wqk_ref, wvo_ref, g2_ref, b2_ref,
                   wrhl_ref, br_ref, tri_ref, h2t_ref, rt_ref, cnt_ref, carry_ref)

    @pl.when(t == 0)
    def _():
        carry_ref[...] = cnts_ref[...]
        swa(0)

    @pl.when((t >= 1) & (t < steps))
    def _():
        swa(par)
        post(1 - par)

    @pl.when(t == steps)
    def _():
        post(1 - par)

    @pl.when(t == steps + 1)
    def _():
        h2t_ref[...] = h2s_ref[...]
        rt_ref[...] = rts_ref[...]


def _swa_post(sinks, q, kx, vx, conv, x, wmix_b, g1, b1, wqk, wvo, g2, b2, wrhl, br, tri, h2t_s, rt_s, cnt_s):
    n = x.shape[0]
    tm = TM_POST
    assert tm == SWA_QB * WINDOW
    steps = n // tm
    cur = lambda w: pl.BlockSpec((tm, w), lambda i: (jnp.minimum(i, steps - 1), 0))
    prev = lambda w: pl.BlockSpec((WINDOW, w), lambda i: (jnp.clip(SWA_QB * i - 1, 0, n // WINDOW - 1), 0))
    lag = lambda w: pl.BlockSpec((tm, w), lambda i: (jnp.clip(i - 1, 0, steps - 1), 0))
    full = lambda a: pl.BlockSpec(a.shape, lambda i: (0,) * a.ndim)
    weights = (wmix_b, g1, b1, wqk, wvo, g2, b2, wrhl, br, tri, h2t_s, rt_s, cnt_s)
    n_out = n + h2t_s.shape[0] // ROW_CHUNKS
    out_idx = lambda i: (jnp.where(i > steps, steps, jnp.clip(i - 1, 0, steps - 1)), 0)
    return pl.pallas_call(
        _swa_post_kernel,
        grid=(steps + 2,),
        in_specs=([pl.BlockSpec(memory_space=pltpu.SMEM), cur(ATTN_WIDTH),
                   cur(4 * LANES), prev(4 * LANES), cur(4 * LANES), prev(4 * LANES),
                   lag(CONV_CH), lag(D_MODEL)] + [full(a) for a in weights]),
        out_specs=[pl.BlockSpec((tm * ROW_CHUNKS, LANES), out_idx),
                   pl.BlockSpec((tm, LANES), out_idx),
                   pl.BlockSpec((SUBLANES, LANES), lambda i: (0, 0))],
        out_shape=[jax.ShapeDtypeStruct((n_out * ROW_CHUNKS, LANES), jnp.float32),
                   jax.ShapeDtypeStruct((n_out, LANES), jnp.float32),
                   jax.ShapeDtypeStruct((SUBLANES, LANES), jnp.float32)],
        scratch_shapes=[pltpu.VMEM((SUBLANES, LANES), jnp.float32),
                        pltpu.VMEM((2, tm, ATTN_WIDTH), jnp.bfloat16)],
        compiler_params=_params(("arbitrary",)),
        name="swa_post_prompt",
    )(sinks, q, kx, kx, vx, vx, conv, x, *weights)


def _post_a_sample_kernel(attn_ref, conv_ref, x_ref, wmix_ref, g1_ref, b1_ref, wq_ref, h1_ref, qm_ref):
    h1 = _mix_ln1(attn_ref, conv_ref, x_ref, wmix_ref, g1_ref, b1_ref)
    h1_ref[...] = h1
    qm_ref[...] = _mem_q(h1, wq_ref).astype(jnp.float32)


def _post_a_sample(attn, conv, x, wmix_b, g1, b1, wq_b):
    n = x.shape[0]
    args = (attn, conv, x, wmix_b, g1, b1, wq_b)
    full = lambda a: pl.BlockSpec(a.shape, lambda i: (0,) * a.ndim)
    blk = pl.BlockSpec((n, D_MODEL), lambda i: (0, 0))
    return pl.pallas_call(
        _post_a_sample_kernel,
        grid=(1,),
        in_specs=[full(a) for a in args],
        out_specs=[blk, blk],
        out_shape=[jax.ShapeDtypeStruct((n, D_MODEL), jnp.float32),
                   jax.ShapeDtypeStruct((n, D_MODEL), jnp.float32)],
        compiler_params=_params(("arbitrary",)),
        name="post_a_sample",
    )(*args)


MEM_ROWS = MEM_HEADS * DEC_SEQ


def _mem_attn_sample_body(q_ref, mk_ref, mv_ref, o_ref):
    nk = N_MEM * MEM_HEADS
    nb = SAMPLE_BB
    rows = nb * MEM_ROWS
    row_h = (lax.broadcasted_iota(jnp.int32, (rows, nk), 0) % MEM_ROWS) // DEC_SEQ
    key_h = lax.broadcasted_iota(jnp.int32, (rows, nk), 1) % MEM_HEADS
    s = jnp.concatenate(
        [_dot_nt(_bf16(q_ref[b]), _bf16(mk_ref[b].reshape(nk, MEM_HEAD_DIM))) for b in range(nb)], axis=0)
    s = jnp.where(row_h == key_h, s, -jnp.inf)
    p = jnp.exp(s - jnp.max(s, axis=1, keepdims=True))
    rden = 1.0 / jnp.sum(p, axis=1, keepdims=True)
    p = _bf16(p)
    for b in range(nb):
        r = slice(b * MEM_ROWS, (b + 1) * MEM_ROWS)
        o_ref[b] = _dot(p[r], _bf16(mv_ref[b].reshape(nk, MEM_HEAD_DIM))) * rden[r]


def _proj_mem_kernel(x_ref, w_ref, tab_ref, cw_ref, mq_ref, mk_ref, mv_ref,
                     q_ref, kx_ref, vx_ref, conv_ref, ktail_ref, vtail_ref, utail_ref, mo_ref, carry_ref):
    _mem_attn_sample_body(mq_ref, mk_ref, mv_ref, mo_ref)
    _proj_prompt_body(x_ref, w_ref, tab_ref, cw_ref,
                      q_ref, kx_ref, vx_ref, conv_ref, ktail_ref, vtail_ref, utail_ref, carry_ref)


def _proj_prompt_mem_sample(x, w_in_b, tab, conv_w, qm, mk, mv):
    n = x.shape[0]
    nb = mk.shape[0]
    bb = SAMPLE_BB
    steps = nb // bb
    tm = n // steps
    row = lambda w: pl.BlockSpec((tm, w), lambda i: (i, 0))
    full = lambda a: pl.BlockSpec(a.shape, lambda i: (0,) * a.ndim)
    const = lambda r, w: pl.BlockSpec((r, w), lambda i: (0, 0))
    mq = qm.reshape(nb, DEC_SEQ, MEM_HEADS, MEM_HEAD_DIM).transpose(0, 2, 1, 3).reshape(nb, MEM_ROWS, MEM_HEAD_DIM)
    mrows = pl.BlockSpec((bb, MEM_ROWS, MEM_HEAD_DIM), lambda i: (i, 0, 0))
    kv = pl.BlockSpec((bb, N_MEM, MEM_HEADS, MEM_HEAD_DIM), lambda i: (i, 0, 0, 0))
    outs = pl.pallas_call(
        _proj_mem_kernel,
        grid=(steps,),
        in_specs=[row(D_MODEL), full(w_in_b), pl.BlockSpec((ROPE_ROWS, tm), lambda i: (0, i)), full(conv_w),
                  mrows, kv, kv],
        out_specs=[row(ATTN_WIDTH), row(4 * LANES), row(4 * LANES), row(CONV_CH),
                   const(WINDOW, KV_WIDTH), const(WINDOW, KV_WIDTH), const(SUBLANES, CONV_CH), mrows],
        out_shape=[jax.ShapeDtypeStruct((n, ATTN_WIDTH), jnp.bfloat16),
                   jax.ShapeDtypeStruct((n, 4 * LANES), jnp.bfloat16),
                   jax.ShapeDtypeStruct((n, 4 * LANES), jnp.bfloat16),
                   jax.ShapeDtypeStruct((n, CONV_CH), jnp.bfloat16),
                   jax.ShapeDtypeStruct((WINDOW, KV_WIDTH), jnp.float32),
                   jax.ShapeDtypeStruct((WINDOW, KV_WIDTH), jnp.float32),
                   jax.ShapeDtypeStruct((SUBLANES, CONV_CH), jnp.float32),
                   jax.ShapeDtypeStruct((nb, MEM_ROWS, MEM_HEAD_DIM), jnp.float32)],
        scratch_shapes=[pltpu.VMEM((SUBLANES, CONV_CH), jnp.float32)],
        compiler_params=_params(("arbitrary",)),
        name="proj_prompt_mem_sample",
    )(x, w_in_b, tab, conv_w, mq, mk, mv)
    o = outs[7].reshape(nb, MEM_HEADS, DEC_SEQ, MEM_HEAD_DIM).transpose(0, 2, 1, 3).reshape(nb * DEC_SEQ, D_MODEL)
    return outs[:7], o


def _post_b_sample_kernel(o_ref, h1_ref, wo_ref, g2_ref, b2_ref, wrhl_ref, br_ref, tri_ref,
                          h2t_ref, rt_ref, cnt_ref):
    h2 = _layer_norm(ALPHA * h1_ref[...] + _dot(_bf16(o_ref[...]), wo_ref[...]), g2_ref[...], b2_ref[...])
    _store_token_tiles(h2t_ref, h2)
    route, carry = _route(h2, wrhl_ref, br_ref, tri_ref, jnp.zeros((1, LANES), jnp.float32))
    rt_ref[...] = route
    cnt_ref[...] = jnp.broadcast_to(carry, cnt_ref.shape)


def _post_b_sample(o, h1, wo_b, g2, b2, wrhl, br, tri):
    n = h1.shape[0]
    args = (o, h1, wo_b, g2, b2, wrhl, br, tri)
    full = lambda a: pl.BlockSpec(a.shape, lambda i: (0,) * a.ndim)
    return pl.pallas_call(
        _post_b_sample_kernel,
        grid=(1,),
        in_specs=[full(a) for a in args],
        out_specs=[pl.BlockSpec((n * ROW_CHUNKS, LANES), lambda i: (0, 0)),
                   pl.BlockSpec((n, LANES), lambda i: (0, 0)),
                   pl.BlockSpec((SUBLANES, LANES), lambda i: (0, 0))],
        out_shape=[jax.ShapeDtypeStruct((n * ROW_CHUNKS, LANES), jnp.float32),
                   jax.ShapeDtypeStruct((n, LANES), jnp.float32),
                   jax.ShapeDtypeStruct((SUBLANES, LANES), jnp.float32)],
        compiler_params=_params(("arbitrary",)),
        name="post_b_sample",
    )(*args)


def _row_gather_copy(src_hbm, idx, dst, dst_row, sem):
    s0 = pl.multiple_of(idx * ROW_CHUNKS, ROW_CHUNKS)
    d0 = pl.multiple_of(dst_row * ROW_CHUNKS, ROW_CHUNKS)
    return pltpu.make_async_copy(src_hbm.at[pl.ds(s0, ROW_CHUNKS), :], dst.at[pl.ds(d0, ROW_CHUNKS), :], sem)


def _dispatch_kernel(pos_ref, h2t_ref, xs_hbm, sem):
    def body(r, c):
        src = h2t_ref.at[pl.ds(pl.multiple_of(r * ROW_CHUNKS, ROW_CHUNKS), ROW_CHUNKS), :]
        for k in range(2):
            d0 = pl.multiple_of(pos_ref[0, 0, k * TM_COMB + r] * ROW_CHUNKS, ROW_CHUNKS)
            pltpu.make_async_copy(src, xs_hbm.at[pl.ds(d0, ROW_CHUNKS), :], sem.at[0]).start(priority=k)
        return c
    lax.fori_loop(0, TM_COMB, body, 0, unroll=8)
    for _ in range(2):
        pltpu.make_async_copy(h2t_ref, xs_hbm.at[pl.ds(0, TM_COMB * ROW_CHUNKS), :], sem.at[0]).wait()


def _dispatch(pos3, h2t):
    nt = N_ALL // TM_COMB
    return pl.pallas_call(
        _dispatch_kernel,
        grid=(nt,),
        in_specs=[pl.BlockSpec((1, 1, 2 * TM_COMB), lambda i: (i, 0, 0), memory_space=pltpu.SMEM),
                  pl.BlockSpec((TM_COMB * ROW_CHUNKS, LANES), lambda i: (i, 0))],
        out_specs=pl.BlockSpec(memory_space=pl.ANY),
        out_shape=jax.ShapeDtypeStruct((N_ASSIGN * ROW_CHUNKS, LANES), jnp.float32),
        scratch_shapes=[pltpu.SemaphoreType.DMA((1,))],
        compiler_params=_params(("arbitrary",)),
        name="moe_dispatch",
    )(pos3, h2t)


def _moe_ffn_kernel(it_ref, ie_ref, lo_ref, hi_ref, x_ref, wg_ref, wu_ref, wd_ref, y_ref, wgb, wub, wdb, cur_e):
    i = pl.program_id(0)
    lo = lo_ref[i]
    hi = hi_ref[i]
    e = ie_ref[i]

    @pl.when(i == 0)
    def _():
        cur_e[0] = -1

    @pl.when((hi > lo) & (cur_e[0] != e))
    def _():
        wgb[...] = _bf16(wg_ref[0])
        wub[...] = _bf16(wu_ref[0])
        wdb[...] = _bf16(wd_ref[0])
        cur_e[0] = e

    def ffn(r0, rows):
        x = _bf16(_load_token_tiles(x_ref, r0 * ROW_CHUNKS, rows))
        hg = _dot(x, wgb[...])
        hu = _dot(x, wub[...])
        h = hg / (1.0 + jnp.exp(-hg)) * hu
        return _dot(_bf16(h), wdb[...])

    whole = (lo == 0) & (hi == TM_MOE)

    @pl.when(whole)
    def _():
        _store_token_tiles(y_ref, ffn(0, TM_MOE))

    for r0 in range(0, TM_MOE, MOE_BLOCK):
        live = (hi > lo) & jnp.logical_not(whole) & (hi > r0) & (lo < r0 + MOE_BLOCK)

        def store(merge, r0=r0):
            y = ffn(r0, MOE_BLOCK)
            row = r0 + lax.broadcasted_iota(jnp.int32, (MOE_BLOCK, LANES), 0)
            mask = (row >= lo) & (row < hi)
            for c in range(ROW_CHUNKS):
                sl = pl.ds(r0 * ROW_CHUNKS + c, MOE_BLOCK, stride=ROW_CHUNKS)
                y_ref[sl, :] = jnp.where(mask, y[:, c * LANES:(c + 1) * LANES], y_ref[sl, :] if merge else 0.0)

        pl.when(live & (lo <= r0))(functools.partial(store, False))
        pl.when(live & (lo > r0))(functools.partial(store, True))


def _moe_ffn(item_tile, item_expert, item_lo, item_hi, x_sorted, w_gate, w_up, w_down):
    wspec = lambda shp: pl.BlockSpec((1,) + shp, lambda i, it, ie, lo, hi: (ie[i], 0, 0))
    tile = pl.BlockSpec((TM_MOE * ROW_CHUNKS, LANES), lambda i, it, ie, lo, hi: (it[i], 0))
    grid_spec = pltpu.PrefetchScalarGridSpec(
        num_scalar_prefetch=4,
        grid=(MOE_ITEMS,),
        in_specs=[tile, wspec((D_MODEL, EXPERT_FF)), wspec((D_MODEL, EXPERT_FF)), wspec((EXPERT_FF, D_MODEL))],
        out_specs=tile,
        scratch_shapes=[pltpu.VMEM((D_MODEL, EXPERT_FF), jnp.bfloat16),
                        pltpu.VMEM((D_MODEL, EXPERT_FF), jnp.bfloat16),
                        pltpu.VMEM((EXPERT_FF, D_MODEL), jnp.bfloat16),
                        pltpu.SMEM((1,), jnp.int32)],
    )
    return pl.pallas_call(
        _moe_ffn_kernel,
        grid_spec=grid_spec,
        out_shape=jax.ShapeDtypeStruct((N_ASSIGN * ROW_CHUNKS, LANES), jnp.float32),
        compiler_params=_params(("arbitrary",)),
        name="moe_ffn",
    )(item_tile, item_expert, item_lo, item_hi, x_sorted, w_gate, w_up, w_down)


def _combine_kernel(nt, pos_cur_ref, pos_nxt_ref, yt_hbm, h2t_ref, rt_ref, g3_ref, b3_ref, o_ref, abuf, sem):
    t = pl.program_id(0)
    slot = t % 2
    rows = 2 * TM_COMB

    def issue(pos_ref, s):
        def body(j, c):
            for k in range(2):
                r = 2 * j + k
                _row_gather_copy(yt_hbm, pos_ref[0, 0, r], abuf, s * rows + r, sem.at[s]).start(priority=k)
            return c
        lax.fori_loop(0, rows // 2, body, 0, unroll=16)

    @pl.when(t == 0)
    def _():
        issue(pos_cur_ref, 0)

    @pl.when(t + 1 < nt)
    def _():
        issue(pos_nxt_ref, 1 - slot)

    base = pl.multiple_of(slot * (rows * ROW_CHUNKS), rows * ROW_CHUNKS)
    pltpu.make_async_copy(yt_hbm.at[pl.ds(0, rows * ROW_CHUNKS), :],
                          abuf.at[pl.ds(base, rows * ROW_CHUNKS), :], sem.at[slot]).wait()
    ya = _load_token_tiles(abuf, base, TM_COMB)
    yb = _load_token_tiles(abuf, base + TM_COMB * ROW_CHUNKS, TM_COMB)
    rt = rt_ref[...]
    ff = rt[:, 2:3] * ya + rt[:, 3:4] * yb
    h2 = _load_token_tiles(h2t_ref, 0, TM_COMB)
    o_ref[...] = _layer_norm(ALPHA * h2 + ff, g3_ref[...], b3_ref[...])


def _combine(pos3, yt, h2t, rt, g3, b3, tile0, n_tiles):
    last = tile0 + n_tiles - 1
    smem_pos = lambda f: pl.BlockSpec((1, 1, 2 * TM_COMB), f, memory_space=pltpu.SMEM)
    full = lambda a: pl.BlockSpec(a.shape, lambda i: (0,) * a.ndim)
    return pl.pallas_call(
        functools.partial(_combine_kernel, n_tiles),
        grid=(n_tiles,),
        in_specs=[smem_pos(lambda i: (tile0 + i, 0, 0)),
                  smem_pos(lambda i: (jnp.minimum(tile0 + i + 1, last), 0, 0)),
                  pl.BlockSpec(memory_space=pl.ANY),
                  pl.BlockSpec((TM_COMB * ROW_CHUNKS, LANES), lambda i: (tile0 + i, 0)),
                  pl.BlockSpec((TM_COMB, LANES), lambda i: (tile0 + i, 0)),
                  full(g3), full(b3)],
        out_specs=pl.BlockSpec((TM_COMB, D_MODEL), lambda i: (i, 0)),
        out_shape=jax.ShapeDtypeStruct((n_tiles * TM_COMB, D_MODEL), jnp.float32),
        scratch_shapes=[pltpu.VMEM((2 * 2 * TM_COMB * ROW_CHUNKS, LANES), jnp.float32),
                        pltpu.SemaphoreType.DMA((2,))],
        compiler_params=_params(("arbitrary",)),
        name="moe_combine",
    )(pos3, pos3, yt, h2t, rt, g3, b3)


POS_TILES = 11


def _positions_kernel(rt_ref, starts_ref, pos_ref):
    expert = lax.broadcasted_iota(jnp.int32, (N_EXPERTS, TM_COMB), 0).astype(jnp.float32)
    starts = jnp.concatenate([starts_ref[...]] * (TM_COMB // LANES), axis=1)
    for j in range(POS_TILES):
        cols = rt_ref[j * TM_COMB:(j + 1) * TM_COMB, :].T
        out = []
        for k in range(2):
            seg = jnp.sum(jnp.where(expert == cols[k:k + 1, :], starts, 0.0), axis=0, keepdims=True)
            out.append(seg + cols[4 + k:5 + k, :])
        pos_ref[j] = jnp.concatenate(out, axis=1).astype(jnp.int32)


def _positions(rt, starts_rep):
    nt = N_ALL // TM_COMB
    assert nt % POS_TILES == 0
    return pl.pallas_call(
        _positions_kernel,
        grid=(nt // POS_TILES,),
        in_specs=[pl.BlockSpec((POS_TILES * TM_COMB, LANES), lambda i: (i, 0)),
                  pl.BlockSpec((N_EXPERTS, LANES), lambda i: (0, 0))],
        out_specs=pl.BlockSpec((POS_TILES, 1, 2 * TM_COMB), lambda i: (i, 0, 0)),
        out_shape=jax.ShapeDtypeStruct((nt, 1, 2 * TM_COMB), jnp.int32),
        compiler_params=_params(("arbitrary",)),
        name="moe_positions",
    )(rt, starts_rep)


def _routing_plan(rt, cnt):
    i32 = jnp.int32
    counts_f = cnt[0, N_GROUPS:N_GROUPS + N_EXPERTS]
    starts_f = jnp.cumsum(counts_f) - counts_f
    pos3 = _positions(rt, jnp.broadcast_to(starts_f[:, None], (N_EXPERTS, LANES)))
    starts = starts_f.astype(i32)
    tiles = jnp.arange(MOE_TILES, dtype=i32) * TM_MOE
    rank_t = jnp.arange(MOE_TILES, dtype=i32) + jnp.sum((starts[None, :] < tiles[:, None]).astype(i32), axis=1)
    rank_s = jnp.arange(N_EXPERTS, dtype=i32) + jnp.sum((tiles[None, :] <= starts[:, None]).astype(i32), axis=1)
    vals = jnp.concatenate([tiles, starts])
    ranks = jnp.concatenate([rank_t, rank_s])
    slot = jnp.arange(MOE_ITEMS, dtype=i32)
    lo = jnp.sum(jnp.where(ranks[None, :] == slot[:, None], vals[None, :], 0), axis=1)
    hi = jnp.concatenate([lo[1:], jnp.full((1,), N_ASSIGN, i32)])
    item_tile = jnp.minimum(lo // TM_MOE, MOE_TILES - 1)
    item_expert = jnp.clip(jnp.sum((starts[None, :] <= lo[:, None]).astype(i32), axis=1) - 1, 0, N_EXPERTS - 1)
    base = item_tile * TM_MOE
    return item_tile, item_expert, lo - base, hi - base, pos3


def kernel(x_prompt, x_sample, mem_prompt, cache_swa_k, cache_swa_v, cache_conv, cache_mem_k, cache_mem_v,
           w_in, sinks, conv_w, w_mix_out, ln1_g, ln1_b, w_q_mem, w_k_mem, w_v_mem, w_o_mem, ln2_g, ln2_b,
           w_router_group, b_router_group, w_router_expert, b_router_expert, w_gate, w_up, w_down,
           ln3_g, ln3_b):
    f32 = jnp.float32
    row = lambda a: a.reshape(1, -1).astype(f32)
    w_in_b, wmix_b, wq_b, wk_b, wv_b, wo_b = (_bf16(w) for w in (w_in, w_mix_out, w_q_mem, w_k_mem, w_v_mem, w_o_mem))
    g1, b1, g2, b2, g3, b3 = (row(a) for a in (ln1_g, ln1_b, ln2_g, ln2_b, ln3_g, ln3_b))
    pad = LANES - N_GROUPS - N_EXPERTS
    wr = jnp.concatenate([w_router_group, w_router_expert, jnp.zeros((D_MODEL, pad), f32)], axis=1)
    wrh = _bf16(wr)
    wrhl = jnp.concatenate([wrh, _bf16(wr - wrh.astype(f32))], axis=1)
    br = jnp.concatenate([b_router_group, b_router_expert, jnp.zeros((pad,), f32)]).reshape(1, LANES)

    xs = x_sample.reshape(N_SAMPLE, D_MODEL)
    tab_s = jnp.tile(_rope_table(PAST_LEN + jnp.arange(DEC_SEQ)), (1, DEC_BATCH))
    c0 = jnp.repeat(cache_conv[:, 0], DEC_SEQ, axis=0)
    c1 = jnp.repeat(cache_conv[:, 1], DEC_SEQ, axis=0)
    q_s, k_s, v_s, conv_s, u_s = _proj_sample(xs, w_in_b, tab_s, conv_w, c0, c1)
    attn_s, swa_k_s, swa_v_s = _swa_sample(sinks, q_s, k_s, v_s, cache_swa_k, cache_swa_v)
    h1_s, qm_s = _post_a_sample(attn_s, conv_s, xs, wmix_b, g1, b1, wq_b)
    xp = x_prompt.reshape(N_PROMPT, D_MODEL)
    tab_p = _rope_table(jnp.arange(N_PROMPT))
    (q_p, kx_p, vx_p, conv_p, k_tail, v_tail, u_tail), o_s = _proj_prompt_mem_sample(
        xp, w_in_b, tab_p, conv_w, qm_s, cache_mem_k, cache_mem_v)
    tri = _bf16(jnp.tril(jnp.ones((TM_POST, TM_POST), f32), -1))
    h2t_s, rt_s, cnt_s = _post_b_sample(o_s, h1_s, wo_b, g2, b2, wrhl, br, tri)

    mk, mv, wqk, wvo = _mem_kv(mem_prompt.reshape(N_MEM, D_MODEL), wk_b, wv_b, wq_b, wo_b)
    h2t, rt, cnt = _swa_post(sinks, q_p, kx_p, vx_p, conv_p, xp, wmix_b, g1, b1, wqk, wvo, g2, b2,
                             wrhl, br, tri, h2t_s, rt_s, cnt_s)

    item_tile, item_expert, item_lo, item_hi, pos3 = _routing_plan(rt, cnt)
    x_sorted = _dispatch(pos3, h2t)
    yt = _moe_ffn(item_tile, item_expert, item_lo, item_hi, x_sorted, w_gate, w_up, w_down)
    y_p = _combine(pos3, yt, h2t, rt, g3, b3, 0, N_PROMPT // TM_COMB)
    y_s = _combine(pos3, yt, h2t, rt, g3, b3, N_PROMPT // TM_COMB, N_SAMPLE // TM_COMB)

    return (y_p.reshape(1, SEQ, D_MODEL),
            y_s.reshape(DEC_BATCH, DEC_SEQ, D_MODEL),
            k_tail.reshape(1, WINDOW, N_KV_HEADS, HEAD_DIM),
            v_tail.reshape(1, WINDOW, N_KV_HEADS, HEAD_DIM),
            u_tail[SUBLANES - (CONV_K - 1):].reshape(1, CONV_K - 1, CONV_CH),
            mk.reshape(1, N_MEM, MEM_HEADS, MEM_HEAD_DIM),
            mv.reshape(1, N_MEM, MEM_HEADS, MEM_HEAD_DIM),
            swa_k_s.reshape(DEC_BATCH, WINDOW, N_KV_HEADS, HEAD_DIM),
            swa_v_s.reshape(DEC_BATCH, WINDOW, N_KV_HEADS, HEAD_DIM),
            u_s.reshape(DEC_BATCH, DEC_SEQ, CONV_CH)[:, DEC_SEQ - (CONV_K - 1):])
```

```python
import functools

import jax
import jax.numpy as jnp
from jax import lax
from jax.experimental import pallas as pl
from jax.experimental.pallas import tpu as pltpu

D_MODEL = 1024
SEQ = 16384
DEC_BATCH = 128
DEC_SEQ = 4
PAST_LEN = 16384
ATTN_WIDTH = 512
CONV_CH = 512
HEAD_DIM = 64
N_HEADS = 8
N_KV_HEADS = 2
KV_WIDTH = 128
WINDOW = 128
ROPE_THETA = 500000.0
ROPE_DIM = 16
CONV_K = 3
Q_END = ATTN_WIDTH
K_END = Q_END + KV_WIDTH
V_END = K_END + KV_WIDTH
B_END = V_END + CONV_CH
C_END = B_END + CONV_CH
IN_WIDTH = C_END + CONV_CH
N_MEM = 256
MEM_HEADS = 4
MEM_HEAD_DIM = 256
N_GROUPS = 4
EXPERTS_PER_GROUP = 8
N_EXPERTS = 32
EXPERT_FF = 256
ALPHA = 2.0 ** 0.25
LN_EPS = 1e-5

LANES = 128
SUBLANES = 8
ROW_CHUNKS = D_MODEL // LANES
VMEM_LIMIT = 56 * 1024 * 1024

N_PROMPT = SEQ
N_SAMPLE = DEC_BATCH * DEC_SEQ
N_ALL = N_PROMPT + N_SAMPLE
TM_POST = 512
TM_MOE = 512
TM_COMB = 512
MOE_BLOCK = 256
N_ASSIGN = 2 * N_ALL
MOE_TILES = N_ASSIGN // TM_MOE
MOE_ITEMS = MOE_TILES + N_EXPERTS
SAMPLE_BB = 4
SWA_BB = 16

assert ROW_CHUNKS == SUBLANES
assert N_SAMPLE == TM_POST
assert N_ASSIGN % TM_MOE == 0 and N_ALL % TM_COMB == 0


def _params(sem, vmem=VMEM_LIMIT):
    return pltpu.CompilerParams(dimension_semantics=sem, vmem_limit_bytes=vmem)


def _bf16(x):
    return x.astype(jnp.bfloat16)


def _dot(a, b):
    return jnp.dot(a, b, preferred_element_type=jnp.float32)


def _dot_nt(a, b):
    return lax.dot_general(a, b, (((1,), (1,)), ((), ())), preferred_element_type=jnp.float32)


def _layer_norm(x, g, b):
    mu = jnp.mean(x, axis=-1, keepdims=True)
    xc = x - mu
    var = jnp.mean(xc * xc, axis=-1, keepdims=True)
    return xc * lax.rsqrt(var + LN_EPS) * g + b


def _rope(x, cos_t, sin_t):
    lane = lax.broadcasted_iota(jnp.int32, x.shape, 1) % HEAD_DIM
    half = ROPE_DIM // 2
    partner = jnp.where(lane < half, pltpu.roll(x, LANES - half, axis=1), pltpu.roll(x, half, axis=1))
    return x * cos_t + partner * sin_t


def _head_slabs(x):
    lane = lax.broadcasted_iota(jnp.int32, x.shape, 1)
    lo = lane < HEAD_DIM
    sw = pltpu.roll(x, HEAD_DIM, axis=1)
    zero = jnp.zeros_like(x)
    slabs = [jnp.where(lo, x, zero), jnp.where(lo, zero, sw), jnp.where(lo, sw, zero), jnp.where(lo, zero, x)]
    return _bf16(jnp.concatenate(slabs, axis=1))


def _store_token_tiles(ref, val):
    rows = val.shape[0]
    for c in range(ROW_CHUNKS):
        ref[pl.ds(c, rows, stride=ROW_CHUNKS), :] = val[:, c * LANES:(c + 1) * LANES]


def _load_token_tiles(ref, base, rows):
    return jnp.concatenate(
        [ref[pl.ds(base + c, rows, stride=ROW_CHUNKS), :] for c in range(ROW_CHUNKS)], axis=1)


ROPE_ONE = 3 * (ROPE_DIM // 2)
ROPE_ROWS = 32


def _rope_patterns(tab):
    half = ROPE_DIM // 2
    m = lax.broadcasted_iota(jnp.int32, tab.shape, 1) % HEAD_DIM
    idx_c = jnp.where(m < ROPE_DIM, m % half, ROPE_ONE)
    idx_s = jnp.where(m < half, 2 * half + m, jnp.where(m < ROPE_DIM, m, ROPE_ONE + 1))
    return jnp.take_along_axis(tab, idx_c, axis=1), jnp.take_along_axis(tab, idx_s, axis=1)


def _proj_common(x_ref, w_ref, tab_ref):
    xb = _bf16(x_ref[...])
    tab = tab_ref[...]
    pad = jnp.zeros((LANES - tab.shape[0], tab.shape[1]), jnp.float32)
    cos_t, sin_t = _rope_patterns(jnp.concatenate([tab, pad], axis=0).T)
    q = _dot(xb, w_ref[:, 0:Q_END])
    q_rot = jnp.concatenate(
        [_rope(q[:, p * LANES:(p + 1) * LANES], cos_t, sin_t) for p in range(ATTN_WIDTH // LANES)], axis=1)
    q_out = _bf16(q_rot * (HEAD_DIM ** -0.5))
    kv = _dot(xb, w_ref[:, Q_END:V_END])
    k = _rope(kv[:, 0:KV_WIDTH], cos_t, sin_t)
    v = kv[:, KV_WIDTH:]
    bg = _dot(xb, w_ref[:, V_END:B_END])
    u = _dot(xb, w_ref[:, B_END:C_END]) * _dot(xb, w_ref[:, C_END:IN_WIDTH])
    return q_out, k, v, bg, u


def _conv3(bg, u, u1, u2, cw_ref):
    cw = cw_ref[...]
    return bg * (cw[0:1, :] * u2 + cw[1:2, :] * u1 + cw[2:3, :] * u)


def _proj_prompt_body(x_ref, w_ref, tab_ref, cw_ref,
                      q_ref, kx_ref, vx_ref, conv_ref, ktail_ref, vtail_ref, utail_ref, carry_ref):
    @pl.when(pl.program_id(0) == 0)
    def _():
        carry_ref[...] = jnp.zeros_like(carry_ref)

    q_out, k, v, bg, u = _proj_common(x_ref, w_ref, tab_ref)
    tm = u.shape[0]
    ext = jnp.concatenate([carry_ref[...], u], axis=0)
    u1 = pltpu.roll(ext, 1, axis=0)[SUBLANES:SUBLANES + tm]
    u2 = pltpu.roll(ext, 2, axis=0)[SUBLANES:SUBLANES + tm]
    q_ref[...] = q_out
    kx_ref[...] = _head_slabs(k)
    vx_ref[...] = _head_slabs(v)
    conv_ref[...] = _bf16(_conv3(bg, u, u1, u2, cw_ref))
    ktail_ref[...] = k[tm - WINDOW:tm]
    vtail_ref[...] = v[tm - WINDOW:tm]
    utail_ref[...] = u[tm - SUBLANES:tm]
    carry_ref[...] = u[tm - SUBLANES:tm]


def _proj_sample_kernel(x_ref, w_ref, tab_ref, cw_ref, c0_ref, c1_ref,
                        q_ref, k_ref, v_ref, conv_ref, u_ref):
    q_out, k, v, bg, u = _proj_common(x_ref, w_ref, tab_ref)
    t = lax.broadcasted_iota(jnp.int32, u.shape, 0) % DEC_SEQ
    c0 = c0_ref[...]
    c1 = c1_ref[...]
    u1 = jnp.where(t >= 1, pltpu.roll(u, 1, axis=0), c1)
    u2 = jnp.where(t >= 2, pltpu.roll(u, 2, axis=0), jnp.where(t == 1, c1, c0))
    q_ref[...] = q_out.astype(jnp.float32)
    k_ref[...] = k
    v_ref[...] = v
    conv_ref[...] = _bf16(_conv3(bg, u, u1, u2, cw_ref))
    u_ref[...] = u


def _rope_table(pos):
    half = ROPE_DIM // 2
    inv = ROPE_THETA ** (-jnp.arange(0, ROPE_DIM, 2, dtype=jnp.float32) / ROPE_DIM)
    ang = pos.astype(jnp.float32)[None, :] * inv[:, None]
    cos, sin = jnp.cos(ang), jnp.sin(ang)
    n = pos.shape[0]
    assert ROPE_ONE == 3 * half
    return jnp.concatenate([cos, sin, -sin, jnp.ones((1, n), jnp.float32),
                            jnp.zeros((ROPE_ROWS - ROPE_ONE - 1, n), jnp.float32)], axis=0)


def _proj_sample(x, w_in_b, tab, conv_w, c0, c1):
    n = x.shape[0]
    full = lambda a: pl.BlockSpec(a.shape, lambda i: (0,) * a.ndim)
    out = lambda w, dt: jax.ShapeDtypeStruct((n, w), dt)
    blk = lambda w: pl.BlockSpec((n, w), lambda i: (0, 0))
    return pl.pallas_call(
        _proj_sample_kernel,
        grid=(1,),
        in_specs=[full(x), full(w_in_b), full(tab), full(conv_w), full(c0), full(c1)],
        out_specs=[blk(ATTN_WIDTH), blk(KV_WIDTH), blk(KV_WIDTH), blk(CONV_CH), blk(CONV_CH)],
        out_shape=[out(ATTN_WIDTH, jnp.float32), out(KV_WIDTH, jnp.float32), out(KV_WIDTH, jnp.float32),
                   out(CONV_CH, jnp.bfloat16), out(CONV_CH, jnp.float32)],
        compiler_params=_params(("arbitrary",)),
        name="proj_sample",
    )(x, w_in_b, tab, conv_w, c0, c1)


def _sink_softmax_pv(s, valid, sink, vx):
    s = jnp.where(valid, s, -jnp.inf)
    m = jnp.maximum(jnp.max(s, axis=1, keepdims=True), sink)
    p = jnp.exp(s - m)
    den = jnp.sum(p, axis=1, keepdims=True) + jnp.exp(sink - m)
    return _dot(_bf16(p), vx) / den


SWA_QB = 4


def _swa_tile(step, sinks_ref, q_ref, kc_ref, kp_ref, vc_ref, vp_ref, store):
    kall = jnp.concatenate([kp_ref[...], kc_ref[...]], axis=0)
    vall = jnp.concatenate([vp_ref[...], vc_ref[...]], axis=0)
    i = lax.broadcasted_iota(jnp.int32, (WINDOW, 2 * WINDOW), 0)
    j = lax.broadcasted_iota(jnp.int32, (WINDOW, 2 * WINDOW), 1)
    band = (j > i) & (j <= i + WINDOW)
    for sb in range(SWA_QB):
        rows = slice(sb * WINDOW, (sb + 1) * WINDOW)
        kcat = kall[sb * WINDOW:(sb + 2) * WINDOW]
        vcat = vall[sb * WINDOW:(sb + 2) * WINDOW]
        valid = band & ((step > 0) | (j >= WINDOW)) if sb == 0 else band
        for p in range(N_HEADS // 2):
            qs = q_ref[rows, p * LANES:(p + 1) * LANES]
            acc = None
            for e in range(2):
                hd = 2 * p + e
                slab = 2 * (hd // (N_HEADS // N_KV_HEADS)) + e
                kx = kcat[:, slab * LANES:(slab + 1) * LANES]
                vx = vcat[:, slab * LANES:(slab + 1) * LANES]
                o = _sink_softmax_pv(_dot_nt(qs, kx), valid, sinks_ref[hd], vx)
                acc = o if acc is None else acc + o
            store(rows, slice(p * LANES, (p + 1) * LANES), _bf16(acc))


SWA_ROWS = N_HEADS * DEC_SEQ
NEW_ROWS = 2 * SUBLANES


def _swa_sample_kernel(q_ref, sink_ref, kn_ref, vn_ref, kt_ref, vt_ref, o_ref, okt_ref, ovt_ref):
    nb = SWA_BB
    rows = nb * SWA_ROWS
    t = lax.broadcasted_iota(jnp.int32, (rows, WINDOW), 0) % DEC_SEQ
    valid_c = lax.broadcasted_iota(jnp.int32, (rows, WINDOW), 1) > t
    valid_n = (lax.broadcasted_iota(jnp.int32, (rows, NEW_ROWS), 1)
               <= lax.broadcasted_iota(jnp.int32, (rows, NEW_ROWS), 0) % DEC_SEQ)
    sink = jnp.concatenate([sink_ref[:, 0:1]] * nb, axis=0)
    qs = [_bf16(q_ref[b]) for b in range(nb)]
    s_c = jnp.concatenate([_dot(qs[b], _bf16(kt_ref[b])) for b in range(nb)], axis=0)
    s_n = jnp.concatenate([_dot_nt(qs[b], _bf16(kn_ref[b])) for b in range(nb)], axis=0)
    s_c = jnp.where(valid_c, s_c, -jnp.inf)
    s_n = jnp.where(valid_n, s_n, -jnp.inf)
    m = jnp.maximum(jnp.maximum(jnp.max(s_c, axis=1, keepdims=True), jnp.max(s_n, axis=1, keepdims=True)), sink)
    p_c = jnp.exp(s_c - m)
    p_n = jnp.exp(s_n - m)
    rden = 1.0 / (jnp.sum(p_c, axis=1, keepdims=True) + jnp.sum(p_n, axis=1, keepdims=True) + jnp.exp(sink - m))
    p_c, p_n = _bf16(p_c), _bf16(p_n)
    lane = lax.broadcasted_iota(jnp.int32, (KV_WIDTH, WINDOW), 1)
    shift = WINDOW - DEC_SEQ
    zrows = jnp.zeros((KV_WIDTH - NEW_ROWS, KV_WIDTH), jnp.float32)
    for b in range(nb):
        r = slice(b * SWA_ROWS, (b + 1) * SWA_ROWS)
        kt, vt = kt_ref[b], vt_ref[b]
        kn, vn = kn_ref[b], vn_ref[b]
        o_ref[b] = (_dot_nt(p_c[r], _bf16(vt)) + _dot(p_n[r], _bf16(vn))) * rden[r]
        for old, new, dst in ((kt, kn, okt_ref), (vt, vn, ovt_ref)):
            new_cols = pltpu.roll(jnp.concatenate([new, zrows], axis=0).T, shift, axis=1)
            dst[b] = jnp.where(lane >= shift, new_cols, pltpu.roll(old, shift, axis=1))


def _swa_sample(sinks, q, kn, vn, cache_k, cache_v):
    nb = cache_k.shape[0]
    bb = SWA_BB
    groups = N_HEADS // N_KV_HEADS
    qh = q.reshape(nb, DEC_SEQ, N_KV_HEADS, groups, HEAD_DIM).transpose(0, 2, 3, 1, 4)
    qh = qh.reshape(nb, N_KV_HEADS, groups * DEC_SEQ, HEAD_DIM)
    zeros = jnp.zeros_like(qh[:, 0])
    qbd = jnp.concatenate([jnp.concatenate([qh[:, 0], zeros], axis=-1),
                           jnp.concatenate([zeros, qh[:, 1]], axis=-1)], axis=1)
    sink_col = jnp.broadcast_to(jnp.repeat(sinks, DEC_SEQ).reshape(SWA_ROWS, 1), (SWA_ROWS, LANES))
    pad8 = lambda a: jnp.pad(a.reshape(nb, DEC_SEQ, KV_WIDTH), ((0, 0), (0, NEW_ROWS - DEC_SEQ), (0, 0)))
    to_t = lambda c: c.transpose(0, 2, 3, 1).reshape(nb, KV_WIDTH, WINDOW)
    blk = lambda r, w: pl.BlockSpec((bb, r, w), lambda i: (i, 0, 0))
    o, okt, ovt = pl.pallas_call(
        _swa_sample_kernel,
        grid=(nb // bb,),
        in_specs=[blk(SWA_ROWS, KV_WIDTH), pl.BlockSpec((SWA_ROWS, LANES), lambda i: (0, 0)),
                  blk(NEW_ROWS, KV_WIDTH), blk(NEW_ROWS, KV_WIDTH), blk(KV_WIDTH, WINDOW), blk(KV_WIDTH, WINDOW)],
        out_specs=[blk(SWA_ROWS, KV_WIDTH), blk(KV_WIDTH, WINDOW), blk(KV_WIDTH, WINDOW)],
        out_shape=[jax.ShapeDtypeStruct((nb, SWA_ROWS, KV_WIDTH), jnp.float32),
                   jax.ShapeDtypeStruct((nb, KV_WIDTH, WINDOW), jnp.float32),
                   jax.ShapeDtypeStruct((nb, KV_WIDTH, WINDOW), jnp.float32)],
        compiler_params=_params(("arbitrary",)),
        name="swa_sample",
    )(qbd, sink_col, pad8(kn), pad8(vn), to_t(cache_k), to_t(cache_v))
    o = o.reshape(nb, N_KV_HEADS, groups, DEC_SEQ, N_KV_HEADS, HEAD_DIM)
    attn = jnp.stack([o[:, h, :, :, h, :] for h in range(N_KV_HEADS)], axis=1)
    attn = attn.transpose(0, 3, 1, 2, 4).reshape(nb * DEC_SEQ, ATTN_WIDTH)
    from_t = lambda c: c.reshape(nb, N_KV_HEADS, HEAD_DIM, WINDOW).transpose(0, 3, 1, 2)
    return attn, from_t(okt), from_t(ovt)


def _mem_kv_kernel(mem_ref, wk_ref, wv_ref, wq_ref, wo_ref, mk_ref, mv_ref, wqk_ref, wvo_ref):
    mb = _bf16(mem_ref[...])
    mk = _dot(mb, wk_ref[...])
    mv = _dot(mb, wv_ref[...])
    mk_ref[...] = mk
    mv_ref[...] = mv
    mkb, mvb = _bf16(mk), _bf16(mv)
    for h in range(MEM_HEADS):
        sl = slice(h * MEM_HEAD_DIM, (h + 1) * MEM_HEAD_DIM)
        keys = slice(h * N_MEM, (h + 1) * N_MEM)
        wqk_ref[:, keys] = _bf16(_dot_nt(wq_ref[:, sl], mkb[:, sl]) * (MEM_HEAD_DIM ** -0.5))
        wvo_ref[keys, :] = _bf16(_dot(mvb[:, sl], wo_ref[sl, :]))


def _mem_kv(mem, wk_b, wv_b, wq_b, wo_b):
    full = lambda a: pl.BlockSpec(a.shape, lambda i: (0,) * a.ndim)
    blk = pl.BlockSpec((N_MEM, D_MODEL), lambda i: (0, 0))
    f32 = jax.ShapeDtypeStruct((N_MEM, D_MODEL), jnp.float32)
    fused = (D_MODEL, MEM_HEADS * N_MEM), (MEM_HEADS * N_MEM, D_MODEL)
    return pl.pallas_call(
        _mem_kv_kernel,
        grid=(1,),
        in_specs=[full(mem), full(wk_b), full(wv_b), full(wq_b), full(wo_b)],
        out_specs=[blk, blk] + [pl.BlockSpec(shp, lambda i: (0, 0)) for shp in fused],
        out_shape=[f32, f32] + [jax.ShapeDtypeStruct(shp, jnp.bfloat16) for shp in fused],
        compiler_params=_params(("arbitrary",)),
        name="mem_kv",
    )(mem, wk_b, wv_b, wq_b, wo_b)


def _mix_ln1(attn_ref, conv_ref, x_ref, wmix_ref, g1_ref, b1_ref):
    mix = _dot(_bf16(attn_ref[...]), wmix_ref[0:ATTN_WIDTH, :]) + _dot(conv_ref[...], wmix_ref[ATTN_WIDTH:, :])
    return _layer_norm(ALPHA * x_ref[...] + mix, g1_ref[...], b1_ref[...])


def _mem_q(h1, wq_ref):
    return _bf16(_dot(_bf16(h1), wq_ref[...]) * (MEM_HEAD_DIM ** -0.5))


def _route(h2, wrhl_ref, br_ref, tri_ref, carry):
    hi = _bf16(h2)
    lo = _bf16(h2 - hi.astype(jnp.float32))
    hh = _dot(hi, wrhl_ref[...])
    logits = hh[:, 0:LANES] + hh[:, LANES:] + _dot(lo, wrhl_ref[:, 0:LANES]) + br_ref[...]
    lane_i = lax.broadcasted_iota(jnp.int32, logits.shape, 1)
    lane = lane_i.astype(jnp.float32)
    big = jnp.float32(LANES)
    is_g = lane_i < N_GROUPS
    gl = jnp.where(is_g, logits, -jnp.inf)
    gmax = jnp.max(gl, axis=1, keepdims=True)
    gidx = jnp.min(jnp.where(is_g & (logits == gmax), lane, big), axis=1, keepdims=True)
    gsum = jnp.sum(jnp.exp(gl - gmax), axis=1, keepdims=True)
    gw = 1.0 / gsum
    eid = lane_i - N_GROUPS
    assert EXPERTS_PER_GROUP == 8
    grp = lax.shift_right_arithmetic(eid, jnp.full_like(eid, 3)).astype(jnp.float32)
    in_e = (lane_i >= N_GROUPS) & (lane_i < N_GROUPS + N_EXPERTS) & (grp == gidx)
    v1 = jnp.max(jnp.where(in_e, logits, -jnp.inf), axis=1, keepdims=True)
    i1 = jnp.min(jnp.where(in_e & (logits == v1), lane, big), axis=1, keepdims=True)
    rest = in_e & (lane != i1)
    v2 = jnp.max(jnp.where(rest, logits, -jnp.inf), axis=1, keepdims=True)
    i2 = jnp.min(jnp.where(rest & (logits == v2), lane, big), axis=1, keepdims=True)
    ex = jnp.exp(v2 - v1)
    den = 1.0 + ex
    w1 = gw / den
    w2 = gw * ex / den
    zero = jnp.zeros_like(logits)
    pick1 = lane == i1
    pick2 = lane == i2
    sel = jnp.where(pick1 | pick2, 1.0, 0.0)
    before = _dot(tri_ref[...], _bf16(sel)) + carry
    rank1 = jnp.sum(jnp.where(pick1, before, zero), axis=1, keepdims=True)
    rank2 = jnp.sum(jnp.where(pick2, before, zero), axis=1, keepdims=True)
    cols = (i1 - N_GROUPS, i2 - N_GROUPS, w1, w2, rank1, rank2)
    route = zero
    for k, col in enumerate(cols):
        route = jnp.where(lane_i == k, col, route)
    return route, carry + jnp.sum(sel, axis=0, keepdims=True)


def _post_tile(attn, conv_ref, x_ref, wmix_ref, g1_ref, b1_ref, wqk_ref, wvo_ref, g2_ref, b2_ref,
               wrhl_ref, br_ref, tri_ref, h2t_ref, rt_ref, cnt_ref, carry_ref):
    mix = _dot(attn, wmix_ref[0:ATTN_WIDTH, :]) + _dot(conv_ref[...], wmix_ref[ATTN_WIDTH:, :])
    h1 = _layer_norm(ALPHA * x_ref[...] + mix, g1_ref[...], b1_ref[...])
    scores = _dot(_bf16(h1), wqk_ref[...])
    probs = []
    for h in range(MEM_HEADS):
        s = scores[:, h * N_MEM:(h + 1) * N_MEM]
        p = jnp.exp(s - jnp.max(s, axis=1, keepdims=True))
        probs.append(_bf16(p / jnp.sum(p, axis=1, keepdims=True)))
    mem_out = _dot(jnp.concatenate(probs, axis=1), wvo_ref[...])
    h2 = _layer_norm(ALPHA * h1 + mem_out, g2_ref[...], b2_ref[...])
    _store_token_tiles(h2t_ref, h2)
    route, carry = _route(h2, wrhl_ref, br_ref, tri_ref, carry_ref[0:1, :])
    rt_ref[...] = route
    carry_ref[...] = jnp.broadcast_to(carry, carry_ref.shape)
    cnt_ref[...] = jnp.broadcast_to(carry, cnt_ref.shape)


def _swa_post_kernel(sinks_ref, q_ref, kc_ref, kp_ref, vc_ref, vp_ref,
                     conv_ref, x_ref, wmix_ref, g1_ref, b1_ref, wqk_ref, wvo_ref, g2_ref, b2_ref,
                     wrhl_ref, br_ref, tri_ref, h2s_ref, rts_ref, cnts_ref, wg_ref, wu_ref, wd_ref,
                     h2t_ref, rt_ref, cnt_ref, wgb_ref, wub_ref, wdb_ref, carry_ref, attn_s):
    t = pl.program_id(0)
    steps = N_PROMPT // TM_POST
    par = t % 2

    def swa(slot):
        def store(rows, cols, val):
            attn_s[slot, rows, cols] = val
        _swa_tile(t, sinks_ref, q_ref, kc_ref, kp_ref, vc_ref, vp_ref, store)

    def post(slot):
        _post_tile(attn_s[slot], conv_ref, x_ref, wmix_ref, g1_ref, b1_ref, wqk_ref, wvo_ref, g2_ref, b2_ref,
                   wrhl_ref, br_ref, tri_ref, h2t_ref, rt_ref, cnt_ref, carry_ref)

    def cast_expert():
        wgb_ref[...] = _bf16(wg_ref[...])
        wub_ref[...] = _bf16(wu_ref[...])
        wdb_ref[...] = _bf16(wd_ref[...])

    @pl.when(t == 0)
    def _():
        carry_ref[...] = cnts_ref[...]
        swa(0)
        cast_expert()

    @pl.when((t >= 1) & (t < steps))
    def _():
        swa(par)
        post(1 - par)
        cast_expert()

    @pl.when(t == steps)
    def _():
        post(1 - par)

    @pl.when(t == steps + 1)
    def _():
        h2t_ref[...] = h2s_ref[...]
        rt_ref[...] = rts_ref[...]


def _swa_post(sinks, q, kx, vx, conv, x, wmix_b, g1, b1, wqk, wvo, g2, b2, wrhl, br, tri, h2t_s, rt_s, cnt_s,
              w_gate, w_up, w_down):
    n = x.shape[0]
    tm = TM_POST
    assert tm == SWA_QB * WINDOW
    steps = n // tm
    assert N_EXPERTS <= steps
    expert = lambda a: pl.BlockSpec((1,) + a.shape[1:], lambda i: (jnp.minimum(i, N_EXPERTS - 1), 0, 0))
    experts = (w_gate, w_up, w_down)
    cur = lambda w: pl.BlockSpec((tm, w), lambda i: (jnp.minimum(i, steps - 1), 0))
    prev = lambda w: pl.BlockSpec((WINDOW, w), lambda i: (jnp.clip(SWA_QB * i - 1, 0, n // WINDOW - 1), 0))
    lag = lambda w: pl.BlockSpec((tm, w), lambda i: (jnp.clip(i - 1, 0, steps - 1), 0))
    full = lambda a: pl.BlockSpec(a.shape, lambda i: (0,) * a.ndim)
    weights = (wmix_b, g1, b1, wqk, wvo, g2, b2, wrhl, br, tri, h2t_s, rt_s, cnt_s)
    n_out = n + h2t_s.shape[0] // ROW_CHUNKS
    out_idx = lambda i: (jnp.where(i > steps, steps, jnp.clip(i - 1, 0, steps - 1)), 0)
    return pl.pallas_call(
        _swa_post_kernel,
        grid=(steps + 2,),
        in_specs=([pl.BlockSpec(memory_space=pltpu.SMEM), cur(ATTN_WIDTH),
                   cur(4 * LANES), prev(4 * LANES), cur(4 * LANES), prev(4 * LANES),
                   lag(CONV_CH), lag(D_MODEL)] + [full(a) for a in weights] + [expert(a) for a in experts]),
        out_specs=[pl.BlockSpec((tm * ROW_CHUNKS, LANES), out_idx),
                   pl.BlockSpec((tm, LANES), out_idx),
                   pl.BlockSpec((SUBLANES, LANES), lambda i: (0, 0))] + [expert(a) for a in experts],
        out_shape=[jax.ShapeDtypeStruct((n_out * ROW_CHUNKS, LANES), jnp.float32),
                   jax.ShapeDtypeStruct((n_out, LANES), jnp.float32),
                   jax.ShapeDtypeStruct((SUBLANES, LANES), jnp.float32)]
                  + [jax.ShapeDtypeStruct(a.shape, jnp.bfloat16) for a in experts],
        scratch_shapes=[pltpu.VMEM((SUBLANES, LANES), jnp.float32),
                        pltpu.VMEM((2, tm, ATTN_WIDTH), jnp.bfloat16)],
        compiler_params=_params(("arbitrary",)),
        name="swa_post_prompt",
    )(sinks, q, kx, kx, vx, vx, conv, x, *weights, *experts)


def _post_a_sample_kernel(attn_ref, conv_ref, x_ref, wmix_ref, g1_ref, b1_ref, wq_ref, h1_ref, qm_ref):
    h1 = _mix_ln1(attn_ref, conv_ref, x_ref, wmix_ref, g1_ref, b1_ref)
    h1_ref[...] = h1
    qm_ref[...] = _mem_q(h1, wq_ref).astype(jnp.float32)


def _post_a_sample(attn, conv, x, wmix_b, g1, b1, wq_b):
    n = x.shape[0]
    args = (attn, conv, x, wmix_b, g1, b1, wq_b)
    full = lambda a: pl.BlockSpec(a.shape, lambda i: (0,) * a.ndim)
    blk = pl.BlockSpec((n, D_MODEL), lambda i: (0, 0))
    return pl.pallas_call(
        _post_a_sample_kernel,
        grid=(1,),
        in_specs=[full(a) for a in args],
        out_specs=[blk, blk],
        out_shape=[jax.ShapeDtypeStruct((n, D_MODEL), jnp.float32),
                   jax.ShapeDtypeStruct((n, D_MODEL), jnp.float32)],
        compiler_params=_params(("arbitrary",)),
        name="post_a_sample",
    )(*args)


MEM_ROWS = MEM_HEADS * DEC_SEQ


def _mem_attn_sample_body(q_ref, mk_ref, mv_ref, o_ref):
    nk = N_MEM * MEM_HEADS
    nb = SAMPLE_BB
    rows = nb * MEM_ROWS
    row_h = (lax.broadcasted_iota(jnp.int32, (rows, nk), 0) % MEM_ROWS) // DEC_SEQ
    key_h = lax.broadcasted_iota(jnp.int32, (rows, nk), 1) % MEM_HEADS
    s = jnp.concatenate(
        [_dot_nt(_bf16(q_ref[b]), _bf16(mk_ref[b].reshape(nk, MEM_HEAD_DIM))) for b in range(nb)], axis=0)
    s = jnp.where(row_h == key_h, s, -jnp.inf)
    p = jnp.exp(s - jnp.max(s, axis=1, keepdims=True))
    rden = 1.0 / jnp.sum(p, axis=1, keepdims=True)
    p = _bf16(p)
    for b in range(nb):
        r = slice(b * MEM_ROWS, (b + 1) * MEM_ROWS)
        o_ref[b] = _dot(p[r], _bf16(mv_ref[b].reshape(nk, MEM_HEAD_DIM))) * rden[r]


def _proj_mem_kernel(x_ref, w_ref, tab_ref, cw_ref, mq_ref, mk_ref, mv_ref,
                     q_ref, kx_ref, vx_ref, conv_ref, ktail_ref, vtail_ref, utail_ref, mo_ref, carry_ref):
    _mem_attn_sample_body(mq_ref, mk_ref, mv_ref, mo_ref)
    _proj_prompt_body(x_ref, w_ref, tab_ref, cw_ref,
                      q_ref, kx_ref, vx_ref, conv_ref, ktail_ref, vtail_ref, utail_ref, carry_ref)


def _proj_prompt_mem_sample(x, w_in_b, tab, conv_w, qm, mk, mv):
    n = x.shape[0]
    nb = mk.shape[0]
    bb = SAMPLE_BB
    steps = nb // bb
    tm = n // steps
    row = lambda w: pl.BlockSpec((tm, w), lambda i: (i, 0))
    full = lambda a: pl.BlockSpec(a.shape, lambda i: (0,) * a.ndim)
    const = lambda r, w: pl.BlockSpec((r, w), lambda i: (0, 0))
    mq = qm.reshape(nb, DEC_SEQ, MEM_HEADS, MEM_HEAD_DIM).transpose(0, 2, 1, 3).reshape(nb, MEM_ROWS, MEM_HEAD_DIM)
    mrows = pl.BlockSpec((bb, MEM_ROWS, MEM_HEAD_DIM), lambda i: (i, 0, 0))
    kv = pl.BlockSpec((bb, N_MEM, MEM_HEADS, MEM_HEAD_DIM), lambda i: (i, 0, 0, 0))
    outs = pl.pallas_call(
        _proj_mem_kernel,
        grid=(steps,),
        in_specs=[row(D_MODEL), full(w_in_b), pl.BlockSpec((ROPE_ROWS, tm), lambda i: (0, i)), full(conv_w),
                  mrows, kv, kv],
        out_specs=[row(ATTN_WIDTH), row(4 * LANES), row(4 * LANES), row(CONV_CH),
                   const(WINDOW, KV_WIDTH), const(WINDOW, KV_WIDTH), const(SUBLANES, CONV_CH), mrows],
        out_shape=[jax.ShapeDtypeStruct((n, ATTN_WIDTH), jnp.bfloat16),
                   jax.ShapeDtypeStruct((n, 4 * LANES), jnp.bfloat16),
                   jax.ShapeDtypeStruct((n, 4 * LANES), jnp.bfloat16),
                   jax.ShapeDtypeStruct((n, CONV_CH), jnp.bfloat16),
                   jax.ShapeDtypeStruct((WINDOW, KV_WIDTH), jnp.float32),
                   jax.ShapeDtypeStruct((WINDOW, KV_WIDTH), jnp.float32),
                   jax.ShapeDtypeStruct((SUBLANES, CONV_CH), jnp.float32),
                   jax.ShapeDtypeStruct((nb, MEM_ROWS, MEM_HEAD_DIM), jnp.float32)],
        scratch_shapes=[pltpu.VMEM((SUBLANES, CONV_CH), jnp.float32)],
        compiler_params=_params(("arbitrary",)),
        name="proj_prompt_mem_sample",
    )(x, w_in_b, tab, conv_w, mq, mk, mv)
    o = outs[7].reshape(nb, MEM_HEADS, DEC_SEQ, MEM_HEAD_DIM).transpose(0, 2, 1, 3).reshape(nb * DEC_SEQ, D_MODEL)
    return outs[:7], o


def _post_b_sample_kernel(o_ref, h1_ref, wo_ref, g2_ref, b2_ref, wrhl_ref, br_ref, tri_ref,
                          h2t_ref, rt_ref, cnt_ref):
    h2 = _layer_norm(ALPHA * h1_ref[...] + _dot(_bf16(o_ref[...]), wo_ref[...]), g2_ref[...], b2_ref[...])
    _store_token_tiles(h2t_ref, h2)
    route, carry = _route(h2, wrhl_ref, br_ref, tri_ref, jnp.zeros((1, LANES), jnp.float32))
    rt_ref[...] = route
    cnt_ref[...] = jnp.broadcast_to(carry, cnt_ref.shape)


def _post_b_sample(o, h1, wo_b, g2, b2, wrhl, br, tri):
    n = h1.shape[0]
    args = (o, h1, wo_b, g2, b2, wrhl, br, tri)
    full = lambda a: pl.BlockSpec(a.shape, lambda i: (0,) * a.ndim)
    return pl.pallas_call(
        _post_b_sample_kernel,
        grid=(1,),
        in_specs=[full(a) for a in args],
        out_specs=[pl.BlockSpec((n * ROW_CHUNKS, LANES), lambda i: (0, 0)),
                   pl.BlockSpec((n, LANES), lambda i: (0, 0)),
                   pl.BlockSpec((SUBLANES, LANES), lambda i: (0, 0))],
        out_shape=[jax.ShapeDtypeStruct((n * ROW_CHUNKS, LANES), jnp.float32),
                   jax.ShapeDtypeStruct((n, LANES), jnp.float32),
                   jax.ShapeDtypeStruct((SUBLANES, LANES), jnp.float32)],
        compiler_params=_params(("arbitrary",)),
        name="post_b_sample",
    )(*args)


def _row_gather_copy(src_hbm, idx, dst, dst_row, sem):
    s0 = pl.multiple_of(idx * ROW_CHUNKS, ROW_CHUNKS)
    d0 = pl.multiple_of(dst_row * ROW_CHUNKS, ROW_CHUNKS)
    return pltpu.make_async_copy(src_hbm.at[pl.ds(s0, ROW_CHUNKS), :], dst.at[pl.ds(d0, ROW_CHUNKS), :], sem)


def _dispatch_kernel(pos_ref, h2t_ref, xs_hbm, sem):
    def body(r, c):
        src = h2t_ref.at[pl.ds(pl.multiple_of(r * ROW_CHUNKS, ROW_CHUNKS), ROW_CHUNKS), :]
        for k in range(2):
            d0 = pl.multiple_of(pos_ref[0, 0, k * TM_COMB + r] * ROW_CHUNKS, ROW_CHUNKS)
            pltpu.make_async_copy(src, xs_hbm.at[pl.ds(d0, ROW_CHUNKS), :], sem.at[0]).start(priority=k)
        return c
    lax.fori_loop(0, TM_COMB, body, 0, unroll=8)
    for _ in range(2):
        pltpu.make_async_copy(h2t_ref, xs_hbm.at[pl.ds(0, TM_COMB * ROW_CHUNKS), :], sem.at[0]).wait()


def _dispatch(pos3, h2t):
    nt = N_ALL // TM_COMB
    return pl.pallas_call(
        _dispatch_kernel,
        grid=(nt,),
        in_specs=[pl.BlockSpec((1, 1, 2 * TM_COMB), lambda i: (i, 0, 0), memory_space=pltpu.SMEM),
                  pl.BlockSpec((TM_COMB * ROW_CHUNKS, LANES), lambda i: (i, 0))],
        out_specs=pl.BlockSpec(memory_space=pl.ANY),
        out_shape=jax.ShapeDtypeStruct((N_ASSIGN * ROW_CHUNKS, LANES), jnp.float32),
        scratch_shapes=[pltpu.SemaphoreType.DMA((1,))],
        compiler_params=_params(("arbitrary",)),
        name="moe_dispatch",
    )(pos3, h2t)


def _moe_ffn_kernel(it_ref, ie_ref, lo_ref, hi_ref, x_ref, wg_ref, wu_ref, wd_ref, y_ref):
    i = pl.program_id(0)
    lo = lo_ref[i]
    hi = hi_ref[i]

    def ffn(r0, rows):
        x = _bf16(_load_token_tiles(x_ref, r0 * ROW_CHUNKS, rows))
        hg = _dot(x, wg_ref[0])
        hu = _dot(x, wu_ref[0])
        h = hg / (1.0 + jnp.exp(-hg)) * hu
        return _dot(_bf16(h), wd_ref[0])

    whole = (lo == 0) & (hi == TM_MOE)

    @pl.when(whole)
    def _():
        _store_token_tiles(y_ref, ffn(0, TM_MOE))

    for r0 in range(0, TM_MOE, MOE_BLOCK):
        live = (hi > lo) & jnp.logical_not(whole) & (hi > r0) & (lo < r0 + MOE_BLOCK)

        def store(merge, r0=r0):
            y = ffn(r0, MOE_BLOCK)
            row = r0 + lax.broadcasted_iota(jnp.int32, (MOE_BLOCK, LANES), 0)
            mask = (row >= lo) & (row < hi)
            for c in range(ROW_CHUNKS):
                sl = pl.ds(r0 * ROW_CHUNKS + c, MOE_BLOCK, stride=ROW_CHUNKS)
                y_ref[sl, :] = jnp.where(mask, y[:, c * LANES:(c + 1) * LANES], y_ref[sl, :] if merge else 0.0)

        pl.when(live & (lo <= r0))(functools.partial(store, False))
        pl.when(live & (lo > r0))(functools.partial(store, True))


def _moe_ffn(item_tile, item_expert, item_lo, item_hi, x_sorted, w_gate, w_up, w_down):
    wspec = lambda shp: pl.BlockSpec((1,) + shp, lambda i, it, ie, lo, hi: (ie[i], 0, 0))
    tile = pl.BlockSpec((TM_MOE * ROW_CHUNKS, LANES), lambda i, it, ie, lo, hi: (it[i], 0))
    grid_spec = pltpu.PrefetchScalarGridSpec(
        num_scalar_prefetch=4,
        grid=(MOE_ITEMS,),
        in_specs=[tile, wspec((D_MODEL, EXPERT_FF)), wspec((D_MODEL, EXPERT_FF)), wspec((EXPERT_FF, D_MODEL))],
        out_specs=tile,
    )
    return pl.pallas_call(
        _moe_ffn_kernel,
        grid_spec=grid_spec,
        out_shape=jax.ShapeDtypeStruct((N_ASSIGN * ROW_CHUNKS, LANES), jnp.float32),
        compiler_params=_params(("arbitrary",)),
        name="moe_ffn",
    )(item_tile, item_expert, item_lo, item_hi, x_sorted, w_gate, w_up, w_down)


def _combine_kernel(nt, pos_cur_ref, pos_nxt_ref, yt_hbm, h2t_ref, rt_ref, g3_ref, b3_ref, o_ref, abuf, sem):
    t = pl.program_id(0)
    slot = t % 2
    rows = 2 * TM_COMB

    def issue(pos_ref, s):
        def body(j, c):
            for k in range(2):
                r = 2 * j + k
                _row_gather_copy(yt_hbm, pos_ref[0, 0, r], abuf, s * rows + r, sem.at[s]).start(priority=k)
            return c
        lax.fori_loop(0, rows // 2, body, 0, unroll=16)

    @pl.when(t == 0)
    def _():
        issue(pos_cur_ref, 0)

    @pl.when(t + 1 < nt)
    def _():
        issue(pos_nxt_ref, 1 - slot)

    base = pl.multiple_of(slot * (rows * ROW_CHUNKS), rows * ROW_CHUNKS)
    pltpu.make_async_copy(yt_hbm.at[pl.ds(0, rows * ROW_CHUNKS), :],
                          abuf.at[pl.ds(base, rows * ROW_CHUNKS), :], sem.at[slot]).wait()
    ya = _load_token_tiles(abuf, base, TM_COMB)
    yb = _load_token_tiles(abuf, base + TM_COMB * ROW_CHUNKS, TM_COMB)
    rt = rt_ref[...]
    ff = rt[:, 2:3] * ya + rt[:, 3:4] * yb
    h2 = _load_token_tiles(h2t_ref, 0, TM_COMB)
    o_ref[...] = _layer_norm(ALPHA * h2 + ff, g3_ref[...], b3_ref[...])


def _combine(pos3, yt, h2t, rt, g3, b3, tile0, n_tiles):
    last = tile0 + n_tiles - 1
    smem_pos = lambda f: pl.BlockSpec((1, 1, 2 * TM_COMB), f, memory_space=pltpu.SMEM)
    full = lambda a: pl.BlockSpec(a.shape, lambda i: (0,) * a.ndim)
    return pl.pallas_call(
        functools.partial(_combine_kernel, n_tiles),
        grid=(n_tiles,),
        in_specs=[smem_pos(lambda i: (tile0 + i, 0, 0)),
                  smem_pos(lambda i: (jnp.minimum(tile0 + i + 1, last), 0, 0)),
                  pl.BlockSpec(memory_space=pl.ANY),
                  pl.BlockSpec((TM_COMB * ROW_CHUNKS, LANES), lambda i: (tile0 + i, 0)),
                  pl.BlockSpec((TM_COMB, LANES), lambda i: (tile0 + i, 0)),
                  full(g3), full(b3)],
        out_specs=pl.BlockSpec((TM_COMB, D_MODEL), lambda i: (i, 0)),
        out_shape=jax.ShapeDtypeStruct((n_tiles * TM_COMB, D_MODEL), jnp.float32),
        scratch_shapes=[pltpu.VMEM((2 * 2 * TM_COMB * ROW_CHUNKS, LANES), jnp.float32),
                        pltpu.SemaphoreType.DMA((2,))],
        compiler_params=_params(("arbitrary",)),
        name="moe_combine",
    )(pos3, pos3, yt, h2t, rt, g3, b3)


POS_TILES = 11


def _positions_kernel(rt_ref, starts_ref, pos_ref):
    expert = lax.broadcasted_iota(jnp.int32, (N_EXPERTS, TM_COMB), 0).astype(jnp.float32)
    starts = jnp.concatenate([starts_ref[...]] * (TM_COMB // LANES), axis=1)
    for j in range(POS_TILES):
        cols = rt_ref[j * TM_COMB:(j + 1) * TM_COMB, :].T
        out = []
        for k in range(2):
            seg = jnp.sum(jnp.where(expert == cols[k:k + 1, :], starts, 0.0), axis=0, keepdims=True)
            out.append(seg + cols[4 + k:5 + k, :])
        pos_ref[j] = jnp.concatenate(out, axis=1).astype(jnp.int32)


def _positions(rt, starts_rep):
    nt = N_ALL // TM_COMB
    assert nt % POS_TILES == 0
    return pl.pallas_call(
        _positions_kernel,
        grid=(nt // POS_TILES,),
        in_specs=[pl.BlockSpec((POS_TILES * TM_COMB, LANES), lambda i: (i, 0)),
                  pl.BlockSpec((N_EXPERTS, LANES), lambda i: (0, 0))],
        out_specs=pl.BlockSpec((POS_TILES, 1, 2 * TM_COMB), lambda i: (i, 0, 0)),
        out_shape=jax.ShapeDtypeStruct((nt, 1, 2 * TM_COMB), jnp.int32),
        compiler_params=_params(("arbitrary",)),
        name="moe_positions",
    )(rt, starts_rep)


def _routing_plan(rt, cnt):
    i32 = jnp.int32
    counts_f = cnt[0, N_GROUPS:N_GROUPS + N_EXPERTS]
    starts_f = jnp.cumsum(counts_f) - counts_f
    pos3 = _positions(rt, jnp.broadcast_to(starts_f[:, None], (N_EXPERTS, LANES)))
    starts = starts_f.astype(i32)
    tiles = jnp.arange(MOE_TILES, dtype=i32) * TM_MOE
    rank_t = jnp.arange(MOE_TILES, dtype=i32) + jnp.sum((starts[None, :] < tiles[:, None]).astype(i32), axis=1)
    rank_s = jnp.arange(N_EXPERTS, dtype=i32) + jnp.sum((tiles[None, :] <= starts[:, None]).astype(i32), axis=1)
    vals = jnp.concatenate([tiles, starts])
    ranks = jnp.concatenate([rank_t, rank_s])
    slot = jnp.arange(MOE_ITEMS, dtype=i32)
    lo = jnp.sum(jnp.where(ranks[None, :] == slot[:, None], vals[None, :], 0), axis=1)
    hi = jnp.concatenate([lo[1:], jnp.full((1,), N_ASSIGN, i32)])
    item_tile = jnp.minimum(lo // TM_MOE, MOE_TILES - 1)
    item_expert = jnp.clip(jnp.sum((starts[None, :] <= lo[:, None]).astype(i32), axis=1) - 1, 0, N_EXPERTS - 1)
    base = item_tile * TM_MOE
    return item_tile, item_expert, lo - base, hi - base, pos3


def kernel(x_prompt, x_sample, mem_prompt, cache_swa_k, cache_swa_v, cache_conv, cache_mem_k, cache_mem_v,
           w_in, sinks, conv_w, w_mix_out, ln1_g, ln1_b, w_q_mem, w_k_mem, w_v_mem, w_o_mem, ln2_g, ln2_b,
           w_router_group, b_router_group, w_router_expert, b_router_expert, w_gate, w_up, w_down,
           ln3_g, ln3_b):
    f32 = jnp.float32
    row = lambda a: a.reshape(1, -1).astype(f32)
    w_in_b, wmix_b, wq_b, wk_b, wv_b, wo_b = (_bf16(w) for w in (w_in, w_mix_out, w_q_mem, w_k_mem, w_v_mem, w_o_mem))
    g1, b1, g2, b2, g3, b3 = (row(a) for a in (ln1_g, ln1_b, ln2_g, ln2_b, ln3_g, ln3_b))
    pad = LANES - N_GROUPS - N_EXPERTS
    wr = jnp.concatenate([w_router_group, w_router_expert, jnp.zeros((D_MODEL, pad), f32)], axis=1)
    wrh = _bf16(wr)
    wrhl = jnp.concatenate([wrh, _bf16(wr - wrh.astype(f32))], axis=1)
    br = jnp.concatenate([b_router_group, b_router_expert, jnp.zeros((pad,), f32)]).reshape(1, LANES)

    xs = x_sample.reshape(N_SAMPLE, D_MODEL)
    tab_s = jnp.tile(_rope_table(PAST_LEN + jnp.arange(DEC_SEQ)), (1, DEC_BATCH))
    c0 = jnp.repeat(cache_conv[:, 0], DEC_SEQ, axis=0)
    c1 = jnp.repeat(cache_conv[:, 1], DEC_SEQ, axis=0)
    q_s, k_s, v_s, conv_s, u_s = _proj_sample(xs, w_in_b, tab_s, conv_w, c0, c1)
    attn_s, swa_k_s, swa_v_s = _swa_sample(sinks, q_s, k_s, v_s, cache_swa_k, cache_swa_v)
    h1_s, qm_s = _post_a_sample(attn_s, conv_s, xs, wmix_b, g1, b1, wq_b)
    xp = x_prompt.reshape(N_PROMPT, D_MODEL)
    tab_p = _rope_table(jnp.arange(N_PROMPT))
    (q_p, kx_p, vx_p, conv_p, k_tail, v_tail, u_tail), o_s = _proj_prompt_mem_sample(
        xp, w_in_b, tab_p, conv_w, qm_s, cache_mem_k, cache_mem_v)
    tri = _bf16(jnp.tril(jnp.ones((TM_POST, TM_POST), f32), -1))
    h2t_s, rt_s, cnt_s = _post_b_sample(o_s, h1_s, wo_b, g2, b2, wrhl, br, tri)

    mk, mv, wqk, wvo = _mem_kv(mem_prompt.reshape(N_MEM, D_MODEL), wk_b, wv_b, wq_b, wo_b)
    h2t, rt, cnt, wg_b, wu_b, wd_b = _swa_post(sinks, q_p, kx_p, vx_p, conv_p, xp, wmix_b, g1, b1, wqk, wvo, g2, b2,
                                               wrhl, br, tri, h2t_s, rt_s, cnt_s, w_gate, w_up, w_down)

    item_tile, item_expert, item_lo, item_hi, pos3 = _routing_plan(rt, cnt)
    x_sorted = _dispatch(pos3, h2t)
    yt = _moe_ffn(item_tile, item_expert, item_lo, item_hi, x_sorted, wg_b, wu_b, wd_b)
    y_p = _combine(pos3, yt, h2t, rt, g3, b3, 0, N_PROMPT // TM_COMB)
    y_s = _combine(pos3, yt, h2t, rt, g3, b3, N_PROMPT // TM_COMB, N_SAMPLE // TM_COMB)

    return (y_p.reshape(1, SEQ, D_MODEL),
            y_s.reshape(DEC_BATCH, DEC_SEQ, D_MODEL),
            k_tail.reshape(1, WINDOW, N_KV_HEADS, HEAD_DIM),
            v_tail.reshape(1, WINDOW, N_KV_HEADS, HEAD_DIM),
            u_tail[SUBLANES - (CONV_K - 1):].reshape(1, CONV_K - 1, CONV_CH),
            mk.reshape(1, N_MEM, MEM_HEADS, MEM_HEAD_DIM),
            mv.reshape(1, N_MEM, MEM_HEADS, MEM_HEAD_DIM),
            swa_k_s.reshape(DEC_BATCH, WINDOW, N_KV_HEADS, HEAD_DIM),
            swa_v_s.reshape(DEC_BATCH, WINDOW, N_KV_HEADS, HEAD_DIM),
            u_s.reshape(DEC_BATCH, DEC_SEQ, CONV_CH)[:, DEC_SEQ - (CONV_K - 1):])
```

```python
import functools

import jax
import jax.numpy as jnp
from jax import lax
from jax.experimental import pallas as pl
from jax.experimental.pallas import tpu as pltpu

D_MODEL = 1024
SEQ = 16384
DEC_BATCH = 128
DEC_SEQ = 4
PAST_LEN = 16384
ATTN_WIDTH = 512
CONV_CH = 512
HEAD_DIM = 64
N_HEADS = 8
N_KV_HEADS = 2
KV_WIDTH = 128
WINDOW = 128
ROPE_THETA = 500000.0
ROPE_DIM = 16
CONV_K = 3
Q_END = ATTN_WIDTH
K_END = Q_END + KV_WIDTH
V_END = K_END + KV_WIDTH
B_END = V_END + CONV_CH
C_END = B_END + CONV_CH
IN_WIDTH = C_END + CONV_CH
N_MEM = 256
MEM_HEADS = 4
MEM_HEAD_DIM = 256
N_GROUPS = 4
EXPERTS_PER_GROUP = 8
N_EXPERTS = 32
EXPERT_FF = 256
ALPHA = 2.0 ** 0.25
LN_EPS = 1e-5

LANES = 128
SUBLANES = 8
ROW_CHUNKS = D_MODEL // LANES
VMEM_LIMIT = 56 * 1024 * 1024

N_PROMPT = SEQ
N_SAMPLE = DEC_BATCH * DEC_SEQ
N_ALL = N_PROMPT + N_SAMPLE
TM_POST = 512
TM_MOE = 512
TM_COMB = 512
MOE_BLOCK = 256
MOE_X_COPIES = 4
N_ASSIGN = 2 * N_ALL
MOE_TILES = N_ASSIGN // TM_MOE
MOE_ITEMS = MOE_TILES + N_EXPERTS
SAMPLE_BB = 4
SWA_BB = 16

assert ROW_CHUNKS == SUBLANES
assert N_SAMPLE == TM_POST
assert N_ASSIGN % TM_MOE == 0 and N_ALL % TM_COMB == 0


def _params(sem, vmem=VMEM_LIMIT):
    return pltpu.CompilerParams(dimension_semantics=sem, vmem_limit_bytes=vmem)


def _bf16(x):
    return x.astype(jnp.bfloat16)


def _dot(a, b):
    return jnp.dot(a, b, preferred_element_type=jnp.float32)


def _dot_nt(a, b):
    return lax.dot_general(a, b, (((1,), (1,)), ((), ())), preferred_element_type=jnp.float32)


def _layer_norm(x, g, b):
    mu = jnp.mean(x, axis=-1, keepdims=True)
    xc = x - mu
    var = jnp.mean(xc * xc, axis=-1, keepdims=True)
    return xc * lax.rsqrt(var + LN_EPS) * g + b


def _rope(x, cos_t, sin_t):
    lane = lax.broadcasted_iota(jnp.int32, x.shape, 1) % HEAD_DIM
    half = ROPE_DIM // 2
    partner = jnp.where(lane < half, pltpu.roll(x, LANES - half, axis=1), pltpu.roll(x, half, axis=1))
    return x * cos_t + partner * sin_t


def _head_slabs(x):
    lane = lax.broadcasted_iota(jnp.int32, x.shape, 1)
    lo = lane < HEAD_DIM
    sw = pltpu.roll(x, HEAD_DIM, axis=1)
    zero = jnp.zeros_like(x)
    slabs = [jnp.where(lo, x, zero), jnp.where(lo, zero, sw), jnp.where(lo, sw, zero), jnp.where(lo, zero, x)]
    return _bf16(jnp.concatenate(slabs, axis=1))


def _store_token_tiles(ref, val):
    rows = val.shape[0]
    for c in range(ROW_CHUNKS):
        ref[pl.ds(c, rows, stride=ROW_CHUNKS), :] = val[:, c * LANES:(c + 1) * LANES]


def _load_token_tiles(ref, base, rows):
    return jnp.concatenate(
        [ref[pl.ds(base + c, rows, stride=ROW_CHUNKS), :] for c in range(ROW_CHUNKS)], axis=1)


ROPE_ONE = 3 * (ROPE_DIM // 2)
ROPE_ROWS = 32


def _rope_patterns(tab):
    half = ROPE_DIM // 2
    m = lax.broadcasted_iota(jnp.int32, tab.shape, 1) % HEAD_DIM
    idx_c = jnp.where(m < ROPE_DIM, m % half, ROPE_ONE)
    idx_s = jnp.where(m < half, 2 * half + m, jnp.where(m < ROPE_DIM, m, ROPE_ONE + 1))
    return jnp.take_along_axis(tab, idx_c, axis=1), jnp.take_along_axis(tab, idx_s, axis=1)


def _proj_common(x_ref, w_ref, tab_ref):
    xb = _bf16(x_ref[...])
    tab = tab_ref[...]
    pad = jnp.zeros((LANES - tab.shape[0], tab.shape[1]), jnp.float32)
    cos_t, sin_t = _rope_patterns(jnp.concatenate([tab, pad], axis=0).T)
    q = _dot(xb, w_ref[:, 0:Q_END])
    q_rot = jnp.concatenate(
        [_rope(q[:, p * LANES:(p + 1) * LANES], cos_t, sin_t) for p in range(ATTN_WIDTH // LANES)], axis=1)
    q_out = _bf16(q_rot * (HEAD_DIM ** -0.5))
    kv = _dot(xb, w_ref[:, Q_END:V_END])
    k = _rope(kv[:, 0:KV_WIDTH], cos_t, sin_t)
    v = kv[:, KV_WIDTH:]
    bg = _dot(xb, w_ref[:, V_END:B_END])
    u = _dot(xb, w_ref[:, B_END:C_END]) * _dot(xb, w_ref[:, C_END:IN_WIDTH])
    return q_out, k, v, bg, u


def _conv3(bg, u, u1, u2, cw_ref):
    cw = cw_ref[...]
    return bg * (cw[0:1, :] * u2 + cw[1:2, :] * u1 + cw[2:3, :] * u)


def _proj_prompt_body(x_ref, w_ref, tab_ref, cw_ref,
                      q_ref, kx_ref, vx_ref, conv_ref, ktail_ref, vtail_ref, utail_ref, carry_ref):
    @pl.when(pl.program_id(0) == 0)
    def _():
        carry_ref[...] = jnp.zeros_like(carry_ref)

    q_out, k, v, bg, u = _proj_common(x_ref, w_ref, tab_ref)
    tm = u.shape[0]
    ext = jnp.concatenate([carry_ref[...], u], axis=0)
    u1 = pltpu.roll(ext, 1, axis=0)[SUBLANES:SUBLANES + tm]
    u2 = pltpu.roll(ext, 2, axis=0)[SUBLANES:SUBLANES + tm]
    q_ref[...] = q_out
    kx_ref[...] = _head_slabs(k)
    vx_ref[...] = _head_slabs(v)
    conv_ref[...] = _bf16(_conv3(bg, u, u1, u2, cw_ref))
    ktail_ref[...] = k[tm - WINDOW:tm]
    vtail_ref[...] = v[tm - WINDOW:tm]
    utail_ref[...] = u[tm - SUBLANES:tm]
    carry_ref[...] = u[tm - SUBLANES:tm]


def _proj_sample_kernel(x_ref, w_ref, tab_ref, cw_ref, c0_ref, c1_ref,
                        q_ref, k_ref, v_ref, conv_ref, u_ref):
    q_out, k, v, bg, u = _proj_common(x_ref, w_ref, tab_ref)
    t = lax.broadcasted_iota(jnp.int32, u.shape, 0) % DEC_SEQ
    c0 = c0_ref[...]
    c1 = c1_ref[...]
    u1 = jnp.where(t >= 1, pltpu.roll(u, 1, axis=0), c1)
    u2 = jnp.where(t >= 2, pltpu.roll(u, 2, axis=0), jnp.where(t == 1, c1, c0))
    q_ref[...] = q_out.astype(jnp.float32)
    k_ref[...] = k
    v_ref[...] = v
    conv_ref[...] = _bf16(_conv3(bg, u, u1, u2, cw_ref))
    u_ref[...] = u


def _rope_table(pos):
    half = ROPE_DIM // 2
    inv = ROPE_THETA ** (-jnp.arange(0, ROPE_DIM, 2, dtype=jnp.float32) / ROPE_DIM)
    ang = pos.astype(jnp.float32)[None, :] * inv[:, None]
    cos, sin = jnp.cos(ang), jnp.sin(ang)
    n = pos.shape[0]
    assert ROPE_ONE == 3 * half
    return jnp.concatenate([cos, sin, -sin, jnp.ones((1, n), jnp.float32),
                            jnp.zeros((ROPE_ROWS - ROPE_ONE - 1, n), jnp.float32)], axis=0)


def _proj_sample(x, w_in_b, tab, conv_w, c0, c1):
    n = x.shape[0]
    full = lambda a: pl.BlockSpec(a.shape, lambda i: (0,) * a.ndim)
    out = lambda w, dt: jax.ShapeDtypeStruct((n, w), dt)
    blk = lambda w: pl.BlockSpec((n, w), lambda i: (0, 0))
    return pl.pallas_call(
        _proj_sample_kernel,
        grid=(1,),
        in_specs=[full(x), full(w_in_b), full(tab), full(conv_w), full(c0), full(c1)],
        out_specs=[blk(ATTN_WIDTH), blk(KV_WIDTH), blk(KV_WIDTH), blk(CONV_CH), blk(CONV_CH)],
        out_shape=[out(ATTN_WIDTH, jnp.float32), out(KV_WIDTH, jnp.float32), out(KV_WIDTH, jnp.float32),
                   out(CONV_CH, jnp.bfloat16), out(CONV_CH, jnp.float32)],
        compiler_params=_params(("arbitrary",)),
        name="proj_sample",
    )(x, w_in_b, tab, conv_w, c0, c1)


def _sink_softmax_pv(s, valid, sink, vx):
    s = jnp.where(valid, s, -jnp.inf)
    m = jnp.maximum(jnp.max(s, axis=1, keepdims=True), sink)
    p = jnp.exp(s - m)
    den = jnp.sum(p, axis=1, keepdims=True) + jnp.exp(sink - m)
    return _dot(_bf16(p), vx) / den


SWA_QB = 4


def _swa_tile(step, sinks_ref, q_ref, kc_ref, kp_ref, vc_ref, vp_ref, store):
    kall = jnp.concatenate([kp_ref[...], kc_ref[...]], axis=0)
    vall = jnp.concatenate([vp_ref[...], vc_ref[...]], axis=0)
    i = lax.broadcasted_iota(jnp.int32, (WINDOW, 2 * WINDOW), 0)
    j = lax.broadcasted_iota(jnp.int32, (WINDOW, 2 * WINDOW), 1)
    band = (j > i) & (j <= i + WINDOW)
    for sb in range(SWA_QB):
        rows = slice(sb * WINDOW, (sb + 1) * WINDOW)
        kcat = kall[sb * WINDOW:(sb + 2) * WINDOW]
        vcat = vall[sb * WINDOW:(sb + 2) * WINDOW]
        valid = band & ((step > 0) | (j >= WINDOW)) if sb == 0 else band
        for p in range(N_HEADS // 2):
            qs = q_ref[rows, p * LANES:(p + 1) * LANES]
            acc = None
            for e in range(2):
                hd = 2 * p + e
                slab = 2 * (hd // (N_HEADS // N_KV_HEADS)) + e
                kx = kcat[:, slab * LANES:(slab + 1) * LANES]
                vx = vcat[:, slab * LANES:(slab + 1) * LANES]
                o = _sink_softmax_pv(_dot_nt(qs, kx), valid, sinks_ref[hd], vx)
                acc = o if acc is None else acc + o
            store(rows, slice(p * LANES, (p + 1) * LANES), _bf16(acc))


SWA_ROWS = N_HEADS * DEC_SEQ
NEW_ROWS = 2 * SUBLANES


def _swa_sample_kernel(q_ref, sink_ref, kn_ref, vn_ref, kt_ref, vt_ref, o_ref, okt_ref, ovt_ref):
    nb = SWA_BB
    rows = nb * SWA_ROWS
    t = lax.broadcasted_iota(jnp.int32, (rows, WINDOW), 0) % DEC_SEQ
    valid_c = lax.broadcasted_iota(jnp.int32, (rows, WINDOW), 1) > t
    valid_n = (lax.broadcasted_iota(jnp.int32, (rows, NEW_ROWS), 1)
               <= lax.broadcasted_iota(jnp.int32, (rows, NEW_ROWS), 0) % DEC_SEQ)
    sink = jnp.concatenate([sink_ref[:, 0:1]] * nb, axis=0)
    qs = [_bf16(q_ref[b]) for b in range(nb)]
    s_c = jnp.concatenate([_dot(qs[b], _bf16(kt_ref[b])) for b in range(nb)], axis=0)
    s_n = jnp.concatenate([_dot_nt(qs[b], _bf16(kn_ref[b])) for b in range(nb)], axis=0)
    s_c = jnp.where(valid_c, s_c, -jnp.inf)
    s_n = jnp.where(valid_n, s_n, -jnp.inf)
    m = jnp.maximum(jnp.maximum(jnp.max(s_c, axis=1, keepdims=True), jnp.max(s_n, axis=1, keepdims=True)), sink)
    p_c = jnp.exp(s_c - m)
    p_n = jnp.exp(s_n - m)
    rden = 1.0 / (jnp.sum(p_c, axis=1, keepdims=True) + jnp.sum(p_n, axis=1, keepdims=True) + jnp.exp(sink - m))
    p_c, p_n = _bf16(p_c), _bf16(p_n)
    lane = lax.broadcasted_iota(jnp.int32, (KV_WIDTH, WINDOW), 1)
    shift = WINDOW - DEC_SEQ
    zrows = jnp.zeros((KV_WIDTH - NEW_ROWS, KV_WIDTH), jnp.float32)
    for b in range(nb):
        r = slice(b * SWA_ROWS, (b + 1) * SWA_ROWS)
        kt, vt = kt_ref[b], vt_ref[b]
        kn, vn = kn_ref[b], vn_ref[b]
        o_ref[b] = (_dot_nt(p_c[r], _bf16(vt)) + _dot(p_n[r], _bf16(vn))) * rden[r]
        for old, new, dst in ((kt, kn, okt_ref), (vt, vn, ovt_ref)):
            new_cols = pltpu.roll(jnp.concatenate([new, zrows], axis=0).T, shift, axis=1)
            dst[b] = jnp.where(lane >= shift, new_cols, pltpu.roll(old, shift, axis=1))


def _swa_sample(sinks, q, kn, vn, cache_k, cache_v):
    nb = cache_k.shape[0]
    bb = SWA_BB
    groups = N_HEADS // N_KV_HEADS
    qh = q.reshape(nb, DEC_SEQ, N_KV_HEADS, groups, HEAD_DIM).transpose(0, 2, 3, 1, 4)
    qh = qh.reshape(nb, N_KV_HEADS, groups * DEC_SEQ, HEAD_DIM)
    zeros = jnp.zeros_like(qh[:, 0])
    qbd = jnp.concatenate([jnp.concatenate([qh[:, 0], zeros], axis=-1),
                           jnp.concatenate([zeros, qh[:, 1]], axis=-1)], axis=1)
    sink_col = jnp.broadcast_to(jnp.repeat(sinks, DEC_SEQ).reshape(SWA_ROWS, 1), (SWA_ROWS, LANES))
    pad8 = lambda a: jnp.pad(a.reshape(nb, DEC_SEQ, KV_WIDTH), ((0, 0), (0, NEW_ROWS - DEC_SEQ), (0, 0)))
    to_t = lambda c: c.transpose(0, 2, 3, 1).reshape(nb, KV_WIDTH, WINDOW)
    blk = lambda r, w: pl.BlockSpec((bb, r, w), lambda i: (i, 0, 0))
    o, okt, ovt = pl.pallas_call(
        _swa_sample_kernel,
        grid=(nb // bb,),
        in_specs=[blk(SWA_ROWS, KV_WIDTH), pl.BlockSpec((SWA_ROWS, LANES), lambda i: (0, 0)),
                  blk(NEW_ROWS, KV_WIDTH), blk(NEW_ROWS, KV_WIDTH), blk(KV_WIDTH, WINDOW), blk(KV_WIDTH, WINDOW)],
        out_specs=[blk(SWA_ROWS, KV_WIDTH), blk(KV_WIDTH, WINDOW), blk(KV_WIDTH, WINDOW)],
        out_shape=[jax.ShapeDtypeStruct((nb, SWA_ROWS, KV_WIDTH), jnp.float32),
                   jax.ShapeDtypeStruct((nb, KV_WIDTH, WINDOW), jnp.float32),
                   jax.ShapeDtypeStruct((nb, KV_WIDTH, WINDOW), jnp.float32)],
        compiler_params=_params(("arbitrary",)),
        name="swa_sample",
    )(qbd, sink_col, pad8(kn), pad8(vn), to_t(cache_k), to_t(cache_v))
    o = o.reshape(nb, N_KV_HEADS, groups, DEC_SEQ, N_KV_HEADS, HEAD_DIM)
    attn = jnp.stack([o[:, h, :, :, h, :] for h in range(N_KV_HEADS)], axis=1)
    attn = attn.transpose(0, 3, 1, 2, 4).reshape(nb * DEC_SEQ, ATTN_WIDTH)
    from_t = lambda c: c.reshape(nb, N_KV_HEADS, HEAD_DIM, WINDOW).transpose(0, 3, 1, 2)
    return attn, from_t(okt), from_t(ovt)


def _mem_kv_kernel(mem_ref, wk_ref, wv_ref, wq_ref, wo_ref, mk_ref, mv_ref, wqk_ref, wvo_ref):
    mb = _bf16(mem_ref[...])
    mk = _dot(mb, wk_ref[...])
    mv = _dot(mb, wv_ref[...])
    mk_ref[...] = mk
    mv_ref[...] = mv
    mkb, mvb = _bf16(mk), _bf16(mv)
    for h in range(MEM_HEADS):
        sl = slice(h * MEM_HEAD_DIM, (h + 1) * MEM_HEAD_DIM)
        keys = slice(h * N_MEM, (h + 1) * N_MEM)
        wqk_ref[:, keys] = _bf16(_dot_nt(wq_ref[:, sl], mkb[:, sl]) * (MEM_HEAD_DIM ** -0.5))
        wvo_ref[keys, :] = _bf16(_dot(mvb[:, sl], wo_ref[sl, :]))


def _mem_kv(mem, wk_b, wv_b, wq_b, wo_b):
    full = lambda a: pl.BlockSpec(a.shape, lambda i: (0,) * a.ndim)
    blk = pl.BlockSpec((N_MEM, D_MODEL), lambda i: (0, 0))
    f32 = jax.ShapeDtypeStruct((N_MEM, D_MODEL), jnp.float32)
    fused = (D_MODEL, MEM_HEADS * N_MEM), (MEM_HEADS * N_MEM, D_MODEL)
    return pl.pallas_call(
        _mem_kv_kernel,
        grid=(1,),
        in_specs=[full(mem), full(wk_b), full(wv_b), full(wq_b), full(wo_b)],
        out_specs=[blk, blk] + [pl.BlockSpec(shp, lambda i: (0, 0)) for shp in fused],
        out_shape=[f32, f32] + [jax.ShapeDtypeStruct(shp, jnp.bfloat16) for shp in fused],
        compiler_params=_params(("arbitrary",)),
        name="mem_kv",
    )(mem, wk_b, wv_b, wq_b, wo_b)


def _mix_ln1(attn_ref, conv_ref, x_ref, wmix_ref, g1_ref, b1_ref):
    mix = _dot(_bf16(attn_ref[...]), wmix_ref[0:ATTN_WIDTH, :]) + _dot(conv_ref[...], wmix_ref[ATTN_WIDTH:, :])
    return _layer_norm(ALPHA * x_ref[...] + mix, g1_ref[...], b1_ref[...])


def _mem_q(h1, wq_ref):
    return _bf16(_dot(_bf16(h1), wq_ref[...]) * (MEM_HEAD_DIM ** -0.5))


def _route(h2, wrhl_ref, br_ref, tri_ref, carry):
    hi = _bf16(h2)
    lo = _bf16(h2 - hi.astype(jnp.float32))
    hh = _dot(hi, wrhl_ref[...])
    logits = hh[:, 0:LANES] + hh[:, LANES:] + _dot(lo, wrhl_ref[:, 0:LANES]) + br_ref[...]
    lane_i = lax.broadcasted_iota(jnp.int32, logits.shape, 1)
    lane = lane_i.astype(jnp.float32)
    big = jnp.float32(LANES)
    is_g = lane_i < N_GROUPS
    gl = jnp.where(is_g, logits, -jnp.inf)
    gmax = jnp.max(gl, axis=1, keepdims=True)
    gidx = jnp.min(jnp.where(is_g & (logits == gmax), lane, big), axis=1, keepdims=True)
    gsum = jnp.sum(jnp.exp(gl - gmax), axis=1, keepdims=True)
    gw = 1.0 / gsum
    eid = lane_i - N_GROUPS
    assert EXPERTS_PER_GROUP == 8
    grp = lax.shift_right_arithmetic(eid, jnp.full_like(eid, 3)).astype(jnp.float32)
    in_e = (lane_i >= N_GROUPS) & (lane_i < N_GROUPS + N_EXPERTS) & (grp == gidx)
    v1 = jnp.max(jnp.where(in_e, logits, -jnp.inf), axis=1, keepdims=True)
    i1 = jnp.min(jnp.where(in_e & (logits == v1), lane, big), axis=1, keepdims=True)
    rest = in_e & (lane != i1)
    v2 = jnp.max(jnp.where(rest, logits, -jnp.inf), axis=1, keepdims=True)
    i2 = jnp.min(jnp.where(rest & (logits == v2), lane, big), axis=1, keepdims=True)
    ex = jnp.exp(v2 - v1)
    den = 1.0 + ex
    w1 = gw / den
    w2 = gw * ex / den
    zero = jnp.zeros_like(logits)
    pick1 = lane == i1
    pick2 = lane == i2
    sel = jnp.where(pick1 | pick2, 1.0, 0.0)
    before = _dot(tri_ref[...], _bf16(sel)) + carry
    rank1 = jnp.sum(jnp.where(pick1, before, zero), axis=1, keepdims=True)
    rank2 = jnp.sum(jnp.where(pick2, before, zero), axis=1, keepdims=True)
    cols = (i1 - N_GROUPS, i2 - N_GROUPS, w1, w2, rank1, rank2)
    route = zero
    for k, col in enumerate(cols):
        route = jnp.where(lane_i == k, col, route)
    return route, carry + jnp.sum(sel, axis=0, keepdims=True)


def _post_tile(attn, conv_ref, x_ref, wmix_ref, g1_ref, b1_ref, wqk_ref, wvo_ref, g2_ref, b2_ref,
               wrhl_ref, br_ref, tri_ref, h2t_ref, rt_ref, cnt_ref, carry_ref):
    mix = _dot(attn, wmix_ref[0:ATTN_WIDTH, :]) + _dot(conv_ref[...], wmix_ref[ATTN_WIDTH:, :])
    h1 = _layer_norm(ALPHA * x_ref[...] + mix, g1_ref[...], b1_ref[...])
    scores = _dot(_bf16(h1), wqk_ref[...])
    probs = []
    for h in range(MEM_HEADS):
        s = scores[:, h * N_MEM:(h + 1) * N_MEM]
        p = jnp.exp(s - jnp.max(s, axis=1, keepdims=True))
        probs.append(_bf16(p / jnp.sum(p, axis=1, keepdims=True)))
    mem_out = _dot(jnp.concatenate(probs, axis=1), wvo_ref[...])
    h2 = _layer_norm(ALPHA * h1 + mem_out, g2_ref[...], b2_ref[...])
    _store_token_tiles(h2t_ref, h2)
    route, carry = _route(h2, wrhl_ref, br_ref, tri_ref, carry_ref[0:1, :])
    rt_ref[...] = route
    carry_ref[...] = jnp.broadcast_to(carry, carry_ref.shape)
    cnt_ref[...] = jnp.broadcast_to(carry, cnt_ref.shape)


def _swa_post_kernel(sinks_ref, q_ref, kc_ref, kp_ref, vc_ref, vp_ref,
                     conv_ref, x_ref, wmix_ref, g1_ref, b1_ref, wqk_ref, wvo_ref, g2_ref, b2_ref,
                     wrhl_ref, br_ref, tri_ref, h2s_ref, rts_ref, cnts_ref, wg_ref, wu_ref, wd_ref,
                     h2t_ref, rt_ref, cnt_ref, wgb_ref, wub_ref, wdb_ref, carry_ref, attn_s):
    t = pl.program_id(0)
    steps = N_PROMPT // TM_POST
    par = t % 2

    def swa(slot):
        def store(rows, cols, val):
            attn_s[slot, rows, cols] = val
        _swa_tile(t, sinks_ref, q_ref, kc_ref, kp_ref, vc_ref, vp_ref, store)

    def post(slot):
        _post_tile(attn_s[slot], conv_ref, x_ref, wmix_ref, g1_ref, b1_ref, wqk_ref, wvo_ref, g2_ref, b2_ref,
                   wrhl_ref, br_ref, tri_ref, h2t_ref, rt_ref, cnt_ref, carry_ref)

    def cast_expert():
        wgb_ref[...] = _bf16(wg_ref[...])
        wub_ref[...] = _bf16(wu_ref[...])
        wdb_ref[...] = _bf16(wd_ref[...])

    @pl.when(t == 0)
    def _():
        carry_ref[...] = cnts_ref[...]
        swa(0)
        cast_expert()

    @pl.when((t >= 1) & (t < steps))
    def _():
        swa(par)
        post(1 - par)
        cast_expert()

    @pl.when(t == steps)
    def _():
        post(1 - par)

    @pl.when(t == steps + 1)
    def _():
        h2t_ref[...] = h2s_ref[...]
        rt_ref[...] = rts_ref[...]


def _swa_post(sinks, q, kx, vx, conv, x, wmix_b, g1, b1, wqk, wvo, g2, b2, wrhl, br, tri, h2t_s, rt_s, cnt_s,
              w_gate, w_up, w_down):
    n = x.shape[0]
    tm = TM_POST
    assert tm == SWA_QB * WINDOW
    steps = n // tm
    assert N_EXPERTS <= steps
    expert = lambda a: pl.BlockSpec((1,) + a.shape[1:], lambda i: (jnp.minimum(i, N_EXPERTS - 1), 0, 0))
    experts = (w_gate, w_up, w_down)
    cur = lambda w: pl.BlockSpec((tm, w), lambda i: (jnp.minimum(i, steps - 1), 0))
    prev = lambda w: pl.BlockSpec((WINDOW, w), lambda i: (jnp.clip(SWA_QB * i - 1, 0, n // WINDOW - 1), 0))
    lag = lambda w: pl.BlockSpec((tm, w), lambda i: (jnp.clip(i - 1, 0, steps - 1), 0))
    full = lambda a: pl.BlockSpec(a.shape, lambda i: (0,) * a.ndim)
    weights = (wmix_b, g1, b1, wqk, wvo, g2, b2, wrhl, br, tri, h2t_s, rt_s, cnt_s)
    n_out = n + h2t_s.shape[0] // ROW_CHUNKS
    out_idx = lambda i: (jnp.where(i > steps, steps, jnp.clip(i - 1, 0, steps - 1)), 0)
    return pl.pallas_call(
        _swa_post_kernel,
        grid=(steps + 2,),
        in_specs=([pl.BlockSpec(memory_space=pltpu.SMEM), cur(ATTN_WIDTH),
                   cur(4 * LANES), prev(4 * LANES), cur(4 * LANES), prev(4 * LANES),
                   lag(CONV_CH), lag(D_MODEL)] + [full(a) for a in weights] + [expert(a) for a in experts]),
        out_specs=[pl.BlockSpec((tm * ROW_CHUNKS, LANES), out_idx),
                   pl.BlockSpec((tm, LANES), out_idx),
                   pl.BlockSpec((SUBLANES, LANES), lambda i: (0, 0))] + [expert(a) for a in experts],
        out_shape=[jax.ShapeDtypeStruct((n_out * ROW_CHUNKS, LANES), jnp.float32),
                   jax.ShapeDtypeStruct((n_out, LANES), jnp.float32),
                   jax.ShapeDtypeStruct((SUBLANES, LANES), jnp.float32)]
                  + [jax.ShapeDtypeStruct(a.shape, jnp.bfloat16) for a in experts],
        scratch_shapes=[pltpu.VMEM((SUBLANES, LANES), jnp.float32),
                        pltpu.VMEM((2, tm, ATTN_WIDTH), jnp.bfloat16)],
        compiler_params=_params(("arbitrary",)),
        name="swa_post_prompt",
    )(sinks, q, kx, kx, vx, vx, conv, x, *weights, *experts)


def _post_a_sample_kernel(attn_ref, conv_ref, x_ref, wmix_ref, g1_ref, b1_ref, wq_ref, h1_ref, qm_ref):
    h1 = _mix_ln1(attn_ref, conv_ref, x_ref, wmix_ref, g1_ref, b1_ref)
    h1_ref[...] = h1
    qm_ref[...] = _mem_q(h1, wq_ref).astype(jnp.float32)


def _post_a_sample(attn, conv, x, wmix_b, g1, b1, wq_b):
    n = x.shape[0]
    args = (attn, conv, x, wmix_b, g1, b1, wq_b)
    full = lambda a: pl.BlockSpec(a.shape, lambda i: (0,) * a.ndim)
    blk = pl.BlockSpec((n, D_MODEL), lambda i: (0, 0))
    return pl.pallas_call(
        _post_a_sample_kernel,
        grid=(1,),
        in_specs=[full(a) for a in args],
        out_specs=[blk, blk],
        out_shape=[jax.ShapeDtypeStruct((n, D_MODEL), jnp.float32),
                   jax.ShapeDtypeStruct((n, D_MODEL), jnp.float32)],
        compiler_params=_params(("arbitrary",)),
        name="post_a_sample",
    )(*args)


MEM_ROWS = MEM_HEADS * DEC_SEQ


def _mem_attn_sample_body(q_ref, mk_ref, mv_ref, o_ref):
    nk = N_MEM * MEM_HEADS
    nb = SAMPLE_BB
    rows = nb * MEM_ROWS
    row_h = (lax.broadcasted_iota(jnp.int32, (rows, nk), 0) % MEM_ROWS) // DEC_SEQ
    key_h = lax.broadcasted_iota(jnp.int32, (rows, nk), 1) % MEM_HEADS
    s = jnp.concatenate(
        [_dot_nt(_bf16(q_ref[b]), _bf16(mk_ref[b].reshape(nk, MEM_HEAD_DIM))) for b in range(nb)], axis=0)
    s = jnp.where(row_h == key_h, s, -jnp.inf)
    p = jnp.exp(s - jnp.max(s, axis=1, keepdims=True))
    rden = 1.0 / jnp.sum(p, axis=1, keepdims=True)
    p = _bf16(p)
    for b in range(nb):
        r = slice(b * MEM_ROWS, (b + 1) * MEM_ROWS)
        o_ref[b] = _dot(p[r], _bf16(mv_ref[b].reshape(nk, MEM_HEAD_DIM))) * rden[r]


def _proj_mem_kernel(x_ref, w_ref, tab_ref, cw_ref, mq_ref, mk_ref, mv_ref,
                     q_ref, kx_ref, vx_ref, conv_ref, ktail_ref, vtail_ref, utail_ref, mo_ref, carry_ref):
    _mem_attn_sample_body(mq_ref, mk_ref, mv_ref, mo_ref)
    _proj_prompt_body(x_ref, w_ref, tab_ref, cw_ref,
                      q_ref, kx_ref, vx_ref, conv_ref, ktail_ref, vtail_ref, utail_ref, carry_ref)


def _proj_prompt_mem_sample(x, w_in_b, tab, conv_w, qm, mk, mv):
    n = x.shape[0]
    nb = mk.shape[0]
    bb = SAMPLE_BB
    steps = nb // bb
    tm = n // steps
    row = lambda w: pl.BlockSpec((tm, w), lambda i: (i, 0))
    full = lambda a: pl.BlockSpec(a.shape, lambda i: (0,) * a.ndim)
    const = lambda r, w: pl.BlockSpec((r, w), lambda i: (0, 0))
    mq = qm.reshape(nb, DEC_SEQ, MEM_HEADS, MEM_HEAD_DIM).transpose(0, 2, 1, 3).reshape(nb, MEM_ROWS, MEM_HEAD_DIM)
    mrows = pl.BlockSpec((bb, MEM_ROWS, MEM_HEAD_DIM), lambda i: (i, 0, 0))
    kv = pl.BlockSpec((bb, N_MEM, MEM_HEADS, MEM_HEAD_DIM), lambda i: (i, 0, 0, 0))
    outs = pl.pallas_call(
        _proj_mem_kernel,
        grid=(steps,),
        in_specs=[row(D_MODEL), full(w_in_b), pl.BlockSpec((ROPE_ROWS, tm), lambda i: (0, i)), full(conv_w),
                  mrows, kv, kv],
        out_specs=[row(ATTN_WIDTH), row(4 * LANES), row(4 * LANES), row(CONV_CH),
                   const(WINDOW, KV_WIDTH), const(WINDOW, KV_WIDTH), const(SUBLANES, CONV_CH), mrows],
        out_shape=[jax.ShapeDtypeStruct((n, ATTN_WIDTH), jnp.bfloat16),
                   jax.ShapeDtypeStruct((n, 4 * LANES), jnp.bfloat16),
                   jax.ShapeDtypeStruct((n, 4 * LANES), jnp.bfloat16),
                   jax.ShapeDtypeStruct((n, CONV_CH), jnp.bfloat16),
                   jax.ShapeDtypeStruct((WINDOW, KV_WIDTH), jnp.float32),
                   jax.ShapeDtypeStruct((WINDOW, KV_WIDTH), jnp.float32),
                   jax.ShapeDtypeStruct((SUBLANES, CONV_CH), jnp.float32),
                   jax.ShapeDtypeStruct((nb, MEM_ROWS, MEM_HEAD_DIM), jnp.float32)],
        scratch_shapes=[pltpu.VMEM((SUBLANES, CONV_CH), jnp.float32)],
        compiler_params=_params(("arbitrary",)),
        name="proj_prompt_mem_sample",
    )(x, w_in_b, tab, conv_w, mq, mk, mv)
    o = outs[7].reshape(nb, MEM_HEADS, DEC_SEQ, MEM_HEAD_DIM).transpose(0, 2, 1, 3).reshape(nb * DEC_SEQ, D_MODEL)
    return outs[:7], o


def _post_b_sample_kernel(o_ref, h1_ref, wo_ref, g2_ref, b2_ref, wrhl_ref, br_ref, tri_ref,
                          h2t_ref, rt_ref, cnt_ref):
    h2 = _layer_norm(ALPHA * h1_ref[...] + _dot(_bf16(o_ref[...]), wo_ref[...]), g2_ref[...], b2_ref[...])
    _store_token_tiles(h2t_ref, h2)
    route, carry = _route(h2, wrhl_ref, br_ref, tri_ref, jnp.zeros((1, LANES), jnp.float32))
    rt_ref[...] = route
    cnt_ref[...] = jnp.broadcast_to(carry, cnt_ref.shape)


def _post_b_sample(o, h1, wo_b, g2, b2, wrhl, br, tri):
    n = h1.shape[0]
    args = (o, h1, wo_b, g2, b2, wrhl, br, tri)
    full = lambda a: pl.BlockSpec(a.shape, lambda i: (0,) * a.ndim)
    return pl.pallas_call(
        _post_b_sample_kernel,
        grid=(1,),
        in_specs=[full(a) for a in args],
        out_specs=[pl.BlockSpec((n * ROW_CHUNKS, LANES), lambda i: (0, 0)),
                   pl.BlockSpec((n, LANES), lambda i: (0, 0)),
                   pl.BlockSpec((SUBLANES, LANES), lambda i: (0, 0))],
        out_shape=[jax.ShapeDtypeStruct((n * ROW_CHUNKS, LANES), jnp.float32),
                   jax.ShapeDtypeStruct((n, LANES), jnp.float32),
                   jax.ShapeDtypeStruct((SUBLANES, LANES), jnp.float32)],
        compiler_params=_params(("arbitrary",)),
        name="post_b_sample",
    )(*args)


def _row_gather_copy(src_hbm, idx, dst, dst_row, sem):
    s0 = pl.multiple_of(idx * ROW_CHUNKS, ROW_CHUNKS)
    d0 = pl.multiple_of(dst_row * ROW_CHUNKS, ROW_CHUNKS)
    return pltpu.make_async_copy(src_hbm.at[pl.ds(s0, ROW_CHUNKS), :], dst.at[pl.ds(d0, ROW_CHUNKS), :], sem)


def _dispatch_kernel(pos_ref, h2t_ref, xs_hbm, sem):
    def body(r, c):
        src = h2t_ref.at[pl.ds(pl.multiple_of(r * ROW_CHUNKS, ROW_CHUNKS), ROW_CHUNKS), :]
        for k in range(2):
            d0 = pl.multiple_of(pos_ref[0, 0, k * TM_COMB + r] * ROW_CHUNKS, ROW_CHUNKS)
            pltpu.make_async_copy(src, xs_hbm.at[pl.ds(d0, ROW_CHUNKS), :], sem.at[0]).start(priority=k)
        return c
    lax.fori_loop(0, TM_COMB, body, 0, unroll=8)
    for _ in range(2):
        pltpu.make_async_copy(h2t_ref, xs_hbm.at[pl.ds(0, TM_COMB * ROW_CHUNKS), :], sem.at[0]).wait()


def _dispatch(pos3, h2t):
    nt = N_ALL // TM_COMB
    return pl.pallas_call(
        _dispatch_kernel,
        grid=(nt,),
        in_specs=[pl.BlockSpec((1, 1, 2 * TM_COMB), lambda i: (i, 0, 0), memory_space=pltpu.SMEM),
                  pl.BlockSpec((TM_COMB * ROW_CHUNKS, LANES), lambda i: (i, 0))],
        out_specs=pl.BlockSpec(memory_space=pl.ANY),
        out_shape=jax.ShapeDtypeStruct((N_ASSIGN * ROW_CHUNKS, LANES), jnp.float32),
        scratch_shapes=[pltpu.SemaphoreType.DMA((1,))],
        compiler_params=_params(("arbitrary",)),
        name="moe_dispatch",
    )(pos3, h2t)


def _moe_ffn_kernel(it_ref, ie_ref, lo_ref, hi_ref, *refs):
    x_refs = refs[:MOE_X_COPIES]
    wg_ref, wu_ref, wd_ref, y_ref = refs[MOE_X_COPIES:]
    i = pl.program_id(0)
    lo = lo_ref[i]
    hi = hi_ref[i]
    piece = TM_MOE // MOE_X_COPIES

    def ffn(r0, rows):
        x = _bf16(jnp.concatenate([_load_token_tiles(x_refs[p], 0, piece)
                                   for p in range(r0 // piece, (r0 + rows) // piece)], axis=0))
        hg = _dot(x, wg_ref[0])
        hu = _dot(x, wu_ref[0])
        h = hg / (1.0 + jnp.exp(-hg)) * hu
        return _dot(_bf16(h), wd_ref[0])

    whole = (lo == 0) & (hi == TM_MOE)

    @pl.when(whole)
    def _():
        _store_token_tiles(y_ref, ffn(0, TM_MOE))

    for r0 in range(0, TM_MOE, MOE_BLOCK):
        live = (hi > lo) & jnp.logical_not(whole) & (hi > r0) & (lo < r0 + MOE_BLOCK)

        def store(merge, r0=r0):
            y = ffn(r0, MOE_BLOCK)
            row = r0 + lax.broadcasted_iota(jnp.int32, (MOE_BLOCK, LANES), 0)
            mask = (row >= lo) & (row < hi)
            for c in range(ROW_CHUNKS):
                sl = pl.ds(r0 * ROW_CHUNKS + c, MOE_BLOCK, stride=ROW_CHUNKS)
                y_ref[sl, :] = jnp.where(mask, y[:, c * LANES:(c + 1) * LANES], y_ref[sl, :] if merge else 0.0)

        pl.when(live & (lo <= r0))(functools.partial(store, False))
        pl.when(live & (lo > r0))(functools.partial(store, True))


def _moe_ffn(item_tile, item_expert, item_lo, item_hi, x_sorted, w_gate, w_up, w_down):
    wspec = lambda shp: pl.BlockSpec((1,) + shp, lambda i, it, ie, lo, hi: (ie[i], 0, 0))
    tile = pl.BlockSpec((TM_MOE * ROW_CHUNKS, LANES), lambda i, it, ie, lo, hi: (it[i], 0))
    piece = lambda p: pl.BlockSpec((TM_MOE // MOE_X_COPIES * ROW_CHUNKS, LANES),
                                   lambda i, it, ie, lo, hi: (it[i] * MOE_X_COPIES + p, 0))
    grid_spec = pltpu.PrefetchScalarGridSpec(
        num_scalar_prefetch=4,
        grid=(MOE_ITEMS,),
        in_specs=[piece(p) for p in range(MOE_X_COPIES)] + [
            wspec((D_MODEL, EXPERT_FF)), wspec((D_MODEL, EXPERT_FF)), wspec((EXPERT_FF, D_MODEL))],
        out_specs=tile,
    )
    return pl.pallas_call(
        _moe_ffn_kernel,
        grid_spec=grid_spec,
        out_shape=jax.ShapeDtypeStruct((N_ASSIGN * ROW_CHUNKS, LANES), jnp.float32),
        compiler_params=_params(("arbitrary",)),
        name="moe_ffn",
    )(item_tile, item_expert, item_lo, item_hi, *([x_sorted] * MOE_X_COPIES), w_gate, w_up, w_down)


def _combine_kernel(nt, pos_cur_ref, pos_nxt_ref, yt_hbm, h2t_ref, rt_ref, g3_ref, b3_ref, o_ref, abuf, sem):
    t = pl.program_id(0)
    slot = t % 2
    rows = 2 * TM_COMB

    def issue(pos_ref, s):
        def body(j, c):
            for k in range(2):
                r = 2 * j + k
                _row_gather_copy(yt_hbm, pos_ref[0, 0, r], abuf, s * rows + r, sem.at[s]).start(priority=k)
            return c
        lax.fori_loop(0, rows // 2, body, 0, unroll=16)

    @pl.when(t == 0)
    def _():
        issue(pos_cur_ref, 0)

    @pl.when(t + 1 < nt)
    def _():
        issue(pos_nxt_ref, 1 - slot)

    base = pl.multiple_of(slot * (rows * ROW_CHUNKS), rows * ROW_CHUNKS)
    pltpu.make_async_copy(yt_hbm.at[pl.ds(0, rows * ROW_CHUNKS), :],
                          abuf.at[pl.ds(base, rows * ROW_CHUNKS), :], sem.at[slot]).wait()
    ya = _load_token_tiles(abuf, base, TM_COMB)
    yb = _load_token_tiles(abuf, base + TM_COMB * ROW_CHUNKS, TM_COMB)
    rt = rt_ref[...]
    ff = rt[:, 2:3] * ya + rt[:, 3:4] * yb
    h2 = _load_token_tiles(h2t_ref, 0, TM_COMB)
    o_ref[...] = _layer_norm(ALPHA * h2 + ff, g3_ref[...], b3_ref[...])


def _combine(pos3, yt, h2t, rt, g3, b3, tile0, n_tiles):
    last = tile0 + n_tiles - 1
    smem_pos = lambda f: pl.BlockSpec((1, 1, 2 * TM_COMB), f, memory_space=pltpu.SMEM)
    full = lambda a: pl.BlockSpec(a.shape, lambda i: (0,) * a.ndim)
    return pl.pallas_call(
        functools.partial(_combine_kernel, n_tiles),
        grid=(n_tiles,),
        in_specs=[smem_pos(lambda i: (tile0 + i, 0, 0)),
                  smem_pos(lambda i: (jnp.minimum(tile0 + i + 1, last), 0, 0)),
                  pl.BlockSpec(memory_space=pl.ANY),
                  pl.BlockSpec((TM_COMB * ROW_CHUNKS, LANES), lambda i: (tile0 + i, 0)),
                  pl.BlockSpec((TM_COMB, LANES), lambda i: (tile0 + i, 0)),
                  full(g3), full(b3)],
        out_specs=pl.BlockSpec((TM_COMB, D_MODEL), lambda i: (i, 0)),
        out_shape=jax.ShapeDtypeStruct((n_tiles * TM_COMB, D_MODEL), jnp.float32),
        scratch_shapes=[pltpu.VMEM((2 * 2 * TM_COMB * ROW_CHUNKS, LANES), jnp.float32),
                        pltpu.SemaphoreType.DMA((2,))],
        compiler_params=_params(("arbitrary",)),
        name="moe_combine",
    )(pos3, pos3, yt, h2t, rt, g3, b3)


POS_TILES = 11


def _positions_kernel(rt_ref, starts_ref, pos_ref):
    expert = lax.broadcasted_iota(jnp.int32, (N_EXPERTS, TM_COMB), 0).astype(jnp.float32)
    starts = jnp.concatenate([starts_ref[...]] * (TM_COMB // LANES), axis=1)
    for j in range(POS_TILES):
        cols = rt_ref[j * TM_COMB:(j + 1) * TM_COMB, :].T
        out = []
        for k in range(2):
            seg = jnp.sum(jnp.where(expert == cols[k:k + 1, :], starts, 0.0), axis=0, keepdims=True)
            out.append(seg + cols[4 + k:5 + k, :])
        pos_ref[j] = jnp.concatenate(out, axis=1).astype(jnp.int32)


def _positions(rt, starts_rep):
    nt = N_ALL // TM_COMB
    assert nt % POS_TILES == 0
    return pl.pallas_call(
        _positions_kernel,
        grid=(nt // POS_TILES,),
        in_specs=[pl.BlockSpec((POS_TILES * TM_COMB, LANES), lambda i: (i, 0)),
                  pl.BlockSpec((N_EXPERTS, LANES), lambda i: (0, 0))],
        out_specs=pl.BlockSpec((POS_TILES, 1, 2 * TM_COMB), lambda i: (i, 0, 0)),
        out_shape=jax.ShapeDtypeStruct((nt, 1, 2 * TM_COMB), jnp.int32),
        compiler_params=_params(("arbitrary",)),
        name="moe_positions",
    )(rt, starts_rep)


def _routing_plan(rt, cnt):
    i32 = jnp.int32
    counts_f = cnt[0, N_GROUPS:N_GROUPS + N_EXPERTS]
    starts_f = jnp.cumsum(counts_f) - counts_f
    pos3 = _positions(rt, jnp.broadcast_to(starts_f[:, None], (N_EXPERTS, LANES)))
    starts = starts_f.astype(i32)
    tiles = jnp.arange(MOE_TILES, dtype=i32) * TM_MOE
    rank_t = jnp.arange(MOE_TILES, dtype=i32) + jnp.sum((starts[None, :] < tiles[:, None]).astype(i32), axis=1)
    rank_s = jnp.arange(N_EXPERTS, dtype=i32) + jnp.sum((tiles[None, :] <= starts[:, None]).astype(i32), axis=1)
    vals = jnp.concatenate([tiles, starts])
    ranks = jnp.concatenate([rank_t, rank_s])
    slot = jnp.arange(MOE_ITEMS, dtype=i32)
    lo = jnp.sum(jnp.where(ranks[None, :] == slot[:, None], vals[None, :], 0), axis=1)
    hi = jnp.concatenate([lo[1:], jnp.full((1,), N_ASSIGN, i32)])
    item_tile = jnp.minimum(lo // TM_MOE, MOE_TILES - 1)
    item_expert = jnp.clip(jnp.sum((starts[None, :] <= lo[:, None]).astype(i32), axis=1) - 1, 0, N_EXPERTS - 1)
    base = item_tile * TM_MOE
    return item_tile, item_expert, lo - base, hi - base, pos3


def kernel(x_prompt, x_sample, mem_prompt, cache_swa_k, cache_swa_v, cache_conv, cache_mem_k, cache_mem_v,
           w_in, sinks, conv_w, w_mix_out, ln1_g, ln1_b, w_q_mem, w_k_mem, w_v_mem, w_o_mem, ln2_g, ln2_b,
           w_router_group, b_router_group, w_router_expert, b_router_expert, w_gate, w_up, w_down,
           ln3_g, ln3_b):
    f32 = jnp.float32
    row = lambda a: a.reshape(1, -1).astype(f32)
    w_in_b, wmix_b, wq_b, wk_b, wv_b, wo_b = (_bf16(w) for w in (w_in, w_mix_out, w_q_mem, w_k_mem, w_v_mem, w_o_mem))
    g1, b1, g2, b2, g3, b3 = (row(a) for a in (ln1_g, ln1_b, ln2_g, ln2_b, ln3_g, ln3_b))
    pad = LANES - N_GROUPS - N_EXPERTS
    wr = jnp.concatenate([w_router_group, w_router_expert, jnp.zeros((D_MODEL, pad), f32)], axis=1)
    wrh = _bf16(wr)
    wrhl = jnp.concatenate([wrh, _bf16(wr - wrh.astype(f32))], axis=1)
    br = jnp.concatenate([b_router_group, b_router_expert, jnp.zeros((pad,), f32)]).reshape(1, LANES)

    xs = x_sample.reshape(N_SAMPLE, D_MODEL)
    tab_s = jnp.tile(_rope_table(PAST_LEN + jnp.arange(DEC_SEQ)), (1, DEC_BATCH))
    c0 = jnp.repeat(cache_conv[:, 0], DEC_SEQ, axis=0)
    c1 = jnp.repeat(cache_conv[:, 1], DEC_SEQ, axis=0)
    q_s, k_s, v_s, conv_s, u_s = _proj_sample(xs, w_in_b, tab_s, conv_w, c0, c1)
    attn_s, swa_k_s, swa_v_s = _swa_sample(sinks, q_s, k_s, v_s, cache_swa_k, cache_swa_v)
    h1_s, qm_s = _post_a_sample(attn_s, conv_s, xs, wmix_b, g1, b1, wq_b)
    xp = x_prompt.reshape(N_PROMPT, D_MODEL)
    tab_p = _rope_table(jnp.arange(N_PROMPT))
    (q_p, kx_p, vx_p, conv_p, k_tail, v_tail, u_tail), o_s = _proj_prompt_mem_sample(
        xp, w_in_b, tab_p, conv_w, qm_s, cache_mem_k, cache_mem_v)
    tri = _bf16(jnp.tril(jnp.ones((TM_POST, TM_POST), f32), -1))
    h2t_s, rt_s, cnt_s = _post_b_sample(o_s, h1_s, wo_b, g2, b2, wrhl, br, tri)

    mk, mv, wqk, wvo = _mem_kv(mem_prompt.reshape(N_MEM, D_MODEL), wk_b, wv_b, wq_b, wo_b)
    h2t, rt, cnt, wg_b, wu_b, wd_b = _swa_post(sinks, q_p, kx_p, vx_p, conv_p, xp, wmix_b, g1, b1, wqk, wvo, g2, b2,
                                               wrhl, br, tri, h2t_s, rt_s, cnt_s, w_gate, w_up, w_down)

    item_tile, item_expert, item_lo, item_hi, pos3 = _routing_plan(rt, cnt)
    x_sorted = _dispatch(pos3, h2t)
    yt = _moe_ffn(item_tile, item_expert, item_lo, item_hi, x_sorted, wg_b, wu_b, wd_b)
    y_p = _combine(pos3, yt, h2t, rt, g3, b3, 0, N_PROMPT // TM_COMB)
    y_s = _combine(pos3, yt, h2t, rt, g3, b3, N_PROMPT // TM_COMB, N_SAMPLE // TM_COMB)

    return (y_p.reshape(1, SEQ, D_MODEL),
            y_s.reshape(DEC_BATCH, DEC_SEQ, D_MODEL),
            k_tail.reshape(1, WINDOW, N_KV_HEADS, HEAD_DIM),
            v_tail.reshape(1, WINDOW, N_KV_HEADS, HEAD_DIM),
            u_tail[SUBLANES - (CONV_K - 1):].reshape(1, CONV_K - 1, CONV_CH),
            mk.reshape(1, N_MEM, MEM_HEADS, MEM_HEAD_DIM),
            mv.reshape(1, N_MEM, MEM_HEADS, MEM_HEAD_DIM),
            swa_k_s.reshape(DEC_BATCH, WINDOW, N_KV_HEADS, HEAD_DIM),
            swa_v_s.reshape(DEC_BATCH, WINDOW, N_KV_HEADS, HEAD_DIM),
            u_s.reshape(DEC_BATCH, DEC_SEQ, CONV_CH)[:, DEC_SEQ - (CONV_K - 1):])
```

```python
import functools

import jax
import jax.numpy as jnp
from jax import lax
from jax.experimental import pallas as pl
from jax.experimental.pallas import tpu as pltpu

D_MODEL = 1024
SEQ = 16384
DEC_BATCH = 128
DEC_SEQ = 4
PAST_LEN = 16384
ATTN_WIDTH = 512
CONV_CH = 512
HEAD_DIM = 64
N_HEADS = 8
N_KV_HEADS = 2
KV_WIDTH = 128
WINDOW = 128
ROPE_THETA = 500000.0
ROPE_DIM = 16
CONV_K = 3
Q_END = ATTN_WIDTH
K_END = Q_END + KV_WIDTH
V_END = K_END + KV_WIDTH
B_END = V_END + CONV_CH
C_END = B_END + CONV_CH
IN_WIDTH = C_END + CONV_CH
N_MEM = 256
MEM_HEADS = 4
MEM_HEAD_DIM = 256
N_GROUPS = 4
EXPERTS_PER_GROUP = 8
N_EXPERTS = 32
EXPERT_FF = 256
ALPHA = 2.0 ** 0.25
LN_EPS = 1e-5

LANES = 128
SUBLANES = 8
ROW_CHUNKS = D_MODEL // LANES
VMEM_LIMIT = 56 * 1024 * 1024

N_PROMPT = SEQ
N_SAMPLE = DEC_BATCH * DEC_SEQ
N_ALL = N_PROMPT + N_SAMPLE
TM_POST = 512
TM_MOE = 512
TM_COMB = 512
MOE_BLOCK = 256
MOE_X_SLOTS = 3
N_ASSIGN = 2 * N_ALL
MOE_TILES = N_ASSIGN // TM_MOE
MOE_ITEMS = MOE_TILES + N_EXPERTS
SAMPLE_BB = 4
SWA_BB = 16

assert ROW_CHUNKS == SUBLANES
assert N_SAMPLE == TM_POST
assert N_ASSIGN % TM_MOE == 0 and N_ALL % TM_COMB == 0


def _params(sem, vmem=VMEM_LIMIT):
    return pltpu.CompilerParams(dimension_semantics=sem, vmem_limit_bytes=vmem)


def _bf16(x):
    return x.astype(jnp.bfloat16)


def _dot(a, b):
    return jnp.dot(a, b, preferred_element_type=jnp.float32)


def _dot_nt(a, b):
    return lax.dot_general(a, b, (((1,), (1,)), ((), ())), preferred_element_type=jnp.float32)


def _layer_norm(x, g, b):
    mu = jnp.mean(x, axis=-1, keepdims=True)
    xc = x - mu
    var = jnp.mean(xc * xc, axis=-1, keepdims=True)
    return xc * lax.rsqrt(var + LN_EPS) * g + b


def _rope(x, cos_t, sin_t):
    lane = lax.broadcasted_iota(jnp.int32, x.shape, 1) % HEAD_DIM
    half = ROPE_DIM // 2
    partner = jnp.where(lane < half, pltpu.roll(x, LANES - half, axis=1), pltpu.roll(x, half, axis=1))
    return x * cos_t + partner * sin_t


def _head_slabs(x):
    lane = lax.broadcasted_iota(jnp.int32, x.shape, 1)
    lo = lane < HEAD_DIM
    sw = pltpu.roll(x, HEAD_DIM, axis=1)
    zero = jnp.zeros_like(x)
    slabs = [jnp.where(lo, x, zero), jnp.where(lo, zero, sw), jnp.where(lo, sw, zero), jnp.where(lo, zero, x)]
    return _bf16(jnp.concatenate(slabs, axis=1))


def _store_token_tiles(ref, val):
    rows = val.shape[0]
    for c in range(ROW_CHUNKS):
        ref[pl.ds(c, rows, stride=ROW_CHUNKS), :] = val[:, c * LANES:(c + 1) * LANES]


def _load_token_tiles(ref, base, rows):
    return jnp.concatenate(
        [ref[pl.ds(base + c, rows, stride=ROW_CHUNKS), :] for c in range(ROW_CHUNKS)], axis=1)


ROPE_ONE = 3 * (ROPE_DIM // 2)
ROPE_ROWS = 32


def _rope_patterns(tab):
    half = ROPE_DIM // 2
    m = lax.broadcasted_iota(jnp.int32, tab.shape, 1) % HEAD_DIM
    idx_c = jnp.where(m < ROPE_DIM, m % half, ROPE_ONE)
    idx_s = jnp.where(m < half, 2 * half + m, jnp.where(m < ROPE_DIM, m, ROPE_ONE + 1))
    return jnp.take_along_axis(tab, idx_c, axis=1), jnp.take_along_axis(tab, idx_s, axis=1)


def _proj_common(x_ref, w_ref, tab_ref):
    xb = _bf16(x_ref[...])
    tab = tab_ref[...]
    pad = jnp.zeros((LANES - tab.shape[0], tab.shape[1]), jnp.float32)
    cos_t, sin_t = _rope_patterns(jnp.concatenate([tab, pad], axis=0).T)
    q = _dot(xb, w_ref[:, 0:Q_END])
    q_rot = jnp.concatenate(
        [_rope(q[:, p * LANES:(p + 1) * LANES], cos_t, sin_t) for p in range(ATTN_WIDTH // LANES)], axis=1)
    q_out = _bf16(q_rot * (HEAD_DIM ** -0.5))
    kv = _dot(xb, w_ref[:, Q_END:V_END])
    k = _rope(kv[:, 0:KV_WIDTH], cos_t, sin_t)
    v = kv[:, KV_WIDTH:]
    bg = _dot(xb, w_ref[:, V_END:B_END])
    u = _dot(xb, w_ref[:, B_END:C_END]) * _dot(xb, w_ref[:, C_END:IN_WIDTH])
    return q_out, k, v, bg, u


def _conv3(bg, u, u1, u2, cw_ref):
    cw = cw_ref[...]
    return bg * (cw[0:1, :] * u2 + cw[1:2, :] * u1 + cw[2:3, :] * u)


def _proj_prompt_body(x_ref, w_ref, tab_ref, cw_ref,
                      q_ref, kx_ref, vx_ref, conv_ref, ktail_ref, vtail_ref, utail_ref, carry_ref):
    @pl.when(pl.program_id(0) == 0)
    def _():
        carry_ref[...] = jnp.zeros_like(carry_ref)

    q_out, k, v, bg, u = _proj_common(x_ref, w_ref, tab_ref)
    tm = u.shape[0]
    ext = jnp.concatenate([carry_ref[...], u], axis=0)
    u1 = pltpu.roll(ext, 1, axis=0)[SUBLANES:SUBLANES + tm]
    u2 = pltpu.roll(ext, 2, axis=0)[SUBLANES:SUBLANES + tm]
    q_ref[...] = q_out
    kx_ref[...] = _head_slabs(k)
    vx_ref[...] = _head_slabs(v)
    conv_ref[...] = _bf16(_conv3(bg, u, u1, u2, cw_ref))
    ktail_ref[...] = k[tm - WINDOW:tm]
    vtail_ref[...] = v[tm - WINDOW:tm]
    utail_ref[...] = u[tm - SUBLANES:tm]
    carry_ref[...] = u[tm - SUBLANES:tm]


def _proj_sample_kernel(x_ref, w_ref, tab_ref, cw_ref, c0_ref, c1_ref,
                        q_ref, k_ref, v_ref, conv_ref, u_ref):
    q_out, k, v, bg, u = _proj_common(x_ref, w_ref, tab_ref)
    t = lax.broadcasted_iota(jnp.int32, u.shape, 0) % DEC_SEQ
    c0 = c0_ref[...]
    c1 = c1_ref[...]
    u1 = jnp.where(t >= 1, pltpu.roll(u, 1, axis=0), c1)
    u2 = jnp.where(t >= 2, pltpu.roll(u, 2, axis=0), jnp.where(t == 1, c1, c0))
    q_ref[...] = q_out.astype(jnp.float32)
    k_ref[...] = k
    v_ref[...] = v
    conv_ref[...] = _bf16(_conv3(bg, u, u1, u2, cw_ref))
    u_ref[...] = u


def _rope_table(pos):
    half = ROPE_DIM // 2
    inv = ROPE_THETA ** (-jnp.arange(0, ROPE_DIM, 2, dtype=jnp.float32) / ROPE_DIM)
    ang = pos.astype(jnp.float32)[None, :] * inv[:, None]
    cos, sin = jnp.cos(ang), jnp.sin(ang)
    n = pos.shape[0]
    assert ROPE_ONE == 3 * half
    return jnp.concatenate([cos, sin, -sin, jnp.ones((1, n), jnp.float32),
                            jnp.zeros((ROPE_ROWS - ROPE_ONE - 1, n), jnp.float32)], axis=0)


def _proj_sample(x, w_in_b, tab, conv_w, c0, c1):
    n = x.shape[0]
    full = lambda a: pl.BlockSpec(a.shape, lambda i: (0,) * a.ndim)
    out = lambda w, dt: jax.ShapeDtypeStruct((n, w), dt)
    blk = lambda w: pl.BlockSpec((n, w), lambda i: (0, 0))
    return pl.pallas_call(
        _proj_sample_kernel,
        grid=(1,),
        in_specs=[full(x), full(w_in_b), full(tab), full(conv_w), full(c0), full(c1)],
        out_specs=[blk(ATTN_WIDTH), blk(KV_WIDTH), blk(KV_WIDTH), blk(CONV_CH), blk(CONV_CH)],
        out_shape=[out(ATTN_WIDTH, jnp.float32), out(KV_WIDTH, jnp.float32), out(KV_WIDTH, jnp.float32),
                   out(CONV_CH, jnp.bfloat16), out(CONV_CH, jnp.float32)],
        compiler_params=_params(("arbitrary",)),
        name="proj_sample",
    )(x, w_in_b, tab, conv_w, c0, c1)


def _sink_softmax_pv(s, valid, sink, vx):
    s = jnp.where(valid, s, -jnp.inf)
    m = jnp.maximum(jnp.max(s, axis=1, keepdims=True), sink)
    p = jnp.exp(s - m)
    den = jnp.sum(p, axis=1, keepdims=True) + jnp.exp(sink - m)
    return _dot(_bf16(p), vx) / den


SWA_QB = 4


def _swa_tile(step, sinks_ref, q_ref, kc_ref, kp_ref, vc_ref, vp_ref, store):
    kall = jnp.concatenate([kp_ref[...], kc_ref[...]], axis=0)
    vall = jnp.concatenate([vp_ref[...], vc_ref[...]], axis=0)
    i = lax.broadcasted_iota(jnp.int32, (WINDOW, 2 * WINDOW), 0)
    j = lax.broadcasted_iota(jnp.int32, (WINDOW, 2 * WINDOW), 1)
    band = (j > i) & (j <= i + WINDOW)
    for sb in range(SWA_QB):
        rows = slice(sb * WINDOW, (sb + 1) * WINDOW)
        kcat = kall[sb * WINDOW:(sb + 2) * WINDOW]
        vcat = vall[sb * WINDOW:(sb + 2) * WINDOW]
        valid = band & ((step > 0) | (j >= WINDOW)) if sb == 0 else band
        for p in range(N_HEADS // 2):
            qs = q_ref[rows, p * LANES:(p + 1) * LANES]
            acc = None
            for e in range(2):
                hd = 2 * p + e
                slab = 2 * (hd // (N_HEADS // N_KV_HEADS)) + e
                kx = kcat[:, slab * LANES:(slab + 1) * LANES]
                vx = vcat[:, slab * LANES:(slab + 1) * LANES]
                o = _sink_softmax_pv(_dot_nt(qs, kx), valid, sinks_ref[hd], vx)
                acc = o if acc is None else acc + o
            store(rows, slice(p * LANES, (p + 1) * LANES), _bf16(acc))


SWA_ROWS = N_HEADS * DEC_SEQ
NEW_ROWS = 2 * SUBLANES


def _swa_sample_kernel(q_ref, sink_ref, kn_ref, vn_ref, kt_ref, vt_ref, o_ref, okt_ref, ovt_ref):
    nb = SWA_BB
    rows = nb * SWA_ROWS
    t = lax.broadcasted_iota(jnp.int32, (rows, WINDOW), 0) % DEC_SEQ
    valid_c = lax.broadcasted_iota(jnp.int32, (rows, WINDOW), 1) > t
    valid_n = (lax.broadcasted_iota(jnp.int32, (rows, NEW_ROWS), 1)
               <= lax.broadcasted_iota(jnp.int32, (rows, NEW_ROWS), 0) % DEC_SEQ)
    sink = jnp.concatenate([sink_ref[:, 0:1]] * nb, axis=0)
    qs = [_bf16(q_ref[b]) for b in range(nb)]
    s_c = jnp.concatenate([_dot(qs[b], _bf16(kt_ref[b])) for b in range(nb)], axis=0)
    s_n = jnp.concatenate([_dot_nt(qs[b], _bf16(kn_ref[b])) for b in range(nb)], axis=0)
    s_c = jnp.where(valid_c, s_c, -jnp.inf)
    s_n = jnp.where(valid_n, s_n, -jnp.inf)
    m = jnp.maximum(jnp.maximum(jnp.max(s_c, axis=1, keepdims=True), jnp.max(s_n, axis=1, keepdims=True)), sink)
    p_c = jnp.exp(s_c - m)
    p_n = jnp.exp(s_n - m)
    rden = 1.0 / (jnp.sum(p_c, axis=1, keepdims=True) + jnp.sum(p_n, axis=1, keepdims=True) + jnp.exp(sink - m))
    p_c, p_n = _bf16(p_c), _bf16(p_n)
    lane = lax.broadcasted_iota(jnp.int32, (KV_WIDTH, WINDOW), 1)
    shift = WINDOW - DEC_SEQ
    zrows = jnp.zeros((KV_WIDTH - NEW_ROWS, KV_WIDTH), jnp.float32)
    for b in range(nb):
        r = slice(b * SWA_ROWS, (b + 1) * SWA_ROWS)
        kt, vt = kt_ref[b], vt_ref[b]
        kn, vn = kn_ref[b], vn_ref[b]
        o_ref[b] = (_dot_nt(p_c[r], _bf16(vt)) + _dot(p_n[r], _bf16(vn))) * rden[r]
        for old, new, dst in ((kt, kn, okt_ref), (vt, vn, ovt_ref)):
            new_cols = pltpu.roll(jnp.concatenate([new, zrows], axis=0).T, shift, axis=1)
            dst[b] = jnp.where(lane >= shift, new_cols, pltpu.roll(old, shift, axis=1))


def _swa_sample(sinks, q, kn, vn, cache_k, cache_v):
    nb = cache_k.shape[0]
    bb = SWA_BB
    groups = N_HEADS // N_KV_HEADS
    qh = q.reshape(nb, DEC_SEQ, N_KV_HEADS, groups, HEAD_DIM).transpose(0, 2, 3, 1, 4)
    qh = qh.reshape(nb, N_KV_HEADS, groups * DEC_SEQ, HEAD_DIM)
    zeros = jnp.zeros_like(qh[:, 0])
    qbd = jnp.concatenate([jnp.concatenate([qh[:, 0], zeros], axis=-1),
                           jnp.concatenate([zeros, qh[:, 1]], axis=-1)], axis=1)
    sink_col = jnp.broadcast_to(jnp.repeat(sinks, DEC_SEQ).reshape(SWA_ROWS, 1), (SWA_ROWS, LANES))
    pad8 = lambda a: jnp.pad(a.reshape(nb, DEC_SEQ, KV_WIDTH), ((0, 0), (0, NEW_ROWS - DEC_SEQ), (0, 0)))
    to_t = lambda c: c.transpose(0, 2, 3, 1).reshape(nb, KV_WIDTH, WINDOW)
    blk = lambda r, w: pl.BlockSpec((bb, r, w), lambda i: (i, 0, 0))
    o, okt, ovt = pl.pallas_call(
        _swa_sample_kernel,
        grid=(nb // bb,),
        in_specs=[blk(SWA_ROWS, KV_WIDTH), pl.BlockSpec((SWA_ROWS, LANES), lambda i: (0, 0)),
                  blk(NEW_ROWS, KV_WIDTH), blk(NEW_ROWS, KV_WIDTH), blk(KV_WIDTH, WINDOW), blk(KV_WIDTH, WINDOW)],
        out_specs=[blk(SWA_ROWS, KV_WIDTH), blk(KV_WIDTH, WINDOW), blk(KV_WIDTH, WINDOW)],
        out_shape=[jax.ShapeDtypeStruct((nb, SWA_ROWS, KV_WIDTH), jnp.float32),
                   jax.ShapeDtypeStruct((nb, KV_WIDTH, WINDOW), jnp.float32),
                   jax.ShapeDtypeStruct((nb, KV_WIDTH, WINDOW), jnp.float32)],
        compiler_params=_params(("arbitrary",)),
        name="swa_sample",
    )(qbd, sink_col, pad8(kn), pad8(vn), to_t(cache_k), to_t(cache_v))
    o = o.reshape(nb, N_KV_HEADS, groups, DEC_SEQ, N_KV_HEADS, HEAD_DIM)
    attn = jnp.stack([o[:, h, :, :, h, :] for h in range(N_KV_HEADS)], axis=1)
    attn = attn.transpose(0, 3, 1, 2, 4).reshape(nb * DEC_SEQ, ATTN_WIDTH)
    from_t = lambda c: c.reshape(nb, N_KV_HEADS, HEAD_DIM, WINDOW).transpose(0, 3, 1, 2)
    return attn, from_t(okt), from_t(ovt)


def _mem_kv_kernel(mem_ref, wk_ref, wv_ref, wq_ref, wo_ref, mk_ref, mv_ref, wqk_ref, wvo_ref):
    mb = _bf16(mem_ref[...])
    mk = _dot(mb, wk_ref[...])
    mv = _dot(mb, wv_ref[...])
    mk_ref[...] = mk
    mv_ref[...] = mv
    mkb, mvb = _bf16(mk), _bf16(mv)
    for h in range(MEM_HEADS):
        sl = slice(h * MEM_HEAD_DIM, (h + 1) * MEM_HEAD_DIM)
        keys = slice(h * N_MEM, (h + 1) * N_MEM)
        wqk_ref[:, keys] = _bf16(_dot_nt(wq_ref[:, sl], mkb[:, sl]) * (MEM_HEAD_DIM ** -0.5))
        wvo_ref[keys, :] = _bf16(_dot(mvb[:, sl], wo_ref[sl, :]))


def _mem_kv(mem, wk_b, wv_b, wq_b, wo_b):
    full = lambda a: pl.BlockSpec(a.shape, lambda i: (0,) * a.ndim)
    blk = pl.BlockSpec((N_MEM, D_MODEL), lambda i: (0, 0))
    f32 = jax.ShapeDtypeStruct((N_MEM, D_MODEL), jnp.float32)
    fused = (D_MODEL, MEM_HEADS * N_MEM), (MEM_HEADS * N_MEM, D_MODEL)
    return pl.pallas_call(
        _mem_kv_kernel,
        grid=(1,),
        in_specs=[full(mem), full(wk_b), full(wv_b), full(wq_b), full(wo_b)],
        out_specs=[blk, blk] + [pl.BlockSpec(shp, lambda i: (0, 0)) for shp in fused],
        out_shape=[f32, f32] + [jax.ShapeDtypeStruct(shp, jnp.bfloat16) for shp in fused],
        compiler_params=_params(("arbitrary",)),
        name="mem_kv",
    )(mem, wk_b, wv_b, wq_b, wo_b)


def _mix_ln1(attn_ref, conv_ref, x_ref, wmix_ref, g1_ref, b1_ref):
    mix = _dot(_bf16(attn_ref[...]), wmix_ref[0:ATTN_WIDTH, :]) + _dot(conv_ref[...], wmix_ref[ATTN_WIDTH:, :])
    return _layer_norm(ALPHA * x_ref[...] + mix, g1_ref[...], b1_ref[...])


def _mem_q(h1, wq_ref):
    return _bf16(_dot(_bf16(h1), wq_ref[...]) * (MEM_HEAD_DIM ** -0.5))


def _route(h2, wrhl_ref, br_ref, tri_ref, carry):
    hi = _bf16(h2)
    lo = _bf16(h2 - hi.astype(jnp.float32))
    hh = _dot(hi, wrhl_ref[...])
    logits = hh[:, 0:LANES] + hh[:, LANES:] + _dot(lo, wrhl_ref[:, 0:LANES]) + br_ref[...]
    lane_i = lax.broadcasted_iota(jnp.int32, logits.shape, 1)
    lane = lane_i.astype(jnp.float32)
    big = jnp.float32(LANES)
    is_g = lane_i < N_GROUPS
    gl = jnp.where(is_g, logits, -jnp.inf)
    gmax = jnp.max(gl, axis=1, keepdims=True)
    gidx = jnp.min(jnp.where(is_g & (logits == gmax), lane, big), axis=1, keepdims=True)
    gsum = jnp.sum(jnp.exp(gl - gmax), axis=1, keepdims=True)
    gw = 1.0 / gsum
    eid = lane_i - N_GROUPS
    assert EXPERTS_PER_GROUP == 8
    grp = lax.shift_right_arithmetic(eid, jnp.full_like(eid, 3)).astype(jnp.float32)
    in_e = (lane_i >= N_GROUPS) & (lane_i < N_GROUPS + N_EXPERTS) & (grp == gidx)
    v1 = jnp.max(jnp.where(in_e, logits, -jnp.inf), axis=1, keepdims=True)
    i1 = jnp.min(jnp.where(in_e & (logits == v1), lane, big), axis=1, keepdims=True)
    rest = in_e & (lane != i1)
    v2 = jnp.max(jnp.where(rest, logits, -jnp.inf), axis=1, keepdims=True)
    i2 = jnp.min(jnp.where(rest & (logits == v2), lane, big), axis=1, keepdims=True)
    ex = jnp.exp(v2 - v1)
    den = 1.0 + ex
    w1 = gw / den
    w2 = gw * ex / den
    zero = jnp.zeros_like(logits)
    pick1 = lane == i1
    pick2 = lane == i2
    sel = jnp.where(pick1 | pick2, 1.0, 0.0)
    before = _dot(tri_ref[...], _bf16(sel)) + carry
    rank1 = jnp.sum(jnp.where(pick1, before, zero), axis=1, keepdims=True)
    rank2 = jnp.sum(jnp.where(pick2, before, zero), axis=1, keepdims=True)
    cols = (i1 - N_GROUPS, i2 - N_GROUPS, w1, w2, rank1, rank2)
    route = zero
    for k, col in enumerate(cols):
        route = jnp.where(lane_i == k, col, route)
    return route, carry + jnp.sum(sel, axis=0, keepdims=True)


def _post_tile(attn, conv_ref, x_ref, wmix_ref, g1_ref, b1_ref, wqk_ref, wvo_ref, g2_ref, b2_ref,
               wrhl_ref, br_ref, tri_ref, h2t_ref, rt_ref, cnt_ref, carry_ref):
    mix = _dot(attn, wmix_ref[0:ATTN_WIDTH, :]) + _dot(conv_ref[...], wmix_ref[ATTN_WIDTH:, :])
    h1 = _layer_norm(ALPHA * x_ref[...] + mix, g1_ref[...], b1_ref[...])
    scores = _dot(_bf16(h1), wqk_ref[...])
    probs = []
    for h in range(MEM_HEADS):
        s = scores[:, h * N_MEM:(h + 1) * N_MEM]
        p = jnp.exp(s - jnp.max(s, axis=1, keepdims=True))
        probs.append(_bf16(p / jnp.sum(p, axis=1, keepdims=True)))
    mem_out = _dot(jnp.concatenate(probs, axis=1), wvo_ref[...])
    h2 = _layer_norm(ALPHA * h1 + mem_out, g2_ref[...], b2_ref[...])
    _store_token_tiles(h2t_ref, h2)
    route, carry = _route(h2, wrhl_ref, br_ref, tri_ref, carry_ref[0:1, :])
    rt_ref[...] = route
    carry_ref[...] = jnp.broadcast_to(carry, carry_ref.shape)
    cnt_ref[...] = jnp.broadcast_to(carry, cnt_ref.shape)


def _swa_post_kernel(sinks_ref, q_ref, kc_ref, kp_ref, vc_ref, vp_ref,
                     conv_ref, x_ref, wmix_ref, g1_ref, b1_ref, wqk_ref, wvo_ref, g2_ref, b2_ref,
                     wrhl_ref, br_ref, tri_ref, h2s_ref, rts_ref, cnts_ref, wg_ref, wu_ref, wd_ref,
                     h2t_ref, rt_ref, cnt_ref, wgb_ref, wub_ref, wdb_ref, carry_ref, attn_s):
    t = pl.program_id(0)
    steps = N_PROMPT // TM_POST
    par = t % 2

    def swa(slot):
        def store(rows, cols, val):
            attn_s[slot, rows, cols] = val
        _swa_tile(t, sinks_ref, q_ref, kc_ref, kp_ref, vc_ref, vp_ref, store)

    def post(slot):
        _post_tile(attn_s[slot], conv_ref, x_ref, wmix_ref, g1_ref, b1_ref, wqk_ref, wvo_ref, g2_ref, b2_ref,
                   wrhl_ref, br_ref, tri_ref, h2t_ref, rt_ref, cnt_ref, carry_ref)

    def cast_expert():
        wgb_ref[...] = _bf16(wg_ref[...])
        wub_ref[...] = _bf16(wu_ref[...])
        wdb_ref[...] = _bf16(wd_ref[...])

    @pl.when(t == 0)
    def _():
        carry_ref[...] = cnts_ref[...]
        swa(0)
        cast_expert()

    @pl.when((t >= 1) & (t < steps))
    def _():
        swa(par)
        post(1 - par)
        cast_expert()

    @pl.when(t == steps)
    def _():
        post(1 - par)

    @pl.when(t == steps + 1)
    def _():
        h2t_ref[...] = h2s_ref[...]
        rt_ref[...] = rts_ref[...]


def _swa_post(sinks, q, kx, vx, conv, x, wmix_b, g1, b1, wqk, wvo, g2, b2, wrhl, br, tri, h2t_s, rt_s, cnt_s,
              w_gate, w_up, w_down):
    n = x.shape[0]
    tm = TM_POST
    assert tm == SWA_QB * WINDOW
    steps = n // tm
    assert N_EXPERTS <= steps
    expert = lambda a: pl.BlockSpec((1,) + a.shape[1:], lambda i: (jnp.minimum(i, N_EXPERTS - 1), 0, 0))
    experts = (w_gate, w_up, w_down)
    cur = lambda w: pl.BlockSpec((tm, w), lambda i: (jnp.minimum(i, steps - 1), 0))
    prev = lambda w: pl.BlockSpec((WINDOW, w), lambda i: (jnp.clip(SWA_QB * i - 1, 0, n // WINDOW - 1), 0))
    lag = lambda w: pl.BlockSpec((tm, w), lambda i: (jnp.clip(i - 1, 0, steps - 1), 0))
    full = lambda a: pl.BlockSpec(a.shape, lambda i: (0,) * a.ndim)
    weights = (wmix_b, g1, b1, wqk, wvo, g2, b2, wrhl, br, tri, h2t_s, rt_s, cnt_s)
    n_out = n + h2t_s.shape[0] // ROW_CHUNKS
    out_idx = lambda i: (jnp.where(i > steps, steps, jnp.clip(i - 1, 0, steps - 1)), 0)
    return pl.pallas_call(
        _swa_post_kernel,
        grid=(steps + 2,),
        in_specs=([pl.BlockSpec(memory_space=pltpu.SMEM), cur(ATTN_WIDTH),
                   cur(4 * LANES), prev(4 * LANES), cur(4 * LANES), prev(4 * LANES),
                   lag(CONV_CH), lag(D_MODEL)] + [full(a) for a in weights] + [expert(a) for a in experts]),
        out_specs=[pl.BlockSpec((tm * ROW_CHUNKS, LANES), out_idx),
                   pl.BlockSpec((tm, LANES), out_idx),
                   pl.BlockSpec((SUBLANES, LANES), lambda i: (0, 0))] + [expert(a) for a in experts],
        out_shape=[jax.ShapeDtypeStruct((n_out * ROW_CHUNKS, LANES), jnp.float32),
                   jax.ShapeDtypeStruct((n_out, LANES), jnp.float32),
                   jax.ShapeDtypeStruct((SUBLANES, LANES), jnp.float32)]
                  + [jax.ShapeDtypeStruct(a.shape, jnp.bfloat16) for a in experts],
        scratch_shapes=[pltpu.VMEM((SUBLANES, LANES), jnp.float32),
                        pltpu.VMEM((2, tm, ATTN_WIDTH), jnp.bfloat16)],
        compiler_params=_params(("arbitrary",)),
        name="swa_post_prompt",
    )(sinks, q, kx, kx, vx, vx, conv, x, *weights, *experts)


def _post_a_sample_kernel(attn_ref, conv_ref, x_ref, wmix_ref, g1_ref, b1_ref, wq_ref, h1_ref, qm_ref):
    h1 = _mix_ln1(attn_ref, conv_ref, x_ref, wmix_ref, g1_ref, b1_ref)
    h1_ref[...] = h1
    qm_ref[...] = _mem_q(h1, wq_ref).astype(jnp.float32)


def _post_a_sample(attn, conv, x, wmix_b, g1, b1, wq_b):
    n = x.shape[0]
    args = (attn, conv, x, wmix_b, g1, b1, wq_b)
    full = lambda a: pl.BlockSpec(a.shape, lambda i: (0,) * a.ndim)
    blk = pl.BlockSpec((n, D_MODEL), lambda i: (0, 0))
    return pl.pallas_call(
        _post_a_sample_kernel,
        grid=(1,),
        in_specs=[full(a) for a in args],
        out_specs=[blk, blk],
        out_shape=[jax.ShapeDtypeStruct((n, D_MODEL), jnp.float32),
                   jax.ShapeDtypeStruct((n, D_MODEL), jnp.float32)],
        compiler_params=_params(("arbitrary",)),
        name="post_a_sample",
    )(*args)


MEM_ROWS = MEM_HEADS * DEC_SEQ


def _mem_attn_sample_body(q_ref, mk_ref, mv_ref, o_ref):
    nk = N_MEM * MEM_HEADS
    nb = SAMPLE_BB
    rows = nb * MEM_ROWS
    row_h = (lax.broadcasted_iota(jnp.int32, (rows, nk), 0) % MEM_ROWS) // DEC_SEQ
    key_h = lax.broadcasted_iota(jnp.int32, (rows, nk), 1) % MEM_HEADS
    s = jnp.concatenate(
        [_dot_nt(_bf16(q_ref[b]), _bf16(mk_ref[b].reshape(nk, MEM_HEAD_DIM))) for b in range(nb)], axis=0)
    s = jnp.where(row_h == key_h, s, -jnp.inf)
    p = jnp.exp(s - jnp.max(s, axis=1, keepdims=True))
    rden = 1.0 / jnp.sum(p, axis=1, keepdims=True)
    p = _bf16(p)
    for b in range(nb):
        r = slice(b * MEM_ROWS, (b + 1) * MEM_ROWS)
        o_ref[b] = _dot(p[r], _bf16(mv_ref[b].reshape(nk, MEM_HEAD_DIM))) * rden[r]


def _proj_mem_kernel(x_ref, w_ref, tab_ref, cw_ref, mq_ref, mk_ref, mv_ref,
                     q_ref, kx_ref, vx_ref, conv_ref, ktail_ref, vtail_ref, utail_ref, mo_ref, carry_ref):
    _mem_attn_sample_body(mq_ref, mk_ref, mv_ref, mo_ref)
    _proj_prompt_body(x_ref, w_ref, tab_ref, cw_ref,
                      q_ref, kx_ref, vx_ref, conv_ref, ktail_ref, vtail_ref, utail_ref, carry_ref)


def _proj_prompt_mem_sample(x, w_in_b, tab, conv_w, qm, mk, mv):
    n = x.shape[0]
    nb = mk.shape[0]
    bb = SAMPLE_BB
    steps = nb // bb
    tm = n // steps
    row = lambda w: pl.BlockSpec((tm, w), lambda i: (i, 0))
    full = lambda a: pl.BlockSpec(a.shape, lambda i: (0,) * a.ndim)
    const = lambda r, w: pl.BlockSpec((r, w), lambda i: (0, 0))
    mq = qm.reshape(nb, DEC_SEQ, MEM_HEADS, MEM_HEAD_DIM).transpose(0, 2, 1, 3).reshape(nb, MEM_ROWS, MEM_HEAD_DIM)
    mrows = pl.BlockSpec((bb, MEM_ROWS, MEM_HEAD_DIM), lambda i: (i, 0, 0))
    kv = pl.BlockSpec((bb, N_MEM, MEM_HEADS, MEM_HEAD_DIM), lambda i: (i, 0, 0, 0))
    outs = pl.pallas_call(
        _proj_mem_kernel,
        grid=(steps,),
        in_specs=[row(D_MODEL), full(w_in_b), pl.BlockSpec((ROPE_ROWS, tm), lambda i: (0, i)), full(conv_w),
                  mrows, kv, kv],
        out_specs=[row(ATTN_WIDTH), row(4 * LANES), row(4 * LANES), row(CONV_CH),
                   const(WINDOW, KV_WIDTH), const(WINDOW, KV_WIDTH), const(SUBLANES, CONV_CH), mrows],
        out_shape=[jax.ShapeDtypeStruct((n, ATTN_WIDTH), jnp.bfloat16),
                   jax.ShapeDtypeStruct((n, 4 * LANES), jnp.bfloat16),
                   jax.ShapeDtypeStruct((n, 4 * LANES), jnp.bfloat16),
                   jax.ShapeDtypeStruct((n, CONV_CH), jnp.bfloat16),
                   jax.ShapeDtypeStruct((WINDOW, KV_WIDTH), jnp.float32),
                   jax.ShapeDtypeStruct((WINDOW, KV_WIDTH), jnp.float32),
                   jax.ShapeDtypeStruct((SUBLANES, CONV_CH), jnp.float32),
                   jax.ShapeDtypeStruct((nb, MEM_ROWS, MEM_HEAD_DIM), jnp.float32)],
        scratch_shapes=[pltpu.VMEM((SUBLANES, CONV_CH), jnp.float32)],
        compiler_params=_params(("arbitrary",)),
        name="proj_prompt_mem_sample",
    )(x, w_in_b, tab, conv_w, mq, mk, mv)
    o = outs[7].reshape(nb, MEM_HEADS, DEC_SEQ, MEM_HEAD_DIM).transpose(0, 2, 1, 3).reshape(nb * DEC_SEQ, D_MODEL)
    return outs[:7], o


def _post_b_sample_kernel(o_ref, h1_ref, wo_ref, g2_ref, b2_ref, wrhl_ref, br_ref, tri_ref,
                          h2t_ref, rt_ref, cnt_ref):
    h2 = _layer_norm(ALPHA * h1_ref[...] + _dot(_bf16(o_ref[...]), wo_ref[...]), g2_ref[...], b2_ref[...])
    _store_token_tiles(h2t_ref, h2)
    route, carry = _route(h2, wrhl_ref, br_ref, tri_ref, jnp.zeros((1, LANES), jnp.float32))
    rt_ref[...] = route
    cnt_ref[...] = jnp.broadcast_to(carry, cnt_ref.shape)


def _post_b_sample(o, h1, wo_b, g2, b2, wrhl, br, tri):
    n = h1.shape[0]
    args = (o, h1, wo_b, g2, b2, wrhl, br, tri)
    full = lambda a: pl.BlockSpec(a.shape, lambda i: (0,) * a.ndim)
    return pl.pallas_call(
        _post_b_sample_kernel,
        grid=(1,),
        in_specs=[full(a) for a in args],
        out_specs=[pl.BlockSpec((n * ROW_CHUNKS, LANES), lambda i: (0, 0)),
                   pl.BlockSpec((n, LANES), lambda i: (0, 0)),
                   pl.BlockSpec((SUBLANES, LANES), lambda i: (0, 0))],
        out_shape=[jax.ShapeDtypeStruct((n * ROW_CHUNKS, LANES), jnp.float32),
                   jax.ShapeDtypeStruct((n, LANES), jnp.float32),
                   jax.ShapeDtypeStruct((SUBLANES, LANES), jnp.float32)],
        compiler_params=_params(("arbitrary",)),
        name="post_b_sample",
    )(*args)


def _row_gather_copy(src_hbm, idx, dst, dst_row, sem):
    s0 = pl.multiple_of(idx * ROW_CHUNKS, ROW_CHUNKS)
    d0 = pl.multiple_of(dst_row * ROW_CHUNKS, ROW_CHUNKS)
    return pltpu.make_async_copy(src_hbm.at[pl.ds(s0, ROW_CHUNKS), :], dst.at[pl.ds(d0, ROW_CHUNKS), :], sem)


def _dispatch_kernel(pos_ref, h2t_ref, xs_hbm, sem):
    def body(r, c):
        src = h2t_ref.at[pl.ds(pl.multiple_of(r * ROW_CHUNKS, ROW_CHUNKS), ROW_CHUNKS), :]
        for k in range(2):
            d0 = pl.multiple_of(pos_ref[0, 0, k * TM_COMB + r] * ROW_CHUNKS, ROW_CHUNKS)
            pltpu.make_async_copy(src, xs_hbm.at[pl.ds(d0, ROW_CHUNKS), :], sem.at[0]).start(priority=k)
        return c
    lax.fori_loop(0, TM_COMB, body, 0, unroll=8)
    for _ in range(2):
        pltpu.make_async_copy(h2t_ref, xs_hbm.at[pl.ds(0, TM_COMB * ROW_CHUNKS), :], sem.at[0]).wait()


def _dispatch(pos3, h2t):
    nt = N_ALL // TM_COMB
    return pl.pallas_call(
        _dispatch_kernel,
        grid=(nt,),
        in_specs=[pl.BlockSpec((1, 1, 2 * TM_COMB), lambda i: (i, 0, 0), memory_space=pltpu.SMEM),
                  pl.BlockSpec((TM_COMB * ROW_CHUNKS, LANES), lambda i: (i, 0))],
        out_specs=pl.BlockSpec(memory_space=pl.ANY),
        out_shape=jax.ShapeDtypeStruct((N_ASSIGN * ROW_CHUNKS, LANES), jnp.float32),
        scratch_shapes=[pltpu.SemaphoreType.DMA((1,))],
        compiler_params=_params(("arbitrary",)),
        name="moe_dispatch",
    )(pos3, h2t)


def _moe_ffn_kernel(it_ref, ie_ref, lo_ref, hi_ref, x_hbm, wg_ref, wu_ref, wd_ref, y_ref, xbuf, sem):
    i = pl.program_id(0)
    lo = lo_ref[i]
    hi = hi_ref[i]
    t = it_ref[i]
    tile_rows = TM_MOE * ROW_CHUNKS

    def fetch(tile):
        src = pl.multiple_of(tile * tile_rows, tile_rows)
        dst = pl.multiple_of((tile % MOE_X_SLOTS) * tile_rows, tile_rows)
        return pltpu.make_async_copy(x_hbm.at[pl.ds(src, tile_rows), :], xbuf.at[pl.ds(dst, tile_rows), :],
                                     sem.at[tile % MOE_X_SLOTS])

    @pl.when(i == 0)
    def _():
        for k in range(MOE_X_SLOTS - 1):
            fetch(jnp.int32(k)).start()

    @pl.when((i == 0) | (t != it_ref[jnp.maximum(i - 1, 0)]))
    def _():
        @pl.when(t + MOE_X_SLOTS - 1 < MOE_TILES)
        def _():
            fetch(t + MOE_X_SLOTS - 1).start()
        fetch(t).wait()

    base = pl.multiple_of((t % MOE_X_SLOTS) * tile_rows, tile_rows)

    def ffn(r0, rows):
        x = _bf16(_load_token_tiles(xbuf, base + r0 * ROW_CHUNKS, rows))
        hg = _dot(x, wg_ref[0])
        hu = _dot(x, wu_ref[0])
        h = hg / (1.0 + jnp.exp(-hg)) * hu
        return _dot(_bf16(h), wd_ref[0])

    whole = (lo == 0) & (hi == TM_MOE)

    @pl.when(whole)
    def _():
        _store_token_tiles(y_ref, ffn(0, TM_MOE))

    for r0 in range(0, TM_MOE, MOE_BLOCK):
        live = (hi > lo) & jnp.logical_not(whole) & (hi > r0) & (lo < r0 + MOE_BLOCK)

        def store(merge, r0=r0):
            y = ffn(r0, MOE_BLOCK)
            row = r0 + lax.broadcasted_iota(jnp.int32, (MOE_BLOCK, LANES), 0)
            mask = (row >= lo) & (row < hi)
            for c in range(ROW_CHUNKS):
                sl = pl.ds(r0 * ROW_CHUNKS + c, MOE_BLOCK, stride=ROW_CHUNKS)
                y_ref[sl, :] = jnp.where(mask, y[:, c * LANES:(c + 1) * LANES], y_ref[sl, :] if merge else 0.0)

        pl.when(live & (lo <= r0))(functools.partial(store, False))
        pl.when(live & (lo > r0))(functools.partial(store, True))


def _moe_ffn(item_tile, item_expert, item_lo, item_hi, x_sorted, w_gate, w_up, w_down):
    wspec = lambda shp: pl.BlockSpec((1,) + shp, lambda i, it, ie, lo, hi: (ie[i], 0, 0))
    tile = pl.BlockSpec((TM_MOE * ROW_CHUNKS, LANES), lambda i, it, ie, lo, hi: (it[i], 0))
    assert MOE_TILES >= MOE_X_SLOTS - 1
    grid_spec = pltpu.PrefetchScalarGridSpec(
        num_scalar_prefetch=4,
        grid=(MOE_ITEMS,),
        in_specs=[pl.BlockSpec(memory_space=pl.ANY),
                  wspec((D_MODEL, EXPERT_FF)), wspec((D_MODEL, EXPERT_FF)), wspec((EXPERT_FF, D_MODEL))],
        out_specs=tile,
        scratch_shapes=[pltpu.VMEM((MOE_X_SLOTS * TM_MOE * ROW_CHUNKS, LANES), jnp.float32),
                        pltpu.SemaphoreType.DMA((MOE_X_SLOTS,))],
    )
    return pl.pallas_call(
        _moe_ffn_kernel,
        grid_spec=grid_spec,
        out_shape=jax.ShapeDtypeStruct((N_ASSIGN * ROW_CHUNKS, LANES), jnp.float32),
        compiler_params=_params(("arbitrary",)),
        name="moe_ffn",
    )(item_tile, item_expert, item_lo, item_hi, x_sorted, w_gate, w_up, w_down)


def _combine_kernel(nt, pos_cur_ref, pos_nxt_ref, yt_hbm, h2t_ref, rt_ref, g3_ref, b3_ref, o_ref, abuf, sem):
    t = pl.program_id(0)
    slot = t % 2
    rows = 2 * TM_COMB

    def issue(pos_ref, s):
        def body(j, c):
            for k in range(2):
                r = 2 * j + k
                _row_gather_copy(yt_hbm, pos_ref[0, 0, r], abuf, s * rows + r, sem.at[s]).start(priority=k)
            return c
        lax.fori_loop(0, rows // 2, body, 0, unroll=16)

    @pl.when(t == 0)
    def _():
        issue(pos_cur_ref, 0)

    @pl.when(t + 1 < nt)
    def _():
        issue(pos_nxt_ref, 1 - slot)

    base = pl.multiple_of(slot * (rows * ROW_CHUNKS), rows * ROW_CHUNKS)
    pltpu.make_async_copy(yt_hbm.at[pl.ds(0, rows * ROW_CHUNKS), :],
                          abuf.at[pl.ds(base, rows * ROW_CHUNKS), :], sem.at[slot]).wait()
    ya = _load_token_tiles(abuf, base, TM_COMB)
    yb = _load_token_tiles(abuf, base + TM_COMB * ROW_CHUNKS, TM_COMB)
    rt = rt_ref[...]
    ff = rt[:, 2:3] * ya + rt[:, 3:4] * yb
    h2 = _load_token_tiles(h2t_ref, 0, TM_COMB)
    o_ref[...] = _layer_norm(ALPHA * h2 + ff, g3_ref[...], b3_ref[...])


def _combine(pos3, yt, h2t, rt, g3, b3, tile0, n_tiles):
    last = tile0 + n_tiles - 1
    smem_pos = lambda f: pl.BlockSpec((1, 1, 2 * TM_COMB), f, memory_space=pltpu.SMEM)
    full = lambda a: pl.BlockSpec(a.shape, lambda i: (0,) * a.ndim)
    return pl.pallas_call(
        functools.partial(_combine_kernel, n_tiles),
        grid=(n_tiles,),
        in_specs=[smem_pos(lambda i: (tile0 + i, 0, 0)),
                  smem_pos(lambda i: (jnp.minimum(tile0 + i + 1, last), 0, 0)),
                  pl.BlockSpec(memory_space=pl.ANY),
                  pl.BlockSpec((TM_COMB * ROW_CHUNKS, LANES), lambda i: (tile0 + i, 0)),
                  pl.BlockSpec((TM_COMB, LANES), lambda i: (tile0 + i, 0)),
                  full(g3), full(b3)],
        out_specs=pl.BlockSpec((TM_COMB, D_MODEL), lambda i: (i, 0)),
        out_shape=jax.ShapeDtypeStruct((n_tiles * TM_COMB, D_MODEL), jnp.float32),
        scratch_shapes=[pltpu.VMEM((2 * 2 * TM_COMB * ROW_CHUNKS, LANES), jnp.float32),
                        pltpu.SemaphoreType.DMA((2,))],
        compiler_params=_params(("arbitrary",)),
        name="moe_combine",
    )(pos3, pos3, yt, h2t, rt, g3, b3)


POS_TILES = 11


def _positions_kernel(rt_ref, starts_ref, pos_ref):
    expert = lax.broadcasted_iota(jnp.int32, (N_EXPERTS, TM_COMB), 0).astype(jnp.float32)
    starts = jnp.concatenate([starts_ref[...]] * (TM_COMB // LANES), axis=1)
    for j in range(POS_TILES):
        cols = rt_ref[j * TM_COMB:(j + 1) * TM_COMB, :].T
        out = []
        for k in range(2):
            seg = jnp.sum(jnp.where(expert == cols[k:k + 1, :], starts, 0.0), axis=0, keepdims=True)
            out.append(seg + cols[4 + k:5 + k, :])
        pos_ref[j] = jnp.concatenate(out, axis=1).astype(jnp.int32)


def _positions(rt, starts_rep):
    nt = N_ALL // TM_COMB
    assert nt % POS_TILES == 0
    return pl.pallas_call(
        _positions_kernel,
        grid=(nt // POS_TILES,),
        in_specs=[pl.BlockSpec((POS_TILES * TM_COMB, LANES), lambda i: (i, 0)),
                  pl.BlockSpec((N_EXPERTS, LANES), lambda i: (0, 0))],
        out_specs=pl.BlockSpec((POS_TILES, 1, 2 * TM_COMB), lambda i: (i, 0, 0)),
        out_shape=jax.ShapeDtypeStruct((nt, 1, 2 * TM_COMB), jnp.int32),
        compiler_params=_params(("arbitrary",)),
        name="moe_positions",
    )(rt, starts_rep)


def _routing_plan(rt, cnt):
    i32 = jnp.int32
    counts_f = cnt[0, N_GROUPS:N_GROUPS + N_EXPERTS]
    starts_f = jnp.cumsum(counts_f) - counts_f
    pos3 = _positions(rt, jnp.broadcast_to(starts_f[:, None], (N_EXPERTS, LANES)))
    starts = starts_f.astype(i32)
    tiles = jnp.arange(MOE_TILES, dtype=i32) * TM_MOE
    rank_t = jnp.arange(MOE_TILES, dtype=i32) + jnp.sum((starts[None, :] < tiles[:, None]).astype(i32), axis=1)
    rank_s = jnp.arange(N_EXPERTS, dtype=i32) + jnp.sum((tiles[None, :] <= starts[:, None]).astype(i32), axis=1)
    vals = jnp.concatenate([tiles, starts])
    ranks = jnp.concatenate([rank_t, rank_s])
    slot = jnp.arange(MOE_ITEMS, dtype=i32)
    lo = jnp.sum(jnp.where(ranks[None, :] == slot[:, None], vals[None, :], 0), axis=1)
    hi = jnp.concatenate([lo[1:], jnp.full((1,), N_ASSIGN, i32)])
    item_tile = jnp.minimum(lo // TM_MOE, MOE_TILES - 1)
    item_expert = jnp.clip(jnp.sum((starts[None, :] <= lo[:, None]).astype(i32), axis=1) - 1, 0, N_EXPERTS - 1)
    base = item_tile * TM_MOE
    return item_tile, item_expert, lo - base, hi - base, pos3


def kernel(x_prompt, x_sample, mem_prompt, cache_swa_k, cache_swa_v, cache_conv, cache_mem_k, cache_mem_v,
           w_in, sinks, conv_w, w_mix_out, ln1_g, ln1_b, w_q_mem, w_k_mem, w_v_mem, w_o_mem, ln2_g, ln2_b,
           w_router_group, b_router_group, w_router_expert, b_router_expert, w_gate, w_up, w_down,
           ln3_g, ln3_b):
    f32 = jnp.float32
    row = lambda a: a.reshape(1, -1).astype(f32)
    w_in_b, wmix_b, wq_b, wk_b, wv_b, wo_b = (_bf16(w) for w in (w_in, w_mix_out, w_q_mem, w_k_mem, w_v_mem, w_o_mem))
    g1, b1, g2, b2, g3, b3 = (row(a) for a in (ln1_g, ln1_b, ln2_g, ln2_b, ln3_g, ln3_b))
    pad = LANES - N_GROUPS - N_EXPERTS
    wr = jnp.concatenate([w_router_group, w_router_expert, jnp.zeros((D_MODEL, pad), f32)], axis=1)
    wrh = _bf16(wr)
    wrhl = jnp.concatenate([wrh, _bf16(wr - wrh.astype(f32))], axis=1)
    br = jnp.concatenate([b_router_group, b_router_expert, jnp.zeros((pad,), f32)]).reshape(1, LANES)

    xs = x_sample.reshape(N_SAMPLE, D_MODEL)
    tab_s = jnp.tile(_rope_table(PAST_LEN + jnp.arange(DEC_SEQ)), (1, DEC_BATCH))
    c0 = jnp.repeat(cache_conv[:, 0], DEC_SEQ, axis=0)
    c1 = jnp.repeat(cache_conv[:, 1], DEC_SEQ, axis=0)
    q_s, k_s, v_s, conv_s, u_s = _proj_sample(xs, w_in_b, tab_s, conv_w, c0, c1)
    attn_s, swa_k_s, swa_v_s = _swa_sample(sinks, q_s, k_s, v_s, cache_swa_k, cache_swa_v)
    h1_s, qm_s = _post_a_sample(attn_s, conv_s, xs, wmix_b, g1, b1, wq_b)
    xp = x_prompt.reshape(N_PROMPT, D_MODEL)
    tab_p = _rope_table(jnp.arange(N_PROMPT))
    (q_p, kx_p, vx_p, conv_p, k_tail, v_tail, u_tail), o_s = _proj_prompt_mem_sample(
        xp, w_in_b, tab_p, conv_w, qm_s, cache_mem_k, cache_mem_v)
    tri = _bf16(jnp.tril(jnp.ones((TM_POST, TM_POST), f32), -1))
    h2t_s, rt_s, cnt_s = _post_b_sample(o_s, h1_s, wo_b, g2, b2, wrhl, br, tri)

    mk, mv, wqk, wvo = _mem_kv(mem_prompt.reshape(N_MEM, D_MODEL), wk_b, wv_b, wq_b, wo_b)
    h2t, rt, cnt, wg_b, wu_b, wd_b = _swa_post(sinks, q_p, kx_p, vx_p, conv_p, xp, wmix_b, g1, b1, wqk, wvo, g2, b2,
                                               wrhl, br, tri, h2t_s, rt_s, cnt_s, w_gate, w_up, w_down)

    item_tile, item_expert, item_lo, item_hi, pos3 = _routing_plan(rt, cnt)
    x_sorted = _dispatch(pos3, h2t)
    yt = _moe_ffn(item_tile, item_expert, item_lo, item_hi, x_sorted, wg_b, wu_b, wd_b)
    y_p = _combine(pos3, yt, h2t, rt, g3, b3, 0, N_PROMPT // TM_COMB)
    y_s = _combine(pos3, yt, h2t, rt, g3, b3, N_PROMPT // TM_COMB, N_SAMPLE // TM_COMB)

    return (y_p.reshape(1, SEQ, D_MODEL),
            y_s.reshape(DEC_BATCH, DEC_SEQ, D_MODEL),
            k_tail.reshape(1, WINDOW, N_KV_HEADS, HEAD_DIM),
            v_tail.reshape(1, WINDOW, N_KV_HEADS, HEAD_DIM),
            u_tail[SUBLANES - (CONV_K - 1):].reshape(1, CONV_K - 1, CONV_CH),
            mk.reshape(1, N_MEM, MEM_HEADS, MEM_HEAD_DIM),
            mv.reshape(1, N_MEM, MEM_HEADS, MEM_HEAD_DIM),
            swa_k_s.reshape(DEC_BATCH, WINDOW, N_KV_HEADS, HEAD_DIM),
            swa_v_s.reshape(DEC_BATCH, WINDOW, N_KV_HEADS, HEAD_DIM),
            u_s.reshape(DEC_BATCH, DEC_SEQ, CONV_CH)[:, DEC_SEQ - (CONV_K - 1):])
```

```python
import functools

import jax
import jax.numpy as jnp
from jax import lax
from jax.experimental import pallas as pl
from jax.experimental.pallas import tpu as pltpu

D_MODEL = 1024
SEQ = 16384
DEC_BATCH = 128
DEC_SEQ = 4
PAST_LEN = 16384
ATTN_WIDTH = 512
CONV_CH = 512
HEAD_DIM = 64
N_HEADS = 8
N_KV_HEADS = 2
KV_WIDTH = 128
WINDOW = 128
ROPE_THETA = 500000.0
ROPE_DIM = 16
CONV_K = 3
Q_END = ATTN_WIDTH
K_END = Q_END + KV_WIDTH
V_END = K_END + KV_WIDTH
B_END = V_END + CONV_CH
C_END = B_END + CONV_CH
IN_WIDTH = C_END + CONV_CH
N_MEM = 256
MEM_HEADS = 4
MEM_HEAD_DIM = 256
N_GROUPS = 4
EXPERTS_PER_GROUP = 8
N_EXPERTS = 32
EXPERT_FF = 256
ALPHA = 2.0 ** 0.25
LN_EPS = 1e-5

LANES = 128
SUBLANES = 8
ROW_CHUNKS = D_MODEL // LANES
VMEM_LIMIT = 56 * 1024 * 1024

N_PROMPT = SEQ
N_SAMPLE = DEC_BATCH * DEC_SEQ
N_ALL = N_PROMPT + N_SAMPLE
TM_POST = 512
TM_MOE = 512
TM_COMB = 512
MOE_BLOCK = 256
MEM_SLOTS = 3
MOE_X_SLOTS = 3
N_ASSIGN = 2 * N_ALL
MOE_TILES = N_ASSIGN // TM_MOE
MOE_ITEMS = MOE_TILES + N_EXPERTS
SAMPLE_BB = 4
SWA_BB = 16

assert ROW_CHUNKS == SUBLANES
assert N_SAMPLE == TM_POST
assert N_ASSIGN % TM_MOE == 0 and N_ALL % TM_COMB == 0


def _params(sem, vmem=VMEM_LIMIT):
    return pltpu.CompilerParams(dimension_semantics=sem, vmem_limit_bytes=vmem)


def _bf16(x):
    return x.astype(jnp.bfloat16)


def _dot(a, b):
    return jnp.dot(a, b, preferred_element_type=jnp.float32)


def _dot_nt(a, b):
    return lax.dot_general(a, b, (((1,), (1,)), ((), ())), preferred_element_type=jnp.float32)


def _layer_norm(x, g, b):
    mu = jnp.mean(x, axis=-1, keepdims=True)
    xc = x - mu
    var = jnp.mean(xc * xc, axis=-1, keepdims=True)
    return xc * lax.rsqrt(var + LN_EPS) * g + b


def _rope(x, cos_t, sin_t):
    lane = lax.broadcasted_iota(jnp.int32, x.shape, 1) % HEAD_DIM
    half = ROPE_DIM // 2
    partner = jnp.where(lane < half, pltpu.roll(x, LANES - half, axis=1), pltpu.roll(x, half, axis=1))
    return x * cos_t + partner * sin_t


def _head_slabs(x):
    lane = lax.broadcasted_iota(jnp.int32, x.shape, 1)
    lo = lane < HEAD_DIM
    sw = pltpu.roll(x, HEAD_DIM, axis=1)
    zero = jnp.zeros_like(x)
    slabs = [jnp.where(lo, x, zero), jnp.where(lo, zero, sw), jnp.where(lo, sw, zero), jnp.where(lo, zero, x)]
    return _bf16(jnp.concatenate(slabs, axis=1))


def _store_token_tiles(ref, val):
    rows = val.shape[0]
    for c in range(ROW_CHUNKS):
        ref[pl.ds(c, rows, stride=ROW_CHUNKS), :] = val[:, c * LANES:(c + 1) * LANES]


def _load_token_tiles(ref, base, rows):
    return jnp.concatenate(
        [ref[pl.ds(base + c, rows, stride=ROW_CHUNKS), :] for c in range(ROW_CHUNKS)], axis=1)


ROPE_ONE = 3 * (ROPE_DIM // 2)
ROPE_ROWS = 32


def _rope_patterns(tab):
    half = ROPE_DIM // 2
    m = lax.broadcasted_iota(jnp.int32, tab.shape, 1) % HEAD_DIM
    idx_c = jnp.where(m < ROPE_DIM, m % half, ROPE_ONE)
    idx_s = jnp.where(m < half, 2 * half + m, jnp.where(m < ROPE_DIM, m, ROPE_ONE + 1))
    return jnp.take_along_axis(tab, idx_c, axis=1), jnp.take_along_axis(tab, idx_s, axis=1)


def _proj_common(x_ref, w_ref, tab_ref):
    xb = _bf16(x_ref[...])
    tab = tab_ref[...]
    pad = jnp.zeros((LANES - tab.shape[0], tab.shape[1]), jnp.float32)
    cos_t, sin_t = _rope_patterns(jnp.concatenate([tab, pad], axis=0).T)
    q = _dot(xb, w_ref[:, 0:Q_END])
    q_rot = jnp.concatenate(
        [_rope(q[:, p * LANES:(p + 1) * LANES], cos_t, sin_t) for p in range(ATTN_WIDTH // LANES)], axis=1)
    q_out = _bf16(q_rot * (HEAD_DIM ** -0.5))
    kv = _dot(xb, w_ref[:, Q_END:V_END])
    k = _rope(kv[:, 0:KV_WIDTH], cos_t, sin_t)
    v = kv[:, KV_WIDTH:]
    bg = _dot(xb, w_ref[:, V_END:B_END])
    u = _dot(xb, w_ref[:, B_END:C_END]) * _dot(xb, w_ref[:, C_END:IN_WIDTH])
    return q_out, k, v, bg, u


def _conv3(bg, u, u1, u2, cw_ref):
    cw = cw_ref[...]
    return bg * (cw[0:1, :] * u2 + cw[1:2, :] * u1 + cw[2:3, :] * u)


def _proj_prompt_body(x_ref, w_ref, tab_ref, cw_ref,
                      q_ref, kx_ref, vx_ref, conv_ref, ktail_ref, vtail_ref, utail_ref, carry_ref):
    @pl.when(pl.program_id(0) == 0)
    def _():
        carry_ref[...] = jnp.zeros_like(carry_ref)

    q_out, k, v, bg, u = _proj_common(x_ref, w_ref, tab_ref)
    tm = u.shape[0]
    ext = jnp.concatenate([carry_ref[...], u], axis=0)
    u1 = pltpu.roll(ext, 1, axis=0)[SUBLANES:SUBLANES + tm]
    u2 = pltpu.roll(ext, 2, axis=0)[SUBLANES:SUBLANES + tm]
    q_ref[...] = q_out
    kx_ref[...] = _head_slabs(k)
    vx_ref[...] = _head_slabs(v)
    conv_ref[...] = _bf16(_conv3(bg, u, u1, u2, cw_ref))
    ktail_ref[...] = k[tm - WINDOW:tm]
    vtail_ref[...] = v[tm - WINDOW:tm]
    utail_ref[...] = u[tm - SUBLANES:tm]
    carry_ref[...] = u[tm - SUBLANES:tm]


def _proj_sample_kernel(x_ref, w_ref, tab_ref, cw_ref, c0_ref, c1_ref,
                        q_ref, k_ref, v_ref, conv_ref, u_ref):
    q_out, k, v, bg, u = _proj_common(x_ref, w_ref, tab_ref)
    t = lax.broadcasted_iota(jnp.int32, u.shape, 0) % DEC_SEQ
    c0 = c0_ref[...]
    c1 = c1_ref[...]
    u1 = jnp.where(t >= 1, pltpu.roll(u, 1, axis=0), c1)
    u2 = jnp.where(t >= 2, pltpu.roll(u, 2, axis=0), jnp.where(t == 1, c1, c0))
    q_ref[...] = q_out.astype(jnp.float32)
    k_ref[...] = k
    v_ref[...] = v
    conv_ref[...] = _bf16(_conv3(bg, u, u1, u2, cw_ref))
    u_ref[...] = u


def _rope_table(pos):
    half = ROPE_DIM // 2
    inv = ROPE_THETA ** (-jnp.arange(0, ROPE_DIM, 2, dtype=jnp.float32) / ROPE_DIM)
    ang = pos.astype(jnp.float32)[None, :] * inv[:, None]
    cos, sin = jnp.cos(ang), jnp.sin(ang)
    n = pos.shape[0]
    assert ROPE_ONE == 3 * half
    return jnp.concatenate([cos, sin, -sin, jnp.ones((1, n), jnp.float32),
                            jnp.zeros((ROPE_ROWS - ROPE_ONE - 1, n), jnp.float32)], axis=0)


def _proj_sample(x, w_in_b, tab, conv_w, c0, c1):
    n = x.shape[0]
    full = lambda a: pl.BlockSpec(a.shape, lambda i: (0,) * a.ndim)
    out = lambda w, dt: jax.ShapeDtypeStruct((n, w), dt)
    blk = lambda w: pl.BlockSpec((n, w), lambda i: (0, 0))
    return pl.pallas_call(
        _proj_sample_kernel,
        grid=(1,),
        in_specs=[full(x), full(w_in_b), full(tab), full(conv_w), full(c0), full(c1)],
        out_specs=[blk(ATTN_WIDTH), blk(KV_WIDTH), blk(KV_WIDTH), blk(CONV_CH), blk(CONV_CH)],
        out_shape=[out(ATTN_WIDTH, jnp.float32), out(KV_WIDTH, jnp.float32), out(KV_WIDTH, jnp.float32),
                   out(CONV_CH, jnp.bfloat16), out(CONV_CH, jnp.float32)],
        compiler_params=_params(("arbitrary",)),
        name="proj_sample",
    )(x, w_in_b, tab, conv_w, c0, c1)


def _sink_softmax_pv(s, valid, sink, vx):
    s = jnp.where(valid, s, -jnp.inf)
    m = jnp.maximum(jnp.max(s, axis=1, keepdims=True), sink)
    p = jnp.exp(s - m)
    den = jnp.sum(p, axis=1, keepdims=True) + jnp.exp(sink - m)
    return _dot(_bf16(p), vx) / den


SWA_QB = 4


def _swa_tile(step, sinks_ref, q_ref, kc_ref, kp_ref, vc_ref, vp_ref, store):
    kall = jnp.concatenate([kp_ref[...], kc_ref[...]], axis=0)
    vall = jnp.concatenate([vp_ref[...], vc_ref[...]], axis=0)
    i = lax.broadcasted_iota(jnp.int32, (WINDOW, 2 * WINDOW), 0)
    j = lax.broadcasted_iota(jnp.int32, (WINDOW, 2 * WINDOW), 1)
    band = (j > i) & (j <= i + WINDOW)
    for sb in range(SWA_QB):
        rows = slice(sb * WINDOW, (sb + 1) * WINDOW)
        kcat = kall[sb * WINDOW:(sb + 2) * WINDOW]
        vcat = vall[sb * WINDOW:(sb + 2) * WINDOW]
        valid = band & ((step > 0) | (j >= WINDOW)) if sb == 0 else band
        for p in range(N_HEADS // 2):
            qs = q_ref[rows, p * LANES:(p + 1) * LANES]
            acc = None
            for e in range(2):
                hd = 2 * p + e
                slab = 2 * (hd // (N_HEADS // N_KV_HEADS)) + e
                kx = kcat[:, slab * LANES:(slab + 1) * LANES]
                vx = vcat[:, slab * LANES:(slab + 1) * LANES]
                o = _sink_softmax_pv(_dot_nt(qs, kx), valid, sinks_ref[hd], vx)
                acc = o if acc is None else acc + o
            store(rows, slice(p * LANES, (p + 1) * LANES), _bf16(acc))


SWA_ROWS = N_HEADS * DEC_SEQ
NEW_ROWS = 2 * SUBLANES


def _swa_sample_kernel(q_ref, sink_ref, kn_ref, vn_ref, kt_ref, vt_ref, o_ref, okt_ref, ovt_ref):
    nb = SWA_BB
    rows = nb * SWA_ROWS
    t = lax.broadcasted_iota(jnp.int32, (rows, WINDOW), 0) % DEC_SEQ
    valid_c = lax.broadcasted_iota(jnp.int32, (rows, WINDOW), 1) > t
    valid_n = (lax.broadcasted_iota(jnp.int32, (rows, NEW_ROWS), 1)
               <= lax.broadcasted_iota(jnp.int32, (rows, NEW_ROWS), 0) % DEC_SEQ)
    sink = jnp.concatenate([sink_ref[:, 0:1]] * nb, axis=0)
    qs = [_bf16(q_ref[b]) for b in range(nb)]
    s_c = jnp.concatenate([_dot(qs[b], _bf16(kt_ref[b])) for b in range(nb)], axis=0)
    s_n = jnp.concatenate([_dot_nt(qs[b], _bf16(kn_ref[b])) for b in range(nb)], axis=0)
    s_c = jnp.where(valid_c, s_c, -jnp.inf)
    s_n = jnp.where(valid_n, s_n, -jnp.inf)
    m = jnp.maximum(jnp.maximum(jnp.max(s_c, axis=1, keepdims=True), jnp.max(s_n, axis=1, keepdims=True)), sink)
    p_c = jnp.exp(s_c - m)
    p_n = jnp.exp(s_n - m)
    rden = 1.0 / (jnp.sum(p_c, axis=1, keepdims=True) + jnp.sum(p_n, axis=1, keepdims=True) + jnp.exp(sink - m))
    p_c, p_n = _bf16(p_c), _bf16(p_n)
    lane = lax.broadcasted_iota(jnp.int32, (KV_WIDTH, WINDOW), 1)
    shift = WINDOW - DEC_SEQ
    zrows = jnp.zeros((KV_WIDTH - NEW_ROWS, KV_WIDTH), jnp.float32)
    for b in range(nb):
        r = slice(b * SWA_ROWS, (b + 1) * SWA_ROWS)
        kt, vt = kt_ref[b], vt_ref[b]
        kn, vn = kn_ref[b], vn_ref[b]
        o_ref[b] = (_dot_nt(p_c[r], _bf16(vt)) + _dot(p_n[r], _bf16(vn))) * rden[r]
        for old, new, dst in ((kt, kn, okt_ref), (vt, vn, ovt_ref)):
            new_cols = pltpu.roll(jnp.concatenate([new, zrows], axis=0).T, shift, axis=1)
            dst[b] = jnp.where(lane >= shift, new_cols, pltpu.roll(old, shift, axis=1))


def _swa_sample(sinks, q, kn, vn, cache_k, cache_v):
    nb = cache_k.shape[0]
    bb = SWA_BB
    groups = N_HEADS // N_KV_HEADS
    qh = q.reshape(nb, DEC_SEQ, N_KV_HEADS, groups, HEAD_DIM).transpose(0, 2, 3, 1, 4)
    qh = qh.reshape(nb, N_KV_HEADS, groups * DEC_SEQ, HEAD_DIM)
    zeros = jnp.zeros_like(qh[:, 0])
    qbd = jnp.concatenate([jnp.concatenate([qh[:, 0], zeros], axis=-1),
                           jnp.concatenate([zeros, qh[:, 1]], axis=-1)], axis=1)
    sink_col = jnp.broadcast_to(jnp.repeat(sinks, DEC_SEQ).reshape(SWA_ROWS, 1), (SWA_ROWS, LANES))
    pad8 = lambda a: jnp.pad(a.reshape(nb, DEC_SEQ, KV_WIDTH), ((0, 0), (0, NEW_ROWS - DEC_SEQ), (0, 0)))
    to_t = lambda c: c.transpose(0, 2, 3, 1).reshape(nb, KV_WIDTH, WINDOW)
    blk = lambda r, w: pl.BlockSpec((bb, r, w), lambda i: (i, 0, 0))
    o, okt, ovt = pl.pallas_call(
        _swa_sample_kernel,
        grid=(nb // bb,),
        in_specs=[blk(SWA_ROWS, KV_WIDTH), pl.BlockSpec((SWA_ROWS, LANES), lambda i: (0, 0)),
                  blk(NEW_ROWS, KV_WIDTH), blk(NEW_ROWS, KV_WIDTH), blk(KV_WIDTH, WINDOW), blk(KV_WIDTH, WINDOW)],
        out_specs=[blk(SWA_ROWS, KV_WIDTH), blk(KV_WIDTH, WINDOW), blk(KV_WIDTH, WINDOW)],
        out_shape=[jax.ShapeDtypeStruct((nb, SWA_ROWS, KV_WIDTH), jnp.float32),
                   jax.ShapeDtypeStruct((nb, KV_WIDTH, WINDOW), jnp.float32),
                   jax.ShapeDtypeStruct((nb, KV_WIDTH, WINDOW), jnp.float32)],
        compiler_params=_params(("arbitrary",)),
        name="swa_sample",
    )(qbd, sink_col, pad8(kn), pad8(vn), to_t(cache_k), to_t(cache_v))
    o = o.reshape(nb, N_KV_HEADS, groups, DEC_SEQ, N_KV_HEADS, HEAD_DIM)
    attn = jnp.stack([o[:, h, :, :, h, :] for h in range(N_KV_HEADS)], axis=1)
    attn = attn.transpose(0, 3, 1, 2, 4).reshape(nb * DEC_SEQ, ATTN_WIDTH)
    from_t = lambda c: c.reshape(nb, N_KV_HEADS, HEAD_DIM, WINDOW).transpose(0, 3, 1, 2)
    return attn, from_t(okt), from_t(ovt)


def _mem_kv_kernel(mem_ref, wk_ref, wv_ref, wq_ref, wo_ref, mk_ref, mv_ref, wqk_ref, wvo_ref):
    mb = _bf16(mem_ref[...])
    mk = _dot(mb, wk_ref[...])
    mv = _dot(mb, wv_ref[...])
    mk_ref[...] = mk
    mv_ref[...] = mv
    mkb, mvb = _bf16(mk), _bf16(mv)
    for h in range(MEM_HEADS):
        sl = slice(h * MEM_HEAD_DIM, (h + 1) * MEM_HEAD_DIM)
        keys = slice(h * N_MEM, (h + 1) * N_MEM)
        wqk_ref[:, keys] = _bf16(_dot_nt(wq_ref[:, sl], mkb[:, sl]) * (MEM_HEAD_DIM ** -0.5))
        wvo_ref[keys, :] = _bf16(_dot(mvb[:, sl], wo_ref[sl, :]))


def _mem_kv(mem, wk_b, wv_b, wq_b, wo_b):
    full = lambda a: pl.BlockSpec(a.shape, lambda i: (0,) * a.ndim)
    blk = pl.BlockSpec((N_MEM, D_MODEL), lambda i: (0, 0))
    f32 = jax.ShapeDtypeStruct((N_MEM, D_MODEL), jnp.float32)
    fused = (D_MODEL, MEM_HEADS * N_MEM), (MEM_HEADS * N_MEM, D_MODEL)
    return pl.pallas_call(
        _mem_kv_kernel,
        grid=(1,),
        in_specs=[full(mem), full(wk_b), full(wv_b), full(wq_b), full(wo_b)],
        out_specs=[blk, blk] + [pl.BlockSpec(shp, lambda i: (0, 0)) for shp in fused],
        out_shape=[f32, f32] + [jax.ShapeDtypeStruct(shp, jnp.bfloat16) for shp in fused],
        compiler_params=_params(("arbitrary",)),
        name="mem_kv",
    )(mem, wk_b, wv_b, wq_b, wo_b)


def _mix_ln1(attn_ref, conv_ref, x_ref, wmix_ref, g1_ref, b1_ref):
    mix = _dot(_bf16(attn_ref[...]), wmix_ref[0:ATTN_WIDTH, :]) + _dot(conv_ref[...], wmix_ref[ATTN_WIDTH:, :])
    return _layer_norm(ALPHA * x_ref[...] + mix, g1_ref[...], b1_ref[...])


def _mem_q(h1, wq_ref):
    return _bf16(_dot(_bf16(h1), wq_ref[...]) * (MEM_HEAD_DIM ** -0.5))


def _route(h2, wrhl_ref, br_ref, tri_ref, carry):
    hi = _bf16(h2)
    lo = _bf16(h2 - hi.astype(jnp.float32))
    hh = _dot(hi, wrhl_ref[...])
    logits = hh[:, 0:LANES] + hh[:, LANES:] + _dot(lo, wrhl_ref[:, 0:LANES]) + br_ref[...]
    lane_i = lax.broadcasted_iota(jnp.int32, logits.shape, 1)
    lane = lane_i.astype(jnp.float32)
    big = jnp.float32(LANES)
    is_g = lane_i < N_GROUPS
    gl = jnp.where(is_g, logits, -jnp.inf)
    gmax = jnp.max(gl, axis=1, keepdims=True)
    gidx = jnp.min(jnp.where(is_g & (logits == gmax), lane, big), axis=1, keepdims=True)
    gsum = jnp.sum(jnp.exp(gl - gmax), axis=1, keepdims=True)
    gw = 1.0 / gsum
    eid = lane_i - N_GROUPS
    assert EXPERTS_PER_GROUP == 8
    grp = lax.shift_right_arithmetic(eid, jnp.full_like(eid, 3)).astype(jnp.float32)
    in_e = (lane_i >= N_GROUPS) & (lane_i < N_GROUPS + N_EXPERTS) & (grp == gidx)
    v1 = jnp.max(jnp.where(in_e, logits, -jnp.inf), axis=1, keepdims=True)
    i1 = jnp.min(jnp.where(in_e & (logits == v1), lane, big), axis=1, keepdims=True)
    rest = in_e & (lane != i1)
    v2 = jnp.max(jnp.where(rest, logits, -jnp.inf), axis=1, keepdims=True)
    i2 = jnp.min(jnp.where(rest & (logits == v2), lane, big), axis=1, keepdims=True)
    ex = jnp.exp(v2 - v1)
    den = 1.0 + ex
    w1 = gw / den
    w2 = gw * ex / den
    zero = jnp.zeros_like(logits)
    pick1 = lane == i1
    pick2 = lane == i2
    sel = jnp.where(pick1 | pick2, 1.0, 0.0)
    before = _dot(tri_ref[...], _bf16(sel)) + carry
    rank1 = jnp.sum(jnp.where(pick1, before, zero), axis=1, keepdims=True)
    rank2 = jnp.sum(jnp.where(pick2, before, zero), axis=1, keepdims=True)
    cols = (i1 - N_GROUPS, i2 - N_GROUPS, w1, w2, rank1, rank2)
    route = zero
    for k, col in enumerate(cols):
        route = jnp.where(lane_i == k, col, route)
    return route, carry + jnp.sum(sel, axis=0, keepdims=True)


def _post_tile(attn, conv_ref, x_ref, wmix_ref, g1_ref, b1_ref, wqk_ref, wvo_ref, g2_ref, b2_ref,
               wrhl_ref, br_ref, tri_ref, h2t_ref, rt_ref, cnt_ref, carry_ref):
    mix = _dot(attn, wmix_ref[0:ATTN_WIDTH, :]) + _dot(conv_ref[...], wmix_ref[ATTN_WIDTH:, :])
    h1 = _layer_norm(ALPHA * x_ref[...] + mix, g1_ref[...], b1_ref[...])
    scores = _dot(_bf16(h1), wqk_ref[...])
    probs = []
    for h in range(MEM_HEADS):
        s = scores[:, h * N_MEM:(h + 1) * N_MEM]
        p = jnp.exp(s - jnp.max(s, axis=1, keepdims=True))
        probs.append(_bf16(p / jnp.sum(p, axis=1, keepdims=True)))
    mem_out = _dot(jnp.concatenate(probs, axis=1), wvo_ref[...])
    h2 = _layer_norm(ALPHA * h1 + mem_out, g2_ref[...], b2_ref[...])
    _store_token_tiles(h2t_ref, h2)
    route, carry = _route(h2, wrhl_ref, br_ref, tri_ref, carry_ref[0:1, :])
    rt_ref[...] = route
    carry_ref[...] = jnp.broadcast_to(carry, carry_ref.shape)
    cnt_ref[...] = jnp.broadcast_to(carry, cnt_ref.shape)


def _swa_post_kernel(sinks_ref, q_ref, kc_ref, kp_ref, vc_ref, vp_ref,
                     conv_ref, x_ref, wmix_ref, g1_ref, b1_ref, wqk_ref, wvo_ref, g2_ref, b2_ref,
                     wrhl_ref, br_ref, tri_ref, h2s_ref, rts_ref, cnts_ref, wg_ref, wu_ref, wd_ref,
                     h2t_ref, rt_ref, cnt_ref, wgb_ref, wub_ref, wdb_ref, carry_ref, attn_s):
    t = pl.program_id(0)
    steps = N_PROMPT // TM_POST
    par = t % 2

    def swa(slot):
        def store(rows, cols, val):
            attn_s[slot, rows, cols] = val
        _swa_tile(t, sinks_ref, q_ref, kc_ref, kp_ref, vc_ref, vp_ref, store)

    def post(slot):
        _post_tile(attn_s[slot], conv_ref, x_ref, wmix_ref, g1_ref, b1_ref, wqk_ref, wvo_ref, g2_ref, b2_ref,
                   wrhl_ref, br_ref, tri_ref, h2t_ref, rt_ref, cnt_ref, carry_ref)

    def cast_expert():
        wgb_ref[...] = _bf16(wg_ref[...])
        wub_ref[...] = _bf16(wu_ref[...])
        wdb_ref[...] = _bf16(wd_ref[...])

    @pl.when(t == 0)
    def _():
        carry_ref[...] = cnts_ref[...]
        swa(0)
        cast_expert()

    @pl.when((t >= 1) & (t < steps))
    def _():
        swa(par)
        post(1 - par)
        cast_expert()

    @pl.when(t == steps)
    def _():
        post(1 - par)

    @pl.when(t == steps + 1)
    def _():
        h2t_ref[...] = h2s_ref[...]
        rt_ref[...] = rts_ref[...]


def _swa_post(sinks, q, kx, vx, conv, x, wmix_b, g1, b1, wqk, wvo, g2, b2, wrhl, br, tri, h2t_s, rt_s, cnt_s,
              w_gate, w_up, w_down):
    n = x.shape[0]
    tm = TM_POST
    assert tm == SWA_QB * WINDOW
    steps = n // tm
    assert N_EXPERTS <= steps
    expert = lambda a: pl.BlockSpec((1,) + a.shape[1:], lambda i: (jnp.minimum(i, N_EXPERTS - 1), 0, 0))
    experts = (w_gate, w_up, w_down)
    cur = lambda w: pl.BlockSpec((tm, w), lambda i: (jnp.minimum(i, steps - 1), 0))
    prev = lambda w: pl.BlockSpec((WINDOW, w), lambda i: (jnp.clip(SWA_QB * i - 1, 0, n // WINDOW - 1), 0))
    lag = lambda w: pl.BlockSpec((tm, w), lambda i: (jnp.clip(i - 1, 0, steps - 1), 0))
    full = lambda a: pl.BlockSpec(a.shape, lambda i: (0,) * a.ndim)
    weights = (wmix_b, g1, b1, wqk, wvo, g2, b2, wrhl, br, tri, h2t_s, rt_s, cnt_s)
    n_out = n + h2t_s.shape[0] // ROW_CHUNKS
    out_idx = lambda i: (jnp.where(i > steps, steps, jnp.clip(i - 1, 0, steps - 1)), 0)
    return pl.pallas_call(
        _swa_post_kernel,
        grid=(steps + 2,),
        in_specs=([pl.BlockSpec(memory_space=pltpu.SMEM), cur(ATTN_WIDTH),
                   cur(4 * LANES), prev(4 * LANES), cur(4 * LANES), prev(4 * LANES),
                   lag(CONV_CH), lag(D_MODEL)] + [full(a) for a in weights] + [expert(a) for a in experts]),
        out_specs=[pl.BlockSpec((tm * ROW_CHUNKS, LANES), out_idx),
                   pl.BlockSpec((tm, LANES), out_idx),
                   pl.BlockSpec((SUBLANES, LANES), lambda i: (0, 0))] + [expert(a) for a in experts],
        out_shape=[jax.ShapeDtypeStruct((n_out * ROW_CHUNKS, LANES), jnp.float32),
                   jax.ShapeDtypeStruct((n_out, LANES), jnp.float32),
                   jax.ShapeDtypeStruct((SUBLANES, LANES), jnp.float32)]
                  + [jax.ShapeDtypeStruct(a.shape, jnp.bfloat16) for a in experts],
        scratch_shapes=[pltpu.VMEM((SUBLANES, LANES), jnp.float32),
                        pltpu.VMEM((2, tm, ATTN_WIDTH), jnp.bfloat16)],
        compiler_params=_params(("arbitrary",)),
        name="swa_post_prompt",
    )(sinks, q, kx, kx, vx, vx, conv, x, *weights, *experts)


def _post_a_sample_kernel(attn_ref, conv_ref, x_ref, wmix_ref, g1_ref, b1_ref, wq_ref, h1_ref, qm_ref):
    h1 = _mix_ln1(attn_ref, conv_ref, x_ref, wmix_ref, g1_ref, b1_ref)
    h1_ref[...] = h1
    qm_ref[...] = _mem_q(h1, wq_ref).astype(jnp.float32)


def _post_a_sample(attn, conv, x, wmix_b, g1, b1, wq_b):
    n = x.shape[0]
    args = (attn, conv, x, wmix_b, g1, b1, wq_b)
    full = lambda a: pl.BlockSpec(a.shape, lambda i: (0,) * a.ndim)
    blk = pl.BlockSpec((n, D_MODEL), lambda i: (0, 0))
    return pl.pallas_call(
        _post_a_sample_kernel,
        grid=(1,),
        in_specs=[full(a) for a in args],
        out_specs=[blk, blk],
        out_shape=[jax.ShapeDtypeStruct((n, D_MODEL), jnp.float32),
                   jax.ShapeDtypeStruct((n, D_MODEL), jnp.float32)],
        compiler_params=_params(("arbitrary",)),
        name="post_a_sample",
    )(*args)


MEM_ROWS = MEM_HEADS * DEC_SEQ


def _mem_attn_sample_body(q_ref, mk_ref, mv_ref, b0, o_ref):
    nk = N_MEM * MEM_HEADS
    nb = SAMPLE_BB
    rows = nb * MEM_ROWS
    row_h = (lax.broadcasted_iota(jnp.int32, (rows, nk), 0) % MEM_ROWS) // DEC_SEQ
    key_h = lax.broadcasted_iota(jnp.int32, (rows, nk), 1) % MEM_HEADS
    s = jnp.concatenate(
        [_dot_nt(_bf16(q_ref[b]), _bf16(mk_ref[b0 + b].reshape(nk, MEM_HEAD_DIM))) for b in range(nb)], axis=0)
    s = jnp.where(row_h == key_h, s, -jnp.inf)
    p = jnp.exp(s - jnp.max(s, axis=1, keepdims=True))
    rden = 1.0 / jnp.sum(p, axis=1, keepdims=True)
    p = _bf16(p)
    for b in range(nb):
        r = slice(b * MEM_ROWS, (b + 1) * MEM_ROWS)
        o_ref[b] = _dot(p[r], _bf16(mv_ref[b0 + b].reshape(nk, MEM_HEAD_DIM))) * rden[r]


def _proj_mem_kernel(steps, x_ref, w_ref, tab_ref, cw_ref, mq_ref, mk_hbm, mv_hbm,
                     q_ref, kx_ref, vx_ref, conv_ref, ktail_ref, vtail_ref, utail_ref, mo_ref,
                     carry_ref, kbuf, vbuf, sem):
    i = pl.program_id(0)

    def fetch(step):
        slot = step % MEM_SLOTS
        src = pl.ds(step * SAMPLE_BB, SAMPLE_BB)
        dst = pl.ds(slot * SAMPLE_BB, SAMPLE_BB)
        return (pltpu.make_async_copy(mk_hbm.at[src], kbuf.at[dst], sem.at[0, slot]),
                pltpu.make_async_copy(mv_hbm.at[src], vbuf.at[dst], sem.at[1, slot]))

    @pl.when(i == 0)
    def _():
        for k in range(MEM_SLOTS - 1):
            for c in fetch(jnp.int32(k)):
                c.start()

    @pl.when(i + MEM_SLOTS - 1 < steps)
    def _():
        for c in fetch(i + MEM_SLOTS - 1):
            c.start()

    for c in fetch(i):
        c.wait()
    _mem_attn_sample_body(mq_ref, kbuf, vbuf, (i % MEM_SLOTS) * SAMPLE_BB, mo_ref)
    _proj_prompt_body(x_ref, w_ref, tab_ref, cw_ref,
                      q_ref, kx_ref, vx_ref, conv_ref, ktail_ref, vtail_ref, utail_ref, carry_ref)


def _proj_prompt_mem_sample(x, w_in_b, tab, conv_w, qm, mk, mv):
    n = x.shape[0]
    nb = mk.shape[0]
    bb = SAMPLE_BB
    steps = nb // bb
    tm = n // steps
    row = lambda w: pl.BlockSpec((tm, w), lambda i: (i, 0))
    full = lambda a: pl.BlockSpec(a.shape, lambda i: (0,) * a.ndim)
    const = lambda r, w: pl.BlockSpec((r, w), lambda i: (0, 0))
    mq = qm.reshape(nb, DEC_SEQ, MEM_HEADS, MEM_HEAD_DIM).transpose(0, 2, 1, 3).reshape(nb, MEM_ROWS, MEM_HEAD_DIM)
    mrows = pl.BlockSpec((bb, MEM_ROWS, MEM_HEAD_DIM), lambda i: (i, 0, 0))
    kv = pl.BlockSpec(memory_space=pl.ANY)
    assert steps >= MEM_SLOTS - 1
    ring = pltpu.VMEM((MEM_SLOTS * bb, N_MEM, MEM_HEADS, MEM_HEAD_DIM), jnp.float32)
    outs = pl.pallas_call(
        functools.partial(_proj_mem_kernel, steps),
        grid=(steps,),
        in_specs=[row(D_MODEL), full(w_in_b), pl.BlockSpec((ROPE_ROWS, tm), lambda i: (0, i)), full(conv_w),
                  mrows, kv, kv],
        out_specs=[row(ATTN_WIDTH), row(4 * LANES), row(4 * LANES), row(CONV_CH),
                   const(WINDOW, KV_WIDTH), const(WINDOW, KV_WIDTH), const(SUBLANES, CONV_CH), mrows],
        out_shape=[jax.ShapeDtypeStruct((n, ATTN_WIDTH), jnp.bfloat16),
                   jax.ShapeDtypeStruct((n, 4 * LANES), jnp.bfloat16),
                   jax.ShapeDtypeStruct((n, 4 * LANES), jnp.bfloat16),
                   jax.ShapeDtypeStruct((n, CONV_CH), jnp.bfloat16),
                   jax.ShapeDtypeStruct((WINDOW, KV_WIDTH), jnp.float32),
                   jax.ShapeDtypeStruct((WINDOW, KV_WIDTH), jnp.float32),
                   jax.ShapeDtypeStruct((SUBLANES, CONV_CH), jnp.float32),
                   jax.ShapeDtypeStruct((nb, MEM_ROWS, MEM_HEAD_DIM), jnp.float32)],
        scratch_shapes=[pltpu.VMEM((SUBLANES, CONV_CH), jnp.float32), ring, ring,
                        pltpu.SemaphoreType.DMA((2, MEM_SLOTS))],
        compiler_params=_params(("arbitrary",)),
        name="proj_prompt_mem_sample",
    )(x, w_in_b, tab, conv_w, mq, mk, mv)
    o = outs[7].reshape(nb, MEM_HEADS, DEC_SEQ, MEM_HEAD_DIM).transpose(0, 2, 1, 3).reshape(nb * DEC_SEQ, D_MODEL)
    return outs[:7], o


def _post_b_sample_kernel(o_ref, h1_ref, wo_ref, g2_ref, b2_ref, wrhl_ref, br_ref, tri_ref,
                          h2t_ref, rt_ref, cnt_ref):
    h2 = _layer_norm(ALPHA * h1_ref[...] + _dot(_bf16(o_ref[...]), wo_ref[...]), g2_ref[...], b2_ref[...])
    _store_token_tiles(h2t_ref, h2)
    route, carry = _route(h2, wrhl_ref, br_ref, tri_ref, jnp.zeros((1, LANES), jnp.float32))
    rt_ref[...] = route
    cnt_ref[...] = jnp.broadcast_to(carry, cnt_ref.shape)


def _post_b_sample(o, h1, wo_b, g2, b2, wrhl, br, tri):
    n = h1.shape[0]
    args = (o, h1, wo_b, g2, b2, wrhl, br, tri)
    full = lambda a: pl.BlockSpec(a.shape, lambda i: (0,) * a.ndim)
    return pl.pallas_call(
        _post_b_sample_kernel,
        grid=(1,),
        in_specs=[full(a) for a in args],
        out_specs=[pl.BlockSpec((n * ROW_CHUNKS, LANES), lambda i: (0, 0)),
                   pl.BlockSpec((n, LANES), lambda i: (0, 0)),
                   pl.BlockSpec((SUBLANES, LANES), lambda i: (0, 0))],
        out_shape=[jax.ShapeDtypeStruct((n * ROW_CHUNKS, LANES), jnp.float32),
                   jax.ShapeDtypeStruct((n, LANES), jnp.float32),
                   jax.ShapeDtypeStruct((SUBLANES, LANES), jnp.float32)],
        compiler_params=_params(("arbitrary",)),
        name="post_b_sample",
    )(*args)


def _row_gather_copy(src_hbm, idx, dst, dst_row, sem):
    s0 = pl.multiple_of(idx * ROW_CHUNKS, ROW_CHUNKS)
    d0 = pl.multiple_of(dst_row * ROW_CHUNKS, ROW_CHUNKS)
    return pltpu.make_async_copy(src_hbm.at[pl.ds(s0, ROW_CHUNKS), :], dst.at[pl.ds(d0, ROW_CHUNKS), :], sem)


def _dispatch_kernel(pos_ref, h2t_ref, xs_hbm, sem):
    def body(r, c):
        src = h2t_ref.at[pl.ds(pl.multiple_of(r * ROW_CHUNKS, ROW_CHUNKS), ROW_CHUNKS), :]
        for k in range(2):
            d0 = pl.multiple_of(pos_ref[0, 0, k * TM_COMB + r] * ROW_CHUNKS, ROW_CHUNKS)
            pltpu.make_async_copy(src, xs_hbm.at[pl.ds(d0, ROW_CHUNKS), :], sem.at[0]).start(priority=k)
        return c
    lax.fori_loop(0, TM_COMB, body, 0, unroll=8)
    for _ in range(2):
        pltpu.make_async_copy(h2t_ref, xs_hbm.at[pl.ds(0, TM_COMB * ROW_CHUNKS), :], sem.at[0]).wait()


def _dispatch(pos3, h2t):
    nt = N_ALL // TM_COMB
    return pl.pallas_call(
        _dispatch_kernel,
        grid=(nt,),
        in_specs=[pl.BlockSpec((1, 1, 2 * TM_COMB), lambda i: (i, 0, 0), memory_space=pltpu.SMEM),
                  pl.BlockSpec((TM_COMB * ROW_CHUNKS, LANES), lambda i: (i, 0))],
        out_specs=pl.BlockSpec(memory_space=pl.ANY),
        out_shape=jax.ShapeDtypeStruct((N_ASSIGN * ROW_CHUNKS, LANES), jnp.float32),
        scratch_shapes=[pltpu.SemaphoreType.DMA((1,))],
        compiler_params=_params(("arbitrary",)),
        name="moe_dispatch",
    )(pos3, h2t)


def _moe_ffn_kernel(it_ref, ie_ref, lo_ref, hi_ref, x_hbm, wg_ref, wu_ref, wd_ref, y_ref, xbuf, sem):
    i = pl.program_id(0)
    lo = lo_ref[i]
    hi = hi_ref[i]
    t = it_ref[i]
    tile_rows = TM_MOE * ROW_CHUNKS

    def fetch(tile):
        src = pl.multiple_of(tile * tile_rows, tile_rows)
        dst = pl.multiple_of((tile % MOE_X_SLOTS) * tile_rows, tile_rows)
        return pltpu.make_async_copy(x_hbm.at[pl.ds(src, tile_rows), :], xbuf.at[pl.ds(dst, tile_rows), :],
                                     sem.at[tile % MOE_X_SLOTS])

    @pl.when(i == 0)
    def _():
        for k in range(MOE_X_SLOTS - 1):
            fetch(jnp.int32(k)).start()

    @pl.when((i == 0) | (t != it_ref[jnp.maximum(i - 1, 0)]))
    def _():
        @pl.when(t + MOE_X_SLOTS - 1 < MOE_TILES)
        def _():
            fetch(t + MOE_X_SLOTS - 1).start()
        fetch(t).wait()

    base = pl.multiple_of((t % MOE_X_SLOTS) * tile_rows, tile_rows)

    def ffn(r0, rows):
        x = _bf16(_load_token_tiles(xbuf, base + r0 * ROW_CHUNKS, rows))
        hg = _dot(x, wg_ref[0])
        hu = _dot(x, wu_ref[0])
        h = hg / (1.0 + jnp.exp(-hg)) * hu
        return _dot(_bf16(h), wd_ref[0])

    whole = (lo == 0) & (hi == TM_MOE)

    @pl.when(whole)
    def _():
        _store_token_tiles(y_ref, ffn(0, TM_MOE))

    for r0 in range(0, TM_MOE, MOE_BLOCK):
        live = (hi > lo) & jnp.logical_not(whole) & (hi > r0) & (lo < r0 + MOE_BLOCK)

        def store(merge, r0=r0):
            y = ffn(r0, MOE_BLOCK)
            row = r0 + lax.broadcasted_iota(jnp.int32, (MOE_BLOCK, LANES), 0)
            mask = (row >= lo) & (row < hi)
            for c in range(ROW_CHUNKS):
                sl = pl.ds(r0 * ROW_CHUNKS + c, MOE_BLOCK, stride=ROW_CHUNKS)
                y_ref[sl, :] = jnp.where(mask, y[:, c * LANES:(c + 1) * LANES], y_ref[sl, :] if merge else 0.0)

        pl.when(live & (lo <= r0))(functools.partial(store, False))
        pl.when(live & (lo > r0))(functools.partial(store, True))


def _moe_ffn(item_tile, item_expert, item_lo, item_hi, x_sorted, w_gate, w_up, w_down):
    wspec = lambda shp: pl.BlockSpec((1,) + shp, lambda i, it, ie, lo, hi: (ie[i], 0, 0))
    tile = pl.BlockSpec((TM_MOE * ROW_CHUNKS, LANES), lambda i, it, ie, lo, hi: (it[i], 0))
    assert MOE_TILES >= MOE_X_SLOTS - 1
    grid_spec = pltpu.PrefetchScalarGridSpec(
        num_scalar_prefetch=4,
        grid=(MOE_ITEMS,),
        in_specs=[pl.BlockSpec(memory_space=pl.ANY),
                  wspec((D_MODEL, EXPERT_FF)), wspec((D_MODEL, EXPERT_FF)), wspec((EXPERT_FF, D_MODEL))],
        out_specs=tile,
        scratch_shapes=[pltpu.VMEM((MOE_X_SLOTS * TM_MOE * ROW_CHUNKS, LANES), jnp.float32),
                        pltpu.SemaphoreType.DMA((MOE_X_SLOTS,))],
    )
    return pl.pallas_call(
        _moe_ffn_kernel,
        grid_spec=grid_spec,
        out_shape=jax.ShapeDtypeStruct((N_ASSIGN * ROW_CHUNKS, LANES), jnp.float32),
        compiler_params=_params(("arbitrary",)),
        name="moe_ffn",
    )(item_tile, item_expert, item_lo, item_hi, x_sorted, w_gate, w_up, w_down)


def _combine_kernel(nt, pos_cur_ref, pos_nxt_ref, yt_hbm, h2t_ref, rt_ref, g3_ref, b3_ref, o_ref, abuf, sem):
    t = pl.program_id(0)
    slot = t % 2
    rows = 2 * TM_COMB

    def issue(pos_ref, s):
        def body(j, c):
            for k in range(2):
                r = 2 * j + k
                _row_gather_copy(yt_hbm, pos_ref[0, 0, r], abuf, s * rows + r, sem.at[s]).start(priority=k)
            return c
        lax.fori_loop(0, rows // 2, body, 0, unroll=16)

    @pl.when(t == 0)
    def _():
        issue(pos_cur_ref, 0)

    @pl.when(t + 1 < nt)
    def _():
        issue(pos_nxt_ref, 1 - slot)

    base = pl.multiple_of(slot * (rows * ROW_CHUNKS), rows * ROW_CHUNKS)
    pltpu.make_async_copy(yt_hbm.at[pl.ds(0, rows * ROW_CHUNKS), :],
                          abuf.at[pl.ds(base, rows * ROW_CHUNKS), :], sem.at[slot]).wait()
    ya = _load_token_tiles(abuf, base, TM_COMB)
    yb = _load_token_tiles(abuf, base + TM_COMB * ROW_CHUNKS, TM_COMB)
    rt = rt_ref[...]
    ff = rt[:, 2:3] * ya + rt[:, 3:4] * yb
    h2 = _load_token_tiles(h2t_ref, 0, TM_COMB)
    o_ref[...] = _layer_norm(ALPHA * h2 + ff, g3_ref[...], b3_ref[...])


def _combine(pos3, yt, h2t, rt, g3, b3, tile0, n_tiles):
    last = tile0 + n_tiles - 1
    smem_pos = lambda f: pl.BlockSpec((1, 1, 2 * TM_COMB), f, memory_space=pltpu.SMEM)
    full = lambda a: pl.BlockSpec(a.shape, lambda i: (0,) * a.ndim)
    return pl.pallas_call(
        functools.partial(_combine_kernel, n_tiles),
        grid=(n_tiles,),
        in_specs=[smem_pos(lambda i: (tile0 + i, 0, 0)),
                  smem_pos(lambda i: (jnp.minimum(tile0 + i + 1, last), 0, 0)),
                  pl.BlockSpec(memory_space=pl.ANY),
                  pl.BlockSpec((TM_COMB * ROW_CHUNKS, LANES), lambda i: (tile0 + i, 0)),
                  pl.BlockSpec((TM_COMB, LANES), lambda i: (tile0 + i, 0)),
                  full(g3), full(b3)],
        out_specs=pl.BlockSpec((TM_COMB, D_MODEL), lambda i: (i, 0)),
        out_shape=jax.ShapeDtypeStruct((n_tiles * TM_COMB, D_MODEL), jnp.float32),
        scratch_shapes=[pltpu.VMEM((2 * 2 * TM_COMB * ROW_CHUNKS, LANES), jnp.float32),
                        pltpu.SemaphoreType.DMA((2,))],
        compiler_params=_params(("arbitrary",)),
        name="moe_combine",
    )(pos3, pos3, yt, h2t, rt, g3, b3)


POS_TILES = 11


def _positions_kernel(rt_ref, starts_ref, pos_ref):
    expert = lax.broadcasted_iota(jnp.int32, (N_EXPERTS, TM_COMB), 0).astype(jnp.float32)
    starts = jnp.concatenate([starts_ref[...]] * (TM_COMB // LANES), axis=1)
    for j in range(POS_TILES):
        cols = rt_ref[j * TM_COMB:(j + 1) * TM_COMB, :].T
        out = []
        for k in range(2):
            seg = jnp.sum(jnp.where(expert == cols[k:k + 1, :], starts, 0.0), axis=0, keepdims=True)
            out.append(seg + cols[4 + k:5 + k, :])
        pos_ref[j] = jnp.concatenate(out, axis=1).astype(jnp.int32)


def _positions(rt, starts_rep):
    nt = N_ALL // TM_COMB
    assert nt % POS_TILES == 0
    return pl.pallas_call(
        _positions_kernel,
        grid=(nt // POS_TILES,),
        in_specs=[pl.BlockSpec((POS_TILES * TM_COMB, LANES), lambda i: (i, 0)),
                  pl.BlockSpec((N_EXPERTS, LANES), lambda i: (0, 0))],
        out_specs=pl.BlockSpec((POS_TILES, 1, 2 * TM_COMB), lambda i: (i, 0, 0)),
        out_shape=jax.ShapeDtypeStruct((nt, 1, 2 * TM_COMB), jnp.int32),
        compiler_params=_params(("arbitrary",)),
        name="moe_positions",
    )(rt, starts_rep)


def _routing_plan(rt, cnt):
    i32 = jnp.int32
    counts_f = cnt[0, N_GROUPS:N_GROUPS + N_EXPERTS]
    starts_f = jnp.cumsum(counts_f) - counts_f
    pos3 = _positions(rt, jnp.broadcast_to(starts_f[:, None], (N_EXPERTS, LANES)))
    starts = starts_f.astype(i32)
    tiles = jnp.arange(MOE_TILES, dtype=i32) * TM_MOE
    rank_t = jnp.arange(MOE_TILES, dtype=i32) + jnp.sum((starts[None, :] < tiles[:, None]).astype(i32), axis=1)
    rank_s = jnp.arange(N_EXPERTS, dtype=i32) + jnp.sum((tiles[None, :] <= starts[:, None]).astype(i32), axis=1)
    vals = jnp.concatenate([tiles, starts])
    ranks = jnp.concatenate([rank_t, rank_s])
    slot = jnp.arange(MOE_ITEMS, dtype=i32)
    lo = jnp.sum(jnp.where(ranks[None, :] == slot[:, None], vals[None, :], 0), axis=1)
    hi = jnp.concatenate([lo[1:], jnp.full((1,), N_ASSIGN, i32)])
    item_tile = jnp.minimum(lo // TM_MOE, MOE_TILES - 1)
    item_expert = jnp.clip(jnp.sum((starts[None, :] <= lo[:, None]).astype(i32), axis=1) - 1, 0, N_EXPERTS - 1)
    base = item_tile * TM_MOE
    return item_tile, item_expert, lo - base, hi - base, pos3


def kernel(x_prompt, x_sample, mem_prompt, cache_swa_k, cache_swa_v, cache_conv, cache_mem_k, cache_mem_v,
           w_in, sinks, conv_w, w_mix_out, ln1_g, ln1_b, w_q_mem, w_k_mem, w_v_mem, w_o_mem, ln2_g, ln2_b,
           w_router_group, b_router_group, w_router_expert, b_router_expert, w_gate, w_up, w_down,
           ln3_g, ln3_b):
    f32 = jnp.float32
    row = lambda a: a.reshape(1, -1).astype(f32)
    w_in_b, wmix_b, wq_b, wk_b, wv_b, wo_b = (_bf16(w) for w in (w_in, w_mix_out, w_q_mem, w_k_mem, w_v_mem, w_o_mem))
    g1, b1, g2, b2, g3, b3 = (row(a) for a in (ln1_g, ln1_b, ln2_g, ln2_b, ln3_g, ln3_b))
    pad = LANES - N_GROUPS - N_EXPERTS
    wr = jnp.concatenate([w_router_group, w_router_expert, jnp.zeros((D_MODEL, pad), f32)], axis=1)
    wrh = _bf16(wr)
    wrhl = jnp.concatenate([wrh, _bf16(wr - wrh.astype(f32))], axis=1)
    br = jnp.concatenate([b_router_group, b_router_expert, jnp.zeros((pad,), f32)]).reshape(1, LANES)

    xs = x_sample.reshape(N_SAMPLE, D_MODEL)
    tab_s = jnp.tile(_rope_table(PAST_LEN + jnp.arange(DEC_SEQ)), (1, DEC_BATCH))
    c0 = jnp.repeat(cache_conv[:, 0], DEC_SEQ, axis=0)
    c1 = jnp.repeat(cache_conv[:, 1], DEC_SEQ, axis=0)
    q_s, k_s, v_s, conv_s, u_s = _proj_sample(xs, w_in_b, tab_s, conv_w, c0, c1)
    attn_s, swa_k_s, swa_v_s = _swa_sample(sinks, q_s, k_s, v_s, cache_swa_k, cache_swa_v)
    h1_s, qm_s = _post_a_sample(attn_s, conv_s, xs, wmix_b, g1, b1, wq_b)
    xp = x_prompt.reshape(N_PROMPT, D_MODEL)
    tab_p = _rope_table(jnp.arange(N_PROMPT))
    (q_p, kx_p, vx_p, conv_p, k_tail, v_tail, u_tail), o_s = _proj_prompt_mem_sample(
        xp, w_in_b, tab_p, conv_w, qm_s, cache_mem_k, cache_mem_v)
    tri = _bf16(jnp.tril(jnp.ones((TM_POST, TM_POST), f32), -1))
    h2t_s, rt_s, cnt_s = _post_b_sample(o_s, h1_s, wo_b, g2, b2, wrhl, br, tri)

    mk, mv, wqk, wvo = _mem_kv(mem_prompt.reshape(N_MEM, D_MODEL), wk_b, wv_b, wq_b, wo_b)
    h2t, rt, cnt, wg_b, wu_b, wd_b = _swa_post(sinks, q_p, kx_p, vx_p, conv_p, xp, wmix_b, g1, b1, wqk, wvo, g2, b2,
                                               wrhl, br, tri, h2t_s, rt_s, cnt_s, w_gate, w_up, w_down)

    item_tile, item_expert, item_lo, item_hi, pos3 = _routing_plan(rt, cnt)
    x_sorted = _dispatch(pos3, h2t)
    yt = _moe_ffn(item_tile, item_expert, item_lo, item_hi, x_sorted, wg_b, wu_b, wd_b)
    y_p = _combine(pos3, yt, h2t, rt, g3, b3, 0, N_PROMPT // TM_COMB)
    y_s = _combine(pos3, yt, h2t, rt, g3, b3, N_PROMPT // TM_COMB, N_SAMPLE // TM_COMB)

    return (y_p.reshape(1, SEQ, D_MODEL),
            y_s.reshape(DEC_BATCH, DEC_SEQ, D_MODEL),
            k_tail.reshape(1, WINDOW, N_KV_HEADS, HEAD_DIM),
            v_tail.reshape(1, WINDOW, N_KV_HEADS, HEAD_DIM),
            u_tail[SUBLANES - (CONV_K - 1):].reshape(1, CONV_K - 1, CONV_CH),
            mk.reshape(1, N_MEM, MEM_HEADS, MEM_HEAD_DIM),
            mv.reshape(1, N_MEM, MEM_HEADS, MEM_HEAD_DIM),
            swa_k_s.reshape(DEC_BATCH, WINDOW, N_KV_HEADS, HEAD_DIM),
            swa_v_s.reshape(DEC_BATCH, WINDOW, N_KV_HEADS, HEAD_DIM),
            u_s.reshape(DEC_BATCH, DEC_SEQ, CONV_CH)[:, DEC_SEQ - (CONV_K - 1):])
```

```python
import functools

import jax
import jax.numpy as jnp
from jax import lax
from jax.experimental import pallas as pl
from jax.experimental.pallas import tpu as pltpu

D_MODEL = 1024
SEQ = 16384
DEC_BATCH = 128
DEC_SEQ = 4
PAST_LEN = 16384
ATTN_WIDTH = 512
CONV_CH = 512
HEAD_DIM = 64
N_HEADS = 8
N_KV_HEADS = 2
KV_WIDTH = 128
WINDOW = 128
ROPE_THETA = 500000.0
ROPE_DIM = 16
CONV_K = 3
Q_END = ATTN_WIDTH
K_END = Q_END + KV_WIDTH
V_END = K_END + KV_WIDTH
B_END = V_END + CONV_CH
C_END = B_END + CONV_CH
IN_WIDTH = C_END + CONV_CH
N_MEM = 256
MEM_HEADS = 4
MEM_HEAD_DIM = 256
N_GROUPS = 4
EXPERTS_PER_GROUP = 8
N_EXPERTS = 32
EXPERT_FF = 256
ALPHA = 2.0 ** 0.25
LN_EPS = 1e-5

LANES = 128
SUBLANES = 8
ROW_CHUNKS = D_MODEL // LANES
VMEM_LIMIT = 56 * 1024 * 1024

N_PROMPT = SEQ
N_SAMPLE = DEC_BATCH * DEC_SEQ
N_ALL = N_PROMPT + N_SAMPLE
TM_POST = 512
TM_MOE = 512
TM_COMB = 512
MOE_BLOCK = 256
MEM_SLOTS = 3
MOE_X_SLOTS = 3
N_ASSIGN = 2 * N_ALL
MOE_TILES = N_ASSIGN // TM_MOE
MOE_ITEMS = MOE_TILES + N_EXPERTS
SAMPLE_BB = 4
SWA_BB = 16

assert ROW_CHUNKS == SUBLANES
assert N_SAMPLE == TM_POST
assert N_ASSIGN % TM_MOE == 0 and N_ALL % TM_COMB == 0


def _params(sem, vmem=VMEM_LIMIT):
    return pltpu.CompilerParams(dimension_semantics=sem, vmem_limit_bytes=vmem)


def _bf16(x):
    return x.astype(jnp.bfloat16)


def _dot(a, b):
    return jnp.dot(a, b, preferred_element_type=jnp.float32)


def _dot_nt(a, b):
    return lax.dot_general(a, b, (((1,), (1,)), ((), ())), preferred_element_type=jnp.float32)


def _layer_norm(x, g, b):
    mu = jnp.mean(x, axis=-1, keepdims=True)
    xc = x - mu
    var = jnp.mean(xc * xc, axis=-1, keepdims=True)
    return xc * lax.rsqrt(var + LN_EPS) * g + b


def _rope(x, cos_t, sin_t):
    lane = lax.broadcasted_iota(jnp.int32, x.shape, 1) % HEAD_DIM
    half = ROPE_DIM // 2
    partner = jnp.where(lane < half, pltpu.roll(x, LANES - half, axis=1), pltpu.roll(x, half, axis=1))
    return x * cos_t + partner * sin_t


def _head_slabs(x):
    lane = lax.broadcasted_iota(jnp.int32, x.shape, 1)
    lo = lane < HEAD_DIM
    sw = pltpu.roll(x, HEAD_DIM, axis=1)
    zero = jnp.zeros_like(x)
    slabs = [jnp.where(lo, x, zero), jnp.where(lo, zero, sw), jnp.where(lo, sw, zero), jnp.where(lo, zero, x)]
    return _bf16(jnp.concatenate(slabs, axis=1))


def _store_token_tiles(ref, val):
    rows = val.shape[0]
    for c in range(ROW_CHUNKS):
        ref[pl.ds(c, rows, stride=ROW_CHUNKS), :] = val[:, c * LANES:(c + 1) * LANES]


def _load_token_tiles(ref, base, rows):
    return jnp.concatenate(
        [ref[pl.ds(base + c, rows, stride=ROW_CHUNKS), :] for c in range(ROW_CHUNKS)], axis=1)


ROPE_ONE = 3 * (ROPE_DIM // 2)
ROPE_ROWS = 32


def _rope_patterns(tab):
    half = ROPE_DIM // 2
    m = lax.broadcasted_iota(jnp.int32, tab.shape, 1) % HEAD_DIM
    idx_c = jnp.where(m < ROPE_DIM, m % half, ROPE_ONE)
    idx_s = jnp.where(m < half, 2 * half + m, jnp.where(m < ROPE_DIM, m, ROPE_ONE + 1))
    return jnp.take_along_axis(tab, idx_c, axis=1), jnp.take_along_axis(tab, idx_s, axis=1)


def _proj_common(x_ref, w_ref, tab_ref):
    xb = _bf16(x_ref[...])
    tab = tab_ref[...]
    pad = jnp.zeros((LANES - tab.shape[0], tab.shape[1]), jnp.float32)
    cos_t, sin_t = _rope_patterns(jnp.concatenate([tab, pad], axis=0).T)
    q = _dot(xb, w_ref[:, 0:Q_END])
    q_rot = jnp.concatenate(
        [_rope(q[:, p * LANES:(p + 1) * LANES], cos_t, sin_t) for p in range(ATTN_WIDTH // LANES)], axis=1)
    q_out = _bf16(q_rot * (HEAD_DIM ** -0.5))
    kv = _dot(xb, w_ref[:, Q_END:V_END])
    k = _rope(kv[:, 0:KV_WIDTH], cos_t, sin_t)
    v = kv[:, KV_WIDTH:]
    bg = _dot(xb, w_ref[:, V_END:B_END])
    u = _dot(xb, w_ref[:, B_END:C_END]) * _dot(xb, w_ref[:, C_END:IN_WIDTH])
    return q_out, k, v, bg, u


def _conv3(bg, u, u1, u2, cw_ref):
    cw = cw_ref[...]
    return bg * (cw[0:1, :] * u2 + cw[1:2, :] * u1 + cw[2:3, :] * u)


def _proj_prompt_body(x_ref, w_ref, tab_ref, cw_ref,
                      q_ref, kx_ref, vx_ref, conv_ref, ktail_ref, vtail_ref, utail_ref, carry_ref):
    @pl.when(pl.program_id(0) == 0)
    def _():
        carry_ref[...] = jnp.zeros_like(carry_ref)

    q_out, k, v, bg, u = _proj_common(x_ref, w_ref, tab_ref)
    tm = u.shape[0]
    ext = jnp.concatenate([carry_ref[...], u], axis=0)
    u1 = pltpu.roll(ext, 1, axis=0)[SUBLANES:SUBLANES + tm]
    u2 = pltpu.roll(ext, 2, axis=0)[SUBLANES:SUBLANES + tm]
    q_ref[...] = q_out
    kx_ref[...] = _head_slabs(k)
    vx_ref[...] = _head_slabs(v)
    conv_ref[...] = _bf16(_conv3(bg, u, u1, u2, cw_ref))
    ktail_ref[...] = k[tm - WINDOW:tm]
    vtail_ref[...] = v[tm - WINDOW:tm]
    utail_ref[...] = u[tm - SUBLANES:tm]
    carry_ref[...] = u[tm - SUBLANES:tm]


def _proj_sample_kernel(x_ref, w_ref, tab_ref, cw_ref, c0_ref, c1_ref,
                        q_ref, k_ref, v_ref, conv_ref, u_ref):
    q_out, k, v, bg, u = _proj_common(x_ref, w_ref, tab_ref)
    t = lax.broadcasted_iota(jnp.int32, u.shape, 0) % DEC_SEQ
    c0 = c0_ref[...]
    c1 = c1_ref[...]
    u1 = jnp.where(t >= 1, pltpu.roll(u, 1, axis=0), c1)
    u2 = jnp.where(t >= 2, pltpu.roll(u, 2, axis=0), jnp.where(t == 1, c1, c0))
    q_ref[...] = q_out.astype(jnp.float32)
    k_ref[...] = k
    v_ref[...] = v
    conv_ref[...] = _bf16(_conv3(bg, u, u1, u2, cw_ref))
    u_ref[...] = u


def _rope_table(pos):
    half = ROPE_DIM // 2
    inv = ROPE_THETA ** (-jnp.arange(0, ROPE_DIM, 2, dtype=jnp.float32) / ROPE_DIM)
    ang = pos.astype(jnp.float32)[None, :] * inv[:, None]
    cos, sin = jnp.cos(ang), jnp.sin(ang)
    n = pos.shape[0]
    assert ROPE_ONE == 3 * half
    return jnp.concatenate([cos, sin, -sin, jnp.ones((1, n), jnp.float32),
                            jnp.zeros((ROPE_ROWS - ROPE_ONE - 1, n), jnp.float32)], axis=0)


def _proj_sample(x, w_in_b, tab, conv_w, c0, c1):
    n = x.shape[0]
    full = lambda a: pl.BlockSpec(a.shape, lambda i: (0,) * a.ndim)
    out = lambda w, dt: jax.ShapeDtypeStruct((n, w), dt)
    blk = lambda w: pl.BlockSpec((n, w), lambda i: (0, 0))
    return pl.pallas_call(
        _proj_sample_kernel,
        grid=(1,),
        in_specs=[full(x), full(w_in_b), full(tab), full(conv_w), full(c0), full(c1)],
        out_specs=[blk(ATTN_WIDTH), blk(KV_WIDTH), blk(KV_WIDTH), blk(CONV_CH), blk(CONV_CH)],
        out_shape=[out(ATTN_WIDTH, jnp.float32), out(KV_WIDTH, jnp.float32), out(KV_WIDTH, jnp.float32),
                   out(CONV_CH, jnp.bfloat16), out(CONV_CH, jnp.float32)],
        compiler_params=_params(("arbitrary",)),
        name="proj_sample",
    )(x, w_in_b, tab, conv_w, c0, c1)


def _sink_softmax_pv(s, valid, sink, vx):
    s = jnp.where(valid, s, -jnp.inf)
    m = jnp.maximum(jnp.max(s, axis=1, keepdims=True), sink)
    p = jnp.exp(s - m)
    den = jnp.sum(p, axis=1, keepdims=True) + jnp.exp(sink - m)
    return _dot(_bf16(p), vx) / den


SWA_QB = 4


def _swa_tile(step, sinks_ref, q_ref, kc_ref, kp_ref, vc_ref, vp_ref, store):
    kall = jnp.concatenate([kp_ref[...], kc_ref[...]], axis=0)
    vall = jnp.concatenate([vp_ref[...], vc_ref[...]], axis=0)
    i = lax.broadcasted_iota(jnp.int32, (WINDOW, 2 * WINDOW), 0)
    j = lax.broadcasted_iota(jnp.int32, (WINDOW, 2 * WINDOW), 1)
    band = (j > i) & (j <= i + WINDOW)
    for sb in range(SWA_QB):
        rows = slice(sb * WINDOW, (sb + 1) * WINDOW)
        kcat = kall[sb * WINDOW:(sb + 2) * WINDOW]
        vcat = vall[sb * WINDOW:(sb + 2) * WINDOW]
        valid = band & ((step > 0) | (j >= WINDOW)) if sb == 0 else band
        for p in range(N_HEADS // 2):
            qs = q_ref[rows, p * LANES:(p + 1) * LANES]
            acc = None
            for e in range(2):
                hd = 2 * p + e
                slab = 2 * (hd // (N_HEADS // N_KV_HEADS)) + e
                kx = kcat[:, slab * LANES:(slab + 1) * LANES]
                vx = vcat[:, slab * LANES:(slab + 1) * LANES]
                o = _sink_softmax_pv(_dot_nt(qs, kx), valid, sinks_ref[hd], vx)
                acc = o if acc is None else acc + o
            store(rows, slice(p * LANES, (p + 1) * LANES), _bf16(acc))


SWA_ROWS = N_HEADS * DEC_SEQ
NEW_ROWS = 2 * SUBLANES


def _swa_sample_kernel(q_ref, sink_ref, kn_ref, vn_ref, kt_ref, vt_ref, o_ref, okt_ref, ovt_ref):
    nb = SWA_BB
    rows = nb * SWA_ROWS
    t = lax.broadcasted_iota(jnp.int32, (rows, WINDOW), 0) % DEC_SEQ
    valid_c = lax.broadcasted_iota(jnp.int32, (rows, WINDOW), 1) > t
    valid_n = (lax.broadcasted_iota(jnp.int32, (rows, NEW_ROWS), 1)
               <= lax.broadcasted_iota(jnp.int32, (rows, NEW_ROWS), 0) % DEC_SEQ)
    sink = jnp.concatenate([sink_ref[:, 0:1]] * nb, axis=0)
    qs = [_bf16(q_ref[b]) for b in range(nb)]
    s_c = jnp.concatenate([_dot(qs[b], _bf16(kt_ref[b])) for b in range(nb)], axis=0)
    s_n = jnp.concatenate([_dot_nt(qs[b], _bf16(kn_ref[b])) for b in range(nb)], axis=0)
    s_c = jnp.where(valid_c, s_c, -jnp.inf)
    s_n = jnp.where(valid_n, s_n, -jnp.inf)
    m = jnp.maximum(jnp.maximum(jnp.max(s_c, axis=1, keepdims=True), jnp.max(s_n, axis=1, keepdims=True)), sink)
    p_c = jnp.exp(s_c - m)
    p_n = jnp.exp(s_n - m)
    rden = 1.0 / (jnp.sum(p_c, axis=1, keepdims=True) + jnp.sum(p_n, axis=1, keepdims=True) + jnp.exp(sink - m))
    p_c, p_n = _bf16(p_c), _bf16(p_n)
    lane = lax.broadcasted_iota(jnp.int32, (KV_WIDTH, WINDOW), 1)
    shift = WINDOW - DEC_SEQ
    zrows = jnp.zeros((KV_WIDTH - NEW_ROWS, KV_WIDTH), jnp.float32)
    for b in range(nb):
        r = slice(b * SWA_ROWS, (b + 1) * SWA_ROWS)
        kt, vt = kt_ref[b], vt_ref[b]
        kn, vn = kn_ref[b], vn_ref[b]
        o_ref[b] = (_dot_nt(p_c[r], _bf16(vt)) + _dot(p_n[r], _bf16(vn))) * rden[r]
        for old, new, dst in ((kt, kn, okt_ref), (vt, vn, ovt_ref)):
            new_cols = pltpu.roll(jnp.concatenate([new, zrows], axis=0).T, shift, axis=1)
            dst[b] = jnp.where(lane >= shift, new_cols, pltpu.roll(old, shift, axis=1))


def _swa_sample(sinks, q, kn, vn, cache_k, cache_v):
    nb = cache_k.shape[0]
    bb = SWA_BB
    groups = N_HEADS // N_KV_HEADS
    qh = q.reshape(nb, DEC_SEQ, N_KV_HEADS, groups, HEAD_DIM).transpose(0, 2, 3, 1, 4)
    qh = qh.reshape(nb, N_KV_HEADS, groups * DEC_SEQ, HEAD_DIM)
    zeros = jnp.zeros_like(qh[:, 0])
    qbd = jnp.concatenate([jnp.concatenate([qh[:, 0], zeros], axis=-1),
                           jnp.concatenate([zeros, qh[:, 1]], axis=-1)], axis=1)
    sink_col = jnp.broadcast_to(jnp.repeat(sinks, DEC_SEQ).reshape(SWA_ROWS, 1), (SWA_ROWS, LANES))
    pad8 = lambda a: jnp.pad(a.reshape(nb, DEC_SEQ, KV_WIDTH), ((0, 0), (0, NEW_ROWS - DEC_SEQ), (0, 0)))
    to_t = lambda c: c.transpose(0, 2, 3, 1).reshape(nb, KV_WIDTH, WINDOW)
    blk = lambda r, w: pl.BlockSpec((bb, r, w), lambda i: (i, 0, 0))
    o, okt, ovt = pl.pallas_call(
        _swa_sample_kernel,
        grid=(nb // bb,),
        in_specs=[blk(SWA_ROWS, KV_WIDTH), pl.BlockSpec((SWA_ROWS, LANES), lambda i: (0, 0)),
                  blk(NEW_ROWS, KV_WIDTH), blk(NEW_ROWS, KV_WIDTH), blk(KV_WIDTH, WINDOW), blk(KV_WIDTH, WINDOW)],
        out_specs=[blk(SWA_ROWS, KV_WIDTH), blk(KV_WIDTH, WINDOW), blk(KV_WIDTH, WINDOW)],
        out_shape=[jax.ShapeDtypeStruct((nb, SWA_ROWS, KV_WIDTH), jnp.float32),
                   jax.ShapeDtypeStruct((nb, KV_WIDTH, WINDOW), jnp.float32),
                   jax.ShapeDtypeStruct((nb, KV_WIDTH, WINDOW), jnp.float32)],
        compiler_params=_params(("arbitrary",)),
        name="swa_sample",
    )(qbd, sink_col, pad8(kn), pad8(vn), to_t(cache_k), to_t(cache_v))
    o = o.reshape(nb, N_KV_HEADS, groups, DEC_SEQ, N_KV_HEADS, HEAD_DIM)
    attn = jnp.stack([o[:, h, :, :, h, :] for h in range(N_KV_HEADS)], axis=1)
    attn = attn.transpose(0, 3, 1, 2, 4).reshape(nb * DEC_SEQ, ATTN_WIDTH)
    from_t = lambda c: c.reshape(nb, N_KV_HEADS, HEAD_DIM, WINDOW).transpose(0, 3, 1, 2)
    return attn, from_t(okt), from_t(ovt)


def _mem_kv_kernel(mem_ref, wk_ref, wv_ref, wq_ref, wo_ref, mk_ref, mv_ref, wqk_ref, wvo_ref):
    mb = _bf16(mem_ref[...])
    mk = _dot(mb, wk_ref[...])
    mv = _dot(mb, wv_ref[...])
    mk_ref[...] = mk
    mv_ref[...] = mv
    mkb, mvb = _bf16(mk), _bf16(mv)
    for h in range(MEM_HEADS):
        sl = slice(h * MEM_HEAD_DIM, (h + 1) * MEM_HEAD_DIM)
        keys = slice(h * N_MEM, (h + 1) * N_MEM)
        wqk_ref[:, keys] = _bf16(_dot_nt(wq_ref[:, sl], mkb[:, sl]) * (MEM_HEAD_DIM ** -0.5))
        wvo_ref[keys, :] = _bf16(_dot(mvb[:, sl], wo_ref[sl, :]))


def _mem_kv(mem, wk_b, wv_b, wq_b, wo_b):
    full = lambda a: pl.BlockSpec(a.shape, lambda i: (0,) * a.ndim)
    blk = pl.BlockSpec((N_MEM, D_MODEL), lambda i: (0, 0))
    f32 = jax.ShapeDtypeStruct((N_MEM, D_MODEL), jnp.float32)
    fused = (D_MODEL, MEM_HEADS * N_MEM), (MEM_HEADS * N_MEM, D_MODEL)
    return pl.pallas_call(
        _mem_kv_kernel,
        grid=(1,),
        in_specs=[full(mem), full(wk_b), full(wv_b), full(wq_b), full(wo_b)],
        out_specs=[blk, blk] + [pl.BlockSpec(shp, lambda i: (0, 0)) for shp in fused],
        out_shape=[f32, f32] + [jax.ShapeDtypeStruct(shp, jnp.bfloat16) for shp in fused],
        compiler_params=_params(("arbitrary",)),
        name="mem_kv",
    )(mem, wk_b, wv_b, wq_b, wo_b)


def _mix_ln1(attn_ref, conv_ref, x_ref, wmix_ref, g1_ref, b1_ref):
    mix = _dot(_bf16(attn_ref[...]), wmix_ref[0:ATTN_WIDTH, :]) + _dot(conv_ref[...], wmix_ref[ATTN_WIDTH:, :])
    return _layer_norm(ALPHA * x_ref[...] + mix, g1_ref[...], b1_ref[...])


def _mem_q(h1, wq_ref):
    return _bf16(_dot(_bf16(h1), wq_ref[...]) * (MEM_HEAD_DIM ** -0.5))


def _route(h2, wrhl_ref, br_ref, tri_ref, carry):
    hi = _bf16(h2)
    lo = _bf16(h2 - hi.astype(jnp.float32))
    hh = _dot(hi, wrhl_ref[...])
    logits = hh[:, 0:LANES] + hh[:, LANES:] + _dot(lo, wrhl_ref[:, 0:LANES]) + br_ref[...]
    lane_i = lax.broadcasted_iota(jnp.int32, logits.shape, 1)
    lane = lane_i.astype(jnp.float32)
    big = jnp.float32(LANES)
    is_g = lane_i < N_GROUPS
    gl = jnp.where(is_g, logits, -jnp.inf)
    gmax = jnp.max(gl, axis=1, keepdims=True)
    gidx = jnp.min(jnp.where(is_g & (logits == gmax), lane, big), axis=1, keepdims=True)
    gsum = jnp.sum(jnp.exp(gl - gmax), axis=1, keepdims=True)
    gw = 1.0 / gsum
    eid = lane_i - N_GROUPS
    assert EXPERTS_PER_GROUP == 8
    grp = lax.shift_right_arithmetic(eid, jnp.full_like(eid, 3)).astype(jnp.float32)
    in_e = (lane_i >= N_GROUPS) & (lane_i < N_GROUPS + N_EXPERTS) & (grp == gidx)
    v1 = jnp.max(jnp.where(in_e, logits, -jnp.inf), axis=1, keepdims=True)
    i1 = jnp.min(jnp.where(in_e & (logits == v1), lane, big), axis=1, keepdims=True)
    rest = in_e & (lane != i1)
    v2 = jnp.max(jnp.where(rest, logits, -jnp.inf), axis=1, keepdims=True)
    i2 = jnp.min(jnp.where(rest & (logits == v2), lane, big), axis=1, keepdims=True)
    ex = jnp.exp(v2 - v1)
    den = 1.0 + ex
    w1 = gw / den
    w2 = gw * ex / den
    zero = jnp.zeros_like(logits)
    pick1 = lane == i1
    pick2 = lane == i2
    sel = jnp.where(pick1 | pick2, 1.0, 0.0)
    before = _dot(tri_ref[...], _bf16(sel)) + carry
    rank1 = jnp.sum(jnp.where(pick1, before, zero), axis=1, keepdims=True)
    rank2 = jnp.sum(jnp.where(pick2, before, zero), axis=1, keepdims=True)
    cols = (i1 - N_GROUPS, i2 - N_GROUPS, w1, w2, rank1, rank2)
    route = zero
    for k, col in enumerate(cols):
        route = jnp.where(lane_i == k, col, route)
    return route, carry + jnp.sum(sel, axis=0, keepdims=True)


def _post_tile(attn, conv_ref, x_ref, wmix_ref, g1_ref, b1_ref, wqk_ref, wvo_ref, g2_ref, b2_ref,
               wrhl_ref, br_ref, tri_ref, h2t_ref, rt_ref, cnt_ref, carry_ref):
    mix = _dot(attn, wmix_ref[0:ATTN_WIDTH, :]) + _dot(conv_ref[...], wmix_ref[ATTN_WIDTH:, :])
    h1 = _layer_norm(ALPHA * x_ref[...] + mix, g1_ref[...], b1_ref[...])
    scores = _dot(_bf16(h1), wqk_ref[...])
    probs = []
    for h in range(MEM_HEADS):
        s = scores[:, h * N_MEM:(h + 1) * N_MEM]
        p = jnp.exp(s - jnp.max(s, axis=1, keepdims=True))
        probs.append(_bf16(p / jnp.sum(p, axis=1, keepdims=True)))
    mem_out = _dot(jnp.concatenate(probs, axis=1), wvo_ref[...])
    h2 = _layer_norm(ALPHA * h1 + mem_out, g2_ref[...], b2_ref[...])
    _store_token_tiles(h2t_ref, h2)
    route, carry = _route(h2, wrhl_ref, br_ref, tri_ref, carry_ref[0:1, :])
    rt_ref[...] = route
    carry_ref[...] = jnp.broadcast_to(carry, carry_ref.shape)
    cnt_ref[...] = jnp.broadcast_to(carry, cnt_ref.shape)


def _swa_post_kernel(sinks_ref, q_ref, kc_ref, kp_ref, vc_ref, vp_ref,
                     conv_ref, x_ref, wmix_ref, g1_ref, b1_ref, wqk_ref, wvo_ref, g2_ref, b2_ref,
                     wrhl_ref, br_ref, tri_ref, h2s_ref, rts_ref, cnts_ref, wg_ref, wu_ref, wd_ref,
                     h2t_ref, rt_ref, cnt_ref, wgb_ref, wub_ref, wdb_ref, carry_ref, attn_s):
    t = pl.program_id(0)
    steps = N_PROMPT // TM_POST
    par = t % 2

    def swa(slot):
        def store(rows, cols, val):
            attn_s[slot, rows, cols] = val
        _swa_tile(t, sinks_ref, q_ref, kc_ref, kp_ref, vc_ref, vp_ref, store)

    def post(slot):
        _post_tile(attn_s[slot], conv_ref, x_ref, wmix_ref, g1_ref, b1_ref, wqk_ref, wvo_ref, g2_ref, b2_ref,
                   wrhl_ref, br_ref, tri_ref, h2t_ref, rt_ref, cnt_ref, carry_ref)

    def cast_expert():
        wgb_ref[...] = _bf16(wg_ref[...])
        wub_ref[...] = _bf16(wu_ref[...])
        wdb_ref[...] = _bf16(wd_ref[...])

    @pl.when(t == 0)
    def _():
        carry_ref[...] = cnts_ref[...]
        swa(0)
        cast_expert()

    @pl.when((t >= 1) & (t < steps))
    def _():
        swa(par)
        post(1 - par)
        cast_expert()

    @pl.when(t == steps)
    def _():
        post(1 - par)

    @pl.when(t == steps + 1)
    def _():
        h2t_ref[...] = h2s_ref[...]
        rt_ref[...] = rts_ref[...]


def _swa_post(sinks, q, kx, vx, conv, x, wmix_b, g1, b1, wqk, wvo, g2, b2, wrhl, br, tri, h2t_s, rt_s, cnt_s,
              w_gate, w_up, w_down):
    n = x.shape[0]
    tm = TM_POST
    assert tm == SWA_QB * WINDOW
    steps = n // tm
    assert N_EXPERTS <= steps
    expert = lambda a: pl.BlockSpec((1,) + a.shape[1:], lambda i: (jnp.minimum(i, N_EXPERTS - 1), 0, 0))
    experts = (w_gate, w_up, w_down)
    cur = lambda w: pl.BlockSpec((tm, w), lambda i: (jnp.minimum(i, steps - 1), 0))
    prev = lambda w: pl.BlockSpec((WINDOW, w), lambda i: (jnp.clip(SWA_QB * i - 1, 0, n // WINDOW - 1), 0))
    lag = lambda w: pl.BlockSpec((tm, w), lambda i: (jnp.clip(i - 1, 0, steps - 1), 0))
    full = lambda a: pl.BlockSpec(a.shape, lambda i: (0,) * a.ndim)
    weights = (wmix_b, g1, b1, wqk, wvo, g2, b2, wrhl, br, tri, h2t_s, rt_s, cnt_s)
    n_out = n + h2t_s.shape[0] // ROW_CHUNKS
    out_idx = lambda i: (jnp.where(i > steps, steps, jnp.clip(i - 1, 0, steps - 1)), 0)
    return pl.pallas_call(
        _swa_post_kernel,
        grid=(steps + 2,),
        in_specs=([pl.BlockSpec(memory_space=pltpu.SMEM), cur(ATTN_WIDTH),
                   cur(4 * LANES), prev(4 * LANES), cur(4 * LANES), prev(4 * LANES),
                   lag(CONV_CH), lag(D_MODEL)] + [full(a) for a in weights] + [expert(a) for a in experts]),
        out_specs=[pl.BlockSpec((tm * ROW_CHUNKS, LANES), out_idx),
                   pl.BlockSpec((tm, LANES), out_idx),
                   pl.BlockSpec((SUBLANES, LANES), lambda i: (0, 0))] + [expert(a) for a in experts],
        out_shape=[jax.ShapeDtypeStruct((n_out * ROW_CHUNKS, LANES), jnp.float32),
                   jax.ShapeDtypeStruct((n_out, LANES), jnp.float32),
                   jax.ShapeDtypeStruct((SUBLANES, LANES), jnp.float32)]
                  + [jax.ShapeDtypeStruct(a.shape, jnp.bfloat16) for a in experts],
        scratch_shapes=[pltpu.VMEM((SUBLANES, LANES), jnp.float32),
                        pltpu.VMEM((2, tm, ATTN_WIDTH), jnp.bfloat16)],
        compiler_params=_params(("arbitrary",)),
        name="swa_post_prompt",
    )(sinks, q, kx, kx, vx, vx, conv, x, *weights, *experts)


def _post_a_sample_kernel(attn_ref, conv_ref, x_ref, wmix_ref, g1_ref, b1_ref, wq_ref, h1_ref, qm_ref):
    h1 = _mix_ln1(attn_ref, conv_ref, x_ref, wmix_ref, g1_ref, b1_ref)
    h1_ref[...] = h1
    qm_ref[...] = _mem_q(h1, wq_ref).astype(jnp.float32)


def _post_a_sample(attn, conv, x, wmix_b, g1, b1, wq_b):
    n = x.shape[0]
    args = (attn, conv, x, wmix_b, g1, b1, wq_b)
    full = lambda a: pl.BlockSpec(a.shape, lambda i: (0,) * a.ndim)
    blk = pl.BlockSpec((n, D_MODEL), lambda i: (0, 0))
    return pl.pallas_call(
        _post_a_sample_kernel,
        grid=(1,),
        in_specs=[full(a) for a in args],
        out_specs=[blk, blk],
        out_shape=[jax.ShapeDtypeStruct((n, D_MODEL), jnp.float32),
                   jax.ShapeDtypeStruct((n, D_MODEL), jnp.float32)],
        compiler_params=_params(("arbitrary",)),
        name="post_a_sample",
    )(*args)


MEM_ROWS = MEM_HEADS * DEC_SEQ


def _mem_attn_sample_body(q_ref, mk_ref, mv_ref, b0, o_ref):
    nk = N_MEM * MEM_HEADS
    nb = SAMPLE_BB
    rows = nb * MEM_ROWS
    row_h = (lax.broadcasted_iota(jnp.int32, (rows, nk), 0) % MEM_ROWS) // DEC_SEQ
    key_h = lax.broadcasted_iota(jnp.int32, (rows, nk), 1) % MEM_HEADS
    s = jnp.concatenate(
        [_dot_nt(_bf16(q_ref[b]), _bf16(mk_ref[b0 + b].reshape(nk, MEM_HEAD_DIM))) for b in range(nb)], axis=0)
    s = jnp.where(row_h == key_h, s, -jnp.inf)
    p = jnp.exp(s - jnp.max(s, axis=1, keepdims=True))
    rden = 1.0 / jnp.sum(p, axis=1, keepdims=True)
    p = _bf16(p)
    for b in range(nb):
        r = slice(b * MEM_ROWS, (b + 1) * MEM_ROWS)
        o_ref[b] = _dot(p[r], _bf16(mv_ref[b0 + b].reshape(nk, MEM_HEAD_DIM))) * rden[r]


def _proj_mem_kernel(steps, x_ref, w_ref, tab_ref, cw_ref, mq_ref, mk_hbm, mv_hbm,
                     q_ref, kx_ref, vx_ref, conv_ref, ktail_ref, vtail_ref, utail_ref, mo_ref,
                     carry_ref, kbuf, vbuf, sem):
    i = pl.program_id(0)

    def fetch(step):
        slot = step % MEM_SLOTS
        src = pl.ds(step * SAMPLE_BB, SAMPLE_BB)
        dst = pl.ds(slot * SAMPLE_BB, SAMPLE_BB)
        return (pltpu.make_async_copy(mk_hbm.at[src], kbuf.at[dst], sem.at[0, slot]),
                pltpu.make_async_copy(mv_hbm.at[src], vbuf.at[dst], sem.at[1, slot]))

    @pl.when(i == 0)
    def _():
        for k in range(MEM_SLOTS - 1):
            for c in fetch(jnp.int32(k)):
                c.start()

    @pl.when(i + MEM_SLOTS - 1 < steps)
    def _():
        for c in fetch(i + MEM_SLOTS - 1):
            c.start()

    for c in fetch(i):
        c.wait()
    _mem_attn_sample_body(mq_ref, kbuf, vbuf, (i % MEM_SLOTS) * SAMPLE_BB, mo_ref)
    _proj_prompt_body(x_ref, w_ref, tab_ref, cw_ref,
                      q_ref, kx_ref, vx_ref, conv_ref, ktail_ref, vtail_ref, utail_ref, carry_ref)


def _proj_prompt_mem_sample(x, w_in_b, tab, conv_w, qm, mk, mv):
    n = x.shape[0]
    nb = mk.shape[0]
    bb = SAMPLE_BB
    steps = nb // bb
    tm = n // steps
    row = lambda w: pl.BlockSpec((tm, w), lambda i: (i, 0))
    full = lambda a: pl.BlockSpec(a.shape, lambda i: (0,) * a.ndim)
    const = lambda r, w: pl.BlockSpec((r, w), lambda i: (0, 0))
    mq = qm.reshape(nb, DEC_SEQ, MEM_HEADS, MEM_HEAD_DIM).transpose(0, 2, 1, 3).reshape(nb, MEM_ROWS, MEM_HEAD_DIM)
    mrows = pl.BlockSpec((bb, MEM_ROWS, MEM_HEAD_DIM), lambda i: (i, 0, 0))
    kv = pl.BlockSpec(memory_space=pl.ANY)
    assert steps >= MEM_SLOTS - 1
    ring = pltpu.VMEM((MEM_SLOTS * bb, N_MEM, MEM_HEADS, MEM_HEAD_DIM), jnp.float32)
    outs = pl.pallas_call(
        functools.partial(_proj_mem_kernel, steps),
        grid=(steps,),
        in_specs=[row(D_MODEL), full(w_in_b), pl.BlockSpec((ROPE_ROWS, tm), lambda i: (0, i)), full(conv_w),
                  mrows, kv, kv],
        out_specs=[row(ATTN_WIDTH), row(4 * LANES), row(4 * LANES), row(CONV_CH),
                   const(WINDOW, KV_WIDTH), const(WINDOW, KV_WIDTH), const(SUBLANES, CONV_CH), mrows],
        out_shape=[jax.ShapeDtypeStruct((n, ATTN_WIDTH), jnp.bfloat16),
                   jax.ShapeDtypeStruct((n, 4 * LANES), jnp.bfloat16),
                   jax.ShapeDtypeStruct((n, 4 * LANES), jnp.bfloat16),
                   jax.ShapeDtypeStruct((n, CONV_CH), jnp.bfloat16),
                   jax.ShapeDtypeStruct((WINDOW, KV_WIDTH), jnp.float32),
                   jax.ShapeDtypeStruct((WINDOW, KV_WIDTH), jnp.float32),
                   jax.ShapeDtypeStruct((SUBLANES, CONV_CH), jnp.float32),
                   jax.ShapeDtypeStruct((nb, MEM_ROWS, MEM_HEAD_DIM), jnp.float32)],
        scratch_shapes=[pltpu.VMEM((SUBLANES, CONV_CH), jnp.float32), ring, ring,
                        pltpu.SemaphoreType.DMA((2, MEM_SLOTS))],
        compiler_params=_params(("arbitrary",)),
        name="proj_prompt_mem_sample",
    )(x, w_in_b, tab, conv_w, mq, mk, mv)
    o = outs[7].reshape(nb, MEM_HEADS, DEC_SEQ, MEM_HEAD_DIM).transpose(0, 2, 1, 3).reshape(nb * DEC_SEQ, D_MODEL)
    return outs[:7], o


def _post_b_sample_kernel(o_ref, h1_ref, wo_ref, g2_ref, b2_ref, wrhl_ref, br_ref, tri_ref,
                          h2t_ref, rt_ref, cnt_ref):
    h2 = _layer_norm(ALPHA * h1_ref[...] + _dot(_bf16(o_ref[...]), wo_ref[...]), g2_ref[...], b2_ref[...])
    _store_token_tiles(h2t_ref, h2)
    route, carry = _route(h2, wrhl_ref, br_ref, tri_ref, jnp.zeros((1, LANES), jnp.float32))
    rt_ref[...] = route
    cnt_ref[...] = jnp.broadcast_to(carry, cnt_ref.shape)


def _post_b_sample(o, h1, wo_b, g2, b2, wrhl, br, tri):
    n = h1.shape[0]
    args = (o, h1, wo_b, g2, b2, wrhl, br, tri)
    full = lambda a: pl.BlockSpec(a.shape, lambda i: (0,) * a.ndim)
    return pl.pallas_call(
        _post_b_sample_kernel,
        grid=(1,),
        in_specs=[full(a) for a in args],
        out_specs=[pl.BlockSpec((n * ROW_CHUNKS, LANES), lambda i: (0, 0)),
                   pl.BlockSpec((n, LANES), lambda i: (0, 0)),
                   pl.BlockSpec((SUBLANES, LANES), lambda i: (0, 0))],
        out_shape=[jax.ShapeDtypeStruct((n * ROW_CHUNKS, LANES), jnp.float32),
                   jax.ShapeDtypeStruct((n, LANES), jnp.float32),
                   jax.ShapeDtypeStruct((SUBLANES, LANES), jnp.float32)],
        compiler_params=_params(("arbitrary",)),
        name="post_b_sample",
    )(*args)


def _row_gather_copy(src_hbm, idx, dst, dst_row, sem):
    s0 = pl.multiple_of(idx * ROW_CHUNKS, ROW_CHUNKS)
    d0 = pl.multiple_of(dst_row * ROW_CHUNKS, ROW_CHUNKS)
    return pltpu.make_async_copy(src_hbm.at[pl.ds(s0, ROW_CHUNKS), :], dst.at[pl.ds(d0, ROW_CHUNKS), :], sem)


def _dispatch_kernel(pos_ref, h2t_ref, xs_hbm, sem):
    def body(r, c):
        src = h2t_ref.at[pl.ds(pl.multiple_of(r * ROW_CHUNKS, ROW_CHUNKS), ROW_CHUNKS), :]
        for k in range(2):
            d0 = pl.multiple_of(pos_ref[0, 0, k * TM_COMB + r] * ROW_CHUNKS, ROW_CHUNKS)
            pltpu.make_async_copy(src, xs_hbm.at[pl.ds(d0, ROW_CHUNKS), :], sem.at[0]).start(priority=k)
        return c
    lax.fori_loop(0, TM_COMB, body, 0, unroll=8)
    for _ in range(2):
        pltpu.make_async_copy(h2t_ref, xs_hbm.at[pl.ds(0, TM_COMB * ROW_CHUNKS), :], sem.at[0]).wait()


def _dispatch(pos3, h2t):
    nt = N_ALL // TM_COMB
    return pl.pallas_call(
        _dispatch_kernel,
        grid=(nt,),
        in_specs=[pl.BlockSpec((1, 1, 2 * TM_COMB), lambda i: (i, 0, 0), memory_space=pltpu.SMEM),
                  pl.BlockSpec((TM_COMB * ROW_CHUNKS, LANES), lambda i: (i, 0))],
        out_specs=pl.BlockSpec(memory_space=pl.ANY),
        out_shape=jax.ShapeDtypeStruct((N_ASSIGN * ROW_CHUNKS, LANES), jnp.float32),
        scratch_shapes=[pltpu.SemaphoreType.DMA((1,))],
        compiler_params=_params(("arbitrary",)),
        name="moe_dispatch",
    )(pos3, h2t)


def _moe_ffn_kernel(it_ref, ie_ref, lo_ref, hi_ref, x_hbm, wg_hbm, wu_hbm, wd_hbm, y_ref,
                    xbuf, wgb, wub, wdb, sem, wsem, run_ref):
    i = pl.program_id(0)
    lo = lo_ref[i]
    hi = hi_ref[i]
    t = it_ref[i]
    tile_rows = TM_MOE * ROW_CHUNKS

    def fetch(tile):
        src = pl.multiple_of(tile * tile_rows, tile_rows)
        dst = pl.multiple_of((tile % MOE_X_SLOTS) * tile_rows, tile_rows)
        return pltpu.make_async_copy(x_hbm.at[pl.ds(src, tile_rows), :], xbuf.at[pl.ds(dst, tile_rows), :],
                                     sem.at[tile % MOE_X_SLOTS])

    @pl.when(i == 0)
    def _():
        for k in range(MOE_X_SLOTS - 1):
            fetch(jnp.int32(k)).start()

    @pl.when((i == 0) | (t != it_ref[jnp.maximum(i - 1, 0)]))
    def _():
        @pl.when(t + MOE_X_SLOTS - 1 < MOE_TILES)
        def _():
            fetch(t + MOE_X_SLOTS - 1).start()
        fetch(t).wait()

    base = pl.multiple_of((t % MOE_X_SLOTS) * tile_rows, tile_rows)

    e = ie_ref[i]

    def wfetch(expert, slot):
        return [pltpu.make_async_copy(src.at[expert], dst.at[slot], wsem.at[slot])
                for src, dst in ((wg_hbm, wgb), (wu_hbm, wub), (wd_hbm, wdb))]

    @pl.when(i == 0)
    def _():
        run_ref[0] = 0
        for c in wfetch(e, 0):
            c.start()

    @pl.when((i == 0) | (e != ie_ref[jnp.maximum(i - 1, 0)]))
    def _():
        @pl.when(i > 0)
        def _():
            run_ref[0] = run_ref[0] + 1
        slot = run_ref[0] % 2
        item = lambda j: ie_ref[jnp.minimum(j, MOE_ITEMS - 1)]
        nxt = lax.while_loop(lambda j: (j < MOE_ITEMS) & (item(j) == e), lambda j: j + 1, i + 1)

        @pl.when(nxt < MOE_ITEMS)
        def _():
            for c in wfetch(item(nxt), 1 - slot):
                c.start()
        for c in wfetch(e, slot):
            c.wait()

    wslot = run_ref[0] % 2

    def ffn(r0, rows):
        x = _bf16(_load_token_tiles(xbuf, base + r0 * ROW_CHUNKS, rows))
        hg = _dot(x, wgb[wslot])
        hu = _dot(x, wub[wslot])
        h = hg / (1.0 + jnp.exp(-hg)) * hu
        return _dot(_bf16(h), wdb[wslot])

    whole = (lo == 0) & (hi == TM_MOE)

    @pl.when(whole)
    def _():
        _store_token_tiles(y_ref, ffn(0, TM_MOE))

    for r0 in range(0, TM_MOE, MOE_BLOCK):
        live = (hi > lo) & jnp.logical_not(whole) & (hi > r0) & (lo < r0 + MOE_BLOCK)

        def store(merge, r0=r0):
            y = ffn(r0, MOE_BLOCK)
            row = r0 + lax.broadcasted_iota(jnp.int32, (MOE_BLOCK, LANES), 0)
            mask = (row >= lo) & (row < hi)
            for c in range(ROW_CHUNKS):
                sl = pl.ds(r0 * ROW_CHUNKS + c, MOE_BLOCK, stride=ROW_CHUNKS)
                y_ref[sl, :] = jnp.where(mask, y[:, c * LANES:(c + 1) * LANES], y_ref[sl, :] if merge else 0.0)

        pl.when(live & (lo <= r0))(functools.partial(store, False))
        pl.when(live & (lo > r0))(functools.partial(store, True))


def _moe_ffn(item_tile, item_expert, item_lo, item_hi, x_sorted, w_gate, w_up, w_down):
    tile = pl.BlockSpec((TM_MOE * ROW_CHUNKS, LANES), lambda i, it, ie, lo, hi: (it[i], 0))
    assert MOE_TILES >= MOE_X_SLOTS - 1
    grid_spec = pltpu.PrefetchScalarGridSpec(
        num_scalar_prefetch=4,
        grid=(MOE_ITEMS,),
        in_specs=[pl.BlockSpec(memory_space=pl.ANY)] * 4,
        out_specs=tile,
        scratch_shapes=[pltpu.VMEM((MOE_X_SLOTS * TM_MOE * ROW_CHUNKS, LANES), jnp.float32),
                        pltpu.VMEM((2, D_MODEL, EXPERT_FF), jnp.bfloat16),
                        pltpu.VMEM((2, D_MODEL, EXPERT_FF), jnp.bfloat16),
                        pltpu.VMEM((2, EXPERT_FF, D_MODEL), jnp.bfloat16),
                        pltpu.SemaphoreType.DMA((MOE_X_SLOTS,)),
                        pltpu.SemaphoreType.DMA((2,)),
                        pltpu.SMEM((1,), jnp.int32)],
    )
    return pl.pallas_call(
        _moe_ffn_kernel,
        grid_spec=grid_spec,
        out_shape=jax.ShapeDtypeStruct((N_ASSIGN * ROW_CHUNKS, LANES), jnp.float32),
        compiler_params=_params(("arbitrary",)),
        name="moe_ffn",
    )(item_tile, item_expert, item_lo, item_hi, x_sorted, w_gate, w_up, w_down)


def _combine_kernel(nt, pos_cur_ref, pos_nxt_ref, yt_hbm, h2t_ref, rt_ref, g3_ref, b3_ref, o_ref, abuf, sem):
    t = pl.program_id(0)
    slot = t % 2
    rows = 2 * TM_COMB

    def issue(pos_ref, s):
        def body(j, c):
            for k in range(2):
                r = 2 * j + k
                _row_gather_copy(yt_hbm, pos_ref[0, 0, r], abuf, s * rows + r, sem.at[s]).start(priority=k)
            return c
        lax.fori_loop(0, rows // 2, body, 0, unroll=16)

    @pl.when(t == 0)
    def _():
        issue(pos_cur_ref, 0)

    @pl.when(t + 1 < nt)
    def _():
        issue(pos_nxt_ref, 1 - slot)

    base = pl.multiple_of(slot * (rows * ROW_CHUNKS), rows * ROW_CHUNKS)
    pltpu.make_async_copy(yt_hbm.at[pl.ds(0, rows * ROW_CHUNKS), :],
                          abuf.at[pl.ds(base, rows * ROW_CHUNKS), :], sem.at[slot]).wait()
    ya = _load_token_tiles(abuf, base, TM_COMB)
    yb = _load_token_tiles(abuf, base + TM_COMB * ROW_CHUNKS, TM_COMB)
    rt = rt_ref[...]
    ff = rt[:, 2:3] * ya + rt[:, 3:4] * yb
    h2 = _load_token_tiles(h2t_ref, 0, TM_COMB)
    o_ref[...] = _layer_norm(ALPHA * h2 + ff, g3_ref[...], b3_ref[...])


def _combine(pos3, yt, h2t, rt, g3, b3, tile0, n_tiles):
    last = tile0 + n_tiles - 1
    smem_pos = lambda f: pl.BlockSpec((1, 1, 2 * TM_COMB), f, memory_space=pltpu.SMEM)
    full = lambda a: pl.BlockSpec(a.shape, lambda i: (0,) * a.ndim)
    return pl.pallas_call(
        functools.partial(_combine_kernel, n_tiles),
        grid=(n_tiles,),
        in_specs=[smem_pos(lambda i: (tile0 + i, 0, 0)),
                  smem_pos(lambda i: (jnp.minimum(tile0 + i + 1, last), 0, 0)),
                  pl.BlockSpec(memory_space=pl.ANY),
                  pl.BlockSpec((TM_COMB * ROW_CHUNKS, LANES), lambda i: (tile0 + i, 0)),
                  pl.BlockSpec((TM_COMB, LANES), lambda i: (tile0 + i, 0)),
                  full(g3), full(b3)],
        out_specs=pl.BlockSpec((TM_COMB, D_MODEL), lambda i: (i, 0)),
        out_shape=jax.ShapeDtypeStruct((n_tiles * TM_COMB, D_MODEL), jnp.float32),
        scratch_shapes=[pltpu.VMEM((2 * 2 * TM_COMB * ROW_CHUNKS, LANES), jnp.float32),
                        pltpu.SemaphoreType.DMA((2,))],
        compiler_params=_params(("arbitrary",)),
        name="moe_combine",
    )(pos3, pos3, yt, h2t, rt, g3, b3)


POS_TILES = 11


def _positions_kernel(rt_ref, starts_ref, pos_ref):
    expert = lax.broadcasted_iota(jnp.int32, (N_EXPERTS, TM_COMB), 0).astype(jnp.float32)
    starts = jnp.concatenate([starts_ref[...]] * (TM_COMB // LANES), axis=1)
    for j in range(POS_TILES):
        cols = rt_ref[j * TM_COMB:(j + 1) * TM_COMB, :].T
        out = []
        for k in range(2):
            seg = jnp.sum(jnp.where(expert == cols[k:k + 1, :], starts, 0.0), axis=0, keepdims=True)
            out.append(seg + cols[4 + k:5 + k, :])
        pos_ref[j] = jnp.concatenate(out, axis=1).astype(jnp.int32)


def _positions(rt, starts_rep):
    nt = N_ALL // TM_COMB
    assert nt % POS_TILES == 0
    return pl.pallas_call(
        _positions_kernel,
        grid=(nt // POS_TILES,),
        in_specs=[pl.BlockSpec((POS_TILES * TM_COMB, LANES), lambda i: (i, 0)),
                  pl.BlockSpec((N_EXPERTS, LANES), lambda i: (0, 0))],
        out_specs=pl.BlockSpec((POS_TILES, 1, 2 * TM_COMB), lambda i: (i, 0, 0)),
        out_shape=jax.ShapeDtypeStruct((nt, 1, 2 * TM_COMB), jnp.int32),
        compiler_params=_params(("arbitrary",)),
        name="moe_positions",
    )(rt, starts_rep)


def _routing_plan(rt, cnt):
    i32 = jnp.int32
    counts_f = cnt[0, N_GROUPS:N_GROUPS + N_EXPERTS]
    starts_f = jnp.cumsum(counts_f) - counts_f
    pos3 = _positions(rt, jnp.broadcast_to(starts_f[:, None], (N_EXPERTS, LANES)))
    starts = starts_f.astype(i32)
    tiles = jnp.arange(MOE_TILES, dtype=i32) * TM_MOE
    rank_t = jnp.arange(MOE_TILES, dtype=i32) + jnp.sum((starts[None, :] < tiles[:, None]).astype(i32), axis=1)
    rank_s = jnp.arange(N_EXPERTS, dtype=i32) + jnp.sum((tiles[None, :] <= starts[:, None]).astype(i32), axis=1)
    vals = jnp.concatenate([tiles, starts])
    ranks = jnp.concatenate([rank_t, rank_s])
    slot = jnp.arange(MOE_ITEMS, dtype=i32)
    lo = jnp.sum(jnp.where(ranks[None, :] == slot[:, None], vals[None, :], 0), axis=1)
    hi = jnp.concatenate([lo[1:], jnp.full((1,), N_ASSIGN, i32)])
    item_tile = jnp.minimum(lo // TM_MOE, MOE_TILES - 1)
    item_expert = jnp.clip(jnp.sum((starts[None, :] <= lo[:, None]).astype(i32), axis=1) - 1, 0, N_EXPERTS - 1)
    base = item_tile * TM_MOE
    return item_tile, item_expert, lo - base, hi - base, pos3


def kernel(x_prompt, x_sample, mem_prompt, cache_swa_k, cache_swa_v, cache_conv, cache_mem_k, cache_mem_v,
           w_in, sinks, conv_w, w_mix_out, ln1_g, ln1_b, w_q_mem, w_k_mem, w_v_mem, w_o_mem, ln2_g, ln2_b,
           w_router_group, b_router_group, w_router_expert, b_router_expert, w_gate, w_up, w_down,
           ln3_g, ln3_b):
    f32 = jnp.float32
    row = lambda a: a.reshape(1, -1).astype(f32)
    w_in_b, wmix_b, wq_b, wk_b, wv_b, wo_b = (_bf16(w) for w in (w_in, w_mix_out, w_q_mem, w_k_mem, w_v_mem, w_o_mem))
    g1, b1, g2, b2, g3, b3 = (row(a) for a in (ln1_g, ln1_b, ln2_g, ln2_b, ln3_g, ln3_b))
    pad = LANES - N_GROUPS - N_EXPERTS
    wr = jnp.concatenate([w_router_group, w_router_expert, jnp.zeros((D_MODEL, pad), f32)], axis=1)
    wrh = _bf16(wr)
    wrhl = jnp.concatenate([wrh, _bf16(wr - wrh.astype(f32))], axis=1)
    br = jnp.concatenate([b_router_group, b_router_expert, jnp.zeros((pad,), f32)]).reshape(1, LANES)

    xs = x_sample.reshape(N_SAMPLE, D_MODEL)
    tab_s = jnp.tile(_rope_table(PAST_LEN + jnp.arange(DEC_SEQ)), (1, DEC_BATCH))
    c0 = jnp.repeat(cache_conv[:, 0], DEC_SEQ, axis=0)
    c1 = jnp.repeat(cache_conv[:, 1], DEC_SEQ, axis=0)
    q_s, k_s, v_s, conv_s, u_s = _proj_sample(xs, w_in_b, tab_s, conv_w, c0, c1)
    attn_s, swa_k_s, swa_v_s = _swa_sample(sinks, q_s, k_s, v_s, cache_swa_k, cache_swa_v)
    h1_s, qm_s = _post_a_sample(attn_s, conv_s, xs, wmix_b, g1, b1, wq_b)
    xp = x_prompt.reshape(N_PROMPT, D_MODEL)
    tab_p = _rope_table(jnp.arange(N_PROMPT))
    (q_p, kx_p, vx_p, conv_p, k_tail, v_tail, u_tail), o_s = _proj_prompt_mem_sample(
        xp, w_in_b, tab_p, conv_w, qm_s, cache_mem_k, cache_mem_v)
    tri = _bf16(jnp.tril(jnp.ones((TM_POST, TM_POST), f32), -1))
    h2t_s, rt_s, cnt_s = _post_b_sample(o_s, h1_s, wo_b, g2, b2, wrhl, br, tri)

    mk, mv, wqk, wvo = _mem_kv(mem_prompt.reshape(N_MEM, D_MODEL), wk_b, wv_b, wq_b, wo_b)
    h2t, rt, cnt, wg_b, wu_b, wd_b = _swa_post(sinks, q_p, kx_p, vx_p, conv_p, xp, wmix_b, g1, b1, wqk, wvo, g2, b2,
                                               wrhl, br, tri, h2t_s, rt_s, cnt_s, w_gate, w_up, w_down)

    item_tile, item_expert, item_lo, item_hi, pos3 = _routing_plan(rt, cnt)
    x_sorted = _dispatch(pos3, h2t)
    yt = _moe_ffn(item_tile, item_expert, item_lo, item_hi, x_sorted, wg_b, wu_b, wd_b)
    y_p = _combine(pos3, yt, h2t, rt, g3, b3, 0, N_PROMPT // TM_COMB)
    y_s = _combine(pos3, yt, h2t, rt, g3, b3, N_PROMPT // TM_COMB, N_SAMPLE // TM_COMB)

    return (y_p.reshape(1, SEQ, D_MODEL),
            y_s.reshape(DEC_BATCH, DEC_SEQ, D_MODEL),
            k_tail.reshape(1, WINDOW, N_KV_HEADS, HEAD_DIM),
            v_tail.reshape(1, WINDOW, N_KV_HEADS, HEAD_DIM),
            u_tail[SUBLANES - (CONV_K - 1):].reshape(1, CONV_K - 1, CONV_CH),
            mk.reshape(1, N_MEM, MEM_HEADS, MEM_HEAD_DIM),
            mv.reshape(1, N_MEM, MEM_HEADS, MEM_HEAD_DIM),
            swa_k_s.reshape(DEC_BATCH, WINDOW, N_KV_HEADS, HEAD_DIM),
            swa_v_s.reshape(DEC_BATCH, WINDOW, N_KV_HEADS, HEAD_DIM),
            u_s.reshape(DEC_BATCH, DEC_SEQ, CONV_CH)[:, DEC_SEQ - (CONV_K - 1):])
```

```python
import functools

import jax
import jax.numpy as jnp
from jax import lax
from jax.experimental import pallas as pl
from jax.experimental.pallas import tpu as pltpu

D_MODEL = 1024
SEQ = 16384
DEC_BATCH = 128
DEC_SEQ = 4
PAST_LEN = 16384
ATTN_WIDTH = 512
CONV_CH = 512
HEAD_DIM = 64
N_HEADS = 8
N_KV_HEADS = 2
KV_WIDTH = 128
WINDOW = 128
ROPE_THETA = 500000.0
ROPE_DIM = 16
CONV_K = 3
Q_END = ATTN_WIDTH
K_END = Q_END + KV_WIDTH
V_END = K_END + KV_WIDTH
B_END = V_END + CONV_CH
C_END = B_END + CONV_CH
IN_WIDTH = C_END + CONV_CH
N_MEM = 256
MEM_HEADS = 4
MEM_HEAD_DIM = 256
N_GROUPS = 4
EXPERTS_PER_GROUP = 8
N_EXPERTS = 32
EXPERT_FF = 256
ALPHA = 2.0 ** 0.25
LN_EPS = 1e-5

LANES = 128
SUBLANES = 8
ROW_CHUNKS = D_MODEL // LANES
VMEM_LIMIT = 56 * 1024 * 1024

N_PROMPT = SEQ
N_SAMPLE = DEC_BATCH * DEC_SEQ
N_ALL = N_PROMPT + N_SAMPLE
TM_POST = 512
TM_MOE = 512
TM_COMB = 512
MOE_BLOCK = 256
MEM_SLOTS = 3
MOE_X_SLOTS = 3
N_ASSIGN = 2 * N_ALL
MOE_TILES = N_ASSIGN // TM_MOE
MOE_ITEMS = MOE_TILES + N_EXPERTS
SAMPLE_BB = 4
SWA_BB = 16

assert ROW_CHUNKS == SUBLANES
assert N_SAMPLE == TM_POST
assert N_ASSIGN % TM_MOE == 0 and N_ALL % TM_COMB == 0


def _params(sem, vmem=VMEM_LIMIT):
    return pltpu.CompilerParams(dimension_semantics=sem, vmem_limit_bytes=vmem)


def _bf16(x):
    return x.astype(jnp.bfloat16)


def _dot(a, b):
    return jnp.dot(a, b, preferred_element_type=jnp.float32)


def _dot_nt(a, b):
    return lax.dot_general(a, b, (((1,), (1,)), ((), ())), preferred_element_type=jnp.float32)


def _layer_norm(x, g, b):
    mu = jnp.mean(x, axis=-1, keepdims=True)
    xc = x - mu
    var = jnp.mean(xc * xc, axis=-1, keepdims=True)
    return xc * lax.rsqrt(var + LN_EPS) * g + b


def _rope(x, cos_t, sin_t):
    lane = lax.broadcasted_iota(jnp.int32, x.shape, 1) % HEAD_DIM
    half = ROPE_DIM // 2
    partner = jnp.where(lane < half, pltpu.roll(x, LANES - half, axis=1), pltpu.roll(x, half, axis=1))
    return x * cos_t + partner * sin_t


def _head_slabs(x):
    lane = lax.broadcasted_iota(jnp.int32, x.shape, 1)
    lo = lane < HEAD_DIM
    sw = pltpu.roll(x, HEAD_DIM, axis=1)
    zero = jnp.zeros_like(x)
    slabs = [jnp.where(lo, x, zero), jnp.where(lo, zero, sw), jnp.where(lo, sw, zero), jnp.where(lo, zero, x)]
    return _bf16(jnp.concatenate(slabs, axis=1))


def _store_token_tiles(ref, val):
    rows = val.shape[0]
    for c in range(ROW_CHUNKS):
        ref[pl.ds(c, rows, stride=ROW_CHUNKS), :] = val[:, c * LANES:(c + 1) * LANES]


def _load_token_tiles(ref, base, rows):
    return jnp.concatenate(
        [ref[pl.ds(base + c, rows, stride=ROW_CHUNKS), :] for c in range(ROW_CHUNKS)], axis=1)


ROPE_ONE = 3 * (ROPE_DIM // 2)
ROPE_ROWS = 32


def _rope_patterns(tab):
    half = ROPE_DIM // 2
    m = lax.broadcasted_iota(jnp.int32, tab.shape, 1) % HEAD_DIM
    idx_c = jnp.where(m < ROPE_DIM, m % half, ROPE_ONE)
    idx_s = jnp.where(m < half, 2 * half + m, jnp.where(m < ROPE_DIM, m, ROPE_ONE + 1))
    return jnp.take_along_axis(tab, idx_c, axis=1), jnp.take_along_axis(tab, idx_s, axis=1)


def _proj_common(x_ref, w_ref, tab_ref):
    xb = _bf16(x_ref[...])
    tab = tab_ref[...]
    pad = jnp.zeros((LANES - tab.shape[0], tab.shape[1]), jnp.float32)
    cos_t, sin_t = _rope_patterns(jnp.concatenate([tab, pad], axis=0).T)
    q = _dot(xb, w_ref[:, 0:Q_END])
    q_rot = jnp.concatenate(
        [_rope(q[:, p * LANES:(p + 1) * LANES], cos_t, sin_t) for p in range(ATTN_WIDTH // LANES)], axis=1)
    q_out = _bf16(q_rot * (HEAD_DIM ** -0.5))
    kv = _dot(xb, w_ref[:, Q_END:V_END])
    k = _rope(kv[:, 0:KV_WIDTH], cos_t, sin_t)
    v = kv[:, KV_WIDTH:]
    bg = _dot(xb, w_ref[:, V_END:B_END])
    u = _dot(xb, w_ref[:, B_END:C_END]) * _dot(xb, w_ref[:, C_END:IN_WIDTH])
    return q_out, k, v, bg, u


def _conv3(bg, u, u1, u2, cw_ref):
    cw = cw_ref[...]
    return bg * (cw[0:1, :] * u2 + cw[1:2, :] * u1 + cw[2:3, :] * u)


def _proj_prompt_body(x_ref, w_ref, tab_ref, cw_ref,
                      q_ref, kx_ref, vx_ref, conv_ref, ktail_ref, vtail_ref, utail_ref, carry_ref):
    @pl.when(pl.program_id(0) == 0)
    def _():
        carry_ref[...] = jnp.zeros_like(carry_ref)

    q_out, k, v, bg, u = _proj_common(x_ref, w_ref, tab_ref)
    tm = u.shape[0]
    ext = jnp.concatenate([carry_ref[...], u], axis=0)
    u1 = pltpu.roll(ext, 1, axis=0)[SUBLANES:SUBLANES + tm]
    u2 = pltpu.roll(ext, 2, axis=0)[SUBLANES:SUBLANES + tm]
    q_ref[...] = q_out
    kx_ref[...] = _head_slabs(k)
    vx_ref[...] = _head_slabs(v)
    conv_ref[...] = _bf16(_conv3(bg, u, u1, u2, cw_ref))
    ktail_ref[...] = k[tm - WINDOW:tm]
    vtail_ref[...] = v[tm - WINDOW:tm]
    utail_ref[...] = u[tm - SUBLANES:tm]
    carry_ref[...] = u[tm - SUBLANES:tm]


def _proj_sample_kernel(x_ref, w_ref, tab_ref, cw_ref, c0_ref, c1_ref,
                        q_ref, k_ref, v_ref, conv_ref, u_ref):
    q_out, k, v, bg, u = _proj_common(x_ref, w_ref, tab_ref)
    t = lax.broadcasted_iota(jnp.int32, u.shape, 0) % DEC_SEQ
    c0 = c0_ref[...]
    c1 = c1_ref[...]
    u1 = jnp.where(t >= 1, pltpu.roll(u, 1, axis=0), c1)
    u2 = jnp.where(t >= 2, pltpu.roll(u, 2, axis=0), jnp.where(t == 1, c1, c0))
    q_ref[...] = q_out.astype(jnp.float32)
    k_ref[...] = k
    v_ref[...] = v
    conv_ref[...] = _bf16(_conv3(bg, u, u1, u2, cw_ref))
    u_ref[...] = u


def _rope_table(pos):
    half = ROPE_DIM // 2
    inv = ROPE_THETA ** (-jnp.arange(0, ROPE_DIM, 2, dtype=jnp.float32) / ROPE_DIM)
    ang = pos.astype(jnp.float32)[None, :] * inv[:, None]
    cos, sin = jnp.cos(ang), jnp.sin(ang)
    n = pos.shape[0]
    assert ROPE_ONE == 3 * half
    return jnp.concatenate([cos, sin, -sin, jnp.ones((1, n), jnp.float32),
                            jnp.zeros((ROPE_ROWS - ROPE_ONE - 1, n), jnp.float32)], axis=0)


def _proj_sample(x, w_in_b, tab, conv_w, c0, c1):
    n = x.shape[0]
    full = lambda a: pl.BlockSpec(a.shape, lambda i: (0,) * a.ndim)
    out = lambda w, dt: jax.ShapeDtypeStruct((n, w), dt)
    blk = lambda w: pl.BlockSpec((n, w), lambda i: (0, 0))
    return pl.pallas_call(
        _proj_sample_kernel,
        grid=(1,),
        in_specs=[full(x), full(w_in_b), full(tab), full(conv_w), full(c0), full(c1)],
        out_specs=[blk(ATTN_WIDTH), blk(KV_WIDTH), blk(KV_WIDTH), blk(CONV_CH), blk(CONV_CH)],
        out_shape=[out(ATTN_WIDTH, jnp.float32), out(KV_WIDTH, jnp.float32), out(KV_WIDTH, jnp.float32),
                   out(CONV_CH, jnp.bfloat16), out(CONV_CH, jnp.float32)],
        compiler_params=_params(("arbitrary",)),
        name="proj_sample",
    )(x, w_in_b, tab, conv_w, c0, c1)


def _sink_softmax_pv(s, valid, sink, vx):
    s = jnp.where(valid, s, -jnp.inf)
    m = jnp.maximum(jnp.max(s, axis=1, keepdims=True), sink)
    p = jnp.exp(s - m)
    den = jnp.sum(p, axis=1, keepdims=True) + jnp.exp(sink - m)
    return _dot(_bf16(p), vx) / den


SWA_QB = 4


def _swa_tile(step, sinks_ref, q_ref, kc_ref, kp_ref, vc_ref, vp_ref, store):
    kall = jnp.concatenate([kp_ref[...], kc_ref[...]], axis=0)
    vall = jnp.concatenate([vp_ref[...], vc_ref[...]], axis=0)
    i = lax.broadcasted_iota(jnp.int32, (WINDOW, 2 * WINDOW), 0)
    j = lax.broadcasted_iota(jnp.int32, (WINDOW, 2 * WINDOW), 1)
    band = (j > i) & (j <= i + WINDOW)
    for sb in range(SWA_QB):
        rows = slice(sb * WINDOW, (sb + 1) * WINDOW)
        kcat = kall[sb * WINDOW:(sb + 2) * WINDOW]
        vcat = vall[sb * WINDOW:(sb + 2) * WINDOW]
        valid = band & ((step > 0) | (j >= WINDOW)) if sb == 0 else band
        for p in range(N_HEADS // 2):
            qs = q_ref[rows, p * LANES:(p + 1) * LANES]
            acc = None
            for e in range(2):
                hd = 2 * p + e
                slab = 2 * (hd // (N_HEADS // N_KV_HEADS)) + e
                kx = kcat[:, slab * LANES:(slab + 1) * LANES]
                vx = vcat[:, slab * LANES:(slab + 1) * LANES]
                o = _sink_softmax_pv(_dot_nt(qs, kx), valid, sinks_ref[hd], vx)
                acc = o if acc is None else acc + o
            store(rows, slice(p * LANES, (p + 1) * LANES), _bf16(acc))


SWA_ROWS = N_HEADS * DEC_SEQ
NEW_ROWS = 2 * SUBLANES


def _swa_sample_kernel(q_ref, sink_ref, kn_ref, vn_ref, kt_ref, vt_ref, o_ref, okt_ref, ovt_ref):
    nb = SWA_BB
    rows = nb * SWA_ROWS
    t = lax.broadcasted_iota(jnp.int32, (rows, WINDOW), 0) % DEC_SEQ
    valid_c = lax.broadcasted_iota(jnp.int32, (rows, WINDOW), 1) > t
    valid_n = (lax.broadcasted_iota(jnp.int32, (rows, NEW_ROWS), 1)
               <= lax.broadcasted_iota(jnp.int32, (rows, NEW_ROWS), 0) % DEC_SEQ)
    sink = jnp.concatenate([sink_ref[:, 0:1]] * nb, axis=0)
    qs = [_bf16(q_ref[b]) for b in range(nb)]
    s_c = jnp.concatenate([_dot(qs[b], _bf16(kt_ref[b])) for b in range(nb)], axis=0)
    s_n = jnp.concatenate([_dot_nt(qs[b], _bf16(kn_ref[b])) for b in range(nb)], axis=0)
    s_c = jnp.where(valid_c, s_c, -jnp.inf)
    s_n = jnp.where(valid_n, s_n, -jnp.inf)
    m = jnp.maximum(jnp.maximum(jnp.max(s_c, axis=1, keepdims=True), jnp.max(s_n, axis=1, keepdims=True)), sink)
    p_c = jnp.exp(s_c - m)
    p_n = jnp.exp(s_n - m)
    rden = 1.0 / (jnp.sum(p_c, axis=1, keepdims=True) + jnp.sum(p_n, axis=1, keepdims=True) + jnp.exp(sink - m))
    p_c, p_n = _bf16(p_c), _bf16(p_n)
    lane = lax.broadcasted_iota(jnp.int32, (KV_WIDTH, WINDOW), 1)
    shift = WINDOW - DEC_SEQ
    zrows = jnp.zeros((KV_WIDTH - NEW_ROWS, KV_WIDTH), jnp.float32)
    for b in range(nb):
        r = slice(b * SWA_ROWS, (b + 1) * SWA_ROWS)
        kt, vt = kt_ref[b], vt_ref[b]
        kn, vn = kn_ref[b], vn_ref[b]
        o_ref[b] = (_dot_nt(p_c[r], _bf16(vt)) + _dot(p_n[r], _bf16(vn))) * rden[r]
        for old, new, dst in ((kt, kn, okt_ref), (vt, vn, ovt_ref)):
            new_cols = pltpu.roll(jnp.concatenate([new, zrows], axis=0).T, shift, axis=1)
            dst[b] = jnp.where(lane >= shift, new_cols, pltpu.roll(old, shift, axis=1))


def _swa_sample(sinks, q, kn, vn, cache_k, cache_v):
    nb = cache_k.shape[0]
    bb = SWA_BB
    groups = N_HEADS // N_KV_HEADS
    qh = q.reshape(nb, DEC_SEQ, N_KV_HEADS, groups, HEAD_DIM).transpose(0, 2, 3, 1, 4)
    qh = qh.reshape(nb, N_KV_HEADS, groups * DEC_SEQ, HEAD_DIM)
    zeros = jnp.zeros_like(qh[:, 0])
    qbd = jnp.concatenate([jnp.concatenate([qh[:, 0], zeros], axis=-1),
                           jnp.concatenate([zeros, qh[:, 1]], axis=-1)], axis=1)
    sink_col = jnp.broadcast_to(jnp.repeat(sinks, DEC_SEQ).reshape(SWA_ROWS, 1), (SWA_ROWS, LANES))
    pad8 = lambda a: jnp.pad(a.reshape(nb, DEC_SEQ, KV_WIDTH), ((0, 0), (0, NEW_ROWS - DEC_SEQ), (0, 0)))
    to_t = lambda c: c.transpose(0, 2, 3, 1).reshape(nb, KV_WIDTH, WINDOW)
    blk = lambda r, w: pl.BlockSpec((bb, r, w), lambda i: (i, 0, 0))
    o, okt, ovt = pl.pallas_call(
        _swa_sample_kernel,
        grid=(nb // bb,),
        in_specs=[blk(SWA_ROWS, KV_WIDTH), pl.BlockSpec((SWA_ROWS, LANES), lambda i: (0, 0)),
                  blk(NEW_ROWS, KV_WIDTH), blk(NEW_ROWS, KV_WIDTH), blk(KV_WIDTH, WINDOW), blk(KV_WIDTH, WINDOW)],
        out_specs=[blk(SWA_ROWS, KV_WIDTH), blk(KV_WIDTH, WINDOW), blk(KV_WIDTH, WINDOW)],
        out_shape=[jax.ShapeDtypeStruct((nb, SWA_ROWS, KV_WIDTH), jnp.float32),
                   jax.ShapeDtypeStruct((nb, KV_WIDTH, WINDOW), jnp.float32),
                   jax.ShapeDtypeStruct((nb, KV_WIDTH, WINDOW), jnp.float32)],
        compiler_params=_params(("arbitrary",)),
        name="swa_sample",
    )(qbd, sink_col, pad8(kn), pad8(vn), to_t(cache_k), to_t(cache_v))
    o = o.reshape(nb, N_KV_HEADS, groups, DEC_SEQ, N_KV_HEADS, HEAD_DIM)
    attn = jnp.stack([o[:, h, :, :, h, :] for h in range(N_KV_HEADS)], axis=1)
    attn = attn.transpose(0, 3, 1, 2, 4).reshape(nb * DEC_SEQ, ATTN_WIDTH)
    from_t = lambda c: c.reshape(nb, N_KV_HEADS, HEAD_DIM, WINDOW).transpose(0, 3, 1, 2)
    return attn, from_t(okt), from_t(ovt)


def _mem_kv_kernel(mem_ref, wk_ref, wv_ref, wq_ref, wo_ref, mk_ref, mv_ref, wqk_ref, wvo_ref):
    mb = _bf16(mem_ref[...])
    mk = _dot(mb, wk_ref[...])
    mv = _dot(mb, wv_ref[...])
    mk_ref[...] = mk
    mv_ref[...] = mv
    mkb, mvb = _bf16(mk), _bf16(mv)
    for h in range(MEM_HEADS):
        sl = slice(h * MEM_HEAD_DIM, (h + 1) * MEM_HEAD_DIM)
        keys = slice(h * N_MEM, (h + 1) * N_MEM)
        wqk_ref[:, keys] = _bf16(_dot_nt(wq_ref[:, sl], mkb[:, sl]) * (MEM_HEAD_DIM ** -0.5))
        wvo_ref[keys, :] = _bf16(_dot(mvb[:, sl], wo_ref[sl, :]))


def _mem_kv(mem, wk_b, wv_b, wq_b, wo_b):
    full = lambda a: pl.BlockSpec(a.shape, lambda i: (0,) * a.ndim)
    blk = pl.BlockSpec((N_MEM, D_MODEL), lambda i: (0, 0))
    f32 = jax.ShapeDtypeStruct((N_MEM, D_MODEL), jnp.float32)
    fused = (D_MODEL, MEM_HEADS * N_MEM), (MEM_HEADS * N_MEM, D_MODEL)
    return pl.pallas_call(
        _mem_kv_kernel,
        grid=(1,),
        in_specs=[full(mem), full(wk_b), full(wv_b), full(wq_b), full(wo_b)],
        out_specs=[blk, blk] + [pl.BlockSpec(shp, lambda i: (0, 0)) for shp in fused],
        out_shape=[f32, f32] + [jax.ShapeDtypeStruct(shp, jnp.bfloat16) for shp in fused],
        compiler_params=_params(("arbitrary",)),
        name="mem_kv",
    )(mem, wk_b, wv_b, wq_b, wo_b)


def _mix_ln1(attn_ref, conv_ref, x_ref, wmix_ref, g1_ref, b1_ref):
    mix = _dot(_bf16(attn_ref[...]), wmix_ref[0:ATTN_WIDTH, :]) + _dot(conv_ref[...], wmix_ref[ATTN_WIDTH:, :])
    return _layer_norm(ALPHA * x_ref[...] + mix, g1_ref[...], b1_ref[...])


def _mem_q(h1, wq_ref):
    return _bf16(_dot(_bf16(h1), wq_ref[...]) * (MEM_HEAD_DIM ** -0.5))


def _route(h2, wrhl_ref, br_ref, tri_ref, carry):
    hi = _bf16(h2)
    lo = _bf16(h2 - hi.astype(jnp.float32))
    hh = _dot(hi, wrhl_ref[...])
    logits = hh[:, 0:LANES] + hh[:, LANES:] + _dot(lo, wrhl_ref[:, 0:LANES]) + br_ref[...]
    lane_i = lax.broadcasted_iota(jnp.int32, logits.shape, 1)
    lane = lane_i.astype(jnp.float32)
    big = jnp.float32(LANES)
    is_g = lane_i < N_GROUPS
    gl = jnp.where(is_g, logits, -jnp.inf)
    gmax = jnp.max(gl, axis=1, keepdims=True)
    gidx = jnp.min(jnp.where(is_g & (logits == gmax), lane, big), axis=1, keepdims=True)
    gsum = jnp.sum(jnp.exp(gl - gmax), axis=1, keepdims=True)
    gw = 1.0 / gsum
    eid = lane_i - N_GROUPS
    assert EXPERTS_PER_GROUP == 8
    grp = lax.shift_right_arithmetic(eid, jnp.full_like(eid, 3)).astype(jnp.float32)
    in_e = (lane_i >= N_GROUPS) & (lane_i < N_GROUPS + N_EXPERTS) & (grp == gidx)
    v1 = jnp.max(jnp.where(in_e, logits, -jnp.inf), axis=1, keepdims=True)
    i1 = jnp.min(jnp.where(in_e & (logits == v1), lane, big), axis=1, keepdims=True)
    rest = in_e & (lane != i1)
    v2 = jnp.max(jnp.where(rest, logits, -jnp.inf), axis=1, keepdims=True)
    i2 = jnp.min(jnp.where(rest & (logits == v2), lane, big), axis=1, keepdims=True)
    ex = jnp.exp(v2 - v1)
    den = 1.0 + ex
    w1 = gw / den
    w2 = gw * ex / den
    zero = jnp.zeros_like(logits)
    pick1 = lane == i1
    pick2 = lane == i2
    sel = jnp.where(pick1 | pick2, 1.0, 0.0)
    before = _dot(tri_ref[...], _bf16(sel)) + carry
    rank1 = jnp.sum(jnp.where(pick1, before, zero), axis=1, keepdims=True)
    rank2 = jnp.sum(jnp.where(pick2, before, zero), axis=1, keepdims=True)
    cols = (i1 - N_GROUPS, i2 - N_GROUPS, w1, w2, rank1, rank2)
    route = zero
    for k, col in enumerate(cols):
        route = jnp.where(lane_i == k, col, route)
    return route, carry + jnp.sum(sel, axis=0, keepdims=True)


def _post_tile(attn, conv_ref, x_ref, wmix_ref, g1_ref, b1_ref, wqk_ref, wvo_ref, g2_ref, b2_ref,
               wrhl_ref, br_ref, tri_ref, h2t_ref, rt_ref, cnt_ref, carry_ref):
    mix = _dot(attn, wmix_ref[0:ATTN_WIDTH, :]) + _dot(conv_ref[...], wmix_ref[ATTN_WIDTH:, :])
    h1 = _layer_norm(ALPHA * x_ref[...] + mix, g1_ref[...], b1_ref[...])
    scores = _dot(_bf16(h1), wqk_ref[...])
    probs = []
    for h in range(MEM_HEADS):
        s = scores[:, h * N_MEM:(h + 1) * N_MEM]
        p = jnp.exp(s - jnp.max(s, axis=1, keepdims=True))
        probs.append(_bf16(p / jnp.sum(p, axis=1, keepdims=True)))
    mem_out = _dot(jnp.concatenate(probs, axis=1), wvo_ref[...])
    h2 = _layer_norm(ALPHA * h1 + mem_out, g2_ref[...], b2_ref[...])
    _store_token_tiles(h2t_ref, h2)
    route, carry = _route(h2, wrhl_ref, br_ref, tri_ref, carry_ref[0:1, :])
    rt_ref[...] = route
    carry_ref[...] = jnp.broadcast_to(carry, carry_ref.shape)
    cnt_ref[...] = jnp.broadcast_to(carry, cnt_ref.shape)


def _swa_post_kernel(sinks_ref, q_ref, kc_ref, kp_ref, vc_ref, vp_ref,
                     conv_ref, x_ref, wmix_ref, g1_ref, b1_ref, wqk_ref, wvo_ref, g2_ref, b2_ref,
                     wrhl_ref, br_ref, tri_ref, h2s_ref, rts_ref, cnts_ref, wg_ref, wu_ref, wd_ref,
                     h2t_ref, rt_ref, cnt_ref, wgb_ref, wub_ref, wdb_ref, carry_ref, attn_s):
    t = pl.program_id(0)
    steps = N_PROMPT // TM_POST
    par = t % 2

    def swa(slot):
        def store(rows, cols, val):
            attn_s[slot, rows, cols] = val
        _swa_tile(t, sinks_ref, q_ref, kc_ref, kp_ref, vc_ref, vp_ref, store)

    def post(slot):
        _post_tile(attn_s[slot], conv_ref, x_ref, wmix_ref, g1_ref, b1_ref, wqk_ref, wvo_ref, g2_ref, b2_ref,
                   wrhl_ref, br_ref, tri_ref, h2t_ref, rt_ref, cnt_ref, carry_ref)

    def cast_expert():
        wgb_ref[...] = _bf16(wg_ref[...])
        wub_ref[...] = _bf16(wu_ref[...])
        wdb_ref[...] = _bf16(wd_ref[...])

    @pl.when(t == 0)
    def _():
        carry_ref[...] = cnts_ref[...]
        swa(0)
        cast_expert()

    @pl.when((t >= 1) & (t < steps))
    def _():
        swa(par)
        post(1 - par)
        cast_expert()

    @pl.when(t == steps)
    def _():
        post(1 - par)

    @pl.when(t == steps + 1)
    def _():
        h2t_ref[...] = h2s_ref[...]
        rt_ref[...] = rts_ref[...]


def _swa_post(sinks, q, kx, vx, conv, x, wmix_b, g1, b1, wqk, wvo, g2, b2, wrhl, br, tri, h2t_s, rt_s, cnt_s,
              w_gate, w_up, w_down):
    n = x.shape[0]
    tm = TM_POST
    assert tm == SWA_QB * WINDOW
    steps = n // tm
    assert N_EXPERTS <= steps
    expert = lambda a: pl.BlockSpec((1,) + a.shape[1:], lambda i: (jnp.minimum(i, N_EXPERTS - 1), 0, 0))
    experts = (w_gate, w_up, w_down)
    cur = lambda w: pl.BlockSpec((tm, w), lambda i: (jnp.minimum(i, steps - 1), 0))
    prev = lambda w: pl.BlockSpec((WINDOW, w), lambda i: (jnp.clip(SWA_QB * i - 1, 0, n // WINDOW - 1), 0))
    lag = lambda w: pl.BlockSpec((tm, w), lambda i: (jnp.clip(i - 1, 0, steps - 1), 0))
    full = lambda a: pl.BlockSpec(a.shape, lambda i: (0,) * a.ndim)
    weights = (wmix_b, g1, b1, wqk, wvo, g2, b2, wrhl, br, tri, h2t_s, rt_s, cnt_s)
    n_out = n + h2t_s.shape[0] // ROW_CHUNKS
    out_idx = lambda i: (jnp.where(i > steps, steps, jnp.clip(i - 1, 0, steps - 1)), 0)
    return pl.pallas_call(
        _swa_post_kernel,
        grid=(steps + 2,),
        in_specs=([pl.BlockSpec(memory_space=pltpu.SMEM), cur(ATTN_WIDTH),
                   cur(4 * LANES), prev(4 * LANES), cur(4 * LANES), prev(4 * LANES),
                   lag(CONV_CH), lag(D_MODEL)] + [full(a) for a in weights] + [expert(a) for a in experts]),
        out_specs=[pl.BlockSpec((tm * ROW_CHUNKS, LANES), out_idx),
                   pl.BlockSpec((tm, LANES), out_idx),
                   pl.BlockSpec((SUBLANES, LANES), lambda i: (0, 0))] + [expert(a) for a in experts],
        out_shape=[jax.ShapeDtypeStruct((n_out * ROW_CHUNKS, LANES), jnp.float32),
                   jax.ShapeDtypeStruct((n_out, LANES), jnp.float32),
                   jax.ShapeDtypeStruct((SUBLANES, LANES), jnp.float32)]
                  + [jax.ShapeDtypeStruct(a.shape, jnp.bfloat16) for a in experts],
        scratch_shapes=[pltpu.VMEM((SUBLANES, LANES), jnp.float32),
                        pltpu.VMEM((2, tm, ATTN_WIDTH), jnp.bfloat16)],
        compiler_params=_params(("arbitrary",)),
        name="swa_post_prompt",
    )(sinks, q, kx, kx, vx, vx, conv, x, *weights, *experts)


def _post_a_sample_kernel(attn_ref, conv_ref, x_ref, wmix_ref, g1_ref, b1_ref, wq_ref, h1_ref, qm_ref):
    h1 = _mix_ln1(attn_ref, conv_ref, x_ref, wmix_ref, g1_ref, b1_ref)
    h1_ref[...] = h1
    qm_ref[...] = _mem_q(h1, wq_ref).astype(jnp.float32)


def _post_a_sample(attn, conv, x, wmix_b, g1, b1, wq_b):
    n = x.shape[0]
    args = (attn, conv, x, wmix_b, g1, b1, wq_b)
    full = lambda a: pl.BlockSpec(a.shape, lambda i: (0,) * a.ndim)
    blk = pl.BlockSpec((n, D_MODEL), lambda i: (0, 0))
    return pl.pallas_call(
        _post_a_sample_kernel,
        grid=(1,),
        in_specs=[full(a) for a in args],
        out_specs=[blk, blk],
        out_shape=[jax.ShapeDtypeStruct((n, D_MODEL), jnp.float32),
                   jax.ShapeDtypeStruct((n, D_MODEL), jnp.float32)],
        compiler_params=_params(("arbitrary",)),
        name="post_a_sample",
    )(*args)


MEM_ROWS = MEM_HEADS * DEC_SEQ


def _mem_attn_sample_body(q_ref, mk_ref, mv_ref, b0, o_ref):
    nk = N_MEM * MEM_HEADS
    nb = SAMPLE_BB
    rows = nb * MEM_ROWS
    row_h = (lax.broadcasted_iota(jnp.int32, (rows, nk), 0) % MEM_ROWS) // DEC_SEQ
    key_h = lax.broadcasted_iota(jnp.int32, (rows, nk), 1) % MEM_HEADS
    s = jnp.concatenate(
        [_dot_nt(_bf16(q_ref[b]), _bf16(mk_ref[b0 + b].reshape(nk, MEM_HEAD_DIM))) for b in range(nb)], axis=0)
    s = jnp.where(row_h == key_h, s, -jnp.inf)
    p = jnp.exp(s - jnp.max(s, axis=1, keepdims=True))
    rden = 1.0 / jnp.sum(p, axis=1, keepdims=True)
    p = _bf16(p)
    for b in range(nb):
        r = slice(b * MEM_ROWS, (b + 1) * MEM_ROWS)
        o_ref[b] = _dot(p[r], _bf16(mv_ref[b0 + b].reshape(nk, MEM_HEAD_DIM))) * rden[r]


def _proj_mem_kernel(steps, x_ref, w_ref, tab_ref, cw_ref, mq_ref, mk_hbm, mv_hbm,
                     q_ref, kx_ref, vx_ref, conv_ref, ktail_ref, vtail_ref, utail_ref, mo_ref,
                     carry_ref, kbuf, vbuf, sem):
    i = pl.program_id(0)

    def fetch(step):
        slot = step % MEM_SLOTS
        src = pl.ds(step * SAMPLE_BB, SAMPLE_BB)
        dst = pl.ds(slot * SAMPLE_BB, SAMPLE_BB)
        return (pltpu.make_async_copy(mk_hbm.at[src], kbuf.at[dst], sem.at[0, slot]),
                pltpu.make_async_copy(mv_hbm.at[src], vbuf.at[dst], sem.at[1, slot]))

    @pl.when(i == 0)
    def _():
        for k in range(MEM_SLOTS - 1):
            for c in fetch(jnp.int32(k)):
                c.start()

    @pl.when(i + MEM_SLOTS - 1 < steps)
    def _():
        for c in fetch(i + MEM_SLOTS - 1):
            c.start()

    for c in fetch(i):
        c.wait()
    _mem_attn_sample_body(mq_ref, kbuf, vbuf, (i % MEM_SLOTS) * SAMPLE_BB, mo_ref)
    _proj_prompt_body(x_ref, w_ref, tab_ref, cw_ref,
                      q_ref, kx_ref, vx_ref, conv_ref, ktail_ref, vtail_ref, utail_ref, carry_ref)


def _proj_prompt_mem_sample(x, w_in_b, tab, conv_w, qm, mk, mv):
    n = x.shape[0]
    nb = mk.shape[0]
    bb = SAMPLE_BB
    steps = nb // bb
    tm = n // steps
    row = lambda w: pl.BlockSpec((tm, w), lambda i: (i, 0))
    full = lambda a: pl.BlockSpec(a.shape, lambda i: (0,) * a.ndim)
    const = lambda r, w: pl.BlockSpec((r, w), lambda i: (0, 0))
    mq = qm.reshape(nb, DEC_SEQ, MEM_HEADS, MEM_HEAD_DIM).transpose(0, 2, 1, 3).reshape(nb, MEM_ROWS, MEM_HEAD_DIM)
    mrows = pl.BlockSpec((bb, MEM_ROWS, MEM_HEAD_DIM), lambda i: (i, 0, 0))
    kv = pl.BlockSpec(memory_space=pl.ANY)
    assert steps >= MEM_SLOTS - 1
    ring = pltpu.VMEM((MEM_SLOTS * bb, N_MEM, MEM_HEADS, MEM_HEAD_DIM), jnp.float32)
    outs = pl.pallas_call(
        functools.partial(_proj_mem_kernel, steps),
        grid=(steps,),
        in_specs=[row(D_MODEL), full(w_in_b), pl.BlockSpec((ROPE_ROWS, tm), lambda i: (0, i)), full(conv_w),
                  mrows, kv, kv],
        out_specs=[row(ATTN_WIDTH), row(4 * LANES), row(4 * LANES), row(CONV_CH),
                   const(WINDOW, KV_WIDTH), const(WINDOW, KV_WIDTH), const(SUBLANES, CONV_CH), mrows],
        out_shape=[jax.ShapeDtypeStruct((n, ATTN_WIDTH), jnp.bfloat16),
                   jax.ShapeDtypeStruct((n, 4 * LANES), jnp.bfloat16),
                   jax.ShapeDtypeStruct((n, 4 * LANES), jnp.bfloat16),
                   jax.ShapeDtypeStruct((n, CONV_CH), jnp.bfloat16),
                   jax.ShapeDtypeStruct((WINDOW, KV_WIDTH), jnp.float32),
                   jax.ShapeDtypeStruct((WINDOW, KV_WIDTH), jnp.float32),
                   jax.ShapeDtypeStruct((SUBLANES, CONV_CH), jnp.float32),
                   jax.ShapeDtypeStruct((nb, MEM_ROWS, MEM_HEAD_DIM), jnp.float32)],
        scratch_shapes=[pltpu.VMEM((SUBLANES, CONV_CH), jnp.float32), ring, ring,
                        pltpu.SemaphoreType.DMA((2, MEM_SLOTS))],
        compiler_params=_params(("arbitrary",)),
        name="proj_prompt_mem_sample",
    )(x, w_in_b, tab, conv_w, mq, mk, mv)
    o = outs[7].reshape(nb, MEM_HEADS, DEC_SEQ, MEM_HEAD_DIM).transpose(0, 2, 1, 3).reshape(nb * DEC_SEQ, D_MODEL)
    return outs[:7], o


def _post_b_sample_kernel(o_ref, h1_ref, wo_ref, g2_ref, b2_ref, wrhl_ref, br_ref, tri_ref,
                          h2t_ref, rt_ref, cnt_ref):
    h2 = _layer_norm(ALPHA * h1_ref[...] + _dot(_bf16(o_ref[...]), wo_ref[...]), g2_ref[...], b2_ref[...])
    _store_token_tiles(h2t_ref, h2)
    route, carry = _route(h2, wrhl_ref, br_ref, tri_ref, jnp.zeros((1, LANES), jnp.float32))
    rt_ref[...] = route
    cnt_ref[...] = jnp.broadcast_to(carry, cnt_ref.shape)


def _post_b_sample(o, h1, wo_b, g2, b2, wrhl, br, tri):
    n = h1.shape[0]
    args = (o, h1, wo_b, g2, b2, wrhl, br, tri)
    full = lambda a: pl.BlockSpec(a.shape, lambda i: (0,) * a.ndim)
    return pl.pallas_call(
        _post_b_sample_kernel,
        grid=(1,),
        in_specs=[full(a) for a in args],
        out_specs=[pl.BlockSpec((n * ROW_CHUNKS, LANES), lambda i: (0, 0)),
                   pl.BlockSpec((n, LANES), lambda i: (0, 0)),
                   pl.BlockSpec((SUBLANES, LANES), lambda i: (0, 0))],
        out_shape=[jax.ShapeDtypeStruct((n * ROW_CHUNKS, LANES), jnp.float32),
                   jax.ShapeDtypeStruct((n, LANES), jnp.float32),
                   jax.ShapeDtypeStruct((SUBLANES, LANES), jnp.float32)],
        compiler_params=_params(("arbitrary",)),
        name="post_b_sample",
    )(*args)


def _row_gather_copy(src_hbm, idx, dst, dst_row, sem):
    s0 = pl.multiple_of(idx * ROW_CHUNKS, ROW_CHUNKS)
    d0 = pl.multiple_of(dst_row * ROW_CHUNKS, ROW_CHUNKS)
    return pltpu.make_async_copy(src_hbm.at[pl.ds(s0, ROW_CHUNKS), :], dst.at[pl.ds(d0, ROW_CHUNKS), :], sem)


def _dispatch_kernel(pos_ref, h2t_ref, xs_hbm, sem):
    def body(r, c):
        src = h2t_ref.at[pl.ds(pl.multiple_of(r * ROW_CHUNKS, ROW_CHUNKS), ROW_CHUNKS), :]
        for k in range(2):
            d0 = pl.multiple_of(pos_ref[0, 0, k * TM_COMB + r] * ROW_CHUNKS, ROW_CHUNKS)
            pltpu.make_async_copy(src, xs_hbm.at[pl.ds(d0, ROW_CHUNKS), :], sem.at[0]).start(priority=k)
        return c
    lax.fori_loop(0, TM_COMB, body, 0, unroll=8)
    for _ in range(2):
        pltpu.make_async_copy(h2t_ref, xs_hbm.at[pl.ds(0, TM_COMB * ROW_CHUNKS), :], sem.at[0]).wait()


def _dispatch(pos3, h2t):
    nt = N_ALL // TM_COMB
    return pl.pallas_call(
        _dispatch_kernel,
        grid=(nt,),
        in_specs=[pl.BlockSpec((1, 1, 2 * TM_COMB), lambda i: (i, 0, 0), memory_space=pltpu.SMEM),
                  pl.BlockSpec((TM_COMB * ROW_CHUNKS, LANES), lambda i: (i, 0))],
        out_specs=pl.BlockSpec(memory_space=pl.ANY),
        out_shape=jax.ShapeDtypeStruct((N_ASSIGN * ROW_CHUNKS, LANES), jnp.float32),
        scratch_shapes=[pltpu.SemaphoreType.DMA((1,))],
        compiler_params=_params(("arbitrary",)),
        name="moe_dispatch",
    )(pos3, h2t)


def _moe_ffn_kernel(it_ref, ie_ref, lo_ref, hi_ref, x_hbm, wg_hbm, wu_hbm, wd_hbm, y_ref,
                    xbuf, wgb, wub, wdb, sem, wsem, run_ref):
    i = pl.program_id(0)
    lo = lo_ref[i]
    hi = hi_ref[i]
    t = it_ref[i]
    tile_rows = TM_MOE * ROW_CHUNKS

    def fetch(tile):
        src = pl.multiple_of(tile * tile_rows, tile_rows)
        dst = pl.multiple_of((tile % MOE_X_SLOTS) * tile_rows, tile_rows)
        return pltpu.make_async_copy(x_hbm.at[pl.ds(src, tile_rows), :], xbuf.at[pl.ds(dst, tile_rows), :],
                                     sem.at[tile % MOE_X_SLOTS])

    @pl.when(i == 0)
    def _():
        for k in range(MOE_X_SLOTS - 1):
            fetch(jnp.int32(k)).start()

    @pl.when((i == 0) | (t != it_ref[jnp.maximum(i - 1, 0)]))
    def _():
        @pl.when(t + MOE_X_SLOTS - 1 < MOE_TILES)
        def _():
            fetch(t + MOE_X_SLOTS - 1).start()
        fetch(t).wait()

    base = pl.multiple_of((t % MOE_X_SLOTS) * tile_rows, tile_rows)

    e = ie_ref[i]

    def wfetch(expert, slot):
        return [pltpu.make_async_copy(src.at[expert], dst.at[slot], wsem.at[slot])
                for src, dst in ((wg_hbm, wgb), (wu_hbm, wub), (wd_hbm, wdb))]

    @pl.when(i == 0)
    def _():
        run_ref[0] = 0
        for c in wfetch(e, 0):
            c.start()

    @pl.when((i == 0) | (e != ie_ref[jnp.maximum(i - 1, 0)]))
    def _():
        @pl.when(i > 0)
        def _():
            run_ref[0] = run_ref[0] + 1
        slot = run_ref[0] % 2
        item = lambda j: ie_ref[jnp.minimum(j, MOE_ITEMS - 1)]
        nxt = lax.while_loop(lambda j: (j < MOE_ITEMS) & (item(j) == e), lambda j: j + 1, i + 1)

        @pl.when(nxt < MOE_ITEMS)
        def _():
            for c in wfetch(item(nxt), 1 - slot):
                c.start()
        for c in wfetch(e, slot):
            c.wait()

    wslot = run_ref[0] % 2

    def ffn(r0, rows):
        x = _bf16(_load_token_tiles(xbuf, base + r0 * ROW_CHUNKS, rows))
        hg = _dot(x, wgb[wslot])
        hu = _dot(x, wub[wslot])
        h = hg / (1.0 + jnp.exp(-hg)) * hu
        return _dot(_bf16(h), wdb[wslot])

    whole = (lo == 0) & (hi == TM_MOE)

    @pl.when(whole)
    def _():
        _store_token_tiles(y_ref, ffn(0, TM_MOE))

    for r0 in range(0, TM_MOE, MOE_BLOCK):
        live = (hi > lo) & jnp.logical_not(whole) & (hi > r0) & (lo < r0 + MOE_BLOCK)

        def store(merge, r0=r0):
            y = ffn(r0, MOE_BLOCK)
            row = r0 + lax.broadcasted_iota(jnp.int32, (MOE_BLOCK, LANES), 0)
            mask = (row >= lo) & (row < hi)
            for c in range(ROW_CHUNKS):
                sl = pl.ds(r0 * ROW_CHUNKS + c, MOE_BLOCK, stride=ROW_CHUNKS)
                y_ref[sl, :] = jnp.where(mask, y[:, c * LANES:(c + 1) * LANES], y_ref[sl, :] if merge else 0.0)

        pl.when(live & (lo <= r0))(functools.partial(store, False))
        pl.when(live & (lo > r0))(functools.partial(store, True))


def _moe_ffn(item_tile, item_expert, item_lo, item_hi, x_sorted, w_gate, w_up, w_down):
    tile = pl.BlockSpec((TM_MOE * ROW_CHUNKS, LANES), lambda i, it, ie, lo, hi: (it[i], 0))
    assert MOE_TILES >= MOE_X_SLOTS - 1
    grid_spec = pltpu.PrefetchScalarGridSpec(
        num_scalar_prefetch=4,
        grid=(MOE_ITEMS,),
        in_specs=[pl.BlockSpec(memory_space=pl.ANY)] * 4,
        out_specs=tile,
        scratch_shapes=[pltpu.VMEM((MOE_X_SLOTS * TM_MOE * ROW_CHUNKS, LANES), jnp.float32),
                        pltpu.VMEM((2, D_MODEL, EXPERT_FF), jnp.bfloat16),
                        pltpu.VMEM((2, D_MODEL, EXPERT_FF), jnp.bfloat16),
                        pltpu.VMEM((2, EXPERT_FF, D_MODEL), jnp.bfloat16),
                        pltpu.SemaphoreType.DMA((MOE_X_SLOTS,)),
                        pltpu.SemaphoreType.DMA((2,)),
                        pltpu.SMEM((1,), jnp.int32)],
    )
    return pl.pallas_call(
        _moe_ffn_kernel,
        grid_spec=grid_spec,
        out_shape=jax.ShapeDtypeStruct((N_ASSIGN * ROW_CHUNKS, LANES), jnp.float32),
        compiler_params=_params(("arbitrary",)),
        name="moe_ffn",
    )(item_tile, item_expert, item_lo, item_hi, x_sorted, w_gate, w_up, w_down)


def _combine_kernel(nt, pos_cur_ref, pos_nxt_ref, yt_hbm, h2t_ref, rt_ref, g3_ref, b3_ref, o_ref, abuf, sem):
    t = pl.program_id(0)
    slot = t % 2
    rows = 2 * TM_COMB

    def issue(pos_ref, s):
        def body(j, c):
            for k in range(2):
                r = 2 * j + k
                _row_gather_copy(yt_hbm, pos_ref[0, 0, r], abuf, s * rows + r, sem.at[s]).start(priority=k)
            return c
        lax.fori_loop(0, rows // 2, body, 0, unroll=16)

    @pl.when(t == 0)
    def _():
        issue(pos_cur_ref, 0)

    def wait(s):
        pltpu.make_async_copy(yt_hbm.at[pl.ds(0, rows * ROW_CHUNKS), :],
                              abuf.at[pl.ds(s * rows * ROW_CHUNKS, rows * ROW_CHUNKS), :], sem.at[s]).wait()

    def compute(s, r0, n):
        base = (s * rows + r0) * ROW_CHUNKS
        ya = _load_token_tiles(abuf, base, n)
        yb = _load_token_tiles(abuf, base + TM_COMB * ROW_CHUNKS, n)
        rt = rt_ref[r0:r0 + n, :]
        ff = rt[:, 2:3] * ya + rt[:, 3:4] * yb
        h2 = _load_token_tiles(h2t_ref, r0 * ROW_CHUNKS, n)
        o_ref[r0:r0 + n, :] = _layer_norm(ALPHA * h2 + ff, g3_ref[...], b3_ref[...])

    for s in range(2):
        @pl.when((t + 1 < nt) & (slot == s))
        def _(s=s):
            wait(s)
            n = TM_COMB // COMB_CHUNKS
            for j in range(COMB_CHUNKS):
                for r in range(j * rows // COMB_CHUNKS, (j + 1) * rows // COMB_CHUNKS):
                    _row_gather_copy(yt_hbm, pos_nxt_ref[0, 0, r], abuf, (1 - s) * rows + r,
                                     sem.at[1 - s]).start(priority=r % 2)
                compute(s, j * n, n)

        @pl.when((t + 1 >= nt) & (slot == s))
        def _(s=s):
            wait(s)
            compute(s, 0, TM_COMB)


def _combine(pos3, yt, h2t, rt, g3, b3, tile0, n_tiles):
    last = tile0 + n_tiles - 1
    smem_pos = lambda f: pl.BlockSpec((1, 1, 2 * TM_COMB), f, memory_space=pltpu.SMEM)
    full = lambda a: pl.BlockSpec(a.shape, lambda i: (0,) * a.ndim)
    return pl.pallas_call(
        functools.partial(_combine_kernel, n_tiles),
        grid=(n_tiles,),
        in_specs=[smem_pos(lambda i: (tile0 + i, 0, 0)),
                  smem_pos(lambda i: (jnp.minimum(tile0 + i + 1, last), 0, 0)),
                  pl.BlockSpec(memory_space=pl.ANY),
                  pl.BlockSpec((TM_COMB * ROW_CHUNKS, LANES), lambda i: (tile0 + i, 0)),
                  pl.BlockSpec((TM_COMB, LANES), lambda i: (tile0 + i, 0)),
                  full(g3), full(b3)],
        out_specs=pl.BlockSpec((TM_COMB, D_MODEL), lambda i: (i, 0)),
        out_shape=jax.ShapeDtypeStruct((n_tiles * TM_COMB, D_MODEL), jnp.float32),
        scratch_shapes=[pltpu.VMEM((2 * 2 * TM_COMB * ROW_CHUNKS, LANES), jnp.float32),
                        pltpu.SemaphoreType.DMA((2,))],
        compiler_params=_params(("arbitrary",)),
        name="moe_combine",
    )(pos3, pos3, yt, h2t, rt, g3, b3)


COMB_CHUNKS = 16
POS_TILES = 11


def _positions_kernel(rt_ref, starts_ref, pos_ref):
    expert = lax.broadcasted_iota(jnp.int32, (N_EXPERTS, TM_COMB), 0).astype(jnp.float32)
    starts = jnp.concatenate([starts_ref[...]] * (TM_COMB // LANES), axis=1)
    for j in range(POS_TILES):
        cols = rt_ref[j * TM_COMB:(j + 1) * TM_COMB, :].T
        out = []
        for k in range(2):
            seg = jnp.sum(jnp.where(expert == cols[k:k + 1, :], starts, 0.0), axis=0, keepdims=True)
            out.append(seg + cols[4 + k:5 + k, :])
        pos_ref[j] = jnp.concatenate(out, axis=1).astype(jnp.int32)


def _positions(rt, starts_rep):
    nt = N_ALL // TM_COMB
    assert nt % POS_TILES == 0
    return pl.pallas_call(
        _positions_kernel,
        grid=(nt // POS_TILES,),
        in_specs=[pl.BlockSpec((POS_TILES * TM_COMB, LANES), lambda i: (i, 0)),
                  pl.BlockSpec((N_EXPERTS, LANES), lambda i: (0, 0))],
        out_specs=pl.BlockSpec((POS_TILES, 1, 2 * TM_COMB), lambda i: (i, 0, 0)),
        out_shape=jax.ShapeDtypeStruct((nt, 1, 2 * TM_COMB), jnp.int32),
        compiler_params=_params(("arbitrary",)),
        name="moe_positions",
    )(rt, starts_rep)


def _routing_plan(rt, cnt):
    i32 = jnp.int32
    counts_f = cnt[0, N_GROUPS:N_GROUPS + N_EXPERTS]
    starts_f = jnp.cumsum(counts_f) - counts_f
    pos3 = _positions(rt, jnp.broadcast_to(starts_f[:, None], (N_EXPERTS, LANES)))
    starts = starts_f.astype(i32)
    tiles = jnp.arange(MOE_TILES, dtype=i32) * TM_MOE
    rank_t = jnp.arange(MOE_TILES, dtype=i32) + jnp.sum((starts[None, :] < tiles[:, None]).astype(i32), axis=1)
    rank_s = jnp.arange(N_EXPERTS, dtype=i32) + jnp.sum((tiles[None, :] <= starts[:, None]).astype(i32), axis=1)
    vals = jnp.concatenate([tiles, starts])
    ranks = jnp.concatenate([rank_t, rank_s])
    slot = jnp.arange(MOE_ITEMS, dtype=i32)
    lo = jnp.sum(jnp.where(ranks[None, :] == slot[:, None], vals[None, :], 0), axis=1)
    hi = jnp.concatenate([lo[1:], jnp.full((1,), N_ASSIGN, i32)])
    item_tile = jnp.minimum(lo // TM_MOE, MOE_TILES - 1)
    item_expert = jnp.clip(jnp.sum((starts[None, :] <= lo[:, None]).astype(i32), axis=1) - 1, 0, N_EXPERTS - 1)
    base = item_tile * TM_MOE
    return item_tile, item_expert, lo - base, hi - base, pos3


def kernel(x_prompt, x_sample, mem_prompt, cache_swa_k, cache_swa_v, cache_conv, cache_mem_k, cache_mem_v,
           w_in, sinks, conv_w, w_mix_out, ln1_g, ln1_b, w_q_mem, w_k_mem, w_v_mem, w_o_mem, ln2_g, ln2_b,
           w_router_group, b_router_group, w_router_expert, b_router_expert, w_gate, w_up, w_down,
           ln3_g, ln3_b):
    f32 = jnp.float32
    row = lambda a: a.reshape(1, -1).astype(f32)
    w_in_b, wmix_b, wq_b, wk_b, wv_b, wo_b = (_bf16(w) for w in (w_in, w_mix_out, w_q_mem, w_k_mem, w_v_mem, w_o_mem))
    g1, b1, g2, b2, g3, b3 = (row(a) for a in (ln1_g, ln1_b, ln2_g, ln2_b, ln3_g, ln3_b))
    pad = LANES - N_GROUPS - N_EXPERTS
    wr = jnp.concatenate([w_router_group, w_router_expert, jnp.zeros((D_MODEL, pad), f32)], axis=1)
    wrh = _bf16(wr)
    wrhl = jnp.concatenate([wrh, _bf16(wr - wrh.astype(f32))], axis=1)
    br = jnp.concatenate([b_router_group, b_router_expert, jnp.zeros((pad,), f32)]).reshape(1, LANES)

    xs = x_sample.reshape(N_SAMPLE, D_MODEL)
    tab_s = jnp.tile(_rope_table(PAST_LEN + jnp.arange(DEC_SEQ)), (1, DEC_BATCH))
    c0 = jnp.repeat(cache_conv[:, 0], DEC_SEQ, axis=0)
    c1 = jnp.repeat(cache_conv[:, 1], DEC_SEQ, axis=0)
    q_s, k_s, v_s, conv_s, u_s = _proj_sample(xs, w_in_b, tab_s, conv_w, c0, c1)
    attn_s, swa_k_s, swa_v_s = _swa_sample(sinks, q_s, k_s, v_s, cache_swa_k, cache_swa_v)
    h1_s, qm_s = _post_a_sample(attn_s, conv_s, xs, wmix_b, g1, b1, wq_b)
    xp = x_prompt.reshape(N_PROMPT, D_MODEL)
    tab_p = _rope_table(jnp.arange(N_PROMPT))
    (q_p, kx_p, vx_p, conv_p, k_tail, v_tail, u_tail), o_s = _proj_prompt_mem_sample(
        xp, w_in_b, tab_p, conv_w, qm_s, cache_mem_k, cache_mem_v)
    tri = _bf16(jnp.tril(jnp.ones((TM_POST, TM_POST), f32), -1))
    h2t_s, rt_s, cnt_s = _post_b_sample(o_s, h1_s, wo_b, g2, b2, wrhl, br, tri)

    mk, mv, wqk, wvo = _mem_kv(mem_prompt.reshape(N_MEM, D_MODEL), wk_b, wv_b, wq_b, wo_b)
    h2t, rt, cnt, wg_b, wu_b, wd_b = _swa_post(sinks, q_p, kx_p, vx_p, conv_p, xp, wmix_b, g1, b1, wqk, wvo, g2, b2,
                                               wrhl, br, tri, h2t_s, rt_s, cnt_s, w_gate, w_up, w_down)

    item_tile, item_expert, item_lo, item_hi, pos3 = _routing_plan(rt, cnt)
    x_sorted = _dispatch(pos3, h2t)
    yt = _moe_ffn(item_tile, item_expert, item_lo, item_hi, x_sorted, wg_b, wu_b, wd_b)
    y_p = _combine(pos3, yt, h2t, rt, g3, b3, 0, N_PROMPT // TM_COMB)
    y_s = _combine(pos3, yt, h2t, rt, g3, b3, N_PROMPT // TM_COMB, N_SAMPLE // TM_COMB)

    return (y_p.reshape(1, SEQ, D_MODEL),
            y_s.reshape(DEC_BATCH, DEC_SEQ, D_MODEL),
            k_tail.reshape(1, WINDOW, N_KV_HEADS, HEAD_DIM),
            v_tail.reshape(1, WINDOW, N_KV_HEADS, HEAD_DIM),
            u_tail[SUBLANES - (CONV_K - 1):].reshape(1, CONV_K - 1, CONV_CH),
            mk.reshape(1, N_MEM, MEM_HEADS, MEM_HEAD_DIM),
            mv.reshape(1, N_MEM, MEM_HEADS, MEM_HEAD_DIM),
            swa_k_s.reshape(DEC_BATCH, WINDOW, N_KV_HEADS, HEAD_DIM),
            swa_v_s.reshape(DEC_BATCH, WINDOW, N_KV_HEADS, HEAD_DIM),
            u_s.reshape(DEC_BATCH, DEC_SEQ, CONV_CH)[:, DEC_SEQ - (CONV_K - 1):])
```
